```python
import math
import jax, jax.numpy as jnp
from jax import lax
import numpy as np

D_MODEL = 1024
BATCH = 32
SEQ = 256
DEPTH = 1
DEC_BATCH = 8
DEC_SEQ = 1024
PAST_LEN = 256

GRID_W = 64
N_DIR = 2
D_LRU = 1024
LRU_HEADS = 16
LRU_HEAD_DIM = D_LRU // LRU_HEADS
LRU_C = 8.0
CONV_W = 4
CONV_LEFT = 2
D_S5 = 512
S5_GROUP = 16
S5_GROUPS = D_S5 // S5_GROUP
S5_STATE = 64
D_IN = 2 * D_LRU + D_S5
N_EXPERTS = 16
D_EXPERT = 1024
CAPACITY_FACTOR = 2
EPS = 1e-6
F32 = jnp.float32

kernel_name = 'hybrid_rglru_s5_expert_choice_diffusion_step'


def rmsnorm(x, g):
    x32 = x.astype(F32)
    y = x32 * lax.rsqrt(jnp.mean(x32 * x32, axis=-1, keepdims=True) + EPS)
    return (y * g.astype(F32)).astype(x.dtype)


def centred_dwconv(x, w, b):
    T = x.shape[1]
    xp = jnp.pad(x, ((0, 0), (CONV_LEFT, CONV_W - 1 - CONV_LEFT), (0, 0)))
    out = b
    for k in range(CONV_W):
        out = out + xp[:, k:k + T] * w[k]
    return out


def _lin_combine(e1, e2):
    a1, b1 = e1
    a2, b2 = e2
    return a1 * a2, a2 * b1 + b2


def linear_scan(a, b, h0, reverse):
    idx = -1 if reverse else 0
    b = b.at[:, idx].add(a[:, idx] * h0)
    _, h = lax.associative_scan(_lin_combine, (a, b), reverse=reverse, axis=1)
    return h


def _cplx_combine(e1, e2):
    a1r, a1i, b1r, b1i = e1
    a2r, a2i, b2r, b2i = e2
    return (a1r * a2r - a1i * a2i, a1r * a2i + a1i * a2r,
            a2r * b1r - a2i * b1i + b2r, a2r * b1i + a2i * b1r + b2i)


def complex_scan(ar, ai, br, bi, h0r, h0i, reverse):
    idx = -1 if reverse else 0
    br = br.at[:, idx].add(ar[:, idx] * h0r - ai[:, idx] * h0i)
    bi = bi.at[:, idx].add(ar[:, idx] * h0i + ai[:, idx] * h0r)
    _, _, hr, hi = lax.associative_scan(_cplx_combine, (ar, ai, br, bi), reverse=reverse, axis=1)
    return hr, hi


def rglru_branch(xb, gb, conv_w, conv_b, wa, ba, wx, bx, lam, h0):
    bsz, T, _ = xb.shape
    xc = centred_dwconv(xb, conv_w, conv_b).astype(F32)
    xh = xc.reshape(bsz, T, LRU_HEADS, LRU_HEAD_DIM)
    r = jax.nn.sigmoid(jnp.einsum('bthi,dhij->dbthj', xh, wa.astype(F32)).reshape(N_DIR, bsz, T, D_LRU)
                       + ba.astype(F32)[:, None, None])
    gi = jax.nn.sigmoid(jnp.einsum('bthi,dhij->dbthj', xh, wx.astype(F32)).reshape(N_DIR, bsz, T, D_LRU)
                        + bx.astype(F32)[:, None, None])
    log_a = -LRU_C * r * jax.nn.softplus(-lam.astype(F32))[:, None, None]
    a = jnp.exp(log_a)
    u = jnp.sqrt(-jnp.expm1(2.0 * log_a)) * gi * xc
    h0 = h0.astype(F32)
    h_f = linear_scan(a[0], u[0], h0[:, 0], False)
    h_b = linear_scan(a[1], u[1], h0[:, 1], True)
    y = jax.nn.gelu(gb.astype(F32)) * (h_f + h_b)
    final = jnp.stack([h_f[:, -1], h_b[:, 0]], axis=1)
    return y, final


def s5_branch(u, lam_re, lam_im, log_step, b_re, b_im, c_re, c_im, d_skip, w_glu, b_glu, h0r, h0i, col_major):
    bsz, T, _ = u.shape
    u32 = u.astype(F32)
    if col_major:
        rows = T // GRID_W
        u_scan = u32.reshape(bsz, rows, GRID_W, D_S5).swapaxes(1, 2).reshape(bsz, T, D_S5)
    else:
        u_scan = u32
    ug = u_scan.reshape(bsz, T, S5_GROUPS, S5_GROUP)
    h0r = h0r.astype(F32)
    h0i = h0i.astype(F32)
    y = jnp.zeros((bsz, T, S5_GROUPS, S5_GROUP), F32)
    fin_r, fin_i = [], []
    for d, reverse in ((0, False), (1, True)):
        step = jnp.exp(log_step[d].astype(F32))[:, None]
        lr = lam_re[d].astype(F32)
        li = lam_im[d].astype(F32)
        mag = jnp.exp(lr * step)
        ang = li * step
        abr, abi = mag * jnp.cos(ang), mag * jnp.sin(ang)
        den = lr * lr + li * li
        fr = ((abr - 1.0) * lr + abi * li) / den
        fi = (abi * lr - (abr - 1.0) * li) / den
        br_w, bi_w = b_re[d].astype(F32), b_im[d].astype(F32)
        bbr = fr[..., None] * br_w - fi[..., None] * bi_w
        bbi = fr[..., None] * bi_w + fi[..., None] * br_w
        br = jnp.einsum('btgp,gnp->btgn', ug, bbr)
        bi = jnp.einsum('btgp,gnp->btgn', ug, bbi)
        ar = jnp.broadcast_to(abr, br.shape)
        ai = jnp.broadcast_to(abi, br.shape)
        hr, hi = complex_scan(ar, ai, br, bi, h0r[:, d], h0i[:, d], reverse)
        y = y + jnp.einsum('btgn,gpn->btgp', hr, c_re[d].astype(F32)) \
              - jnp.einsum('btgn,gpn->btgp', hi, c_im[d].astype(F32))
        idx = 0 if reverse else -1
        fin_r.append(hr[:, idx])
        fin_i.append(hi[:, idx])
    y = y.reshape(bsz, T, D_S5)
    if col_major:
        rows = T // GRID_W
        y = y.reshape(bsz, GRID_W, rows, D_S5).swapaxes(1, 2).reshape(bsz, T, D_S5)
    y = y + d_skip.astype(F32) * u32
    v = jax.nn.gelu(y)
    out = v * jax.nn.sigmoid(v @ w_glu.astype(F32) + b_glu.astype(F32))
    return out, jnp.stack(fin_r, axis=1), jnp.stack(fin_i, axis=1)


def mixer(h, p, lru_h0, s5_h0r, s5_h0i, col_major):
    z = h @ p['w_in']
    xb, gb, us = z[..., :D_LRU], z[..., D_LRU:2 * D_LRU], z[..., 2 * D_LRU:]
    ya, lru_fin = rglru_branch(xb, gb, p['conv_w'], p['conv_b'], p['lru_wa'], p['lru_ba'],
                               p['lru_wx'], p['lru_bx'], p['lru_lambda'], lru_h0)
    yb, s5_fr, s5_fi = s5_branch(us, p['s5_lambda_re'], p['s5_lambda_im'], p['s5_log_step'],
                                 p['s5_b_re'], p['s5_b_im'], p['s5_c_re'], p['s5_c_im'], p['s5_d'],
                                 p['s5_w_glu'], p['s5_b_glu'], s5_h0r, s5_h0i, col_major)
    gates = jax.nn.sigmoid((h @ p['w_branch_gate'] + p['b_branch_gate']).astype(F32))
    g_a, g_b = gates[..., :D_MODEL], gates[..., D_MODEL:]
    merged = g_a * (ya @ p['w_lru_out'].astype(F32)) + g_b * (yb @ p['w_s5_out'].astype(F32))
    out = (merged @ p['w_o'].astype(F32)).astype(h.dtype)
    return out, (lru_fin, s5_fr, s5_fi)


def expert_choice_ffn(h, w_router, b_router, w_e_gate, w_e_up, w_e_down):
    bsz, T, D = h.shape
    n_tok = bsz * T
    cap = (CAPACITY_FACTOR * n_tok) // N_EXPERTS
    hf = h.reshape(n_tok, D)
    aff = jax.nn.softmax((hf @ w_router + b_router).astype(F32), axis=-1)
    g, idx = lax.top_k(aff.T, cap)
    xe = hf[idx]
    hid = jax.nn.silu(jnp.einsum('ecd,edf->ecf', xe, w_e_gate)) * jnp.einsum('ecd,edf->ecf', xe, w_e_up)
    ye = jnp.einsum('ecf,efd->ecd', hid, w_e_down) * g[..., None].astype(hid.dtype)
    out = jnp.zeros((n_tok, D), ye.dtype).at[idx.reshape(-1)].add(ye.reshape(-1, D))
    return out.reshape(bsz, T, D).astype(h.dtype)


def layer(x, mod, p, lru_h0, s5_h0r, s5_h0i, col_major):
    sh1, sc1, gt1, sh2, sc2, gt2 = jnp.split(mod, 6, axis=-1)
    h = rmsnorm(x, p['g_norm1']) * (1 + sc1) + sh1
    mix, states = mixer(h, p, lru_h0, s5_h0r, s5_h0i, col_major)
    x = x + gt1 * mix
    h = rmsnorm(x, p['g_norm2']) * (1 + sc2) + sh2
    x = x + gt2 * expert_choice_ffn(h, p['w_router'], p['b_router'], p['w_e_gate'], p['w_e_up'], p['w_e_down'])
    return x, states


def setup_inputs(seed: int = 0) -> dict:
    key = jax.random.key(seed)
    ks = iter(jax.random.split(key, 64))

    def nrm(shape, scale):
        return scale * jax.random.normal(next(ks), shape, F32)

    L = DEPTH
    G, N, P = S5_GROUPS, S5_STATE, S5_GROUP
    u = jax.random.uniform(next(ks), (L, N_DIR, D_LRU), F32, 0.9, 0.999)
    pa = u ** (1.0 / LRU_C)
    lru_lambda = jnp.log(pa) - jnp.log1p(-pa)
    n_idx = jnp.arange(N, dtype=F32)
    return {
        'x_prompt': nrm((BATCH, SEQ, D_MODEL), 1.0),
        'x_sample': nrm((DEC_BATCH, DEC_SEQ, D_MODEL), 1.0),
        'state_lru': nrm((DEC_BATCH, L, N_DIR, D_LRU), 0.5),
        'state_s5_re': nrm((DEC_BATCH, L, N_DIR, G, N), 0.1),
        'state_s5_im': nrm((DEC_BATCH, L, N_DIR, G, N), 0.1),
        'c': nrm((DEC_BATCH, D_MODEL), 1.0),
        'c_ctx': nrm((D_MODEL,), 1.0),
        'w_mod': nrm((L, D_MODEL, 6 * D_MODEL), 0.5 * D_MODEL ** -0.5),
        'b_mod': nrm((L, 6 * D_MODEL), 0.02),
        'g_norm1': 1.0 + nrm((L, D_MODEL), 0.02),
        'g_norm2': 1.0 + nrm((L, D_MODEL), 0.02),
        'w_in': nrm((L, D_MODEL, D_IN), D_MODEL ** -0.5),
        'conv_w': nrm((L, CONV_W, D_LRU), CONV_W ** -0.5),
        'conv_b': nrm((L, D_LRU), 0.02),
        'lru_wa': nrm((L, N_DIR, LRU_HEADS, LRU_HEAD_DIM, LRU_HEAD_DIM), LRU_HEAD_DIM ** -0.5),
        'lru_ba': nrm((L, N_DIR, D_LRU), 0.02),
        'lru_wx': nrm((L, N_DIR, LRU_HEADS, LRU_HEAD_DIM, LRU_HEAD_DIM), LRU_HEAD_DIM ** -0.5),
        'lru_bx': nrm((L, N_DIR, D_LRU), 0.02),
        'lru_lambda': lru_lambda,
        's5_lambda_re': -0.5 + nrm((L, N_DIR, G, N), 0.01),
        's5_lambda_im': jnp.pi * n_idx + nrm((L, N_DIR, G, N), 0.01),
        's5_log_step': jax.random.uniform(next(ks), (L, N_DIR, G), F32, math.log(1e-3), math.log(1e-1)),
        's5_b_re': nrm((L, N_DIR, G, N, P), (2.0 * P) ** -0.5),
        's5_b_im': nrm((L, N_DIR, G, N, P), (2.0 * P) ** -0.5),
        's5_c_re': nrm((L, N_DIR, G, P, N), (2.0 * N) ** -0.5 * 4.0),
        's5_c_im': nrm((L, N_DIR, G, P, N), (2.0 * N) ** -0.5 * 4.0),
        's5_d': nrm((L, D_S5), 1.0),
        's5_w_glu': nrm((L, D_S5, D_S5), D_S5 ** -0.5),
        's5_b_glu': nrm((L, D_S5), 0.02),
        'w_lru_out': nrm((L, D_LRU, D_MODEL), D_LRU ** -0.5),
        'w_s5_out': nrm((L, D_S5, D_MODEL), D_S5 ** -0.5),
        'w_branch_gate': nrm((L, D_MODEL, 2 * D_MODEL), D_MODEL ** -0.5),
        'b_branch_gate': nrm((L, 2 * D_MODEL), 0.02),
        'w_o': nrm((L, D_MODEL, D_MODEL), D_MODEL ** -0.5),
        'w_router': nrm((L, D_MODEL, N_EXPERTS), D_MODEL ** -0.5),
        'b_router': nrm((L, N_EXPERTS), 0.01),
        'w_e_gate': nrm((L, N_EXPERTS, D_MODEL, D_EXPERT), D_MODEL ** -0.5),
        'w_e_up': nrm((L, N_EXPERTS, D_MODEL, D_EXPERT), D_MODEL ** -0.5),
        'w_e_down': nrm((L, N_EXPERTS, D_EXPERT, D_MODEL), D_EXPERT ** -0.5),
        'g_final': 1.0 + nrm((D_MODEL,), 0.02),
    }


def reference(x_prompt, x_sample, state_lru, state_s5_re, state_s5_im, c, c_ctx,
              w_mod, b_mod, g_norm1, g_norm2, w_in, conv_w, conv_b,
              lru_wa, lru_ba, lru_wx, lru_bx, lru_lambda,
              s5_lambda_re, s5_lambda_im, s5_log_step, s5_b_re, s5_b_im, s5_c_re, s5_c_im,
              s5_d, s5_w_glu, s5_b_glu, w_lru_out, w_s5_out, w_branch_gate, b_branch_gate, w_o,
              w_router, b_router, w_e_gate, w_e_up, w_e_down, g_final):
    xp, xs = x_prompt, x_sample
    bp = xp.shape[0]
    zero_lru = jnp.zeros((bp, N_DIR, D_LRU), F32)
    zero_s5 = jnp.zeros((bp, N_DIR, S5_GROUPS, S5_STATE), F32)
    new_lru, new_s5r, new_s5i = [], [], []
    for l in range(DEPTH):
        p = {
            'g_norm1': g_norm1[l], 'g_norm2': g_norm2[l], 'w_in': w_in[l],
            'conv_w': conv_w[l], 'conv_b': conv_b[l],
            'lru_wa': lru_wa[l], 'lru_ba': lru_ba[l], 'lru_wx': lru_wx[l], 'lru_bx': lru_bx[l],
            'lru_lambda': lru_lambda[l],
            's5_lambda_re': s5_lambda_re[l], 's5_lambda_im': s5_lambda_im[l], 's5_log_step': s5_log_step[l],
            's5_b_re': s5_b_re[l], 's5_b_im': s5_b_im[l], 's5_c_re': s5_c_re[l], 's5_c_im': s5_c_im[l],
            's5_d': s5_d[l], 's5_w_glu': s5_w_glu[l], 's5_b_glu': s5_b_glu[l],
            'w_lru_out': w_lru_out[l], 'w_s5_out': w_s5_out[l],
            'w_branch_gate': w_branch_gate[l], 'b_branch_gate': b_branch_gate[l], 'w_o': w_o[l],
            'w_router': w_router[l], 'b_router': b_router[l],
            'w_e_gate': w_e_gate[l], 'w_e_up': w_e_up[l], 'w_e_down': w_e_down[l],
        }
        mod_ctx = (jax.nn.silu(c_ctx) @ w_mod[l] + b_mod[l])[None, None, :]
        mod_lat = (jax.nn.silu(c) @ w_mod[l] + b_mod[l])[:, None, :]
        xp, (lru_fin, s5_fr, s5_fi) = layer(xp, mod_ctx, p, zero_lru, zero_s5, zero_s5, False)
        new_lru.append(lru_fin.astype(xp.dtype))
        new_s5r.append(s5_fr.astype(xp.dtype))
        new_s5i.append(s5_fi.astype(xp.dtype))
        xs, _ = layer(xs, mod_lat, p, state_lru[:, l], state_s5_re[:, l], state_s5_im[:, l], True)
    y_prompt = rmsnorm(xp, g_final)
    y_sample = rmsnorm(xs, g_final)
    return (y_prompt, y_sample, jnp.stack(new_lru, axis=1), jnp.stack(new_s5r, axis=1), jnp.stack(new_s5i, axis=1))
```

```python
import functools
import math

import jax
import jax.numpy as jnp
from jax import lax
from jax.experimental import pallas as pl
from jax.experimental.pallas import tpu as pltpu

F32 = jnp.float32
BF16 = jnp.bfloat16
I32 = jnp.int32

D_MODEL = 1024
D_LRU = 1024
LRU_HEADS = 16
LRU_HEAD_DIM = 64
LRU_C = 8.0
CONV_W = 4
D_S5 = 512
S5_P = 16
S5_G = 32
S5_N = 64
GRID_W = 64
N_EXPERTS = 16
CAPACITY_FACTOR = 2
EPS = 1e-6

LANES = 128
SUBLANES = 8
S5_L = 8
S5_BLK = 4
S5_SW = 4 * 512
TOK_TILE = 128
SLOT_CHUNK = 64
VMEM_LIMIT = 56 * 1024 * 1024


def _cp(sem, vmem=VMEM_LIMIT):
    return pltpu.CompilerParams(dimension_semantics=sem, vmem_limit_bytes=vmem)


def _bdot(a, b):
    return jnp.dot(a.astype(BF16), b.astype(BF16), preferred_element_type=F32)


def _split2(a):
    hi = a.astype(BF16)
    lo = (a - hi.astype(F32)).astype(BF16)
    return hi, lo


def _dot3(a, b, dims=(((1,), (0,)), ((), ()))):
    ah, al = _split2(a)
    bh, bl = _split2(b)
    d = functools.partial(lax.dot_general, dimension_numbers=dims, preferred_element_type=F32)
    return d(ah, bh) + (d(al, bh) + d(ah, bl))


def _sigmoid(x):
    return 1.0 / (1.0 + jnp.exp(-x))


def _gelu(x):
    c = math.sqrt(2.0 / math.pi)
    return 0.5 * x * (1.0 + jnp.tanh(c * (x + 0.044715 * (x * x * x))))


def _norm_mod(x, g, scale, shift):
    ms = jnp.mean(x * x, axis=-1, keepdims=True)
    return (x * lax.rsqrt(ms + EPS) * g) * (1.0 + scale) + shift


def _mod_body(c_ref, w_ref, b_ref, o_ref):
    c = c_ref[...]
    s = c * _sigmoid(c)
    o_ref[...] = _dot3(s, w_ref[...]) + b_ref[...]


def _mod_call(c_all, w_mod, b_mod):
    n = w_mod.shape[1]
    tn = 1536
    return pl.pallas_call(
        _mod_body,
        grid=(n // tn,),
        in_specs=[pl.BlockSpec((16, D_MODEL), lambda j: (0, 0)),
                  pl.BlockSpec((D_MODEL, tn), lambda j: (0, j)),
                  pl.BlockSpec((1, tn), lambda j: (0, j))],
        out_specs=pl.BlockSpec((16, tn), lambda j: (0, j)),
        out_shape=jax.ShapeDtypeStruct((16, n), F32),
        compiler_params=_cp(("arbitrary",)),
        name="mod",
    )(c_all, w_mod, b_mod)


def _inproj_body(x_ref, mod_ref, g_ref, w_ref, ws_ref, xb_ref, gg_ref, us_ref):
    m = mod_ref[0]
    h = _norm_mod(x_ref[...], g_ref[...], m[:, D_MODEL:2 * D_MODEL], m[:, 0:D_MODEL])
    hb = h.astype(BF16)
    z = jnp.dot(hb, w_ref[...], preferred_element_type=F32)
    xb_ref[...] = z[:, :D_LRU]
    gg_ref[...] = _gelu(z[:, D_LRU:]).astype(BF16)
    us = jnp.dot(hb, ws_ref[...], preferred_element_type=F32)
    for k in range(S5_BLK):
        us_ref[k] = us[:, k * LANES:(k + 1) * LANES]


def _inproj_nos5_body(x_ref, mod_ref, g_ref, w_ref, xb_ref, gg_ref):
    m = mod_ref[0]
    h = _norm_mod(x_ref[...], g_ref[...], m[:, D_MODEL:2 * D_MODEL], m[:, 0:D_MODEL])
    z = jnp.dot(h.astype(BF16), w_ref[...], preferred_element_type=F32)
    xb_ref[...] = z[:, :D_LRU]
    gg_ref[...] = _gelu(z[:, D_LRU:]).astype(BF16)


def _inproj_call(x2d, mod3, g1, w_main, w_s5, mod_row, tm, with_s5):
    n = x2d.shape[0]
    in_specs = [pl.BlockSpec((tm, D_MODEL), lambda i: (i, 0)),
                pl.BlockSpec((1, 1, 6 * D_MODEL), lambda i: (mod_row(i), 0, 0)),
                pl.BlockSpec((1, D_MODEL), lambda i: (0, 0)),
                pl.BlockSpec((D_MODEL, 2 * D_LRU), lambda i: (0, 0))]
    out_specs = [pl.BlockSpec((tm, D_LRU), lambda i: (i, 0)),
                 pl.BlockSpec((tm, D_LRU), lambda i: (i, 0))]
    out_shape = [jax.ShapeDtypeStruct((n, D_LRU), F32), jax.ShapeDtypeStruct((n, D_LRU), BF16)]
    args = [x2d, mod3, g1, w_main]
    body = _inproj_nos5_body
    if with_s5:
        in_specs.append(pl.BlockSpec((D_MODEL, D_S5), lambda i: (0, 0)))
        out_specs.append(pl.BlockSpec((S5_BLK, tm, LANES), lambda i: (0, i, 0)))
        out_shape.append(jax.ShapeDtypeStruct((S5_BLK, n, LANES), F32))
        args.append(w_s5)
        body = _inproj_body
    return pl.pallas_call(
        body, grid=(n // tm,), in_specs=in_specs, out_specs=out_specs, out_shape=out_shape,
        compiler_params=_cp(("parallel",)), name="inproj",
    )(*args)


def _inproj_scan_body(x_ref, mod_ref, g_ref, ws_ref, us_ref, *, ncol):
    xs = jnp.concatenate([x_ref[0, :, c * D_MODEL:(c + 1) * D_MODEL] for c in range(ncol)], axis=0)
    m = mod_ref[0]
    h = _norm_mod(xs, g_ref[...], m[:, D_MODEL:2 * D_MODEL], m[:, 0:D_MODEL])
    us = jnp.dot(h.astype(BF16), ws_ref[...], preferred_element_type=F32)
    for k in range(S5_BLK):
        us_ref[k] = us[:, k * LANES:(k + 1) * LANES]


def _inproj_scan_call(x_sample, mod3, g1, w_s5):
    b, t, _ = x_sample.shape
    rows = t // GRID_W
    ncol = 16
    xv = x_sample.reshape(b, rows, GRID_W * D_MODEL)
    tm = ncol * rows
    nj = GRID_W // ncol
    return pl.pallas_call(
        functools.partial(_inproj_scan_body, ncol=ncol),
        grid=(b, nj),
        in_specs=[pl.BlockSpec((1, rows, ncol * D_MODEL), lambda i, j: (i, 0, j)),
                  pl.BlockSpec((1, 1, 6 * D_MODEL), lambda i, j: (1 + i, 0, 0)),
                  pl.BlockSpec((1, D_MODEL), lambda i, j: (0, 0)),
                  pl.BlockSpec((D_MODEL, D_S5), lambda i, j: (0, 0))],
        out_specs=pl.BlockSpec((S5_BLK, tm, LANES), lambda i, j: (0, i * nj + j, 0)),
        out_shape=jax.ShapeDtypeStruct((S5_BLK, b * t, LANES), F32),
        compiler_params=_cp(("parallel", "parallel")), name="inproj_scan",
    )(xv, mod3, g1, w_s5)


def _lru_body(xb_ref, gg_ref, cw_ref, cb_ref, wg_ref, ba_ref, bx_ref, lam_ref, h0_ref,
              ya_ref, fin_ref, xpad, a_f, b_f, a_b, b_b, *, T):
    ch = 256
    nch = T // ch
    zero8 = jnp.zeros((SUBLANES, D_LRU), F32)
    xpad[0:SUBLANES, :] = zero8
    xpad[T + SUBLANES:T + 2 * SUBLANES, :] = zero8
    for c in range(nch):
        xpad[SUBLANES + c * ch:SUBLANES + (c + 1) * ch, :] = xb_ref[0, c * ch:(c + 1) * ch, :]

    nl = -lam_ref[...]
    softplus = jnp.maximum(nl, 0.0) + jnp.log1p(jnp.exp(-jnp.abs(nl)))
    sub = lax.broadcasted_iota(I32, (1, SUBLANES, D_LRU), 1)
    scr = ((a_f, b_f), (a_b, b_b))

    for c in range(nch):
        t0 = c * ch
        xc = cb_ref[...] + sum(
            xpad[t0 + SUBLANES - 2 + k:t0 + SUBLANES - 2 + k + ch, :] * cw_ref[k:k + 1, :]
            for k in range(CONV_W))
        xcb = xc.astype(BF16)
        pre = [jnp.dot(xcb[:, kb * 256:(kb + 1) * 256], wg_ref[kb], preferred_element_type=F32)
               for kb in range(4)]
        for d in range(2):
            ra = jnp.concatenate([p[:, d * 256:(d + 1) * 256] for p in pre], axis=1)
            gx = jnp.concatenate([p[:, 512 + d * 256:512 + (d + 1) * 256] for p in pre], axis=1)
            r = _sigmoid(ra + ba_ref[d:d + 1, :])
            gi = _sigmoid(gx + bx_ref[d:d + 1, :])
            log_a = (-LRU_C) * r * softplus[d:d + 1, :]
            a = jnp.exp(log_a)
            u = jnp.sqrt(1.0 - a * a) * gi * xc
            a3 = a.reshape(ch // SUBLANES, SUBLANES, D_LRU)
            u3 = u.reshape(ch // SUBLANES, SUBLANES, D_LRU)
            for s in (1, 2, 4):
                if d == 0:
                    a_s = pltpu.roll(a3, s, 1)
                    u_s = pltpu.roll(u3, s, 1)
                    keep = sub >= s
                else:
                    a_s = pltpu.roll(a3, SUBLANES - s, 1)
                    u_s = pltpu.roll(u3, SUBLANES - s, 1)
                    keep = sub < SUBLANES - s
                u3 = u3 + a3 * jnp.where(keep, u_s, 0.0)
                a3 = a3 * jnp.where(keep, a_s, 1.0)
            scr[d][0][t0:t0 + ch, :] = a3.reshape(ch, D_LRU)
            scr[d][1][t0:t0 + ch, :] = u3.reshape(ch, D_LRU)

    ntile = T // SUBLANES

    def fwd(i, carry):
        r0 = pl.multiple_of(i * SUBLANES, SUBLANES)
        h = a_f[pl.ds(r0, SUBLANES), :] * carry + b_f[pl.ds(r0, SUBLANES), :]
        b_f[pl.ds(r0, SUBLANES), :] = h
        return h[SUBLANES - 1:SUBLANES, :]

    def bwd(i, carry):
        r0 = pl.multiple_of((ntile - 1 - i) * SUBLANES, SUBLANES)
        h = a_b[pl.ds(r0, SUBLANES), :] * carry + b_b[pl.ds(r0, SUBLANES), :]
        b_b[pl.ds(r0, SUBLANES), :] = h
        return h[0:1, :]

    fin_ref[0, 0:1, :] = lax.fori_loop(0, ntile, fwd, h0_ref[0, 0:1, :])
    fin_ref[0, 1:2, :] = lax.fori_loop(0, ntile, bwd, h0_ref[0, 1:2, :])

    for c in range(nch):
        sl = slice(c * ch, (c + 1) * ch)
        ya_ref[0, sl, :] = (gg_ref[0, sl, :].astype(F32) * (b_f[sl, :] + b_b[sl, :])).astype(BF16)


def _lru_call(xb, gg, conv_w, conv_b, wg, ba, bx, lam, h0, n_seq, T):
    xb3 = xb.reshape(n_seq, T, D_LRU)
    gg3 = gg.reshape(n_seq, T, D_LRU)
    full = lambda shape: pl.BlockSpec(shape, lambda i: (0,) * len(shape))
    seq = lambda shape: pl.BlockSpec(shape, lambda i: (i,) + (0,) * (len(shape) - 1))
    ya, fin = pl.pallas_call(
        functools.partial(_lru_body, T=T),
        grid=(n_seq,),
        in_specs=[seq((1, T, D_LRU)), seq((1, T, D_LRU)), full((CONV_W, D_LRU)), full((1, D_LRU)),
                  full((4, 256, 1024)), full((2, D_LRU)), full((2, D_LRU)), full((2, D_LRU)),
                  seq((1, 2, D_LRU))],
        out_specs=[seq((1, T, D_LRU)), seq((1, 2, D_LRU))],
        out_shape=[jax.ShapeDtypeStruct((n_seq, T, D_LRU), BF16),
                   jax.ShapeDtypeStruct((n_seq, 2, D_LRU), F32)],
        scratch_shapes=[pltpu.VMEM((T + 2 * SUBLANES, D_LRU), F32)] + [pltpu.VMEM((T, D_LRU), F32)] * 4,
        compiler_params=_cp(("parallel",)), name="lru",
    )(xb3, gg3, conv_w, conv_b, wg, ba, bx, lam, h0)
    return ya.reshape(n_seq * T, D_LRU), fin


def _s5_body(u_ref, wts_ref, wc_ref, ap_ref, h0_ref, y_ref, fin_ref, s_scr, hp_scr,
             *, rt, n_c, with_h0):
    nseq = rt // n_c
    q = 512
    ub = u_ref[0].astype(BF16)
    r1 = jnp.dot(ub, wts_ref[0], preferred_element_type=F32)
    y_in = r1[:, :S5_L * LANES]
    s_scr[...] = r1[:, S5_L * LANES:]
    ap = ap_ref[0]
    if with_h0:
        for s in range(nseq):
            h0 = h0_ref[0, 0, s:s + 1, :]
            r_f = s * n_c
            r_b = s * n_c + n_c - 1
            ar, ai = ap[0:1, 0:q], ap[0:1, q:2 * q]
            s_scr[r_f:r_f + 1, 0:q] += ar * h0[:, 0:q] - ai * h0[:, q:2 * q]
            s_scr[r_f:r_f + 1, q:2 * q] += ar * h0[:, q:2 * q] + ai * h0[:, 0:q]
            br, bi = ap[0:1, 2 * q:3 * q], ap[0:1, 3 * q:4 * q]
            s_scr[r_b:r_b + 1, 2 * q:3 * q] += br * h0[:, 2 * q:3 * q] - bi * h0[:, 3 * q:4 * q]
            s_scr[r_b:r_b + 1, 3 * q:4 * q] += br * h0[:, 3 * q:4 * q] + bi * h0[:, 2 * q:3 * q]

    cidx = lax.broadcasted_iota(I32, (rt, 1), 0) % n_c
    hfr, hfi = s_scr[:, 0:q], s_scr[:, q:2 * q]
    hbr, hbi = s_scr[:, 2 * q:3 * q], s_scr[:, 3 * q:4 * q]
    lvl = 0
    dist = 1
    while dist < n_c:
        ar, ai = ap[lvl:lvl + 1, 0:q], ap[lvl:lvl + 1, q:2 * q]
        br, bi = ap[lvl:lvl + 1, 2 * q:3 * q], ap[lvl:lvl + 1, 3 * q:4 * q]
        kf = cidx >= dist
        kb = cidx < n_c - dist
        sfr = jnp.where(kf, pltpu.roll(hfr, dist, 0), 0.0)
        sfi = jnp.where(kf, pltpu.roll(hfi, dist, 0), 0.0)
        sbr = jnp.where(kb, pltpu.roll(hbr, rt - dist, 0), 0.0)
        sbi = jnp.where(kb, pltpu.roll(hbi, rt - dist, 0), 0.0)
        hfr, hfi = hfr + (ar * sfr - ai * sfi), hfi + (ar * sfi + ai * sfr)
        hbr, hbi = hbr + (br * sbr - bi * sbi), hbi + (br * sbi + bi * sbr)
        lvl += 1
        dist *= 2

    s_scr[:, 0:q] = hfr
    s_scr[:, q:2 * q] = hfi
    s_scr[:, 2 * q:3 * q] = hbr
    s_scr[:, 3 * q:4 * q] = hbi
    for s in range(nseq):
        fin_ref[0, 0, s:s + 1, 0:2 * q] = s_scr[s * n_c + n_c - 1:s * n_c + n_c, 0:2 * q]
        fin_ref[0, 0, s:s + 1, 2 * q:4 * q] = s_scr[s * n_c:s * n_c + 1, 2 * q:4 * q]

    kf = cidx >= 1
    kb = cidx < n_c - 1
    hp_scr[:, 0:q] = jnp.where(kf, pltpu.roll(hfr, 1, 0), 0.0)
    hp_scr[:, q:2 * q] = jnp.where(kf, pltpu.roll(hfi, 1, 0), 0.0)
    hp_scr[:, 2 * q:3 * q] = jnp.where(kb, pltpu.roll(hbr, rt - 1, 0), 0.0)
    hp_scr[:, 3 * q:4 * q] = jnp.where(kb, pltpu.roll(hbi, rt - 1, 0), 0.0)
    if with_h0:
        for s in range(nseq):
            hp_scr[s * n_c:s * n_c + 1, 0:2 * q] = h0_ref[0, 0, s:s + 1, 0:2 * q]
            hp_scr[s * n_c + n_c - 1:s * n_c + n_c, 2 * q:4 * q] = h0_ref[0, 0, s:s + 1, 2 * q:4 * q]
    y_ref[0] = y_in + jnp.dot(hp_scr[...].astype(BF16), wc_ref[0], preferred_element_type=F32)


def _s5_call(us4, wts, wc, apow, h0, n_seq, T, with_h0):
    n_c = T // S5_L
    rows = n_seq * n_c
    rt = 256
    nseq_t = rt // n_c
    u4 = us4.reshape(S5_BLK, rows, S5_L * LANES)
    h0 = h0.reshape(S5_BLK, n_seq // nseq_t, nseq_t, S5_SW)
    y4, fin = pl.pallas_call(
        functools.partial(_s5_body, rt=rt, n_c=n_c, with_h0=with_h0),
        grid=(S5_BLK, rows // rt),
        in_specs=[pl.BlockSpec((1, rt, S5_L * LANES), lambda b, r: (b, r, 0)),
                  pl.BlockSpec((1, S5_L * LANES, S5_L * LANES + S5_SW), lambda b, r: (b, 0, 0)),
                  pl.BlockSpec((1, S5_SW, S5_L * LANES), lambda b, r: (b, 0, 0)),
                  pl.BlockSpec((1, 8, S5_SW), lambda b, r: (b, 0, 0)),
                  pl.BlockSpec((1, 1, nseq_t, S5_SW), lambda b, r: (b, r, 0, 0))],
        out_specs=[pl.BlockSpec((1, rt, S5_L * LANES), lambda b, r: (b, r, 0)),
                   pl.BlockSpec((1, 1, nseq_t, S5_SW), lambda b, r: (b, r, 0, 0))],
        out_shape=[jax.ShapeDtypeStruct((S5_BLK, rows, S5_L * LANES), F32),
                   jax.ShapeDtypeStruct((S5_BLK, n_seq // nseq_t, nseq_t, S5_SW), F32)],
        scratch_shapes=[pltpu.VMEM((rt, S5_SW), F32), pltpu.VMEM((rt, S5_SW), F32)],
        compiler_params=_cp(("parallel", "parallel")), name="s5",
    )(u4, wts, wc, apow, h0)
    return y4.reshape(S5_BLK, n_seq * T, LANES), fin.reshape(S5_BLK, n_seq, S5_SW)


def _s5_weights(lam_re, lam_im, log_step, b_re, b_im, c_re, c_im):
    L, G, N, P = S5_L, S5_G, S5_N, S5_P
    hp = lax.Precision.HIGHEST
    step = jnp.exp(log_step)[..., None]
    lr, li = lam_re, lam_im
    mag = jnp.exp(lr * step)
    ang = li * step
    ar, ai = mag * jnp.cos(ang), mag * jnp.sin(ang)
    den = lr * lr + li * li
    fr = ((ar - 1.0) * lr + ai * li) / den
    fi = (ai * lr - (ar - 1.0) * li) / den
    bbr = fr[..., None] * b_re - fi[..., None] * b_im
    bbi = fr[..., None] * b_im + fi[..., None] * b_re
    pr, pi = [jnp.ones_like(ar)], [jnp.zeros_like(ar)]
    for _ in range(L):
        pr.append(pr[-1] * ar - pi[-1] * ai)
        pi.append(pr[-2] * ai + pi[-1] * ar)
    pr = jnp.stack(pr, axis=2)
    pi = jnp.stack(pi, axis=2)
    er = pr[..., None] * bbr[:, :, None] - pi[..., None] * bbi[:, :, None]
    ei = pr[..., None] * bbi[:, :, None] + pi[..., None] * bbr[:, :, None]
    kt = (jnp.einsum('dgpn,dgknq->dgkpq', c_re, er, precision=hp)
          - jnp.einsum('dgpn,dgknq->dgkpq', c_im, ei, precision=hp))
    ii = jnp.arange(L)
    dif = ii[None, :] - ii[:, None]
    tf = jnp.where((dif >= 0)[None, :, :, None, None], kt[0][:, jnp.clip(dif, 0, L)], 0.0)
    tb = jnp.where((dif <= 0)[None, :, :, None, None], kt[1][:, jnp.clip(-dif, 0, L)], 0.0)
    toe = (tf + tb).reshape(S5_BLK, 8, L, L, P, P)
    eye = jnp.eye(8, dtype=F32)
    wt = jnp.einsum('bgjipq,gh->bjgqihp', toe, eye).reshape(S5_BLK, L * LANES, L * LANES)

    def embed_state(e):
        e6 = e.reshape(S5_BLK, 8, L, N, P)
        return jnp.einsum('bgjnq,gh->bjgqhn', e6, eye).reshape(S5_BLK, L * LANES, 8 * N)

    rev = jnp.arange(L - 1, -1, -1)
    ws = jnp.concatenate([embed_state(er[0][:, rev]), embed_state(ei[0][:, rev]),
                          embed_state(er[1][:, :L]), embed_state(ei[1][:, :L])], axis=2)
    wts = jnp.concatenate([wt, ws], axis=2).astype(BF16)

    def embed_corr(cr, ci, p_r, p_i):
        g_r = cr[:, None] * p_r[:, :, None, :] - ci[:, None] * p_i[:, :, None, :]
        g_i = -(cr[:, None] * p_i[:, :, None, :] + ci[:, None] * p_r[:, :, None, :])
        def emb(gm):
            g6 = gm.reshape(S5_BLK, 8, L, P, N)
            return jnp.einsum('bgipn,gh->bgnihp', g6, eye).reshape(S5_BLK, 8 * N, L * LANES)
        return emb(g_r), emb(g_i)

    kf = jnp.arange(1, L + 1)
    kb = jnp.arange(L, 0, -1)
    cfr, cfi = embed_corr(c_re[0], c_im[0], pr[0][:, kf], pi[0][:, kf])
    cbr, cbi = embed_corr(c_re[1], c_im[1], pr[1][:, kb], pi[1][:, kb])
    wc = jnp.concatenate([cfr, cfi, cbr, cbi], axis=1).astype(BF16)

    qr, qi = pr[:, :, L], pi[:, :, L]
    lv = []
    for _ in range(8):
        lv.append(jnp.concatenate([qr[0].reshape(S5_BLK, 8 * N), qi[0].reshape(S5_BLK, 8 * N),
                                   qr[1].reshape(S5_BLK, 8 * N), qi[1].reshape(S5_BLK, 8 * N)], axis=1))
        qr, qi = qr * qr - qi * qi, 2.0 * qr * qi
    apow = jnp.stack(lv, axis=1)
    return wts, wc, apow


def _mixout_body(x_ref, mod_ref, g1_ref, g2_ref, wbg_ref, bbg_ref, ya_ref, y4_ref, u4_ref,
                 dsk_ref, wglu_ref, bglu_ref, wso_ref, wlo_ref, wo_ref, wrt_ref, brt_ref,
                 x1_ref, h2_ref, aff_ref, *, permuted):
    m = mod_ref[0]
    sh1, sc1, gt1 = m[:, 0:D_MODEL], m[:, D_MODEL:2 * D_MODEL], m[:, 2 * D_MODEL:3 * D_MODEL]
    sh2, sc2 = m[:, 3 * D_MODEL:4 * D_MODEL], m[:, 4 * D_MODEL:5 * D_MODEL]
    x = x_ref[...]
    h = _norm_mod(x, g1_ref[...], sc1, sh1)
    gates = _sigmoid(jnp.dot(h.astype(BF16), wbg_ref[...], preferred_element_type=F32) + bbg_ref[...])

    def assemble(ref):
        if not permuted:
            return jnp.concatenate([ref[k] for k in range(S5_BLK)], axis=1)
        nj = ref.shape[3] // LANES
        return jnp.concatenate(
            [jnp.concatenate([ref[k, 0, :, j * LANES:(j + 1) * LANES] for k in range(S5_BLK)], axis=1)
             for j in range(nj)], axis=0)

    ys = assemble(y4_ref) + dsk_ref[...] * assemble(u4_ref)
    v = _gelu(ys)
    ob = v * _sigmoid(jnp.dot(v.astype(BF16), wglu_ref[...], preferred_element_type=F32) + bglu_ref[...])
    merged = (gates[:, :D_MODEL] * jnp.dot(ya_ref[...], wlo_ref[...], preferred_element_type=F32)
              + gates[:, D_MODEL:] * jnp.dot(ob.astype(BF16), wso_ref[...], preferred_element_type=F32))
    mix = jnp.dot(merged.astype(BF16), wo_ref[...], preferred_element_type=F32)
    x1 = x + gt1 * mix
    x1_ref[...] = x1
    h2 = _norm_mod(x1, g2_ref[...], sc2, sh2)
    h2_ref[...] = h2
    logits = _dot3(wrt_ref[...], h2, dims=(((1,), (1,)), ((), ()))) + brt_ref[...]
    mx = jnp.max(logits, axis=0, keepdims=True)
    ex = jnp.exp(logits - mx)
    aff_ref[...] = ex / jnp.sum(ex, axis=0, keepdims=True)


def _mixout_call(x2d, mod3, g1, g2, wbg, bbg, ya, y4, u4, dsk, wglu, bglu, wso, wlo, wo, wrt, brt,
                 mod_row, permuted, tm):
    n = x2d.shape[0]
    full = lambda shape: pl.BlockSpec(shape, lambda i: (0,) * len(shape))
    if permuted:
        rows_t = tm // GRID_W
        per_seq = (n // 8) // tm
        seq_rows = (n // 8) // GRID_W
        y4 = y4.reshape(S5_BLK, 8, GRID_W, seq_rows * LANES)
        u4 = u4.reshape(S5_BLK, 8, GRID_W, seq_rows * LANES)
        s5_spec = pl.BlockSpec((S5_BLK, 1, GRID_W, rows_t * LANES),
                               lambda i: (0, i // per_seq, 0, i % per_seq))
    else:
        s5_spec = pl.BlockSpec((S5_BLK, tm, LANES), lambda i: (0, i, 0))
    tok = lambda w: pl.BlockSpec((tm, w), lambda i: (i, 0))
    return pl.pallas_call(
        functools.partial(_mixout_body, permuted=permuted),
        grid=(n // tm,),
        in_specs=[tok(D_MODEL), pl.BlockSpec((1, 1, 6 * D_MODEL), lambda i: (mod_row(i), 0, 0)),
                  full((1, D_MODEL)), full((1, D_MODEL)), full((D_MODEL, 2 * D_MODEL)),
                  full((1, 2 * D_MODEL)), tok(D_LRU), s5_spec, s5_spec, full((1, D_S5)),
                  full((D_S5, D_S5)), full((1, D_S5)), full((D_S5, D_MODEL)), full((D_LRU, D_MODEL)),
                  full((D_MODEL, D_MODEL)), full((N_EXPERTS, D_MODEL)), full((N_EXPERTS, 1))],
        out_specs=[tok(D_MODEL), tok(D_MODEL), pl.BlockSpec((N_EXPERTS, tm), lambda i: (0, i))],
        out_shape=[jax.ShapeDtypeStruct((n, D_MODEL), F32), jax.ShapeDtypeStruct((n, D_MODEL), F32),
                   jax.ShapeDtypeStruct((N_EXPERTS, n), F32)],
        compiler_params=_cp(("parallel",)), name="mixout",
    )(x2d, mod3, g1, g2, wbg, bbg, ya, y4, u4, dsk, wglu, bglu, wso, wlo, wo, wrt, brt)


def _select_body(aff_ref, gate_ref, pos_ref, offs_ref, *, n_tok, cap):
    aff = aff_ref[...]
    bits = pltpu.bitcast(aff, I32)
    capf = float(cap)

    def bis(_, lh):
        lo, hi = lh
        mid = lo + ((hi - lo + 1) >> 1)
        cnt = jnp.sum(jnp.where(bits >= mid, 1.0, 0.0), axis=1, keepdims=True)
        ok = cnt >= capf
        return jnp.where(ok, mid, lo), jnp.where(ok, hi, mid - 1)

    lo0 = jnp.zeros((N_EXPERTS, 1), I32)
    hi0 = jnp.full((N_EXPERTS, 1), 0x7F800000, I32)
    thr, _ = lax.fori_loop(0, 31, bis, (lo0, hi0))
    need = capf - jnp.sum(jnp.where(bits > thr, 1.0, 0.0), axis=1, keepdims=True)

    r = lax.broadcasted_iota(I32, (TOK_TILE, TOK_TILE), 0)
    c = lax.broadcasted_iota(I32, (TOK_TILE, TOK_TILE), 1)
    tri = jnp.where(r < c, 1.0, 0.0).astype(BF16)
    lane = lax.broadcasted_iota(I32, (N_EXPERTS, LANES), 1)
    n_tiles = n_tok // TOK_TILE
    run_eq = jnp.zeros((N_EXPERTS, 1), F32)
    run_sel = jnp.zeros((N_EXPERTS, 1), F32)
    offs = jnp.zeros((N_EXPERTS, LANES), F32)
    for t in range(n_tiles):
        sl = slice(t * TOK_TILE, (t + 1) * TOK_TILE)
        aff_t = aff_ref[:, sl]
        bits_t = pltpu.bitcast(aff_t, I32)
        is_eq = bits_t == thr
        eq_t = jnp.where(is_eq, 1.0, 0.0)
        rank_eq = jnp.dot(eq_t.astype(BF16), tri, preferred_element_type=F32) + run_eq
        sel_t = (bits_t > thr) | (is_eq & (rank_eq < need))
        sel_f = jnp.where(sel_t, 1.0, 0.0)
        pos_t = jnp.dot(sel_f.astype(BF16), tri, preferred_element_type=F32) + run_sel
        offs = jnp.where(lane == t, run_sel, offs)
        gate_ref[:, sl] = jnp.where(sel_t, aff_t, 0.0)
        pos_ref[:, sl] = jnp.where(sel_t, pos_t, -1.0).astype(I32)
        run_eq = run_eq + jnp.sum(eq_t, axis=1, keepdims=True)
        run_sel = run_sel + jnp.sum(sel_f, axis=1, keepdims=True)
    offs = jnp.where(lane >= n_tiles, run_sel, offs)
    offs_ref[...] = offs.astype(I32)


def _select_call(aff_t, cap):
    n_tok = aff_t.shape[1]
    full = lambda shape: pl.BlockSpec(shape, lambda i: (0,) * len(shape))
    return pl.pallas_call(
        functools.partial(_select_body, n_tok=n_tok, cap=cap),
        grid=(1,),
        in_specs=[full((N_EXPERTS, n_tok))],
        out_specs=[full((N_EXPERTS, n_tok)), full((N_EXPERTS, n_tok)), full((N_EXPERTS, LANES))],
        out_shape=[jax.ShapeDtypeStruct((N_EXPERTS, n_tok), F32),
                   jax.ShapeDtypeStruct((N_EXPERTS, n_tok), I32),
                   jax.ShapeDtypeStruct((N_EXPERTS, LANES), I32)],
        compiler_params=_cp(("arbitrary",)), name="select",
    )(aff_t)


def _invert_body(offs_ref, pos_ref, gate_ref, idx_ref, gcol_ref, *, n_tiles, cap):
    n_chunks = cap // SLOT_CHUNK
    jcol = lax.broadcasted_iota(I32, (SLOT_CHUNK, LANES), 0)
    lane = lax.broadcasted_iota(I32, (1, LANES), 1)

    def per_expert(e, _):
        base = e * LANES

        def per_chunk(sc, t_start):
            lo = sc * SLOT_CHUNK
            hi = lo + SLOT_CHUNK
            t0 = lax.while_loop(lambda t: offs_ref[base + t + 1] <= lo, lambda t: t + 1, t_start)

            def cond(st):
                t = st[0]
                return jnp.logical_and(t < n_tiles, offs_ref[base + jnp.minimum(t, n_tiles - 1)] < hi)

            def step(st):
                t, acc_i, acc_g = st
                prow = pos_ref[pl.ds(e, 1), pl.ds(t, 1), :][0]
                grow = gate_ref[pl.ds(e, 1), pl.ds(t, 1), :][0]
                hit = prow == (jcol + lo)
                tok = (lane + t * TOK_TILE).astype(F32)
                acc_i = jnp.maximum(acc_i, jnp.where(hit, tok, -1.0))
                acc_g = jnp.maximum(acc_g, jnp.where(hit, grow, 0.0))
                return t + 1, acc_i, acc_g

            init = (t0, jnp.full((SLOT_CHUNK, LANES), -1.0, F32), jnp.zeros((SLOT_CHUNK, LANES), F32))
            _, acc_i, acc_g = lax.while_loop(cond, step, init)
            r0 = pl.multiple_of(lo, SLOT_CHUNK)
            idx_ref[pl.ds(e, 1), pl.ds(r0, SLOT_CHUNK), :] = (
                jnp.max(acc_i, axis=1, keepdims=True).astype(I32)[None])
            gcol_ref[pl.ds(e, 1), pl.ds(r0, SLOT_CHUNK), :] = jnp.max(acc_g, axis=1, keepdims=True)[None]
            return t0

        lax.fori_loop(0, n_chunks, per_chunk, 0)
        return 0

    lax.fori_loop(0, N_EXPERTS, per_expert, 0)


def _invert_call(offs, pos, gate, cap):
    n_tok = pos.shape[1]
    n_tiles = n_tok // TOK_TILE
    pos3 = pos.reshape(N_EXPERTS, n_tiles, TOK_TILE)
    gate3 = gate.reshape(N_EXPERTS, n_tiles, TOK_TILE)
    full = lambda shape: pl.BlockSpec(shape, lambda i, o: (0,) * len(shape))
    return pl.pallas_call(
        functools.partial(_invert_body, n_tiles=n_tiles, cap=cap),
        grid_spec=pltpu.PrefetchScalarGridSpec(
            num_scalar_prefetch=1, grid=(1,),
            in_specs=[full((N_EXPERTS, n_tiles, TOK_TILE)), full((N_EXPERTS, n_tiles, TOK_TILE))],
            out_specs=[full((N_EXPERTS, cap, 1)), full((N_EXPERTS, cap, 1))]),
        out_shape=[jax.ShapeDtypeStruct((N_EXPERTS, cap, 1), I32),
                   jax.ShapeDtypeStruct((N_EXPERTS, cap, 1), F32)],
        compiler_params=_cp(("arbitrary",)), name="invert",
    )(offs.reshape(-1), pos3, gate3)


def _ffn_body(idx_ref, gcol_ref, wg_ref, wu_ref, wd_ref, h0_hbm, h1_hbm, out_ref,
              xe, wgb, wub, wdb, sem, *, cap):
    p = pl.program_id(1)

    @pl.when(p == 0)
    def _():
        wgb[...] = wg_ref[0].astype(BF16)
        wub[...] = wu_ref[0].astype(BF16)
        wdb[...] = wd_ref[0].astype(BF16)

    def row_copy(src, tok, j):
        return pltpu.make_async_copy(src.at[pl.ds(tok, 1), :], xe.at[pl.ds(j, 1), :], sem)

    def gather(src):
        def issue(j, _):
            row_copy(src, idx_ref[0, 0, 0, j], j).start()
            return 0
        lax.fori_loop(0, cap, issue, 0)

        def drain(j, _):
            row_copy(src, 0, j).wait()
            return 0
        lax.fori_loop(0, cap, drain, 0)

    @pl.when(p == 0)
    def _():
        gather(h0_hbm)

    @pl.when(p == 1)
    def _():
        gather(h1_hbm)

    ch = 256
    for c in range(cap // ch):
        sl = slice(c * ch, (c + 1) * ch)
        xb = xe[sl, :].astype(BF16)
        g = jnp.dot(xb, wgb[...], preferred_element_type=F32)
        u = jnp.dot(xb, wub[...], preferred_element_type=F32)
        hid = (g * _sigmoid(g)) * u
        ye = jnp.dot(hid.astype(BF16), wdb[...], preferred_element_type=F32)
        out_ref[0, 0, sl, :] = ye * gcol_ref[0, 0, sl, :]


def _ffn_call(idx, gcol, w_gate, w_up, w_down, h_ctx, h_lat, cap):
    d_e = w_gate.shape[2]
    wspec = lambda a, b: pl.BlockSpec((1, a, b), lambda e, p: (e, 0, 0))
    return pl.pallas_call(
        functools.partial(_ffn_body, cap=cap),
        grid=(N_EXPERTS, 2),
        in_specs=[pl.BlockSpec((1, 1, 1, cap), lambda e, p: (p, e, 0, 0), memory_space=pltpu.SMEM),
                  pl.BlockSpec((1, 1, cap, 1), lambda e, p: (p, e, 0, 0)),
                  wspec(D_MODEL, d_e), wspec(D_MODEL, d_e), wspec(d_e, D_MODEL),
                  pl.BlockSpec(memory_space=pl.ANY), pl.BlockSpec(memory_space=pl.ANY)],
        out_specs=pl.BlockSpec((1, 1, cap, D_MODEL), lambda e, p: (p, e, 0, 0)),
        out_shape=jax.ShapeDtypeStruct((2, N_EXPERTS, cap, D_MODEL), F32),
        scratch_shapes=[pltpu.VMEM((cap, D_MODEL), F32), pltpu.VMEM((D_MODEL, d_e), BF16),
                        pltpu.VMEM((D_MODEL, d_e), BF16), pltpu.VMEM((d_e, D_MODEL), BF16),
                        pltpu.SemaphoreType.DMA(())],
        compiler_params=_cp(("arbitrary", "arbitrary")), name="ffn",
    )(idx, gcol, w_gate, w_up, w_down, h_ctx, h_lat)


def _combine_body(offs_ref, idx_ref, x1_ref, mod_ref, gf_ref, yb_hbm, y_ref, stage, sem, *, cap, path):
    t = pl.program_id(0)
    stage[...] = jnp.zeros(stage.shape, F32)

    def row_copy(e, j, row):
        return pltpu.make_async_copy(yb_hbm.at[path, e, pl.ds(j, 1), :],
                                     stage.at[e, pl.ds(row, 1), :], sem)

    total = 0
    for e in range(N_EXPERTS):
        s0 = offs_ref[e * LANES + t]
        s1 = offs_ref[e * LANES + t + 1]

        def issue(j, _, e=e):
            row_copy(e, j, idx_ref[e * cap + j] - t * TOK_TILE).start()
            return 0
        lax.fori_loop(s0, s1, issue, 0)
        total = total + (s1 - s0)

    def drain(_, c):
        row_copy(0, 0, 0).wait()
        return c
    lax.fori_loop(0, total, drain, 0)

    acc = stage[0]
    for e in range(1, N_EXPERTS):
        acc = acc + stage[e]
    gt2 = mod_ref[0][:, 5 * D_MODEL:6 * D_MODEL]
    x2 = x1_ref[...] + gt2 * acc
    ms = jnp.mean(x2 * x2, axis=-1, keepdims=True)
    y_ref[...] = x2 * lax.rsqrt(ms + EPS) * gf_ref[...]


def _combine_call(offs, idx, x1, mod3, gf, ybuf, mod_row, cap, path):
    n = x1.shape[0]
    return pl.pallas_call(
        functools.partial(_combine_body, cap=cap, path=path),
        grid_spec=pltpu.PrefetchScalarGridSpec(
            num_scalar_prefetch=2, grid=(n // TOK_TILE,),
            in_specs=[pl.BlockSpec((TOK_TILE, D_MODEL), lambda i, o, x: (i, 0)),
                      pl.BlockSpec((1, 1, 6 * D_MODEL), lambda i, o, x: (mod_row(i), 0, 0)),
                      pl.BlockSpec((1, D_MODEL), lambda i, o, x: (0, 0)),
                      pl.BlockSpec(memory_space=pl.ANY)],
            out_specs=pl.BlockSpec((TOK_TILE, D_MODEL), lambda i, o, x: (i, 0)),
            scratch_shapes=[pltpu.VMEM((N_EXPERTS, TOK_TILE, D_MODEL), F32),
                            pltpu.SemaphoreType.DMA(())]),
        out_shape=jax.ShapeDtypeStruct((n, D_MODEL), F32),
        compiler_params=_cp(("arbitrary",)), name="combine",
    )(offs.reshape(-1), idx.reshape(-1), x1, mod3, gf, ybuf)


def _lru_gate_weights(wa, wx):
    eye = jnp.eye(4, dtype=F32)

    def bd(w):
        w5 = w.reshape(4, 4, LRU_HEAD_DIM, LRU_HEAD_DIM)
        return jnp.einsum('khij,hg->khigj', w5, eye).reshape(4, 256, 256)

    return jnp.concatenate([bd(wa[0]), bd(wa[1]), bd(wx[0]), bd(wx[1])], axis=2).astype(BF16)


def kernel(x_prompt, x_sample, state_lru, state_s5_re, state_s5_im, c, c_ctx, w_mod, b_mod, g_norm1, g_norm2, w_in, conv_w, conv_b, lru_wa, lru_ba, lru_wx, lru_bx, lru_lambda, s5_lambda_re, s5_lambda_im, s5_log_step, s5_b_re, s5_b_im, s5_c_re, s5_c_im, s5_d, s5_w_glu, s5_b_glu, w_lru_out, w_s5_out, w_branch_gate, b_branch_gate, w_o, w_router, b_router, w_e_gate, w_e_up, w_e_down, g_final):
    bp, tp, _ = x_prompt.shape
    bs, ts, _ = x_sample.shape
    n_p, n_s = bp * tp, bs * ts
    l = 0

    c_all = jnp.zeros((16, D_MODEL), F32).at[0].set(c_ctx).at[1:1 + bs].set(c)
    mod3 = _mod_call(c_all, w_mod[l], b_mod[l][None, :]).reshape(16, 1, 6 * D_MODEL)

    w_in_b = w_in[l].astype(BF16)
    w_main, w_s5in = w_in_b[:, :2 * D_LRU], w_in_b[:, 2 * D_LRU:]
    g1 = g_norm1[l][None, :]
    g2 = g_norm2[l][None, :]
    wg = _lru_gate_weights(lru_wa[l], lru_wx[l])
    wts, wc, apow = _s5_weights(s5_lambda_re[l], s5_lambda_im[l], s5_log_step[l],
                                s5_b_re[l], s5_b_im[l], s5_c_re[l], s5_c_im[l])
    wbg = w_branch_gate[l].astype(BF16)
    bbg = b_branch_gate[l][None, :]
    wglu = s5_w_glu[l].astype(BF16)
    wso = w_s5_out[l].astype(BF16)
    wlo = w_lru_out[l].astype(BF16)
    wo = w_o[l].astype(BF16)
    wrt = w_router[l].T
    brt = b_router[l][:, None]

    xp2 = x_prompt.reshape(n_p, D_MODEL)
    xs2 = x_sample.reshape(n_s, D_MODEL)
    ctx_row = lambda i: 0
    tm_in = 512
    lat_row_in = lambda i: 1 + i // (ts // tm_in)

    def s5_state(re, im):
        def part(a, d):
            return a[:, d].reshape(-1, S5_BLK, 8 * S5_N).transpose(1, 0, 2)
        return jnp.concatenate([part(re, 0), part(im, 0), part(re, 1), part(im, 1)], axis=2)

    xb_p, gg_p, us4_p = _inproj_call(xp2, mod3, g1, w_main, w_s5in, ctx_row, tm_in, True)
    ya_p, lru_fin = _lru_call(xb_p, gg_p, conv_w[l], conv_b[l][None, :], wg, lru_ba[l], lru_bx[l],
                              lru_lambda[l], jnp.zeros((bp, 2, D_LRU), F32), bp, tp)
    y4_p, s5_fin = _s5_call(us4_p, wts, wc, apow, jnp.zeros((S5_BLK, bp, S5_SW), F32), bp, tp, False)
    xb_s, gg_s = _inproj_call(xs2, mod3, g1, w_main, None, lat_row_in, tm_in, False)
    us4_s = _inproj_scan_call(x_sample, mod3, g1, w_s5in)
    ya_s, _ = _lru_call(xb_s, gg_s, conv_w[l], conv_b[l][None, :], wg, lru_ba[l], lru_bx[l],
                        lru_lambda[l], state_lru[:, l], bs, ts)
    y4_s, _ = _s5_call(us4_s, wts, wc, apow, s5_state(state_s5_re[:, l], state_s5_im[:, l]), bs, ts, True)

    mo_args = (wbg, bbg)
    mo_tail = (s5_d[l][None, :], wglu, s5_b_glu[l][None, :], wso, wlo, wo, wrt, brt)
    tm_mo = 256
    x1_p, h2_p, aff_p = _mixout_call(xp2, mod3, g1, g2, *mo_args, ya_p, y4_p, us4_p, *mo_tail,
                                     mod_row=ctx_row, permuted=False, tm=tm_mo)
    x1_s, h2_s, aff_s = _mixout_call(xs2, mod3, g1, g2, *mo_args, ya_s, y4_s, us4_s, *mo_tail,
                                     mod_row=lambda i: 1 + i // (ts // tm_mo), permuted=True, tm=tm_mo)

    cap = (CAPACITY_FACTOR * n_p) // N_EXPERTS
    routes = []
    for aff in (aff_p, aff_s):
        gate, pos, offs = _select_call(aff, cap)
        idx, gcol = _invert_call(offs, pos, gate, cap)
        routes.append((offs, idx.reshape(N_EXPERTS, cap), gcol))
    idx_all = jnp.stack([routes[0][1], routes[1][1]])[:, :, None, :]
    gcol_all = jnp.stack([routes[0][2], routes[1][2]])
    ybuf = _ffn_call(idx_all, gcol_all, w_e_gate[l], w_e_up[l], w_e_down[l], h2_p, h2_s, cap)

    gf = g_final[None, :]
    y_p = _combine_call(routes[0][0], routes[0][1], x1_p, mod3, gf, ybuf, ctx_row, cap, 0)
    y_s = _combine_call(routes[1][0], routes[1][1], x1_s, mod3, gf, ybuf,
                        lambda i: 1 + i // (ts // TOK_TILE), cap, 1)

    new_lru = lru_fin[:, None]
    sf = s5_fin.transpose(1, 0, 2).reshape(bp, S5_BLK, 2, 2, 8, S5_N)
    sf = sf.transpose(0, 2, 3, 1, 4, 5).reshape(bp, 2, 2, S5_G, S5_N)
    new_s5r = sf[:, :, 0][:, None]
    new_s5i = sf[:, :, 1][:, None]
    return (y_p.reshape(bp, tp, D_MODEL), y_s.reshape(bs, ts, D_MODEL), new_lru, new_s5r, new_s5i)
```

```python
import functools
import math

import jax
import jax.numpy as jnp
from jax import lax
from jax.experimental import pallas as pl
from jax.experimental.pallas import tpu as pltpu

F32 = jnp.float32
BF16 = jnp.bfloat16
I32 = jnp.int32

D_MODEL = 1024
D_LRU = 1024
LRU_HEADS = 16
LRU_HEAD_DIM = 64
LRU_C = 8.0
CONV_W = 4
D_S5 = 512
S5_P = 16
S5_G = 32
S5_N = 64
GRID_W = 64
N_EXPERTS = 16
CAPACITY_FACTOR = 2
EPS = 1e-6

LANES = 128
SUBLANES = 8
S5_L = 8
S5_BLK = 4
S5_SW = 4 * 512
TOK_TILE = 128
SLOT_CHUNK = 64
VMEM_LIMIT = 56 * 1024 * 1024


def _cp(sem, vmem=VMEM_LIMIT):
    return pltpu.CompilerParams(dimension_semantics=sem, vmem_limit_bytes=vmem)


def _bdot(a, b):
    return jnp.dot(a.astype(BF16), b.astype(BF16), preferred_element_type=F32)


def _split2(a):
    hi = a.astype(BF16)
    lo = (a - hi.astype(F32)).astype(BF16)
    return hi, lo


def _dot3(a, b, dims=(((1,), (0,)), ((), ()))):
    ah, al = _split2(a)
    bh, bl = _split2(b)
    d = functools.partial(lax.dot_general, dimension_numbers=dims, preferred_element_type=F32)
    return d(ah, bh) + (d(al, bh) + d(ah, bl))


def _sigmoid(x):
    return 1.0 / (1.0 + jnp.exp(-x))


def _gelu(x):
    c = math.sqrt(2.0 / math.pi)
    return 0.5 * x * (1.0 + jnp.tanh(c * (x + 0.044715 * (x * x * x))))


def _norm_mod(x, g, scale, shift):
    ms = jnp.mean(x * x, axis=-1, keepdims=True)
    return (x * lax.rsqrt(ms + EPS) * g) * (1.0 + scale) + shift


def _mod_body(c_ref, w_ref, b_ref, o_ref):
    c = c_ref[...]
    s = c * _sigmoid(c)
    o_ref[...] = _dot3(s, w_ref[...]) + b_ref[...]


def _mod_call(c_all, w_mod, b_mod):
    n = w_mod.shape[1]
    tn = 1536
    return pl.pallas_call(
        _mod_body,
        grid=(n // tn,),
        in_specs=[pl.BlockSpec((16, D_MODEL), lambda j: (0, 0)),
                  pl.BlockSpec((D_MODEL, tn), lambda j: (0, j)),
                  pl.BlockSpec((1, tn), lambda j: (0, j))],
        out_specs=pl.BlockSpec((16, tn), lambda j: (0, j)),
        out_shape=jax.ShapeDtypeStruct((16, n), F32),
        compiler_params=_cp(("arbitrary",)),
        name="mod",
    )(c_all, w_mod, b_mod)


def _inproj_body(x_ref, mod_ref, g_ref, w_ref, ws_ref, xb_ref, gg_ref, us_ref):
    m = mod_ref[0]
    h = _norm_mod(x_ref[...], g_ref[...], m[:, D_MODEL:2 * D_MODEL], m[:, 0:D_MODEL])
    hb = h.astype(BF16)
    z = jnp.dot(hb, w_ref[...], preferred_element_type=F32)
    xb_ref[...] = z[:, :D_LRU]
    gg_ref[...] = _gelu(z[:, D_LRU:]).astype(BF16)
    us = jnp.dot(hb, ws_ref[...], preferred_element_type=F32)
    for k in range(S5_BLK):
        us_ref[k] = us[:, k * LANES:(k + 1) * LANES]


def _inproj_nos5_body(x_ref, mod_ref, g_ref, w_ref, xb_ref, gg_ref):
    m = mod_ref[0]
    h = _norm_mod(x_ref[...], g_ref[...], m[:, D_MODEL:2 * D_MODEL], m[:, 0:D_MODEL])
    z = jnp.dot(h.astype(BF16), w_ref[...], preferred_element_type=F32)
    xb_ref[...] = z[:, :D_LRU]
    gg_ref[...] = _gelu(z[:, D_LRU:]).astype(BF16)


def _inproj_call(x2d, mod3, g1, w_main, w_s5, mod_row, tm, with_s5):
    n = x2d.shape[0]
    in_specs = [pl.BlockSpec((tm, D_MODEL), lambda i: (i, 0)),
                pl.BlockSpec((1, 1, 6 * D_MODEL), lambda i: (mod_row(i), 0, 0)),
                pl.BlockSpec((1, D_MODEL), lambda i: (0, 0)),
                pl.BlockSpec((D_MODEL, 2 * D_LRU), lambda i: (0, 0))]
    out_specs = [pl.BlockSpec((tm, D_LRU), lambda i: (i, 0)),
                 pl.BlockSpec((tm, D_LRU), lambda i: (i, 0))]
    out_shape = [jax.ShapeDtypeStruct((n, D_LRU), F32), jax.ShapeDtypeStruct((n, D_LRU), BF16)]
    args = [x2d, mod3, g1, w_main]
    body = _inproj_nos5_body
    if with_s5:
        in_specs.append(pl.BlockSpec((D_MODEL, D_S5), lambda i: (0, 0)))
        out_specs.append(pl.BlockSpec((S5_BLK, tm, LANES), lambda i: (0, i, 0)))
        out_shape.append(jax.ShapeDtypeStruct((S5_BLK, n, LANES), F32))
        args.append(w_s5)
        body = _inproj_body
    return pl.pallas_call(
        body, grid=(n // tm,), in_specs=in_specs, out_specs=out_specs, out_shape=out_shape,
        compiler_params=_cp(("parallel",)), name="inproj",
    )(*args)


def _inproj_scan_body(x_ref, mod_ref, g_ref, ws_ref, us_ref, *, ncol):
    xs = jnp.concatenate([x_ref[0, :, c * D_MODEL:(c + 1) * D_MODEL] for c in range(ncol)], axis=0)
    m = mod_ref[0]
    h = _norm_mod(xs, g_ref[...], m[:, D_MODEL:2 * D_MODEL], m[:, 0:D_MODEL])
    us = jnp.dot(h.astype(BF16), ws_ref[...], preferred_element_type=F32)
    for k in range(S5_BLK):
        us_ref[k] = us[:, k * LANES:(k + 1) * LANES]


def _inproj_scan_call(x_sample, mod3, g1, w_s5):
    b, t, _ = x_sample.shape
    rows = t // GRID_W
    ncol = 16
    xv = x_sample.reshape(b, rows, GRID_W * D_MODEL)
    tm = ncol * rows
    nj = GRID_W // ncol
    return pl.pallas_call(
        functools.partial(_inproj_scan_body, ncol=ncol),
        grid=(b, nj),
        in_specs=[pl.BlockSpec((1, rows, ncol * D_MODEL), lambda i, j: (i, 0, j)),
                  pl.BlockSpec((1, 1, 6 * D_MODEL), lambda i, j: (1 + i, 0, 0)),
                  pl.BlockSpec((1, D_MODEL), lambda i, j: (0, 0)),
                  pl.BlockSpec((D_MODEL, D_S5), lambda i, j: (0, 0))],
        out_specs=pl.BlockSpec((S5_BLK, tm, LANES), lambda i, j: (0, i * nj + j, 0)),
        out_shape=jax.ShapeDtypeStruct((S5_BLK, b * t, LANES), F32),
        compiler_params=_cp(("parallel", "parallel")), name="inproj_scan",
    )(xv, mod3, g1, w_s5)


def _lru_body(xb_ref, gg_ref, cw_ref, cb_ref, wg_ref, ba_ref, bx_ref, lam_ref, h0_ref,
              ya_ref, fin_ref, xpad, a_f, b_f, a_b, b_b, *, T):
    ch = 256
    nch = T // ch
    zero8 = jnp.zeros((SUBLANES, D_LRU), F32)
    xpad[0:SUBLANES, :] = zero8
    xpad[T + SUBLANES:T + 2 * SUBLANES, :] = zero8
    for c in range(nch):
        xpad[SUBLANES + c * ch:SUBLANES + (c + 1) * ch, :] = xb_ref[0, c * ch:(c + 1) * ch, :]

    nl = -lam_ref[...]
    softplus = jnp.maximum(nl, 0.0) + jnp.log1p(jnp.exp(-jnp.abs(nl)))
    sub = lax.broadcasted_iota(I32, (1, SUBLANES, D_LRU), 1)
    scr = ((a_f, b_f), (a_b, b_b))

    for c in range(nch):
        t0 = c * ch
        xc = cb_ref[...] + sum(
            xpad[t0 + SUBLANES - 2 + k:t0 + SUBLANES - 2 + k + ch, :] * cw_ref[k:k + 1, :]
            for k in range(CONV_W))
        xcb = xc.astype(BF16)
        pre = [jnp.dot(xcb[:, kb * 256:(kb + 1) * 256], wg_ref[kb], preferred_element_type=F32)
               for kb in range(4)]
        for d in range(2):
            ra = jnp.concatenate([p[:, d * 256:(d + 1) * 256] for p in pre], axis=1)
            gx = jnp.concatenate([p[:, 512 + d * 256:512 + (d + 1) * 256] for p in pre], axis=1)
            r = _sigmoid(ra + ba_ref[d:d + 1, :])
            gi = _sigmoid(gx + bx_ref[d:d + 1, :])
            log_a = (-LRU_C) * r * softplus[d:d + 1, :]
            a = jnp.exp(log_a)
            u = jnp.sqrt(1.0 - a * a) * gi * xc
            a3 = a.reshape(ch // SUBLANES, SUBLANES, D_LRU)
            u3 = u.reshape(ch // SUBLANES, SUBLANES, D_LRU)
            for s in (1, 2, 4):
                if d == 0:
                    a_s = pltpu.roll(a3, s, 1)
                    u_s = pltpu.roll(u3, s, 1)
                    keep = sub >= s
                else:
                    a_s = pltpu.roll(a3, SUBLANES - s, 1)
                    u_s = pltpu.roll(u3, SUBLANES - s, 1)
                    keep = sub < SUBLANES - s
                u3 = u3 + a3 * jnp.where(keep, u_s, 0.0)
                a3 = a3 * jnp.where(keep, a_s, 1.0)
            scr[d][0][t0:t0 + ch, :] = a3.reshape(ch, D_LRU)
            scr[d][1][t0:t0 + ch, :] = u3.reshape(ch, D_LRU)

    ntile = T // SUBLANES

    def fwd(i, carry):
        r0 = pl.multiple_of(i * SUBLANES, SUBLANES)
        h = a_f[pl.ds(r0, SUBLANES), :] * carry + b_f[pl.ds(r0, SUBLANES), :]
        b_f[pl.ds(r0, SUBLANES), :] = h
        return h[SUBLANES - 1:SUBLANES, :]

    def bwd(i, carry):
        r0 = pl.multiple_of((ntile - 1 - i) * SUBLANES, SUBLANES)
        h = a_b[pl.ds(r0, SUBLANES), :] * carry + b_b[pl.ds(r0, SUBLANES), :]
        b_b[pl.ds(r0, SUBLANES), :] = h
        return h[0:1, :]

    fin_ref[0, 0:1, :] = lax.fori_loop(0, ntile, fwd, h0_ref[0, 0:1, :])
    fin_ref[0, 1:2, :] = lax.fori_loop(0, ntile, bwd, h0_ref[0, 1:2, :])

    for c in range(nch):
        sl = slice(c * ch, (c + 1) * ch)
        ya_ref[0, sl, :] = (gg_ref[0, sl, :].astype(F32) * (b_f[sl, :] + b_b[sl, :])).astype(BF16)


def _lru_call(xb, gg, conv_w, conv_b, wg, ba, bx, lam, h0, n_seq, T):
    xb3 = xb.reshape(n_seq, T, D_LRU)
    gg3 = gg.reshape(n_seq, T, D_LRU)
    full = lambda shape: pl.BlockSpec(shape, lambda i: (0,) * len(shape))
    seq = lambda shape: pl.BlockSpec(shape, lambda i: (i,) + (0,) * (len(shape) - 1))
    ya, fin = pl.pallas_call(
        functools.partial(_lru_body, T=T),
        grid=(n_seq,),
        in_specs=[seq((1, T, D_LRU)), seq((1, T, D_LRU)), full((CONV_W, D_LRU)), full((1, D_LRU)),
                  full((4, 256, 1024)), full((2, D_LRU)), full((2, D_LRU)), full((2, D_LRU)),
                  seq((1, 2, D_LRU))],
        out_specs=[seq((1, T, D_LRU)), seq((1, 2, D_LRU))],
        out_shape=[jax.ShapeDtypeStruct((n_seq, T, D_LRU), BF16),
                   jax.ShapeDtypeStruct((n_seq, 2, D_LRU), F32)],
        scratch_shapes=[pltpu.VMEM((T + 2 * SUBLANES, D_LRU), F32)] + [pltpu.VMEM((T, D_LRU), F32)] * 4,
        compiler_params=_cp(("parallel",)), name="lru",
    )(xb3, gg3, conv_w, conv_b, wg, ba, bx, lam, h0)
    return ya.reshape(n_seq * T, D_LRU), fin


def _s5_body(u_ref, wts_ref, wc_ref, ap_ref, h0_ref, y_ref, fin_ref, s_scr, hp_scr,
             *, rt, n_c, with_h0):
    nseq = rt // n_c
    q = 512
    ub = u_ref[0].astype(BF16)
    r1 = jnp.dot(ub, wts_ref[0], preferred_element_type=F32)
    y_in = r1[:, :S5_L * LANES]
    s_scr[...] = r1[:, S5_L * LANES:]
    ap = ap_ref[0]
    if with_h0:
        for s in range(nseq):
            h0 = h0_ref[0, 0, s:s + 1, :]
            r_f = s * n_c
            r_b = s * n_c + n_c - 1
            ar, ai = ap[0:1, 0:q], ap[0:1, q:2 * q]
            s_scr[r_f:r_f + 1, 0:q] += ar * h0[:, 0:q] - ai * h0[:, q:2 * q]
            s_scr[r_f:r_f + 1, q:2 * q] += ar * h0[:, q:2 * q] + ai * h0[:, 0:q]
            br, bi = ap[0:1, 2 * q:3 * q], ap[0:1, 3 * q:4 * q]
            s_scr[r_b:r_b + 1, 2 * q:3 * q] += br * h0[:, 2 * q:3 * q] - bi * h0[:, 3 * q:4 * q]
            s_scr[r_b:r_b + 1, 3 * q:4 * q] += br * h0[:, 3 * q:4 * q] + bi * h0[:, 2 * q:3 * q]

    cidx = lax.broadcasted_iota(I32, (rt, 1), 0) % n_c
    hfr, hfi = s_scr[:, 0:q], s_scr[:, q:2 * q]
    hbr, hbi = s_scr[:, 2 * q:3 * q], s_scr[:, 3 * q:4 * q]
    lvl = 0
    dist = 1
    while dist < n_c:
        ar, ai = ap[lvl:lvl + 1, 0:q], ap[lvl:lvl + 1, q:2 * q]
        br, bi = ap[lvl:lvl + 1, 2 * q:3 * q], ap[lvl:lvl + 1, 3 * q:4 * q]
        kf = cidx >= dist
        kb = cidx < n_c - dist
        sfr = jnp.where(kf, pltpu.roll(hfr, dist, 0), 0.0)
        sfi = jnp.where(kf, pltpu.roll(hfi, dist, 0), 0.0)
        sbr = jnp.where(kb, pltpu.roll(hbr, rt - dist, 0), 0.0)
        sbi = jnp.where(kb, pltpu.roll(hbi, rt - dist, 0), 0.0)
        hfr, hfi = hfr + (ar * sfr - ai * sfi), hfi + (ar * sfi + ai * sfr)
        hbr, hbi = hbr + (br * sbr - bi * sbi), hbi + (br * sbi + bi * sbr)
        lvl += 1
        dist *= 2

    s_scr[:, 0:q] = hfr
    s_scr[:, q:2 * q] = hfi
    s_scr[:, 2 * q:3 * q] = hbr
    s_scr[:, 3 * q:4 * q] = hbi
    for s in range(nseq):
        fin_ref[0, 0, s:s + 1, 0:2 * q] = s_scr[s * n_c + n_c - 1:s * n_c + n_c, 0:2 * q]
        fin_ref[0, 0, s:s + 1, 2 * q:4 * q] = s_scr[s * n_c:s * n_c + 1, 2 * q:4 * q]

    kf = cidx >= 1
    kb = cidx < n_c - 1
    hp_scr[:, 0:q] = jnp.where(kf, pltpu.roll(hfr, 1, 0), 0.0)
    hp_scr[:, q:2 * q] = jnp.where(kf, pltpu.roll(hfi, 1, 0), 0.0)
    hp_scr[:, 2 * q:3 * q] = jnp.where(kb, pltpu.roll(hbr, rt - 1, 0), 0.0)
    hp_scr[:, 3 * q:4 * q] = jnp.where(kb, pltpu.roll(hbi, rt - 1, 0), 0.0)
    if with_h0:
        for s in range(nseq):
            hp_scr[s * n_c:s * n_c + 1, 0:2 * q] = h0_ref[0, 0, s:s + 1, 0:2 * q]
            hp_scr[s * n_c + n_c - 1:s * n_c + n_c, 2 * q:4 * q] = h0_ref[0, 0, s:s + 1, 2 * q:4 * q]
    y_ref[0] = y_in + jnp.dot(hp_scr[...].astype(BF16), wc_ref[0], preferred_element_type=F32)


def _s5_call(us4, wts, wc, apow, h0, n_seq, T, with_h0):
    n_c = T // S5_L
    rows = n_seq * n_c
    rt = 256
    nseq_t = rt // n_c
    u4 = us4.reshape(S5_BLK, rows, S5_L * LANES)
    h0 = h0.reshape(S5_BLK, n_seq // nseq_t, nseq_t, S5_SW)
    y4, fin = pl.pallas_call(
        functools.partial(_s5_body, rt=rt, n_c=n_c, with_h0=with_h0),
        grid=(S5_BLK, rows // rt),
        in_specs=[pl.BlockSpec((1, rt, S5_L * LANES), lambda b, r: (b, r, 0)),
                  pl.BlockSpec((1, S5_L * LANES, S5_L * LANES + S5_SW), lambda b, r: (b, 0, 0)),
                  pl.BlockSpec((1, S5_SW, S5_L * LANES), lambda b, r: (b, 0, 0)),
                  pl.BlockSpec((1, 8, S5_SW), lambda b, r: (b, 0, 0)),
                  pl.BlockSpec((1, 1, nseq_t, S5_SW), lambda b, r: (b, r, 0, 0))],
        out_specs=[pl.BlockSpec((1, rt, S5_L * LANES), lambda b, r: (b, r, 0)),
                   pl.BlockSpec((1, 1, nseq_t, S5_SW), lambda b, r: (b, r, 0, 0))],
        out_shape=[jax.ShapeDtypeStruct((S5_BLK, rows, S5_L * LANES), F32),
                   jax.ShapeDtypeStruct((S5_BLK, n_seq // nseq_t, nseq_t, S5_SW), F32)],
        scratch_shapes=[pltpu.VMEM((rt, S5_SW), F32), pltpu.VMEM((rt, S5_SW), F32)],
        compiler_params=_cp(("parallel", "parallel")), name="s5",
    )(u4, wts, wc, apow, h0)
    return y4.reshape(S5_BLK, n_seq * T, LANES), fin.reshape(S5_BLK, n_seq, S5_SW)


def _s5_weights(lam_re, lam_im, log_step, b_re, b_im, c_re, c_im):
    L, G, N, P = S5_L, S5_G, S5_N, S5_P
    hp = lax.Precision.HIGHEST
    step = jnp.exp(log_step)[..., None]
    lr, li = lam_re, lam_im
    mag = jnp.exp(lr * step)
    ang = li * step
    ar, ai = mag * jnp.cos(ang), mag * jnp.sin(ang)
    den = lr * lr + li * li
    fr = ((ar - 1.0) * lr + ai * li) / den
    fi = (ai * lr - (ar - 1.0) * li) / den
    bbr = fr[..., None] * b_re - fi[..., None] * b_im
    bbi = fr[..., None] * b_im + fi[..., None] * b_re
    pr, pi = [jnp.ones_like(ar)], [jnp.zeros_like(ar)]
    for _ in range(L):
        pr.append(pr[-1] * ar - pi[-1] * ai)
        pi.append(pr[-2] * ai + pi[-1] * ar)
    pr = jnp.stack(pr, axis=2)
    pi = jnp.stack(pi, axis=2)
    er = pr[..., None] * bbr[:, :, None] - pi[..., None] * bbi[:, :, None]
    ei = pr[..., None] * bbi[:, :, None] + pi[..., None] * bbr[:, :, None]
    kt = (jnp.einsum('dgpn,dgknq->dgkpq', c_re, er, precision=hp)
          - jnp.einsum('dgpn,dgknq->dgkpq', c_im, ei, precision=hp))
    ii = jnp.arange(L)
    dif = ii[None, :] - ii[:, None]
    tf = jnp.where((dif >= 0)[None, :, :, None, None], kt[0][:, jnp.clip(dif, 0, L)], 0.0)
    tb = jnp.where((dif <= 0)[None, :, :, None, None], kt[1][:, jnp.clip(-dif, 0, L)], 0.0)
    toe = (tf + tb).reshape(S5_BLK, 8, L, L, P, P)
    eye = jnp.eye(8, dtype=F32)
    wt = jnp.einsum('bgjipq,gh->bjgqihp', toe, eye).reshape(S5_BLK, L * LANES, L * LANES)

    def embed_state(e):
        e6 = e.reshape(S5_BLK, 8, L, N, P)
        return jnp.einsum('bgjnq,gh->bjgqhn', e6, eye).reshape(S5_BLK, L * LANES, 8 * N)

    rev = jnp.arange(L - 1, -1, -1)
    ws = jnp.concatenate([embed_state(er[0][:, rev]), embed_state(ei[0][:, rev]),
                          embed_state(er[1][:, :L]), embed_state(ei[1][:, :L])], axis=2)
    wts = jnp.concatenate([wt, ws], axis=2).astype(BF16)

    def embed_corr(cr, ci, p_r, p_i):
        g_r = cr[:, None] * p_r[:, :, None, :] - ci[:, None] * p_i[:, :, None, :]
        g_i = -(cr[:, None] * p_i[:, :, None, :] + ci[:, None] * p_r[:, :, None, :])
        def emb(gm):
            g6 = gm.reshape(S5_BLK, 8, L, P, N)
            return jnp.einsum('bgipn,gh->bgnihp', g6, eye).reshape(S5_BLK, 8 * N, L * LANES)
        return emb(g_r), emb(g_i)

    kf = jnp.arange(1, L + 1)
    kb = jnp.arange(L, 0, -1)
    cfr, cfi = embed_corr(c_re[0], c_im[0], pr[0][:, kf], pi[0][:, kf])
    cbr, cbi = embed_corr(c_re[1], c_im[1], pr[1][:, kb], pi[1][:, kb])
    wc = jnp.concatenate([cfr, cfi, cbr, cbi], axis=1).astype(BF16)

    qr, qi = pr[:, :, L], pi[:, :, L]
    lv = []
    for _ in range(8):
        lv.append(jnp.concatenate([qr[0].reshape(S5_BLK, 8 * N), qi[0].reshape(S5_BLK, 8 * N),
                                   qr[1].reshape(S5_BLK, 8 * N), qi[1].reshape(S5_BLK, 8 * N)], axis=1))
        qr, qi = qr * qr - qi * qi, 2.0 * qr * qi
    apow = jnp.stack(lv, axis=1)
    return wts, wc, apow


def _mixout_body(x_ref, mod_ref, g1_ref, g2_ref, wbg_ref, bbg_ref, ya_ref, y4_ref, u4_ref,
                 dsk_ref, wglu_ref, bglu_ref, wso_ref, wlo_ref, wo_ref, wrt_ref, brt_ref,
                 x1_ref, h2_ref, aff_ref, *, permuted):
    m = mod_ref[0]
    sh1, sc1, gt1 = m[:, 0:D_MODEL], m[:, D_MODEL:2 * D_MODEL], m[:, 2 * D_MODEL:3 * D_MODEL]
    sh2, sc2 = m[:, 3 * D_MODEL:4 * D_MODEL], m[:, 4 * D_MODEL:5 * D_MODEL]
    x = x_ref[...]
    h = _norm_mod(x, g1_ref[...], sc1, sh1)
    gates = _sigmoid(jnp.dot(h.astype(BF16), wbg_ref[...], preferred_element_type=F32) + bbg_ref[...])

    def assemble(ref):
        if not permuted:
            return jnp.concatenate([ref[k] for k in range(S5_BLK)], axis=1)
        nj = ref.shape[3] // LANES
        return jnp.concatenate(
            [jnp.concatenate([ref[k, 0, :, j * LANES:(j + 1) * LANES] for k in range(S5_BLK)], axis=1)
             for j in range(nj)], axis=0)

    ys = assemble(y4_ref) + dsk_ref[...] * assemble(u4_ref)
    v = _gelu(ys)
    ob = v * _sigmoid(jnp.dot(v.astype(BF16), wglu_ref[...], preferred_element_type=F32) + bglu_ref[...])
    merged = (gates[:, :D_MODEL] * jnp.dot(ya_ref[...], wlo_ref[...], preferred_element_type=F32)
              + gates[:, D_MODEL:] * jnp.dot(ob.astype(BF16), wso_ref[...], preferred_element_type=F32))
    mix = jnp.dot(merged.astype(BF16), wo_ref[...], preferred_element_type=F32)
    x1 = x + gt1 * mix
    x1_ref[...] = x1
    h2 = _norm_mod(x1, g2_ref[...], sc2, sh2)
    h2_ref[...] = h2
    logits = _dot3(wrt_ref[...], h2, dims=(((1,), (1,)), ((), ()))) + brt_ref[...]
    mx = jnp.max(logits, axis=0, keepdims=True)
    ex = jnp.exp(logits - mx)
    aff_ref[...] = ex / jnp.sum(ex, axis=0, keepdims=True)


def _mixout_call(x2d, mod3, g1, g2, wbg, bbg, ya, y4, u4, dsk, wglu, bglu, wso, wlo, wo, wrt, brt,
                 mod_row, permuted, tm):
    n = x2d.shape[0]
    full = lambda shape: pl.BlockSpec(shape, lambda i: (0,) * len(shape))
    if permuted:
        rows_t = tm // GRID_W
        per_seq = (n // 8) // tm
        seq_rows = (n // 8) // GRID_W
        y4 = y4.reshape(S5_BLK, 8, GRID_W, seq_rows * LANES)
        u4 = u4.reshape(S5_BLK, 8, GRID_W, seq_rows * LANES)
        s5_spec = pl.BlockSpec((S5_BLK, 1, GRID_W, rows_t * LANES),
                               lambda i: (0, i // per_seq, 0, i % per_seq))
    else:
        s5_spec = pl.BlockSpec((S5_BLK, tm, LANES), lambda i: (0, i, 0))
    tok = lambda w: pl.BlockSpec((tm, w), lambda i: (i, 0))
    return pl.pallas_call(
        functools.partial(_mixout_body, permuted=permuted),
        grid=(n // tm,),
        in_specs=[tok(D_MODEL), pl.BlockSpec((1, 1, 6 * D_MODEL), lambda i: (mod_row(i), 0, 0)),
                  full((1, D_MODEL)), full((1, D_MODEL)), full((D_MODEL, 2 * D_MODEL)),
                  full((1, 2 * D_MODEL)), tok(D_LRU), s5_spec, s5_spec, full((1, D_S5)),
                  full((D_S5, D_S5)), full((1, D_S5)), full((D_S5, D_MODEL)), full((D_LRU, D_MODEL)),
                  full((D_MODEL, D_MODEL)), full((N_EXPERTS, D_MODEL)), full((N_EXPERTS, 1))],
        out_specs=[tok(D_MODEL), tok(D_MODEL), pl.BlockSpec((N_EXPERTS, tm), lambda i: (0, i))],
        out_shape=[jax.ShapeDtypeStruct((n, D_MODEL), F32), jax.ShapeDtypeStruct((n, D_MODEL), F32),
                   jax.ShapeDtypeStruct((N_EXPERTS, n), F32)],
        compiler_params=_cp(("parallel",)), name="mixout",
    )(x2d, mod3, g1, g2, wbg, bbg, ya, y4, u4, dsk, wglu, bglu, wso, wlo, wo, wrt, brt)


def _select_body(aff_ref, gate_ref, pos_ref, offs_ref, *, n_tok, cap):
    aff = aff_ref[...]
    capf = float(cap)

    def bis(_, lh):
        lo, hi = lh
        mid = lo + ((hi - lo + 1) >> 1)
        cnt = jnp.sum(jnp.where(aff >= pltpu.bitcast(mid, F32), 1.0, 0.0), axis=1, keepdims=True)
        ok = cnt >= capf
        return jnp.where(ok, mid, lo), jnp.where(ok, hi, mid - 1)

    lo0 = jnp.zeros((N_EXPERTS, 1), I32)
    hi0 = jnp.full((N_EXPERTS, 1), 0x7F800000, I32)
    thr_bits, _ = lax.fori_loop(0, 31, bis, (lo0, hi0))
    thr = pltpu.bitcast(thr_bits, F32)
    need = capf - jnp.sum(jnp.where(aff > thr, 1.0, 0.0), axis=1, keepdims=True)

    r = lax.broadcasted_iota(I32, (TOK_TILE, TOK_TILE), 0)
    c = lax.broadcasted_iota(I32, (TOK_TILE, TOK_TILE), 1)
    tri = jnp.where(r < c, 1.0, 0.0).astype(BF16)
    lane = lax.broadcasted_iota(I32, (N_EXPERTS, LANES), 1)
    n_tiles = n_tok // TOK_TILE
    run_eq = jnp.zeros((N_EXPERTS, 1), F32)
    run_sel = jnp.zeros((N_EXPERTS, 1), F32)
    offs = jnp.zeros((N_EXPERTS, LANES), F32)
    for t in range(n_tiles):
        sl = slice(t * TOK_TILE, (t + 1) * TOK_TILE)
        aff_t = aff_ref[:, sl]
        is_eq = aff_t == thr
        eq_t = jnp.where(is_eq, 1.0, 0.0)
        rank_eq = jnp.dot(eq_t.astype(BF16), tri, preferred_element_type=F32) + run_eq
        cand = (aff_t > thr) | (is_eq & (rank_eq < need))
        pos_t = jnp.dot(jnp.where(cand, 1.0, 0.0).astype(BF16), tri, preferred_element_type=F32) + run_sel
        sel_t = cand & (pos_t < capf)
        sel_f = jnp.where(sel_t, 1.0, 0.0)
        offs = jnp.where(lane == t, run_sel, offs)
        gate_ref[:, sl] = jnp.where(sel_t, aff_t, 0.0)
        pos_ref[:, sl] = jnp.where(sel_t, pos_t, -1.0).astype(I32)
        run_eq = run_eq + jnp.sum(eq_t, axis=1, keepdims=True)
        run_sel = run_sel + jnp.sum(sel_f, axis=1, keepdims=True)
    offs = jnp.where(lane >= n_tiles, run_sel, offs)
    offs_ref[...] = offs.astype(I32)


def _select_call(aff_t, cap):
    n_tok = aff_t.shape[1]
    full = lambda shape: pl.BlockSpec(shape, lambda i: (0,) * len(shape))
    return pl.pallas_call(
        functools.partial(_select_body, n_tok=n_tok, cap=cap),
        grid=(1,),
        in_specs=[full((N_EXPERTS, n_tok))],
        out_specs=[full((N_EXPERTS, n_tok)), full((N_EXPERTS, n_tok)), full((N_EXPERTS, LANES))],
        out_shape=[jax.ShapeDtypeStruct((N_EXPERTS, n_tok), F32),
                   jax.ShapeDtypeStruct((N_EXPERTS, n_tok), I32),
                   jax.ShapeDtypeStruct((N_EXPERTS, LANES), I32)],
        compiler_params=_cp(("arbitrary",)), name="select",
    )(aff_t)


def _invert_body(offs_ref, pos_ref, gate_ref, idx_ref, gcol_ref, *, n_tiles, cap):
    n_chunks = cap // SLOT_CHUNK
    jcol = lax.broadcasted_iota(I32, (SLOT_CHUNK, LANES), 0)
    lane = lax.broadcasted_iota(I32, (1, LANES), 1)

    def per_expert(e, _):
        base = e * LANES

        def per_chunk(sc, t_start):
            lo = sc * SLOT_CHUNK
            hi = lo + SLOT_CHUNK
            t0 = lax.while_loop(lambda t: offs_ref[base + t + 1] <= lo, lambda t: t + 1, t_start)

            def cond(st):
                t = st[0]
                return jnp.logical_and(t < n_tiles, offs_ref[base + jnp.minimum(t, n_tiles - 1)] < hi)

            def step(st):
                t, acc_i, acc_g = st
                prow = pos_ref[pl.ds(e, 1), pl.ds(t, 1), :][0]
                grow = gate_ref[pl.ds(e, 1), pl.ds(t, 1), :][0]
                hit = prow == (jcol + lo)
                tok = (lane + t * TOK_TILE).astype(F32)
                acc_i = jnp.maximum(acc_i, jnp.where(hit, tok, -1.0))
                acc_g = jnp.maximum(acc_g, jnp.where(hit, grow, 0.0))
                return t + 1, acc_i, acc_g

            init = (t0, jnp.full((SLOT_CHUNK, LANES), -1.0, F32), jnp.zeros((SLOT_CHUNK, LANES), F32))
            _, acc_i, acc_g = lax.while_loop(cond, step, init)
            r0 = pl.multiple_of(lo, SLOT_CHUNK)
            idx_ref[pl.ds(e, 1), pl.ds(r0, SLOT_CHUNK), :] = (
                jnp.max(acc_i, axis=1, keepdims=True).astype(I32)[None])
            gcol_ref[pl.ds(e, 1), pl.ds(r0, SLOT_CHUNK), :] = jnp.max(acc_g, axis=1, keepdims=True)[None]
            return t0

        lax.fori_loop(0, n_chunks, per_chunk, 0)
        return 0

    lax.fori_loop(0, N_EXPERTS, per_expert, 0)


def _invert_call(offs, pos, gate, cap):
    n_tok = pos.shape[1]
    n_tiles = n_tok // TOK_TILE
    pos3 = pos.reshape(N_EXPERTS, n_tiles, TOK_TILE)
    gate3 = gate.reshape(N_EXPERTS, n_tiles, TOK_TILE)
    full = lambda shape: pl.BlockSpec(shape, lambda i, o: (0,) * len(shape))
    return pl.pallas_call(
        functools.partial(_invert_body, n_tiles=n_tiles, cap=cap),
        grid_spec=pltpu.PrefetchScalarGridSpec(
            num_scalar_prefetch=1, grid=(1,),
            in_specs=[full((N_EXPERTS, n_tiles, TOK_TILE)), full((N_EXPERTS, n_tiles, TOK_TILE))],
            out_specs=[full((N_EXPERTS, cap, 1)), full((N_EXPERTS, cap, 1))]),
        out_shape=[jax.ShapeDtypeStruct((N_EXPERTS, cap, 1), I32),
                   jax.ShapeDtypeStruct((N_EXPERTS, cap, 1), F32)],
        compiler_params=_cp(("arbitrary",)), name="invert",
    )(offs.reshape(-1), pos3, gate3)


def _ffn_body(idx0_ref, idxn_ref, gcol_ref, wg_ref, wu_ref, wd_ref, h0_hbm, h1_hbm, out_ref,
              xe, wgb, wub, wdb, sem, *, cap):
    e = pl.program_id(0)
    p = pl.program_id(1)
    step = e * 2 + p
    last = N_EXPERTS * 2 - 1

    def issue_rows(idx_ref, src, slot):
        def issue(j, _):
            pltpu.make_async_copy(src.at[pl.ds(idx_ref[0, 0, 0, j], 1), :],
                                  xe.at[slot, pl.ds(j, 1), :], sem.at[slot]).start()
            return 0
        lax.fori_loop(0, cap, issue, 0, unroll=8)

    @pl.when(step == 0)
    def _():
        issue_rows(idx0_ref, h0_hbm, 0)

    @pl.when(jnp.logical_and(step < last, p == 0))
    def _():
        issue_rows(idxn_ref, h1_hbm, 1)

    @pl.when(jnp.logical_and(step < last, p == 1))
    def _():
        issue_rows(idxn_ref, h0_hbm, 0)

    @pl.when(p == 0)
    def _():
        wgb[...] = wg_ref[0].astype(BF16)
        wub[...] = wu_ref[0].astype(BF16)
        wdb[...] = wd_ref[0].astype(BF16)

    pltpu.make_async_copy(h0_hbm.at[pl.ds(0, cap), :], xe.at[p], sem.at[p]).wait()

    ch = 256
    for c in range(cap // ch):
        sl = slice(c * ch, (c + 1) * ch)
        xb = xe[p, sl, :].astype(BF16)
        g = jnp.dot(xb, wgb[...], preferred_element_type=F32)
        u = jnp.dot(xb, wub[...], preferred_element_type=F32)
        hid = (g * _sigmoid(g)) * u
        ye = jnp.dot(hid.astype(BF16), wdb[...], preferred_element_type=F32)
        out_ref[0, 0, sl, :] = ye * gcol_ref[0, 0, sl, :]


def _ffn_call(idx, gcol, w_gate, w_up, w_down, h_ctx, h_lat, cap):
    d_e = w_gate.shape[2]
    wspec = lambda a, b: pl.BlockSpec((1, a, b), lambda e, p: (e, 0, 0))
    return pl.pallas_call(
        functools.partial(_ffn_body, cap=cap),
        grid=(N_EXPERTS, 2),
        in_specs=[pl.BlockSpec((1, 1, 1, cap), lambda e, p: (0, 0, 0, 0), memory_space=pltpu.SMEM),
                  pl.BlockSpec((1, 1, 1, cap), lambda e, p: (1 - p, jnp.minimum(e + p, N_EXPERTS - 1), 0, 0),
                               memory_space=pltpu.SMEM),
                  pl.BlockSpec((1, 1, cap, 1), lambda e, p: (p, e, 0, 0)),
                  wspec(D_MODEL, d_e), wspec(D_MODEL, d_e), wspec(d_e, D_MODEL),
                  pl.BlockSpec(memory_space=pl.ANY), pl.BlockSpec(memory_space=pl.ANY)],
        out_specs=pl.BlockSpec((1, 1, cap, D_MODEL), lambda e, p: (p, e, 0, 0)),
        out_shape=jax.ShapeDtypeStruct((2, N_EXPERTS, cap, D_MODEL), F32),
        scratch_shapes=[pltpu.VMEM((2, cap, D_MODEL), F32), pltpu.VMEM((D_MODEL, d_e), BF16),
                        pltpu.VMEM((D_MODEL, d_e), BF16), pltpu.VMEM((d_e, D_MODEL), BF16),
                        pltpu.SemaphoreType.DMA((2,))],
        compiler_params=_cp(("arbitrary", "arbitrary")), name="ffn",
    )(idx, idx, gcol, w_gate, w_up, w_down, h_ctx, h_lat)


def _combine_body(offs_ref, idx_ref, x1_ref, mod_ref, gf_ref, yb_hbm, y_ref, stage, sem, *, cap, path):
    t = pl.program_id(0)
    n_t = pl.num_programs(0)
    slot = t % 2
    wchunk = 16

    def rows_copy(e, j, row, n, s):
        return pltpu.make_async_copy(yb_hbm.at[path, e, pl.ds(j, n), :],
                                     stage.at[s, e, pl.ds(row, n), :], sem.at[s])

    def issue_tile(tt, s):
        stage[s] = jnp.zeros(stage.shape[1:], F32)
        for e in range(N_EXPERTS):
            def issue(j, _, e=e):
                rows_copy(e, j, idx_ref[e * cap + j] - tt * TOK_TILE, 1, s).start()
                return 0
            lax.fori_loop(offs_ref[e * LANES + tt], offs_ref[e * LANES + tt + 1], issue, 0)

    @pl.when(t == 0)
    def _():
        issue_tile(0, 0)

    @pl.when(t + 1 < n_t)
    def _():
        issue_tile(t + 1, 1 - slot)

    total = 0
    for e in range(N_EXPERTS):
        total = total + (offs_ref[e * LANES + t + 1] - offs_ref[e * LANES + t])

    def drain_chunk(_, c):
        rows_copy(0, 0, 0, wchunk, slot).wait()
        return c
    lax.fori_loop(0, total // wchunk, drain_chunk, 0)

    def drain_row(_, c):
        rows_copy(0, 0, 0, 1, slot).wait()
        return c
    lax.fori_loop(0, total % wchunk, drain_row, 0)

    acc = stage[slot, 0]
    for e in range(1, N_EXPERTS):
        acc = acc + stage[slot, e]
    gt2 = mod_ref[0][:, 5 * D_MODEL:6 * D_MODEL]
    x2 = x1_ref[...] + gt2 * acc
    ms = jnp.mean(x2 * x2, axis=-1, keepdims=True)
    y_ref[...] = x2 * lax.rsqrt(ms + EPS) * gf_ref[...]


def _combine_call(offs, idx, x1, mod3, gf, ybuf, mod_row, cap, path):
    n = x1.shape[0]
    return pl.pallas_call(
        functools.partial(_combine_body, cap=cap, path=path),
        grid_spec=pltpu.PrefetchScalarGridSpec(
            num_scalar_prefetch=2, grid=(n // TOK_TILE,),
            in_specs=[pl.BlockSpec((TOK_TILE, D_MODEL), lambda i, o, x: (i, 0)),
                      pl.BlockSpec((1, 1, 6 * D_MODEL), lambda i, o, x: (mod_row(i), 0, 0)),
                      pl.BlockSpec((1, D_MODEL), lambda i, o, x: (0, 0)),
                      pl.BlockSpec(memory_space=pl.ANY)],
            out_specs=pl.BlockSpec((TOK_TILE, D_MODEL), lambda i, o, x: (i, 0)),
            scratch_shapes=[pltpu.VMEM((2, N_EXPERTS, TOK_TILE, D_MODEL), F32),
                            pltpu.SemaphoreType.DMA((2,))]),
        out_shape=jax.ShapeDtypeStruct((n, D_MODEL), F32),
        compiler_params=_cp(("arbitrary",)), name="combine",
    )(offs.reshape(-1), idx.reshape(-1), x1, mod3, gf, ybuf)


def _lru_gate_weights(wa, wx):
    eye = jnp.eye(4, dtype=F32)

    def bd(w):
        w5 = w.reshape(4, 4, LRU_HEAD_DIM, LRU_HEAD_DIM)
        return jnp.einsum('khij,hg->khigj', w5, eye).reshape(4, 256, 256)

    return jnp.concatenate([bd(wa[0]), bd(wa[1]), bd(wx[0]), bd(wx[1])], axis=2).astype(BF16)


def kernel(x_prompt, x_sample, state_lru, state_s5_re, state_s5_im, c, c_ctx, w_mod, b_mod, g_norm1, g_norm2, w_in, conv_w, conv_b, lru_wa, lru_ba, lru_wx, lru_bx, lru_lambda, s5_lambda_re, s5_lambda_im, s5_log_step, s5_b_re, s5_b_im, s5_c_re, s5_c_im, s5_d, s5_w_glu, s5_b_glu, w_lru_out, w_s5_out, w_branch_gate, b_branch_gate, w_o, w_router, b_router, w_e_gate, w_e_up, w_e_down, g_final):
    bp, tp, _ = x_prompt.shape
    bs, ts, _ = x_sample.shape
    n_p, n_s = bp * tp, bs * ts
    l = 0

    c_all = jnp.zeros((16, D_MODEL), F32).at[0].set(c_ctx).at[1:1 + bs].set(c)
    mod3 = _mod_call(c_all, w_mod[l], b_mod[l][None, :]).reshape(16, 1, 6 * D_MODEL)

    w_in_b = w_in[l].astype(BF16)
    w_main, w_s5in = w_in_b[:, :2 * D_LRU], w_in_b[:, 2 * D_LRU:]
    g1 = g_norm1[l][None, :]
    g2 = g_norm2[l][None, :]
    wg = _lru_gate_weights(lru_wa[l], lru_wx[l])
    wts, wc, apow = _s5_weights(s5_lambda_re[l], s5_lambda_im[l], s5_log_step[l],
                                s5_b_re[l], s5_b_im[l], s5_c_re[l], s5_c_im[l])
    wbg = w_branch_gate[l].astype(BF16)
    bbg = b_branch_gate[l][None, :]
    wglu = s5_w_glu[l].astype(BF16)
    wso = w_s5_out[l].astype(BF16)
    wlo = w_lru_out[l].astype(BF16)
    wo = w_o[l].astype(BF16)
    wrt = w_router[l].T
    brt = b_router[l][:, None]

    xp2 = x_prompt.reshape(n_p, D_MODEL)
    xs2 = x_sample.reshape(n_s, D_MODEL)
    ctx_row = lambda i: 0
    tm_in = 512
    lat_row_in = lambda i: 1 + i // (ts // tm_in)

    def s5_state(re, im):
        def part(a, d):
            return a[:, d].reshape(-1, S5_BLK, 8 * S5_N).transpose(1, 0, 2)
        return jnp.concatenate([part(re, 0), part(im, 0), part(re, 1), part(im, 1)], axis=2)

    xb_p, gg_p, us4_p = _inproj_call(xp2, mod3, g1, w_main, w_s5in, ctx_row, tm_in, True)
    ya_p, lru_fin = _lru_call(xb_p, gg_p, conv_w[l], conv_b[l][None, :], wg, lru_ba[l], lru_bx[l],
                              lru_lambda[l], jnp.zeros((bp, 2, D_LRU), F32), bp, tp)
    y4_p, s5_fin = _s5_call(us4_p, wts, wc, apow, jnp.zeros((S5_BLK, bp, S5_SW), F32), bp, tp, False)
    xb_s, gg_s = _inproj_call(xs2, mod3, g1, w_main, None, lat_row_in, tm_in, False)
    us4_s = _inproj_scan_call(x_sample, mod3, g1, w_s5in)
    ya_s, _ = _lru_call(xb_s, gg_s, conv_w[l], conv_b[l][None, :], wg, lru_ba[l], lru_bx[l],
                        lru_lambda[l], state_lru[:, l], bs, ts)
    y4_s, _ = _s5_call(us4_s, wts, wc, apow, s5_state(state_s5_re[:, l], state_s5_im[:, l]), bs, ts, True)

    mo_args = (wbg, bbg)
    mo_tail = (s5_d[l][None, :], wglu, s5_b_glu[l][None, :], wso, wlo, wo, wrt, brt)
    tm_mo = 256
    x1_p, h2_p, aff_p = _mixout_call(xp2, mod3, g1, g2, *mo_args, ya_p, y4_p, us4_p, *mo_tail,
                                     mod_row=ctx_row, permuted=False, tm=tm_mo)
    x1_s, h2_s, aff_s = _mixout_call(xs2, mod3, g1, g2, *mo_args, ya_s, y4_s, us4_s, *mo_tail,
                                     mod_row=lambda i: 1 + i // (ts // tm_mo), permuted=True, tm=tm_mo)

    cap = (CAPACITY_FACTOR * n_p) // N_EXPERTS
    routes = []
    for aff in (aff_p, aff_s):
        gate, pos, offs = _select_call(aff, cap)
        idx, gcol = _invert_call(offs, pos, gate, cap)
        routes.append((offs, idx.reshape(N_EXPERTS, cap), gcol))
    idx_all = jnp.stack([routes[0][1], routes[1][1]])[:, :, None, :]
    gcol_all = jnp.stack([routes[0][2], routes[1][2]])
    ybuf = _ffn_call(idx_all, gcol_all, w_e_gate[l], w_e_up[l], w_e_down[l], h2_p, h2_s, cap)

    gf = g_final[None, :]
    y_p = _combine_call(routes[0][0], routes[0][1], x1_p, mod3, gf, ybuf, ctx_row, cap, 0)
    y_s = _combine_call(routes[1][0], routes[1][1], x1_s, mod3, gf, ybuf,
                        lambda i: 1 + i // (ts // TOK_TILE), cap, 1)

    new_lru = lru_fin[:, None]
    sf = s5_fin.transpose(1, 0, 2).reshape(bp, S5_BLK, 2, 2, 8, S5_N)
    sf = sf.transpose(0, 2, 3, 1, 4, 5).reshape(bp, 2, 2, S5_G, S5_N)
    new_s5r = sf[:, :, 0][:, None]
    new_s5i = sf[:, :, 1][:, None]
    return (y_p.reshape(bp, tp, D_MODEL), y_s.reshape(bs, ts, D_MODEL), new_lru, new_s5r, new_s5i)
```

```python
import functools
import math

import jax
import jax.numpy as jnp
from jax import lax
from jax.experimental import pallas as pl
from jax.experimental.pallas import tpu as pltpu

F32 = jnp.float32
BF16 = jnp.bfloat16
I32 = jnp.int32

D_MODEL = 1024
D_LRU = 1024
LRU_HEADS = 16
LRU_HEAD_DIM = 64
LRU_C = 8.0
CONV_W = 4
D_S5 = 512
S5_P = 16
S5_G = 32
S5_N = 64
GRID_W = 64
N_EXPERTS = 16
CAPACITY_FACTOR = 2
EPS = 1e-6

LANES = 128
SUBLANES = 8
S5_L = 8
S5_BLK = 4
S5_SW = 4 * 512
TOK_TILE = 128
SLOT_CHUNK = 64
COMBINE_WIN = 32
VMEM_LIMIT = 56 * 1024 * 1024


def _cp(sem, vmem=VMEM_LIMIT):
    return pltpu.CompilerParams(dimension_semantics=sem, vmem_limit_bytes=vmem)


def _bdot(a, b):
    return jnp.dot(a.astype(BF16), b.astype(BF16), preferred_element_type=F32)


def _split2(a):
    hi = a.astype(BF16)
    lo = (a - hi.astype(F32)).astype(BF16)
    return hi, lo


def _dot3(a, b, dims=(((1,), (0,)), ((), ()))):
    ah, al = _split2(a)
    bh, bl = _split2(b)
    d = functools.partial(lax.dot_general, dimension_numbers=dims, preferred_element_type=F32)
    return d(ah, bh) + (d(al, bh) + d(ah, bl))


def _sigmoid(x):
    return 1.0 / (1.0 + jnp.exp(-x))


def _gelu(x):
    c = math.sqrt(2.0 / math.pi)
    return 0.5 * x * (1.0 + jnp.tanh(c * (x + 0.044715 * (x * x * x))))


def _norm_mod(x, g, scale, shift):
    ms = jnp.mean(x * x, axis=-1, keepdims=True)
    return (x * lax.rsqrt(ms + EPS) * g) * (1.0 + scale) + shift


def _mod_body(c_ref, w_ref, b_ref, o_ref):
    c = c_ref[...]
    s = c * _sigmoid(c)
    o_ref[...] = _dot3(s, w_ref[...]) + b_ref[...]


def _mod_call(c_all, w_mod, b_mod):
    n = w_mod.shape[1]
    tn = 1536
    return pl.pallas_call(
        _mod_body,
        grid=(n // tn,),
        in_specs=[pl.BlockSpec((16, D_MODEL), lambda j: (0, 0)),
                  pl.BlockSpec((D_MODEL, tn), lambda j: (0, j)),
                  pl.BlockSpec((1, tn), lambda j: (0, j))],
        out_specs=pl.BlockSpec((16, tn), lambda j: (0, j)),
        out_shape=jax.ShapeDtypeStruct((16, n), F32),
        compiler_params=_cp(("arbitrary",)),
        name="mod",
    )(c_all, w_mod, b_mod)


def _inproj_body(x_ref, mod_ref, g_ref, w_ref, ws_ref, xb_ref, gg_ref, us_ref):
    m = mod_ref[0]
    h = _norm_mod(x_ref[...], g_ref[...], m[:, D_MODEL:2 * D_MODEL], m[:, 0:D_MODEL])
    hb = h.astype(BF16)
    z = jnp.dot(hb, w_ref[...], preferred_element_type=F32)
    xb_ref[...] = z[:, :D_LRU]
    gg_ref[...] = _gelu(z[:, D_LRU:]).astype(BF16)
    us = jnp.dot(hb, ws_ref[...], preferred_element_type=F32)
    for k in range(S5_BLK):
        us_ref[k] = us[:, k * LANES:(k + 1) * LANES]


def _inproj_call(x2d, mod3, g1, w_main, w_s5, mod_row, tm):
    n = x2d.shape[0]
    return pl.pallas_call(
        _inproj_body, grid=(n // tm,),
        in_specs=[pl.BlockSpec((tm, D_MODEL), lambda i: (i, 0)),
                  pl.BlockSpec((1, 1, 6 * D_MODEL), lambda i: (mod_row(i), 0, 0)),
                  pl.BlockSpec((1, D_MODEL), lambda i: (0, 0)),
                  pl.BlockSpec((D_MODEL, 2 * D_LRU), lambda i: (0, 0)),
                  pl.BlockSpec((D_MODEL, D_S5), lambda i: (0, 0))],
        out_specs=[pl.BlockSpec((tm, D_LRU), lambda i: (i, 0)),
                   pl.BlockSpec((tm, D_LRU), lambda i: (i, 0)),
                   pl.BlockSpec((S5_BLK, tm, LANES), lambda i: (0, i, 0))],
        out_shape=[jax.ShapeDtypeStruct((n, D_LRU), F32), jax.ShapeDtypeStruct((n, D_LRU), BF16),
                   jax.ShapeDtypeStruct((S5_BLK, n, LANES), F32)],
        compiler_params=_cp(("parallel",)), name="inproj",
    )(x2d, mod3, g1, w_main, w_s5)


def _lru_body(xb_ref, gg_ref, cw_ref, cb_ref, wg_ref, ba_ref, bx_ref, lam_ref, h0_ref,
              ya_ref, fin_ref, xpad, a_f, b_f, a_b, b_b, *, T):
    ch = 256
    nch = T // ch
    zero8 = jnp.zeros((SUBLANES, D_LRU), F32)
    xpad[0:SUBLANES, :] = zero8
    xpad[T + SUBLANES:T + 2 * SUBLANES, :] = zero8
    for c in range(nch):
        xpad[SUBLANES + c * ch:SUBLANES + (c + 1) * ch, :] = xb_ref[0, c * ch:(c + 1) * ch, :]

    nl = -lam_ref[...]
    softplus = jnp.maximum(nl, 0.0) + jnp.log1p(jnp.exp(-jnp.abs(nl)))
    sub = lax.broadcasted_iota(I32, (1, SUBLANES, D_LRU), 1)
    scr = ((a_f, b_f), (a_b, b_b))

    for c in range(nch):
        t0 = c * ch
        xc = cb_ref[...] + sum(
            xpad[t0 + SUBLANES - 2 + k:t0 + SUBLANES - 2 + k + ch, :] * cw_ref[k:k + 1, :]
            for k in range(CONV_W))
        xcb = xc.astype(BF16)
        pre = [jnp.dot(xcb[:, kb * 256:(kb + 1) * 256], wg_ref[kb], preferred_element_type=F32)
               for kb in range(4)]
        for d in range(2):
            ra = jnp.concatenate([p[:, d * 256:(d + 1) * 256] for p in pre], axis=1)
            gx = jnp.concatenate([p[:, 512 + d * 256:512 + (d + 1) * 256] for p in pre], axis=1)
            r = _sigmoid(ra + ba_ref[d:d + 1, :])
            gi = _sigmoid(gx + bx_ref[d:d + 1, :])
            log_a = (-LRU_C) * r * softplus[d:d + 1, :]
            a = jnp.exp(log_a)
            u = jnp.sqrt(1.0 - a * a) * gi * xc
            a3 = a.reshape(ch // SUBLANES, SUBLANES, D_LRU)
            u3 = u.reshape(ch // SUBLANES, SUBLANES, D_LRU)
            for s in (1, 2, 4):
                if d == 0:
                    a_s = pltpu.roll(a3, s, 1)
                    u_s = pltpu.roll(u3, s, 1)
                    keep = sub >= s
                else:
                    a_s = pltpu.roll(a3, SUBLANES - s, 1)
                    u_s = pltpu.roll(u3, SUBLANES - s, 1)
                    keep = sub < SUBLANES - s
                u3 = u3 + a3 * jnp.where(keep, u_s, 0.0)
                a3 = a3 * jnp.where(keep, a_s, 1.0)
            scr[d][0][t0:t0 + ch, :] = a3.reshape(ch, D_LRU)
            scr[d][1][t0:t0 + ch, :] = u3.reshape(ch, D_LRU)

    ntile = T // SUBLANES

    def fwd(i, carry):
        r0 = pl.multiple_of(i * SUBLANES, SUBLANES)
        h = a_f[pl.ds(r0, SUBLANES), :] * carry + b_f[pl.ds(r0, SUBLANES), :]
        b_f[pl.ds(r0, SUBLANES), :] = h
        return h[SUBLANES - 1:SUBLANES, :]

    def bwd(i, carry):
        r0 = pl.multiple_of((ntile - 1 - i) * SUBLANES, SUBLANES)
        h = a_b[pl.ds(r0, SUBLANES), :] * carry + b_b[pl.ds(r0, SUBLANES), :]
        b_b[pl.ds(r0, SUBLANES), :] = h
        return h[0:1, :]

    fin_ref[0, 0:1, :] = lax.fori_loop(0, ntile, fwd, h0_ref[0, 0:1, :])
    fin_ref[0, 1:2, :] = lax.fori_loop(0, ntile, bwd, h0_ref[0, 1:2, :])

    for c in range(nch):
        sl = slice(c * ch, (c + 1) * ch)
        ya_ref[0, sl, :] = (gg_ref[0, sl, :].astype(F32) * (b_f[sl, :] + b_b[sl, :])).astype(BF16)


def _lru_call(xb, gg, conv_w, conv_b, wg, ba, bx, lam, h0, n_seq, T):
    xb3 = xb.reshape(n_seq, T, D_LRU)
    gg3 = gg.reshape(n_seq, T, D_LRU)
    full = lambda shape: pl.BlockSpec(shape, lambda i: (0,) * len(shape))
    seq = lambda shape: pl.BlockSpec(shape, lambda i: (i,) + (0,) * (len(shape) - 1))
    ya, fin = pl.pallas_call(
        functools.partial(_lru_body, T=T),
        grid=(n_seq,),
        in_specs=[seq((1, T, D_LRU)), seq((1, T, D_LRU)), full((CONV_W, D_LRU)), full((1, D_LRU)),
                  full((4, 256, 1024)), full((2, D_LRU)), full((2, D_LRU)), full((2, D_LRU)),
                  seq((1, 2, D_LRU))],
        out_specs=[seq((1, T, D_LRU)), seq((1, 2, D_LRU))],
        out_shape=[jax.ShapeDtypeStruct((n_seq, T, D_LRU), BF16),
                   jax.ShapeDtypeStruct((n_seq, 2, D_LRU), F32)],
        scratch_shapes=[pltpu.VMEM((T + 2 * SUBLANES, D_LRU), F32)] + [pltpu.VMEM((T, D_LRU), F32)] * 4,
        compiler_params=_cp(("parallel",)), name="lru",
    )(xb3, gg3, conv_w, conv_b, wg, ba, bx, lam, h0)
    return ya.reshape(n_seq * T, D_LRU), fin


def _s5_body(u_ref, wts_ref, wc_ref, ap_ref, h0_ref, y_ref, fin_ref, s_scr, hp_scr, slab,
             *, rt, n_c, with_h0, col_major):
    nseq = rt // n_c
    q = 512
    seq_tok = n_c * S5_L
    halves = seq_tok // (GRID_W * S5_L)

    def tok_rows(s, h, i):
        return pl.ds(s * seq_tok + (h * S5_L + i) * GRID_W, GRID_W)

    def chunk_rows(s, h):
        return pl.ds(s * n_c + h, GRID_W, stride=halves)

    if col_major:
        for i in range(S5_L):
            for s in range(nseq):
                for h in range(halves):
                    slab[i, chunk_rows(s, h), :] = u_ref[0, tok_rows(s, h, i), :]
        ub = jnp.concatenate([slab[i] for i in range(S5_L)], axis=1).astype(BF16)
    else:
        ub = jnp.concatenate([u_ref[0, pl.ds(i, rt, stride=S5_L), :] for i in range(S5_L)],
                             axis=1).astype(BF16)
    r1 = jnp.dot(ub, wts_ref[0], preferred_element_type=F32)
    y_in = r1[:, :S5_L * LANES]
    s_scr[...] = r1[:, S5_L * LANES:]
    ap = ap_ref[0]
    if with_h0:
        for s in range(nseq):
            h0 = h0_ref[0, 0, s:s + 1, :]
            r_f = s * n_c
            r_b = s * n_c + n_c - 1
            ar, ai = ap[0:1, 0:q], ap[0:1, q:2 * q]
            s_scr[r_f:r_f + 1, 0:q] += ar * h0[:, 0:q] - ai * h0[:, q:2 * q]
            s_scr[r_f:r_f + 1, q:2 * q] += ar * h0[:, q:2 * q] + ai * h0[:, 0:q]
            br, bi = ap[0:1, 2 * q:3 * q], ap[0:1, 3 * q:4 * q]
            s_scr[r_b:r_b + 1, 2 * q:3 * q] += br * h0[:, 2 * q:3 * q] - bi * h0[:, 3 * q:4 * q]
            s_scr[r_b:r_b + 1, 3 * q:4 * q] += br * h0[:, 3 * q:4 * q] + bi * h0[:, 2 * q:3 * q]

    cidx = lax.broadcasted_iota(I32, (rt, 1), 0) % n_c
    hfr, hfi = s_scr[:, 0:q], s_scr[:, q:2 * q]
    hbr, hbi = s_scr[:, 2 * q:3 * q], s_scr[:, 3 * q:4 * q]
    lvl = 0
    dist = 1
    while dist < n_c:
        ar, ai = ap[lvl:lvl + 1, 0:q], ap[lvl:lvl + 1, q:2 * q]
        br, bi = ap[lvl:lvl + 1, 2 * q:3 * q], ap[lvl:lvl + 1, 3 * q:4 * q]
        kf = cidx >= dist
        kb = cidx < n_c - dist
        sfr = jnp.where(kf, pltpu.roll(hfr, dist, 0), 0.0)
        sfi = jnp.where(kf, pltpu.roll(hfi, dist, 0), 0.0)
        sbr = jnp.where(kb, pltpu.roll(hbr, rt - dist, 0), 0.0)
        sbi = jnp.where(kb, pltpu.roll(hbi, rt - dist, 0), 0.0)
        hfr, hfi = hfr + (ar * sfr - ai * sfi), hfi + (ar * sfi + ai * sfr)
        hbr, hbi = hbr + (br * sbr - bi * sbi), hbi + (br * sbi + bi * sbr)
        lvl += 1
        dist *= 2

    s_scr[:, 0:q] = hfr
    s_scr[:, q:2 * q] = hfi
    s_scr[:, 2 * q:3 * q] = hbr
    s_scr[:, 3 * q:4 * q] = hbi
    for s in range(nseq):
        fin_ref[0, 0, s:s + 1, 0:2 * q] = s_scr[s * n_c + n_c - 1:s * n_c + n_c, 0:2 * q]
        fin_ref[0, 0, s:s + 1, 2 * q:4 * q] = s_scr[s * n_c:s * n_c + 1, 2 * q:4 * q]

    kf = cidx >= 1
    kb = cidx < n_c - 1
    hp_scr[:, 0:q] = jnp.where(kf, pltpu.roll(hfr, 1, 0), 0.0)
    hp_scr[:, q:2 * q] = jnp.where(kf, pltpu.roll(hfi, 1, 0), 0.0)
    hp_scr[:, 2 * q:3 * q] = jnp.where(kb, pltpu.roll(hbr, rt - 1, 0), 0.0)
    hp_scr[:, 3 * q:4 * q] = jnp.where(kb, pltpu.roll(hbi, rt - 1, 0), 0.0)
    if with_h0:
        for s in range(nseq):
            hp_scr[s * n_c:s * n_c + 1, 0:2 * q] = h0_ref[0, 0, s:s + 1, 0:2 * q]
            hp_scr[s * n_c + n_c - 1:s * n_c + n_c, 2 * q:4 * q] = h0_ref[0, 0, s:s + 1, 2 * q:4 * q]
    y = y_in + jnp.dot(hp_scr[...].astype(BF16), wc_ref[0], preferred_element_type=F32)
    for i in range(S5_L):
        y_i = y[:, i * LANES:(i + 1) * LANES]
        if col_major:
            slab[i] = y_i
            for s in range(nseq):
                for h in range(halves):
                    y_ref[0, tok_rows(s, h, i), :] = slab[i, chunk_rows(s, h), :]
        else:
            y_ref[0, pl.ds(i, rt, stride=S5_L), :] = y_i


def _s5_call(us4, wts, wc, apow, h0, n_seq, T, with_h0, col_major):
    n_c = T // S5_L
    rows = n_seq * n_c
    rt = 256
    nseq_t = rt // n_c
    h0 = h0.reshape(S5_BLK, n_seq // nseq_t, nseq_t, S5_SW)
    y4, fin = pl.pallas_call(
        functools.partial(_s5_body, rt=rt, n_c=n_c, with_h0=with_h0, col_major=col_major),
        grid=(S5_BLK, rows // rt),
        in_specs=[pl.BlockSpec((1, rt * S5_L, LANES), lambda b, r: (b, r, 0)),
                  pl.BlockSpec((1, S5_L * LANES, S5_L * LANES + S5_SW), lambda b, r: (b, 0, 0)),
                  pl.BlockSpec((1, S5_SW, S5_L * LANES), lambda b, r: (b, 0, 0)),
                  pl.BlockSpec((1, 8, S5_SW), lambda b, r: (b, 0, 0)),
                  pl.BlockSpec((1, 1, nseq_t, S5_SW), lambda b, r: (b, r, 0, 0))],
        out_specs=[pl.BlockSpec((1, rt * S5_L, LANES), lambda b, r: (b, r, 0)),
                   pl.BlockSpec((1, 1, nseq_t, S5_SW), lambda b, r: (b, r, 0, 0))],
        out_shape=[jax.ShapeDtypeStruct((S5_BLK, n_seq * T, LANES), F32),
                   jax.ShapeDtypeStruct((S5_BLK, n_seq // nseq_t, nseq_t, S5_SW), F32)],
        scratch_shapes=[pltpu.VMEM((rt, S5_SW), F32), pltpu.VMEM((rt, S5_SW), F32),
                        pltpu.VMEM((S5_L, rt, LANES), F32)],
        compiler_params=_cp(("parallel", "parallel")), name="s5",
    )(us4, wts, wc, apow, h0)
    return y4, fin.reshape(S5_BLK, n_seq, S5_SW)


def _s5_weights(lam_re, lam_im, log_step, b_re, b_im, c_re, c_im):
    L, G, N, P = S5_L, S5_G, S5_N, S5_P
    hp = lax.Precision.HIGHEST
    step = jnp.exp(log_step)[..., None]
    lr, li = lam_re, lam_im
    mag = jnp.exp(lr * step)
    ang = li * step
    ar, ai = mag * jnp.cos(ang), mag * jnp.sin(ang)
    den = lr * lr + li * li
    fr = ((ar - 1.0) * lr + ai * li) / den
    fi = (ai * lr - (ar - 1.0) * li) / den
    bbr = fr[..., None] * b_re - fi[..., None] * b_im
    bbi = fr[..., None] * b_im + fi[..., None] * b_re
    pr, pi = [jnp.ones_like(ar)], [jnp.zeros_like(ar)]
    for _ in range(L):
        pr.append(pr[-1] * ar - pi[-1] * ai)
        pi.append(pr[-2] * ai + pi[-1] * ar)
    pr = jnp.stack(pr, axis=2)
    pi = jnp.stack(pi, axis=2)
    er = pr[..., None] * bbr[:, :, None] - pi[..., None] * bbi[:, :, None]
    ei = pr[..., None] * bbi[:, :, None] + pi[..., None] * bbr[:, :, None]
    kt = (jnp.einsum('dgpn,dgknq->dgkpq', c_re, er, precision=hp)
          - jnp.einsum('dgpn,dgknq->dgkpq', c_im, ei, precision=hp))
    ii = jnp.arange(L)
    dif = ii[None, :] - ii[:, None]
    tf = jnp.where((dif >= 0)[None, :, :, None, None], kt[0][:, jnp.clip(dif, 0, L)], 0.0)
    tb = jnp.where((dif <= 0)[None, :, :, None, None], kt[1][:, jnp.clip(-dif, 0, L)], 0.0)
    toe = (tf + tb).reshape(S5_BLK, 8, L, L, P, P)
    eye = jnp.eye(8, dtype=F32)
    wt = jnp.einsum('bgjipq,gh->bjgqihp', toe, eye).reshape(S5_BLK, L * LANES, L * LANES)

    def embed_state(e):
        e6 = e.reshape(S5_BLK, 8, L, N, P)
        return jnp.einsum('bgjnq,gh->bjgqhn', e6, eye).reshape(S5_BLK, L * LANES, 8 * N)

    rev = jnp.arange(L - 1, -1, -1)
    ws = jnp.concatenate([embed_state(er[0][:, rev]), embed_state(ei[0][:, rev]),
                          embed_state(er[1][:, :L]), embed_state(ei[1][:, :L])], axis=2)
    wts = jnp.concatenate([wt, ws], axis=2).astype(BF16)

    def embed_corr(cr, ci, p_r, p_i):
        g_r = cr[:, None] * p_r[:, :, None, :] - ci[:, None] * p_i[:, :, None, :]
        g_i = -(cr[:, None] * p_i[:, :, None, :] + ci[:, None] * p_r[:, :, None, :])
        def emb(gm):
            g6 = gm.reshape(S5_BLK, 8, L, P, N)
            return jnp.einsum('bgipn,gh->bgnihp', g6, eye).reshape(S5_BLK, 8 * N, L * LANES)
        return emb(g_r), emb(g_i)

    kf = jnp.arange(1, L + 1)
    kb = jnp.arange(L, 0, -1)
    cfr, cfi = embed_corr(c_re[0], c_im[0], pr[0][:, kf], pi[0][:, kf])
    cbr, cbi = embed_corr(c_re[1], c_im[1], pr[1][:, kb], pi[1][:, kb])
    wc = jnp.concatenate([cfr, cfi, cbr, cbi], axis=1).astype(BF16)

    qr, qi = pr[:, :, L], pi[:, :, L]
    lv = []
    for _ in range(8):
        lv.append(jnp.concatenate([qr[0].reshape(S5_BLK, 8 * N), qi[0].reshape(S5_BLK, 8 * N),
                                   qr[1].reshape(S5_BLK, 8 * N), qi[1].reshape(S5_BLK, 8 * N)], axis=1))
        qr, qi = qr * qr - qi * qi, 2.0 * qr * qi
    apow = jnp.stack(lv, axis=1)
    return wts, wc, apow


def _mixout_body(x_ref, mod_ref, g1_ref, g2_ref, wbg_ref, bbg_ref, ya_ref, y4_ref, u4_ref,
                 dsk_ref, wglu_ref, bglu_ref, wso_ref, wlo_ref, wo_ref, wrt_ref, brt_ref,
                 x1_ref, h2_ref, aff_ref):
    m = mod_ref[0]
    sh1, sc1, gt1 = m[:, 0:D_MODEL], m[:, D_MODEL:2 * D_MODEL], m[:, 2 * D_MODEL:3 * D_MODEL]
    sh2, sc2 = m[:, 3 * D_MODEL:4 * D_MODEL], m[:, 4 * D_MODEL:5 * D_MODEL]
    x = x_ref[...]
    h = _norm_mod(x, g1_ref[...], sc1, sh1)
    gates = _sigmoid(jnp.dot(h.astype(BF16), wbg_ref[...], preferred_element_type=F32) + bbg_ref[...])

    def assemble(ref):
        return jnp.concatenate([ref[k] for k in range(S5_BLK)], axis=1)

    ys = assemble(y4_ref) + dsk_ref[...] * assemble(u4_ref)
    v = _gelu(ys)
    ob = v * _sigmoid(jnp.dot(v.astype(BF16), wglu_ref[...], preferred_element_type=F32) + bglu_ref[...])
    merged = (gates[:, :D_MODEL] * jnp.dot(ya_ref[...], wlo_ref[...], preferred_element_type=F32)
              + gates[:, D_MODEL:] * jnp.dot(ob.astype(BF16), wso_ref[...], preferred_element_type=F32))
    mix = jnp.dot(merged.astype(BF16), wo_ref[...], preferred_element_type=F32)
    x1 = x + gt1 * mix
    x1_ref[...] = x1
    h2 = _norm_mod(x1, g2_ref[...], sc2, sh2)
    h2_ref[...] = h2
    logits = _dot3(wrt_ref[...], h2, dims=(((1,), (1,)), ((), ()))) + brt_ref[...]
    mx = jnp.max(logits, axis=0, keepdims=True)
    ex = jnp.exp(logits - mx)
    aff_ref[...] = ex / jnp.sum(ex, axis=0, keepdims=True)


def _mixout_call(x2d, mod3, g1, g2, wbg, bbg, ya, y4, u4, dsk, wglu, bglu, wso, wlo, wo, wrt, brt,
                 mod_row, tm):
    n = x2d.shape[0]
    full = lambda shape: pl.BlockSpec(shape, lambda i: (0,) * len(shape))
    s5_spec = pl.BlockSpec((S5_BLK, tm, LANES), lambda i: (0, i, 0))
    tok = lambda w: pl.BlockSpec((tm, w), lambda i: (i, 0))
    return pl.pallas_call(
        _mixout_body,
        grid=(n // tm,),
        in_specs=[tok(D_MODEL), pl.BlockSpec((1, 1, 6 * D_MODEL), lambda i: (mod_row(i), 0, 0)),
                  full((1, D_MODEL)), full((1, D_MODEL)), full((D_MODEL, 2 * D_MODEL)),
                  full((1, 2 * D_MODEL)), tok(D_LRU), s5_spec, s5_spec, full((1, D_S5)),
                  full((D_S5, D_S5)), full((1, D_S5)), full((D_S5, D_MODEL)), full((D_LRU, D_MODEL)),
                  full((D_MODEL, D_MODEL)), full((N_EXPERTS, D_MODEL)), full((N_EXPERTS, 1))],
        out_specs=[tok(D_MODEL), tok(D_MODEL), pl.BlockSpec((N_EXPERTS, tm), lambda i: (0, i))],
        out_shape=[jax.ShapeDtypeStruct((n, D_MODEL), F32), jax.ShapeDtypeStruct((n, D_MODEL), F32),
                   jax.ShapeDtypeStruct((N_EXPERTS, n), F32)],
        compiler_params=_cp(("parallel",)), name="mixout",
    )(x2d, mod3, g1, g2, wbg, bbg, ya, y4, u4, dsk, wglu, bglu, wso, wlo, wo, wrt, brt)


def _select_body(aff_ref, gate_ref, pos_ref, offs_ref, *, n_tok, cap):
    aff = aff_ref[...]
    capf = float(cap)

    def bis(_, lh):
        lo, hi = lh
        mid = lo + ((hi - lo + 1) >> 1)
        cnt = jnp.sum(jnp.where(aff >= pltpu.bitcast(mid, F32), 1.0, 0.0), axis=1, keepdims=True)
        ok = cnt >= capf
        return jnp.where(ok, mid, lo), jnp.where(ok, hi, mid - 1)

    lo0 = jnp.zeros((N_EXPERTS, 1), I32)
    hi0 = jnp.full((N_EXPERTS, 1), 0x7F800000, I32)
    thr_bits, _ = lax.fori_loop(0, 31, bis, (lo0, hi0))
    thr = pltpu.bitcast(thr_bits, F32)
    need = capf - jnp.sum(jnp.where(aff > thr, 1.0, 0.0), axis=1, keepdims=True)

    r = lax.broadcasted_iota(I32, (TOK_TILE, TOK_TILE), 0)
    c = lax.broadcasted_iota(I32, (TOK_TILE, TOK_TILE), 1)
    tri = jnp.where(r < c, 1.0, 0.0).astype(BF16)
    lane = lax.broadcasted_iota(I32, (N_EXPERTS, LANES), 1)
    n_tiles = n_tok // TOK_TILE
    run_eq = jnp.zeros((N_EXPERTS, 1), F32)
    run_sel = jnp.zeros((N_EXPERTS, 1), F32)
    offs = jnp.zeros((N_EXPERTS, LANES), F32)
    for t in range(n_tiles):
        sl = slice(t * TOK_TILE, (t + 1) * TOK_TILE)
        aff_t = aff_ref[:, sl]
        is_eq = aff_t == thr
        eq_t = jnp.where(is_eq, 1.0, 0.0)
        rank_eq = jnp.dot(eq_t.astype(BF16), tri, preferred_element_type=F32) + run_eq
        cand = (aff_t > thr) | (is_eq & (rank_eq < need))
        pos_t = jnp.dot(jnp.where(cand, 1.0, 0.0).astype(BF16), tri, preferred_element_type=F32) + run_sel
        sel_t = cand & (pos_t < capf)
        sel_f = jnp.where(sel_t, 1.0, 0.0)
        offs = jnp.where(lane == t, run_sel, offs)
        gate_ref[:, sl] = jnp.where(sel_t, aff_t, 0.0)
        pos_ref[:, sl] = jnp.where(sel_t, pos_t, -1.0).astype(I32)
        run_eq = run_eq + jnp.sum(eq_t, axis=1, keepdims=True)
        run_sel = run_sel + jnp.sum(sel_f, axis=1, keepdims=True)
    offs = jnp.where(lane >= n_tiles, run_sel, offs)
    offs_ref[...] = offs.astype(I32)


def _select_call(aff_t, cap):
    n_tok = aff_t.shape[1]
    full = lambda shape: pl.BlockSpec(shape, lambda i: (0,) * len(shape))
    return pl.pallas_call(
        functools.partial(_select_body, n_tok=n_tok, cap=cap),
        grid=(1,),
        in_specs=[full((N_EXPERTS, n_tok))],
        out_specs=[full((N_EXPERTS, n_tok)), full((N_EXPERTS, n_tok)), full((N_EXPERTS, LANES))],
        out_shape=[jax.ShapeDtypeStruct((N_EXPERTS, n_tok), F32),
                   jax.ShapeDtypeStruct((N_EXPERTS, n_tok), I32),
                   jax.ShapeDtypeStruct((N_EXPERTS, LANES), I32)],
        compiler_params=_cp(("arbitrary",)), name="select",
    )(aff_t)


def _invert_body(offs_ref, pos_ref, gate_ref, idx_ref, gcol_ref, *, n_tiles, cap):
    n_chunks = cap // SLOT_CHUNK
    jcol = lax.broadcasted_iota(I32, (SLOT_CHUNK, LANES), 0)
    lane = lax.broadcasted_iota(I32, (1, LANES), 1)

    def per_expert(e, _):
        base = e * LANES

        def per_chunk(sc, t_start):
            lo = sc * SLOT_CHUNK
            hi = lo + SLOT_CHUNK
            t0 = lax.while_loop(lambda t: offs_ref[base + t + 1] <= lo, lambda t: t + 1, t_start)

            def cond(st):
                t = st[0]
                return jnp.logical_and(t < n_tiles, offs_ref[base + jnp.minimum(t, n_tiles - 1)] < hi)

            def step(st):
                t, acc_i, acc_g = st
                prow = pos_ref[pl.ds(e, 1), pl.ds(t, 1), :][0]
                grow = gate_ref[pl.ds(e, 1), pl.ds(t, 1), :][0]
                hit = prow == (jcol + lo)
                tok = (lane + t * TOK_TILE).astype(F32)
                acc_i = jnp.maximum(acc_i, jnp.where(hit, tok, -1.0))
                acc_g = jnp.maximum(acc_g, jnp.where(hit, grow, 0.0))
                return t + 1, acc_i, acc_g

            init = (t0, jnp.full((SLOT_CHUNK, LANES), -1.0, F32), jnp.zeros((SLOT_CHUNK, LANES), F32))
            _, acc_i, acc_g = lax.while_loop(cond, step, init)
            r0 = pl.multiple_of(lo, SLOT_CHUNK)
            idx_ref[pl.ds(e, 1), pl.ds(r0, SLOT_CHUNK), :] = (
                jnp.max(acc_i, axis=1, keepdims=True).astype(I32)[None])
            gcol_ref[pl.ds(e, 1), pl.ds(r0, SLOT_CHUNK), :] = jnp.max(acc_g, axis=1, keepdims=True)[None]
            return t0

        lax.fori_loop(0, n_chunks, per_chunk, 0)
        return 0

    lax.fori_loop(0, N_EXPERTS, per_expert, 0)


def _invert_call(offs, pos, gate, cap):
    n_tok = pos.shape[1]
    n_tiles = n_tok // TOK_TILE
    pos3 = pos.reshape(N_EXPERTS, n_tiles, TOK_TILE)
    gate3 = gate.reshape(N_EXPERTS, n_tiles, TOK_TILE)
    full = lambda shape: pl.BlockSpec(shape, lambda i, o: (0,) * len(shape))
    return pl.pallas_call(
        functools.partial(_invert_body, n_tiles=n_tiles, cap=cap),
        grid_spec=pltpu.PrefetchScalarGridSpec(
            num_scalar_prefetch=1, grid=(1,),
            in_specs=[full((N_EXPERTS, n_tiles, TOK_TILE)), full((N_EXPERTS, n_tiles, TOK_TILE))],
            out_specs=[full((N_EXPERTS, cap, 1)), full((N_EXPERTS, cap, 1))]),
        out_shape=[jax.ShapeDtypeStruct((N_EXPERTS, cap, 1), I32),
                   jax.ShapeDtypeStruct((N_EXPERTS, cap, 1), F32)],
        compiler_params=_cp(("arbitrary",)), name="invert",
    )(offs.reshape(-1), pos3, gate3)


def _ffn_body(idx0_ref, idxn_ref, gcol_ref, wg_ref, wu_ref, wd_ref, h0_hbm, h1_hbm, out_ref,
              xe, wgb, wub, wdb, sem, *, cap):
    e = pl.program_id(0)
    p = pl.program_id(1)
    step = e * 2 + p
    last = N_EXPERTS * 2 - 1

    def issue_rows(idx_ref, src, slot):
        def issue(j, _):
            pltpu.make_async_copy(src.at[pl.ds(idx_ref[0, 0, 0, j], 1), :],
                                  xe.at[slot, pl.ds(j, 1), :], sem.at[slot]).start()
            return 0
        lax.fori_loop(0, cap, issue, 0, unroll=8)

    @pl.when(step == 0)
    def _():
        issue_rows(idx0_ref, h0_hbm, 0)

    @pl.when(jnp.logical_and(step < last, p == 0))
    def _():
        issue_rows(idxn_ref, h1_hbm, 1)

    @pl.when(jnp.logical_and(step < last, p == 1))
    def _():
        issue_rows(idxn_ref, h0_hbm, 0)

    @pl.when(p == 0)
    def _():
        wgb[...] = wg_ref[0].astype(BF16)
        wub[...] = wu_ref[0].astype(BF16)
        wdb[...] = wd_ref[0].astype(BF16)

    pltpu.make_async_copy(h0_hbm.at[pl.ds(0, cap), :], xe.at[p], sem.at[p]).wait()

    ch = 256
    for c in range(cap // ch):
        sl = slice(c * ch, (c + 1) * ch)
        xb = xe[p, sl, :].astype(BF16)
        g = jnp.dot(xb, wgb[...], preferred_element_type=F32)
        u = jnp.dot(xb, wub[...], preferred_element_type=F32)
        hid = (g * _sigmoid(g)) * u
        ye = jnp.dot(hid.astype(BF16), wdb[...], preferred_element_type=F32)
        out_ref[0, 0, sl, :] = ye * gcol_ref[0, 0, sl, :]


def _ffn_call(idx, gcol, w_gate, w_up, w_down, h_ctx, h_lat, cap):
    d_e = w_gate.shape[2]
    wspec = lambda a, b: pl.BlockSpec((1, a, b), lambda e, p: (e, 0, 0))
    return pl.pallas_call(
        functools.partial(_ffn_body, cap=cap),
        grid=(N_EXPERTS, 2),
        in_specs=[pl.BlockSpec((1, 1, 1, cap), lambda e, p: (0, 0, 0, 0), memory_space=pltpu.SMEM),
                  pl.BlockSpec((1, 1, 1, cap), lambda e, p: (1 - p, jnp.minimum(e + p, N_EXPERTS - 1), 0, 0),
                               memory_space=pltpu.SMEM),
                  pl.BlockSpec((1, 1, cap, 1), lambda e, p: (p, e, 0, 0)),
                  wspec(D_MODEL, d_e), wspec(D_MODEL, d_e), wspec(d_e, D_MODEL),
                  pl.BlockSpec(memory_space=pl.ANY), pl.BlockSpec(memory_space=pl.ANY)],
        out_specs=pl.BlockSpec((1, 1, cap, D_MODEL), lambda e, p: (p, e, 0, 0)),
        out_shape=jax.ShapeDtypeStruct((2, N_EXPERTS, cap, D_MODEL), F32),
        scratch_shapes=[pltpu.VMEM((2, cap, D_MODEL), F32), pltpu.VMEM((D_MODEL, d_e), BF16),
                        pltpu.VMEM((D_MODEL, d_e), BF16), pltpu.VMEM((d_e, D_MODEL), BF16),
                        pltpu.SemaphoreType.DMA((2,))],
        compiler_params=_cp(("arbitrary", "arbitrary")), name="ffn",
    )(idx, idx, gcol, w_gate, w_up, w_down, h_ctx, h_lat)


def _combine_body(offs_ref, x1_ref, mod_ref, gf_ref, pos_ref, yb_hbm, y_ref, wins, sem, *, cap, path):
    t = pl.program_id(0)
    n_t = pl.num_programs(0)
    slot = t % 2
    fetch = COMBINE_WIN + SUBLANES
    sub = lax.broadcasted_iota(I32, (fetch, TOK_TILE), 0)

    def lo_of(tt, e, w):
        return offs_ref[e * LANES + tt] + w * COMBINE_WIN

    def start_of(lo):
        return pl.multiple_of(jnp.minimum((lo // SUBLANES) * SUBLANES, cap - fetch), SUBLANES)

    def issue(tt, w, s):
        for e in range(N_EXPERTS):
            pltpu.make_async_copy(yb_hbm.at[path, e, pl.ds(start_of(lo_of(tt, e, w)), fetch), :],
                                  wins.at[s, pl.ds(e * fetch, fetch), :], sem.at[s]).start()

    def drain(s):
        pltpu.make_async_copy(yb_hbm.at[path, 0, pl.ds(0, N_EXPERTS * fetch), :],
                              wins.at[s], sem.at[s]).wait()

    def expand(w, s):
        rows = []
        for e in range(N_EXPERTS):
            lo = lo_of(t, e, w)
            pos = pos_ref[e:e + 1, :]
            in_round = jnp.logical_and(pos >= lo, pos < lo + COMBINE_WIN)
            hit = jnp.logical_and(pos - start_of(lo) == sub, in_round)
            rows.append(jnp.where(hit, 1.0, 0.0))
        onehot = jnp.concatenate(rows, axis=0).T.astype(BF16)
        win = wins[s]
        hi = win.astype(BF16)
        lo_part = (win - hi.astype(F32)).astype(BF16)
        return (jnp.dot(onehot, hi, preferred_element_type=F32)
                + jnp.dot(onehot, lo_part, preferred_element_type=F32))

    @pl.when(t == 0)
    def _():
        issue(0, 0, 0)

    @pl.when(t + 1 < n_t)
    def _():
        issue(t + 1, 0, 1 - slot)

    drain(slot)
    acc = expand(0, slot)

    kmax = 0
    for e in range(N_EXPERTS):
        kmax = jnp.maximum(kmax, offs_ref[e * LANES + t + 1] - offs_ref[e * LANES + t])

    def more(w, a):
        issue(t, w, slot)
        drain(slot)
        return a + expand(w, slot)
    acc = lax.fori_loop(1, (kmax + COMBINE_WIN - 1) // COMBINE_WIN, more, acc)

    gt2 = mod_ref[0][:, 5 * D_MODEL:6 * D_MODEL]
    x2 = x1_ref[...] + gt2 * acc
    ms = jnp.mean(x2 * x2, axis=-1, keepdims=True)
    y_ref[...] = x2 * lax.rsqrt(ms + EPS) * gf_ref[...]


def _combine_call(offs, pos, x1, mod3, gf, ybuf, mod_row, cap, path):
    n = x1.shape[0]
    return pl.pallas_call(
        functools.partial(_combine_body, cap=cap, path=path),
        grid_spec=pltpu.PrefetchScalarGridSpec(
            num_scalar_prefetch=1, grid=(n // TOK_TILE,),
            in_specs=[pl.BlockSpec((TOK_TILE, D_MODEL), lambda i, o: (i, 0)),
                      pl.BlockSpec((1, 1, 6 * D_MODEL), lambda i, o: (mod_row(i), 0, 0)),
                      pl.BlockSpec((1, D_MODEL), lambda i, o: (0, 0)),
                      pl.BlockSpec((N_EXPERTS, TOK_TILE), lambda i, o: (0, i)),
                      pl.BlockSpec(memory_space=pl.ANY)],
            out_specs=pl.BlockSpec((TOK_TILE, D_MODEL), lambda i, o: (i, 0)),
            scratch_shapes=[pltpu.VMEM((2, N_EXPERTS * (COMBINE_WIN + SUBLANES), D_MODEL), F32),
                            pltpu.SemaphoreType.DMA((2,))]),
        out_shape=jax.ShapeDtypeStruct((n, D_MODEL), F32),
        compiler_params=_cp(("arbitrary",)), name="combine",
    )(offs.reshape(-1), x1, mod3, gf, pos, ybuf)


def _lru_gate_weights(wa, wx):
    eye = jnp.eye(4, dtype=F32)

    def bd(w):
        w5 = w.reshape(4, 4, LRU_HEAD_DIM, LRU_HEAD_DIM)
        return jnp.einsum('khij,hg->khigj', w5, eye).reshape(4, 256, 256)

    return jnp.concatenate([bd(wa[0]), bd(wa[1]), bd(wx[0]), bd(wx[1])], axis=2).astype(BF16)


def kernel(x_prompt, x_sample, state_lru, state_s5_re, state_s5_im, c, c_ctx, w_mod, b_mod, g_norm1, g_norm2, w_in, conv_w, conv_b, lru_wa, lru_ba, lru_wx, lru_bx, lru_lambda, s5_lambda_re, s5_lambda_im, s5_log_step, s5_b_re, s5_b_im, s5_c_re, s5_c_im, s5_d, s5_w_glu, s5_b_glu, w_lru_out, w_s5_out, w_branch_gate, b_branch_gate, w_o, w_router, b_router, w_e_gate, w_e_up, w_e_down, g_final):
    bp, tp, _ = x_prompt.shape
    bs, ts, _ = x_sample.shape
    n_p, n_s = bp * tp, bs * ts
    l = 0

    c_all = jnp.zeros((16, D_MODEL), F32).at[0].set(c_ctx).at[1:1 + bs].set(c)
    mod3 = _mod_call(c_all, w_mod[l], b_mod[l][None, :]).reshape(16, 1, 6 * D_MODEL)

    w_in_b = w_in[l].astype(BF16)
    w_main, w_s5in = w_in_b[:, :2 * D_LRU], w_in_b[:, 2 * D_LRU:]
    g1 = g_norm1[l][None, :]
    g2 = g_norm2[l][None, :]
    wg = _lru_gate_weights(lru_wa[l], lru_wx[l])
    wts, wc, apow = _s5_weights(s5_lambda_re[l], s5_lambda_im[l], s5_log_step[l],
                                s5_b_re[l], s5_b_im[l], s5_c_re[l], s5_c_im[l])
    wbg = w_branch_gate[l].astype(BF16)
    bbg = b_branch_gate[l][None, :]
    wglu = s5_w_glu[l].astype(BF16)
    wso = w_s5_out[l].astype(BF16)
    wlo = w_lru_out[l].astype(BF16)
    wo = w_o[l].astype(BF16)
    wrt = w_router[l].T
    brt = b_router[l][:, None]

    xp2 = x_prompt.reshape(n_p, D_MODEL)
    xs2 = x_sample.reshape(n_s, D_MODEL)
    ctx_row = lambda i: 0
    tm_in = 512
    lat_row_in = lambda i: 1 + i // (ts // tm_in)

    def s5_state(re, im):
        def part(a, d):
            return a[:, d].reshape(-1, S5_BLK, 8 * S5_N).transpose(1, 0, 2)
        return jnp.concatenate([part(re, 0), part(im, 0), part(re, 1), part(im, 1)], axis=2)

    xb_p, gg_p, us4_p = _inproj_call(xp2, mod3, g1, w_main, w_s5in, ctx_row, tm_in)
    ya_p, lru_fin = _lru_call(xb_p, gg_p, conv_w[l], conv_b[l][None, :], wg, lru_ba[l], lru_bx[l],
                              lru_lambda[l], jnp.zeros((bp, 2, D_LRU), F32), bp, tp)
    y4_p, s5_fin = _s5_call(us4_p, wts, wc, apow, jnp.zeros((S5_BLK, bp, S5_SW), F32), bp, tp,
                            with_h0=False, col_major=False)
    xb_s, gg_s, us4_s = _inproj_call(xs2, mod3, g1, w_main, w_s5in, lat_row_in, tm_in)
    ya_s, _ = _lru_call(xb_s, gg_s, conv_w[l], conv_b[l][None, :], wg, lru_ba[l], lru_bx[l],
                        lru_lambda[l], state_lru[:, l], bs, ts)
    y4_s, _ = _s5_call(us4_s, wts, wc, apow, s5_state(state_s5_re[:, l], state_s5_im[:, l]), bs, ts,
                       with_h0=True, col_major=True)

    mo_args = (wbg, bbg)
    mo_tail = (s5_d[l][None, :], wglu, s5_b_glu[l][None, :], wso, wlo, wo, wrt, brt)
    tm_mo = 256
    x1_p, h2_p, aff_p = _mixout_call(xp2, mod3, g1, g2, *mo_args, ya_p, y4_p, us4_p, *mo_tail,
                                     mod_row=ctx_row, tm=tm_mo)
    x1_s, h2_s, aff_s = _mixout_call(xs2, mod3, g1, g2, *mo_args, ya_s, y4_s, us4_s, *mo_tail,
                                     mod_row=lambda i: 1 + i // (ts // tm_mo), tm=tm_mo)

    cap = (CAPACITY_FACTOR * n_p) // N_EXPERTS
    routes = []
    for aff in (aff_p, aff_s):
        gate, pos, offs = _select_call(aff, cap)
        idx, gcol = _invert_call(offs, pos, gate, cap)
        routes.append((offs, idx.reshape(N_EXPERTS, cap), gcol, pos))
    idx_all = jnp.stack([routes[0][1], routes[1][1]])[:, :, None, :]
    gcol_all = jnp.stack([routes[0][2], routes[1][2]])
    ybuf = _ffn_call(idx_all, gcol_all, w_e_gate[l], w_e_up[l], w_e_down[l], h2_p, h2_s, cap)

    gf = g_final[None, :]
    y_p = _combine_call(routes[0][0], routes[0][3], x1_p, mod3, gf, ybuf, ctx_row, cap, 0)
    y_s = _combine_call(routes[1][0], routes[1][3], x1_s, mod3, gf, ybuf,
                        lambda i: 1 + i // (ts // TOK_TILE), cap, 1)

    new_lru = lru_fin[:, None]
    sf = s5_fin.transpose(1, 0, 2).reshape(bp, S5_BLK, 2, 2, 8, S5_N)
    sf = sf.transpose(0, 2, 3, 1, 4, 5).reshape(bp, 2, 2, S5_G, S5_N)
    new_s5r = sf[:, :, 0][:, None]
    new_s5i = sf[:, :, 1][:, None]
    return (y_p.reshape(bp, tp, D_MODEL), y_s.reshape(bs, ts, D_MODEL), new_lru, new_s5r, new_s5i)
```

```python
import functools
import math

import jax
import jax.numpy as jnp
from jax import lax
from jax.experimental import pallas as pl
from jax.experimental.pallas import tpu as pltpu

F32 = jnp.float32
BF16 = jnp.bfloat16
I32 = jnp.int32

D_MODEL = 1024
D_LRU = 1024
LRU_HEADS = 16
LRU_HEAD_DIM = 64
LRU_C = 8.0
CONV_W = 4
D_S5 = 512
S5_P = 16
S5_G = 32
S5_N = 64
GRID_W = 64
N_EXPERTS = 16
CAPACITY_FACTOR = 2
EPS = 1e-6

LANES = 128
SUBLANES = 8
S5_L = 8
S5_BLK = 4
S5_SW = 4 * 512
TOK_TILE = 128
SLOT_CHUNK = 64
COMBINE_WIN = 32
VMEM_LIMIT = 56 * 1024 * 1024


def _cp(sem, vmem=VMEM_LIMIT):
    return pltpu.CompilerParams(dimension_semantics=sem, vmem_limit_bytes=vmem)


def _bdot(a, b):
    return jnp.dot(a.astype(BF16), b.astype(BF16), preferred_element_type=F32)


def _split2(a):
    hi = a.astype(BF16)
    lo = (a - hi.astype(F32)).astype(BF16)
    return hi, lo


def _dot3(a, b, dims=(((1,), (0,)), ((), ()))):
    ah, al = _split2(a)
    bh, bl = _split2(b)
    d = functools.partial(lax.dot_general, dimension_numbers=dims, preferred_element_type=F32)
    return d(ah, bh) + (d(al, bh) + d(ah, bl))


def _sigmoid(x):
    return 1.0 / (1.0 + jnp.exp(-x))


def _gelu(x):
    c = math.sqrt(2.0 / math.pi)
    return 0.5 * x * (1.0 + jnp.tanh(c * (x + 0.044715 * (x * x * x))))


def _norm_mod(x, g, scale, shift):
    ms = jnp.mean(x * x, axis=-1, keepdims=True)
    return (x * lax.rsqrt(ms + EPS) * g) * (1.0 + scale) + shift


def _mod_body(c_ref, w_ref, b_ref, o_ref):
    c = c_ref[...]
    s = c * _sigmoid(c)
    o_ref[...] = _dot3(s, w_ref[...]) + b_ref[...]


def _mod_call(c_all, w_mod, b_mod):
    n = w_mod.shape[1]
    tn = 1536
    return pl.pallas_call(
        _mod_body,
        grid=(n // tn,),
        in_specs=[pl.BlockSpec((16, D_MODEL), lambda j: (0, 0)),
                  pl.BlockSpec((D_MODEL, tn), lambda j: (0, j)),
                  pl.BlockSpec((1, tn), lambda j: (0, j))],
        out_specs=pl.BlockSpec((16, tn), lambda j: (0, j)),
        out_shape=jax.ShapeDtypeStruct((16, n), F32),
        compiler_params=_cp(("arbitrary",)),
        name="mod",
    )(c_all, w_mod, b_mod)


def _inproj_body(x_ref, mod_ref, g_ref, w_ref, ws_ref, xb_ref, gg_ref, us_ref):
    m = mod_ref[0]
    h = _norm_mod(x_ref[...], g_ref[...], m[:, D_MODEL:2 * D_MODEL], m[:, 0:D_MODEL])
    hb = h.astype(BF16)
    z = jnp.dot(hb, w_ref[...], preferred_element_type=F32)
    xb_ref[...] = z[:, :D_LRU]
    gg_ref[...] = _gelu(z[:, D_LRU:]).astype(BF16)
    us = jnp.dot(hb, ws_ref[...], preferred_element_type=F32)
    for k in range(S5_BLK):
        us_ref[k] = us[:, k * LANES:(k + 1) * LANES]


def _inproj_call(x2d, mod3, g1, w_main, w_s5, mod_row, tm):
    n = x2d.shape[0]
    return pl.pallas_call(
        _inproj_body, grid=(n // tm,),
        in_specs=[pl.BlockSpec((tm, D_MODEL), lambda i: (i, 0)),
                  pl.BlockSpec((1, 1, 6 * D_MODEL), lambda i: (mod_row(i), 0, 0)),
                  pl.BlockSpec((1, D_MODEL), lambda i: (0, 0)),
                  pl.BlockSpec((D_MODEL, 2 * D_LRU), lambda i: (0, 0)),
                  pl.BlockSpec((D_MODEL, D_S5), lambda i: (0, 0))],
        out_specs=[pl.BlockSpec((tm, D_LRU), lambda i: (i, 0)),
                   pl.BlockSpec((tm, D_LRU), lambda i: (i, 0)),
                   pl.BlockSpec((S5_BLK, tm, LANES), lambda i: (0, i, 0))],
        out_shape=[jax.ShapeDtypeStruct((n, D_LRU), F32), jax.ShapeDtypeStruct((n, D_LRU), BF16),
                   jax.ShapeDtypeStruct((S5_BLK, n, LANES), F32)],
        compiler_params=_cp(("parallel",)), name="inproj",
    )(x2d, mod3, g1, w_main, w_s5)


def _lru_body(xb_ref, gg_ref, cw_ref, cb_ref, wg_ref, ba_ref, bx_ref, lam_ref, h0_ref,
              ya_ref, fin_ref, xpad, a_f, b_f, a_b, b_b, *, T):
    ch = 256
    nch = T // ch
    zero8 = jnp.zeros((SUBLANES, D_LRU), F32)
    xpad[0:SUBLANES, :] = zero8
    xpad[T + SUBLANES:T + 2 * SUBLANES, :] = zero8
    for c in range(nch):
        xpad[SUBLANES + c * ch:SUBLANES + (c + 1) * ch, :] = xb_ref[0, c * ch:(c + 1) * ch, :]

    nl = -lam_ref[...]
    softplus = jnp.maximum(nl, 0.0) + jnp.log1p(jnp.exp(-jnp.abs(nl)))
    sub = lax.broadcasted_iota(I32, (1, SUBLANES, D_LRU), 1)
    scr = ((a_f, b_f), (a_b, b_b))

    for c in range(nch):
        t0 = c * ch
        xc = cb_ref[...] + sum(
            xpad[t0 + SUBLANES - 2 + k:t0 + SUBLANES - 2 + k + ch, :] * cw_ref[k:k + 1, :]
            for k in range(CONV_W))
        xcb = xc.astype(BF16)
        pre = [jnp.dot(xcb[:, kb * 256:(kb + 1) * 256], wg_ref[kb], preferred_element_type=F32)
               for kb in range(4)]
        for d in range(2):
            ra = jnp.concatenate([p[:, d * 256:(d + 1) * 256] for p in pre], axis=1)
            gx = jnp.concatenate([p[:, 512 + d * 256:512 + (d + 1) * 256] for p in pre], axis=1)
            r = _sigmoid(ra + ba_ref[d:d + 1, :])
            gi = _sigmoid(gx + bx_ref[d:d + 1, :])
            log_a = (-LRU_C) * r * softplus[d:d + 1, :]
            a = jnp.exp(log_a)
            u = jnp.sqrt(1.0 - a * a) * gi * xc
            a3 = a.reshape(ch // SUBLANES, SUBLANES, D_LRU)
            u3 = u.reshape(ch // SUBLANES, SUBLANES, D_LRU)
            for s in (1, 2, 4):
                if d == 0:
                    a_s = pltpu.roll(a3, s, 1)
                    u_s = pltpu.roll(u3, s, 1)
                    keep = sub >= s
                else:
                    a_s = pltpu.roll(a3, SUBLANES - s, 1)
                    u_s = pltpu.roll(u3, SUBLANES - s, 1)
                    keep = sub < SUBLANES - s
                u3 = u3 + a3 * jnp.where(keep, u_s, 0.0)
                a3 = a3 * jnp.where(keep, a_s, 1.0)
            scr[d][0][t0:t0 + ch, :] = a3.reshape(ch, D_LRU)
            scr[d][1][t0:t0 + ch, :] = u3.reshape(ch, D_LRU)

    ntile = T // SUBLANES

    def fwd(i, carry):
        r0 = pl.multiple_of(i * SUBLANES, SUBLANES)
        h = a_f[pl.ds(r0, SUBLANES), :] * carry + b_f[pl.ds(r0, SUBLANES), :]
        b_f[pl.ds(r0, SUBLANES), :] = h
        return h[SUBLANES - 1:SUBLANES, :]

    def bwd(i, carry):
        r0 = pl.multiple_of((ntile - 1 - i) * SUBLANES, SUBLANES)
        h = a_b[pl.ds(r0, SUBLANES), :] * carry + b_b[pl.ds(r0, SUBLANES), :]
        b_b[pl.ds(r0, SUBLANES), :] = h
        return h[0:1, :]

    fin_ref[0, 0:1, :] = lax.fori_loop(0, ntile, fwd, h0_ref[0, 0:1, :])
    fin_ref[0, 1:2, :] = lax.fori_loop(0, ntile, bwd, h0_ref[0, 1:2, :])

    for c in range(nch):
        sl = slice(c * ch, (c + 1) * ch)
        ya_ref[0, sl, :] = (gg_ref[0, sl, :].astype(F32) * (b_f[sl, :] + b_b[sl, :])).astype(BF16)


def _lru_call(xb, gg, conv_w, conv_b, wg, ba, bx, lam, h0, n_seq, T):
    xb3 = xb.reshape(n_seq, T, D_LRU)
    gg3 = gg.reshape(n_seq, T, D_LRU)
    full = lambda shape: pl.BlockSpec(shape, lambda i: (0,) * len(shape))
    seq = lambda shape: pl.BlockSpec(shape, lambda i: (i,) + (0,) * (len(shape) - 1))
    ya, fin = pl.pallas_call(
        functools.partial(_lru_body, T=T),
        grid=(n_seq,),
        in_specs=[seq((1, T, D_LRU)), seq((1, T, D_LRU)), full((CONV_W, D_LRU)), full((1, D_LRU)),
                  full((4, 256, 1024)), full((2, D_LRU)), full((2, D_LRU)), full((2, D_LRU)),
                  seq((1, 2, D_LRU))],
        out_specs=[seq((1, T, D_LRU)), seq((1, 2, D_LRU))],
        out_shape=[jax.ShapeDtypeStruct((n_seq, T, D_LRU), BF16),
                   jax.ShapeDtypeStruct((n_seq, 2, D_LRU), F32)],
        scratch_shapes=[pltpu.VMEM((T + 2 * SUBLANES, D_LRU), F32)] + [pltpu.VMEM((T, D_LRU), F32)] * 4,
        compiler_params=_cp(("parallel",)), name="lru",
    )(xb3, gg3, conv_w, conv_b, wg, ba, bx, lam, h0)
    return ya.reshape(n_seq * T, D_LRU), fin


def _s5_body(u_ref, wts_ref, wc_ref, ap_ref, h0_ref, y_ref, fin_ref, s_scr, hp_scr, slab,
             *, rt, n_c, with_h0, col_major):
    nseq = rt // n_c
    q = 512
    seq_tok = n_c * S5_L
    halves = seq_tok // (GRID_W * S5_L)

    def tok_rows(s, h, i):
        return pl.ds(s * seq_tok + (h * S5_L + i) * GRID_W, GRID_W)

    def chunk_rows(s, h):
        return pl.ds(s * n_c + h, GRID_W, stride=halves)

    if col_major:
        for i in range(S5_L):
            for s in range(nseq):
                for h in range(halves):
                    slab[i, chunk_rows(s, h), :] = u_ref[0, tok_rows(s, h, i), :]
        ub = jnp.concatenate([slab[i] for i in range(S5_L)], axis=1).astype(BF16)
    else:
        ub = jnp.concatenate([u_ref[0, pl.ds(i, rt, stride=S5_L), :] for i in range(S5_L)],
                             axis=1).astype(BF16)
    r1 = jnp.dot(ub, wts_ref[0], preferred_element_type=F32)
    y_in = r1[:, :S5_L * LANES]
    s_scr[...] = r1[:, S5_L * LANES:]
    ap = ap_ref[0]
    if with_h0:
        for s in range(nseq):
            h0 = h0_ref[0, 0, s:s + 1, :]
            r_f = s * n_c
            r_b = s * n_c + n_c - 1
            ar, ai = ap[0:1, 0:q], ap[0:1, q:2 * q]
            s_scr[r_f:r_f + 1, 0:q] += ar * h0[:, 0:q] - ai * h0[:, q:2 * q]
            s_scr[r_f:r_f + 1, q:2 * q] += ar * h0[:, q:2 * q] + ai * h0[:, 0:q]
            br, bi = ap[0:1, 2 * q:3 * q], ap[0:1, 3 * q:4 * q]
            s_scr[r_b:r_b + 1, 2 * q:3 * q] += br * h0[:, 2 * q:3 * q] - bi * h0[:, 3 * q:4 * q]
            s_scr[r_b:r_b + 1, 3 * q:4 * q] += br * h0[:, 3 * q:4 * q] + bi * h0[:, 2 * q:3 * q]

    cidx = lax.broadcasted_iota(I32, (rt, 1), 0) % n_c
    hfr, hfi = s_scr[:, 0:q], s_scr[:, q:2 * q]
    hbr, hbi = s_scr[:, 2 * q:3 * q], s_scr[:, 3 * q:4 * q]
    lvl = 0
    dist = 1
    while dist < n_c:
        ar, ai = ap[lvl:lvl + 1, 0:q], ap[lvl:lvl + 1, q:2 * q]
        br, bi = ap[lvl:lvl + 1, 2 * q:3 * q], ap[lvl:lvl + 1, 3 * q:4 * q]
        kf = cidx >= dist
        kb = cidx < n_c - dist
        sfr = jnp.where(kf, pltpu.roll(hfr, dist, 0), 0.0)
        sfi = jnp.where(kf, pltpu.roll(hfi, dist, 0), 0.0)
        sbr = jnp.where(kb, pltpu.roll(hbr, rt - dist, 0), 0.0)
        sbi = jnp.where(kb, pltpu.roll(hbi, rt - dist, 0), 0.0)
        hfr, hfi = hfr + (ar * sfr - ai * sfi), hfi + (ar * sfi + ai * sfr)
        hbr, hbi = hbr + (br * sbr - bi * sbi), hbi + (br * sbi + bi * sbr)
        lvl += 1
        dist *= 2

    s_scr[:, 0:q] = hfr
    s_scr[:, q:2 * q] = hfi
    s_scr[:, 2 * q:3 * q] = hbr
    s_scr[:, 3 * q:4 * q] = hbi
    for s in range(nseq):
        fin_ref[0, 0, s:s + 1, 0:2 * q] = s_scr[s * n_c + n_c - 1:s * n_c + n_c, 0:2 * q]
        fin_ref[0, 0, s:s + 1, 2 * q:4 * q] = s_scr[s * n_c:s * n_c + 1, 2 * q:4 * q]

    kf = cidx >= 1
    kb = cidx < n_c - 1
    hp_scr[:, 0:q] = jnp.where(kf, pltpu.roll(hfr, 1, 0), 0.0)
    hp_scr[:, q:2 * q] = jnp.where(kf, pltpu.roll(hfi, 1, 0), 0.0)
    hp_scr[:, 2 * q:3 * q] = jnp.where(kb, pltpu.roll(hbr, rt - 1, 0), 0.0)
    hp_scr[:, 3 * q:4 * q] = jnp.where(kb, pltpu.roll(hbi, rt - 1, 0), 0.0)
    if with_h0:
        for s in range(nseq):
            hp_scr[s * n_c:s * n_c + 1, 0:2 * q] = h0_ref[0, 0, s:s + 1, 0:2 * q]
            hp_scr[s * n_c + n_c - 1:s * n_c + n_c, 2 * q:4 * q] = h0_ref[0, 0, s:s + 1, 2 * q:4 * q]
    y = y_in + jnp.dot(hp_scr[...].astype(BF16), wc_ref[0], preferred_element_type=F32)
    for i in range(S5_L):
        y_i = y[:, i * LANES:(i + 1) * LANES]
        if col_major:
            slab[i] = y_i
            for s in range(nseq):
                for h in range(halves):
                    y_ref[0, tok_rows(s, h, i), :] = slab[i, chunk_rows(s, h), :]
        else:
            y_ref[0, pl.ds(i, rt, stride=S5_L), :] = y_i


def _s5_call(us4, wts, wc, apow, h0, n_seq, T, with_h0, col_major):
    n_c = T // S5_L
    rows = n_seq * n_c
    rt = 256
    nseq_t = rt // n_c
    h0 = h0.reshape(S5_BLK, n_seq // nseq_t, nseq_t, S5_SW)
    y4, fin = pl.pallas_call(
        functools.partial(_s5_body, rt=rt, n_c=n_c, with_h0=with_h0, col_major=col_major),
        grid=(S5_BLK, rows // rt),
        in_specs=[pl.BlockSpec((1, rt * S5_L, LANES), lambda b, r: (b, r, 0)),
                  pl.BlockSpec((1, S5_L * LANES, S5_L * LANES + S5_SW), lambda b, r: (b, 0, 0)),
                  pl.BlockSpec((1, S5_SW, S5_L * LANES), lambda b, r: (b, 0, 0)),
                  pl.BlockSpec((1, 8, S5_SW), lambda b, r: (b, 0, 0)),
                  pl.BlockSpec((1, 1, nseq_t, S5_SW), lambda b, r: (b, r, 0, 0))],
        out_specs=[pl.BlockSpec((1, rt * S5_L, LANES), lambda b, r: (b, r, 0)),
                   pl.BlockSpec((1, 1, nseq_t, S5_SW), lambda b, r: (b, r, 0, 0))],
        out_shape=[jax.ShapeDtypeStruct((S5_BLK, n_seq * T, LANES), F32),
                   jax.ShapeDtypeStruct((S5_BLK, n_seq // nseq_t, nseq_t, S5_SW), F32)],
        scratch_shapes=[pltpu.VMEM((rt, S5_SW), F32), pltpu.VMEM((rt, S5_SW), F32),
                        pltpu.VMEM((S5_L, rt, LANES), F32)],
        compiler_params=_cp(("parallel", "parallel")), name="s5",
    )(us4, wts, wc, apow, h0)
    return y4, fin.reshape(S5_BLK, n_seq, S5_SW)


def _s5_expand_body(toe_ref, sf_ref, sb_ref, cf_ref, cb_ref, wts_ref, wc_ref):
    L, N, P = S5_L, S5_N, S5_P
    w = L * LANES
    r = lax.broadcasted_iota(I32, (LANES, w), 0)
    c = lax.broadcasted_iota(I32, (LANES, w), 1)
    one = lambda m: jnp.where(m, 1.0, 0.0).astype(BF16)
    div = lambda x, d: x >> (d.bit_length() - 1)
    mod = lambda x, d: x & (d - 1)
    t_tap = one((div(r, P) == div(c, LANES)) & (mod(r, P) == mod(c, P)))
    m_tap = div(r, P) == div(mod(c, LANES), P)
    t_st = one((div(r, N) == div(c, 8 * N)) & (mod(r, N) == mod(c, N)))
    m_st = div(r, P) == div(mod(c, 8 * N), N)
    for j in range(L):
        rows = slice(j * LANES, (j + 1) * LANES)
        tap = jnp.dot(toe_ref[0, j].astype(BF16), t_tap, preferred_element_type=F32)
        wts_ref[0, rows, 0:w] = jnp.where(m_tap, tap, 0.0).astype(BF16)
        for k, ref in enumerate((sf_ref, sb_ref)):
            st = jnp.dot(ref[0, j].astype(BF16), t_st, preferred_element_type=F32)
            wts_ref[0, rows, w + k * w:w + (k + 1) * w] = jnp.where(m_st, st, 0.0).astype(BF16)
    rr = lax.broadcasted_iota(I32, (w, LANES), 0)
    cc = lax.broadcasted_iota(I32, (w, LANES), 1)
    t_row = one((div(rr, 8 * N) == div(cc, N)) & (mod(rr, N) == mod(cc, N)))
    r2 = lax.broadcasted_iota(I32, (w, w), 0)
    c2 = lax.broadcasted_iota(I32, (w, w), 1)
    m_row = div(mod(r2, 8 * N), N) == div(mod(c2, LANES), P)
    for k, ref in enumerate((cf_ref, cb_ref)):
        corr = jnp.dot(t_row, ref[0].astype(BF16), preferred_element_type=F32)
        wc_ref[0, k * w:(k + 1) * w, :] = jnp.where(m_row, corr, 0.0).astype(BF16)


def _s5_expand_call(toe, sf, sb, cf, cb):
    L = S5_L
    w = L * LANES
    blk4 = lambda a, b_: pl.BlockSpec((1, L, a, b_), lambda i: (i, 0, 0, 0))
    blk3 = lambda a, b_: pl.BlockSpec((1, a, b_), lambda i: (i, 0, 0))
    return pl.pallas_call(
        _s5_expand_body, grid=(S5_BLK,),
        in_specs=[blk4(LANES, LANES), blk4(LANES, LANES), blk4(LANES, LANES), blk3(LANES, w), blk3(LANES, w)],
        out_specs=[blk3(w, w + S5_SW), blk3(S5_SW, w)],
        out_shape=[jax.ShapeDtypeStruct((S5_BLK, w, w + S5_SW), BF16),
                   jax.ShapeDtypeStruct((S5_BLK, S5_SW, w), BF16)],
        compiler_params=_cp(("parallel",)), name="s5_expand",
    )(toe, sf, sb, cf, cb)


def _s5_weights(lam_re, lam_im, log_step, b_re, b_im, c_re, c_im):
    L, G, N, P = S5_L, S5_G, S5_N, S5_P
    hp = lax.Precision.HIGHEST
    step = jnp.exp(log_step)[..., None]
    lr, li = lam_re, lam_im
    mag = jnp.exp(lr * step)
    ang = li * step
    ar, ai = mag * jnp.cos(ang), mag * jnp.sin(ang)
    den = lr * lr + li * li
    fr = ((ar - 1.0) * lr + ai * li) / den
    fi = (ai * lr - (ar - 1.0) * li) / den
    bbr = fr[..., None] * b_re - fi[..., None] * b_im
    bbi = fr[..., None] * b_im + fi[..., None] * b_re
    pr, pi = [jnp.ones_like(ar)], [jnp.zeros_like(ar)]
    for _ in range(L):
        pr.append(pr[-1] * ar - pi[-1] * ai)
        pi.append(pr[-2] * ai + pi[-1] * ar)
    pr = jnp.stack(pr, axis=2)
    pi = jnp.stack(pi, axis=2)
    bbr_t, bbi_t = bbr.transpose(0, 1, 3, 2), bbi.transpose(0, 1, 3, 2)
    er = pr[:, :, :, None, :] * bbr_t[:, :, None] - pi[:, :, :, None, :] * bbi_t[:, :, None]
    ei = pr[:, :, :, None, :] * bbi_t[:, :, None] + pi[:, :, :, None, :] * bbr_t[:, :, None]
    kt = (jnp.einsum('dgpn,dgkqn->dgkqp', c_re, er, precision=hp)
          - jnp.einsum('dgpn,dgkqn->dgkqp', c_im, ei, precision=hp))
    ii = jnp.arange(L)
    dif = ii[None, :] - ii[:, None]
    tf = jnp.where((dif >= 0)[None, :, :, None, None], kt[0][:, jnp.clip(dif, 0, L)], 0.0)
    tb = jnp.where((dif <= 0)[None, :, :, None, None], kt[1][:, jnp.clip(-dif, 0, L)], 0.0)
    toe = (tf + tb).reshape(S5_BLK, 8, L, L, P, P)
    toe = toe.transpose(0, 2, 1, 4, 3, 5).reshape(S5_BLK, L, LANES, LANES)

    def state_map(d, ks):
        e = jnp.concatenate([er[d][:, ks], ei[d][:, ks]], axis=-1)
        return e.reshape(S5_BLK, 8, L, P, 2 * N).transpose(0, 2, 1, 3, 4).reshape(S5_BLK, L, LANES, 2 * N)

    def corr_map(d, ks):
        p_r, p_i = pr[d][:, ks][:, :, None, :], pi[d][:, ks][:, :, None, :]
        cr, ci = c_re[d][:, None], c_im[d][:, None]
        g = jnp.stack([cr * p_r - ci * p_i, -(cr * p_i + ci * p_r)])
        g = g.reshape(2, S5_BLK, 8, L, P, N).transpose(1, 0, 5, 3, 2, 4)
        return g.reshape(S5_BLK, 2 * N, L * LANES)

    wts, wc = _s5_expand_call(toe, state_map(0, jnp.arange(L - 1, -1, -1)), state_map(1, jnp.arange(L)),
                              corr_map(0, jnp.arange(1, L + 1)), corr_map(1, jnp.arange(L, 0, -1)))

    qr, qi = pr[:, :, L], pi[:, :, L]
    lv = []
    for _ in range(8):
        lv.append(jnp.concatenate([qr[0].reshape(S5_BLK, 8 * N), qi[0].reshape(S5_BLK, 8 * N),
                                   qr[1].reshape(S5_BLK, 8 * N), qi[1].reshape(S5_BLK, 8 * N)], axis=1))
        qr, qi = qr * qr - qi * qi, 2.0 * qr * qi
    apow = jnp.stack(lv, axis=1)
    return wts, wc, apow


def _mixout_body(x_ref, mod_ref, g1_ref, g2_ref, wbg_ref, bbg_ref, ya_ref, y4_ref, u4_ref,
                 dsk_ref, wglu_ref, bglu_ref, wso_ref, wlo_ref, wo_ref, wrt_ref, brt_ref,
                 x1_ref, h2_ref, aff_ref):
    m = mod_ref[0]
    sh1, sc1, gt1 = m[:, 0:D_MODEL], m[:, D_MODEL:2 * D_MODEL], m[:, 2 * D_MODEL:3 * D_MODEL]
    sh2, sc2 = m[:, 3 * D_MODEL:4 * D_MODEL], m[:, 4 * D_MODEL:5 * D_MODEL]
    x = x_ref[...]
    h = _norm_mod(x, g1_ref[...], sc1, sh1)
    gates = _sigmoid(jnp.dot(h.astype(BF16), wbg_ref[...], preferred_element_type=F32) + bbg_ref[...])

    def assemble(ref):
        return jnp.concatenate([ref[k] for k in range(S5_BLK)], axis=1)

    ys = assemble(y4_ref) + dsk_ref[...] * assemble(u4_ref)
    v = _gelu(ys)
    ob = v * _sigmoid(jnp.dot(v.astype(BF16), wglu_ref[...], preferred_element_type=F32) + bglu_ref[...])
    merged = (gates[:, :D_MODEL] * jnp.dot(ya_ref[...], wlo_ref[...], preferred_element_type=F32)
              + gates[:, D_MODEL:] * jnp.dot(ob.astype(BF16), wso_ref[...], preferred_element_type=F32))
    mix = jnp.dot(merged.astype(BF16), wo_ref[...], preferred_element_type=F32)
    x1 = x + gt1 * mix
    x1_ref[...] = x1
    h2 = _norm_mod(x1, g2_ref[...], sc2, sh2)
    h2_ref[...] = h2
    logits = _dot3(wrt_ref[...], h2, dims=(((1,), (1,)), ((), ()))) + brt_ref[...]
    mx = jnp.max(logits, axis=0, keepdims=True)
    ex = jnp.exp(logits - mx)
    aff_ref[...] = ex / jnp.sum(ex, axis=0, keepdims=True)


def _mixout_call(x2d, mod3, g1, g2, wbg, bbg, ya, y4, u4, dsk, wglu, bglu, wso, wlo, wo, wrt, brt,
                 mod_row, tm):
    n = x2d.shape[0]
    full = lambda shape: pl.BlockSpec(shape, lambda i: (0,) * len(shape))
    s5_spec = pl.BlockSpec((S5_BLK, tm, LANES), lambda i: (0, i, 0))
    tok = lambda w: pl.BlockSpec((tm, w), lambda i: (i, 0))
    return pl.pallas_call(
        _mixout_body,
        grid=(n // tm,),
        in_specs=[tok(D_MODEL), pl.BlockSpec((1, 1, 6 * D_MODEL), lambda i: (mod_row(i), 0, 0)),
                  full((1, D_MODEL)), full((1, D_MODEL)), full((D_MODEL, 2 * D_MODEL)),
                  full((1, 2 * D_MODEL)), tok(D_LRU), s5_spec, s5_spec, full((1, D_S5)),
                  full((D_S5, D_S5)), full((1, D_S5)), full((D_S5, D_MODEL)), full((D_LRU, D_MODEL)),
                  full((D_MODEL, D_MODEL)), full((N_EXPERTS, D_MODEL)), full((N_EXPERTS, 1))],
        out_specs=[tok(D_MODEL), tok(D_MODEL), pl.BlockSpec((N_EXPERTS, tm), lambda i: (0, i))],
        out_shape=[jax.ShapeDtypeStruct((n, D_MODEL), F32), jax.ShapeDtypeStruct((n, D_MODEL), F32),
                   jax.ShapeDtypeStruct((N_EXPERTS, n), F32)],
        compiler_params=_cp(("parallel",)), name="mixout",
    )(x2d, mod3, g1, g2, wbg, bbg, ya, y4, u4, dsk, wglu, bglu, wso, wlo, wo, wrt, brt)


def _select_body(aff_ref, gate_ref, pos_ref, offs_ref, *, n_tok, cap):
    aff = aff_ref[...]
    capf = float(cap)

    def bis(_, lh):
        lo, hi = lh
        mid = lo + ((hi - lo + 1) >> 1)
        cnt = jnp.sum(jnp.where(aff >= pltpu.bitcast(mid, F32), 1.0, 0.0), axis=1, keepdims=True)
        ok = cnt >= capf
        return jnp.where(ok, mid, lo), jnp.where(ok, hi, mid - 1)

    lo0 = jnp.zeros((N_EXPERTS, 1), I32)
    hi0 = jnp.full((N_EXPERTS, 1), 0x7F800000, I32)
    thr_bits, _ = lax.fori_loop(0, 31, bis, (lo0, hi0))
    thr = pltpu.bitcast(thr_bits, F32)
    need = capf - jnp.sum(jnp.where(aff > thr, 1.0, 0.0), axis=1, keepdims=True)

    r = lax.broadcasted_iota(I32, (TOK_TILE, TOK_TILE), 0)
    c = lax.broadcasted_iota(I32, (TOK_TILE, TOK_TILE), 1)
    tri = jnp.where(r < c, 1.0, 0.0).astype(BF16)
    lane = lax.broadcasted_iota(I32, (N_EXPERTS, LANES), 1)
    n_tiles = n_tok // TOK_TILE
    run_eq = jnp.zeros((N_EXPERTS, 1), F32)
    run_sel = jnp.zeros((N_EXPERTS, 1), F32)
    offs = jnp.zeros((N_EXPERTS, LANES), F32)
    for t in range(n_tiles):
        sl = slice(t * TOK_TILE, (t + 1) * TOK_TILE)
        aff_t = aff_ref[:, sl]
        is_eq = aff_t == thr
        eq_t = jnp.where(is_eq, 1.0, 0.0)
        rank_eq = jnp.dot(eq_t.astype(BF16), tri, preferred_element_type=F32) + run_eq
        cand = (aff_t > thr) | (is_eq & (rank_eq < need))
        pos_t = jnp.dot(jnp.where(cand, 1.0, 0.0).astype(BF16), tri, preferred_element_type=F32) + run_sel
        sel_t = cand & (pos_t < capf)
        sel_f = jnp.where(sel_t, 1.0, 0.0)
        offs = jnp.where(lane == t, run_sel, offs)
        gate_ref[:, sl] = jnp.where(sel_t, aff_t, 0.0)
        pos_ref[:, sl] = jnp.where(sel_t, pos_t, -1.0).astype(I32)
        run_eq = run_eq + jnp.sum(eq_t, axis=1, keepdims=True)
        run_sel = run_sel + jnp.sum(sel_f, axis=1, keepdims=True)
    offs = jnp.where(lane >= n_tiles, run_sel, offs)
    offs_ref[...] = offs.astype(I32)


def _select_call(aff_t, cap):
    n_tok = aff_t.shape[1]
    full = lambda shape: pl.BlockSpec(shape, lambda i: (0,) * len(shape))
    return pl.pallas_call(
        functools.partial(_select_body, n_tok=n_tok, cap=cap),
        grid=(1,),
        in_specs=[full((N_EXPERTS, n_tok))],
        out_specs=[full((N_EXPERTS, n_tok)), full((N_EXPERTS, n_tok)), full((N_EXPERTS, LANES))],
        out_shape=[jax.ShapeDtypeStruct((N_EXPERTS, n_tok), F32),
                   jax.ShapeDtypeStruct((N_EXPERTS, n_tok), I32),
                   jax.ShapeDtypeStruct((N_EXPERTS, LANES), I32)],
        compiler_params=_cp(("arbitrary",)), name="select",
    )(aff_t)


def _invert_body(offs_ref, pos_ref, gate_ref, idx_ref, gcol_ref, *, n_tiles, cap):
    n_chunks = cap // SLOT_CHUNK
    jcol = lax.broadcasted_iota(I32, (SLOT_CHUNK, LANES), 0)
    lane = lax.broadcasted_iota(I32, (1, LANES), 1)

    def per_expert(e, _):
        base = e * LANES

        def per_chunk(sc, t_start):
            lo = sc * SLOT_CHUNK
            hi = lo + SLOT_CHUNK
            t0 = lax.while_loop(lambda t: offs_ref[base + t + 1] <= lo, lambda t: t + 1, t_start)

            def cond(st):
                t = st[0]
                return jnp.logical_and(t < n_tiles, offs_ref[base + jnp.minimum(t, n_tiles - 1)] < hi)

            def step(st):
                t, acc_i, acc_g = st
                prow = pos_ref[pl.ds(e, 1), pl.ds(t, 1), :][0]
                grow = gate_ref[pl.ds(e, 1), pl.ds(t, 1), :][0]
                hit = prow == (jcol + lo)
                tok = (lane + t * TOK_TILE).astype(F32)
                acc_i = jnp.maximum(acc_i, jnp.where(hit, tok, -1.0))
                acc_g = jnp.maximum(acc_g, jnp.where(hit, grow, 0.0))
                return t + 1, acc_i, acc_g

            init = (t0, jnp.full((SLOT_CHUNK, LANES), -1.0, F32), jnp.zeros((SLOT_CHUNK, LANES), F32))
            _, acc_i, acc_g = lax.while_loop(cond, step, init)
            r0 = pl.multiple_of(lo, SLOT_CHUNK)
            idx_ref[pl.ds(e, 1), pl.ds(r0, SLOT_CHUNK), :] = (
                jnp.max(acc_i, axis=1, keepdims=True).astype(I32)[None])
            gcol_ref[pl.ds(e, 1), pl.ds(r0, SLOT_CHUNK), :] = jnp.max(acc_g, axis=1, keepdims=True)[None]
            return t0

        lax.fori_loop(0, n_chunks, per_chunk, 0)
        return 0

    lax.fori_loop(0, N_EXPERTS, per_expert, 0)


def _invert_call(offs, pos, gate, cap):
    n_tok = pos.shape[1]
    n_tiles = n_tok // TOK_TILE
    pos3 = pos.reshape(N_EXPERTS, n_tiles, TOK_TILE)
    gate3 = gate.reshape(N_EXPERTS, n_tiles, TOK_TILE)
    full = lambda shape: pl.BlockSpec(shape, lambda i, o: (0,) * len(shape))
    return pl.pallas_call(
        functools.partial(_invert_body, n_tiles=n_tiles, cap=cap),
        grid_spec=pltpu.PrefetchScalarGridSpec(
            num_scalar_prefetch=1, grid=(1,),
            in_specs=[full((N_EXPERTS, n_tiles, TOK_TILE)), full((N_EXPERTS, n_tiles, TOK_TILE))],
            out_specs=[full((N_EXPERTS, cap, 1)), full((N_EXPERTS, cap, 1))]),
        out_shape=[jax.ShapeDtypeStruct((N_EXPERTS, cap, 1), I32),
                   jax.ShapeDtypeStruct((N_EXPERTS, cap, 1), F32)],
        compiler_params=_cp(("arbitrary",)), name="invert",
    )(offs.reshape(-1), pos3, gate3)


def _ffn_body(idx0_ref, idxn_ref, gcol_ref, wg_ref, wu_ref, wd_ref, h_hbm, out_ref,
              xe, sem, *, cap):
    e = pl.program_id(0)
    slot = e % 2
    nxt = 1 - slot

    def issue_rows(idx_ref, s, j0, n):
        for j in range(j0, j0 + n):
            pltpu.make_async_copy(h_hbm.at[pl.ds(idx_ref[0, 0, j], 1), :],
                                  xe.at[s, pl.ds(j, 1), :], sem.at[s]).start()

    @pl.when(e == 0)
    def _():
        def issue(j, _):
            pltpu.make_async_copy(h_hbm.at[pl.ds(idx0_ref[0, 0, j], 1), :],
                                  xe.at[0, pl.ds(j, 1), :], sem.at[0]).start()
            return 0
        lax.fori_loop(0, cap, issue, 0, unroll=8)

    wgb = wg_ref[0].astype(BF16)
    wub = wu_ref[0].astype(BF16)
    wdb = wd_ref[0].astype(BF16)

    pltpu.make_async_copy(h_hbm.at[pl.ds(0, cap), :], xe.at[slot], sem.at[slot]).wait()

    ch = 256
    for c in range(cap // ch):
        sl = slice(c * ch, (c + 1) * ch)

        issue_rows(idxn_ref, nxt, c * ch, ch)
        xb = xe[slot, sl, :].astype(BF16)
        g = jnp.dot(xb, wgb, preferred_element_type=F32)
        u = jnp.dot(xb, wub, preferred_element_type=F32)
        hid = (g * _sigmoid(g)) * u
        ye = jnp.dot(hid.astype(BF16), wdb, preferred_element_type=F32)
        out_ref[0, sl, :] = ye * gcol_ref[0, sl, :]

    @pl.when(e == N_EXPERTS - 1)
    def _():
        pltpu.make_async_copy(h_hbm.at[pl.ds(0, cap), :], xe.at[nxt], sem.at[nxt]).wait()


def _ffn_call(idx, gcol, w_gate, w_up, w_down, h2, cap):
    d_e = w_gate.shape[2]
    wspec = lambda a, b: pl.BlockSpec((1, a, b), lambda e: (e, 0, 0))
    return pl.pallas_call(
        functools.partial(_ffn_body, cap=cap),
        grid=(N_EXPERTS,),
        in_specs=[pl.BlockSpec((1, 1, cap), lambda e: (0, 0, 0), memory_space=pltpu.SMEM),
                  pl.BlockSpec((1, 1, cap), lambda e: (jnp.minimum(e + 1, N_EXPERTS - 1), 0, 0),
                               memory_space=pltpu.SMEM),
                  pl.BlockSpec((1, cap, 1), lambda e: (e, 0, 0)),
                  wspec(D_MODEL, d_e), wspec(D_MODEL, d_e), wspec(d_e, D_MODEL),
                  pl.BlockSpec(memory_space=pl.ANY)],
        out_specs=pl.BlockSpec((1, cap, D_MODEL), lambda e: (e, 0, 0)),
        out_shape=jax.ShapeDtypeStruct((N_EXPERTS, cap, D_MODEL), F32),
        scratch_shapes=[pltpu.VMEM((2, cap, D_MODEL), F32), pltpu.SemaphoreType.DMA((2,))],
        compiler_params=_cp(("arbitrary",)), name="ffn",
    )(idx, idx, gcol, w_gate, w_up, w_down, h2)


def _combine_body(offs_ref, x1_ref, mod_ref, gf_ref, pos_ref, yb_hbm, y_ref, wins, sem, *, cap):
    t = pl.program_id(0)
    n_t = pl.num_programs(0)
    slot = t % 2
    fetch = COMBINE_WIN + SUBLANES
    sub = lax.broadcasted_iota(I32, (fetch, TOK_TILE), 0)

    def lo_of(tt, e, w):
        return offs_ref[e * LANES + tt] + w * COMBINE_WIN

    def start_of(lo):
        return pl.multiple_of(jnp.minimum((lo // SUBLANES) * SUBLANES, cap - fetch), SUBLANES)

    def issue(tt, w, s):
        for e in range(N_EXPERTS):
            pltpu.make_async_copy(yb_hbm.at[e, pl.ds(start_of(lo_of(tt, e, w)), fetch), :],
                                  wins.at[s, pl.ds(e * fetch, fetch), :], sem.at[s]).start()

    def drain(s):
        pltpu.make_async_copy(yb_hbm.at[0, pl.ds(0, N_EXPERTS * fetch), :],
                              wins.at[s], sem.at[s]).wait()

    def expand(w, s):
        rows = []
        for e in range(N_EXPERTS):
            lo = lo_of(t, e, w)
            pos = pos_ref[e:e + 1, :]
            in_round = jnp.logical_and(pos >= lo, pos < lo + COMBINE_WIN)
            hit = jnp.logical_and(pos - start_of(lo) == sub, in_round)
            rows.append(jnp.where(hit, 1.0, 0.0))
        onehot = jnp.concatenate(rows, axis=0).T.astype(BF16)
        win = wins[s]
        hi = win.astype(BF16)
        lo_part = (win - hi.astype(F32)).astype(BF16)
        return (jnp.dot(onehot, hi, preferred_element_type=F32)
                + jnp.dot(onehot, lo_part, preferred_element_type=F32))

    @pl.when(t == 0)
    def _():
        issue(0, 0, 0)

    @pl.when(t + 1 < n_t)
    def _():
        issue(t + 1, 0, 1 - slot)

    drain(slot)
    acc = expand(0, slot)

    kmax = 0
    for e in range(N_EXPERTS):
        kmax = jnp.maximum(kmax, offs_ref[e * LANES + t + 1] - offs_ref[e * LANES + t])

    def more(w, a):
        issue(t, w, slot)
        drain(slot)
        return a + expand(w, slot)
    acc = lax.fori_loop(1, (kmax + COMBINE_WIN - 1) // COMBINE_WIN, more, acc)

    gt2 = mod_ref[0][:, 5 * D_MODEL:6 * D_MODEL]
    x2 = x1_ref[...] + gt2 * acc
    ms = jnp.mean(x2 * x2, axis=-1, keepdims=True)
    y_ref[...] = x2 * lax.rsqrt(ms + EPS) * gf_ref[...]


def _combine_call(offs, pos, x1, mod3, gf, ybuf, mod_row, cap):
    n = x1.shape[0]
    return pl.pallas_call(
        functools.partial(_combine_body, cap=cap),
        grid_spec=pltpu.PrefetchScalarGridSpec(
            num_scalar_prefetch=1, grid=(n // TOK_TILE,),
            in_specs=[pl.BlockSpec((TOK_TILE, D_MODEL), lambda i, o: (i, 0)),
                      pl.BlockSpec((1, 1, 6 * D_MODEL), lambda i, o: (mod_row(i), 0, 0)),
                      pl.BlockSpec((1, D_MODEL), lambda i, o: (0, 0)),
                      pl.BlockSpec((N_EXPERTS, TOK_TILE), lambda i, o: (0, i)),
                      pl.BlockSpec(memory_space=pl.ANY)],
            out_specs=pl.BlockSpec((TOK_TILE, D_MODEL), lambda i, o: (i, 0)),
            scratch_shapes=[pltpu.VMEM((2, N_EXPERTS * (COMBINE_WIN + SUBLANES), D_MODEL), F32),
                            pltpu.SemaphoreType.DMA((2,))]),
        out_shape=jax.ShapeDtypeStruct((n, D_MODEL), F32),
        compiler_params=_cp(("arbitrary",)), name="combine",
    )(offs.reshape(-1), x1, mod3, gf, pos, ybuf)


def _lru_gate_weights(wa, wx):
    eye = jnp.eye(4, dtype=F32)

    def bd(w):
        w5 = w.reshape(4, 4, LRU_HEAD_DIM, LRU_HEAD_DIM)
        return jnp.einsum('khij,hg->khigj', w5, eye).reshape(4, 256, 256)

    return jnp.concatenate([bd(wa[0]), bd(wa[1]), bd(wx[0]), bd(wx[1])], axis=2).astype(BF16)


def kernel(x_prompt, x_sample, state_lru, state_s5_re, state_s5_im, c, c_ctx, w_mod, b_mod, g_norm1, g_norm2, w_in, conv_w, conv_b, lru_wa, lru_ba, lru_wx, lru_bx, lru_lambda, s5_lambda_re, s5_lambda_im, s5_log_step, s5_b_re, s5_b_im, s5_c_re, s5_c_im, s5_d, s5_w_glu, s5_b_glu, w_lru_out, w_s5_out, w_branch_gate, b_branch_gate, w_o, w_router, b_router, w_e_gate, w_e_up, w_e_down, g_final):
    bp, tp, _ = x_prompt.shape
    bs, ts, _ = x_sample.shape
    n_p, n_s = bp * tp, bs * ts
    l = 0

    c_all = jnp.zeros((16, D_MODEL), F32).at[0].set(c_ctx).at[1:1 + bs].set(c)
    mod3 = _mod_call(c_all, w_mod[l], b_mod[l][None, :]).reshape(16, 1, 6 * D_MODEL)

    w_in_b = w_in[l].astype(BF16)
    w_main, w_s5in = w_in_b[:, :2 * D_LRU], w_in_b[:, 2 * D_LRU:]
    g1 = g_norm1[l][None, :]
    g2 = g_norm2[l][None, :]
    wg = _lru_gate_weights(lru_wa[l], lru_wx[l])
    wts, wc, apow = _s5_weights(s5_lambda_re[l], s5_lambda_im[l], s5_log_step[l],
                                s5_b_re[l], s5_b_im[l], s5_c_re[l], s5_c_im[l])
    wbg = w_branch_gate[l].astype(BF16)
    bbg = b_branch_gate[l][None, :]
    wglu = s5_w_glu[l].astype(BF16)
    wso = w_s5_out[l].astype(BF16)
    wlo = w_lru_out[l].astype(BF16)
    wo = w_o[l].astype(BF16)
    wrt = w_router[l].T
    brt = b_router[l][:, None]

    xp2 = x_prompt.reshape(n_p, D_MODEL)
    xs2 = x_sample.reshape(n_s, D_MODEL)
    ctx_row = lambda i: 0
    tm_in = 512
    lat_row_in = lambda i: 1 + i // (ts // tm_in)

    def s5_state(re, im):
        def part(a, d):
            return a[:, d].reshape(-1, S5_BLK, 8 * S5_N).transpose(1, 0, 2)
        return jnp.concatenate([part(re, 0), part(im, 0), part(re, 1), part(im, 1)], axis=2)

    xb_p, gg_p, us4_p = _inproj_call(xp2, mod3, g1, w_main, w_s5in, ctx_row, tm_in)
    ya_p, lru_fin = _lru_call(xb_p, gg_p, conv_w[l], conv_b[l][None, :], wg, lru_ba[l], lru_bx[l],
                              lru_lambda[l], jnp.zeros((bp, 2, D_LRU), F32), bp, tp)
    y4_p, s5_fin = _s5_call(us4_p, wts, wc, apow, jnp.zeros((S5_BLK, bp, S5_SW), F32), bp, tp,
                            with_h0=False, col_major=False)
    xb_s, gg_s, us4_s = _inproj_call(xs2, mod3, g1, w_main, w_s5in, lat_row_in, tm_in)
    ya_s, _ = _lru_call(xb_s, gg_s, conv_w[l], conv_b[l][None, :], wg, lru_ba[l], lru_bx[l],
                        lru_lambda[l], state_lru[:, l], bs, ts)
    y4_s, _ = _s5_call(us4_s, wts, wc, apow, s5_state(state_s5_re[:, l], state_s5_im[:, l]), bs, ts,
                       with_h0=True, col_major=True)

    mo_args = (wbg, bbg)
    mo_tail = (s5_d[l][None, :], wglu, s5_b_glu[l][None, :], wso, wlo, wo, wrt, brt)
    tm_mo = 256
    x1_p, h2_p, aff_p = _mixout_call(xp2, mod3, g1, g2, *mo_args, ya_p, y4_p, us4_p, *mo_tail,
                                     mod_row=ctx_row, tm=tm_mo)
    x1_s, h2_s, aff_s = _mixout_call(xs2, mod3, g1, g2, *mo_args, ya_s, y4_s, us4_s, *mo_tail,
                                     mod_row=lambda i: 1 + i // (ts // tm_mo), tm=tm_mo)

    cap = (CAPACITY_FACTOR * n_p) // N_EXPERTS
    gf = g_final[None, :]
    ys = []
    for aff, h2, x1, mod_row in ((aff_p, h2_p, x1_p, ctx_row),
                                 (aff_s, h2_s, x1_s, lambda i: 1 + i // (ts // TOK_TILE))):
        gate, pos, offs = _select_call(aff, cap)
        idx, gcol = _invert_call(offs, pos, gate, cap)
        ybuf = _ffn_call(idx.reshape(N_EXPERTS, 1, cap), gcol, w_e_gate[l], w_e_up[l], w_e_down[l], h2, cap)
        ys.append(_combine_call(offs, pos, x1, mod3, gf, ybuf, mod_row, cap))
    y_p, y_s = ys

    new_lru = lru_fin[:, None]
    sf = s5_fin.transpose(1, 0, 2).reshape(bp, S5_BLK, 2, 2, 8, S5_N)
    sf = sf.transpose(0, 2, 3, 1, 4, 5).reshape(bp, 2, 2, S5_G, S5_N)
    new_s5r = sf[:, :, 0][:, None]
    new_s5i = sf[:, :, 1][:, None]
    return (y_p.reshape(bp, tp, D_MODEL), y_s.reshape(bs, ts, D_MODEL), new_lru, new_s5r, new_s5i)
```

```python
import functools
import math

import jax
import jax.numpy as jnp
from jax import lax
from jax.experimental import pallas as pl
from jax.experimental.pallas import tpu as pltpu

F32 = jnp.float32
BF16 = jnp.bfloat16
I32 = jnp.int32

D_MODEL = 1024
D_LRU = 1024
LRU_HEADS = 16
LRU_HEAD_DIM = 64
LRU_C = 8.0
CONV_W = 4
D_S5 = 512
S5_P = 16
S5_G = 32
S5_N = 64
GRID_W = 64
N_EXPERTS = 16
CAPACITY_FACTOR = 2
EPS = 1e-6

LANES = 128
SUBLANES = 8
S5_L = 8
S5_BLK = 4
S5_SW = 4 * 512
TOK_TILE = 128
SLOT_CHUNK = 64
COMBINE_WIN = 32
COMBINE_WIN_WIDE = 64
VMEM_LIMIT = 56 * 1024 * 1024


def _cp(sem, vmem=VMEM_LIMIT):
    return pltpu.CompilerParams(dimension_semantics=sem, vmem_limit_bytes=vmem)


def _bdot(a, b):
    return jnp.dot(a.astype(BF16), b.astype(BF16), preferred_element_type=F32)


def _split2(a):
    hi = a.astype(BF16)
    lo = (a - hi.astype(F32)).astype(BF16)
    return hi, lo


def _dot3(a, b, dims=(((1,), (0,)), ((), ()))):
    ah, al = _split2(a)
    bh, bl = _split2(b)
    d = functools.partial(lax.dot_general, dimension_numbers=dims, preferred_element_type=F32)
    return d(ah, bh) + (d(al, bh) + d(ah, bl))


def _sigmoid(x):
    return 1.0 / (1.0 + jnp.exp(-x))


def _gelu(x):
    c = math.sqrt(2.0 / math.pi)
    return 0.5 * x * (1.0 + jnp.tanh(c * (x + 0.044715 * (x * x * x))))


def _norm_mod(x, g, scale, shift):
    ms = jnp.mean(x * x, axis=-1, keepdims=True)
    return (x * lax.rsqrt(ms + EPS) * g) * (1.0 + scale) + shift


def _mod_body(c_ref, w_ref, b_ref, o_ref):
    c = c_ref[...]
    s = c * _sigmoid(c)
    o_ref[...] = _dot3(s, w_ref[...]) + b_ref[...]


def _mod_call(c_all, w_mod, b_mod):
    n = w_mod.shape[1]
    tn = 1536
    return pl.pallas_call(
        _mod_body,
        grid=(n // tn,),
        in_specs=[pl.BlockSpec((16, D_MODEL), lambda j: (0, 0)),
                  pl.BlockSpec((D_MODEL, tn), lambda j: (0, j)),
                  pl.BlockSpec((1, tn), lambda j: (0, j))],
        out_specs=pl.BlockSpec((16, tn), lambda j: (0, j)),
        out_shape=jax.ShapeDtypeStruct((16, n), F32),
        compiler_params=_cp(("arbitrary",)),
        name="mod",
    )(c_all, w_mod, b_mod)


def _inproj_body(x_ref, mod_ref, g_ref, w_ref, ws_ref, xb_ref, gg_ref, us_ref):
    m = mod_ref[0]
    h = _norm_mod(x_ref[...], g_ref[...], m[:, D_MODEL:2 * D_MODEL], m[:, 0:D_MODEL])
    hb = h.astype(BF16)
    z = jnp.dot(hb, w_ref[...], preferred_element_type=F32)
    xb_ref[...] = z[:, :D_LRU]
    gg_ref[...] = _gelu(z[:, D_LRU:]).astype(BF16)
    us = jnp.dot(hb, ws_ref[...], preferred_element_type=F32)
    for k in range(S5_BLK):
        us_ref[k] = us[:, k * LANES:(k + 1) * LANES]


def _inproj_call(x2d, mod3, g1, w_main, w_s5, mod_row, tm):
    n = x2d.shape[0]
    return pl.pallas_call(
        _inproj_body, grid=(n // tm,),
        in_specs=[pl.BlockSpec((tm, D_MODEL), lambda i: (i, 0)),
                  pl.BlockSpec((1, 1, 6 * D_MODEL), lambda i: (mod_row(i), 0, 0)),
                  pl.BlockSpec((1, D_MODEL), lambda i: (0, 0)),
                  pl.BlockSpec((D_MODEL, 2 * D_LRU), lambda i: (0, 0)),
                  pl.BlockSpec((D_MODEL, D_S5), lambda i: (0, 0))],
        out_specs=[pl.BlockSpec((tm, D_LRU), lambda i: (i, 0)),
                   pl.BlockSpec((tm, D_LRU), lambda i: (i, 0)),
                   pl.BlockSpec((S5_BLK, tm, LANES), lambda i: (0, i, 0))],
        out_shape=[jax.ShapeDtypeStruct((n, D_LRU), F32), jax.ShapeDtypeStruct((n, D_LRU), BF16),
                   jax.ShapeDtypeStruct((S5_BLK, n, LANES), F32)],
        compiler_params=_cp(("parallel",)), name="inproj",
    )(x2d, mod3, g1, w_main, w_s5)


def _lru_body(xb_ref, gg_ref, cw_ref, cb_ref, wg_ref, ba_ref, bx_ref, lam_ref, h0_ref,
              ya_ref, fin_ref, xpad, a_f, b_f, a_b, b_b, *, T):
    ch = 256
    nch = T // ch
    zero8 = jnp.zeros((SUBLANES, D_LRU), F32)
    xpad[0:SUBLANES, :] = zero8
    xpad[T + SUBLANES:T + 2 * SUBLANES, :] = zero8
    for c in range(nch):
        xpad[SUBLANES + c * ch:SUBLANES + (c + 1) * ch, :] = xb_ref[0, c * ch:(c + 1) * ch, :]

    nl = -lam_ref[...]
    softplus = jnp.maximum(nl, 0.0) + jnp.log1p(jnp.exp(-jnp.abs(nl)))
    sub = lax.broadcasted_iota(I32, (1, SUBLANES, D_LRU), 1)
    scr = ((a_f, b_f), (a_b, b_b))

    for c in range(nch):
        t0 = c * ch
        xc = cb_ref[...] + sum(
            xpad[t0 + SUBLANES - 2 + k:t0 + SUBLANES - 2 + k + ch, :] * cw_ref[k:k + 1, :]
            for k in range(CONV_W))
        xcb = xc.astype(BF16)
        pre = [jnp.dot(xcb[:, kb * 256:(kb + 1) * 256], wg_ref[kb], preferred_element_type=F32)
               for kb in range(4)]
        for d in range(2):
            ra = jnp.concatenate([p[:, d * 256:(d + 1) * 256] for p in pre], axis=1)
            gx = jnp.concatenate([p[:, 512 + d * 256:512 + (d + 1) * 256] for p in pre], axis=1)
            r = _sigmoid(ra + ba_ref[d:d + 1, :])
            gi = _sigmoid(gx + bx_ref[d:d + 1, :])
            log_a = (-LRU_C) * r * softplus[d:d + 1, :]
            a = jnp.exp(log_a)
            u = jnp.sqrt(1.0 - a * a) * gi * xc
            a3 = a.reshape(ch // SUBLANES, SUBLANES, D_LRU)
            u3 = u.reshape(ch // SUBLANES, SUBLANES, D_LRU)
            for s in (1, 2, 4):
                if d == 0:
                    a_s = pltpu.roll(a3, s, 1)
                    u_s = pltpu.roll(u3, s, 1)
                    keep = sub >= s
                else:
                    a_s = pltpu.roll(a3, SUBLANES - s, 1)
                    u_s = pltpu.roll(u3, SUBLANES - s, 1)
                    keep = sub < SUBLANES - s
                u3 = u3 + a3 * jnp.where(keep, u_s, 0.0)
                a3 = a3 * jnp.where(keep, a_s, 1.0)
            scr[d][0][t0:t0 + ch, :] = a3.reshape(ch, D_LRU)
            scr[d][1][t0:t0 + ch, :] = u3.reshape(ch, D_LRU)

    ntile = T // SUBLANES

    def fwd(i, carry):
        r0 = pl.multiple_of(i * SUBLANES, SUBLANES)
        h = a_f[pl.ds(r0, SUBLANES), :] * carry + b_f[pl.ds(r0, SUBLANES), :]
        b_f[pl.ds(r0, SUBLANES), :] = h
        return h[SUBLANES - 1:SUBLANES, :]

    def bwd(i, carry):
        r0 = pl.multiple_of((ntile - 1 - i) * SUBLANES, SUBLANES)
        h = a_b[pl.ds(r0, SUBLANES), :] * carry + b_b[pl.ds(r0, SUBLANES), :]
        b_b[pl.ds(r0, SUBLANES), :] = h
        return h[0:1, :]

    fin_ref[0, 0:1, :] = lax.fori_loop(0, ntile, fwd, h0_ref[0, 0:1, :])
    fin_ref[0, 1:2, :] = lax.fori_loop(0, ntile, bwd, h0_ref[0, 1:2, :])

    for c in range(nch):
        sl = slice(c * ch, (c + 1) * ch)
        ya_ref[0, sl, :] = (gg_ref[0, sl, :].astype(F32) * (b_f[sl, :] + b_b[sl, :])).astype(BF16)


def _lru_call(xb, gg, conv_w, conv_b, wg, ba, bx, lam, h0, n_seq, T):
    xb3 = xb.reshape(n_seq, T, D_LRU)
    gg3 = gg.reshape(n_seq, T, D_LRU)
    full = lambda shape: pl.BlockSpec(shape, lambda i: (0,) * len(shape))
    seq = lambda shape: pl.BlockSpec(shape, lambda i: (i,) + (0,) * (len(shape) - 1))
    ya, fin = pl.pallas_call(
        functools.partial(_lru_body, T=T),
        grid=(n_seq,),
        in_specs=[seq((1, T, D_LRU)), seq((1, T, D_LRU)), full((CONV_W, D_LRU)), full((1, D_LRU)),
                  full((4, 256, 1024)), full((2, D_LRU)), full((2, D_LRU)), full((2, D_LRU)),
                  seq((1, 2, D_LRU))],
        out_specs=[seq((1, T, D_LRU)), seq((1, 2, D_LRU))],
        out_shape=[jax.ShapeDtypeStruct((n_seq, T, D_LRU), BF16),
                   jax.ShapeDtypeStruct((n_seq, 2, D_LRU), F32)],
        scratch_shapes=[pltpu.VMEM((T + 2 * SUBLANES, D_LRU), F32)] + [pltpu.VMEM((T, D_LRU), F32)] * 4,
        compiler_params=_cp(("parallel",)), name="lru",
    )(xb3, gg3, conv_w, conv_b, wg, ba, bx, lam, h0)
    return ya.reshape(n_seq * T, D_LRU), fin


def _s5_body(u_ref, wts_ref, wc_ref, ap_ref, h0_ref, y_ref, fin_ref, s_scr, hp_scr, slab,
             *, rt, n_c, with_h0, col_major):
    nseq = rt // n_c
    q = 512
    seq_tok = n_c * S5_L
    halves = seq_tok // (GRID_W * S5_L)

    def tok_rows(s, h, i):
        return pl.ds(s * seq_tok + (h * S5_L + i) * GRID_W, GRID_W)

    def chunk_rows(s, h):
        return pl.ds(s * n_c + h, GRID_W, stride=halves)

    if col_major:
        for i in range(S5_L):
            for s in range(nseq):
                for h in range(halves):
                    slab[i, chunk_rows(s, h), :] = u_ref[0, tok_rows(s, h, i), :]
        ub = jnp.concatenate([slab[i] for i in range(S5_L)], axis=1).astype(BF16)
    else:
        ub = jnp.concatenate([u_ref[0, pl.ds(i, rt, stride=S5_L), :] for i in range(S5_L)],
                             axis=1).astype(BF16)
    r1 = jnp.dot(ub, wts_ref[0], preferred_element_type=F32)
    y_in = r1[:, :S5_L * LANES]
    s_scr[...] = r1[:, S5_L * LANES:]
    ap = ap_ref[0]
    if with_h0:
        for s in range(nseq):
            h0 = h0_ref[0, 0, s:s + 1, :]
            r_f = s * n_c
            r_b = s * n_c + n_c - 1
            ar, ai = ap[0:1, 0:q], ap[0:1, q:2 * q]
            s_scr[r_f:r_f + 1, 0:q] += ar * h0[:, 0:q] - ai * h0[:, q:2 * q]
            s_scr[r_f:r_f + 1, q:2 * q] += ar * h0[:, q:2 * q] + ai * h0[:, 0:q]
            br, bi = ap[0:1, 2 * q:3 * q], ap[0:1, 3 * q:4 * q]
            s_scr[r_b:r_b + 1, 2 * q:3 * q] += br * h0[:, 2 * q:3 * q] - bi * h0[:, 3 * q:4 * q]
            s_scr[r_b:r_b + 1, 3 * q:4 * q] += br * h0[:, 3 * q:4 * q] + bi * h0[:, 2 * q:3 * q]

    cidx = lax.broadcasted_iota(I32, (rt, 1), 0) % n_c
    hfr, hfi = s_scr[:, 0:q], s_scr[:, q:2 * q]
    hbr, hbi = s_scr[:, 2 * q:3 * q], s_scr[:, 3 * q:4 * q]
    lvl = 0
    dist = 1
    while dist < n_c:
        ar, ai = ap[lvl:lvl + 1, 0:q], ap[lvl:lvl + 1, q:2 * q]
        br, bi = ap[lvl:lvl + 1, 2 * q:3 * q], ap[lvl:lvl + 1, 3 * q:4 * q]
        kf = cidx >= dist
        kb = cidx < n_c - dist
        sfr = jnp.where(kf, pltpu.roll(hfr, dist, 0), 0.0)
        sfi = jnp.where(kf, pltpu.roll(hfi, dist, 0), 0.0)
        sbr = jnp.where(kb, pltpu.roll(hbr, rt - dist, 0), 0.0)
        sbi = jnp.where(kb, pltpu.roll(hbi, rt - dist, 0), 0.0)
        hfr, hfi = hfr + (ar * sfr - ai * sfi), hfi + (ar * sfi + ai * sfr)
        hbr, hbi = hbr + (br * sbr - bi * sbi), hbi + (br * sbi + bi * sbr)
        lvl += 1
        dist *= 2

    s_scr[:, 0:q] = hfr
    s_scr[:, q:2 * q] = hfi
    s_scr[:, 2 * q:3 * q] = hbr
    s_scr[:, 3 * q:4 * q] = hbi
    for s in range(nseq):
        fin_ref[0, 0, s:s + 1, 0:2 * q] = s_scr[s * n_c + n_c - 1:s * n_c + n_c, 0:2 * q]
        fin_ref[0, 0, s:s + 1, 2 * q:4 * q] = s_scr[s * n_c:s * n_c + 1, 2 * q:4 * q]

    kf = cidx >= 1
    kb = cidx < n_c - 1
    hp_scr[:, 0:q] = jnp.where(kf, pltpu.roll(hfr, 1, 0), 0.0)
    hp_scr[:, q:2 * q] = jnp.where(kf, pltpu.roll(hfi, 1, 0), 0.0)
    hp_scr[:, 2 * q:3 * q] = jnp.where(kb, pltpu.roll(hbr, rt - 1, 0), 0.0)
    hp_scr[:, 3 * q:4 * q] = jnp.where(kb, pltpu.roll(hbi, rt - 1, 0), 0.0)
    if with_h0:
        for s in range(nseq):
            hp_scr[s * n_c:s * n_c + 1, 0:2 * q] = h0_ref[0, 0, s:s + 1, 0:2 * q]
            hp_scr[s * n_c + n_c - 1:s * n_c + n_c, 2 * q:4 * q] = h0_ref[0, 0, s:s + 1, 2 * q:4 * q]
    y = y_in + jnp.dot(hp_scr[...].astype(BF16), wc_ref[0], preferred_element_type=F32)
    for i in range(S5_L):
        y_i = y[:, i * LANES:(i + 1) * LANES]
        if col_major:
            slab[i] = y_i
            for s in range(nseq):
                for h in range(halves):
                    y_ref[0, tok_rows(s, h, i), :] = slab[i, chunk_rows(s, h), :]
        else:
            y_ref[0, pl.ds(i, rt, stride=S5_L), :] = y_i


def _s5_call(us4, wts, wc, apow, h0, n_seq, T, with_h0, col_major):
    n_c = T // S5_L
    rows = n_seq * n_c
    rt = 256
    nseq_t = rt // n_c
    h0 = h0.reshape(S5_BLK, n_seq // nseq_t, nseq_t, S5_SW)
    y4, fin = pl.pallas_call(
        functools.partial(_s5_body, rt=rt, n_c=n_c, with_h0=with_h0, col_major=col_major),
        grid=(S5_BLK, rows // rt),
        in_specs=[pl.BlockSpec((1, rt * S5_L, LANES), lambda b, r: (b, r, 0)),
                  pl.BlockSpec((1, S5_L * LANES, S5_L * LANES + S5_SW), lambda b, r: (b, 0, 0)),
                  pl.BlockSpec((1, S5_SW, S5_L * LANES), lambda b, r: (b, 0, 0)),
                  pl.BlockSpec((1, 8, S5_SW), lambda b, r: (b, 0, 0)),
                  pl.BlockSpec((1, 1, nseq_t, S5_SW), lambda b, r: (b, r, 0, 0))],
        out_specs=[pl.BlockSpec((1, rt * S5_L, LANES), lambda b, r: (b, r, 0)),
                   pl.BlockSpec((1, 1, nseq_t, S5_SW), lambda b, r: (b, r, 0, 0))],
        out_shape=[jax.ShapeDtypeStruct((S5_BLK, n_seq * T, LANES), F32),
                   jax.ShapeDtypeStruct((S5_BLK, n_seq // nseq_t, nseq_t, S5_SW), F32)],
        scratch_shapes=[pltpu.VMEM((rt, S5_SW), F32), pltpu.VMEM((rt, S5_SW), F32),
                        pltpu.VMEM((S5_L, rt, LANES), F32)],
        compiler_params=_cp(("parallel", "parallel")), name="s5",
    )(us4, wts, wc, apow, h0)
    return y4, fin.reshape(S5_BLK, n_seq, S5_SW)


def _s5_expand_body(toe_ref, sf_ref, sb_ref, cf_ref, cb_ref, wts_ref, wc_ref):
    L, N, P = S5_L, S5_N, S5_P
    w = L * LANES
    r = lax.broadcasted_iota(I32, (LANES, w), 0)
    c = lax.broadcasted_iota(I32, (LANES, w), 1)
    one = lambda m: jnp.where(m, 1.0, 0.0).astype(BF16)
    div = lambda x, d: x >> (d.bit_length() - 1)
    mod = lambda x, d: x & (d - 1)
    t_tap = one((div(r, P) == div(c, LANES)) & (mod(r, P) == mod(c, P)))
    m_tap = div(r, P) == div(mod(c, LANES), P)
    t_st = one((div(r, N) == div(c, 8 * N)) & (mod(r, N) == mod(c, N)))
    m_st = div(r, P) == div(mod(c, 8 * N), N)
    for j in range(L):
        rows = slice(j * LANES, (j + 1) * LANES)
        tap = jnp.dot(toe_ref[0, j].astype(BF16), t_tap, preferred_element_type=F32)
        wts_ref[0, rows, 0:w] = jnp.where(m_tap, tap, 0.0).astype(BF16)
        for k, ref in enumerate((sf_ref, sb_ref)):
            st = jnp.dot(ref[0, j].astype(BF16), t_st, preferred_element_type=F32)
            wts_ref[0, rows, w + k * w:w + (k + 1) * w] = jnp.where(m_st, st, 0.0).astype(BF16)
    rr = lax.broadcasted_iota(I32, (w, LANES), 0)
    cc = lax.broadcasted_iota(I32, (w, LANES), 1)
    t_row = one((div(rr, 8 * N) == div(cc, N)) & (mod(rr, N) == mod(cc, N)))
    r2 = lax.broadcasted_iota(I32, (w, w), 0)
    c2 = lax.broadcasted_iota(I32, (w, w), 1)
    m_row = div(mod(r2, 8 * N), N) == div(mod(c2, LANES), P)
    for k, ref in enumerate((cf_ref, cb_ref)):
        corr = jnp.dot(t_row, ref[0].astype(BF16), preferred_element_type=F32)
        wc_ref[0, k * w:(k + 1) * w, :] = jnp.where(m_row, corr, 0.0).astype(BF16)


def _s5_expand_call(toe, sf, sb, cf, cb):
    L = S5_L
    w = L * LANES
    blk4 = lambda a, b_: pl.BlockSpec((1, L, a, b_), lambda i: (i, 0, 0, 0))
    blk3 = lambda a, b_: pl.BlockSpec((1, a, b_), lambda i: (i, 0, 0))
    return pl.pallas_call(
        _s5_expand_body, grid=(S5_BLK,),
        in_specs=[blk4(LANES, LANES), blk4(LANES, LANES), blk4(LANES, LANES), blk3(LANES, w), blk3(LANES, w)],
        out_specs=[blk3(w, w + S5_SW), blk3(S5_SW, w)],
        out_shape=[jax.ShapeDtypeStruct((S5_BLK, w, w + S5_SW), BF16),
                   jax.ShapeDtypeStruct((S5_BLK, S5_SW, w), BF16)],
        compiler_params=_cp(("parallel",)), name="s5_expand",
    )(toe, sf, sb, cf, cb)


def _s5_weights(lam_re, lam_im, log_step, b_re, b_im, c_re, c_im):
    L, G, N, P = S5_L, S5_G, S5_N, S5_P
    hp = lax.Precision.HIGHEST
    step = jnp.exp(log_step)[..., None]
    lr, li = lam_re, lam_im
    mag = jnp.exp(lr * step)
    ang = li * step
    ar, ai = mag * jnp.cos(ang), mag * jnp.sin(ang)
    den = lr * lr + li * li
    fr = ((ar - 1.0) * lr + ai * li) / den
    fi = (ai * lr - (ar - 1.0) * li) / den
    bbr = fr[..., None] * b_re - fi[..., None] * b_im
    bbi = fr[..., None] * b_im + fi[..., None] * b_re
    pr, pi = [jnp.ones_like(ar)], [jnp.zeros_like(ar)]
    for _ in range(L):
        pr.append(pr[-1] * ar - pi[-1] * ai)
        pi.append(pr[-2] * ai + pi[-1] * ar)
    pr = jnp.stack(pr, axis=2)
    pi = jnp.stack(pi, axis=2)
    bbr_t, bbi_t = bbr.transpose(0, 1, 3, 2), bbi.transpose(0, 1, 3, 2)
    er = pr[:, :, :, None, :] * bbr_t[:, :, None] - pi[:, :, :, None, :] * bbi_t[:, :, None]
    ei = pr[:, :, :, None, :] * bbi_t[:, :, None] + pi[:, :, :, None, :] * bbr_t[:, :, None]
    kt = (jnp.einsum('dgpn,dgkqn->dgkqp', c_re, er, precision=hp)
          - jnp.einsum('dgpn,dgkqn->dgkqp', c_im, ei, precision=hp))
    ii = jnp.arange(L)
    dif = ii[None, :] - ii[:, None]
    tf = jnp.where((dif >= 0)[None, :, :, None, None], kt[0][:, jnp.clip(dif, 0, L)], 0.0)
    tb = jnp.where((dif <= 0)[None, :, :, None, None], kt[1][:, jnp.clip(-dif, 0, L)], 0.0)
    toe = (tf + tb).reshape(S5_BLK, 8, L, L, P, P)
    toe = toe.transpose(0, 2, 1, 4, 3, 5).reshape(S5_BLK, L, LANES, LANES)

    def state_map(d, ks):
        e = jnp.concatenate([er[d][:, ks], ei[d][:, ks]], axis=-1)
        return e.reshape(S5_BLK, 8, L, P, 2 * N).transpose(0, 2, 1, 3, 4).reshape(S5_BLK, L, LANES, 2 * N)

    def corr_map(d, ks):
        p_r, p_i = pr[d][:, ks][:, :, None, :], pi[d][:, ks][:, :, None, :]
        cr, ci = c_re[d][:, None], c_im[d][:, None]
        g = jnp.stack([cr * p_r - ci * p_i, -(cr * p_i + ci * p_r)])
        g = g.reshape(2, S5_BLK, 8, L, P, N).transpose(1, 0, 5, 3, 2, 4)
        return g.reshape(S5_BLK, 2 * N, L * LANES)

    wts, wc = _s5_expand_call(toe, state_map(0, jnp.arange(L - 1, -1, -1)), state_map(1, jnp.arange(L)),
                              corr_map(0, jnp.arange(1, L + 1)), corr_map(1, jnp.arange(L, 0, -1)))

    qr, qi = pr[:, :, L], pi[:, :, L]
    lv = []
    for _ in range(8):
        lv.append(jnp.concatenate([qr[0].reshape(S5_BLK, 8 * N), qi[0].reshape(S5_BLK, 8 * N),
                                   qr[1].reshape(S5_BLK, 8 * N), qi[1].reshape(S5_BLK, 8 * N)], axis=1))
        qr, qi = qr * qr - qi * qi, 2.0 * qr * qi
    apow = jnp.stack(lv, axis=1)
    return wts, wc, apow


def _mixout_body(x_ref, mod_ref, g1_ref, g2_ref, wbg_ref, bbg_ref, ya_ref, y4_ref, u4_ref,
                 dsk_ref, wglu_ref, bglu_ref, wso_ref, wlo_ref, wo_ref, wrt_ref, brt_ref,
                 x1_ref, h2_ref, aff_ref):
    m = mod_ref[0]
    sh1, sc1, gt1 = m[:, 0:D_MODEL], m[:, D_MODEL:2 * D_MODEL], m[:, 2 * D_MODEL:3 * D_MODEL]
    sh2, sc2 = m[:, 3 * D_MODEL:4 * D_MODEL], m[:, 4 * D_MODEL:5 * D_MODEL]
    x = x_ref[...]
    h = _norm_mod(x, g1_ref[...], sc1, sh1)
    gates = _sigmoid(jnp.dot(h.astype(BF16), wbg_ref[...], preferred_element_type=F32) + bbg_ref[...])

    def assemble(ref):
        return jnp.concatenate([ref[k] for k in range(S5_BLK)], axis=1)

    ys = assemble(y4_ref) + dsk_ref[...] * assemble(u4_ref)
    v = _gelu(ys)
    ob = v * _sigmoid(jnp.dot(v.astype(BF16), wglu_ref[...], preferred_element_type=F32) + bglu_ref[...])
    merged = (gates[:, :D_MODEL] * jnp.dot(ya_ref[...], wlo_ref[...], preferred_element_type=F32)
              + gates[:, D_MODEL:] * jnp.dot(ob.astype(BF16), wso_ref[...], preferred_element_type=F32))
    mix = jnp.dot(merged.astype(BF16), wo_ref[...], preferred_element_type=F32)
    x1 = x + gt1 * mix
    x1_ref[...] = x1
    h2 = _norm_mod(x1, g2_ref[...], sc2, sh2)
    h2_ref[...] = h2
    logits = _dot3(wrt_ref[...], h2, dims=(((1,), (1,)), ((), ()))) + brt_ref[...]
    mx = jnp.max(logits, axis=0, keepdims=True)
    ex = jnp.exp(logits - mx)
    aff_ref[...] = ex / jnp.sum(ex, axis=0, keepdims=True)


def _mixout_call(x2d, mod3, g1, g2, wbg, bbg, ya, y4, u4, dsk, wglu, bglu, wso, wlo, wo, wrt, brt,
                 mod_row, tm):
    n = x2d.shape[0]
    full = lambda shape: pl.BlockSpec(shape, lambda i: (0,) * len(shape))
    s5_spec = pl.BlockSpec((S5_BLK, tm, LANES), lambda i: (0, i, 0))
    tok = lambda w: pl.BlockSpec((tm, w), lambda i: (i, 0))
    return pl.pallas_call(
        _mixout_body,
        grid=(n // tm,),
        in_specs=[tok(D_MODEL), pl.BlockSpec((1, 1, 6 * D_MODEL), lambda i: (mod_row(i), 0, 0)),
                  full((1, D_MODEL)), full((1, D_MODEL)), full((D_MODEL, 2 * D_MODEL)),
                  full((1, 2 * D_MODEL)), tok(D_LRU), s5_spec, s5_spec, full((1, D_S5)),
                  full((D_S5, D_S5)), full((1, D_S5)), full((D_S5, D_MODEL)), full((D_LRU, D_MODEL)),
                  full((D_MODEL, D_MODEL)), full((N_EXPERTS, D_MODEL)), full((N_EXPERTS, 1))],
        out_specs=[tok(D_MODEL), tok(D_MODEL), pl.BlockSpec((N_EXPERTS, tm), lambda i: (0, i))],
        out_shape=[jax.ShapeDtypeStruct((n, D_MODEL), F32), jax.ShapeDtypeStruct((n, D_MODEL), F32),
                   jax.ShapeDtypeStruct((N_EXPERTS, n), F32)],
        compiler_params=_cp(("parallel",)), name="mixout",
    )(x2d, mod3, g1, g2, wbg, bbg, ya, y4, u4, dsk, wglu, bglu, wso, wlo, wo, wrt, brt)


def _select_body(aff_ref, gate_ref, pos_ref, offs_ref, *, n_tok, cap):
    aff = aff_ref[...]
    capf = float(cap)

    def bis(_, lh):
        lo, hi = lh
        mid = lo + ((hi - lo + 1) >> 1)
        cnt = jnp.sum(jnp.where(aff >= pltpu.bitcast(mid, F32), 1.0, 0.0), axis=1, keepdims=True)
        ok = cnt >= capf
        return jnp.where(ok, mid, lo), jnp.where(ok, hi, mid - 1)

    lo0 = jnp.zeros((N_EXPERTS, 1), I32)
    hi0 = jnp.full((N_EXPERTS, 1), 0x7F800000, I32)
    thr_bits, _ = lax.fori_loop(0, 31, bis, (lo0, hi0))
    thr = pltpu.bitcast(thr_bits, F32)
    need = capf - jnp.sum(jnp.where(aff > thr, 1.0, 0.0), axis=1, keepdims=True)

    r = lax.broadcasted_iota(I32, (TOK_TILE, TOK_TILE), 0)
    c = lax.broadcasted_iota(I32, (TOK_TILE, TOK_TILE), 1)
    tri = jnp.where(r < c, 1.0, 0.0).astype(BF16)
    lane = lax.broadcasted_iota(I32, (N_EXPERTS, LANES), 1)
    n_tiles = n_tok // TOK_TILE
    run_eq = jnp.zeros((N_EXPERTS, 1), F32)
    run_sel = jnp.zeros((N_EXPERTS, 1), F32)
    offs = jnp.zeros((N_EXPERTS, LANES), F32)
    for t in range(n_tiles):
        sl = slice(t * TOK_TILE, (t + 1) * TOK_TILE)
        aff_t = aff_ref[:, sl]
        is_eq = aff_t == thr
        eq_t = jnp.where(is_eq, 1.0, 0.0)
        rank_eq = jnp.dot(eq_t.astype(BF16), tri, preferred_element_type=F32) + run_eq
        cand = (aff_t > thr) | (is_eq & (rank_eq < need))
        pos_t = jnp.dot(jnp.where(cand, 1.0, 0.0).astype(BF16), tri, preferred_element_type=F32) + run_sel
        sel_t = cand & (pos_t < capf)
        sel_f = jnp.where(sel_t, 1.0, 0.0)
        offs = jnp.where(lane == t, run_sel, offs)
        gate_ref[:, sl] = jnp.where(sel_t, aff_t, 0.0)
        pos_ref[:, sl] = jnp.where(sel_t, pos_t, -1.0).astype(I32)
        run_eq = run_eq + jnp.sum(eq_t, axis=1, keepdims=True)
        run_sel = run_sel + jnp.sum(sel_f, axis=1, keepdims=True)
    offs = jnp.where(lane >= n_tiles, run_sel, offs)
    offs_ref[...] = offs.astype(I32)


def _select_call(aff_t, cap):
    n_tok = aff_t.shape[1]
    full = lambda shape: pl.BlockSpec(shape, lambda i: (0,) * len(shape))
    return pl.pallas_call(
        functools.partial(_select_body, n_tok=n_tok, cap=cap),
        grid=(1,),
        in_specs=[full((N_EXPERTS, n_tok))],
        out_specs=[full((N_EXPERTS, n_tok)), full((N_EXPERTS, n_tok)), full((N_EXPERTS, LANES))],
        out_shape=[jax.ShapeDtypeStruct((N_EXPERTS, n_tok), F32),
                   jax.ShapeDtypeStruct((N_EXPERTS, n_tok), I32),
                   jax.ShapeDtypeStruct((N_EXPERTS, LANES), I32)],
        compiler_params=_cp(("arbitrary",)), name="select",
    )(aff_t)


def _invert_body(offs_ref, pos_ref, gate_ref, idx_ref, gcol_ref, acc_i, acc_g, *, n_tiles, cap):
    base = pl.program_id(0) * LANES
    acc_i[...] = jnp.full(acc_i.shape, -1.0, F32)
    acc_g[...] = jnp.zeros(acc_g.shape, F32)
    jrow = lax.broadcasted_iota(I32, (SLOT_CHUNK, LANES), 0)
    lane = lax.broadcasted_iota(I32, (1, LANES), 1)

    def per_tile(t, _):
        first = (offs_ref[base + t] >> 3) << 3
        prow = pos_ref[0, pl.ds(t, 1), :]
        grow = gate_ref[0, pl.ds(t, 1), :]
        tok = (lane + t * TOK_TILE).astype(F32)

        def window(w, _):
            start = pl.multiple_of(first + w * SLOT_CHUNK, SUBLANES)
            rows = pl.ds(start, SLOT_CHUNK)
            hit = prow == (jrow + start)
            acc_i[rows, :] = jnp.maximum(acc_i[rows, :], jnp.where(hit, tok, -1.0))
            acc_g[rows, :] = jnp.maximum(acc_g[rows, :], jnp.where(hit, grow, 0.0))
            return 0

        n_win = (offs_ref[base + t + 1] - first + SLOT_CHUNK - 1) >> (SLOT_CHUNK.bit_length() - 1)
        lax.fori_loop(0, n_win, window, 0)
        return 0

    lax.fori_loop(0, n_tiles, per_tile, 0)
    idx_ref[0] = jnp.max(acc_i[0:cap, :], axis=1, keepdims=True).astype(I32)
    gcol_ref[0] = jnp.max(acc_g[0:cap, :], axis=1, keepdims=True)


def _invert_call(offs, pos, gate, cap):
    n_tok = pos.shape[1]
    n_tiles = n_tok // TOK_TILE
    pos3 = pos.reshape(N_EXPERTS, n_tiles, TOK_TILE)
    gate3 = gate.reshape(N_EXPERTS, n_tiles, TOK_TILE)
    per_e = lambda shape: pl.BlockSpec(shape, lambda e, o: (e, 0, 0))
    return pl.pallas_call(
        functools.partial(_invert_body, n_tiles=n_tiles, cap=cap),
        grid_spec=pltpu.PrefetchScalarGridSpec(
            num_scalar_prefetch=1, grid=(N_EXPERTS,),
            in_specs=[per_e((1, n_tiles, TOK_TILE)), per_e((1, n_tiles, TOK_TILE))],
            out_specs=[per_e((1, cap, 1)), per_e((1, cap, 1))],
            scratch_shapes=[pltpu.VMEM((cap + SLOT_CHUNK, LANES), F32)] * 2),
        out_shape=[jax.ShapeDtypeStruct((N_EXPERTS, cap, 1), I32),
                   jax.ShapeDtypeStruct((N_EXPERTS, cap, 1), F32)],
        compiler_params=_cp(("parallel",)), name="invert",
    )(offs.reshape(-1), pos3, gate3)


def _ffn_body(idx0_ref, idxn_ref, gcol_ref, wg_ref, wu_ref, wd_ref, h_hbm, out_ref,
              xe, sem, *, cap):
    e = pl.program_id(0)
    slot = e % 2
    nxt = 1 - slot

    def issue_rows(idx_ref, s, j0, n):
        for j in range(j0, j0 + n):
            pltpu.make_async_copy(h_hbm.at[pl.ds(idx_ref[0, 0, j], 1), :],
                                  xe.at[s, pl.ds(j, 1), :], sem.at[s]).start()

    @pl.when(e == 0)
    def _():
        def issue(j, _):
            pltpu.make_async_copy(h_hbm.at[pl.ds(idx0_ref[0, 0, j], 1), :],
                                  xe.at[0, pl.ds(j, 1), :], sem.at[0]).start()
            return 0
        lax.fori_loop(0, cap, issue, 0, unroll=8)

    wgb = wg_ref[0].astype(BF16)
    wub = wu_ref[0].astype(BF16)
    wdb = wd_ref[0].astype(BF16)

    pltpu.make_async_copy(h_hbm.at[pl.ds(0, cap), :], xe.at[slot], sem.at[slot]).wait()

    ch = 256
    for c in range(cap // ch):
        sl = slice(c * ch, (c + 1) * ch)

        issue_rows(idxn_ref, nxt, c * ch, ch)
        xb = xe[slot, sl, :].astype(BF16)
        g = jnp.dot(xb, wgb, preferred_element_type=F32)
        u = jnp.dot(xb, wub, preferred_element_type=F32)
        hid = (g * _sigmoid(g)) * u
        ye = jnp.dot(hid.astype(BF16), wdb, preferred_element_type=F32)
        out_ref[0, sl, :] = ye * gcol_ref[0, sl, :]

    @pl.when(e == N_EXPERTS - 1)
    def _():
        pltpu.make_async_copy(h_hbm.at[pl.ds(0, cap), :], xe.at[nxt], sem.at[nxt]).wait()


def _ffn_call(idx, gcol, w_gate, w_up, w_down, h2, cap):
    d_e = w_gate.shape[2]
    wspec = lambda a, b: pl.BlockSpec((1, a, b), lambda e: (e, 0, 0))
    return pl.pallas_call(
        functools.partial(_ffn_body, cap=cap),
        grid=(N_EXPERTS,),
        in_specs=[pl.BlockSpec((1, 1, cap), lambda e: (0, 0, 0), memory_space=pltpu.SMEM),
                  pl.BlockSpec((1, 1, cap), lambda e: (jnp.minimum(e + 1, N_EXPERTS - 1), 0, 0),
                               memory_space=pltpu.SMEM),
                  pl.BlockSpec((1, cap, 1), lambda e: (e, 0, 0)),
                  wspec(D_MODEL, d_e), wspec(D_MODEL, d_e), wspec(d_e, D_MODEL),
                  pl.BlockSpec(memory_space=pl.ANY)],
        out_specs=pl.BlockSpec((1, cap, D_MODEL), lambda e: (e, 0, 0)),
        out_shape=jax.ShapeDtypeStruct((N_EXPERTS, cap, D_MODEL), F32),
        scratch_shapes=[pltpu.VMEM((2, cap, D_MODEL), F32), pltpu.SemaphoreType.DMA((2,))],
        compiler_params=_cp(("arbitrary",)), name="ffn",
    )(idx, idx, gcol, w_gate, w_up, w_down, h2)


def _combine_body(offs_ref, x1_ref, mod_ref, gf_ref, pos_ref, yb_hbm, y_ref, wins, acc_ref, sem, *, cap):
    t = pl.program_id(0)
    n_t = pl.num_programs(0)
    slot = t % 2

    def kmax_of(tt):
        k = 0
        for e in range(N_EXPERTS):
            k = jnp.maximum(k, offs_ref[e * LANES + tt + 1] - offs_ref[e * LANES + tt])
        return k

    def geometry(tt, e, w, win):
        fetch = win + SUBLANES
        lo = offs_ref[e * LANES + tt] + w * win
        start = pl.multiple_of(jnp.minimum((lo >> 3) << 3, cap - fetch), SUBLANES)
        return lo, start, fetch

    def issue(tt, w, s, win):
        for e in range(N_EXPERTS):
            _, start, fetch = geometry(tt, e, w, win)
            pltpu.make_async_copy(yb_hbm.at[e, pl.ds(start, fetch), :],
                                  wins.at[s, pl.ds(e * fetch, fetch), :], sem.at[s]).start()

    def drain(s, win):
        rows = N_EXPERTS * (win + SUBLANES)
        pltpu.make_async_copy(yb_hbm.at[0, pl.ds(0, rows), :], wins.at[s, pl.ds(0, rows), :],
                              sem.at[s]).wait()

    def expand(w, s, win):
        sub = lax.broadcasted_iota(I32, (win + SUBLANES, TOK_TILE), 0)
        rows = []
        for e in range(N_EXPERTS):
            lo, start, _ = geometry(t, e, w, win)
            pos = pos_ref[e:e + 1, :]
            in_round = jnp.logical_and(pos >= lo, pos < lo + win)
            rows.append(jnp.where(jnp.logical_and(pos - start == sub, in_round), 1.0, 0.0))
        onehot = jnp.concatenate(rows, axis=0).T.astype(BF16)
        data = wins[s, 0:N_EXPERTS * (win + SUBLANES), :]
        hi = data.astype(BF16)
        lo_part = (data - hi.astype(F32)).astype(BF16)
        return (jnp.dot(onehot, hi, preferred_element_type=F32)
                + jnp.dot(onehot, lo_part, preferred_element_type=F32))

    def issue_first(tt, s):
        wide = kmax_of(tt) > COMBINE_WIN

        @pl.when(wide)
        def _():
            issue(tt, 0, s, COMBINE_WIN_WIDE)

        @pl.when(jnp.logical_not(wide))
        def _():
            issue(tt, 0, s, COMBINE_WIN)

    @pl.when(t == 0)
    def _():
        issue_first(0, 0)

    @pl.when(t + 1 < n_t)
    def _():
        issue_first(t + 1, 1 - slot)

    kmax = kmax_of(t)

    @pl.when(kmax > COMBINE_WIN)
    def _():
        drain(slot, COMBINE_WIN_WIDE)
        acc_ref[...] = expand(0, slot, COMBINE_WIN_WIDE)

    @pl.when(kmax <= COMBINE_WIN)
    def _():
        drain(slot, COMBINE_WIN)
        acc_ref[...] = expand(0, slot, COMBINE_WIN)

    def more(w, _):
        issue(t, w, slot, COMBINE_WIN_WIDE)
        drain(slot, COMBINE_WIN_WIDE)
        acc_ref[...] += expand(w, slot, COMBINE_WIN_WIDE)
        return 0
    lax.fori_loop(1, (kmax + COMBINE_WIN_WIDE - 1) >> (COMBINE_WIN_WIDE.bit_length() - 1), more, 0)
    acc = acc_ref[...]

    gt2 = mod_ref[0][:, 5 * D_MODEL:6 * D_MODEL]
    x2 = x1_ref[...] + gt2 * acc
    ms = jnp.mean(x2 * x2, axis=-1, keepdims=True)
    y_ref[...] = x2 * lax.rsqrt(ms + EPS) * gf_ref[...]


def _combine_call(offs, pos, x1, mod3, gf, ybuf, mod_row, cap):
    n = x1.shape[0]
    return pl.pallas_call(
        functools.partial(_combine_body, cap=cap),
        grid_spec=pltpu.PrefetchScalarGridSpec(
            num_scalar_prefetch=1, grid=(n // TOK_TILE,),
            in_specs=[pl.BlockSpec((TOK_TILE, D_MODEL), lambda i, o: (i, 0)),
                      pl.BlockSpec((1, 1, 6 * D_MODEL), lambda i, o: (mod_row(i), 0, 0)),
                      pl.BlockSpec((1, D_MODEL), lambda i, o: (0, 0)),
                      pl.BlockSpec((N_EXPERTS, TOK_TILE), lambda i, o: (0, i)),
                      pl.BlockSpec(memory_space=pl.ANY)],
            out_specs=pl.BlockSpec((TOK_TILE, D_MODEL), lambda i, o: (i, 0)),
            scratch_shapes=[pltpu.VMEM((2, N_EXPERTS * (COMBINE_WIN_WIDE + SUBLANES), D_MODEL), F32),
                            pltpu.VMEM((TOK_TILE, D_MODEL), F32),
                            pltpu.SemaphoreType.DMA((2,))]),
        out_shape=jax.ShapeDtypeStruct((n, D_MODEL), F32),
        compiler_params=_cp(("arbitrary",)), name="combine",
    )(offs.reshape(-1), x1, mod3, gf, pos, ybuf)


def _lru_gate_weights(wa, wx):
    eye = jnp.eye(4, dtype=F32)

    def bd(w):
        w5 = w.reshape(4, 4, LRU_HEAD_DIM, LRU_HEAD_DIM)
        return jnp.einsum('khij,hg->khigj', w5, eye).reshape(4, 256, 256)

    return jnp.concatenate([bd(wa[0]), bd(wa[1]), bd(wx[0]), bd(wx[1])], axis=2).astype(BF16)


def kernel(x_prompt, x_sample, state_lru, state_s5_re, state_s5_im, c, c_ctx, w_mod, b_mod, g_norm1, g_norm2, w_in, conv_w, conv_b, lru_wa, lru_ba, lru_wx, lru_bx, lru_lambda, s5_lambda_re, s5_lambda_im, s5_log_step, s5_b_re, s5_b_im, s5_c_re, s5_c_im, s5_d, s5_w_glu, s5_b_glu, w_lru_out, w_s5_out, w_branch_gate, b_branch_gate, w_o, w_router, b_router, w_e_gate, w_e_up, w_e_down, g_final):
    bp, tp, _ = x_prompt.shape
    bs, ts, _ = x_sample.shape
    n_p, n_s = bp * tp, bs * ts
    l = 0

    c_all = jnp.zeros((16, D_MODEL), F32).at[0].set(c_ctx).at[1:1 + bs].set(c)
    mod3 = _mod_call(c_all, w_mod[l], b_mod[l][None, :]).reshape(16, 1, 6 * D_MODEL)

    w_in_b = w_in[l].astype(BF16)
    w_main, w_s5in = w_in_b[:, :2 * D_LRU], w_in_b[:, 2 * D_LRU:]
    g1 = g_norm1[l][None, :]
    g2 = g_norm2[l][None, :]
    wg = _lru_gate_weights(lru_wa[l], lru_wx[l])
    wts, wc, apow = _s5_weights(s5_lambda_re[l], s5_lambda_im[l], s5_log_step[l],
                                s5_b_re[l], s5_b_im[l], s5_c_re[l], s5_c_im[l])
    wbg = w_branch_gate[l].astype(BF16)
    bbg = b_branch_gate[l][None, :]
    wglu = s5_w_glu[l].astype(BF16)
    wso = w_s5_out[l].astype(BF16)
    wlo = w_lru_out[l].astype(BF16)
    wo = w_o[l].astype(BF16)
    wrt = w_router[l].T
    brt = b_router[l][:, None]

    xp2 = x_prompt.reshape(n_p, D_MODEL)
    xs2 = x_sample.reshape(n_s, D_MODEL)
    ctx_row = lambda i: 0
    tm_in = 512
    lat_row_in = lambda i: 1 + i // (ts // tm_in)

    def s5_state(re, im):
        def part(a, d):
            return a[:, d].reshape(-1, S5_BLK, 8 * S5_N).transpose(1, 0, 2)
        return jnp.concatenate([part(re, 0), part(im, 0), part(re, 1), part(im, 1)], axis=2)

    xb_p, gg_p, us4_p = _inproj_call(xp2, mod3, g1, w_main, w_s5in, ctx_row, tm_in)
    ya_p, lru_fin = _lru_call(xb_p, gg_p, conv_w[l], conv_b[l][None, :], wg, lru_ba[l], lru_bx[l],
                              lru_lambda[l], jnp.zeros((bp, 2, D_LRU), F32), bp, tp)
    y4_p, s5_fin = _s5_call(us4_p, wts, wc, apow, jnp.zeros((S5_BLK, bp, S5_SW), F32), bp, tp,
                            with_h0=False, col_major=False)
    xb_s, gg_s, us4_s = _inproj_call(xs2, mod3, g1, w_main, w_s5in, lat_row_in, tm_in)
    ya_s, _ = _lru_call(xb_s, gg_s, conv_w[l], conv_b[l][None, :], wg, lru_ba[l], lru_bx[l],
                        lru_lambda[l], state_lru[:, l], bs, ts)
    y4_s, _ = _s5_call(us4_s, wts, wc, apow, s5_state(state_s5_re[:, l], state_s5_im[:, l]), bs, ts,
                       with_h0=True, col_major=True)

    mo_args = (wbg, bbg)
    mo_tail = (s5_d[l][None, :], wglu, s5_b_glu[l][None, :], wso, wlo, wo, wrt, brt)
    tm_mo = 512
    x1_p, h2_p, aff_p = _mixout_call(xp2, mod3, g1, g2, *mo_args, ya_p, y4_p, us4_p, *mo_tail,
                                     mod_row=ctx_row, tm=tm_mo)
    x1_s, h2_s, aff_s = _mixout_call(xs2, mod3, g1, g2, *mo_args, ya_s, y4_s, us4_s, *mo_tail,
                                     mod_row=lambda i: 1 + i // (ts // tm_mo), tm=tm_mo)

    cap = (CAPACITY_FACTOR * n_p) // N_EXPERTS
    gf = g_final[None, :]
    ys = []
    for aff, h2, x1, mod_row in ((aff_p, h2_p, x1_p, ctx_row),
                                 (aff_s, h2_s, x1_s, lambda i: 1 + i // (ts // TOK_TILE))):
        gate, pos, offs = _select_call(aff, cap)
        idx, gcol = _invert_call(offs, pos, gate, cap)
        ybuf = _ffn_call(idx.reshape(N_EXPERTS, 1, cap), gcol, w_e_gate[l], w_e_up[l], w_e_down[l], h2, cap)
        ys.append(_combine_call(offs, pos, x1, mod3, gf, ybuf, mod_row, cap))
    y_p, y_s = ys

    new_lru = lru_fin[:, None]
    sf = s5_fin.transpose(1, 0, 2).reshape(bp, S5_BLK, 2, 2, 8, S5_N)
    sf = sf.transpose(0, 2, 3, 1, 4, 5).reshape(bp, 2, 2, S5_G, S5_N)
    new_s5r = sf[:, :, 0][:, None]
    new_s5i = sf[:, :, 1][:, None]
    return (y_p.reshape(bp, tp, D_MODEL), y_s.reshape(bs, ts, D_MODEL), new_lru, new_s5r, new_s5i)
```

```python
import functools
import math

import jax
import jax.numpy as jnp
from jax import lax
from jax.experimental import pallas as pl
from jax.experimental.pallas import tpu as pltpu

F32 = jnp.float32
BF16 = jnp.bfloat16
I32 = jnp.int32

D_MODEL = 1024
D_LRU = 1024
LRU_HEADS = 16
LRU_HEAD_DIM = 64
LRU_C = 8.0
CONV_W = 4
D_S5 = 512
S5_P = 16
S5_G = 32
S5_N = 64
GRID_W = 64
N_EXPERTS = 16
CAPACITY_FACTOR = 2
EPS = 1e-6

LANES = 128
SUBLANES = 8
S5_L = 8
S5_BLK = 4
S5_SW = 4 * 512
TOK_TILE = 128
SLOT_CHUNK = 64
COMBINE_WIN = 32
COMBINE_WIN_WIDE = 64
VMEM_LIMIT = 56 * 1024 * 1024


def _cp(sem, vmem=VMEM_LIMIT):
    return pltpu.CompilerParams(dimension_semantics=sem, vmem_limit_bytes=vmem)


def _bdot(a, b):
    return jnp.dot(a.astype(BF16), b.astype(BF16), preferred_element_type=F32)


def _split2(a):
    hi = a.astype(BF16)
    lo = (a - hi.astype(F32)).astype(BF16)
    return hi, lo


def _dot3(a, b, dims=(((1,), (0,)), ((), ()))):
    ah, al = _split2(a)
    bh, bl = _split2(b)
    d = functools.partial(lax.dot_general, dimension_numbers=dims, preferred_element_type=F32)
    return d(ah, bh) + (d(al, bh) + d(ah, bl))


def _sigmoid(x):
    return 1.0 / (1.0 + jnp.exp(-x))


def _gelu(x):
    c = math.sqrt(2.0 / math.pi)
    return 0.5 * x * (1.0 + jnp.tanh(c * (x + 0.044715 * (x * x * x))))


def _norm_mod(x, g, scale, shift):
    ms = jnp.mean(x * x, axis=-1, keepdims=True)
    return (x * lax.rsqrt(ms + EPS) * g) * (1.0 + scale) + shift


def _mod_body(c_ref, w_ref, b_ref, o_ref):
    c = c_ref[...]
    s = c * _sigmoid(c)
    o_ref[...] = _dot3(s, w_ref[...]) + b_ref[...]


def _mod_call(c_all, w_mod, b_mod):
    n = w_mod.shape[1]
    tn = 1536
    return pl.pallas_call(
        _mod_body,
        grid=(n // tn,),
        in_specs=[pl.BlockSpec((16, D_MODEL), lambda j: (0, 0)),
                  pl.BlockSpec((D_MODEL, tn), lambda j: (0, j)),
                  pl.BlockSpec((1, tn), lambda j: (0, j))],
        out_specs=pl.BlockSpec((16, tn), lambda j: (0, j)),
        out_shape=jax.ShapeDtypeStruct((16, n), F32),
        compiler_params=_cp(("arbitrary",)),
        name="mod",
    )(c_all, w_mod, b_mod)


def _inproj_body(x_ref, mod_ref, g_ref, w_ref, ws_ref, xb_ref, gg_ref, us_ref, *, seq_len):
    m = mod_ref[0]
    h = _norm_mod(x_ref[...], g_ref[...], m[:, D_MODEL:2 * D_MODEL], m[:, 0:D_MODEL])
    hb = h.astype(BF16)
    z = jnp.dot(hb, w_ref[...], preferred_element_type=F32)
    seg = seq_len // SUBLANES
    for s in range(x_ref.shape[0] // seq_len):
        for k in range(SUBLANES):
            t0 = s * seq_len + k * seg
            for c in range(D_LRU // LANES):
                xb_ref[c, pl.ds(s * seq_len + k, seg, stride=SUBLANES), :] = (
                    z[t0:t0 + seg, c * LANES:(c + 1) * LANES])
    gg_ref[...] = _gelu(z[:, D_LRU:]).astype(BF16)
    us = jnp.dot(hb, ws_ref[...], preferred_element_type=F32)
    for k in range(S5_BLK):
        us_ref[k] = us[:, k * LANES:(k + 1) * LANES]


def _inproj_call(x2d, mod3, g1, w_main, w_s5, mod_row, tm, seq_len):
    n = x2d.shape[0]
    return pl.pallas_call(
        functools.partial(_inproj_body, seq_len=seq_len), grid=(n // tm,),
        in_specs=[pl.BlockSpec((tm, D_MODEL), lambda i: (i, 0)),
                  pl.BlockSpec((1, 1, 6 * D_MODEL), lambda i: (mod_row(i), 0, 0)),
                  pl.BlockSpec((1, D_MODEL), lambda i: (0, 0)),
                  pl.BlockSpec((D_MODEL, 2 * D_LRU), lambda i: (0, 0)),
                  pl.BlockSpec((D_MODEL, D_S5), lambda i: (0, 0))],
        out_specs=[pl.BlockSpec((D_LRU // LANES, tm, LANES), lambda i: (0, i, 0)),
                   pl.BlockSpec((tm, D_LRU), lambda i: (i, 0)),
                   pl.BlockSpec((S5_BLK, tm, LANES), lambda i: (0, i, 0))],
        out_shape=[jax.ShapeDtypeStruct((D_LRU // LANES, n, LANES), F32),
                   jax.ShapeDtypeStruct((n, D_LRU), BF16),
                   jax.ShapeDtypeStruct((S5_BLK, n, LANES), F32)],
        compiler_params=_cp(("parallel",)), name="inproj",
    )(x2d, mod3, g1, w_main, w_s5)


def _lru_body(xb_ref, gg_ref, cw_ref, cb_ref, wg_ref, ba_ref, bx_ref, lam_ref, h0_ref,
              ya_ref, fin_ref, xpad, a_f, b_f, a_b, b_b, hs, *, T):
    seg = T // SUBLANES
    ch = 256
    nch = T // ch
    n_slab = D_LRU // LANES
    pad = 2 * SUBLANES
    sub = lax.broadcasted_iota(I32, (SUBLANES, D_LRU), 0)

    def rows_of(r0, n):
        return jnp.concatenate([xb_ref[c, r0:r0 + n, :] for c in range(n_slab)], axis=1)

    def from_prev_segment(tile):
        return jnp.where(sub >= 1, pltpu.roll(tile, 1, 0), 0.0)

    def from_next_segment(tile):
        return jnp.where(sub < SUBLANES - 1, pltpu.roll(tile, SUBLANES - 1, 0), 0.0)

    xpad[0:SUBLANES, :] = from_prev_segment(rows_of((seg - 2) * SUBLANES, SUBLANES))
    xpad[SUBLANES:pad, :] = from_prev_segment(rows_of((seg - 1) * SUBLANES, SUBLANES))
    xpad[pad + T:pad + T + SUBLANES, :] = from_next_segment(rows_of(0, SUBLANES))
    for c in range(nch):
        xpad[pad + c * ch:pad + (c + 1) * ch, :] = rows_of(c * ch, ch)

    nl = -lam_ref[...]
    softplus = jnp.maximum(nl, 0.0) + jnp.log1p(jnp.exp(-jnp.abs(nl)))
    scr = ((a_f, b_f), (a_b, b_b))

    for c in range(nch):
        t0 = c * ch
        xc = cb_ref[...] + sum(
            xpad[t0 + k * SUBLANES:t0 + k * SUBLANES + ch, :] * cw_ref[k:k + 1, :]
            for k in range(CONV_W))
        xcb = xc.astype(BF16)
        pre = [jnp.dot(xcb[:, kb * 256:(kb + 1) * 256], wg_ref[kb], preferred_element_type=F32)
               for kb in range(4)]
        for d in range(2):
            ra = jnp.concatenate([p[:, d * 256:(d + 1) * 256] for p in pre], axis=1)
            gx = jnp.concatenate([p[:, 512 + d * 256:512 + (d + 1) * 256] for p in pre], axis=1)
            r = _sigmoid(ra + ba_ref[d:d + 1, :])
            gi = _sigmoid(gx + bx_ref[d:d + 1, :])
            log_a = (-LRU_C) * r * softplus[d:d + 1, :]
            a = jnp.exp(log_a)
            scr[d][0][t0:t0 + ch, :] = a
            scr[d][1][t0:t0 + ch, :] = jnp.sqrt(1.0 - a * a) * gi * xc

    def local_scan(a_ref, u_ref, reverse):
        def step(i, carry):
            h, p = carry
            r0 = pl.multiple_of((seg - 1 - i if reverse else i) * SUBLANES, SUBLANES)
            a = a_ref[pl.ds(r0, SUBLANES), :]
            h = a * h + u_ref[pl.ds(r0, SUBLANES), :]
            p = a * p
            u_ref[pl.ds(r0, SUBLANES), :] = h
            a_ref[pl.ds(r0, SUBLANES), :] = p
            return h, p
        init = (jnp.zeros((SUBLANES, D_LRU), F32), jnp.ones((SUBLANES, D_LRU), F32))
        return lax.fori_loop(0, seg, step, init, unroll=2)

    def carry_in(h_end, p_end, h0_row, reverse):
        edge = sub == (SUBLANES - 1 if reverse else 0)
        shift = SUBLANES - 1 if reverse else 1
        c = jnp.broadcast_to(h0_row, (SUBLANES, D_LRU))
        for _ in range(SUBLANES - 1):
            c = jnp.where(edge, h0_row, pltpu.roll(h_end + p_end * c, shift, 0))
        return c

    hf_end, pf_end = local_scan(a_f, b_f, False)
    hb_end, pb_end = local_scan(a_b, b_b, True)
    c_f = carry_in(hf_end, pf_end, h0_ref[0, 0:1, :], False)
    c_b = carry_in(hb_end, pb_end, h0_ref[0, 1:2, :], True)
    fin_ref[0, 0:1, :] = (hf_end + pf_end * c_f)[SUBLANES - 1:SUBLANES, :]
    fin_ref[0, 1:2, :] = (hb_end + pb_end * c_b)[0:1, :]

    for c in range(nch):
        sl = slice(c * ch, (c + 1) * ch)
        tile3 = (ch // SUBLANES, SUBLANES, D_LRU)
        hsum = ((b_f[sl, :].reshape(tile3) + a_f[sl, :].reshape(tile3) * c_f[None])
                + (b_b[sl, :].reshape(tile3) + a_b[sl, :].reshape(tile3) * c_b[None])).reshape(ch, D_LRU)
        for j in range(n_slab):
            hs[j, sl, :] = hsum[:, j * LANES:(j + 1) * LANES]

    for k in range(SUBLANES):
        rows = slice(k * seg, (k + 1) * seg)
        h_seg = jnp.concatenate([hs[j, pl.ds(k, seg, stride=SUBLANES), :] for j in range(n_slab)], axis=1)
        ya_ref[0, rows, :] = (gg_ref[0, rows, :].astype(F32) * h_seg).astype(BF16)


def _lru_call(xb8, gg, conv_w, conv_b, wg, ba, bx, lam, h0, n_seq, T):
    gg3 = gg.reshape(n_seq, T, D_LRU)
    n_slab = D_LRU // LANES
    full = lambda shape: pl.BlockSpec(shape, lambda i: (0,) * len(shape))
    seq = lambda shape: pl.BlockSpec(shape, lambda i: (i,) + (0,) * (len(shape) - 1))
    ya, fin = pl.pallas_call(
        functools.partial(_lru_body, T=T),
        grid=(n_seq,),
        in_specs=[pl.BlockSpec((n_slab, T, LANES), lambda i: (0, i, 0)), seq((1, T, D_LRU)),
                  full((CONV_W, D_LRU)), full((1, D_LRU)),
                  full((4, 256, 1024)), full((2, D_LRU)), full((2, D_LRU)), full((2, D_LRU)),
                  seq((1, 2, D_LRU))],
        out_specs=[seq((1, T, D_LRU)), seq((1, 2, D_LRU))],
        out_shape=[jax.ShapeDtypeStruct((n_seq, T, D_LRU), BF16),
                   jax.ShapeDtypeStruct((n_seq, 2, D_LRU), F32)],
        scratch_shapes=[pltpu.VMEM((T + 3 * SUBLANES, D_LRU), F32)] + [pltpu.VMEM((T, D_LRU), F32)] * 4
                       + [pltpu.VMEM((n_slab, T, LANES), F32)],
        compiler_params=_cp(("parallel",)), name="lru",
    )(xb8, gg3, conv_w, conv_b, wg, ba, bx, lam, h0)
    return ya.reshape(n_seq * T, D_LRU), fin


def _s5_body(u_ref, wts_ref, wc_ref, ap_ref, h0_ref, y_ref, fin_ref, s_scr, hp_scr, slab,
             *, rt, n_c, with_h0, col_major):
    nseq = rt // n_c
    q = 512
    seq_tok = n_c * S5_L
    halves = seq_tok // (GRID_W * S5_L)

    def tok_rows(s, h, i):
        return pl.ds(s * seq_tok + (h * S5_L + i) * GRID_W, GRID_W)

    def chunk_rows(s, h):
        return pl.ds(s * n_c + h, GRID_W, stride=halves)

    if col_major:
        for i in range(S5_L):
            for s in range(nseq):
                for h in range(halves):
                    slab[i, chunk_rows(s, h), :] = u_ref[0, tok_rows(s, h, i), :]
        ub = jnp.concatenate([slab[i] for i in range(S5_L)], axis=1).astype(BF16)
    else:
        ub = jnp.concatenate([u_ref[0, pl.ds(i, rt, stride=S5_L), :] for i in range(S5_L)],
                             axis=1).astype(BF16)
    r1 = jnp.dot(ub, wts_ref[0], preferred_element_type=F32)
    y_in = r1[:, :S5_L * LANES]
    s_scr[...] = r1[:, S5_L * LANES:]
    ap = ap_ref[0]
    if with_h0:
        for s in range(nseq):
            h0 = h0_ref[0, 0, s:s + 1, :]
            r_f = s * n_c
            r_b = s * n_c + n_c - 1
            ar, ai = ap[0:1, 0:q], ap[0:1, q:2 * q]
            s_scr[r_f:r_f + 1, 0:q] += ar * h0[:, 0:q] - ai * h0[:, q:2 * q]
            s_scr[r_f:r_f + 1, q:2 * q] += ar * h0[:, q:2 * q] + ai * h0[:, 0:q]
            br, bi = ap[0:1, 2 * q:3 * q], ap[0:1, 3 * q:4 * q]
            s_scr[r_b:r_b + 1, 2 * q:3 * q] += br * h0[:, 2 * q:3 * q] - bi * h0[:, 3 * q:4 * q]
            s_scr[r_b:r_b + 1, 3 * q:4 * q] += br * h0[:, 3 * q:4 * q] + bi * h0[:, 2 * q:3 * q]

    cidx = lax.broadcasted_iota(I32, (rt, 1), 0) % n_c
    hfr, hfi = s_scr[:, 0:q], s_scr[:, q:2 * q]
    hbr, hbi = s_scr[:, 2 * q:3 * q], s_scr[:, 3 * q:4 * q]
    lvl = 0
    dist = 1
    while dist < n_c:
        ar, ai = ap[lvl:lvl + 1, 0:q], ap[lvl:lvl + 1, q:2 * q]
        br, bi = ap[lvl:lvl + 1, 2 * q:3 * q], ap[lvl:lvl + 1, 3 * q:4 * q]
        kf = cidx >= dist
        kb = cidx < n_c - dist
        sfr = jnp.where(kf, pltpu.roll(hfr, dist, 0), 0.0)
        sfi = jnp.where(kf, pltpu.roll(hfi, dist, 0), 0.0)
        sbr = jnp.where(kb, pltpu.roll(hbr, rt - dist, 0), 0.0)
        sbi = jnp.where(kb, pltpu.roll(hbi, rt - dist, 0), 0.0)
        hfr, hfi = hfr + (ar * sfr - ai * sfi), hfi + (ar * sfi + ai * sfr)
        hbr, hbi = hbr + (br * sbr - bi * sbi), hbi + (br * sbi + bi * sbr)
        lvl += 1
        dist *= 2

    s_scr[:, 0:q] = hfr
    s_scr[:, q:2 * q] = hfi
    s_scr[:, 2 * q:3 * q] = hbr
    s_scr[:, 3 * q:4 * q] = hbi
    for s in range(nseq):
        fin_ref[0, 0, s:s + 1, 0:2 * q] = s_scr[s * n_c + n_c - 1:s * n_c + n_c, 0:2 * q]
        fin_ref[0, 0, s:s + 1, 2 * q:4 * q] = s_scr[s * n_c:s * n_c + 1, 2 * q:4 * q]

    kf = cidx >= 1
    kb = cidx < n_c - 1
    hp_scr[:, 0:q] = jnp.where(kf, pltpu.roll(hfr, 1, 0), 0.0)
    hp_scr[:, q:2 * q] = jnp.where(kf, pltpu.roll(hfi, 1, 0), 0.0)
    hp_scr[:, 2 * q:3 * q] = jnp.where(kb, pltpu.roll(hbr, rt - 1, 0), 0.0)
    hp_scr[:, 3 * q:4 * q] = jnp.where(kb, pltpu.roll(hbi, rt - 1, 0), 0.0)
    if with_h0:
        for s in range(nseq):
            hp_scr[s * n_c:s * n_c + 1, 0:2 * q] = h0_ref[0, 0, s:s + 1, 0:2 * q]
            hp_scr[s * n_c + n_c - 1:s * n_c + n_c, 2 * q:4 * q] = h0_ref[0, 0, s:s + 1, 2 * q:4 * q]
    y = y_in + jnp.dot(hp_scr[...].astype(BF16), wc_ref[0], preferred_element_type=F32)
    for i in range(S5_L):
        y_i = y[:, i * LANES:(i + 1) * LANES]
        if col_major:
            slab[i] = y_i
            for s in range(nseq):
                for h in range(halves):
                    y_ref[0, tok_rows(s, h, i), :] = slab[i, chunk_rows(s, h), :]
        else:
            y_ref[0, pl.ds(i, rt, stride=S5_L), :] = y_i


def _s5_call(us4, wts, wc, apow, h0, n_seq, T, with_h0, col_major):
    n_c = T // S5_L
    rows = n_seq * n_c
    rt = 256
    nseq_t = rt // n_c
    h0 = h0.reshape(S5_BLK, n_seq // nseq_t, nseq_t, S5_SW)
    y4, fin = pl.pallas_call(
        functools.partial(_s5_body, rt=rt, n_c=n_c, with_h0=with_h0, col_major=col_major),
        grid=(S5_BLK, rows // rt),
        in_specs=[pl.BlockSpec((1, rt * S5_L, LANES), lambda b, r: (b, r, 0)),
                  pl.BlockSpec((1, S5_L * LANES, S5_L * LANES + S5_SW), lambda b, r: (b, 0, 0)),
                  pl.BlockSpec((1, S5_SW, S5_L * LANES), lambda b, r: (b, 0, 0)),
                  pl.BlockSpec((1, 8, S5_SW), lambda b, r: (b, 0, 0)),
                  pl.BlockSpec((1, 1, nseq_t, S5_SW), lambda b, r: (b, r, 0, 0))],
        out_specs=[pl.BlockSpec((1, rt * S5_L, LANES), lambda b, r: (b, r, 0)),
                   pl.BlockSpec((1, 1, nseq_t, S5_SW), lambda b, r: (b, r, 0, 0))],
        out_shape=[jax.ShapeDtypeStruct((S5_BLK, n_seq * T, LANES), F32),
                   jax.ShapeDtypeStruct((S5_BLK, n_seq // nseq_t, nseq_t, S5_SW), F32)],
        scratch_shapes=[pltpu.VMEM((rt, S5_SW), F32), pltpu.VMEM((rt, S5_SW), F32),
                        pltpu.VMEM((S5_L, rt, LANES), F32)],
        compiler_params=_cp(("parallel", "parallel")), name="s5",
    )(us4, wts, wc, apow, h0)
    return y4, fin.reshape(S5_BLK, n_seq, S5_SW)


def _s5_expand_body(toe_ref, sf_ref, sb_ref, cf_ref, cb_ref, wts_ref, wc_ref):
    L, N, P = S5_L, S5_N, S5_P
    w = L * LANES
    r = lax.broadcasted_iota(I32, (LANES, w), 0)
    c = lax.broadcasted_iota(I32, (LANES, w), 1)
    one = lambda m: jnp.where(m, 1.0, 0.0).astype(BF16)
    div = lambda x, d: x >> (d.bit_length() - 1)
    mod = lambda x, d: x & (d - 1)
    t_tap = one((div(r, P) == div(c, LANES)) & (mod(r, P) == mod(c, P)))
    m_tap = div(r, P) == div(mod(c, LANES), P)
    t_st = one((div(r, N) == div(c, 8 * N)) & (mod(r, N) == mod(c, N)))
    m_st = div(r, P) == div(mod(c, 8 * N), N)
    for j in range(L):
        rows = slice(j * LANES, (j + 1) * LANES)
        tap = jnp.dot(toe_ref[0, j].astype(BF16), t_tap, preferred_element_type=F32)
        wts_ref[0, rows, 0:w] = jnp.where(m_tap, tap, 0.0).astype(BF16)
        for k, ref in enumerate((sf_ref, sb_ref)):
            st = jnp.dot(ref[0, j].astype(BF16), t_st, preferred_element_type=F32)
            wts_ref[0, rows, w + k * w:w + (k + 1) * w] = jnp.where(m_st, st, 0.0).astype(BF16)
    rr = lax.broadcasted_iota(I32, (w, LANES), 0)
    cc = lax.broadcasted_iota(I32, (w, LANES), 1)
    t_row = one((div(rr, 8 * N) == div(cc, N)) & (mod(rr, N) == mod(cc, N)))
    r2 = lax.broadcasted_iota(I32, (w, w), 0)
    c2 = lax.broadcasted_iota(I32, (w, w), 1)
    m_row = div(mod(r2, 8 * N), N) == div(mod(c2, LANES), P)
    for k, ref in enumerate((cf_ref, cb_ref)):
        corr = jnp.dot(t_row, ref[0].astype(BF16), preferred_element_type=F32)
        wc_ref[0, k * w:(k + 1) * w, :] = jnp.where(m_row, corr, 0.0).astype(BF16)


def _s5_expand_call(toe, sf, sb, cf, cb):
    L = S5_L
    w = L * LANES
    blk4 = lambda a, b_: pl.BlockSpec((1, L, a, b_), lambda i: (i, 0, 0, 0))
    blk3 = lambda a, b_: pl.BlockSpec((1, a, b_), lambda i: (i, 0, 0))
    return pl.pallas_call(
        _s5_expand_body, grid=(S5_BLK,),
        in_specs=[blk4(LANES, LANES), blk4(LANES, LANES), blk4(LANES, LANES), blk3(LANES, w), blk3(LANES, w)],
        out_specs=[blk3(w, w + S5_SW), blk3(S5_SW, w)],
        out_shape=[jax.ShapeDtypeStruct((S5_BLK, w, w + S5_SW), BF16),
                   jax.ShapeDtypeStruct((S5_BLK, S5_SW, w), BF16)],
        compiler_params=_cp(("parallel",)), name="s5_expand",
    )(toe, sf, sb, cf, cb)


def _s5_weights(lam_re, lam_im, log_step, b_re, b_im, c_re, c_im):
    L, G, N, P = S5_L, S5_G, S5_N, S5_P
    hp = lax.Precision.HIGHEST
    step = jnp.exp(log_step)[..., None]
    lr, li = lam_re, lam_im
    mag = jnp.exp(lr * step)
    ang = li * step
    ar, ai = mag * jnp.cos(ang), mag * jnp.sin(ang)
    den = lr * lr + li * li
    fr = ((ar - 1.0) * lr + ai * li) / den
    fi = (ai * lr - (ar - 1.0) * li) / den
    bbr = fr[..., None] * b_re - fi[..., None] * b_im
    bbi = fr[..., None] * b_im + fi[..., None] * b_re
    pr, pi = [jnp.ones_like(ar)], [jnp.zeros_like(ar)]
    for _ in range(L):
        pr.append(pr[-1] * ar - pi[-1] * ai)
        pi.append(pr[-2] * ai + pi[-1] * ar)
    pr = jnp.stack(pr, axis=2)
    pi = jnp.stack(pi, axis=2)
    bbr_t, bbi_t = bbr.transpose(0, 1, 3, 2), bbi.transpose(0, 1, 3, 2)
    er = pr[:, :, :, None, :] * bbr_t[:, :, None] - pi[:, :, :, None, :] * bbi_t[:, :, None]
    ei = pr[:, :, :, None, :] * bbi_t[:, :, None] + pi[:, :, :, None, :] * bbr_t[:, :, None]
    kt = (jnp.einsum('dgpn,dgkqn->dgkqp', c_re, er, precision=hp)
          - jnp.einsum('dgpn,dgkqn->dgkqp', c_im, ei, precision=hp))
    ii = jnp.arange(L)
    dif = ii[None, :] - ii[:, None]
    tf = jnp.where((dif >= 0)[None, :, :, None, None], kt[0][:, jnp.clip(dif, 0, L)], 0.0)
    tb = jnp.where((dif <= 0)[None, :, :, None, None], kt[1][:, jnp.clip(-dif, 0, L)], 0.0)
    toe = (tf + tb).reshape(S5_BLK, 8, L, L, P, P)
    toe = toe.transpose(0, 2, 1, 4, 3, 5).reshape(S5_BLK, L, LANES, LANES)

    def state_map(d, ks):
        e = jnp.concatenate([er[d][:, ks], ei[d][:, ks]], axis=-1)
        return e.reshape(S5_BLK, 8, L, P, 2 * N).transpose(0, 2, 1, 3, 4).reshape(S5_BLK, L, LANES, 2 * N)

    def corr_map(d, ks):
        p_r, p_i = pr[d][:, ks][:, :, None, :], pi[d][:, ks][:, :, None, :]
        cr, ci = c_re[d][:, None], c_im[d][:, None]
        g = jnp.stack([cr * p_r - ci * p_i, -(cr * p_i + ci * p_r)])
        g = g.reshape(2, S5_BLK, 8, L, P, N).transpose(1, 0, 5, 3, 2, 4)
        return g.reshape(S5_BLK, 2 * N, L * LANES)

    wts, wc = _s5_expand_call(toe, state_map(0, jnp.arange(L - 1, -1, -1)), state_map(1, jnp.arange(L)),
                              corr_map(0, jnp.arange(1, L + 1)), corr_map(1, jnp.arange(L, 0, -1)))

    qr, qi = pr[:, :, L], pi[:, :, L]
    lv = []
    for _ in range(8):
        lv.append(jnp.concatenate([qr[0].reshape(S5_BLK, 8 * N), qi[0].reshape(S5_BLK, 8 * N),
                                   qr[1].reshape(S5_BLK, 8 * N), qi[1].reshape(S5_BLK, 8 * N)], axis=1))
        qr, qi = qr * qr - qi * qi, 2.0 * qr * qi
    apow = jnp.stack(lv, axis=1)
    return wts, wc, apow


def _mixout_body(x_ref, mod_ref, g1_ref, g2_ref, wbg_ref, bbg_ref, ya_ref, y4_ref, u4_ref,
                 dsk_ref, wglu_ref, bglu_ref, wso_ref, wlo_ref, wo_ref, wrt_ref, brt_ref,
                 x1_ref, h2_ref, aff_ref):
    m = mod_ref[0]
    sh1, sc1, gt1 = m[:, 0:D_MODEL], m[:, D_MODEL:2 * D_MODEL], m[:, 2 * D_MODEL:3 * D_MODEL]
    sh2, sc2 = m[:, 3 * D_MODEL:4 * D_MODEL], m[:, 4 * D_MODEL:5 * D_MODEL]
    x = x_ref[...]
    h = _norm_mod(x, g1_ref[...], sc1, sh1)
    gates = _sigmoid(jnp.dot(h.astype(BF16), wbg_ref[...], preferred_element_type=F32) + bbg_ref[...])

    def assemble(ref):
        return jnp.concatenate([ref[k] for k in range(S5_BLK)], axis=1)

    ys = assemble(y4_ref) + dsk_ref[...] * assemble(u4_ref)
    v = _gelu(ys)
    ob = v * _sigmoid(jnp.dot(v.astype(BF16), wglu_ref[...], preferred_element_type=F32) + bglu_ref[...])
    merged = (gates[:, :D_MODEL] * jnp.dot(ya_ref[...], wlo_ref[...], preferred_element_type=F32)
              + gates[:, D_MODEL:] * jnp.dot(ob.astype(BF16), wso_ref[...], preferred_element_type=F32))
    mix = jnp.dot(merged.astype(BF16), wo_ref[...], preferred_element_type=F32)
    x1 = x + gt1 * mix
    x1_ref[...] = x1
    h2 = _norm_mod(x1, g2_ref[...], sc2, sh2)
    h2_ref[...] = h2
    logits = _dot3(wrt_ref[...], h2, dims=(((1,), (1,)), ((), ()))) + brt_ref[...]
    mx = jnp.max(logits, axis=0, keepdims=True)
    ex = jnp.exp(logits - mx)
    aff_ref[...] = ex / jnp.sum(ex, axis=0, keepdims=True)


def _mixout_call(x2d, mod3, g1, g2, wbg, bbg, ya, y4, u4, dsk, wglu, bglu, wso, wlo, wo, wrt, brt,
                 mod_row, tm):
    n = x2d.shape[0]
    full = lambda shape: pl.BlockSpec(shape, lambda i: (0,) * len(shape))
    s5_spec = pl.BlockSpec((S5_BLK, tm, LANES), lambda i: (0, i, 0))
    tok = lambda w: pl.BlockSpec((tm, w), lambda i: (i, 0))
    return pl.pallas_call(
        _mixout_body,
        grid=(n // tm,),
        in_specs=[tok(D_MODEL), pl.BlockSpec((1, 1, 6 * D_MODEL), lambda i: (mod_row(i), 0, 0)),
                  full((1, D_MODEL)), full((1, D_MODEL)), full((D_MODEL, 2 * D_MODEL)),
                  full((1, 2 * D_MODEL)), tok(D_LRU), s5_spec, s5_spec, full((1, D_S5)),
                  full((D_S5, D_S5)), full((1, D_S5)), full((D_S5, D_MODEL)), full((D_LRU, D_MODEL)),
                  full((D_MODEL, D_MODEL)), full((N_EXPERTS, D_MODEL)), full((N_EXPERTS, 1))],
        out_specs=[tok(D_MODEL), tok(D_MODEL), pl.BlockSpec((N_EXPERTS, tm), lambda i: (0, i))],
        out_shape=[jax.ShapeDtypeStruct((n, D_MODEL), F32), jax.ShapeDtypeStruct((n, D_MODEL), F32),
                   jax.ShapeDtypeStruct((N_EXPERTS, n), F32)],
        compiler_params=_cp(("parallel",)), name="mixout",
    )(x2d, mod3, g1, g2, wbg, bbg, ya, y4, u4, dsk, wglu, bglu, wso, wlo, wo, wrt, brt)


def _select_body(aff_ref, gate_ref, pos_ref, offs_ref, *, n_tok, cap):
    aff = aff_ref[...]
    capf = float(cap)

    def bis(_, lh):
        lo, hi = lh
        mid = lo + ((hi - lo + 1) >> 1)
        cnt = jnp.sum(jnp.where(aff >= pltpu.bitcast(mid, F32), 1.0, 0.0), axis=1, keepdims=True)
        ok = cnt >= capf
        return jnp.where(ok, mid, lo), jnp.where(ok, hi, mid - 1)

    lo0 = jnp.zeros((N_EXPERTS, 1), I32)
    hi0 = jnp.full((N_EXPERTS, 1), 0x7F800000, I32)
    thr_bits, _ = lax.fori_loop(0, 31, bis, (lo0, hi0))
    thr = pltpu.bitcast(thr_bits, F32)
    need = capf - jnp.sum(jnp.where(aff > thr, 1.0, 0.0), axis=1, keepdims=True)

    r = lax.broadcasted_iota(I32, (TOK_TILE, TOK_TILE), 0)
    c = lax.broadcasted_iota(I32, (TOK_TILE, TOK_TILE), 1)
    tri = jnp.where(r < c, 1.0, 0.0).astype(BF16)
    lane = lax.broadcasted_iota(I32, (N_EXPERTS, LANES), 1)
    n_tiles = n_tok // TOK_TILE
    run_eq = jnp.zeros((N_EXPERTS, 1), F32)
    run_sel = jnp.zeros((N_EXPERTS, 1), F32)
    offs = jnp.zeros((N_EXPERTS, LANES), F32)
    for t in range(n_tiles):
        sl = slice(t * TOK_TILE, (t + 1) * TOK_TILE)
        aff_t = aff_ref[:, sl]
        is_eq = aff_t == thr
        eq_t = jnp.where(is_eq, 1.0, 0.0)
        rank_eq = jnp.dot(eq_t.astype(BF16), tri, preferred_element_type=F32) + run_eq
        cand = (aff_t > thr) | (is_eq & (rank_eq < need))
        pos_t = jnp.dot(jnp.where(cand, 1.0, 0.0).astype(BF16), tri, preferred_element_type=F32) + run_sel
        sel_t = cand & (pos_t < capf)
        sel_f = jnp.where(sel_t, 1.0, 0.0)
        offs = jnp.where(lane == t, run_sel, offs)
        gate_ref[:, sl] = jnp.where(sel_t, aff_t, 0.0)
        pos_ref[:, sl] = jnp.where(sel_t, pos_t, -1.0).astype(I32)
        run_eq = run_eq + jnp.sum(eq_t, axis=1, keepdims=True)
        run_sel = run_sel + jnp.sum(sel_f, axis=1, keepdims=True)
    offs = jnp.where(lane >= n_tiles, run_sel, offs)
    offs_ref[...] = offs.astype(I32)


def _select_call(aff_t, cap):
    n_tok = aff_t.shape[1]
    full = lambda shape: pl.BlockSpec(shape, lambda i: (0,) * len(shape))
    return pl.pallas_call(
        functools.partial(_select_body, n_tok=n_tok, cap=cap),
        grid=(1,),
        in_specs=[full((N_EXPERTS, n_tok))],
        out_specs=[full((N_EXPERTS, n_tok)), full((N_EXPERTS, n_tok)), full((N_EXPERTS, LANES))],
        out_shape=[jax.ShapeDtypeStruct((N_EXPERTS, n_tok), F32),
                   jax.ShapeDtypeStruct((N_EXPERTS, n_tok), I32),
                   jax.ShapeDtypeStruct((N_EXPERTS, LANES), I32)],
        compiler_params=_cp(("arbitrary",)), name="select",
    )(aff_t)


def _invert_body(offs_ref, pos_ref, gate_ref, idx_ref, gcol_ref, acc_i, acc_g, *, n_tiles, cap):
    base = pl.program_id(0) * LANES
    acc_i[...] = jnp.full(acc_i.shape, -1.0, F32)
    acc_g[...] = jnp.zeros(acc_g.shape, F32)
    jrow = lax.broadcasted_iota(I32, (SLOT_CHUNK, LANES), 0)
    lane = lax.broadcasted_iota(I32, (1, LANES), 1)

    def per_tile(t, _):
        first = (offs_ref[base + t] >> 3) << 3
        prow = pos_ref[0, pl.ds(t, 1), :]
        grow = gate_ref[0, pl.ds(t, 1), :]
        tok = (lane + t * TOK_TILE).astype(F32)

        def window(w, _):
            start = pl.multiple_of(first + w * SLOT_CHUNK, SUBLANES)
            rows = pl.ds(start, SLOT_CHUNK)
            hit = prow == (jrow + start)
            acc_i[rows, :] = jnp.maximum(acc_i[rows, :], jnp.where(hit, tok, -1.0))
            acc_g[rows, :] = jnp.maximum(acc_g[rows, :], jnp.where(hit, grow, 0.0))
            return 0

        n_win = (offs_ref[base + t + 1] - first + SLOT_CHUNK - 1) >> (SLOT_CHUNK.bit_length() - 1)
        lax.fori_loop(0, n_win, window, 0)
        return 0

    lax.fori_loop(0, n_tiles, per_tile, 0)
    idx_ref[0] = jnp.max(acc_i[0:cap, :], axis=1, keepdims=True).astype(I32)
    gcol_ref[0] = jnp.max(acc_g[0:cap, :], axis=1, keepdims=True)


def _invert_call(offs, pos, gate, cap):
    n_tok = pos.shape[1]
    n_tiles = n_tok // TOK_TILE
    pos3 = pos.reshape(N_EXPERTS, n_tiles, TOK_TILE)
    gate3 = gate.reshape(N_EXPERTS, n_tiles, TOK_TILE)
    per_e = lambda shape: pl.BlockSpec(shape, lambda e, o: (e, 0, 0))
    return pl.pallas_call(
        functools.partial(_invert_body, n_tiles=n_tiles, cap=cap),
        grid_spec=pltpu.PrefetchScalarGridSpec(
            num_scalar_prefetch=1, grid=(N_EXPERTS,),
            in_specs=[per_e((1, n_tiles, TOK_TILE)), per_e((1, n_tiles, TOK_TILE))],
            out_specs=[per_e((1, cap, 1)), per_e((1, cap, 1))],
            scratch_shapes=[pltpu.VMEM((cap + SLOT_CHUNK, LANES), F32)] * 2),
        out_shape=[jax.ShapeDtypeStruct((N_EXPERTS, cap, 1), I32),
                   jax.ShapeDtypeStruct((N_EXPERTS, cap, 1), F32)],
        compiler_params=_cp(("parallel",)), name="invert",
    )(offs.reshape(-1), pos3, gate3)


def _ffn_body(idx0_ref, idxn_ref, gcol_ref, wg_ref, wu_ref, wd_ref, h_hbm, out_ref,
              xe, sem, *, cap):
    e = pl.program_id(0)
    slot = e % 2
    nxt = 1 - slot

    def issue_rows(idx_ref, s, j0, n):
        for j in range(j0, j0 + n):
            pltpu.make_async_copy(h_hbm.at[pl.ds(idx_ref[0, 0, j], 1), :],
                                  xe.at[s, pl.ds(j, 1), :], sem.at[s]).start()

    @pl.when(e == 0)
    def _():
        def issue(j, _):
            pltpu.make_async_copy(h_hbm.at[pl.ds(idx0_ref[0, 0, j], 1), :],
                                  xe.at[0, pl.ds(j, 1), :], sem.at[0]).start()
            return 0
        lax.fori_loop(0, cap, issue, 0, unroll=8)

    wgb = wg_ref[0].astype(BF16)
    wub = wu_ref[0].astype(BF16)
    wdb = wd_ref[0].astype(BF16)

    pltpu.make_async_copy(h_hbm.at[pl.ds(0, cap), :], xe.at[slot], sem.at[slot]).wait()

    ch = 256
    n_ch = cap // ch
    per = -(-cap // (n_ch - 1))
    for c in range(n_ch):
        sl = slice(c * ch, (c + 1) * ch)
        xb = xe[slot, sl, :].astype(BF16)
        issue_rows(idxn_ref, nxt, min(c * per, cap), min((c + 1) * per, cap) - min(c * per, cap))
        g = jnp.dot(xb, wgb, preferred_element_type=F32)
        u = jnp.dot(xb, wub, preferred_element_type=F32)
        hid = (g * _sigmoid(g)) * u
        ye = jnp.dot(hid.astype(BF16), wdb, preferred_element_type=F32)
        out_ref[0, sl, :] = ye * gcol_ref[0, sl, :]

    @pl.when(e == N_EXPERTS - 1)
    def _():
        pltpu.make_async_copy(h_hbm.at[pl.ds(0, cap), :], xe.at[nxt], sem.at[nxt]).wait()


def _ffn_call(idx, gcol, w_gate, w_up, w_down, h2, cap):
    d_e = w_gate.shape[2]
    wspec = lambda a, b: pl.BlockSpec((1, a, b), lambda e: (e, 0, 0))
    return pl.pallas_call(
        functools.partial(_ffn_body, cap=cap),
        grid=(N_EXPERTS,),
        in_specs=[pl.BlockSpec((1, 1, cap), lambda e: (0, 0, 0), memory_space=pltpu.SMEM),
                  pl.BlockSpec((1, 1, cap), lambda e: (jnp.minimum(e + 1, N_EXPERTS - 1), 0, 0),
                               memory_space=pltpu.SMEM),
                  pl.BlockSpec((1, cap, 1), lambda e: (e, 0, 0)),
                  wspec(D_MODEL, d_e), wspec(D_MODEL, d_e), wspec(d_e, D_MODEL),
                  pl.BlockSpec(memory_space=pl.ANY)],
        out_specs=pl.BlockSpec((1, cap, D_MODEL), lambda e: (e, 0, 0)),
        out_shape=jax.ShapeDtypeStruct((N_EXPERTS, cap, D_MODEL), F32),
        scratch_shapes=[pltpu.VMEM((2, cap, D_MODEL), F32), pltpu.SemaphoreType.DMA((2,))],
        compiler_params=_cp(("arbitrary",)), name="ffn",
    )(idx, idx, gcol, w_gate, w_up, w_down, h2)


def _combine_body(offs_ref, x1_ref, mod_ref, gf_ref, pos_ref, yb_hbm, y_ref, wins, acc_ref, sem, *, cap):
    t = pl.program_id(0)
    n_t = pl.num_programs(0)
    slot = t % 2

    def kmax_of(tt):
        k = 0
        for e in range(N_EXPERTS):
            k = jnp.maximum(k, offs_ref[e * LANES + tt + 1] - offs_ref[e * LANES + tt])
        return k

    def geometry(tt, e, w, win):
        fetch = win + SUBLANES
        lo = offs_ref[e * LANES + tt] + w * win
        start = pl.multiple_of(jnp.minimum((lo >> 3) << 3, cap - fetch), SUBLANES)
        return lo, start, fetch

    def issue(tt, w, s, win):
        for e in range(N_EXPERTS):
            _, start, fetch = geometry(tt, e, w, win)
            pltpu.make_async_copy(yb_hbm.at[e, pl.ds(start, fetch), :],
                                  wins.at[s, pl.ds(e * fetch, fetch), :], sem.at[s]).start()

    def drain(s, win):
        rows = N_EXPERTS * (win + SUBLANES)
        pltpu.make_async_copy(yb_hbm.at[0, pl.ds(0, rows), :], wins.at[s, pl.ds(0, rows), :],
                              sem.at[s]).wait()

    def expand(w, s, win):
        sub = lax.broadcasted_iota(I32, (win + SUBLANES, TOK_TILE), 0)
        rows = []
        for e in range(N_EXPERTS):
            lo, start, _ = geometry(t, e, w, win)
            pos = pos_ref[e:e + 1, :]
            in_round = jnp.logical_and(pos >= lo, pos < lo + win)
            rows.append(jnp.where(jnp.logical_and(pos - start == sub, in_round), 1.0, 0.0))
        onehot = jnp.concatenate(rows, axis=0).T.astype(BF16)
        data = wins[s, 0:N_EXPERTS * (win + SUBLANES), :]
        hi = data.astype(BF16)
        lo_part = (data - hi.astype(F32)).astype(BF16)
        return (jnp.dot(onehot, hi, preferred_element_type=F32)
                + jnp.dot(onehot, lo_part, preferred_element_type=F32))

    def issue_first(tt, s):
        wide = kmax_of(tt) > COMBINE_WIN

        @pl.when(wide)
        def _():
            issue(tt, 0, s, COMBINE_WIN_WIDE)

        @pl.when(jnp.logical_not(wide))
        def _():
            issue(tt, 0, s, COMBINE_WIN)

    @pl.when(t == 0)
    def _():
        issue_first(0, 0)

    @pl.when(t + 1 < n_t)
    def _():
        issue_first(t + 1, 1 - slot)

    kmax = kmax_of(t)

    @pl.when(kmax > COMBINE_WIN)
    def _():
        drain(slot, COMBINE_WIN_WIDE)
        acc_ref[...] = expand(0, slot, COMBINE_WIN_WIDE)

    @pl.when(kmax <= COMBINE_WIN)
    def _():
        drain(slot, COMBINE_WIN)
        acc_ref[...] = expand(0, slot, COMBINE_WIN)

    def more(w, _):
        issue(t, w, slot, COMBINE_WIN_WIDE)
        drain(slot, COMBINE_WIN_WIDE)
        acc_ref[...] += expand(w, slot, COMBINE_WIN_WIDE)
        return 0
    lax.fori_loop(1, (kmax + COMBINE_WIN_WIDE - 1) >> (COMBINE_WIN_WIDE.bit_length() - 1), more, 0)
    acc = acc_ref[...]

    gt2 = mod_ref[0][:, 5 * D_MODEL:6 * D_MODEL]
    x2 = x1_ref[...] + gt2 * acc
    ms = jnp.mean(x2 * x2, axis=-1, keepdims=True)
    y_ref[...] = x2 * lax.rsqrt(ms + EPS) * gf_ref[...]


def _combine_call(offs, pos, x1, mod3, gf, ybuf, mod_row, cap):
    n = x1.shape[0]
    return pl.pallas_call(
        functools.partial(_combine_body, cap=cap),
        grid_spec=pltpu.PrefetchScalarGridSpec(
            num_scalar_prefetch=1, grid=(n // TOK_TILE,),
            in_specs=[pl.BlockSpec((TOK_TILE, D_MODEL), lambda i, o: (i, 0)),
                      pl.BlockSpec((1, 1, 6 * D_MODEL), lambda i, o: (mod_row(i), 0, 0)),
                      pl.BlockSpec((1, D_MODEL), lambda i, o: (0, 0)),
                      pl.BlockSpec((N_EXPERTS, TOK_TILE), lambda i, o: (0, i)),
                      pl.BlockSpec(memory_space=pl.ANY)],
            out_specs=pl.BlockSpec((TOK_TILE, D_MODEL), lambda i, o: (i, 0)),
            scratch_shapes=[pltpu.VMEM((2, N_EXPERTS * (COMBINE_WIN_WIDE + SUBLANES), D_MODEL), F32),
                            pltpu.VMEM((TOK_TILE, D_MODEL), F32),
                            pltpu.SemaphoreType.DMA((2,))]),
        out_shape=jax.ShapeDtypeStruct((n, D_MODEL), F32),
        compiler_params=_cp(("arbitrary",)), name="combine",
    )(offs.reshape(-1), x1, mod3, gf, pos, ybuf)


def _lru_gate_weights(wa, wx):
    eye = jnp.eye(4, dtype=F32)

    def bd(w):
        w5 = w.reshape(4, 4, LRU_HEAD_DIM, LRU_HEAD_DIM)
        return jnp.einsum('khij,hg->khigj', w5, eye).reshape(4, 256, 256)

    return jnp.concatenate([bd(wa[0]), bd(wa[1]), bd(wx[0]), bd(wx[1])], axis=2).astype(BF16)


def kernel(x_prompt, x_sample, state_lru, state_s5_re, state_s5_im, c, c_ctx, w_mod, b_mod, g_norm1, g_norm2, w_in, conv_w, conv_b, lru_wa, lru_ba, lru_wx, lru_bx, lru_lambda, s5_lambda_re, s5_lambda_im, s5_log_step, s5_b_re, s5_b_im, s5_c_re, s5_c_im, s5_d, s5_w_glu, s5_b_glu, w_lru_out, w_s5_out, w_branch_gate, b_branch_gate, w_o, w_router, b_router, w_e_gate, w_e_up, w_e_down, g_final):
    bp, tp, _ = x_prompt.shape
    bs, ts, _ = x_sample.shape
    n_p, n_s = bp * tp, bs * ts
    l = 0

    c_all = jnp.zeros((16, D_MODEL), F32).at[0].set(c_ctx).at[1:1 + bs].set(c)
    mod3 = _mod_call(c_all, w_mod[l], b_mod[l][None, :]).reshape(16, 1, 6 * D_MODEL)

    w_in_b = w_in[l].astype(BF16)
    w_main, w_s5in = w_in_b[:, :2 * D_LRU], w_in_b[:, 2 * D_LRU:]
    g1 = g_norm1[l][None, :]
    g2 = g_norm2[l][None, :]
    wg = _lru_gate_weights(lru_wa[l], lru_wx[l])
    wts, wc, apow = _s5_weights(s5_lambda_re[l], s5_lambda_im[l], s5_log_step[l],
                                s5_b_re[l], s5_b_im[l], s5_c_re[l], s5_c_im[l])
    wbg = w_branch_gate[l].astype(BF16)
    bbg = b_branch_gate[l][None, :]
    wglu = s5_w_glu[l].astype(BF16)
    wso = w_s5_out[l].astype(BF16)
    wlo = w_lru_out[l].astype(BF16)
    wo = w_o[l].astype(BF16)
    wrt = w_router[l].T
    brt = b_router[l][:, None]

    xp2 = x_prompt.reshape(n_p, D_MODEL)
    xs2 = x_sample.reshape(n_s, D_MODEL)
    ctx_row = lambda i: 0
    tm_in = 512
    lat_row_in = lambda i: 1 + i // (ts // max(tm_in, ts))

    def s5_state(re, im):
        def part(a, d):
            return a[:, d].reshape(-1, S5_BLK, 8 * S5_N).transpose(1, 0, 2)
        return jnp.concatenate([part(re, 0), part(im, 0), part(re, 1), part(im, 1)], axis=2)

    xb_p, gg_p, us4_p = _inproj_call(xp2, mod3, g1, w_main, w_s5in, ctx_row, max(tm_in, tp), tp)
    ya_p, lru_fin = _lru_call(xb_p, gg_p, conv_w[l], conv_b[l][None, :], wg, lru_ba[l], lru_bx[l],
                              lru_lambda[l], jnp.zeros((bp, 2, D_LRU), F32), bp, tp)
    y4_p, s5_fin = _s5_call(us4_p, wts, wc, apow, jnp.zeros((S5_BLK, bp, S5_SW), F32), bp, tp,
                            with_h0=False, col_major=False)
    xb_s, gg_s, us4_s = _inproj_call(xs2, mod3, g1, w_main, w_s5in, lat_row_in, max(tm_in, ts), ts)
    ya_s, _ = _lru_call(xb_s, gg_s, conv_w[l], conv_b[l][None, :], wg, lru_ba[l], lru_bx[l],
                        lru_lambda[l], state_lru[:, l], bs, ts)
    y4_s, _ = _s5_call(us4_s, wts, wc, apow, s5_state(state_s5_re[:, l], state_s5_im[:, l]), bs, ts,
                       with_h0=True, col_major=True)

    mo_args = (wbg, bbg)
    mo_tail = (s5_d[l][None, :], wglu, s5_b_glu[l][None, :], wso, wlo, wo, wrt, brt)
    tm_mo = 512
    x1_p, h2_p, aff_p = _mixout_call(xp2, mod3, g1, g2, *mo_args, ya_p, y4_p, us4_p, *mo_tail,
                                     mod_row=ctx_row, tm=tm_mo)
    x1_s, h2_s, aff_s = _mixout_call(xs2, mod3, g1, g2, *mo_args, ya_s, y4_s, us4_s, *mo_tail,
                                     mod_row=lambda i: 1 + i // (ts // tm_mo), tm=tm_mo)

    cap = (CAPACITY_FACTOR * n_p) // N_EXPERTS
    gf = g_final[None, :]
    ys = []
    for aff, h2, x1, mod_row in ((aff_p, h2_p, x1_p, ctx_row),
                                 (aff_s, h2_s, x1_s, lambda i: 1 + i // (ts // TOK_TILE))):
        gate, pos, offs = _select_call(aff, cap)
        idx, gcol = _invert_call(offs, pos, gate, cap)
        ybuf = _ffn_call(idx.reshape(N_EXPERTS, 1, cap), gcol, w_e_gate[l], w_e_up[l], w_e_down[l], h2, cap)
        ys.append(_combine_call(offs, pos, x1, mod3, gf, ybuf, mod_row, cap))
    y_p, y_s = ys

    new_lru = lru_fin[:, None]
    sf = s5_fin.transpose(1, 0, 2).reshape(bp, S5_BLK, 2, 2, 8, S5_N)
    sf = sf.transpose(0, 2, 3, 1, 4, 5).reshape(bp, 2, 2, S5_G, S5_N)
    new_s5r = sf[:, :, 0][:, None]
    new_s5i = sf[:, :, 1][:, None]
    return (y_p.reshape(bp, tp, D_MODEL), y_s.reshape(bs, ts, D_MODEL), new_lru, new_s5r, new_s5i)
```

```python
import functools
import math

import jax
import jax.numpy as jnp
from jax import lax
from jax.experimental import pallas as pl
from jax.experimental.pallas import tpu as pltpu

F32 = jnp.float32
BF16 = jnp.bfloat16
I32 = jnp.int32

D_MODEL = 1024
D_LRU = 1024
LRU_HEADS = 16
LRU_HEAD_DIM = 64
LRU_C = 8.0
CONV_W = 4
D_S5 = 512
S5_P = 16
S5_G = 32
S5_N = 64
GRID_W = 64
N_EXPERTS = 16
CAPACITY_FACTOR = 2
EPS = 1e-6

LANES = 128
SUBLANES = 8
S5_L = 8
S5_BLK = 4
S5_SW = 4 * 512
TOK_TILE = 128
SLOT_CHUNK = 64
COMBINE_WIN = 32
COMBINE_WIN_WIDE = 64
VMEM_LIMIT = 56 * 1024 * 1024


def _cp(sem, vmem=VMEM_LIMIT):
    return pltpu.CompilerParams(dimension_semantics=sem, vmem_limit_bytes=vmem)


def _bdot(a, b):
    return jnp.dot(a.astype(BF16), b.astype(BF16), preferred_element_type=F32)


def _split2(a):
    hi = a.astype(BF16)
    lo = (a - hi.astype(F32)).astype(BF16)
    return hi, lo


def _dot3(a, b, dims=(((1,), (0,)), ((), ()))):
    ah, al = _split2(a)
    bh, bl = _split2(b)
    d = functools.partial(lax.dot_general, dimension_numbers=dims, preferred_element_type=F32)
    return d(ah, bh) + (d(al, bh) + d(ah, bl))


def _sigmoid(x):
    return 0.5 * jnp.tanh(0.5 * x) + 0.5


def _gelu(x):
    c = math.sqrt(2.0 / math.pi)
    return 0.5 * x * (1.0 + jnp.tanh(c * (x + 0.044715 * (x * x * x))))


def _norm_mod(x, g, scale, shift):
    ms = jnp.mean(x * x, axis=-1, keepdims=True)
    return (x * lax.rsqrt(ms + EPS) * g) * (1.0 + scale) + shift


def _mod_body(c_ref, w_ref, b_ref, o_ref):
    c = c_ref[...]
    s = c * _sigmoid(c)
    o_ref[...] = _dot3(s, w_ref[...]) + b_ref[...]


def _mod_call(c_all, w_mod, b_mod):
    n = w_mod.shape[1]
    tn = 1536
    return pl.pallas_call(
        _mod_body,
        grid=(n // tn,),
        in_specs=[pl.BlockSpec((16, D_MODEL), lambda j: (0, 0)),
                  pl.BlockSpec((D_MODEL, tn), lambda j: (0, j)),
                  pl.BlockSpec((1, tn), lambda j: (0, j))],
        out_specs=pl.BlockSpec((16, tn), lambda j: (0, j)),
        out_shape=jax.ShapeDtypeStruct((16, n), F32),
        compiler_params=_cp(("arbitrary",)),
        name="mod",
    )(c_all, w_mod, b_mod)


def _inproj_body(x_ref, mod_ref, g_ref, w_ref, ws_ref, xb_ref, gg_ref, us_ref, *, seq_len):
    m = mod_ref[0]
    h = _norm_mod(x_ref[...], g_ref[...], m[:, D_MODEL:2 * D_MODEL], m[:, 0:D_MODEL])
    hb = h.astype(BF16)
    z = jnp.dot(hb, w_ref[...], preferred_element_type=F32)
    seg = seq_len // SUBLANES
    for s in range(x_ref.shape[0] // seq_len):
        for k in range(SUBLANES):
            t0 = s * seq_len + k * seg
            for c in range(D_LRU // LANES):
                xb_ref[c, pl.ds(s * seq_len + k, seg, stride=SUBLANES), :] = (
                    z[t0:t0 + seg, c * LANES:(c + 1) * LANES])
    gg_ref[...] = _gelu(z[:, D_LRU:]).astype(BF16)
    us = jnp.dot(hb, ws_ref[...], preferred_element_type=F32)
    for k in range(S5_BLK):
        us_ref[k] = us[:, k * LANES:(k + 1) * LANES]


def _inproj_call(x2d, mod3, g1, w_main, w_s5, mod_row, tm, seq_len):
    n = x2d.shape[0]
    return pl.pallas_call(
        functools.partial(_inproj_body, seq_len=seq_len), grid=(n // tm,),
        in_specs=[pl.BlockSpec((tm, D_MODEL), lambda i: (i, 0)),
                  pl.BlockSpec((1, 1, 6 * D_MODEL), lambda i: (mod_row(i), 0, 0)),
                  pl.BlockSpec((1, D_MODEL), lambda i: (0, 0)),
                  pl.BlockSpec((D_MODEL, 2 * D_LRU), lambda i: (0, 0)),
                  pl.BlockSpec((D_MODEL, D_S5), lambda i: (0, 0))],
        out_specs=[pl.BlockSpec((D_LRU // LANES, tm, LANES), lambda i: (0, i, 0)),
                   pl.BlockSpec((tm, D_LRU), lambda i: (i, 0)),
                   pl.BlockSpec((S5_BLK, tm, LANES), lambda i: (0, i, 0))],
        out_shape=[jax.ShapeDtypeStruct((D_LRU // LANES, n, LANES), F32),
                   jax.ShapeDtypeStruct((n, D_LRU), BF16),
                   jax.ShapeDtypeStruct((S5_BLK, n, LANES), F32)],
        compiler_params=_cp(("parallel",)), name="inproj",
    )(x2d, mod3, g1, w_main, w_s5)


def _lru_body(xb_ref, gg_ref, cw_ref, cb_ref, wg_ref, ba_ref, bx_ref, lam_ref, h0_ref,
              ya_ref, fin_ref, xpad, a_f, b_f, a_b, b_b, hs, *, T):
    seg = T // SUBLANES
    ch = 256
    nch = T // ch
    n_slab = D_LRU // LANES
    pad = 2 * SUBLANES
    sub = lax.broadcasted_iota(I32, (SUBLANES, D_LRU), 0)

    def rows_of(r0, n):
        return jnp.concatenate([xb_ref[c, r0:r0 + n, :] for c in range(n_slab)], axis=1)

    def from_prev_segment(tile):
        return jnp.where(sub >= 1, pltpu.roll(tile, 1, 0), 0.0)

    def from_next_segment(tile):
        return jnp.where(sub < SUBLANES - 1, pltpu.roll(tile, SUBLANES - 1, 0), 0.0)

    xpad[0:SUBLANES, :] = from_prev_segment(rows_of((seg - 2) * SUBLANES, SUBLANES))
    xpad[SUBLANES:pad, :] = from_prev_segment(rows_of((seg - 1) * SUBLANES, SUBLANES))
    xpad[pad + T:pad + T + SUBLANES, :] = from_next_segment(rows_of(0, SUBLANES))
    for c in range(nch):
        xpad[pad + c * ch:pad + (c + 1) * ch, :] = rows_of(c * ch, ch)

    nl = -lam_ref[...]
    softplus = jnp.maximum(nl, 0.0) + jnp.log1p(jnp.exp(-jnp.abs(nl)))
    scr = ((a_f, b_f), (a_b, b_b))

    for c in range(nch):
        t0 = c * ch
        xc = cb_ref[...] + sum(
            xpad[t0 + k * SUBLANES:t0 + k * SUBLANES + ch, :] * cw_ref[k:k + 1, :]
            for k in range(CONV_W))
        xcb = xc.astype(BF16)
        pre = [jnp.dot(xcb[:, kb * 256:(kb + 1) * 256], wg_ref[kb], preferred_element_type=F32)
               for kb in range(4)]
        for d in range(2):
            ra = jnp.concatenate([p[:, d * 256:(d + 1) * 256] for p in pre], axis=1)
            gx = jnp.concatenate([p[:, 512 + d * 256:512 + (d + 1) * 256] for p in pre], axis=1)
            r = _sigmoid(ra + ba_ref[d:d + 1, :])
            gi = _sigmoid(gx + bx_ref[d:d + 1, :])
            log_a = (-LRU_C) * r * softplus[d:d + 1, :]
            a = jnp.exp(log_a)
            scr[d][0][t0:t0 + ch, :] = a
            scr[d][1][t0:t0 + ch, :] = jnp.sqrt(1.0 - a * a) * gi * xc

    def local_scan(a_ref, u_ref, reverse):
        def step(i, carry):
            h, p = carry
            r0 = pl.multiple_of((seg - 1 - i if reverse else i) * SUBLANES, SUBLANES)
            a = a_ref[pl.ds(r0, SUBLANES), :]
            h = a * h + u_ref[pl.ds(r0, SUBLANES), :]
            p = a * p
            u_ref[pl.ds(r0, SUBLANES), :] = h
            a_ref[pl.ds(r0, SUBLANES), :] = p
            return h, p
        init = (jnp.zeros((SUBLANES, D_LRU), F32), jnp.ones((SUBLANES, D_LRU), F32))
        return lax.fori_loop(0, seg, step, init, unroll=2)

    def carry_in(h_end, p_end, h0_row, reverse):
        edge = sub == (SUBLANES - 1 if reverse else 0)
        shift = SUBLANES - 1 if reverse else 1
        c = jnp.broadcast_to(h0_row, (SUBLANES, D_LRU))
        for _ in range(SUBLANES - 1):
            c = jnp.where(edge, h0_row, pltpu.roll(h_end + p_end * c, shift, 0))
        return c

    hf_end, pf_end = local_scan(a_f, b_f, False)
    hb_end, pb_end = local_scan(a_b, b_b, True)
    c_f = carry_in(hf_end, pf_end, h0_ref[0, 0:1, :], False)
    c_b = carry_in(hb_end, pb_end, h0_ref[0, 1:2, :], True)
    fin_ref[0, 0:1, :] = (hf_end + pf_end * c_f)[SUBLANES - 1:SUBLANES, :]
    fin_ref[0, 1:2, :] = (hb_end + pb_end * c_b)[0:1, :]

    for c in range(nch):
        sl = slice(c * ch, (c + 1) * ch)
        tile3 = (ch // SUBLANES, SUBLANES, D_LRU)
        hsum = ((b_f[sl, :].reshape(tile3) + a_f[sl, :].reshape(tile3) * c_f[None])
                + (b_b[sl, :].reshape(tile3) + a_b[sl, :].reshape(tile3) * c_b[None])).reshape(ch, D_LRU)
        for j in range(n_slab):
            hs[j, sl, :] = hsum[:, j * LANES:(j + 1) * LANES]

    for k in range(SUBLANES):
        rows = slice(k * seg, (k + 1) * seg)
        h_seg = jnp.concatenate([hs[j, pl.ds(k, seg, stride=SUBLANES), :] for j in range(n_slab)], axis=1)
        ya_ref[0, rows, :] = (gg_ref[0, rows, :].astype(F32) * h_seg).astype(BF16)


def _lru_call(xb8, gg, conv_w, conv_b, wg, ba, bx, lam, h0, n_seq, T):
    gg3 = gg.reshape(n_seq, T, D_LRU)
    n_slab = D_LRU // LANES
    full = lambda shape: pl.BlockSpec(shape, lambda i: (0,) * len(shape))
    seq = lambda shape: pl.BlockSpec(shape, lambda i: (i,) + (0,) * (len(shape) - 1))
    ya, fin = pl.pallas_call(
        functools.partial(_lru_body, T=T),
        grid=(n_seq,),
        in_specs=[pl.BlockSpec((n_slab, T, LANES), lambda i: (0, i, 0)), seq((1, T, D_LRU)),
                  full((CONV_W, D_LRU)), full((1, D_LRU)),
                  full((4, 256, 1024)), full((2, D_LRU)), full((2, D_LRU)), full((2, D_LRU)),
                  seq((1, 2, D_LRU))],
        out_specs=[seq((1, T, D_LRU)), seq((1, 2, D_LRU))],
        out_shape=[jax.ShapeDtypeStruct((n_seq, T, D_LRU), BF16),
                   jax.ShapeDtypeStruct((n_seq, 2, D_LRU), F32)],
        scratch_shapes=[pltpu.VMEM((T + 3 * SUBLANES, D_LRU), F32)] + [pltpu.VMEM((T, D_LRU), F32)] * 4
                       + [pltpu.VMEM((n_slab, T, LANES), F32)],
        compiler_params=_cp(("parallel",)), name="lru",
    )(xb8, gg3, conv_w, conv_b, wg, ba, bx, lam, h0)
    return ya.reshape(n_seq * T, D_LRU), fin


def _s5_body(u_ref, wts_ref, wc_ref, ap_ref, h0_ref, y_ref, fin_ref, s_scr, hp_scr, slab,
             *, rt, n_c, with_h0, col_major):
    nseq = rt // n_c
    q = 512
    seq_tok = n_c * S5_L
    halves = seq_tok // (GRID_W * S5_L)

    def tok_rows(s, h, i):
        return pl.ds(s * seq_tok + (h * S5_L + i) * GRID_W, GRID_W)

    def chunk_rows(s, h):
        return pl.ds(s * n_c + h, GRID_W, stride=halves)

    if col_major:
        for i in range(S5_L):
            for s in range(nseq):
                for h in range(halves):
                    slab[i, chunk_rows(s, h), :] = u_ref[0, tok_rows(s, h, i), :]
        ub = jnp.concatenate([slab[i] for i in range(S5_L)], axis=1).astype(BF16)
    else:
        ub = jnp.concatenate([u_ref[0, pl.ds(i, rt, stride=S5_L), :] for i in range(S5_L)],
                             axis=1).astype(BF16)
    r1 = jnp.dot(ub, wts_ref[0], preferred_element_type=F32)
    y_in = r1[:, :S5_L * LANES]
    s_scr[...] = r1[:, S5_L * LANES:]
    ap = ap_ref[0]
    if with_h0:
        for s in range(nseq):
            h0 = h0_ref[0, 0, s:s + 1, :]
            r_f = s * n_c
            r_b = s * n_c + n_c - 1
            ar, ai = ap[0:1, 0:q], ap[0:1, q:2 * q]
            s_scr[r_f:r_f + 1, 0:q] += ar * h0[:, 0:q] - ai * h0[:, q:2 * q]
            s_scr[r_f:r_f + 1, q:2 * q] += ar * h0[:, q:2 * q] + ai * h0[:, 0:q]
            br, bi = ap[0:1, 2 * q:3 * q], ap[0:1, 3 * q:4 * q]
            s_scr[r_b:r_b + 1, 2 * q:3 * q] += br * h0[:, 2 * q:3 * q] - bi * h0[:, 3 * q:4 * q]
            s_scr[r_b:r_b + 1, 3 * q:4 * q] += br * h0[:, 3 * q:4 * q] + bi * h0[:, 2 * q:3 * q]

    cidx = lax.broadcasted_iota(I32, (rt, 1), 0) % n_c
    hfr, hfi = s_scr[:, 0:q], s_scr[:, q:2 * q]
    hbr, hbi = s_scr[:, 2 * q:3 * q], s_scr[:, 3 * q:4 * q]
    lvl = 0
    dist = 1
    while dist < n_c:
        ar, ai = ap[lvl:lvl + 1, 0:q], ap[lvl:lvl + 1, q:2 * q]
        br, bi = ap[lvl:lvl + 1, 2 * q:3 * q], ap[lvl:lvl + 1, 3 * q:4 * q]
        kf = cidx >= dist
        kb = cidx < n_c - dist
        sfr = jnp.where(kf, pltpu.roll(hfr, dist, 0), 0.0)
        sfi = jnp.where(kf, pltpu.roll(hfi, dist, 0), 0.0)
        sbr = jnp.where(kb, pltpu.roll(hbr, rt - dist, 0), 0.0)
        sbi = jnp.where(kb, pltpu.roll(hbi, rt - dist, 0), 0.0)
        hfr, hfi = hfr + (ar * sfr - ai * sfi), hfi + (ar * sfi + ai * sfr)
        hbr, hbi = hbr + (br * sbr - bi * sbi), hbi + (br * sbi + bi * sbr)
        lvl += 1
        dist *= 2

    s_scr[:, 0:q] = hfr
    s_scr[:, q:2 * q] = hfi
    s_scr[:, 2 * q:3 * q] = hbr
    s_scr[:, 3 * q:4 * q] = hbi
    for s in range(nseq):
        fin_ref[0, 0, s:s + 1, 0:2 * q] = s_scr[s * n_c + n_c - 1:s * n_c + n_c, 0:2 * q]
        fin_ref[0, 0, s:s + 1, 2 * q:4 * q] = s_scr[s * n_c:s * n_c + 1, 2 * q:4 * q]

    kf = cidx >= 1
    kb = cidx < n_c - 1
    hp_scr[:, 0:q] = jnp.where(kf, pltpu.roll(hfr, 1, 0), 0.0)
    hp_scr[:, q:2 * q] = jnp.where(kf, pltpu.roll(hfi, 1, 0), 0.0)
    hp_scr[:, 2 * q:3 * q] = jnp.where(kb, pltpu.roll(hbr, rt - 1, 0), 0.0)
    hp_scr[:, 3 * q:4 * q] = jnp.where(kb, pltpu.roll(hbi, rt - 1, 0), 0.0)
    if with_h0:
        for s in range(nseq):
            hp_scr[s * n_c:s * n_c + 1, 0:2 * q] = h0_ref[0, 0, s:s + 1, 0:2 * q]
            hp_scr[s * n_c + n_c - 1:s * n_c + n_c, 2 * q:4 * q] = h0_ref[0, 0, s:s + 1, 2 * q:4 * q]
    y = y_in + jnp.dot(hp_scr[...].astype(BF16), wc_ref[0], preferred_element_type=F32)
    for i in range(S5_L):
        y_i = y[:, i * LANES:(i + 1) * LANES]
        if col_major:
            slab[i] = y_i
            for s in range(nseq):
                for h in range(halves):
                    y_ref[0, tok_rows(s, h, i), :] = slab[i, chunk_rows(s, h), :]
        else:
            y_ref[0, pl.ds(i, rt, stride=S5_L), :] = y_i


def _s5_call(us4, wts, wc, apow, h0, n_seq, T, with_h0, col_major):
    n_c = T // S5_L
    rows = n_seq * n_c
    rt = 256
    nseq_t = rt // n_c
    h0 = h0.reshape(S5_BLK, n_seq // nseq_t, nseq_t, S5_SW)
    y4, fin = pl.pallas_call(
        functools.partial(_s5_body, rt=rt, n_c=n_c, with_h0=with_h0, col_major=col_major),
        grid=(S5_BLK, rows // rt),
        in_specs=[pl.BlockSpec((1, rt * S5_L, LANES), lambda b, r: (b, r, 0)),
                  pl.BlockSpec((1, S5_L * LANES, S5_L * LANES + S5_SW), lambda b, r: (b, 0, 0)),
                  pl.BlockSpec((1, S5_SW, S5_L * LANES), lambda b, r: (b, 0, 0)),
                  pl.BlockSpec((1, 8, S5_SW), lambda b, r: (b, 0, 0)),
                  pl.BlockSpec((1, 1, nseq_t, S5_SW), lambda b, r: (b, r, 0, 0))],
        out_specs=[pl.BlockSpec((1, rt * S5_L, LANES), lambda b, r: (b, r, 0)),
                   pl.BlockSpec((1, 1, nseq_t, S5_SW), lambda b, r: (b, r, 0, 0))],
        out_shape=[jax.ShapeDtypeStruct((S5_BLK, n_seq * T, LANES), F32),
                   jax.ShapeDtypeStruct((S5_BLK, n_seq // nseq_t, nseq_t, S5_SW), F32)],
        scratch_shapes=[pltpu.VMEM((rt, S5_SW), F32), pltpu.VMEM((rt, S5_SW), F32),
                        pltpu.VMEM((S5_L, rt, LANES), F32)],
        compiler_params=_cp(("parallel", "parallel")), name="s5",
    )(us4, wts, wc, apow, h0)
    return y4, fin.reshape(S5_BLK, n_seq, S5_SW)


def _s5_disc(lr, li, st):
    mag = jnp.exp(lr * st)
    ang = li * st
    ar, ai = mag * jnp.cos(ang), mag * jnp.sin(ang)
    den = lr * lr + li * li
    fr = ((ar - 1.0) * lr + ai * li) / den
    fi = (ai * lr - (ar - 1.0) * li) / den
    return ar, ai, fr, fi


def _s5_powers(ar, ai, n):
    out = [(jnp.ones_like(ar), jnp.zeros_like(ar))]
    for _ in range(n):
        pr, pi = out[-1]
        out.append((pr * ar - pi * ai, pr * ai + pi * ar))
    return out


def _s5_prep_body(rowp_ref, colp_ref, flatp_ref, bt_ref, cm_ref, wts_ref, wc_ref, ap_ref):
    L, N, P = S5_L, S5_N, S5_P
    w = L * LANES
    div = lambda x, d: x >> (d.bit_length() - 1)
    mod = lambda x, d: x & (d - 1)
    one = lambda m: jnp.where(m, 1.0, 0.0).astype(BF16)
    r1 = lax.broadcasted_iota(I32, (LANES, LANES), 0)
    c1 = lax.broadcasted_iota(I32, (LANES, LANES), 1)
    first_half = c1 < N
    sgn = jnp.where(first_half, -1.0, 1.0)

    e_maps, taps = [], []
    same_group = div(r1, P) == div(c1, P)
    fold = one(mod(r1, P) == mod(c1, P))
    for d in range(2):
        ar, ai, fr, fi = _s5_disc(rowp_ref[0, d, 0], rowp_ref[1, d, 0], rowp_ref[2, d, 0])
        bx1 = bt_ref[d, 0]
        bx2 = pltpu.roll(bx1, N, 1)
        x1 = fr * bx1 + (fi * sgn) * bx2
        x2 = pltpu.roll(x1, N, 1)
        e_d, t_d = [], []
        for pr, pi in _s5_powers(ar, ai, L):
            e_k = pr * x1 + (pi * sgn) * x2
            e_d.append(e_k)
            full = _dot3(e_k, cm_ref[d, 0])
            kept = jnp.where(same_group, full, 0.0)
            hi = kept.astype(BF16)
            mid = (kept - hi.astype(F32)).astype(BF16)
            lo = (kept - hi.astype(F32) - mid.astype(F32)).astype(BF16)
            t_d.append(sum(jnp.dot(part, fold, preferred_element_type=F32) for part in (hi, mid, lo)))
        e_maps.append(e_d)
        taps.append(t_d)

    r = lax.broadcasted_iota(I32, (LANES, w), 0)
    c = lax.broadcasted_iota(I32, (LANES, w), 1)
    t_tap = one((div(r, P) == div(c, LANES)) & (mod(r, P) == mod(c, P)))
    m_tap = div(r, P) == div(mod(c, LANES), P)
    t_st = one((div(r, N) == div(c, 8 * N)) & (mod(r, N) == mod(c, N)))
    m_st = div(r, P) == div(mod(c, 8 * N), N)
    col_blk = div(c1, P)
    for j in range(L):
        rows = slice(j * LANES, (j + 1) * LANES)
        toe = jnp.zeros((LANES, LANES), F32)
        for k in range(L):
            toe = (toe + jnp.where(col_blk == j + k, taps[0][k], 0.0)
                   + jnp.where(col_blk == j - k, taps[1][k], 0.0))
        tap = jnp.dot(toe.astype(BF16), t_tap, preferred_element_type=F32)
        wts_ref[0, rows, 0:w] = jnp.where(m_tap, tap, 0.0).astype(BF16)
        for k, e_j in enumerate((e_maps[0][L - 1 - j], e_maps[1][j])):
            st = jnp.dot(e_j.astype(BF16), t_st, preferred_element_type=F32)
            wts_ref[0, rows, w + k * w:w + (k + 1) * w] = jnp.where(m_st, st, 0.0).astype(BF16)

    rr = lax.broadcasted_iota(I32, (w, LANES), 0)
    cc = lax.broadcasted_iota(I32, (w, LANES), 1)
    t_row = one((div(rr, 8 * N) == div(cc, N)) & (mod(rr, N) == mod(cc, N)))
    r2 = lax.broadcasted_iota(I32, (w, w), 0)
    c2 = lax.broadcasted_iota(I32, (w, w), 1)
    m_row = div(mod(r2, 8 * N), N) == div(mod(c2, LANES), P)
    for d in range(2):
        ar, ai, _, _ = _s5_disc(colp_ref[0, d, 0], colp_ref[1, d, 0], colp_ref[2, d, 0])
        a1 = cm_ref[d, 0]
        swapped = pltpu.roll(a1, N, 0)
        a2 = jnp.where(r1 < N, swapped, -swapped)
        pw = _s5_powers(ar, ai, L)
        ks = [i + 1 for i in range(L)] if d == 0 else [L - i for i in range(L)]
        g = jnp.concatenate([a1 * pw[k][0] + a2 * pw[k][1] for k in ks], axis=1)
        corr = jnp.dot(t_row, g.astype(BF16), preferred_element_type=F32)
        wc_ref[0, d * w:(d + 1) * w, :] = jnp.where(m_row, corr, 0.0).astype(BF16)

    q = []
    for d in range(2):
        ar, ai, _, _ = _s5_disc(flatp_ref[0, d, 0], flatp_ref[1, d, 0], flatp_ref[2, d, 0])
        q.append(_s5_powers(ar, ai, L)[L])
    for m in range(8):
        ap_ref[0, m:m + 1, :] = jnp.concatenate([q[0][0], q[0][1], q[1][0], q[1][1]], axis=1)
        q = [(qr * qr - qi * qi, 2.0 * qr * qi) for qr, qi in q]


def _s5_weights(lam_re, lam_im, log_step, b_re, b_im, c_re, c_im):
    L, G, N, P = S5_L, S5_G, S5_N, S5_P
    w = L * LANES
    step = jnp.broadcast_to(jnp.exp(log_step)[..., None], lam_re.shape)
    prm = jnp.stack([lam_re, lam_im, step])
    rowp = jnp.broadcast_to(prm[:, :, :, None, None, :], (3, 2, G, P, 2, N)).reshape(3, 2, S5_BLK, LANES, 2 * N)
    colp = jnp.broadcast_to(prm.transpose(0, 1, 3, 2)[:, :, None, :, :, None], (3, 2, 2, N, G, P))
    colp = colp.reshape(3, 2, 2 * N, S5_BLK, LANES).transpose(0, 1, 3, 2, 4)
    flatp = prm.reshape(3, 2, S5_BLK, 1, 8 * N)
    bt = jnp.stack([b_re, b_im], axis=3).transpose(0, 1, 4, 3, 2)
    bt = bt.reshape(2, S5_BLK, LANES, 2 * N)
    cm = jnp.stack([c_re, -c_im], axis=1).transpose(0, 1, 4, 2, 3)
    cm = cm.reshape(2, 2 * N, S5_BLK, LANES).transpose(0, 2, 1, 3)
    blk = lambda lead, a, b_: pl.BlockSpec(lead + (1, a, b_), lambda i: (0,) * len(lead) + (i, 0, 0))
    out = lambda a, b_: pl.BlockSpec((1, a, b_), lambda i: (i, 0, 0))
    return pl.pallas_call(
        _s5_prep_body, grid=(S5_BLK,),
        in_specs=[blk((3, 2), LANES, LANES), blk((3, 2), LANES, LANES), blk((3, 2), 1, 8 * N),
                  blk((2,), LANES, LANES), blk((2,), LANES, LANES)],
        out_specs=[out(w, w + S5_SW), out(S5_SW, w), out(8, S5_SW)],
        out_shape=[jax.ShapeDtypeStruct((S5_BLK, w, w + S5_SW), BF16),
                   jax.ShapeDtypeStruct((S5_BLK, S5_SW, w), BF16),
                   jax.ShapeDtypeStruct((S5_BLK, 8, S5_SW), F32)],
        compiler_params=_cp(("parallel",)), name="s5_prep",
    )(rowp, colp, flatp, bt, cm)


def _mixout_body(x_ref, mod_ref, g1_ref, g2_ref, wbg_ref, bbg_ref, ya_ref, y4_ref, u4_ref,
                 dsk_ref, wglu_ref, bglu_ref, wso_ref, wlo_ref, wo_ref, wrt_ref, brt_ref,
                 x1_ref, h2_ref, aff_ref):
    m = mod_ref[0]
    sh1, sc1, gt1 = m[:, 0:D_MODEL], m[:, D_MODEL:2 * D_MODEL], m[:, 2 * D_MODEL:3 * D_MODEL]
    sh2, sc2 = m[:, 3 * D_MODEL:4 * D_MODEL], m[:, 4 * D_MODEL:5 * D_MODEL]
    x = x_ref[...]
    h = _norm_mod(x, g1_ref[...], sc1, sh1)
    gates = _sigmoid(jnp.dot(h.astype(BF16), wbg_ref[...], preferred_element_type=F32) + bbg_ref[...])

    def assemble(ref):
        return jnp.concatenate([ref[k] for k in range(S5_BLK)], axis=1)

    ys = assemble(y4_ref) + dsk_ref[...] * assemble(u4_ref)
    v = _gelu(ys)
    ob = v * _sigmoid(jnp.dot(v.astype(BF16), wglu_ref[...], preferred_element_type=F32) + bglu_ref[...])
    merged = (gates[:, :D_MODEL] * jnp.dot(ya_ref[...], wlo_ref[...], preferred_element_type=F32)
              + gates[:, D_MODEL:] * jnp.dot(ob.astype(BF16), wso_ref[...], preferred_element_type=F32))
    mix = jnp.dot(merged.astype(BF16), wo_ref[...], preferred_element_type=F32)
    x1 = x + gt1 * mix
    x1_ref[...] = x1
    h2 = _norm_mod(x1, g2_ref[...], sc2, sh2)
    h2_ref[...] = h2
    logits = _dot3(wrt_ref[...], h2, dims=(((1,), (1,)), ((), ()))) + brt_ref[...]
    mx = jnp.max(logits, axis=0, keepdims=True)
    ex = jnp.exp(logits - mx)
    aff_ref[...] = ex / jnp.sum(ex, axis=0, keepdims=True)


def _mixout_call(x2d, mod3, g1, g2, wbg, bbg, ya, y4, u4, dsk, wglu, bglu, wso, wlo, wo, wrt, brt,
                 mod_row, tm):
    n = x2d.shape[0]
    full = lambda shape: pl.BlockSpec(shape, lambda i: (0,) * len(shape))
    s5_spec = pl.BlockSpec((S5_BLK, tm, LANES), lambda i: (0, i, 0))
    tok = lambda w: pl.BlockSpec((tm, w), lambda i: (i, 0))
    return pl.pallas_call(
        _mixout_body,
        grid=(n // tm,),
        in_specs=[tok(D_MODEL), pl.BlockSpec((1, 1, 6 * D_MODEL), lambda i: (mod_row(i), 0, 0)),
                  full((1, D_MODEL)), full((1, D_MODEL)), full((D_MODEL, 2 * D_MODEL)),
                  full((1, 2 * D_MODEL)), tok(D_LRU), s5_spec, s5_spec, full((1, D_S5)),
                  full((D_S5, D_S5)), full((1, D_S5)), full((D_S5, D_MODEL)), full((D_LRU, D_MODEL)),
                  full((D_MODEL, D_MODEL)), full((N_EXPERTS, D_MODEL)), full((N_EXPERTS, 1))],
        out_specs=[tok(D_MODEL), tok(D_MODEL), pl.BlockSpec((N_EXPERTS, tm), lambda i: (0, i))],
        out_shape=[jax.ShapeDtypeStruct((n, D_MODEL), F32), jax.ShapeDtypeStruct((n, D_MODEL), F32),
                   jax.ShapeDtypeStruct((N_EXPERTS, n), F32)],
        compiler_params=_cp(("parallel",)), name="mixout",
    )(x2d, mod3, g1, g2, wbg, bbg, ya, y4, u4, dsk, wglu, bglu, wso, wlo, wo, wrt, brt)


def _select_body(aff_ref, gate_ref, pos_ref, offs_ref, *, n_tok, cap):
    aff = aff_ref[...]
    capf = float(cap)

    def bis(_, lh):
        lo, hi = lh
        mid = lo + ((hi - lo + 1) >> 1)
        cnt = jnp.sum(jnp.where(aff >= pltpu.bitcast(mid, F32), 1.0, 0.0), axis=1, keepdims=True)
        ok = cnt >= capf
        return jnp.where(ok, mid, lo), jnp.where(ok, hi, mid - 1)

    lo0 = jnp.zeros((N_EXPERTS, 1), I32)
    hi0 = jnp.full((N_EXPERTS, 1), 0x7F800000, I32)
    thr_bits, _ = lax.fori_loop(0, 31, bis, (lo0, hi0))
    thr = pltpu.bitcast(thr_bits, F32)
    need = capf - jnp.sum(jnp.where(aff > thr, 1.0, 0.0), axis=1, keepdims=True)

    r = lax.broadcasted_iota(I32, (TOK_TILE, TOK_TILE), 0)
    c = lax.broadcasted_iota(I32, (TOK_TILE, TOK_TILE), 1)
    tri = jnp.where(r < c, 1.0, 0.0).astype(BF16)
    lane = lax.broadcasted_iota(I32, (N_EXPERTS, LANES), 1)
    n_tiles = n_tok // TOK_TILE
    run_eq = jnp.zeros((N_EXPERTS, 1), F32)
    run_sel = jnp.zeros((N_EXPERTS, 1), F32)
    offs = jnp.zeros((N_EXPERTS, LANES), F32)
    for t in range(n_tiles):
        sl = slice(t * TOK_TILE, (t + 1) * TOK_TILE)
        aff_t = aff_ref[:, sl]
        is_eq = aff_t == thr
        eq_t = jnp.where(is_eq, 1.0, 0.0)
        rank_eq = jnp.dot(eq_t.astype(BF16), tri, preferred_element_type=F32) + run_eq
        cand = (aff_t > thr) | (is_eq & (rank_eq < need))
        pos_t = jnp.dot(jnp.where(cand, 1.0, 0.0).astype(BF16), tri, preferred_element_type=F32) + run_sel
        sel_t = cand & (pos_t < capf)
        sel_f = jnp.where(sel_t, 1.0, 0.0)
        offs = jnp.where(lane == t, run_sel, offs)
        gate_ref[:, sl] = jnp.where(sel_t, aff_t, 0.0)
        pos_ref[:, sl] = jnp.where(sel_t, pos_t, -1.0).astype(I32)
        run_eq = run_eq + jnp.sum(eq_t, axis=1, keepdims=True)
        run_sel = run_sel + jnp.sum(sel_f, axis=1, keepdims=True)
    offs = jnp.where(lane >= n_tiles, run_sel, offs)
    offs_ref[...] = offs.astype(I32)


def _select_call(aff_t, cap):
    n_tok = aff_t.shape[1]
    full = lambda shape: pl.BlockSpec(shape, lambda i: (0,) * len(shape))
    return pl.pallas_call(
        functools.partial(_select_body, n_tok=n_tok, cap=cap),
        grid=(1,),
        in_specs=[full((N_EXPERTS, n_tok))],
        out_specs=[full((N_EXPERTS, n_tok)), full((N_EXPERTS, n_tok)), full((N_EXPERTS, LANES))],
        out_shape=[jax.ShapeDtypeStruct((N_EXPERTS, n_tok), F32),
                   jax.ShapeDtypeStruct((N_EXPERTS, n_tok), I32),
                   jax.ShapeDtypeStruct((N_EXPERTS, LANES), I32)],
        compiler_params=_cp(("arbitrary",)), name="select",
    )(aff_t)


def _invert_body(offs_ref, pos_ref, gate_ref, idx_ref, gcol_ref, acc_i, acc_g, *, n_tiles, cap):
    base = pl.program_id(0) * LANES
    acc_i[...] = jnp.full(acc_i.shape, -1.0, F32)
    acc_g[...] = jnp.zeros(acc_g.shape, F32)
    jrow = lax.broadcasted_iota(I32, (SLOT_CHUNK, LANES), 0)
    lane = lax.broadcasted_iota(I32, (1, LANES), 1)

    def per_tile(t, _):
        first = (offs_ref[base + t] >> 3) << 3
        prow = pos_ref[0, pl.ds(t, 1), :]
        grow = gate_ref[0, pl.ds(t, 1), :]
        tok = (lane + t * TOK_TILE).astype(F32)

        def window(w, _):
            start = pl.multiple_of(first + w * SLOT_CHUNK, SUBLANES)
            rows = pl.ds(start, SLOT_CHUNK)
            hit = prow == (jrow + start)
            acc_i[rows, :] = jnp.maximum(acc_i[rows, :], jnp.where(hit, tok, -1.0))
            acc_g[rows, :] = jnp.maximum(acc_g[rows, :], jnp.where(hit, grow, 0.0))
            return 0

        n_win = (offs_ref[base + t + 1] - first + SLOT_CHUNK - 1) >> (SLOT_CHUNK.bit_length() - 1)
        lax.fori_loop(0, n_win, window, 0)
        return 0

    lax.fori_loop(0, n_tiles, per_tile, 0)
    idx_ref[0] = jnp.max(acc_i[0:cap, :], axis=1, keepdims=True).astype(I32)
    gcol_ref[0] = jnp.max(acc_g[0:cap, :], axis=1, keepdims=True)


def _invert_call(offs, pos, gate, cap):
    n_tok = pos.shape[1]
    n_tiles = n_tok // TOK_TILE
    pos3 = pos.reshape(N_EXPERTS, n_tiles, TOK_TILE)
    gate3 = gate.reshape(N_EXPERTS, n_tiles, TOK_TILE)
    per_e = lambda shape: pl.BlockSpec(shape, lambda e, o: (e, 0, 0))
    return pl.pallas_call(
        functools.partial(_invert_body, n_tiles=n_tiles, cap=cap),
        grid_spec=pltpu.PrefetchScalarGridSpec(
            num_scalar_prefetch=1, grid=(N_EXPERTS,),
            in_specs=[per_e((1, n_tiles, TOK_TILE)), per_e((1, n_tiles, TOK_TILE))],
            out_specs=[per_e((1, cap, 1)), per_e((1, cap, 1))],
            scratch_shapes=[pltpu.VMEM((cap + SLOT_CHUNK, LANES), F32)] * 2),
        out_shape=[jax.ShapeDtypeStruct((N_EXPERTS, cap, 1), I32),
                   jax.ShapeDtypeStruct((N_EXPERTS, cap, 1), F32)],
        compiler_params=_cp(("parallel",)), name="invert",
    )(offs.reshape(-1), pos3, gate3)


def _ffn_body(idx0_ref, idxn_ref, gcol_ref, wg_ref, wu_ref, wd_ref, h_hbm, out_ref,
              xe, sem, *, cap):
    e = pl.program_id(0)
    slot = e % 2
    nxt = 1 - slot

    def issue_rows(idx_ref, s, j0, n):
        for j in range(j0, j0 + n):
            pltpu.make_async_copy(h_hbm.at[pl.ds(idx_ref[0, 0, j], 1), :],
                                  xe.at[s, pl.ds(j, 1), :], sem.at[s]).start()

    @pl.when(e == 0)
    def _():
        def issue(j, _):
            pltpu.make_async_copy(h_hbm.at[pl.ds(idx0_ref[0, 0, j], 1), :],
                                  xe.at[0, pl.ds(j, 1), :], sem.at[0]).start()
            return 0
        lax.fori_loop(0, cap, issue, 0, unroll=8)

    wgb = wg_ref[0].astype(BF16)
    wub = wu_ref[0].astype(BF16)
    wdb = wd_ref[0].astype(BF16)

    pltpu.make_async_copy(h_hbm.at[pl.ds(0, cap), :], xe.at[slot], sem.at[slot]).wait()

    ch = 256
    n_ch = cap // ch
    per = -(-cap // (n_ch - 1))
    for c in range(n_ch):
        sl = slice(c * ch, (c + 1) * ch)
        xb = xe[slot, sl, :].astype(BF16)
        issue_rows(idxn_ref, nxt, min(c * per, cap), min((c + 1) * per, cap) - min(c * per, cap))
        g = jnp.dot(xb, wgb, preferred_element_type=F32)
        u = jnp.dot(xb, wub, preferred_element_type=F32)
        hid = (g * _sigmoid(g)) * u
        ye = jnp.dot(hid.astype(BF16), wdb, preferred_element_type=F32)
        out_ref[0, sl, :] = ye * gcol_ref[0, sl, :]

    @pl.when(e == N_EXPERTS - 1)
    def _():
        pltpu.make_async_copy(h_hbm.at[pl.ds(0, cap), :], xe.at[nxt], sem.at[nxt]).wait()


def _ffn_call(idx, gcol, w_gate, w_up, w_down, h2, cap):
    d_e = w_gate.shape[2]
    wspec = lambda a, b: pl.BlockSpec((1, a, b), lambda e: (e, 0, 0))
    return pl.pallas_call(
        functools.partial(_ffn_body, cap=cap),
        grid=(N_EXPERTS,),
        in_specs=[pl.BlockSpec((1, 1, cap), lambda e: (0, 0, 0), memory_space=pltpu.SMEM),
                  pl.BlockSpec((1, 1, cap), lambda e: (jnp.minimum(e + 1, N_EXPERTS - 1), 0, 0),
                               memory_space=pltpu.SMEM),
                  pl.BlockSpec((1, cap, 1), lambda e: (e, 0, 0)),
                  wspec(D_MODEL, d_e), wspec(D_MODEL, d_e), wspec(d_e, D_MODEL),
                  pl.BlockSpec(memory_space=pl.ANY)],
        out_specs=pl.BlockSpec((1, cap, D_MODEL), lambda e: (e, 0, 0)),
        out_shape=jax.ShapeDtypeStruct((N_EXPERTS, cap, D_MODEL), F32),
        scratch_shapes=[pltpu.VMEM((2, cap, D_MODEL), F32), pltpu.SemaphoreType.DMA((2,))],
        compiler_params=_cp(("arbitrary",)), name="ffn",
    )(idx, idx, gcol, w_gate, w_up, w_down, h2)


def _combine_body(offs_ref, x1_ref, mod_ref, gf_ref, pos_ref, yb_hbm, y_ref, wins, acc_ref, sem, *, cap):
    t = pl.program_id(0)
    n_t = pl.num_programs(0)
    slot = t % 2

    def kmax_of(tt):
        k = 0
        for e in range(N_EXPERTS):
            k = jnp.maximum(k, offs_ref[e * LANES + tt + 1] - offs_ref[e * LANES + tt])
        return k

    def geometry(tt, e, w, win):
        fetch = win + SUBLANES
        lo = offs_ref[e * LANES + tt] + w * win
        start = pl.multiple_of(jnp.minimum((lo >> 3) << 3, cap - fetch), SUBLANES)
        return lo, start, fetch

    def issue(tt, w, s, win):
        for e in range(N_EXPERTS):
            _, start, fetch = geometry(tt, e, w, win)
            pltpu.make_async_copy(yb_hbm.at[e, pl.ds(start, fetch), :],
                                  wins.at[s, pl.ds(e * fetch, fetch), :], sem.at[s]).start()

    def drain(s, win):
        rows = N_EXPERTS * (win + SUBLANES)
        pltpu.make_async_copy(yb_hbm.at[0, pl.ds(0, rows), :], wins.at[s, pl.ds(0, rows), :],
                              sem.at[s]).wait()

    def expand(w, s, win):
        sub = lax.broadcasted_iota(I32, (win + SUBLANES, TOK_TILE), 0)
        rows = []
        for e in range(N_EXPERTS):
            lo, start, _ = geometry(t, e, w, win)
            pos = pos_ref[e:e + 1, :]
            in_round = jnp.logical_and(pos >= lo, pos < lo + win)
            rows.append(jnp.where(jnp.logical_and(pos - start == sub, in_round), 1.0, 0.0))
        onehot = jnp.concatenate(rows, axis=0).T.astype(BF16)
        data = wins[s, 0:N_EXPERTS * (win + SUBLANES), :]
        hi = data.astype(BF16)
        lo_part = (data - hi.astype(F32)).astype(BF16)
        return (jnp.dot(onehot, hi, preferred_element_type=F32)
                + jnp.dot(onehot, lo_part, preferred_element_type=F32))

    def issue_first(tt, s):
        wide = kmax_of(tt) > COMBINE_WIN

        @pl.when(wide)
        def _():
            issue(tt, 0, s, COMBINE_WIN_WIDE)

        @pl.when(jnp.logical_not(wide))
        def _():
            issue(tt, 0, s, COMBINE_WIN)

    @pl.when(t == 0)
    def _():
        issue_first(0, 0)

    @pl.when(t + 1 < n_t)
    def _():
        issue_first(t + 1, 1 - slot)

    kmax = kmax_of(t)

    @pl.when(kmax > COMBINE_WIN)
    def _():
        drain(slot, COMBINE_WIN_WIDE)
        acc_ref[...] = expand(0, slot, COMBINE_WIN_WIDE)

    @pl.when(kmax <= COMBINE_WIN)
    def _():
        drain(slot, COMBINE_WIN)
        acc_ref[...] = expand(0, slot, COMBINE_WIN)

    def more(w, _):
        issue(t, w, slot, COMBINE_WIN_WIDE)
        drain(slot, COMBINE_WIN_WIDE)
        acc_ref[...] += expand(w, slot, COMBINE_WIN_WIDE)
        return 0
    lax.fori_loop(1, (kmax + COMBINE_WIN_WIDE - 1) >> (COMBINE_WIN_WIDE.bit_length() - 1), more, 0)
    acc = acc_ref[...]

    gt2 = mod_ref[0][:, 5 * D_MODEL:6 * D_MODEL]
    x2 = x1_ref[...] + gt2 * acc
    ms = jnp.mean(x2 * x2, axis=-1, keepdims=True)
    y_ref[...] = x2 * lax.rsqrt(ms + EPS) * gf_ref[...]


def _combine_call(offs, pos, x1, mod3, gf, ybuf, mod_row, cap):
    n = x1.shape[0]
    return pl.pallas_call(
        functools.partial(_combine_body, cap=cap),
        grid_spec=pltpu.PrefetchScalarGridSpec(
            num_scalar_prefetch=1, grid=(n // TOK_TILE,),
            in_specs=[pl.BlockSpec((TOK_TILE, D_MODEL), lambda i, o: (i, 0)),
                      pl.BlockSpec((1, 1, 6 * D_MODEL), lambda i, o: (mod_row(i), 0, 0)),
                      pl.BlockSpec((1, D_MODEL), lambda i, o: (0, 0)),
                      pl.BlockSpec((N_EXPERTS, TOK_TILE), lambda i, o: (0, i)),
                      pl.BlockSpec(memory_space=pl.ANY)],
            out_specs=pl.BlockSpec((TOK_TILE, D_MODEL), lambda i, o: (i, 0)),
            scratch_shapes=[pltpu.VMEM((2, N_EXPERTS * (COMBINE_WIN_WIDE + SUBLANES), D_MODEL), F32),
                            pltpu.VMEM((TOK_TILE, D_MODEL), F32),
                            pltpu.SemaphoreType.DMA((2,))]),
        out_shape=jax.ShapeDtypeStruct((n, D_MODEL), F32),
        compiler_params=_cp(("arbitrary",)), name="combine",
    )(offs.reshape(-1), x1, mod3, gf, pos, ybuf)


def _lru_gate_weights(wa, wx):
    eye = jnp.eye(4, dtype=F32)

    def bd(w):
        w5 = w.reshape(4, 4, LRU_HEAD_DIM, LRU_HEAD_DIM)
        return jnp.einsum('khij,hg->khigj', w5, eye).reshape(4, 256, 256)

    return jnp.concatenate([bd(wa[0]), bd(wa[1]), bd(wx[0]), bd(wx[1])], axis=2).astype(BF16)


def kernel(x_prompt, x_sample, state_lru, state_s5_re, state_s5_im, c, c_ctx, w_mod, b_mod, g_norm1, g_norm2, w_in, conv_w, conv_b, lru_wa, lru_ba, lru_wx, lru_bx, lru_lambda, s5_lambda_re, s5_lambda_im, s5_log_step, s5_b_re, s5_b_im, s5_c_re, s5_c_im, s5_d, s5_w_glu, s5_b_glu, w_lru_out, w_s5_out, w_branch_gate, b_branch_gate, w_o, w_router, b_router, w_e_gate, w_e_up, w_e_down, g_final):
    bp, tp, _ = x_prompt.shape
    bs, ts, _ = x_sample.shape
    n_p, n_s = bp * tp, bs * ts
    l = 0

    c_all = jnp.zeros((16, D_MODEL), F32).at[0].set(c_ctx).at[1:1 + bs].set(c)
    mod3 = _mod_call(c_all, w_mod[l], b_mod[l][None, :]).reshape(16, 1, 6 * D_MODEL)

    w_in_b = w_in[l].astype(BF16)
    w_main, w_s5in = w_in_b[:, :2 * D_LRU], w_in_b[:, 2 * D_LRU:]
    g1 = g_norm1[l][None, :]
    g2 = g_norm2[l][None, :]
    wg = _lru_gate_weights(lru_wa[l], lru_wx[l])
    wts, wc, apow = _s5_weights(s5_lambda_re[l], s5_lambda_im[l], s5_log_step[l],
                                s5_b_re[l], s5_b_im[l], s5_c_re[l], s5_c_im[l])
    wbg = w_branch_gate[l].astype(BF16)
    bbg = b_branch_gate[l][None, :]
    wglu = s5_w_glu[l].astype(BF16)
    wso = w_s5_out[l].astype(BF16)
    wlo = w_lru_out[l].astype(BF16)
    wo = w_o[l].astype(BF16)
    wrt = w_router[l].T
    brt = b_router[l][:, None]

    xp2 = x_prompt.reshape(n_p, D_MODEL)
    xs2 = x_sample.reshape(n_s, D_MODEL)
    ctx_row = lambda i: 0
    tm_in = 512
    lat_row_in = lambda i: 1 + i // (ts // max(tm_in, ts))

    def s5_state(re, im):
        def part(a, d):
            return a[:, d].reshape(-1, S5_BLK, 8 * S5_N).transpose(1, 0, 2)
        return jnp.concatenate([part(re, 0), part(im, 0), part(re, 1), part(im, 1)], axis=2)

    xb_p, gg_p, us4_p = _inproj_call(xp2, mod3, g1, w_main, w_s5in, ctx_row, max(tm_in, tp), tp)
    ya_p, lru_fin = _lru_call(xb_p, gg_p, conv_w[l], conv_b[l][None, :], wg, lru_ba[l], lru_bx[l],
                              lru_lambda[l], jnp.zeros((bp, 2, D_LRU), F32), bp, tp)
    y4_p, s5_fin = _s5_call(us4_p, wts, wc, apow, jnp.zeros((S5_BLK, bp, S5_SW), F32), bp, tp,
                            with_h0=False, col_major=False)
    xb_s, gg_s, us4_s = _inproj_call(xs2, mod3, g1, w_main, w_s5in, lat_row_in, max(tm_in, ts), ts)
    ya_s, _ = _lru_call(xb_s, gg_s, conv_w[l], conv_b[l][None, :], wg, lru_ba[l], lru_bx[l],
                        lru_lambda[l], state_lru[:, l], bs, ts)
    y4_s, _ = _s5_call(us4_s, wts, wc, apow, s5_state(state_s5_re[:, l], state_s5_im[:, l]), bs, ts,
                       with_h0=True, col_major=True)

    mo_args = (wbg, bbg)
    mo_tail = (s5_d[l][None, :], wglu, s5_b_glu[l][None, :], wso, wlo, wo, wrt, brt)
    tm_mo = 512
    x1_p, h2_p, aff_p = _mixout_call(xp2, mod3, g1, g2, *mo_args, ya_p, y4_p, us4_p, *mo_tail,
                                     mod_row=ctx_row, tm=tm_mo)
    x1_s, h2_s, aff_s = _mixout_call(xs2, mod3, g1, g2, *mo_args, ya_s, y4_s, us4_s, *mo_tail,
                                     mod_row=lambda i: 1 + i // (ts // tm_mo), tm=tm_mo)

    cap = (CAPACITY_FACTOR * n_p) // N_EXPERTS
    gf = g_final[None, :]
    ys = []
    for aff, h2, x1, mod_row in ((aff_p, h2_p, x1_p, ctx_row),
                                 (aff_s, h2_s, x1_s, lambda i: 1 + i // (ts // TOK_TILE))):
        gate, pos, offs = _select_call(aff, cap)
        idx, gcol = _invert_call(offs, pos, gate, cap)
        ybuf = _ffn_call(idx.reshape(N_EXPERTS, 1, cap), gcol, w_e_gate[l], w_e_up[l], w_e_down[l], h2, cap)
        ys.append(_combine_call(offs, pos, x1, mod3, gf, ybuf, mod_row, cap))
    y_p, y_s = ys

    new_lru = lru_fin[:, None]
    sf = s5_fin.transpose(1, 0, 2).reshape(bp, S5_BLK, 2, 2, 8, S5_N)
    sf = sf.transpose(0, 2, 3, 1, 4, 5).reshape(bp, 2, 2, S5_G, S5_N)
    new_s5r = sf[:, :, 0][:, None]
    new_s5i = sf[:, :, 1][:, None]
    return (y_p.reshape(bp, tp, D_MODEL), y_s.reshape(bs, ts, D_MODEL), new_lru, new_s5r, new_s5i)
```

```python
import functools
import math

import jax
import jax.numpy as jnp
from jax import lax
from jax.experimental import pallas as pl
from jax.experimental.pallas import tpu as pltpu

F32 = jnp.float32
BF16 = jnp.bfloat16
I32 = jnp.int32

D_MODEL = 1024
D_LRU = 1024
LRU_HEADS = 16
LRU_HEAD_DIM = 64
LRU_C = 8.0
CONV_W = 4
D_S5 = 512
S5_P = 16
S5_G = 32
S5_N = 64
GRID_W = 64
N_EXPERTS = 16
CAPACITY_FACTOR = 2
EPS = 1e-6

LANES = 128
SUBLANES = 8
S5_L = 8
S5_BLK = 4
S5_SW = 4 * 512
TOK_TILE = 128
SLOT_CHUNK = 64
COMBINE_WIN = 32
COMBINE_WIN_WIDE = 64
VMEM_LIMIT = 56 * 1024 * 1024


def _cp(sem, vmem=VMEM_LIMIT):
    return pltpu.CompilerParams(dimension_semantics=sem, vmem_limit_bytes=vmem)


def _bdot(a, b):
    return jnp.dot(a.astype(BF16), b.astype(BF16), preferred_element_type=F32)


def _split2(a):
    hi = a.astype(BF16)
    lo = (a - hi.astype(F32)).astype(BF16)
    return hi, lo


def _dot3(a, b, dims=(((1,), (0,)), ((), ()))):
    ah, al = _split2(a)
    bh, bl = _split2(b)
    d = functools.partial(lax.dot_general, dimension_numbers=dims, preferred_element_type=F32)
    return d(ah, bh) + (d(al, bh) + d(ah, bl))


def _sigmoid(x):
    return 0.5 * jnp.tanh(0.5 * x) + 0.5


def _gelu(x):
    c = math.sqrt(2.0 / math.pi)
    return 0.5 * x * (1.0 + jnp.tanh(c * (x + 0.044715 * (x * x * x))))


def _norm_mod(x, g, scale, shift):
    ms = jnp.mean(x * x, axis=-1, keepdims=True)
    return (x * lax.rsqrt(ms + EPS) * g) * (1.0 + scale) + shift


def _mod_body(c_ref, w_ref, b_ref, o_ref):
    c = c_ref[...]
    s = c * _sigmoid(c)
    o_ref[...] = _dot3(s, w_ref[...]) + b_ref[...]


def _mod_call(c_all, w_mod, b_mod):
    n = w_mod.shape[1]
    tn = 1536
    return pl.pallas_call(
        _mod_body,
        grid=(n // tn,),
        in_specs=[pl.BlockSpec((16, D_MODEL), lambda j: (0, 0)),
                  pl.BlockSpec((D_MODEL, tn), lambda j: (0, j)),
                  pl.BlockSpec((1, tn), lambda j: (0, j))],
        out_specs=pl.BlockSpec((16, tn), lambda j: (0, j)),
        out_shape=jax.ShapeDtypeStruct((16, n), F32),
        compiler_params=_cp(("arbitrary",)),
        name="mod",
    )(c_all, w_mod, b_mod)


def _inproj_body(x_ref, mod_ref, g_ref, w_ref, ws_ref, xb_ref, gg_ref, us_ref, *, seq_len):
    m = mod_ref[0]
    h = _norm_mod(x_ref[...], g_ref[...], m[:, D_MODEL:2 * D_MODEL], m[:, 0:D_MODEL])
    hb = h.astype(BF16)
    z = jnp.dot(hb, w_ref[...], preferred_element_type=F32)
    seg = seq_len // SUBLANES
    for s in range(x_ref.shape[0] // seq_len):
        for k in range(SUBLANES):
            t0 = s * seq_len + k * seg
            for c in range(D_LRU // LANES):
                xb_ref[c, pl.ds(s * seq_len + k, seg, stride=SUBLANES), :] = (
                    z[t0:t0 + seg, c * LANES:(c + 1) * LANES])
    gg_ref[...] = _gelu(z[:, D_LRU:]).astype(BF16)
    us = jnp.dot(hb, ws_ref[...], preferred_element_type=F32)
    for k in range(S5_BLK):
        us_ref[k] = us[:, k * LANES:(k + 1) * LANES]


def _inproj_call(x2d, mod3, g1, w_main, w_s5, mod_row, tm, seq_len):
    n = x2d.shape[0]
    return pl.pallas_call(
        functools.partial(_inproj_body, seq_len=seq_len), grid=(n // tm,),
        in_specs=[pl.BlockSpec((tm, D_MODEL), lambda i: (i, 0)),
                  pl.BlockSpec((1, 1, 6 * D_MODEL), lambda i: (mod_row(i), 0, 0)),
                  pl.BlockSpec((1, D_MODEL), lambda i: (0, 0)),
                  pl.BlockSpec((D_MODEL, 2 * D_LRU), lambda i: (0, 0)),
                  pl.BlockSpec((D_MODEL, D_S5), lambda i: (0, 0))],
        out_specs=[pl.BlockSpec((D_LRU // LANES, tm, LANES), lambda i: (0, i, 0)),
                   pl.BlockSpec((tm, D_LRU), lambda i: (i, 0)),
                   pl.BlockSpec((S5_BLK, tm, LANES), lambda i: (0, i, 0))],
        out_shape=[jax.ShapeDtypeStruct((D_LRU // LANES, n, LANES), F32),
                   jax.ShapeDtypeStruct((n, D_LRU), BF16),
                   jax.ShapeDtypeStruct((S5_BLK, n, LANES), F32)],
        compiler_params=_cp(("parallel",)), name="inproj",
    )(x2d, mod3, g1, w_main, w_s5)


def _lru_body(xb_ref, gg_ref, cw_ref, cb_ref, wg_ref, ba_ref, bx_ref, lam_ref, h0_ref,
              ya_ref, fin_ref, xpad, a_f, b_f, a_b, b_b, hs, *, T):
    seg = T // SUBLANES
    ch = 256
    nch = T // ch
    n_slab = D_LRU // LANES
    pad = 2 * SUBLANES
    sub = lax.broadcasted_iota(I32, (SUBLANES, D_LRU), 0)

    def rows_of(r0, n):
        return jnp.concatenate([xb_ref[c, r0:r0 + n, :] for c in range(n_slab)], axis=1)

    def from_prev_segment(tile):
        return jnp.where(sub >= 1, pltpu.roll(tile, 1, 0), 0.0)

    def from_next_segment(tile):
        return jnp.where(sub < SUBLANES - 1, pltpu.roll(tile, SUBLANES - 1, 0), 0.0)

    xpad[0:SUBLANES, :] = from_prev_segment(rows_of((seg - 2) * SUBLANES, SUBLANES))
    xpad[SUBLANES:pad, :] = from_prev_segment(rows_of((seg - 1) * SUBLANES, SUBLANES))
    xpad[pad + T:pad + T + SUBLANES, :] = from_next_segment(rows_of(0, SUBLANES))
    for c in range(nch):
        xpad[pad + c * ch:pad + (c + 1) * ch, :] = rows_of(c * ch, ch)

    nl = -lam_ref[...]
    softplus = jnp.maximum(nl, 0.0) + jnp.log1p(jnp.exp(-jnp.abs(nl)))
    scr = ((a_f, b_f), (a_b, b_b))

    for c in range(nch):
        t0 = c * ch
        xc = cb_ref[...] + sum(
            xpad[t0 + k * SUBLANES:t0 + k * SUBLANES + ch, :] * cw_ref[k:k + 1, :]
            for k in range(CONV_W))
        xcb = xc.astype(BF16)
        pre = [jnp.dot(xcb[:, kb * 256:(kb + 1) * 256], wg_ref[kb], preferred_element_type=F32)
               for kb in range(4)]
        for d in range(2):
            ra = jnp.concatenate([p[:, d * 256:(d + 1) * 256] for p in pre], axis=1)
            gx = jnp.concatenate([p[:, 512 + d * 256:512 + (d + 1) * 256] for p in pre], axis=1)
            r = _sigmoid(ra + ba_ref[d:d + 1, :])
            gi = _sigmoid(gx + bx_ref[d:d + 1, :])
            log_a = (-LRU_C) * r * softplus[d:d + 1, :]
            a = jnp.exp(log_a)
            scr[d][0][t0:t0 + ch, :] = a
            scr[d][1][t0:t0 + ch, :] = jnp.sqrt(1.0 - a * a) * gi * xc

    def local_scan(a_ref, u_ref, reverse):
        def step(i, carry):
            h, p = carry
            r0 = pl.multiple_of((seg - 1 - i if reverse else i) * SUBLANES, SUBLANES)
            a = a_ref[pl.ds(r0, SUBLANES), :]
            h = a * h + u_ref[pl.ds(r0, SUBLANES), :]
            p = a * p
            u_ref[pl.ds(r0, SUBLANES), :] = h
            a_ref[pl.ds(r0, SUBLANES), :] = p
            return h, p
        init = (jnp.zeros((SUBLANES, D_LRU), F32), jnp.ones((SUBLANES, D_LRU), F32))
        return lax.fori_loop(0, seg, step, init, unroll=2)

    def carry_in(h_end, p_end, h0_row, reverse):
        edge = sub == (SUBLANES - 1 if reverse else 0)
        shift = SUBLANES - 1 if reverse else 1
        c = jnp.broadcast_to(h0_row, (SUBLANES, D_LRU))
        for _ in range(SUBLANES - 1):
            c = jnp.where(edge, h0_row, pltpu.roll(h_end + p_end * c, shift, 0))
        return c

    hf_end, pf_end = local_scan(a_f, b_f, False)
    hb_end, pb_end = local_scan(a_b, b_b, True)
    c_f = carry_in(hf_end, pf_end, h0_ref[0, 0:1, :], False)
    c_b = carry_in(hb_end, pb_end, h0_ref[0, 1:2, :], True)
    fin_ref[0, 0:1, :] = (hf_end + pf_end * c_f)[SUBLANES - 1:SUBLANES, :]
    fin_ref[0, 1:2, :] = (hb_end + pb_end * c_b)[0:1, :]

    for c in range(nch):
        sl = slice(c * ch, (c + 1) * ch)
        tile3 = (ch // SUBLANES, SUBLANES, D_LRU)
        hsum = ((b_f[sl, :].reshape(tile3) + a_f[sl, :].reshape(tile3) * c_f[None])
                + (b_b[sl, :].reshape(tile3) + a_b[sl, :].reshape(tile3) * c_b[None])).reshape(ch, D_LRU)
        for j in range(n_slab):
            hs[j, sl, :] = hsum[:, j * LANES:(j + 1) * LANES]

    for k in range(SUBLANES):
        rows = slice(k * seg, (k + 1) * seg)
        h_seg = jnp.concatenate([hs[j, pl.ds(k, seg, stride=SUBLANES), :] for j in range(n_slab)], axis=1)
        ya_ref[0, rows, :] = (gg_ref[0, rows, :].astype(F32) * h_seg).astype(BF16)


def _lru_call(xb8, gg, conv_w, conv_b, wg, ba, bx, lam, h0, n_seq, T):
    gg3 = gg.reshape(n_seq, T, D_LRU)
    n_slab = D_LRU // LANES
    full = lambda shape: pl.BlockSpec(shape, lambda i: (0,) * len(shape))
    seq = lambda shape: pl.BlockSpec(shape, lambda i: (i,) + (0,) * (len(shape) - 1))
    ya, fin = pl.pallas_call(
        functools.partial(_lru_body, T=T),
        grid=(n_seq,),
        in_specs=[pl.BlockSpec((n_slab, T, LANES), lambda i: (0, i, 0)), seq((1, T, D_LRU)),
                  full((CONV_W, D_LRU)), full((1, D_LRU)),
                  full((4, 256, 1024)), full((2, D_LRU)), full((2, D_LRU)), full((2, D_LRU)),
                  seq((1, 2, D_LRU))],
        out_specs=[seq((1, T, D_LRU)), seq((1, 2, D_LRU))],
        out_shape=[jax.ShapeDtypeStruct((n_seq, T, D_LRU), BF16),
                   jax.ShapeDtypeStruct((n_seq, 2, D_LRU), F32)],
        scratch_shapes=[pltpu.VMEM((T + 3 * SUBLANES, D_LRU), F32)] + [pltpu.VMEM((T, D_LRU), F32)] * 4
                       + [pltpu.VMEM((n_slab, T, LANES), F32)],
        compiler_params=_cp(("parallel",)), name="lru",
    )(xb8, gg3, conv_w, conv_b, wg, ba, bx, lam, h0)
    return ya.reshape(n_seq * T, D_LRU), fin


def _s5_body(u_ref, wts_ref, wc_ref, ap_ref, h0_ref, y_ref, fin_ref, s_scr, hp_scr, slab,
             *, rt, n_c, with_h0, col_major):
    nseq = rt // n_c
    q = 512
    seq_tok = n_c * S5_L
    halves = seq_tok // (GRID_W * S5_L)

    def tok_rows(s, h, i):
        return pl.ds(s * seq_tok + (h * S5_L + i) * GRID_W, GRID_W)

    def chunk_rows(s, h):
        return pl.ds(s * n_c + h, GRID_W, stride=halves)

    if col_major:
        for i in range(S5_L):
            for s in range(nseq):
                for h in range(halves):
                    slab[i, chunk_rows(s, h), :] = u_ref[0, tok_rows(s, h, i), :]
        ub = jnp.concatenate([slab[i] for i in range(S5_L)], axis=1).astype(BF16)
    else:
        ub = jnp.concatenate([u_ref[0, pl.ds(i, rt, stride=S5_L), :] for i in range(S5_L)],
                             axis=1).astype(BF16)
    r1 = jnp.dot(ub, wts_ref[0], preferred_element_type=F32)
    y_in = r1[:, :S5_L * LANES]
    s_scr[...] = r1[:, S5_L * LANES:]
    ap = ap_ref[0]
    if with_h0:
        for s in range(nseq):
            h0 = h0_ref[0, 0, s:s + 1, :]
            r_f = s * n_c
            r_b = s * n_c + n_c - 1
            ar, ai = ap[0:1, 0:q], ap[0:1, q:2 * q]
            s_scr[r_f:r_f + 1, 0:q] += ar * h0[:, 0:q] - ai * h0[:, q:2 * q]
            s_scr[r_f:r_f + 1, q:2 * q] += ar * h0[:, q:2 * q] + ai * h0[:, 0:q]
            br, bi = ap[0:1, 2 * q:3 * q], ap[0:1, 3 * q:4 * q]
            s_scr[r_b:r_b + 1, 2 * q:3 * q] += br * h0[:, 2 * q:3 * q] - bi * h0[:, 3 * q:4 * q]
            s_scr[r_b:r_b + 1, 3 * q:4 * q] += br * h0[:, 3 * q:4 * q] + bi * h0[:, 2 * q:3 * q]

    cidx = lax.broadcasted_iota(I32, (rt, 1), 0) % n_c
    hfr, hfi = s_scr[:, 0:q], s_scr[:, q:2 * q]
    hbr, hbi = s_scr[:, 2 * q:3 * q], s_scr[:, 3 * q:4 * q]
    lvl = 0
    dist = 1
    while dist < n_c:
        ar, ai = ap[lvl:lvl + 1, 0:q], ap[lvl:lvl + 1, q:2 * q]
        br, bi = ap[lvl:lvl + 1, 2 * q:3 * q], ap[lvl:lvl + 1, 3 * q:4 * q]
        kf = cidx >= dist
        kb = cidx < n_c - dist
        sfr = jnp.where(kf, pltpu.roll(hfr, dist, 0), 0.0)
        sfi = jnp.where(kf, pltpu.roll(hfi, dist, 0), 0.0)
        sbr = jnp.where(kb, pltpu.roll(hbr, rt - dist, 0), 0.0)
        sbi = jnp.where(kb, pltpu.roll(hbi, rt - dist, 0), 0.0)
        hfr, hfi = hfr + (ar * sfr - ai * sfi), hfi + (ar * sfi + ai * sfr)
        hbr, hbi = hbr + (br * sbr - bi * sbi), hbi + (br * sbi + bi * sbr)
        lvl += 1
        dist *= 2

    s_scr[:, 0:q] = hfr
    s_scr[:, q:2 * q] = hfi
    s_scr[:, 2 * q:3 * q] = hbr
    s_scr[:, 3 * q:4 * q] = hbi
    for s in range(nseq):
        fin_ref[0, 0, s:s + 1, 0:2 * q] = s_scr[s * n_c + n_c - 1:s * n_c + n_c, 0:2 * q]
        fin_ref[0, 0, s:s + 1, 2 * q:4 * q] = s_scr[s * n_c:s * n_c + 1, 2 * q:4 * q]

    kf = cidx >= 1
    kb = cidx < n_c - 1
    hp_scr[:, 0:q] = jnp.where(kf, pltpu.roll(hfr, 1, 0), 0.0)
    hp_scr[:, q:2 * q] = jnp.where(kf, pltpu.roll(hfi, 1, 0), 0.0)
    hp_scr[:, 2 * q:3 * q] = jnp.where(kb, pltpu.roll(hbr, rt - 1, 0), 0.0)
    hp_scr[:, 3 * q:4 * q] = jnp.where(kb, pltpu.roll(hbi, rt - 1, 0), 0.0)
    if with_h0:
        for s in range(nseq):
            hp_scr[s * n_c:s * n_c + 1, 0:2 * q] = h0_ref[0, 0, s:s + 1, 0:2 * q]
            hp_scr[s * n_c + n_c - 1:s * n_c + n_c, 2 * q:4 * q] = h0_ref[0, 0, s:s + 1, 2 * q:4 * q]
    y = y_in + jnp.dot(hp_scr[...].astype(BF16), wc_ref[0], preferred_element_type=F32)
    for i in range(S5_L):
        y_i = y[:, i * LANES:(i + 1) * LANES]
        if col_major:
            slab[i] = y_i
            for s in range(nseq):
                for h in range(halves):
                    y_ref[0, tok_rows(s, h, i), :] = slab[i, chunk_rows(s, h), :]
        else:
            y_ref[0, pl.ds(i, rt, stride=S5_L), :] = y_i


def _s5_call(us4, wts, wc, apow, h0, n_seq, T, with_h0, col_major):
    n_c = T // S5_L
    rows = n_seq * n_c
    rt = 256
    nseq_t = rt // n_c
    h0 = h0.reshape(S5_BLK, n_seq // nseq_t, nseq_t, S5_SW)
    y4, fin = pl.pallas_call(
        functools.partial(_s5_body, rt=rt, n_c=n_c, with_h0=with_h0, col_major=col_major),
        grid=(S5_BLK, rows // rt),
        in_specs=[pl.BlockSpec((1, rt * S5_L, LANES), lambda b, r: (b, r, 0)),
                  pl.BlockSpec((1, S5_L * LANES, S5_L * LANES + S5_SW), lambda b, r: (b, 0, 0)),
                  pl.BlockSpec((1, S5_SW, S5_L * LANES), lambda b, r: (b, 0, 0)),
                  pl.BlockSpec((1, 8, S5_SW), lambda b, r: (b, 0, 0)),
                  pl.BlockSpec((1, 1, nseq_t, S5_SW), lambda b, r: (b, r, 0, 0))],
        out_specs=[pl.BlockSpec((1, rt * S5_L, LANES), lambda b, r: (b, r, 0)),
                   pl.BlockSpec((1, 1, nseq_t, S5_SW), lambda b, r: (b, r, 0, 0))],
        out_shape=[jax.ShapeDtypeStruct((S5_BLK, n_seq * T, LANES), F32),
                   jax.ShapeDtypeStruct((S5_BLK, n_seq // nseq_t, nseq_t, S5_SW), F32)],
        scratch_shapes=[pltpu.VMEM((rt, S5_SW), F32), pltpu.VMEM((rt, S5_SW), F32),
                        pltpu.VMEM((S5_L, rt, LANES), F32)],
        compiler_params=_cp(("parallel", "parallel")), name="s5",
    )(us4, wts, wc, apow, h0)
    return y4, fin.reshape(S5_BLK, n_seq, S5_SW)


def _s5_disc(lr, li, st):
    mag = jnp.exp(lr * st)
    ang = li * st
    ar, ai = mag * jnp.cos(ang), mag * jnp.sin(ang)
    den = lr * lr + li * li
    fr = ((ar - 1.0) * lr + ai * li) / den
    fi = (ai * lr - (ar - 1.0) * li) / den
    return ar, ai, fr, fi


def _s5_powers(ar, ai, n):
    out = [(jnp.ones_like(ar), jnp.zeros_like(ar))]
    for _ in range(n):
        pr, pi = out[-1]
        out.append((pr * ar - pi * ai, pr * ai + pi * ar))
    return out


def _s5_prep_body(rowp_ref, colp_ref, flatp_ref, bt_ref, cm_ref, wts_ref, wc_ref, ap_ref):
    L, N, P = S5_L, S5_N, S5_P
    w = L * LANES
    div = lambda x, d: x >> (d.bit_length() - 1)
    mod = lambda x, d: x & (d - 1)
    one = lambda m: jnp.where(m, 1.0, 0.0).astype(BF16)
    r1 = lax.broadcasted_iota(I32, (LANES, LANES), 0)
    c1 = lax.broadcasted_iota(I32, (LANES, LANES), 1)
    first_half = c1 < N
    sgn = jnp.where(first_half, -1.0, 1.0)

    e_maps, taps = [], []
    same_group = div(r1, P) == div(c1, P)
    fold = one(mod(r1, P) == mod(c1, P))
    for d in range(2):
        ar, ai, fr, fi = _s5_disc(rowp_ref[0, d, 0], rowp_ref[1, d, 0], rowp_ref[2, d, 0])
        bx1 = bt_ref[d, 0]
        bx2 = pltpu.roll(bx1, N, 1)
        x1 = fr * bx1 + (fi * sgn) * bx2
        x2 = pltpu.roll(x1, N, 1)
        e_d, t_d = [], []
        for pr, pi in _s5_powers(ar, ai, L):
            e_k = pr * x1 + (pi * sgn) * x2
            e_d.append(e_k)
            full = _dot3(e_k, cm_ref[d, 0])
            kept = jnp.where(same_group, full, 0.0)
            hi = kept.astype(BF16)
            mid = (kept - hi.astype(F32)).astype(BF16)
            lo = (kept - hi.astype(F32) - mid.astype(F32)).astype(BF16)
            t_d.append(sum(jnp.dot(part, fold, preferred_element_type=F32) for part in (hi, mid, lo)))
        e_maps.append(e_d)
        taps.append(t_d)

    r = lax.broadcasted_iota(I32, (LANES, w), 0)
    c = lax.broadcasted_iota(I32, (LANES, w), 1)
    t_tap = one((div(r, P) == div(c, LANES)) & (mod(r, P) == mod(c, P)))
    m_tap = div(r, P) == div(mod(c, LANES), P)
    t_st = one((div(r, N) == div(c, 8 * N)) & (mod(r, N) == mod(c, N)))
    m_st = div(r, P) == div(mod(c, 8 * N), N)
    col_blk = div(c1, P)
    for j in range(L):
        rows = slice(j * LANES, (j + 1) * LANES)
        toe = jnp.zeros((LANES, LANES), F32)
        for k in range(L):
            toe = (toe + jnp.where(col_blk == j + k, taps[0][k], 0.0)
                   + jnp.where(col_blk == j - k, taps[1][k], 0.0))
        tap = jnp.dot(toe.astype(BF16), t_tap, preferred_element_type=F32)
        wts_ref[0, rows, 0:w] = jnp.where(m_tap, tap, 0.0).astype(BF16)
        for k, e_j in enumerate((e_maps[0][L - 1 - j], e_maps[1][j])):
            st = jnp.dot(e_j.astype(BF16), t_st, preferred_element_type=F32)
            wts_ref[0, rows, w + k * w:w + (k + 1) * w] = jnp.where(m_st, st, 0.0).astype(BF16)

    rr = lax.broadcasted_iota(I32, (w, LANES), 0)
    cc = lax.broadcasted_iota(I32, (w, LANES), 1)
    t_row = one((div(rr, 8 * N) == div(cc, N)) & (mod(rr, N) == mod(cc, N)))
    r2 = lax.broadcasted_iota(I32, (w, w), 0)
    c2 = lax.broadcasted_iota(I32, (w, w), 1)
    m_row = div(mod(r2, 8 * N), N) == div(mod(c2, LANES), P)
    for d in range(2):
        ar, ai, _, _ = _s5_disc(colp_ref[0, d, 0], colp_ref[1, d, 0], colp_ref[2, d, 0])
        a1 = cm_ref[d, 0]
        swapped = pltpu.roll(a1, N, 0)
        a2 = jnp.where(r1 < N, swapped, -swapped)
        pw = _s5_powers(ar, ai, L)
        ks = [i + 1 for i in range(L)] if d == 0 else [L - i for i in range(L)]
        g = jnp.concatenate([a1 * pw[k][0] + a2 * pw[k][1] for k in ks], axis=1)
        corr = jnp.dot(t_row, g.astype(BF16), preferred_element_type=F32)
        wc_ref[0, d * w:(d + 1) * w, :] = jnp.where(m_row, corr, 0.0).astype(BF16)

    q = []
    for d in range(2):
        ar, ai, _, _ = _s5_disc(flatp_ref[0, d, 0], flatp_ref[1, d, 0], flatp_ref[2, d, 0])
        q.append(_s5_powers(ar, ai, L)[L])
    for m in range(8):
        ap_ref[0, m:m + 1, :] = jnp.concatenate([q[0][0], q[0][1], q[1][0], q[1][1]], axis=1)
        q = [(qr * qr - qi * qi, 2.0 * qr * qi) for qr, qi in q]


def _s5_weights(lam_re, lam_im, log_step, b_re, b_im, c_re, c_im):
    L, G, N, P = S5_L, S5_G, S5_N, S5_P
    w = L * LANES
    step = jnp.broadcast_to(jnp.exp(log_step)[..., None], lam_re.shape)
    prm = jnp.stack([lam_re, lam_im, step])
    rowp = jnp.broadcast_to(prm[:, :, :, None, None, :], (3, 2, G, P, 2, N)).reshape(3, 2, S5_BLK, LANES, 2 * N)
    colp = jnp.broadcast_to(prm.transpose(0, 1, 3, 2)[:, :, None, :, :, None], (3, 2, 2, N, G, P))
    colp = colp.reshape(3, 2, 2 * N, S5_BLK, LANES).transpose(0, 1, 3, 2, 4)
    flatp = prm.reshape(3, 2, S5_BLK, 1, 8 * N)
    bt = jnp.stack([b_re, b_im], axis=3).transpose(0, 1, 4, 3, 2)
    bt = bt.reshape(2, S5_BLK, LANES, 2 * N)
    cm = jnp.stack([c_re, -c_im], axis=1).transpose(0, 1, 4, 2, 3)
    cm = cm.reshape(2, 2 * N, S5_BLK, LANES).transpose(0, 2, 1, 3)
    blk = lambda lead, a, b_: pl.BlockSpec(lead + (1, a, b_), lambda i: (0,) * len(lead) + (i, 0, 0))
    out = lambda a, b_: pl.BlockSpec((1, a, b_), lambda i: (i, 0, 0))
    return pl.pallas_call(
        _s5_prep_body, grid=(S5_BLK,),
        in_specs=[blk((3, 2), LANES, LANES), blk((3, 2), LANES, LANES), blk((3, 2), 1, 8 * N),
                  blk((2,), LANES, LANES), blk((2,), LANES, LANES)],
        out_specs=[out(w, w + S5_SW), out(S5_SW, w), out(8, S5_SW)],
        out_shape=[jax.ShapeDtypeStruct((S5_BLK, w, w + S5_SW), BF16),
                   jax.ShapeDtypeStruct((S5_BLK, S5_SW, w), BF16),
                   jax.ShapeDtypeStruct((S5_BLK, 8, S5_SW), F32)],
        compiler_params=_cp(("parallel",)), name="s5_prep",
    )(rowp, colp, flatp, bt, cm)


def _mixout_body(xp_ref, xs_ref, mod_ref, g1_ref, g2_ref, wbg_ref, bbg_ref, yap_ref, yas_ref,
                 y4p_ref, y4s_ref, u4p_ref, u4s_ref, *rest, tiles_p):
    i = pl.program_id(0)

    @pl.when(i < tiles_p)
    def _():
        _mixout_tile(xp_ref, mod_ref, g1_ref, g2_ref, wbg_ref, bbg_ref, yap_ref, y4p_ref, u4p_ref, *rest)

    @pl.when(i >= tiles_p)
    def _():
        _mixout_tile(xs_ref, mod_ref, g1_ref, g2_ref, wbg_ref, bbg_ref, yas_ref, y4s_ref, u4s_ref, *rest)


def _mixout_tile(x_ref, mod_ref, g1_ref, g2_ref, wbg_ref, bbg_ref, ya_ref, y4_ref, u4_ref,
                 dsk_ref, wglu_ref, bglu_ref, wso_ref, wlo_ref, wo_ref, wrt_ref, brt_ref,
                 x1_ref, h2_ref, aff_ref):
    m = mod_ref[0]
    sh1, sc1, gt1 = m[:, 0:D_MODEL], m[:, D_MODEL:2 * D_MODEL], m[:, 2 * D_MODEL:3 * D_MODEL]
    sh2, sc2 = m[:, 3 * D_MODEL:4 * D_MODEL], m[:, 4 * D_MODEL:5 * D_MODEL]
    x = x_ref[...]
    h = _norm_mod(x, g1_ref[...], sc1, sh1)
    gates = _sigmoid(jnp.dot(h.astype(BF16), wbg_ref[...], preferred_element_type=F32) + bbg_ref[...])

    def assemble(ref):
        return jnp.concatenate([ref[k] for k in range(S5_BLK)], axis=1)

    ys = assemble(y4_ref) + dsk_ref[...] * assemble(u4_ref)
    v = _gelu(ys)
    ob = v * _sigmoid(jnp.dot(v.astype(BF16), wglu_ref[...], preferred_element_type=F32) + bglu_ref[...])
    merged = (gates[:, :D_MODEL] * jnp.dot(ya_ref[...], wlo_ref[...], preferred_element_type=F32)
              + gates[:, D_MODEL:] * jnp.dot(ob.astype(BF16), wso_ref[...], preferred_element_type=F32))
    mix = jnp.dot(merged.astype(BF16), wo_ref[...], preferred_element_type=F32)
    x1 = x + gt1 * mix
    x1_ref[...] = x1
    h2 = _norm_mod(x1, g2_ref[...], sc2, sh2)
    h2_ref[...] = h2
    logits = _dot3(wrt_ref[...], h2, dims=(((1,), (1,)), ((), ()))) + brt_ref[...]
    mx = jnp.max(logits, axis=0, keepdims=True)
    ex = jnp.exp(logits - mx)
    aff_ref[...] = ex / jnp.sum(ex, axis=0, keepdims=True)


def _mixout_call(xp, xs, mod3, g1, g2, wbg, bbg, ya_p, ya_s, y4_p, y4_s, u4_p, u4_s,
                 dsk, wglu, bglu, wso, wlo, wo, wrt, brt, mod_row, tm):
    n_p, n_s = xp.shape[0], xs.shape[0]
    tiles_p = n_p // tm
    n = n_p + n_s
    full = lambda shape: pl.BlockSpec(shape, lambda i: (0,) * len(shape))
    at_p = lambda i: jnp.minimum(i, tiles_p - 1)
    at_s = lambda i: jnp.maximum(i - tiles_p, 0)
    tok = lambda w, at: pl.BlockSpec((tm, w), lambda i: (at(i), 0))
    s5 = lambda at: pl.BlockSpec((S5_BLK, tm, LANES), lambda i: (0, at(i), 0))
    out = lambda w: pl.BlockSpec((tm, w), lambda i: (i, 0))
    return pl.pallas_call(
        functools.partial(_mixout_body, tiles_p=tiles_p),
        grid=(n // tm,),
        in_specs=[tok(D_MODEL, at_p), tok(D_MODEL, at_s),
                  pl.BlockSpec((1, 1, 6 * D_MODEL), lambda i: (mod_row(i), 0, 0)),
                  full((1, D_MODEL)), full((1, D_MODEL)), full((D_MODEL, 2 * D_MODEL)),
                  full((1, 2 * D_MODEL)), tok(D_LRU, at_p), tok(D_LRU, at_s),
                  s5(at_p), s5(at_s), s5(at_p), s5(at_s), full((1, D_S5)),
                  full((D_S5, D_S5)), full((1, D_S5)), full((D_S5, D_MODEL)), full((D_LRU, D_MODEL)),
                  full((D_MODEL, D_MODEL)), full((N_EXPERTS, D_MODEL)), full((N_EXPERTS, 1))],
        out_specs=[out(D_MODEL), out(D_MODEL), pl.BlockSpec((N_EXPERTS, tm), lambda i: (0, i))],
        out_shape=[jax.ShapeDtypeStruct((n, D_MODEL), F32), jax.ShapeDtypeStruct((n, D_MODEL), F32),
                   jax.ShapeDtypeStruct((N_EXPERTS, n), F32)],
        compiler_params=_cp(("arbitrary",)), name="mixout",
    )(xp, xs, mod3, g1, g2, wbg, bbg, ya_p, ya_s, y4_p, y4_s, u4_p, u4_s,
      dsk, wglu, bglu, wso, wlo, wo, wrt, brt)


def _select_body(aff_ref, gate_ref, pos_ref, offs_ref, *, n_tok, cap):
    aff = aff_ref[...]
    capf = float(cap)

    def bis(_, lh):
        lo, hi = lh
        mid = lo + ((hi - lo + 1) >> 1)
        cnt = jnp.sum(jnp.where(aff >= pltpu.bitcast(mid, F32), 1.0, 0.0), axis=1, keepdims=True)
        ok = cnt >= capf
        return jnp.where(ok, mid, lo), jnp.where(ok, hi, mid - 1)

    lo0 = jnp.zeros((N_EXPERTS, 1), I32)
    hi0 = jnp.full((N_EXPERTS, 1), 0x7F800000, I32)
    thr_bits, _ = lax.fori_loop(0, 31, bis, (lo0, hi0))
    thr = pltpu.bitcast(thr_bits, F32)
    need = capf - jnp.sum(jnp.where(aff > thr, 1.0, 0.0), axis=1, keepdims=True)

    r = lax.broadcasted_iota(I32, (TOK_TILE, TOK_TILE), 0)
    c = lax.broadcasted_iota(I32, (TOK_TILE, TOK_TILE), 1)
    tri = jnp.where(r < c, 1.0, 0.0).astype(BF16)
    lane = lax.broadcasted_iota(I32, (N_EXPERTS, LANES), 1)
    n_tiles = n_tok // TOK_TILE
    run_eq = jnp.zeros((N_EXPERTS, 1), F32)
    run_sel = jnp.zeros((N_EXPERTS, 1), F32)
    offs = jnp.zeros((N_EXPERTS, LANES), F32)
    for t in range(n_tiles):
        sl = slice(t * TOK_TILE, (t + 1) * TOK_TILE)
        aff_t = aff_ref[:, sl]
        is_eq = aff_t == thr
        eq_t = jnp.where(is_eq, 1.0, 0.0)
        rank_eq = jnp.dot(eq_t.astype(BF16), tri, preferred_element_type=F32) + run_eq
        cand = (aff_t > thr) | (is_eq & (rank_eq < need))
        pos_t = jnp.dot(jnp.where(cand, 1.0, 0.0).astype(BF16), tri, preferred_element_type=F32) + run_sel
        sel_t = cand & (pos_t < capf)
        sel_f = jnp.where(sel_t, 1.0, 0.0)
        offs = jnp.where(lane == t, run_sel, offs)
        gate_ref[:, sl] = jnp.where(sel_t, aff_t, 0.0)
        pos_ref[:, sl] = jnp.where(sel_t, pos_t, -1.0).astype(I32)
        run_eq = run_eq + jnp.sum(eq_t, axis=1, keepdims=True)
        run_sel = run_sel + jnp.sum(sel_f, axis=1, keepdims=True)
    offs = jnp.where(lane >= n_tiles, run_sel, offs)
    offs_ref[...] = offs.astype(I32)


def _select_call(aff_t, cap, n_tok, path):
    full = lambda shape: pl.BlockSpec(shape, lambda i: (0,) * len(shape))
    return pl.pallas_call(
        functools.partial(_select_body, n_tok=n_tok, cap=cap),
        grid=(1,),
        in_specs=[pl.BlockSpec((N_EXPERTS, n_tok), lambda i: (0, path))],
        out_specs=[full((N_EXPERTS, n_tok)), full((N_EXPERTS, n_tok)), full((N_EXPERTS, LANES))],
        out_shape=[jax.ShapeDtypeStruct((N_EXPERTS, n_tok), F32),
                   jax.ShapeDtypeStruct((N_EXPERTS, n_tok), I32),
                   jax.ShapeDtypeStruct((N_EXPERTS, LANES), I32)],
        compiler_params=_cp(("arbitrary",)), name="select",
    )(aff_t)


def _invert_body(offs_ref, pos_ref, gate_ref, idx_ref, gcol_ref, acc_i, acc_g, *, n_tiles, cap):
    base = pl.program_id(0) * LANES
    acc_i[...] = jnp.full(acc_i.shape, -1.0, F32)
    acc_g[...] = jnp.zeros(acc_g.shape, F32)
    jrow = lax.broadcasted_iota(I32, (SLOT_CHUNK, LANES), 0)
    lane = lax.broadcasted_iota(I32, (1, LANES), 1)

    def per_tile(t, _):
        first = (offs_ref[base + t] >> 3) << 3
        prow = pos_ref[0, pl.ds(t, 1), :]
        grow = gate_ref[0, pl.ds(t, 1), :]
        tok = (lane + t * TOK_TILE).astype(F32)

        def window(w, _):
            start = pl.multiple_of(first + w * SLOT_CHUNK, SUBLANES)
            rows = pl.ds(start, SLOT_CHUNK)
            hit = prow == (jrow + start)
            acc_i[rows, :] = jnp.maximum(acc_i[rows, :], jnp.where(hit, tok, -1.0))
            acc_g[rows, :] = jnp.maximum(acc_g[rows, :], jnp.where(hit, grow, 0.0))
            return 0

        n_win = (offs_ref[base + t + 1] - first + SLOT_CHUNK - 1) >> (SLOT_CHUNK.bit_length() - 1)
        lax.fori_loop(0, n_win, window, 0)
        return 0

    lax.fori_loop(0, n_tiles, per_tile, 0)
    idx_ref[0] = jnp.max(acc_i[0:cap, :], axis=1, keepdims=True).astype(I32)
    gcol_ref[0] = jnp.max(acc_g[0:cap, :], axis=1, keepdims=True)


def _invert_call(offs, pos, gate, cap):
    n_tok = pos.shape[1]
    n_tiles = n_tok // TOK_TILE
    pos3 = pos.reshape(N_EXPERTS, n_tiles, TOK_TILE)
    gate3 = gate.reshape(N_EXPERTS, n_tiles, TOK_TILE)
    per_e = lambda shape: pl.BlockSpec(shape, lambda e, o: (e, 0, 0))
    return pl.pallas_call(
        functools.partial(_invert_body, n_tiles=n_tiles, cap=cap),
        grid_spec=pltpu.PrefetchScalarGridSpec(
            num_scalar_prefetch=1, grid=(N_EXPERTS,),
            in_specs=[per_e((1, n_tiles, TOK_TILE)), per_e((1, n_tiles, TOK_TILE))],
            out_specs=[per_e((1, cap, 1)), per_e((1, cap, 1))],
            scratch_shapes=[pltpu.VMEM((cap + SLOT_CHUNK, LANES), F32)] * 2),
        out_shape=[jax.ShapeDtypeStruct((N_EXPERTS, cap, 1), I32),
                   jax.ShapeDtypeStruct((N_EXPERTS, cap, 1), F32)],
        compiler_params=_cp(("parallel",)), name="invert",
    )(offs.reshape(-1), pos3, gate3)


def _ffn_body(idx0_ref, idxn_ref, gcol_ref, wg_ref, wu_ref, wd_ref, h_hbm, out_ref,
              xe, wgb, wub, wdb, sem, *, cap):
    e = pl.program_id(0)
    p = pl.program_id(1)
    nxt = 1 - p
    step = e * 2 + p

    def issue_rows(idx_ref, s, j0, n):
        for j in range(j0, j0 + n):
            pltpu.make_async_copy(h_hbm.at[pl.ds(idx_ref[0, 0, 0, j], 1), :],
                                  xe.at[s, pl.ds(j, 1), :], sem.at[s]).start()

    @pl.when(step == 0)
    def _():
        def issue(j, _):
            pltpu.make_async_copy(h_hbm.at[pl.ds(idx0_ref[0, 0, 0, j], 1), :],
                                  xe.at[0, pl.ds(j, 1), :], sem.at[0]).start()
            return 0
        lax.fori_loop(0, cap, issue, 0, unroll=8)

    @pl.when(p == 0)
    def _():
        wgb[...] = wg_ref[0].astype(BF16)
        wub[...] = wu_ref[0].astype(BF16)
        wdb[...] = wd_ref[0].astype(BF16)

    pltpu.make_async_copy(h_hbm.at[pl.ds(0, cap), :], xe.at[p], sem.at[p]).wait()

    ch = 256
    n_ch = cap // ch
    per = -(-cap // (n_ch - 1))
    for c in range(n_ch):
        sl = slice(c * ch, (c + 1) * ch)
        xb = xe[p, sl, :].astype(BF16)
        issue_rows(idxn_ref, nxt, min(c * per, cap), min((c + 1) * per, cap) - min(c * per, cap))
        g = jnp.dot(xb, wgb[...], preferred_element_type=F32)
        u = jnp.dot(xb, wub[...], preferred_element_type=F32)
        hid = (g * _sigmoid(g)) * u
        ye = jnp.dot(hid.astype(BF16), wdb[...], preferred_element_type=F32)
        out_ref[0, 0, sl, :] = ye * gcol_ref[0, 0, sl, :]

    @pl.when(step == 2 * N_EXPERTS - 1)
    def _():
        pltpu.make_async_copy(h_hbm.at[pl.ds(0, cap), :], xe.at[nxt], sem.at[nxt]).wait()


def _ffn_call(idx, gcol, w_gate, w_up, w_down, h2, cap):
    d_e = w_gate.shape[2]
    wspec = lambda a, b: pl.BlockSpec((1, a, b), lambda e, p: (e, 0, 0))
    return pl.pallas_call(
        functools.partial(_ffn_body, cap=cap),
        grid=(N_EXPERTS, 2),
        in_specs=[pl.BlockSpec((1, 1, 1, cap), lambda e, p: (0, 0, 0, 0), memory_space=pltpu.SMEM),
                  pl.BlockSpec((1, 1, 1, cap),
                               lambda e, p: (1 - p, jnp.minimum(e + p, N_EXPERTS - 1), 0, 0),
                               memory_space=pltpu.SMEM),
                  pl.BlockSpec((1, 1, cap, 1), lambda e, p: (p, e, 0, 0)),
                  wspec(D_MODEL, d_e), wspec(D_MODEL, d_e), wspec(d_e, D_MODEL),
                  pl.BlockSpec(memory_space=pl.ANY)],
        out_specs=pl.BlockSpec((1, 1, cap, D_MODEL), lambda e, p: (p, e, 0, 0)),
        out_shape=jax.ShapeDtypeStruct((2, N_EXPERTS, cap, D_MODEL), F32),
        scratch_shapes=[pltpu.VMEM((2, cap, D_MODEL), F32), pltpu.VMEM((D_MODEL, d_e), BF16),
                        pltpu.VMEM((D_MODEL, d_e), BF16), pltpu.VMEM((d_e, D_MODEL), BF16),
                        pltpu.SemaphoreType.DMA((2,))],
        compiler_params=_cp(("arbitrary", "arbitrary")), name="ffn",
    )(idx, idx, gcol, w_gate, w_up, w_down, h2)


def _combine_body(offs_ref, x1_ref, mod_ref, gf_ref, pos_ref, yb_hbm, y_ref, wins, acc_ref, sem,
                  *, cap, path):
    t = pl.program_id(0)
    n_t = pl.num_programs(0)
    slot = t % 2

    def kmax_of(tt):
        k = 0
        for e in range(N_EXPERTS):
            k = jnp.maximum(k, offs_ref[e * LANES + tt + 1] - offs_ref[e * LANES + tt])
        return k

    def geometry(tt, e, w, win):
        fetch = win + SUBLANES
        lo = offs_ref[e * LANES + tt] + w * win
        start = pl.multiple_of(jnp.minimum((lo >> 3) << 3, cap - fetch), SUBLANES)
        return lo, start, fetch

    def issue(tt, w, s, win):
        for e in range(N_EXPERTS):
            _, start, fetch = geometry(tt, e, w, win)
            pltpu.make_async_copy(yb_hbm.at[path, e, pl.ds(start, fetch), :],
                                  wins.at[s, pl.ds(e * fetch, fetch), :], sem.at[s]).start()

    def drain(s, win):
        rows = N_EXPERTS * (win + SUBLANES)
        pltpu.make_async_copy(yb_hbm.at[0, 0, pl.ds(0, rows), :], wins.at[s, pl.ds(0, rows), :],
                              sem.at[s]).wait()

    def expand(w, s, win):
        sub = lax.broadcasted_iota(I32, (win + SUBLANES, TOK_TILE), 0)
        rows = []
        for e in range(N_EXPERTS):
            lo, start, _ = geometry(t, e, w, win)
            pos = pos_ref[e:e + 1, :]
            in_round = jnp.logical_and(pos >= lo, pos < lo + win)
            rows.append(jnp.where(jnp.logical_and(pos - start == sub, in_round), 1.0, 0.0))
        onehot = jnp.concatenate(rows, axis=0).T.astype(BF16)
        data = wins[s, 0:N_EXPERTS * (win + SUBLANES), :]
        hi = data.astype(BF16)
        lo_part = (data - hi.astype(F32)).astype(BF16)
        return (jnp.dot(onehot, hi, preferred_element_type=F32)
                + jnp.dot(onehot, lo_part, preferred_element_type=F32))

    def issue_first(tt, s):
        wide = kmax_of(tt) > COMBINE_WIN

        @pl.when(wide)
        def _():
            issue(tt, 0, s, COMBINE_WIN_WIDE)

        @pl.when(jnp.logical_not(wide))
        def _():
            issue(tt, 0, s, COMBINE_WIN)

    @pl.when(t == 0)
    def _():
        issue_first(0, 0)

    @pl.when(t + 1 < n_t)
    def _():
        issue_first(t + 1, 1 - slot)

    kmax = kmax_of(t)

    @pl.when(kmax > COMBINE_WIN)
    def _():
        drain(slot, COMBINE_WIN_WIDE)
        acc_ref[...] = expand(0, slot, COMBINE_WIN_WIDE)

    @pl.when(kmax <= COMBINE_WIN)
    def _():
        drain(slot, COMBINE_WIN)
        acc_ref[...] = expand(0, slot, COMBINE_WIN)

    def more(w, _):
        issue(t, w, slot, COMBINE_WIN_WIDE)
        drain(slot, COMBINE_WIN_WIDE)
        acc_ref[...] += expand(w, slot, COMBINE_WIN_WIDE)
        return 0
    lax.fori_loop(1, (kmax + COMBINE_WIN_WIDE - 1) >> (COMBINE_WIN_WIDE.bit_length() - 1), more, 0)
    acc = acc_ref[...]

    gt2 = mod_ref[0][:, 5 * D_MODEL:6 * D_MODEL]
    x2 = x1_ref[...] + gt2 * acc
    ms = jnp.mean(x2 * x2, axis=-1, keepdims=True)
    y_ref[...] = x2 * lax.rsqrt(ms + EPS) * gf_ref[...]


def _combine_call(offs, pos, x1, mod3, gf, ybuf, mod_row, cap, path, n):
    first = path * n // TOK_TILE
    return pl.pallas_call(
        functools.partial(_combine_body, cap=cap, path=path),
        grid_spec=pltpu.PrefetchScalarGridSpec(
            num_scalar_prefetch=1, grid=(n // TOK_TILE,),
            in_specs=[pl.BlockSpec((TOK_TILE, D_MODEL), lambda i, o: (first + i, 0)),
                      pl.BlockSpec((1, 1, 6 * D_MODEL), lambda i, o: (mod_row(i), 0, 0)),
                      pl.BlockSpec((1, D_MODEL), lambda i, o: (0, 0)),
                      pl.BlockSpec((N_EXPERTS, TOK_TILE), lambda i, o: (0, i)),
                      pl.BlockSpec(memory_space=pl.ANY)],
            out_specs=pl.BlockSpec((TOK_TILE, D_MODEL), lambda i, o: (i, 0)),
            scratch_shapes=[pltpu.VMEM((2, N_EXPERTS * (COMBINE_WIN_WIDE + SUBLANES), D_MODEL), F32),
                            pltpu.VMEM((TOK_TILE, D_MODEL), F32),
                            pltpu.SemaphoreType.DMA((2,))]),
        out_shape=jax.ShapeDtypeStruct((n, D_MODEL), F32),
        compiler_params=_cp(("arbitrary",)), name="combine",
    )(offs.reshape(-1), x1, mod3, gf, pos, ybuf)


def _lru_gate_weights(wa, wx):
    eye = jnp.eye(4, dtype=F32)

    def bd(w):
        w5 = w.reshape(4, 4, LRU_HEAD_DIM, LRU_HEAD_DIM)
        return jnp.einsum('khij,hg->khigj', w5, eye).reshape(4, 256, 256)

    return jnp.concatenate([bd(wa[0]), bd(wa[1]), bd(wx[0]), bd(wx[1])], axis=2).astype(BF16)


def kernel(x_prompt, x_sample, state_lru, state_s5_re, state_s5_im, c, c_ctx, w_mod, b_mod, g_norm1, g_norm2, w_in, conv_w, conv_b, lru_wa, lru_ba, lru_wx, lru_bx, lru_lambda, s5_lambda_re, s5_lambda_im, s5_log_step, s5_b_re, s5_b_im, s5_c_re, s5_c_im, s5_d, s5_w_glu, s5_b_glu, w_lru_out, w_s5_out, w_branch_gate, b_branch_gate, w_o, w_router, b_router, w_e_gate, w_e_up, w_e_down, g_final):
    bp, tp, _ = x_prompt.shape
    bs, ts, _ = x_sample.shape
    n_p, n_s = bp * tp, bs * ts
    l = 0

    c_all = jnp.zeros((16, D_MODEL), F32).at[0].set(c_ctx).at[1:1 + bs].set(c)
    mod3 = _mod_call(c_all, w_mod[l], b_mod[l][None, :]).reshape(16, 1, 6 * D_MODEL)

    w_in_b = w_in[l].astype(BF16)
    w_main, w_s5in = w_in_b[:, :2 * D_LRU], w_in_b[:, 2 * D_LRU:]
    g1 = g_norm1[l][None, :]
    g2 = g_norm2[l][None, :]
    wg = _lru_gate_weights(lru_wa[l], lru_wx[l])
    wts, wc, apow = _s5_weights(s5_lambda_re[l], s5_lambda_im[l], s5_log_step[l],
                                s5_b_re[l], s5_b_im[l], s5_c_re[l], s5_c_im[l])
    wbg = w_branch_gate[l].astype(BF16)
    bbg = b_branch_gate[l][None, :]
    wglu = s5_w_glu[l].astype(BF16)
    wso = w_s5_out[l].astype(BF16)
    wlo = w_lru_out[l].astype(BF16)
    wo = w_o[l].astype(BF16)
    wrt = w_router[l].T
    brt = b_router[l][:, None]

    xp2 = x_prompt.reshape(n_p, D_MODEL)
    xs2 = x_sample.reshape(n_s, D_MODEL)
    ctx_row = lambda i: 0
    tm_in = 512
    lat_row_in = lambda i: 1 + i // (ts // max(tm_in, ts))

    def s5_state(re, im):
        def part(a, d):
            return a[:, d].reshape(-1, S5_BLK, 8 * S5_N).transpose(1, 0, 2)
        return jnp.concatenate([part(re, 0), part(im, 0), part(re, 1), part(im, 1)], axis=2)

    xb_p, gg_p, us4_p = _inproj_call(xp2, mod3, g1, w_main, w_s5in, ctx_row, max(tm_in, tp), tp)
    ya_p, lru_fin = _lru_call(xb_p, gg_p, conv_w[l], conv_b[l][None, :], wg, lru_ba[l], lru_bx[l],
                              lru_lambda[l], jnp.zeros((bp, 2, D_LRU), F32), bp, tp)
    y4_p, s5_fin = _s5_call(us4_p, wts, wc, apow, jnp.zeros((S5_BLK, bp, S5_SW), F32), bp, tp,
                            with_h0=False, col_major=False)
    xb_s, gg_s, us4_s = _inproj_call(xs2, mod3, g1, w_main, w_s5in, lat_row_in, max(tm_in, ts), ts)
    ya_s, _ = _lru_call(xb_s, gg_s, conv_w[l], conv_b[l][None, :], wg, lru_ba[l], lru_bx[l],
                        lru_lambda[l], state_lru[:, l], bs, ts)
    y4_s, _ = _s5_call(us4_s, wts, wc, apow, s5_state(state_s5_re[:, l], state_s5_im[:, l]), bs, ts,
                       with_h0=True, col_major=True)

    tm_mo = 512
    tiles_p = n_p // tm_mo
    x1, h2, aff = _mixout_call(
        xp2, xs2, mod3, g1, g2, wbg, bbg, ya_p, ya_s, y4_p, y4_s, us4_p, us4_s,
        s5_d[l][None, :], wglu, s5_b_glu[l][None, :], wso, wlo, wo, wrt, brt,
        mod_row=lambda i: jnp.where(i < tiles_p, 0, 1 + (i - tiles_p) // (ts // tm_mo)), tm=tm_mo)

    assert n_p == n_s
    cap = (CAPACITY_FACTOR * n_p) // N_EXPERTS
    routes = []
    for path in range(2):
        gate, pos, offs = _select_call(aff, cap, n_p, path)
        idx, gcol = _invert_call(offs, pos, gate, cap)
        routes.append((offs, pos, idx.reshape(N_EXPERTS, 1, cap) + path * n_p, gcol))
    ybuf = _ffn_call(jnp.stack([routes[0][2], routes[1][2]]), jnp.stack([routes[0][3], routes[1][3]]),
                     w_e_gate[l], w_e_up[l], w_e_down[l], h2, cap)
    gf = g_final[None, :]
    y_p = _combine_call(routes[0][0], routes[0][1], x1, mod3, gf, ybuf, ctx_row, cap, 0, n_p)
    y_s = _combine_call(routes[1][0], routes[1][1], x1, mod3, gf, ybuf,
                        lambda i: 1 + i // (ts // TOK_TILE), cap, 1, n_s)

    new_lru = lru_fin[:, None]
    sf = s5_fin.transpose(1, 0, 2).reshape(bp, S5_BLK, 2, 2, 8, S5_N)
    sf = sf.transpose(0, 2, 3, 1, 4, 5).reshape(bp, 2, 2, S5_G, S5_N)
    new_s5r = sf[:, :, 0][:, None]
    new_s5i = sf[:, :, 1][:, None]
    return (y_p.reshape(bp, tp, D_MODEL), y_s.reshape(bs, ts, D_MODEL), new_lru, new_s5r, new_s5i)
```

```python
import functools
import math

import jax
import jax.numpy as jnp
from jax import lax
from jax.experimental import pallas as pl
from jax.experimental.pallas import tpu as pltpu

F32 = jnp.float32
BF16 = jnp.bfloat16
I32 = jnp.int32

D_MODEL = 1024
D_LRU = 1024
LRU_HEADS = 16
LRU_HEAD_DIM = 64
LRU_C = 8.0
CONV_W = 4
D_S5 = 512
S5_P = 16
S5_G = 32
S5_N = 64
GRID_W = 64
N_EXPERTS = 16
CAPACITY_FACTOR = 2
EPS = 1e-6

LANES = 128
SUBLANES = 8
S5_L = 8
S5_BLK = 4
S5_SW = 4 * 512
TOK_TILE = 128
SLOT_CHUNK = 64
COMBINE_WIN = 32
COMBINE_WIN_WIDE = 64
VMEM_LIMIT = 56 * 1024 * 1024


def _cp(sem, vmem=VMEM_LIMIT):
    return pltpu.CompilerParams(dimension_semantics=sem, vmem_limit_bytes=vmem)


def _bdot(a, b):
    return jnp.dot(a.astype(BF16), b.astype(BF16), preferred_element_type=F32)


def _split2(a):
    hi = a.astype(BF16)
    lo = (a - hi.astype(F32)).astype(BF16)
    return hi, lo


def _dot3(a, b, dims=(((1,), (0,)), ((), ()))):
    ah, al = _split2(a)
    bh, bl = _split2(b)
    d = functools.partial(lax.dot_general, dimension_numbers=dims, preferred_element_type=F32)
    return d(ah, bh) + (d(al, bh) + d(ah, bl))


def _sigmoid(x):
    return 0.5 * jnp.tanh(0.5 * x) + 0.5


def _gelu(x):
    c = math.sqrt(2.0 / math.pi)
    return 0.5 * x * (1.0 + jnp.tanh(c * (x + 0.044715 * (x * x * x))))


def _norm_mod(x, g, scale, shift):
    ms = jnp.mean(x * x, axis=-1, keepdims=True)
    return (x * lax.rsqrt(ms + EPS) * g) * (1.0 + scale) + shift


def _mod_body(c_ref, w_ref, b_ref, o_ref):
    c = c_ref[...]
    s = c * _sigmoid(c)
    o_ref[...] = _dot3(s, w_ref[...]) + b_ref[...]


def _mod_call(c_all, w_mod, b_mod):
    n = w_mod.shape[1]
    tn = 1536
    return pl.pallas_call(
        _mod_body,
        grid=(n // tn,),
        in_specs=[pl.BlockSpec((16, D_MODEL), lambda j: (0, 0)),
                  pl.BlockSpec((D_MODEL, tn), lambda j: (0, j)),
                  pl.BlockSpec((1, tn), lambda j: (0, j))],
        out_specs=pl.BlockSpec((16, tn), lambda j: (0, j)),
        out_shape=jax.ShapeDtypeStruct((16, n), F32),
        compiler_params=_cp(("arbitrary",)),
        name="mod",
    )(c_all, w_mod, b_mod)


def _inproj_body(x_ref, mod_ref, g_ref, w_ref, ws_ref, xb_ref, gg_ref, us_ref, *, seq_len):
    m = mod_ref[0]
    h = _norm_mod(x_ref[...], g_ref[...], m[:, D_MODEL:2 * D_MODEL], m[:, 0:D_MODEL])
    hb = h.astype(BF16)
    z = jnp.dot(hb, w_ref[...], preferred_element_type=F32)
    seg = seq_len // SUBLANES
    for s in range(x_ref.shape[0] // seq_len):
        for k in range(SUBLANES):
            t0 = s * seq_len + k * seg
            for c in range(D_LRU // LANES):
                xb_ref[c, pl.ds(s * seq_len + k, seg, stride=SUBLANES), :] = (
                    z[t0:t0 + seg, c * LANES:(c + 1) * LANES])
    gg_ref[...] = _gelu(z[:, D_LRU:]).astype(BF16)
    us = jnp.dot(hb, ws_ref[...], preferred_element_type=F32)
    for k in range(S5_BLK):
        us_ref[k] = us[:, k * LANES:(k + 1) * LANES]


def _inproj_call(x2d, mod3, g1, w_main, w_s5, mod_row, tm, seq_len):
    n = x2d.shape[0]
    return pl.pallas_call(
        functools.partial(_inproj_body, seq_len=seq_len), grid=(n // tm,),
        in_specs=[pl.BlockSpec((tm, D_MODEL), lambda i: (i, 0)),
                  pl.BlockSpec((1, 1, 6 * D_MODEL), lambda i: (mod_row(i), 0, 0)),
                  pl.BlockSpec((1, D_MODEL), lambda i: (0, 0)),
                  pl.BlockSpec((D_MODEL, 2 * D_LRU), lambda i: (0, 0)),
                  pl.BlockSpec((D_MODEL, D_S5), lambda i: (0, 0))],
        out_specs=[pl.BlockSpec((D_LRU // LANES, tm, LANES), lambda i: (0, i, 0)),
                   pl.BlockSpec((tm, D_LRU), lambda i: (i, 0)),
                   pl.BlockSpec((S5_BLK, tm, LANES), lambda i: (0, i, 0))],
        out_shape=[jax.ShapeDtypeStruct((D_LRU // LANES, n, LANES), F32),
                   jax.ShapeDtypeStruct((n, D_LRU), BF16),
                   jax.ShapeDtypeStruct((S5_BLK, n, LANES), F32)],
        compiler_params=_cp(("parallel",)), name="inproj",
    )(x2d, mod3, g1, w_main, w_s5)


def _lru_body(xb_ref, gg_ref, cw_ref, cb_ref, wg_ref, ba_ref, bx_ref, lam_ref, h0_ref,
              ya_ref, fin_ref, xpad, a_f, b_f, a_b, b_b, hs, *, T):
    seg = T // SUBLANES
    ch = 256
    nch = T // ch
    n_slab = D_LRU // LANES
    pad = 2 * SUBLANES
    sub = lax.broadcasted_iota(I32, (SUBLANES, D_LRU), 0)

    def rows_of(r0, n):
        return jnp.concatenate([xb_ref[c, r0:r0 + n, :] for c in range(n_slab)], axis=1)

    def from_prev_segment(tile):
        return jnp.where(sub >= 1, pltpu.roll(tile, 1, 0), 0.0)

    def from_next_segment(tile):
        return jnp.where(sub < SUBLANES - 1, pltpu.roll(tile, SUBLANES - 1, 0), 0.0)

    xpad[0:SUBLANES, :] = from_prev_segment(rows_of((seg - 2) * SUBLANES, SUBLANES))
    xpad[SUBLANES:pad, :] = from_prev_segment(rows_of((seg - 1) * SUBLANES, SUBLANES))
    xpad[pad + T:pad + T + SUBLANES, :] = from_next_segment(rows_of(0, SUBLANES))
    for c in range(nch):
        xpad[pad + c * ch:pad + (c + 1) * ch, :] = rows_of(c * ch, ch)

    nl = -lam_ref[...]
    softplus = jnp.maximum(nl, 0.0) + jnp.log1p(jnp.exp(-jnp.abs(nl)))
    scr = ((a_f, b_f), (a_b, b_b))

    for c in range(nch):
        t0 = c * ch
        xc = cb_ref[...] + sum(
            xpad[t0 + k * SUBLANES:t0 + k * SUBLANES + ch, :] * cw_ref[k:k + 1, :]
            for k in range(CONV_W))
        xcb = xc.astype(BF16)
        pre = [jnp.dot(xcb[:, kb * 256:(kb + 1) * 256], wg_ref[kb], preferred_element_type=F32)
               for kb in range(4)]
        for d in range(2):
            ra = jnp.concatenate([p[:, d * 256:(d + 1) * 256] for p in pre], axis=1)
            gx = jnp.concatenate([p[:, 512 + d * 256:512 + (d + 1) * 256] for p in pre], axis=1)
            r = _sigmoid(ra + ba_ref[d:d + 1, :])
            gi = _sigmoid(gx + bx_ref[d:d + 1, :])
            log_a = (-LRU_C) * r * softplus[d:d + 1, :]
            a = jnp.exp(log_a)
            scr[d][0][t0:t0 + ch, :] = a
            scr[d][1][t0:t0 + ch, :] = jnp.sqrt(1.0 - a * a) * gi * xc

    def local_scan(a_ref, u_ref, reverse):
        def step(i, carry):
            h, p = carry
            r0 = pl.multiple_of((seg - 1 - i if reverse else i) * SUBLANES, SUBLANES)
            a = a_ref[pl.ds(r0, SUBLANES), :]
            h = a * h + u_ref[pl.ds(r0, SUBLANES), :]
            p = a * p
            u_ref[pl.ds(r0, SUBLANES), :] = h
            a_ref[pl.ds(r0, SUBLANES), :] = p
            return h, p
        init = (jnp.zeros((SUBLANES, D_LRU), F32), jnp.ones((SUBLANES, D_LRU), F32))
        return lax.fori_loop(0, seg, step, init, unroll=2)

    def carry_in(h_end, p_end, h0_row, reverse):
        edge = sub == (SUBLANES - 1 if reverse else 0)
        shift = SUBLANES - 1 if reverse else 1
        c = jnp.broadcast_to(h0_row, (SUBLANES, D_LRU))
        for _ in range(SUBLANES - 1):
            c = jnp.where(edge, h0_row, pltpu.roll(h_end + p_end * c, shift, 0))
        return c

    hf_end, pf_end = local_scan(a_f, b_f, False)
    hb_end, pb_end = local_scan(a_b, b_b, True)
    c_f = carry_in(hf_end, pf_end, h0_ref[0, 0:1, :], False)
    c_b = carry_in(hb_end, pb_end, h0_ref[0, 1:2, :], True)
    fin_ref[0, 0:1, :] = (hf_end + pf_end * c_f)[SUBLANES - 1:SUBLANES, :]
    fin_ref[0, 1:2, :] = (hb_end + pb_end * c_b)[0:1, :]

    for c in range(nch):
        sl = slice(c * ch, (c + 1) * ch)
        tile3 = (ch // SUBLANES, SUBLANES, D_LRU)
        hsum = ((b_f[sl, :].reshape(tile3) + a_f[sl, :].reshape(tile3) * c_f[None])
                + (b_b[sl, :].reshape(tile3) + a_b[sl, :].reshape(tile3) * c_b[None])).reshape(ch, D_LRU)
        for j in range(n_slab):
            hs[j, sl, :] = hsum[:, j * LANES:(j + 1) * LANES]

    for k in range(SUBLANES):
        rows = slice(k * seg, (k + 1) * seg)
        h_seg = jnp.concatenate([hs[j, pl.ds(k, seg, stride=SUBLANES), :] for j in range(n_slab)], axis=1)
        ya_ref[0, rows, :] = (gg_ref[0, rows, :].astype(F32) * h_seg).astype(BF16)


def _lru_call(xb8, gg, conv_w, conv_b, wg, ba, bx, lam, h0, n_seq, T):
    gg3 = gg.reshape(n_seq, T, D_LRU)
    n_slab = D_LRU // LANES
    full = lambda shape: pl.BlockSpec(shape, lambda i: (0,) * len(shape))
    seq = lambda shape: pl.BlockSpec(shape, lambda i: (i,) + (0,) * (len(shape) - 1))
    ya, fin = pl.pallas_call(
        functools.partial(_lru_body, T=T),
        grid=(n_seq,),
        in_specs=[pl.BlockSpec((n_slab, T, LANES), lambda i: (0, i, 0)), seq((1, T, D_LRU)),
                  full((CONV_W, D_LRU)), full((1, D_LRU)),
                  full((4, 256, 1024)), full((2, D_LRU)), full((2, D_LRU)), full((2, D_LRU)),
                  seq((1, 2, D_LRU))],
        out_specs=[seq((1, T, D_LRU)), seq((1, 2, D_LRU))],
        out_shape=[jax.ShapeDtypeStruct((n_seq, T, D_LRU), BF16),
                   jax.ShapeDtypeStruct((n_seq, 2, D_LRU), F32)],
        scratch_shapes=[pltpu.VMEM((T + 3 * SUBLANES, D_LRU), F32)] + [pltpu.VMEM((T, D_LRU), F32)] * 4
                       + [pltpu.VMEM((n_slab, T, LANES), F32)],
        compiler_params=_cp(("parallel",)), name="lru",
    )(xb8, gg3, conv_w, conv_b, wg, ba, bx, lam, h0)
    return ya.reshape(n_seq * T, D_LRU), fin


def _s5_body(u_ref, wts_ref, wc_ref, ap_ref, h0_ref, y_ref, fin_ref, s_scr, hp_scr, slab,
             *, rt, n_c, with_h0, col_major):
    nseq = rt // n_c
    q = 512
    seq_tok = n_c * S5_L
    halves = seq_tok // (GRID_W * S5_L)

    def tok_rows(s, h, i):
        return pl.ds(s * seq_tok + (h * S5_L + i) * GRID_W, GRID_W)

    def chunk_rows(s, h):
        return pl.ds(s * n_c + h, GRID_W, stride=halves)

    if col_major:
        for i in range(S5_L):
            for s in range(nseq):
                for h in range(halves):
                    slab[i, chunk_rows(s, h), :] = u_ref[0, tok_rows(s, h, i), :]
        ub = jnp.concatenate([slab[i] for i in range(S5_L)], axis=1).astype(BF16)
    else:
        ub = jnp.concatenate([u_ref[0, pl.ds(i, rt, stride=S5_L), :] for i in range(S5_L)],
                             axis=1).astype(BF16)
    r1 = jnp.dot(ub, wts_ref[0], preferred_element_type=F32)
    y_in = r1[:, :S5_L * LANES]
    s_scr[...] = r1[:, S5_L * LANES:]
    ap = ap_ref[0]
    if with_h0:
        for s in range(nseq):
            h0 = h0_ref[0, 0, s:s + 1, :]
            r_f = s * n_c
            r_b = s * n_c + n_c - 1
            ar, ai = ap[0:1, 0:q], ap[0:1, q:2 * q]
            s_scr[r_f:r_f + 1, 0:q] += ar * h0[:, 0:q] - ai * h0[:, q:2 * q]
            s_scr[r_f:r_f + 1, q:2 * q] += ar * h0[:, q:2 * q] + ai * h0[:, 0:q]
            br, bi = ap[0:1, 2 * q:3 * q], ap[0:1, 3 * q:4 * q]
            s_scr[r_b:r_b + 1, 2 * q:3 * q] += br * h0[:, 2 * q:3 * q] - bi * h0[:, 3 * q:4 * q]
            s_scr[r_b:r_b + 1, 3 * q:4 * q] += br * h0[:, 3 * q:4 * q] + bi * h0[:, 2 * q:3 * q]

    cidx = lax.broadcasted_iota(I32, (rt, 1), 0) % n_c
    hfr, hfi = s_scr[:, 0:q], s_scr[:, q:2 * q]
    hbr, hbi = s_scr[:, 2 * q:3 * q], s_scr[:, 3 * q:4 * q]
    lvl = 0
    dist = 1
    while dist < n_c:
        ar, ai = ap[lvl:lvl + 1, 0:q], ap[lvl:lvl + 1, q:2 * q]
        br, bi = ap[lvl:lvl + 1, 2 * q:3 * q], ap[lvl:lvl + 1, 3 * q:4 * q]
        kf = cidx >= dist
        kb = cidx < n_c - dist
        sfr = jnp.where(kf, pltpu.roll(hfr, dist, 0), 0.0)
        sfi = jnp.where(kf, pltpu.roll(hfi, dist, 0), 0.0)
        sbr = jnp.where(kb, pltpu.roll(hbr, rt - dist, 0), 0.0)
        sbi = jnp.where(kb, pltpu.roll(hbi, rt - dist, 0), 0.0)
        hfr, hfi = hfr + (ar * sfr - ai * sfi), hfi + (ar * sfi + ai * sfr)
        hbr, hbi = hbr + (br * sbr - bi * sbi), hbi + (br * sbi + bi * sbr)
        lvl += 1
        dist *= 2

    s_scr[:, 0:q] = hfr
    s_scr[:, q:2 * q] = hfi
    s_scr[:, 2 * q:3 * q] = hbr
    s_scr[:, 3 * q:4 * q] = hbi
    for s in range(nseq):
        fin_ref[0, 0, s:s + 1, 0:2 * q] = s_scr[s * n_c + n_c - 1:s * n_c + n_c, 0:2 * q]
        fin_ref[0, 0, s:s + 1, 2 * q:4 * q] = s_scr[s * n_c:s * n_c + 1, 2 * q:4 * q]

    kf = cidx >= 1
    kb = cidx < n_c - 1
    hp_scr[:, 0:q] = jnp.where(kf, pltpu.roll(hfr, 1, 0), 0.0)
    hp_scr[:, q:2 * q] = jnp.where(kf, pltpu.roll(hfi, 1, 0), 0.0)
    hp_scr[:, 2 * q:3 * q] = jnp.where(kb, pltpu.roll(hbr, rt - 1, 0), 0.0)
    hp_scr[:, 3 * q:4 * q] = jnp.where(kb, pltpu.roll(hbi, rt - 1, 0), 0.0)
    if with_h0:
        for s in range(nseq):
            hp_scr[s * n_c:s * n_c + 1, 0:2 * q] = h0_ref[0, 0, s:s + 1, 0:2 * q]
            hp_scr[s * n_c + n_c - 1:s * n_c + n_c, 2 * q:4 * q] = h0_ref[0, 0, s:s + 1, 2 * q:4 * q]
    y = y_in + jnp.dot(hp_scr[...].astype(BF16), wc_ref[0], preferred_element_type=F32)
    for i in range(S5_L):
        y_i = y[:, i * LANES:(i + 1) * LANES]
        if col_major:
            slab[i] = y_i
            for s in range(nseq):
                for h in range(halves):
                    y_ref[0, tok_rows(s, h, i), :] = slab[i, chunk_rows(s, h), :]
        else:
            y_ref[0, pl.ds(i, rt, stride=S5_L), :] = y_i


def _s5_call(us4, wts, wc, apow, h0, n_seq, T, with_h0, col_major):
    n_c = T // S5_L
    rows = n_seq * n_c
    rt = 256
    nseq_t = rt // n_c
    h0 = h0.reshape(S5_BLK, n_seq // nseq_t, nseq_t, S5_SW)
    y4, fin = pl.pallas_call(
        functools.partial(_s5_body, rt=rt, n_c=n_c, with_h0=with_h0, col_major=col_major),
        grid=(S5_BLK, rows // rt),
        in_specs=[pl.BlockSpec((1, rt * S5_L, LANES), lambda b, r: (b, r, 0)),
                  pl.BlockSpec((1, S5_L * LANES, S5_L * LANES + S5_SW), lambda b, r: (b, 0, 0)),
                  pl.BlockSpec((1, S5_SW, S5_L * LANES), lambda b, r: (b, 0, 0)),
                  pl.BlockSpec((1, 8, S5_SW), lambda b, r: (b, 0, 0)),
                  pl.BlockSpec((1, 1, nseq_t, S5_SW), lambda b, r: (b, r, 0, 0))],
        out_specs=[pl.BlockSpec((1, rt * S5_L, LANES), lambda b, r: (b, r, 0)),
                   pl.BlockSpec((1, 1, nseq_t, S5_SW), lambda b, r: (b, r, 0, 0))],
        out_shape=[jax.ShapeDtypeStruct((S5_BLK, n_seq * T, LANES), F32),
                   jax.ShapeDtypeStruct((S5_BLK, n_seq // nseq_t, nseq_t, S5_SW), F32)],
        scratch_shapes=[pltpu.VMEM((rt, S5_SW), F32), pltpu.VMEM((rt, S5_SW), F32),
                        pltpu.VMEM((S5_L, rt, LANES), F32)],
        compiler_params=_cp(("parallel", "parallel")), name="s5",
    )(us4, wts, wc, apow, h0)
    return y4, fin.reshape(S5_BLK, n_seq, S5_SW)


def _s5_disc(lr, li, st):
    mag = jnp.exp(lr * st)
    ang = li * st
    ar, ai = mag * jnp.cos(ang), mag * jnp.sin(ang)
    den = lr * lr + li * li
    fr = ((ar - 1.0) * lr + ai * li) / den
    fi = (ai * lr - (ar - 1.0) * li) / den
    return ar, ai, fr, fi


def _s5_powers(ar, ai, n):
    out = [(jnp.ones_like(ar), jnp.zeros_like(ar))]
    for _ in range(n):
        pr, pi = out[-1]
        out.append((pr * ar - pi * ai, pr * ai + pi * ar))
    return out


def _s5_prep_body(rowp_ref, colp_ref, flatp_ref, bt_ref, cm_ref, wts_ref, wc_ref, ap_ref):
    L, N, P = S5_L, S5_N, S5_P
    w = L * LANES
    div = lambda x, d: x >> (d.bit_length() - 1)
    mod = lambda x, d: x & (d - 1)
    one = lambda m: jnp.where(m, 1.0, 0.0).astype(BF16)
    r1 = lax.broadcasted_iota(I32, (LANES, LANES), 0)
    c1 = lax.broadcasted_iota(I32, (LANES, LANES), 1)
    first_half = c1 < N
    sgn = jnp.where(first_half, -1.0, 1.0)

    e_maps, taps = [], []
    same_group = div(r1, P) == div(c1, P)
    fold = one(mod(r1, P) == mod(c1, P))
    for d in range(2):
        ar, ai, fr, fi = _s5_disc(rowp_ref[0, d, 0], rowp_ref[1, d, 0], rowp_ref[2, d, 0])
        bx1 = bt_ref[d, 0]
        bx2 = pltpu.roll(bx1, N, 1)
        x1 = fr * bx1 + (fi * sgn) * bx2
        x2 = pltpu.roll(x1, N, 1)
        e_d, t_d = [], []
        for pr, pi in _s5_powers(ar, ai, L):
            e_k = pr * x1 + (pi * sgn) * x2
            e_d.append(e_k)
            full = _dot3(e_k, cm_ref[d, 0])
            kept = jnp.where(same_group, full, 0.0)
            hi = kept.astype(BF16)
            mid = (kept - hi.astype(F32)).astype(BF16)
            lo = (kept - hi.astype(F32) - mid.astype(F32)).astype(BF16)
            t_d.append(sum(jnp.dot(part, fold, preferred_element_type=F32) for part in (hi, mid, lo)))
        e_maps.append(e_d)
        taps.append(t_d)

    r = lax.broadcasted_iota(I32, (LANES, w), 0)
    c = lax.broadcasted_iota(I32, (LANES, w), 1)
    t_tap = one((div(r, P) == div(c, LANES)) & (mod(r, P) == mod(c, P)))
    m_tap = div(r, P) == div(mod(c, LANES), P)
    t_st = one((div(r, N) == div(c, 8 * N)) & (mod(r, N) == mod(c, N)))
    m_st = div(r, P) == div(mod(c, 8 * N), N)
    col_blk = div(c1, P)
    for j in range(L):
        rows = slice(j * LANES, (j + 1) * LANES)
        toe = jnp.zeros((LANES, LANES), F32)
        for k in range(L):
            toe = (toe + jnp.where(col_blk == j + k, taps[0][k], 0.0)
                   + jnp.where(col_blk == j - k, taps[1][k], 0.0))
        tap = jnp.dot(toe.astype(BF16), t_tap, preferred_element_type=F32)
        wts_ref[0, rows, 0:w] = jnp.where(m_tap, tap, 0.0).astype(BF16)
        for k, e_j in enumerate((e_maps[0][L - 1 - j], e_maps[1][j])):
            st = jnp.dot(e_j.astype(BF16), t_st, preferred_element_type=F32)
            wts_ref[0, rows, w + k * w:w + (k + 1) * w] = jnp.where(m_st, st, 0.0).astype(BF16)

    rr = lax.broadcasted_iota(I32, (w, LANES), 0)
    cc = lax.broadcasted_iota(I32, (w, LANES), 1)
    t_row = one((div(rr, 8 * N) == div(cc, N)) & (mod(rr, N) == mod(cc, N)))
    r2 = lax.broadcasted_iota(I32, (w, w), 0)
    c2 = lax.broadcasted_iota(I32, (w, w), 1)
    m_row = div(mod(r2, 8 * N), N) == div(mod(c2, LANES), P)
    for d in range(2):
        ar, ai, _, _ = _s5_disc(colp_ref[0, d, 0], colp_ref[1, d, 0], colp_ref[2, d, 0])
        a1 = cm_ref[d, 0]
        swapped = pltpu.roll(a1, N, 0)
        a2 = jnp.where(r1 < N, swapped, -swapped)
        pw = _s5_powers(ar, ai, L)
        ks = [i + 1 for i in range(L)] if d == 0 else [L - i for i in range(L)]
        g = jnp.concatenate([a1 * pw[k][0] + a2 * pw[k][1] for k in ks], axis=1)
        corr = jnp.dot(t_row, g.astype(BF16), preferred_element_type=F32)
        wc_ref[0, d * w:(d + 1) * w, :] = jnp.where(m_row, corr, 0.0).astype(BF16)

    q = []
    for d in range(2):
        ar, ai, _, _ = _s5_disc(flatp_ref[0, d, 0], flatp_ref[1, d, 0], flatp_ref[2, d, 0])
        q.append(_s5_powers(ar, ai, L)[L])
    for m in range(8):
        ap_ref[0, m:m + 1, :] = jnp.concatenate([q[0][0], q[0][1], q[1][0], q[1][1]], axis=1)
        q = [(qr * qr - qi * qi, 2.0 * qr * qi) for qr, qi in q]


def _s5_weights(lam_re, lam_im, log_step, b_re, b_im, c_re, c_im):
    L, G, N, P = S5_L, S5_G, S5_N, S5_P
    w = L * LANES
    step = jnp.broadcast_to(jnp.exp(log_step)[..., None], lam_re.shape)
    prm = jnp.stack([lam_re, lam_im, step])
    rowp = jnp.broadcast_to(prm[:, :, :, None, None, :], (3, 2, G, P, 2, N)).reshape(3, 2, S5_BLK, LANES, 2 * N)
    colp = jnp.broadcast_to(prm.transpose(0, 1, 3, 2)[:, :, None, :, :, None], (3, 2, 2, N, G, P))
    colp = colp.reshape(3, 2, 2 * N, S5_BLK, LANES).transpose(0, 1, 3, 2, 4)
    flatp = prm.reshape(3, 2, S5_BLK, 1, 8 * N)
    bt = jnp.stack([b_re, b_im], axis=3).transpose(0, 1, 4, 3, 2)
    bt = bt.reshape(2, S5_BLK, LANES, 2 * N)
    cm = jnp.stack([c_re, -c_im], axis=1).transpose(0, 1, 4, 2, 3)
    cm = cm.reshape(2, 2 * N, S5_BLK, LANES).transpose(0, 2, 1, 3)
    blk = lambda lead, a, b_: pl.BlockSpec(lead + (1, a, b_), lambda i: (0,) * len(lead) + (i, 0, 0))
    out = lambda a, b_: pl.BlockSpec((1, a, b_), lambda i: (i, 0, 0))
    return pl.pallas_call(
        _s5_prep_body, grid=(S5_BLK,),
        in_specs=[blk((3, 2), LANES, LANES), blk((3, 2), LANES, LANES), blk((3, 2), 1, 8 * N),
                  blk((2,), LANES, LANES), blk((2,), LANES, LANES)],
        out_specs=[out(w, w + S5_SW), out(S5_SW, w), out(8, S5_SW)],
        out_shape=[jax.ShapeDtypeStruct((S5_BLK, w, w + S5_SW), BF16),
                   jax.ShapeDtypeStruct((S5_BLK, S5_SW, w), BF16),
                   jax.ShapeDtypeStruct((S5_BLK, 8, S5_SW), F32)],
        compiler_params=_cp(("parallel",)), name="s5_prep",
    )(rowp, colp, flatp, bt, cm)


def _mixout_body(xp_ref, xs_ref, mod_ref, g1_ref, g2_ref, wbg_ref, bbg_ref, yap_ref, yas_ref,
                 y4p_ref, y4s_ref, u4p_ref, u4s_ref, *rest, tiles_p):
    i = pl.program_id(0)

    @pl.when(i < tiles_p)
    def _():
        _mixout_tile(xp_ref, mod_ref, g1_ref, g2_ref, wbg_ref, bbg_ref, yap_ref, y4p_ref, u4p_ref, *rest)

    @pl.when(i >= tiles_p)
    def _():
        _mixout_tile(xs_ref, mod_ref, g1_ref, g2_ref, wbg_ref, bbg_ref, yas_ref, y4s_ref, u4s_ref, *rest)


def _mixout_tile(x_ref, mod_ref, g1_ref, g2_ref, wbg_ref, bbg_ref, ya_ref, y4_ref, u4_ref,
                 dsk_ref, wglu_ref, bglu_ref, wso_ref, wlo_ref, wo_ref, wrt_ref, brt_ref,
                 x1_ref, h2_ref, aff_ref):
    m = mod_ref[0]
    sh1, sc1, gt1 = m[:, 0:D_MODEL], m[:, D_MODEL:2 * D_MODEL], m[:, 2 * D_MODEL:3 * D_MODEL]
    sh2, sc2 = m[:, 3 * D_MODEL:4 * D_MODEL], m[:, 4 * D_MODEL:5 * D_MODEL]
    x = x_ref[...]
    h = _norm_mod(x, g1_ref[...], sc1, sh1)
    gates = _sigmoid(jnp.dot(h.astype(BF16), wbg_ref[...], preferred_element_type=F32) + bbg_ref[...])

    def assemble(ref):
        return jnp.concatenate([ref[k] for k in range(S5_BLK)], axis=1)

    ys = assemble(y4_ref) + dsk_ref[...] * assemble(u4_ref)
    v = _gelu(ys)
    ob = v * _sigmoid(jnp.dot(v.astype(BF16), wglu_ref[...], preferred_element_type=F32) + bglu_ref[...])
    merged = (gates[:, :D_MODEL] * jnp.dot(ya_ref[...], wlo_ref[...], preferred_element_type=F32)
              + gates[:, D_MODEL:] * jnp.dot(ob.astype(BF16), wso_ref[...], preferred_element_type=F32))
    mix = jnp.dot(merged.astype(BF16), wo_ref[...], preferred_element_type=F32)
    x1 = x + gt1 * mix
    x1_ref[...] = x1
    h2 = _norm_mod(x1, g2_ref[...], sc2, sh2)
    h2_ref[...] = h2
    logits = _dot3(wrt_ref[...], h2, dims=(((1,), (1,)), ((), ()))) + brt_ref[...]
    mx = jnp.max(logits, axis=0, keepdims=True)
    ex = jnp.exp(logits - mx)
    aff_ref[...] = ex / jnp.sum(ex, axis=0, keepdims=True)


def _mixout_call(xp, xs, mod3, g1, g2, wbg, bbg, ya_p, ya_s, y4_p, y4_s, u4_p, u4_s,
                 dsk, wglu, bglu, wso, wlo, wo, wrt, brt, mod_row, tm):
    n_p, n_s = xp.shape[0], xs.shape[0]
    tiles_p = n_p // tm
    n = n_p + n_s
    full = lambda shape: pl.BlockSpec(shape, lambda i: (0,) * len(shape))
    at_p = lambda i: jnp.minimum(i, tiles_p - 1)
    at_s = lambda i: jnp.maximum(i - tiles_p, 0)
    tok = lambda w, at: pl.BlockSpec((tm, w), lambda i: (at(i), 0))
    s5 = lambda at: pl.BlockSpec((S5_BLK, tm, LANES), lambda i: (0, at(i), 0))
    out = lambda w: pl.BlockSpec((tm, w), lambda i: (i, 0))
    return pl.pallas_call(
        functools.partial(_mixout_body, tiles_p=tiles_p),
        grid=(n // tm,),
        in_specs=[tok(D_MODEL, at_p), tok(D_MODEL, at_s),
                  pl.BlockSpec((1, 1, 6 * D_MODEL), lambda i: (mod_row(i), 0, 0)),
                  full((1, D_MODEL)), full((1, D_MODEL)), full((D_MODEL, 2 * D_MODEL)),
                  full((1, 2 * D_MODEL)), tok(D_LRU, at_p), tok(D_LRU, at_s),
                  s5(at_p), s5(at_s), s5(at_p), s5(at_s), full((1, D_S5)),
                  full((D_S5, D_S5)), full((1, D_S5)), full((D_S5, D_MODEL)), full((D_LRU, D_MODEL)),
                  full((D_MODEL, D_MODEL)), full((N_EXPERTS, D_MODEL)), full((N_EXPERTS, 1))],
        out_specs=[out(D_MODEL), out(D_MODEL), pl.BlockSpec((N_EXPERTS, tm), lambda i: (0, i))],
        out_shape=[jax.ShapeDtypeStruct((n, D_MODEL), F32), jax.ShapeDtypeStruct((n, D_MODEL), F32),
                   jax.ShapeDtypeStruct((N_EXPERTS, n), F32)],
        compiler_params=_cp(("arbitrary",)), name="mixout",
    )(xp, xs, mod3, g1, g2, wbg, bbg, ya_p, ya_s, y4_p, y4_s, u4_p, u4_s,
      dsk, wglu, bglu, wso, wlo, wo, wrt, brt)


def _select_body(aff_ref, gate_ref, pos_ref, offs_ref, *, n_tok, cap):
    aff = aff_ref[...]
    capf = float(cap)

    def bis(_, lh):
        lo, hi = lh
        mid = lo + ((hi - lo + 1) >> 1)
        cnt = jnp.sum(jnp.where(aff >= pltpu.bitcast(mid, F32), 1.0, 0.0), axis=1, keepdims=True)
        ok = cnt >= capf
        return jnp.where(ok, mid, lo), jnp.where(ok, hi, mid - 1)

    lo0 = jnp.zeros((N_EXPERTS, 1), I32)
    hi0 = jnp.full((N_EXPERTS, 1), 0x7F800000, I32)
    thr_bits, _ = lax.fori_loop(0, 31, bis, (lo0, hi0))
    thr = pltpu.bitcast(thr_bits, F32)
    need = capf - jnp.sum(jnp.where(aff > thr, 1.0, 0.0), axis=1, keepdims=True)

    r = lax.broadcasted_iota(I32, (TOK_TILE, TOK_TILE), 0)
    c = lax.broadcasted_iota(I32, (TOK_TILE, TOK_TILE), 1)
    tri = jnp.where(r < c, 1.0, 0.0).astype(BF16)
    lane = lax.broadcasted_iota(I32, (N_EXPERTS, LANES), 1)
    n_tiles = n_tok // TOK_TILE
    run_eq = jnp.zeros((N_EXPERTS, 1), F32)
    run_sel = jnp.zeros((N_EXPERTS, 1), F32)
    offs = jnp.zeros((N_EXPERTS, LANES), F32)
    for t in range(n_tiles):
        sl = slice(t * TOK_TILE, (t + 1) * TOK_TILE)
        aff_t = aff_ref[:, sl]
        is_eq = aff_t == thr
        eq_t = jnp.where(is_eq, 1.0, 0.0)
        rank_eq = jnp.dot(eq_t.astype(BF16), tri, preferred_element_type=F32) + run_eq
        cand = (aff_t > thr) | (is_eq & (rank_eq < need))
        pos_t = jnp.dot(jnp.where(cand, 1.0, 0.0).astype(BF16), tri, preferred_element_type=F32) + run_sel
        sel_t = cand & (pos_t < capf)
        sel_f = jnp.where(sel_t, 1.0, 0.0)
        offs = jnp.where(lane == t, run_sel, offs)
        gate_ref[:, sl] = jnp.where(sel_t, aff_t, 0.0)
        pos_ref[:, sl] = jnp.where(sel_t, pos_t, -1.0).astype(I32)
        run_eq = run_eq + jnp.sum(eq_t, axis=1, keepdims=True)
        run_sel = run_sel + jnp.sum(sel_f, axis=1, keepdims=True)
    offs = jnp.where(lane >= n_tiles, run_sel, offs)
    offs_ref[...] = offs.astype(I32)


def _select_call(aff_t, cap, n_tok):
    n_path = aff_t.shape[1] // n_tok
    per_p = lambda a, b: pl.BlockSpec((None, a, b), lambda p: (p, 0, 0))
    return pl.pallas_call(
        functools.partial(_select_body, n_tok=n_tok, cap=cap),
        grid=(n_path,),
        in_specs=[pl.BlockSpec((N_EXPERTS, n_tok), lambda p: (0, p))],
        out_specs=[per_p(N_EXPERTS, n_tok), per_p(N_EXPERTS, n_tok), per_p(N_EXPERTS, LANES)],
        out_shape=[jax.ShapeDtypeStruct((n_path, N_EXPERTS, n_tok), F32),
                   jax.ShapeDtypeStruct((n_path, N_EXPERTS, n_tok), I32),
                   jax.ShapeDtypeStruct((n_path, N_EXPERTS, LANES), I32)],
        compiler_params=_cp(("parallel",)), name="select",
    )(aff_t)


def _invert_body(offs_ref, pos_ref, gate_ref, idx_ref, gcol_ref, acc_i, acc_g, *, n_tiles, cap):
    path = pl.program_id(0)
    base = (path * N_EXPERTS + pl.program_id(1)) * LANES
    acc_i[...] = jnp.zeros(acc_i.shape, F32)
    acc_g[...] = jnp.zeros(acc_g.shape, F32)
    jrow = lax.broadcasted_iota(I32, (SLOT_CHUNK, LANES), 0)
    lane = lax.broadcasted_iota(I32, (1, LANES), 1)

    def per_tile(t, _):
        first = (offs_ref[base + t] >> 3) << 3
        prow = pos_ref[0, 0, pl.ds(t, 1), :]
        grow = gate_ref[0, 0, pl.ds(t, 1), :]
        tok = (lane + (t * TOK_TILE + 1)).astype(F32)

        def window(w, _):
            start = pl.multiple_of(first + w * SLOT_CHUNK, SUBLANES)
            rows = pl.ds(start, SLOT_CHUNK)
            hit = prow == (jrow + start)
            acc_i[rows, :] = jnp.maximum(acc_i[rows, :], jnp.where(hit, tok, 0.0))
            acc_g[rows, :] = jnp.maximum(acc_g[rows, :], jnp.where(hit, grow, 0.0))
            return 0

        n_win = (offs_ref[base + t + 1] - first + SLOT_CHUNK - 1) >> (SLOT_CHUNK.bit_length() - 1)
        lax.fori_loop(0, n_win, window, 0)
        return 0

    lax.fori_loop(0, n_tiles, per_tile, 0)
    hit_i = acc_i[0:cap, :]
    hi = hit_i.astype(BF16)
    mid = (hit_i - hi.astype(F32)).astype(BF16)
    lo = (hit_i - hi.astype(F32) - mid.astype(F32)).astype(BF16)
    ones = jnp.ones((SUBLANES, LANES), BF16)
    nt = (((1,), (1,)), ((), ()))
    tok1 = sum(lax.dot_general(ones, part, nt, preferred_element_type=F32) for part in (hi, mid, lo))
    idx_ref[0, 0] = tok1[0:1, :].astype(I32) - 1 + path * (n_tiles * TOK_TILE)
    gcol_ref[0, 0] = jnp.max(acc_g[0:cap, :], axis=1, keepdims=True)


def _invert_call(offs, pos, gate, cap):
    n_path, _, n_tok = pos.shape
    n_tiles = n_tok // TOK_TILE
    pos4 = pos.reshape(n_path, N_EXPERTS, n_tiles, TOK_TILE)
    gate4 = gate.reshape(n_path, N_EXPERTS, n_tiles, TOK_TILE)
    per_e = lambda a, b: pl.BlockSpec((1, 1, a, b), lambda p, e, o: (p, e, 0, 0))
    return pl.pallas_call(
        functools.partial(_invert_body, n_tiles=n_tiles, cap=cap),
        grid_spec=pltpu.PrefetchScalarGridSpec(
            num_scalar_prefetch=1, grid=(n_path, N_EXPERTS),
            in_specs=[per_e(n_tiles, TOK_TILE), per_e(n_tiles, TOK_TILE)],
            out_specs=[per_e(1, cap), per_e(cap, 1)],
            scratch_shapes=[pltpu.VMEM((cap + SLOT_CHUNK, LANES), F32)] * 2),
        out_shape=[jax.ShapeDtypeStruct((n_path, N_EXPERTS, 1, cap), I32),
                   jax.ShapeDtypeStruct((n_path, N_EXPERTS, cap, 1), F32)],
        compiler_params=_cp(("parallel", "parallel")), name="invert",
    )(offs.reshape(-1), pos4, gate4)


def _ffn_body(idx0_ref, idxn_ref, gcol_ref, wg_ref, wu_ref, wd_ref, h_hbm, out_ref,
              xe, wgb, wub, wdb, sem, *, cap):
    e = pl.program_id(0)
    p = pl.program_id(1)
    nxt = 1 - p
    step = e * 2 + p

    def issue_rows(idx_ref, s, j0, n):
        for j in range(j0, j0 + n):
            pltpu.make_async_copy(h_hbm.at[pl.ds(idx_ref[0, 0, 0, j], 1), :],
                                  xe.at[s, pl.ds(j, 1), :], sem.at[s]).start()

    @pl.when(step == 0)
    def _():
        def issue(j, _):
            pltpu.make_async_copy(h_hbm.at[pl.ds(idx0_ref[0, 0, 0, j], 1), :],
                                  xe.at[0, pl.ds(j, 1), :], sem.at[0]).start()
            return 0
        lax.fori_loop(0, cap, issue, 0, unroll=8)

    @pl.when(p == 0)
    def _():
        wgb[...] = wg_ref[0].astype(BF16)
        wub[...] = wu_ref[0].astype(BF16)
        wdb[...] = wd_ref[0].astype(BF16)

    pltpu.make_async_copy(h_hbm.at[pl.ds(0, cap), :], xe.at[p], sem.at[p]).wait()

    ch = 256
    n_ch = cap // ch
    per = -(-cap // (n_ch - 1))
    for c in range(n_ch):
        sl = slice(c * ch, (c + 1) * ch)
        xb = xe[p, sl, :].astype(BF16)
        issue_rows(idxn_ref, nxt, min(c * per, cap), min((c + 1) * per, cap) - min(c * per, cap))
        g = jnp.dot(xb, wgb[...], preferred_element_type=F32)
        u = jnp.dot(xb, wub[...], preferred_element_type=F32)
        hid = (g * _sigmoid(g)) * u
        ye = jnp.dot(hid.astype(BF16), wdb[...], preferred_element_type=F32)
        out_ref[0, 0, sl, :] = ye * gcol_ref[0, 0, sl, :]

    @pl.when(step == 2 * N_EXPERTS - 1)
    def _():
        pltpu.make_async_copy(h_hbm.at[pl.ds(0, cap), :], xe.at[nxt], sem.at[nxt]).wait()


def _ffn_call(idx, gcol, w_gate, w_up, w_down, h2, cap):
    d_e = w_gate.shape[2]
    wspec = lambda a, b: pl.BlockSpec((1, a, b), lambda e, p: (e, 0, 0))
    return pl.pallas_call(
        functools.partial(_ffn_body, cap=cap),
        grid=(N_EXPERTS, 2),
        in_specs=[pl.BlockSpec((1, 1, 1, cap), lambda e, p: (0, 0, 0, 0), memory_space=pltpu.SMEM),
                  pl.BlockSpec((1, 1, 1, cap),
                               lambda e, p: (1 - p, jnp.minimum(e + p, N_EXPERTS - 1), 0, 0),
                               memory_space=pltpu.SMEM),
                  pl.BlockSpec((1, 1, cap, 1), lambda e, p: (p, e, 0, 0)),
                  wspec(D_MODEL, d_e), wspec(D_MODEL, d_e), wspec(d_e, D_MODEL),
                  pl.BlockSpec(memory_space=pl.ANY)],
        out_specs=pl.BlockSpec((1, 1, cap, D_MODEL), lambda e, p: (p, e, 0, 0)),
        out_shape=jax.ShapeDtypeStruct((2, N_EXPERTS, cap, D_MODEL), F32),
        scratch_shapes=[pltpu.VMEM((2, cap, D_MODEL), F32), pltpu.VMEM((D_MODEL, d_e), BF16),
                        pltpu.VMEM((D_MODEL, d_e), BF16), pltpu.VMEM((d_e, D_MODEL), BF16),
                        pltpu.SemaphoreType.DMA((2,))],
        compiler_params=_cp(("arbitrary", "arbitrary")), name="ffn",
    )(idx, idx, gcol, w_gate, w_up, w_down, h2)


def _combine_body(offs_ref, x1_ref, mod_ref, gf_ref, pos_ref, yb_hbm, y_ref, wins, acc_ref, sem,
                  *, cap, path):
    t = pl.program_id(0)
    n_t = pl.num_programs(0)
    slot = t % 2
    row0 = path * N_EXPERTS * LANES

    def kmax_of(tt):
        k = 0
        for e in range(N_EXPERTS):
            k = jnp.maximum(k, offs_ref[row0 + e * LANES + tt + 1] - offs_ref[row0 + e * LANES + tt])
        return k

    def geometry(tt, e, w, win):
        fetch = win + SUBLANES
        lo = offs_ref[row0 + e * LANES + tt] + w * win
        start = pl.multiple_of(jnp.minimum((lo >> 3) << 3, cap - fetch), SUBLANES)
        return lo, start, fetch

    def issue(tt, w, s, win):
        for e in range(N_EXPERTS):
            _, start, fetch = geometry(tt, e, w, win)
            pltpu.make_async_copy(yb_hbm.at[path, e, pl.ds(start, fetch), :],
                                  wins.at[s, pl.ds(e * fetch, fetch), :], sem.at[s]).start()

    def drain(s, win):
        rows = N_EXPERTS * (win + SUBLANES)
        pltpu.make_async_copy(yb_hbm.at[0, 0, pl.ds(0, rows), :], wins.at[s, pl.ds(0, rows), :],
                              sem.at[s]).wait()

    def expand(w, s, win):
        sub = lax.broadcasted_iota(I32, (win + SUBLANES, TOK_TILE), 0)
        rows = []
        for e in range(N_EXPERTS):
            lo, start, _ = geometry(t, e, w, win)
            pos = pos_ref[e:e + 1, :]
            in_round = jnp.logical_and(pos >= lo, pos < lo + win)
            rows.append(jnp.where(jnp.logical_and(pos - start == sub, in_round), 1.0, 0.0))
        onehot = jnp.concatenate(rows, axis=0).T.astype(BF16)
        data = wins[s, 0:N_EXPERTS * (win + SUBLANES), :]
        hi = data.astype(BF16)
        lo_part = (data - hi.astype(F32)).astype(BF16)
        return (jnp.dot(onehot, hi, preferred_element_type=F32)
                + jnp.dot(onehot, lo_part, preferred_element_type=F32))

    def issue_first(tt, s):
        wide = kmax_of(tt) > COMBINE_WIN

        @pl.when(wide)
        def _():
            issue(tt, 0, s, COMBINE_WIN_WIDE)

        @pl.when(jnp.logical_not(wide))
        def _():
            issue(tt, 0, s, COMBINE_WIN)

    @pl.when(t == 0)
    def _():
        issue_first(0, 0)

    @pl.when(t + 1 < n_t)
    def _():
        issue_first(t + 1, 1 - slot)

    kmax = kmax_of(t)

    @pl.when(kmax > COMBINE_WIN)
    def _():
        drain(slot, COMBINE_WIN_WIDE)
        acc_ref[...] = expand(0, slot, COMBINE_WIN_WIDE)

    @pl.when(kmax <= COMBINE_WIN)
    def _():
        drain(slot, COMBINE_WIN)
        acc_ref[...] = expand(0, slot, COMBINE_WIN)

    def more(w, _):
        issue(t, w, slot, COMBINE_WIN_WIDE)
        drain(slot, COMBINE_WIN_WIDE)
        acc_ref[...] += expand(w, slot, COMBINE_WIN_WIDE)
        return 0
    lax.fori_loop(1, (kmax + COMBINE_WIN_WIDE - 1) >> (COMBINE_WIN_WIDE.bit_length() - 1), more, 0)
    acc = acc_ref[...]

    gt2 = mod_ref[0][:, 5 * D_MODEL:6 * D_MODEL]
    x2 = x1_ref[...] + gt2 * acc
    ms = jnp.mean(x2 * x2, axis=-1, keepdims=True)
    y_ref[...] = x2 * lax.rsqrt(ms + EPS) * gf_ref[...]


def _combine_call(offs, pos, x1, mod3, gf, ybuf, mod_row, cap, path, n):
    first = path * n // TOK_TILE
    return pl.pallas_call(
        functools.partial(_combine_body, cap=cap, path=path),
        grid_spec=pltpu.PrefetchScalarGridSpec(
            num_scalar_prefetch=1, grid=(n // TOK_TILE,),
            in_specs=[pl.BlockSpec((TOK_TILE, D_MODEL), lambda i, o: (first + i, 0)),
                      pl.BlockSpec((1, 1, 6 * D_MODEL), lambda i, o: (mod_row(i), 0, 0)),
                      pl.BlockSpec((1, D_MODEL), lambda i, o: (0, 0)),
                      pl.BlockSpec((None, N_EXPERTS, TOK_TILE), lambda i, o: (path, 0, i)),
                      pl.BlockSpec(memory_space=pl.ANY)],
            out_specs=pl.BlockSpec((TOK_TILE, D_MODEL), lambda i, o: (i, 0)),
            scratch_shapes=[pltpu.VMEM((2, N_EXPERTS * (COMBINE_WIN_WIDE + SUBLANES), D_MODEL), F32),
                            pltpu.VMEM((TOK_TILE, D_MODEL), F32),
                            pltpu.SemaphoreType.DMA((2,))]),
        out_shape=jax.ShapeDtypeStruct((n, D_MODEL), F32),
        compiler_params=_cp(("arbitrary",)), name="combine",
    )(offs.reshape(-1), x1, mod3, gf, pos, ybuf)


def _lru_gate_weights(wa, wx):
    eye = jnp.eye(4, dtype=F32)

    def bd(w):
        w5 = w.reshape(4, 4, LRU_HEAD_DIM, LRU_HEAD_DIM)
        return jnp.einsum('khij,hg->khigj', w5, eye).reshape(4, 256, 256)

    return jnp.concatenate([bd(wa[0]), bd(wa[1]), bd(wx[0]), bd(wx[1])], axis=2).astype(BF16)


def kernel(x_prompt, x_sample, state_lru, state_s5_re, state_s5_im, c, c_ctx, w_mod, b_mod, g_norm1, g_norm2, w_in, conv_w, conv_b, lru_wa, lru_ba, lru_wx, lru_bx, lru_lambda, s5_lambda_re, s5_lambda_im, s5_log_step, s5_b_re, s5_b_im, s5_c_re, s5_c_im, s5_d, s5_w_glu, s5_b_glu, w_lru_out, w_s5_out, w_branch_gate, b_branch_gate, w_o, w_router, b_router, w_e_gate, w_e_up, w_e_down, g_final):
    bp, tp, _ = x_prompt.shape
    bs, ts, _ = x_sample.shape
    n_p, n_s = bp * tp, bs * ts
    l = 0

    c_all = jnp.zeros((16, D_MODEL), F32).at[0].set(c_ctx).at[1:1 + bs].set(c)
    mod3 = _mod_call(c_all, w_mod[l], b_mod[l][None, :]).reshape(16, 1, 6 * D_MODEL)

    w_in_b = w_in[l].astype(BF16)
    w_main, w_s5in = w_in_b[:, :2 * D_LRU], w_in_b[:, 2 * D_LRU:]
    g1 = g_norm1[l][None, :]
    g2 = g_norm2[l][None, :]
    wg = _lru_gate_weights(lru_wa[l], lru_wx[l])
    wts, wc, apow = _s5_weights(s5_lambda_re[l], s5_lambda_im[l], s5_log_step[l],
                                s5_b_re[l], s5_b_im[l], s5_c_re[l], s5_c_im[l])
    wbg = w_branch_gate[l].astype(BF16)
    bbg = b_branch_gate[l][None, :]
    wglu = s5_w_glu[l].astype(BF16)
    wso = w_s5_out[l].astype(BF16)
    wlo = w_lru_out[l].astype(BF16)
    wo = w_o[l].astype(BF16)
    wrt = w_router[l].T
    brt = b_router[l][:, None]

    xp2 = x_prompt.reshape(n_p, D_MODEL)
    xs2 = x_sample.reshape(n_s, D_MODEL)
    ctx_row = lambda i: 0
    tm_in = 512
    lat_row_in = lambda i: 1 + i // (ts // max(tm_in, ts))

    def s5_state(re, im):
        def part(a, d):
            return a[:, d].reshape(-1, S5_BLK, 8 * S5_N).transpose(1, 0, 2)
        return jnp.concatenate([part(re, 0), part(im, 0), part(re, 1), part(im, 1)], axis=2)

    xb_p, gg_p, us4_p = _inproj_call(xp2, mod3, g1, w_main, w_s5in, ctx_row, max(tm_in, tp), tp)
    ya_p, lru_fin = _lru_call(xb_p, gg_p, conv_w[l], conv_b[l][None, :], wg, lru_ba[l], lru_bx[l],
                              lru_lambda[l], jnp.zeros((bp, 2, D_LRU), F32), bp, tp)
    y4_p, s5_fin = _s5_call(us4_p, wts, wc, apow, jnp.zeros((S5_BLK, bp, S5_SW), F32), bp, tp,
                            with_h0=False, col_major=False)
    xb_s, gg_s, us4_s = _inproj_call(xs2, mod3, g1, w_main, w_s5in, lat_row_in, max(tm_in, ts), ts)
    ya_s, _ = _lru_call(xb_s, gg_s, conv_w[l], conv_b[l][None, :], wg, lru_ba[l], lru_bx[l],
                        lru_lambda[l], state_lru[:, l], bs, ts)
    y4_s, _ = _s5_call(us4_s, wts, wc, apow, s5_state(state_s5_re[:, l], state_s5_im[:, l]), bs, ts,
                       with_h0=True, col_major=True)

    tm_mo = 512
    tiles_p = n_p // tm_mo
    x1, h2, aff = _mixout_call(
        xp2, xs2, mod3, g1, g2, wbg, bbg, ya_p, ya_s, y4_p, y4_s, us4_p, us4_s,
        s5_d[l][None, :], wglu, s5_b_glu[l][None, :], wso, wlo, wo, wrt, brt,
        mod_row=lambda i: jnp.where(i < tiles_p, 0, 1 + (i - tiles_p) // (ts // tm_mo)), tm=tm_mo)

    assert n_p == n_s
    cap = (CAPACITY_FACTOR * n_p) // N_EXPERTS
    gate, pos, offs = _select_call(aff, cap, n_p)
    idx, gcol = _invert_call(offs, pos, gate, cap)
    ybuf = _ffn_call(idx, gcol, w_e_gate[l], w_e_up[l], w_e_down[l], h2, cap)
    gf = g_final[None, :]
    y_p = _combine_call(offs, pos, x1, mod3, gf, ybuf, ctx_row, cap, 0, n_p)
    y_s = _combine_call(offs, pos, x1, mod3, gf, ybuf, lambda i: 1 + i // (ts // TOK_TILE), cap, 1, n_s)

    new_lru = lru_fin[:, None]
    sf = s5_fin.transpose(1, 0, 2).reshape(bp, S5_BLK, 2, 2, 8, S5_N)
    sf = sf.transpose(0, 2, 3, 1, 4, 5).reshape(bp, 2, 2, S5_G, S5_N)
    new_s5r = sf[:, :, 0][:, None]
    new_s5i = sf[:, :, 1][:, None]
    return (y_p.reshape(bp, tp, D_MODEL), y_s.reshape(bs, ts, D_MODEL), new_lru, new_s5r, new_s5i)
```

```python
import functools
import math

import jax
import jax.numpy as jnp
from jax import lax
from jax.experimental import pallas as pl
from jax.experimental.pallas import tpu as pltpu

F32 = jnp.float32
BF16 = jnp.bfloat16
I32 = jnp.int32

D_MODEL = 1024
D_LRU = 1024
LRU_HEADS = 16
LRU_HEAD_DIM = 64
LRU_C = 8.0
CONV_W = 4
D_S5 = 512
S5_P = 16
S5_G = 32
S5_N = 64
GRID_W = 64
N_EXPERTS = 16
CAPACITY_FACTOR = 2
EPS = 1e-6

LANES = 128
SUBLANES = 8
S5_L = 8
S5_BLK = 4
S5_SW = 4 * 512
S5_SEG = 32
TOK_TILE = 128
SLOT_CHUNK = 64
COMBINE_WIN = 32
COMBINE_WIN_WIDE = 64
VMEM_LIMIT = 56 * 1024 * 1024


def _cp(sem, vmem=VMEM_LIMIT):
    return pltpu.CompilerParams(dimension_semantics=sem, vmem_limit_bytes=vmem)


def _bdot(a, b):
    return jnp.dot(a.astype(BF16), b.astype(BF16), preferred_element_type=F32)


def _split2(a):
    hi = a.astype(BF16)
    lo = (a - hi.astype(F32)).astype(BF16)
    return hi, lo


def _dot3(a, b, dims=(((1,), (0,)), ((), ()))):
    ah, al = _split2(a)
    bh, bl = _split2(b)
    d = functools.partial(lax.dot_general, dimension_numbers=dims, preferred_element_type=F32)
    return d(ah, bh) + (d(al, bh) + d(ah, bl))


def _sigmoid(x):
    return 0.5 * jnp.tanh(0.5 * x) + 0.5


def _gelu(x):
    c = math.sqrt(2.0 / math.pi)
    return 0.5 * x * (1.0 + jnp.tanh(c * (x + 0.044715 * (x * x * x))))


def _norm_mod(x, g, scale, shift):
    ms = jnp.mean(x * x, axis=-1, keepdims=True)
    return (x * lax.rsqrt(ms + EPS) * g) * (1.0 + scale) + shift


def _mod_body(c_ref, w_ref, b_ref, o_ref):
    c = c_ref[...]
    s = c * _sigmoid(c)
    o_ref[...] = _dot3(s, w_ref[...]) + b_ref[...]


def _mod_call(c_all, w_mod, b_mod):
    n = w_mod.shape[1]
    tn = 1536
    return pl.pallas_call(
        _mod_body,
        grid=(n // tn,),
        in_specs=[pl.BlockSpec((16, D_MODEL), lambda j: (0, 0)),
                  pl.BlockSpec((D_MODEL, tn), lambda j: (0, j)),
                  pl.BlockSpec((1, tn), lambda j: (0, j))],
        out_specs=pl.BlockSpec((16, tn), lambda j: (0, j)),
        out_shape=jax.ShapeDtypeStruct((16, n), F32),
        compiler_params=_cp(("arbitrary",)),
        name="mod",
    )(c_all, w_mod, b_mod)


def _inproj_body(x_ref, mod_ref, g_ref, w_ref, ws_ref, xb_ref, gg_ref, us_ref, *, seq_len):
    m = mod_ref[0]
    h = _norm_mod(x_ref[...], g_ref[...], m[:, D_MODEL:2 * D_MODEL], m[:, 0:D_MODEL])
    hb = h.astype(BF16)
    z = jnp.dot(hb, w_ref[...], preferred_element_type=F32)
    seg = seq_len // SUBLANES
    for s in range(x_ref.shape[0] // seq_len):
        for k in range(SUBLANES):
            t0 = s * seq_len + k * seg
            for c in range(D_LRU // LANES):
                xb_ref[c, pl.ds(s * seq_len + k, seg, stride=SUBLANES), :] = (
                    z[t0:t0 + seg, c * LANES:(c + 1) * LANES])
    gg_ref[...] = _gelu(z[:, D_LRU:]).astype(BF16)
    us = jnp.dot(hb, ws_ref[...], preferred_element_type=F32)
    for k in range(S5_BLK):
        us_ref[k] = us[:, k * LANES:(k + 1) * LANES]


def _inproj_call(x2d, mod3, g1, w_main, w_s5, mod_row, tm, seq_len):
    n = x2d.shape[0]
    return pl.pallas_call(
        functools.partial(_inproj_body, seq_len=seq_len), grid=(n // tm,),
        in_specs=[pl.BlockSpec((tm, D_MODEL), lambda i: (i, 0)),
                  pl.BlockSpec((1, 1, 6 * D_MODEL), lambda i: (mod_row(i), 0, 0)),
                  pl.BlockSpec((1, D_MODEL), lambda i: (0, 0)),
                  pl.BlockSpec((D_MODEL, 2 * D_LRU), lambda i: (0, 0)),
                  pl.BlockSpec((D_MODEL, D_S5), lambda i: (0, 0))],
        out_specs=[pl.BlockSpec((D_LRU // LANES, tm, LANES), lambda i: (0, i, 0)),
                   pl.BlockSpec((tm, D_LRU), lambda i: (i, 0)),
                   pl.BlockSpec((S5_BLK, tm, LANES), lambda i: (0, i, 0))],
        out_shape=[jax.ShapeDtypeStruct((D_LRU // LANES, n, LANES), F32),
                   jax.ShapeDtypeStruct((n, D_LRU), BF16),
                   jax.ShapeDtypeStruct((S5_BLK, n, LANES), F32)],
        compiler_params=_cp(("parallel",)), name="inproj",
    )(x2d, mod3, g1, w_main, w_s5)


def _lru_body(xb_ref, gg_ref, cw_ref, cb_ref, wg_ref, ba_ref, bx_ref, lam_ref, h0_ref,
              ya_ref, fin_ref, xpad, a_f, b_f, a_b, b_b, hs, *, T):
    seg = T // SUBLANES
    ch = 256
    nch = T // ch
    n_slab = D_LRU // LANES
    pad = 2 * SUBLANES
    sub = lax.broadcasted_iota(I32, (SUBLANES, D_LRU), 0)

    def rows_of(r0, n):
        return jnp.concatenate([xb_ref[c, r0:r0 + n, :] for c in range(n_slab)], axis=1)

    def from_prev_segment(tile):
        return jnp.where(sub >= 1, pltpu.roll(tile, 1, 0), 0.0)

    def from_next_segment(tile):
        return jnp.where(sub < SUBLANES - 1, pltpu.roll(tile, SUBLANES - 1, 0), 0.0)

    xpad[0:SUBLANES, :] = from_prev_segment(rows_of((seg - 2) * SUBLANES, SUBLANES))
    xpad[SUBLANES:pad, :] = from_prev_segment(rows_of((seg - 1) * SUBLANES, SUBLANES))
    xpad[pad + T:pad + T + SUBLANES, :] = from_next_segment(rows_of(0, SUBLANES))
    for c in range(nch):
        xpad[pad + c * ch:pad + (c + 1) * ch, :] = rows_of(c * ch, ch)

    nl = -lam_ref[...]
    softplus = jnp.maximum(nl, 0.0) + jnp.log1p(jnp.exp(-jnp.abs(nl)))
    scr = ((a_f, b_f), (a_b, b_b))

    for c in range(nch):
        t0 = c * ch
        xc = cb_ref[...] + sum(
            xpad[t0 + k * SUBLANES:t0 + k * SUBLANES + ch, :] * cw_ref[k:k + 1, :]
            for k in range(CONV_W))
        xcb = xc.astype(BF16)
        pre = [jnp.dot(xcb[:, kb * 256:(kb + 1) * 256], wg_ref[kb], preferred_element_type=F32)
               for kb in range(4)]
        for d in range(2):
            ra = jnp.concatenate([p[:, d * 256:(d + 1) * 256] for p in pre], axis=1)
            gx = jnp.concatenate([p[:, 512 + d * 256:512 + (d + 1) * 256] for p in pre], axis=1)
            r = _sigmoid(ra + ba_ref[d:d + 1, :])
            gi = _sigmoid(gx + bx_ref[d:d + 1, :])
            log_a = (-LRU_C) * r * softplus[d:d + 1, :]
            a = jnp.exp(log_a)
            scr[d][0][t0:t0 + ch, :] = a
            scr[d][1][t0:t0 + ch, :] = jnp.sqrt(1.0 - a * a) * gi * xc

    def local_scan(a_ref, u_ref, reverse):
        def step(i, carry):
            h, p = carry
            r0 = pl.multiple_of((seg - 1 - i if reverse else i) * SUBLANES, SUBLANES)
            a = a_ref[pl.ds(r0, SUBLANES), :]
            h = a * h + u_ref[pl.ds(r0, SUBLANES), :]
            p = a * p
            u_ref[pl.ds(r0, SUBLANES), :] = h
            a_ref[pl.ds(r0, SUBLANES), :] = p
            return h, p
        init = (jnp.zeros((SUBLANES, D_LRU), F32), jnp.ones((SUBLANES, D_LRU), F32))
        return lax.fori_loop(0, seg, step, init, unroll=2)

    def carry_in(h_end, p_end, h0_row, reverse):
        edge = sub == (SUBLANES - 1 if reverse else 0)
        shift = SUBLANES - 1 if reverse else 1
        c = jnp.broadcast_to(h0_row, (SUBLANES, D_LRU))
        for _ in range(SUBLANES - 1):
            c = jnp.where(edge, h0_row, pltpu.roll(h_end + p_end * c, shift, 0))
        return c

    hf_end, pf_end = local_scan(a_f, b_f, False)
    hb_end, pb_end = local_scan(a_b, b_b, True)
    c_f = carry_in(hf_end, pf_end, h0_ref[0, 0:1, :], False)
    c_b = carry_in(hb_end, pb_end, h0_ref[0, 1:2, :], True)
    fin_ref[0, 0:1, :] = (hf_end + pf_end * c_f)[SUBLANES - 1:SUBLANES, :]
    fin_ref[0, 1:2, :] = (hb_end + pb_end * c_b)[0:1, :]

    for c in range(nch):
        sl = slice(c * ch, (c + 1) * ch)
        tile3 = (ch // SUBLANES, SUBLANES, D_LRU)
        hsum = ((b_f[sl, :].reshape(tile3) + a_f[sl, :].reshape(tile3) * c_f[None])
                + (b_b[sl, :].reshape(tile3) + a_b[sl, :].reshape(tile3) * c_b[None])).reshape(ch, D_LRU)
        for j in range(n_slab):
            hs[j, sl, :] = hsum[:, j * LANES:(j + 1) * LANES]

    for k in range(SUBLANES):
        rows = slice(k * seg, (k + 1) * seg)
        h_seg = jnp.concatenate([hs[j, pl.ds(k, seg, stride=SUBLANES), :] for j in range(n_slab)], axis=1)
        ya_ref[0, rows, :] = (gg_ref[0, rows, :].astype(F32) * h_seg).astype(BF16)


def _lru_call(xb8, gg, conv_w, conv_b, wg, ba, bx, lam, h0, n_seq, T):
    gg3 = gg.reshape(n_seq, T, D_LRU)
    n_slab = D_LRU // LANES
    full = lambda shape: pl.BlockSpec(shape, lambda i: (0,) * len(shape))
    seq = lambda shape: pl.BlockSpec(shape, lambda i: (i,) + (0,) * (len(shape) - 1))
    ya, fin = pl.pallas_call(
        functools.partial(_lru_body, T=T),
        grid=(n_seq,),
        in_specs=[pl.BlockSpec((n_slab, T, LANES), lambda i: (0, i, 0)), seq((1, T, D_LRU)),
                  full((CONV_W, D_LRU)), full((1, D_LRU)),
                  full((4, 256, 1024)), full((2, D_LRU)), full((2, D_LRU)), full((2, D_LRU)),
                  seq((1, 2, D_LRU))],
        out_specs=[seq((1, T, D_LRU)), seq((1, 2, D_LRU))],
        out_shape=[jax.ShapeDtypeStruct((n_seq, T, D_LRU), BF16),
                   jax.ShapeDtypeStruct((n_seq, 2, D_LRU), F32)],
        scratch_shapes=[pltpu.VMEM((T + 3 * SUBLANES, D_LRU), F32)] + [pltpu.VMEM((T, D_LRU), F32)] * 4
                       + [pltpu.VMEM((n_slab, T, LANES), F32)],
        compiler_params=_cp(("parallel",)), name="lru",
    )(xb8, gg3, conv_w, conv_b, wg, ba, bx, lam, h0)
    return ya.reshape(n_seq * T, D_LRU), fin


def _s5_body(u_ref, wts_ref, wc_ref, ap_ref, h0_ref, y_ref, fin_ref, s_scr, hp_scr, slab,
             *, rt, n_c, with_h0, col_major):
    q = 512
    m = S5_SEG
    grp = SUBLANES * m
    n_grp = rt // grp
    seg_per_seq = n_c // m
    seq_per_grp = SUBLANES // seg_per_seq
    seq_tok = n_c * S5_L
    halves = seq_tok // (GRID_W * S5_L)
    sub = lax.broadcasted_iota(I32, (SUBLANES, q), 0)

    def pieces(g, i):
        out = []
        for v in range(SUBLANES):
            s, sq = divmod(v, seg_per_seq)
            tok0 = (g * seq_per_grp + s) * seq_tok
            if col_major:
                for h in range(halves):
                    n = m // halves
                    out.append((pl.ds(g * grp + h * SUBLANES + v, n, stride=halves * SUBLANES),
                                pl.ds(tok0 + (h * S5_L + i) * GRID_W + sq * n, n)))
            else:
                out.append((pl.ds(g * grp + v, m, stride=SUBLANES),
                            pl.ds(tok0 + sq * m * S5_L + i, m, stride=S5_L)))
        return out

    for i in range(S5_L):
        for g in range(n_grp):
            for rows, toks in pieces(g, i):
                slab[i, rows, :] = u_ref[0, toks, :]
    ub = jnp.concatenate([slab[i] for i in range(S5_L)], axis=1).astype(BF16)
    r1 = jnp.dot(ub, wts_ref[0], preferred_element_type=F32)
    y_in = r1[:, :S5_L * LANES]
    s_scr[...] = r1[:, S5_L * LANES:]

    def cmul(ar, ai, xr, xi):
        return ar * xr - ai * xi, ar * xi + ai * xr

    def local_scan(base, col0, reverse):
        ar, ai = ap_ref[0, 0:1, col0:col0 + q], ap_ref[0, 0:1, col0 + q:col0 + 2 * q]

        def step(k, carry):
            hr, hi = carry
            r0 = pl.multiple_of(base + (m - 1 - k if reverse else k) * SUBLANES, SUBLANES)
            pr, pi = cmul(ar, ai, hr, hi)
            hr = pr + s_scr[pl.ds(r0, SUBLANES), col0:col0 + q]
            hi = pi + s_scr[pl.ds(r0, SUBLANES), col0 + q:col0 + 2 * q]
            s_scr[pl.ds(r0, SUBLANES), col0:col0 + q] = hr
            s_scr[pl.ds(r0, SUBLANES), col0 + q:col0 + 2 * q] = hi
            return hr, hi
        zero = jnp.zeros((SUBLANES, q), F32)
        return lax.fori_loop(0, m, step, (zero, zero), unroll=2)

    def chain(g, col0, reverse, h_end):
        base = g * grp
        first = (sub & (seg_per_seq - 1)) == (seg_per_seq - 1 if reverse else 0)
        h0r = jnp.zeros((SUBLANES, q), F32)
        h0i = jnp.zeros((SUBLANES, q), F32)
        if with_h0:
            for s in range(seq_per_grp):
                row = h0_ref[0, 0, g * seq_per_grp + s:g * seq_per_grp + s + 1, :]
                mine = (sub >> (seg_per_seq.bit_length() - 1)) == s
                h0r = jnp.where(mine, row[:, col0:col0 + q], h0r)
                h0i = jnp.where(mine, row[:, col0 + q:col0 + 2 * q], h0i)
        cr, ci = h0r, h0i
        if seg_per_seq > 1:
            er, ei = ap_ref[0, m - 1:m, col0:col0 + q], ap_ref[0, m - 1:m, col0 + q:col0 + 2 * q]
            shift = SUBLANES - 1 if reverse else 1
            for _ in range(seg_per_seq - 1):
                pr, pi = cmul(er, ei, cr, ci)
                cr = jnp.where(first, h0r, pltpu.roll(h_end[0] + pr, shift, 0))
                ci = jnp.where(first, h0i, pltpu.roll(h_end[1] + pi, shift, 0))
        if with_h0 or seg_per_seq > 1:
            def add(c, _):
                r0 = pl.multiple_of(base + c * SUBLANES, SUBLANES)
                k = m - 1 - c if reverse else c
                pr, pi = cmul(ap_ref[0, pl.ds(k, 1), col0:col0 + q],
                              ap_ref[0, pl.ds(k, 1), col0 + q:col0 + 2 * q], cr, ci)
                s_scr[pl.ds(r0, SUBLANES), col0:col0 + q] += pr
                s_scr[pl.ds(r0, SUBLANES), col0 + q:col0 + 2 * q] += pi
                return 0
            lax.fori_loop(0, m, add, 0, unroll=2)
        return cr, ci

    for g in range(n_grp):
        base = g * grp
        body = slice(base + SUBLANES, base + grp)
        head = slice(base, base + SUBLANES)
        prev = slice(base, base + grp - SUBLANES)
        tail = slice(base + grp - SUBLANES, base + grp)
        cr, ci = chain(g, 0, False, local_scan(base, 0, False))
        hp_scr[body, 0:2 * q] = s_scr[prev, 0:2 * q]
        hp_scr[head, 0:q] = cr
        hp_scr[head, q:2 * q] = ci
        cr, ci = chain(g, 2 * q, True, local_scan(base, 2 * q, True))
        hp_scr[prev, 2 * q:4 * q] = s_scr[body, 2 * q:4 * q]
        hp_scr[tail, 2 * q:3 * q] = cr
        hp_scr[tail, 3 * q:4 * q] = ci
        if seg_per_seq == 1:
            rows = slice(g * SUBLANES, (g + 1) * SUBLANES)
            fin_ref[0, 0, rows, 0:2 * q] = s_scr[tail, 0:2 * q]
            fin_ref[0, 0, rows, 2 * q:4 * q] = s_scr[head, 2 * q:4 * q]
    if seg_per_seq > 1:
        fin_ref[...] = jnp.zeros(fin_ref.shape, F32)

    y = y_in + jnp.dot(hp_scr[...].astype(BF16), wc_ref[0], preferred_element_type=F32)
    for i in range(S5_L):
        slab[i] = y[:, i * LANES:(i + 1) * LANES]
        for g in range(n_grp):
            for rows, toks in pieces(g, i):
                y_ref[0, toks, :] = slab[i, rows, :]


def _s5_call(us4, wts, wc, apow, h0, n_seq, T, with_h0, col_major):
    n_c = T // S5_L
    rows = n_seq * n_c
    rt = 512
    nseq_t = rt // n_c
    h0 = h0.reshape(S5_BLK, n_seq // nseq_t, nseq_t, S5_SW)
    y4, fin = pl.pallas_call(
        functools.partial(_s5_body, rt=rt, n_c=n_c, with_h0=with_h0, col_major=col_major),
        grid=(S5_BLK, rows // rt),
        in_specs=[pl.BlockSpec((1, rt * S5_L, LANES), lambda b, r: (b, r, 0)),
                  pl.BlockSpec((1, S5_L * LANES, S5_L * LANES + S5_SW), lambda b, r: (b, 0, 0)),
                  pl.BlockSpec((1, S5_SW, S5_L * LANES), lambda b, r: (b, 0, 0)),
                  pl.BlockSpec((1, S5_SEG, S5_SW), lambda b, r: (b, 0, 0)),
                  pl.BlockSpec((1, 1, nseq_t, S5_SW), lambda b, r: (b, r, 0, 0))],
        out_specs=[pl.BlockSpec((1, rt * S5_L, LANES), lambda b, r: (b, r, 0)),
                   pl.BlockSpec((1, 1, nseq_t, S5_SW), lambda b, r: (b, r, 0, 0))],
        out_shape=[jax.ShapeDtypeStruct((S5_BLK, n_seq * T, LANES), F32),
                   jax.ShapeDtypeStruct((S5_BLK, n_seq // nseq_t, nseq_t, S5_SW), F32)],
        scratch_shapes=[pltpu.VMEM((rt, S5_SW), F32), pltpu.VMEM((rt, S5_SW), F32),
                        pltpu.VMEM((S5_L, rt, LANES), F32)],
        compiler_params=_cp(("parallel", "parallel")), name="s5",
    )(us4, wts, wc, apow, h0)
    return y4, fin.reshape(S5_BLK, n_seq, S5_SW)


def _s5_disc(lr, li, st):
    mag = jnp.exp(lr * st)
    ang = li * st
    ar, ai = mag * jnp.cos(ang), mag * jnp.sin(ang)
    den = lr * lr + li * li
    fr = ((ar - 1.0) * lr + ai * li) / den
    fi = (ai * lr - (ar - 1.0) * li) / den
    return ar, ai, fr, fi


def _s5_powers(ar, ai, n):
    out = [(jnp.ones_like(ar), jnp.zeros_like(ar))]
    for _ in range(n):
        pr, pi = out[-1]
        out.append((pr * ar - pi * ai, pr * ai + pi * ar))
    return out


def _s5_prep_body(rowp_ref, colp_ref, flatp_ref, bt_ref, cm_ref, wts_ref, wc_ref, ap_ref):
    L, N, P = S5_L, S5_N, S5_P
    w = L * LANES
    div = lambda x, d: x >> (d.bit_length() - 1)
    mod = lambda x, d: x & (d - 1)
    one = lambda m: jnp.where(m, 1.0, 0.0).astype(BF16)
    r1 = lax.broadcasted_iota(I32, (LANES, LANES), 0)
    c1 = lax.broadcasted_iota(I32, (LANES, LANES), 1)
    first_half = c1 < N
    sgn = jnp.where(first_half, -1.0, 1.0)

    e_maps, taps = [], []
    same_group = div(r1, P) == div(c1, P)
    fold = one(mod(r1, P) == mod(c1, P))
    for d in range(2):
        ar, ai, fr, fi = _s5_disc(rowp_ref[0, d, 0], rowp_ref[1, d, 0], rowp_ref[2, d, 0])
        bx1 = bt_ref[d, 0]
        bx2 = pltpu.roll(bx1, N, 1)
        x1 = fr * bx1 + (fi * sgn) * bx2
        x2 = pltpu.roll(x1, N, 1)
        e_d, t_d = [], []
        for pr, pi in _s5_powers(ar, ai, L):
            e_k = pr * x1 + (pi * sgn) * x2
            e_d.append(e_k)
            full = _dot3(e_k, cm_ref[d, 0])
            kept = jnp.where(same_group, full, 0.0)
            hi = kept.astype(BF16)
            mid = (kept - hi.astype(F32)).astype(BF16)
            lo = (kept - hi.astype(F32) - mid.astype(F32)).astype(BF16)
            t_d.append(sum(jnp.dot(part, fold, preferred_element_type=F32) for part in (hi, mid, lo)))
        e_maps.append(e_d)
        taps.append(t_d)

    r = lax.broadcasted_iota(I32, (LANES, w), 0)
    c = lax.broadcasted_iota(I32, (LANES, w), 1)
    t_tap = one((div(r, P) == div(c, LANES)) & (mod(r, P) == mod(c, P)))
    m_tap = div(r, P) == div(mod(c, LANES), P)
    t_st = one((div(r, N) == div(c, 8 * N)) & (mod(r, N) == mod(c, N)))
    m_st = div(r, P) == div(mod(c, 8 * N), N)
    col_blk = div(c1, P)
    for j in range(L):
        rows = slice(j * LANES, (j + 1) * LANES)
        toe = jnp.zeros((LANES, LANES), F32)
        for k in range(L):
            toe = (toe + jnp.where(col_blk == j + k, taps[0][k], 0.0)
                   + jnp.where(col_blk == j - k, taps[1][k], 0.0))
        tap = jnp.dot(toe.astype(BF16), t_tap, preferred_element_type=F32)
        wts_ref[0, rows, 0:w] = jnp.where(m_tap, tap, 0.0).astype(BF16)
        for k, e_j in enumerate((e_maps[0][L - 1 - j], e_maps[1][j])):
            st = jnp.dot(e_j.astype(BF16), t_st, preferred_element_type=F32)
            wts_ref[0, rows, w + k * w:w + (k + 1) * w] = jnp.where(m_st, st, 0.0).astype(BF16)

    rr = lax.broadcasted_iota(I32, (w, LANES), 0)
    cc = lax.broadcasted_iota(I32, (w, LANES), 1)
    t_row = one((div(rr, 8 * N) == div(cc, N)) & (mod(rr, N) == mod(cc, N)))
    r2 = lax.broadcasted_iota(I32, (w, w), 0)
    c2 = lax.broadcasted_iota(I32, (w, w), 1)
    m_row = div(mod(r2, 8 * N), N) == div(mod(c2, LANES), P)
    for d in range(2):
        ar, ai, _, _ = _s5_disc(colp_ref[0, d, 0], colp_ref[1, d, 0], colp_ref[2, d, 0])
        a1 = cm_ref[d, 0]
        swapped = pltpu.roll(a1, N, 0)
        a2 = jnp.where(r1 < N, swapped, -swapped)
        pw = _s5_powers(ar, ai, L)
        ks = [i + 1 for i in range(L)] if d == 0 else [L - i for i in range(L)]
        g = jnp.concatenate([a1 * pw[k][0] + a2 * pw[k][1] for k in ks], axis=1)
        corr = jnp.dot(t_row, g.astype(BF16), preferred_element_type=F32)
        wc_ref[0, d * w:(d + 1) * w, :] = jnp.where(m_row, corr, 0.0).astype(BF16)

    al = []
    for d in range(2):
        ar, ai, _, _ = _s5_disc(flatp_ref[0, d, 0], flatp_ref[1, d, 0], flatp_ref[2, d, 0])
        al.append(_s5_powers(ar, ai, L)[L])
    cur = al
    for k in range(S5_SEG):
        ap_ref[0, k:k + 1, :] = jnp.concatenate([cur[0][0], cur[0][1], cur[1][0], cur[1][1]], axis=1)
        cur = [(cr * br - ci * bi, cr * bi + ci * br) for (cr, ci), (br, bi) in zip(cur, al)]


def _s5_weights(lam_re, lam_im, log_step, b_re, b_im, c_re, c_im):
    L, G, N, P = S5_L, S5_G, S5_N, S5_P
    w = L * LANES
    step = jnp.broadcast_to(jnp.exp(log_step)[..., None], lam_re.shape)
    prm = jnp.stack([lam_re, lam_im, step])
    rowp = jnp.broadcast_to(prm[:, :, :, None, None, :], (3, 2, G, P, 2, N)).reshape(3, 2, S5_BLK, LANES, 2 * N)
    colp = jnp.broadcast_to(prm.transpose(0, 1, 3, 2)[:, :, None, :, :, None], (3, 2, 2, N, G, P))
    colp = colp.reshape(3, 2, 2 * N, S5_BLK, LANES).transpose(0, 1, 3, 2, 4)
    flatp = prm.reshape(3, 2, S5_BLK, 1, 8 * N)
    bt = jnp.stack([b_re, b_im], axis=3).transpose(0, 1, 4, 3, 2)
    bt = bt.reshape(2, S5_BLK, LANES, 2 * N)
    cm = jnp.stack([c_re, -c_im], axis=1).transpose(0, 1, 4, 2, 3)
    cm = cm.reshape(2, 2 * N, S5_BLK, LANES).transpose(0, 2, 1, 3)
    blk = lambda lead, a, b_: pl.BlockSpec(lead + (1, a, b_), lambda i: (0,) * len(lead) + (i, 0, 0))
    out = lambda a, b_: pl.BlockSpec((1, a, b_), lambda i: (i, 0, 0))
    return pl.pallas_call(
        _s5_prep_body, grid=(S5_BLK,),
        in_specs=[blk((3, 2), LANES, LANES), blk((3, 2), LANES, LANES), blk((3, 2), 1, 8 * N),
                  blk((2,), LANES, LANES), blk((2,), LANES, LANES)],
        out_specs=[out(w, w + S5_SW), out(S5_SW, w), out(S5_SEG, S5_SW)],
        out_shape=[jax.ShapeDtypeStruct((S5_BLK, w, w + S5_SW), BF16),
                   jax.ShapeDtypeStruct((S5_BLK, S5_SW, w), BF16),
                   jax.ShapeDtypeStruct((S5_BLK, S5_SEG, S5_SW), F32)],
        compiler_params=_cp(("parallel",)), name="s5_prep",
    )(rowp, colp, flatp, bt, cm)


def _mixout_body(xp_ref, xs_ref, mod_ref, g1_ref, g2_ref, wbg_ref, bbg_ref, yap_ref, yas_ref,
                 y4p_ref, y4s_ref, u4p_ref, u4s_ref, *rest, tiles_p):
    i = pl.program_id(0)

    @pl.when(i < tiles_p)
    def _():
        _mixout_tile(xp_ref, mod_ref, g1_ref, g2_ref, wbg_ref, bbg_ref, yap_ref, y4p_ref, u4p_ref, *rest)

    @pl.when(i >= tiles_p)
    def _():
        _mixout_tile(xs_ref, mod_ref, g1_ref, g2_ref, wbg_ref, bbg_ref, yas_ref, y4s_ref, u4s_ref, *rest)


def _mixout_tile(x_ref, mod_ref, g1_ref, g2_ref, wbg_ref, bbg_ref, ya_ref, y4_ref, u4_ref,
                 dsk_ref, wglu_ref, bglu_ref, wso_ref, wlo_ref, wo_ref, wrt_ref, brt_ref,
                 x1_ref, h2_ref, aff_ref):
    m = mod_ref[0]
    sh1, sc1, gt1 = m[:, 0:D_MODEL], m[:, D_MODEL:2 * D_MODEL], m[:, 2 * D_MODEL:3 * D_MODEL]
    sh2, sc2 = m[:, 3 * D_MODEL:4 * D_MODEL], m[:, 4 * D_MODEL:5 * D_MODEL]
    x = x_ref[...]
    h = _norm_mod(x, g1_ref[...], sc1, sh1)
    gates = _sigmoid(jnp.dot(h.astype(BF16), wbg_ref[...], preferred_element_type=F32) + bbg_ref[...])

    def assemble(ref):
        return jnp.concatenate([ref[k] for k in range(S5_BLK)], axis=1)

    ys = assemble(y4_ref) + dsk_ref[...] * assemble(u4_ref)
    v = _gelu(ys)
    ob = v * _sigmoid(jnp.dot(v.astype(BF16), wglu_ref[...], preferred_element_type=F32) + bglu_ref[...])
    merged = (gates[:, :D_MODEL] * jnp.dot(ya_ref[...], wlo_ref[...], preferred_element_type=F32)
              + gates[:, D_MODEL:] * jnp.dot(ob.astype(BF16), wso_ref[...], preferred_element_type=F32))
    mix = jnp.dot(merged.astype(BF16), wo_ref[...], preferred_element_type=F32)
    x1 = x + gt1 * mix
    x1_ref[...] = x1
    h2 = _norm_mod(x1, g2_ref[...], sc2, sh2)
    h2_ref[...] = h2
    logits = _dot3(wrt_ref[...], h2, dims=(((1,), (1,)), ((), ()))) + brt_ref[...]
    mx = jnp.max(logits, axis=0, keepdims=True)
    ex = jnp.exp(logits - mx)
    aff_ref[...] = ex / jnp.sum(ex, axis=0, keepdims=True)


def _mixout_call(xp, xs, mod3, g1, g2, wbg, bbg, ya_p, ya_s, y4_p, y4_s, u4_p, u4_s,
                 dsk, wglu, bglu, wso, wlo, wo, wrt, brt, mod_row, tm):
    n_p, n_s = xp.shape[0], xs.shape[0]
    tiles_p = n_p // tm
    n = n_p + n_s
    full = lambda shape: pl.BlockSpec(shape, lambda i: (0,) * len(shape))
    at_p = lambda i: jnp.minimum(i, tiles_p - 1)
    at_s = lambda i: jnp.maximum(i - tiles_p, 0)
    tok = lambda w, at: pl.BlockSpec((tm, w), lambda i: (at(i), 0))
    s5 = lambda at: pl.BlockSpec((S5_BLK, tm, LANES), lambda i: (0, at(i), 0))
    out = lambda w: pl.BlockSpec((tm, w), lambda i: (i, 0))
    return pl.pallas_call(
        functools.partial(_mixout_body, tiles_p=tiles_p),
        grid=(n // tm,),
        in_specs=[tok(D_MODEL, at_p), tok(D_MODEL, at_s),
                  pl.BlockSpec((1, 1, 6 * D_MODEL), lambda i: (mod_row(i), 0, 0)),
                  full((1, D_MODEL)), full((1, D_MODEL)), full((D_MODEL, 2 * D_MODEL)),
                  full((1, 2 * D_MODEL)), tok(D_LRU, at_p), tok(D_LRU, at_s),
                  s5(at_p), s5(at_s), s5(at_p), s5(at_s), full((1, D_S5)),
                  full((D_S5, D_S5)), full((1, D_S5)), full((D_S5, D_MODEL)), full((D_LRU, D_MODEL)),
                  full((D_MODEL, D_MODEL)), full((N_EXPERTS, D_MODEL)), full((N_EXPERTS, 1))],
        out_specs=[out(D_MODEL), out(D_MODEL), pl.BlockSpec((N_EXPERTS, tm), lambda i: (0, i))],
        out_shape=[jax.ShapeDtypeStruct((n, D_MODEL), F32), jax.ShapeDtypeStruct((n, D_MODEL), F32),
                   jax.ShapeDtypeStruct((N_EXPERTS, n), F32)],
        compiler_params=_cp(("arbitrary",)), name="mixout",
    )(xp, xs, mod3, g1, g2, wbg, bbg, ya_p, ya_s, y4_p, y4_s, u4_p, u4_s,
      dsk, wglu, bglu, wso, wlo, wo, wrt, brt)


def _select_body(aff_ref, gate_ref, pos_ref, offs_ref, *, n_tok, cap):
    aff = aff_ref[...]
    capf = float(cap)

    def bis(_, lh):
        lo, hi = lh
        mid = lo + ((hi - lo + 1) >> 1)
        cnt = jnp.sum(jnp.where(aff >= pltpu.bitcast(mid, F32), 1.0, 0.0), axis=1, keepdims=True)
        ok = cnt >= capf
        return jnp.where(ok, mid, lo), jnp.where(ok, hi, mid - 1)

    lo0 = jnp.zeros((N_EXPERTS, 1), I32)
    hi0 = jnp.full((N_EXPERTS, 1), 0x7F800000, I32)
    thr_bits, _ = lax.fori_loop(0, 31, bis, (lo0, hi0))
    thr = pltpu.bitcast(thr_bits, F32)
    need = capf - jnp.sum(jnp.where(aff > thr, 1.0, 0.0), axis=1, keepdims=True)

    r = lax.broadcasted_iota(I32, (TOK_TILE, TOK_TILE), 0)
    c = lax.broadcasted_iota(I32, (TOK_TILE, TOK_TILE), 1)
    tri = jnp.where(r < c, 1.0, 0.0).astype(BF16)
    lane = lax.broadcasted_iota(I32, (N_EXPERTS, LANES), 1)
    n_tiles = n_tok // TOK_TILE
    run_eq = jnp.zeros((N_EXPERTS, 1), F32)
    run_sel = jnp.zeros((N_EXPERTS, 1), F32)
    offs = jnp.zeros((N_EXPERTS, LANES), F32)
    for t in range(n_tiles):
        sl = slice(t * TOK_TILE, (t + 1) * TOK_TILE)
        aff_t = aff_ref[:, sl]
        is_eq = aff_t == thr
        eq_t = jnp.where(is_eq, 1.0, 0.0)
        rank_eq = jnp.dot(eq_t.astype(BF16), tri, preferred_element_type=F32) + run_eq
        cand = (aff_t > thr) | (is_eq & (rank_eq < need))
        pos_t = jnp.dot(jnp.where(cand, 1.0, 0.0).astype(BF16), tri, preferred_element_type=F32) + run_sel
        sel_t = cand & (pos_t < capf)
        sel_f = jnp.where(sel_t, 1.0, 0.0)
        offs = jnp.where(lane == t, run_sel, offs)
        gate_ref[:, sl] = jnp.where(sel_t, aff_t, 0.0)
        pos_ref[:, sl] = jnp.where(sel_t, pos_t, -1.0).astype(I32)
        run_eq = run_eq + jnp.sum(eq_t, axis=1, keepdims=True)
        run_sel = run_sel + jnp.sum(sel_f, axis=1, keepdims=True)
    offs = jnp.where(lane >= n_tiles, run_sel, offs)
    offs_ref[...] = offs.astype(I32)


def _select_call(aff_t, cap, n_tok):
    n_path = aff_t.shape[1] // n_tok
    per_p = lambda a, b: pl.BlockSpec((None, a, b), lambda p: (p, 0, 0))
    return pl.pallas_call(
        functools.partial(_select_body, n_tok=n_tok, cap=cap),
        grid=(n_path,),
        in_specs=[pl.BlockSpec((N_EXPERTS, n_tok), lambda p: (0, p))],
        out_specs=[per_p(N_EXPERTS, n_tok), per_p(N_EXPERTS, n_tok), per_p(N_EXPERTS, LANES)],
        out_shape=[jax.ShapeDtypeStruct((n_path, N_EXPERTS, n_tok), F32),
                   jax.ShapeDtypeStruct((n_path, N_EXPERTS, n_tok), I32),
                   jax.ShapeDtypeStruct((n_path, N_EXPERTS, LANES), I32)],
        compiler_params=_cp(("parallel",)), name="select",
    )(aff_t)


def _invert_body(offs_ref, pos_ref, gate_ref, idx_ref, gcol_ref, acc_i, acc_g, *, n_tiles, cap):
    path = pl.program_id(0)
    base = (path * N_EXPERTS + pl.program_id(1)) * LANES
    acc_i[...] = jnp.zeros(acc_i.shape, F32)
    acc_g[...] = jnp.zeros(acc_g.shape, F32)
    jrow = lax.broadcasted_iota(I32, (SLOT_CHUNK, LANES), 0)
    lane = lax.broadcasted_iota(I32, (1, LANES), 1)

    def per_tile(t, _):
        first = (offs_ref[base + t] >> 3) << 3
        prow = pos_ref[0, 0, pl.ds(t, 1), :]
        grow = gate_ref[0, 0, pl.ds(t, 1), :]
        tok = (lane + (t * TOK_TILE + 1)).astype(F32)

        def window(w, _):
            start = pl.multiple_of(first + w * SLOT_CHUNK, SUBLANES)
            rows = pl.ds(start, SLOT_CHUNK)
            hit = prow == (jrow + start)
            acc_i[rows, :] = jnp.maximum(acc_i[rows, :], jnp.where(hit, tok, 0.0))
            acc_g[rows, :] = jnp.maximum(acc_g[rows, :], jnp.where(hit, grow, 0.0))
            return 0

        n_win = (offs_ref[base + t + 1] - first + SLOT_CHUNK - 1) >> (SLOT_CHUNK.bit_length() - 1)
        lax.fori_loop(0, n_win, window, 0)
        return 0

    lax.fori_loop(0, n_tiles, per_tile, 0)
    hit_i = acc_i[0:cap, :]
    hi = hit_i.astype(BF16)
    mid = (hit_i - hi.astype(F32)).astype(BF16)
    lo = (hit_i - hi.astype(F32) - mid.astype(F32)).astype(BF16)
    ones = jnp.ones((SUBLANES, LANES), BF16)
    nt = (((1,), (1,)), ((), ()))
    tok1 = sum(lax.dot_general(ones, part, nt, preferred_element_type=F32) for part in (hi, mid, lo))
    idx_ref[0, 0] = tok1[0:1, :].astype(I32) - 1 + path * (n_tiles * TOK_TILE)
    gcol_ref[0, 0] = jnp.max(acc_g[0:cap, :], axis=1, keepdims=True)


def _invert_call(offs, pos, gate, cap):
    n_path, _, n_tok = pos.shape
    n_tiles = n_tok // TOK_TILE
    pos4 = pos.reshape(n_path, N_EXPERTS, n_tiles, TOK_TILE)
    gate4 = gate.reshape(n_path, N_EXPERTS, n_tiles, TOK_TILE)
    per_e = lambda a, b: pl.BlockSpec((1, 1, a, b), lambda p, e, o: (p, e, 0, 0))
    return pl.pallas_call(
        functools.partial(_invert_body, n_tiles=n_tiles, cap=cap),
        grid_spec=pltpu.PrefetchScalarGridSpec(
            num_scalar_prefetch=1, grid=(n_path, N_EXPERTS),
            in_specs=[per_e(n_tiles, TOK_TILE), per_e(n_tiles, TOK_TILE)],
            out_specs=[per_e(1, cap), per_e(cap, 1)],
            scratch_shapes=[pltpu.VMEM((cap + SLOT_CHUNK, LANES), F32)] * 2),
        out_shape=[jax.ShapeDtypeStruct((n_path, N_EXPERTS, 1, cap), I32),
                   jax.ShapeDtypeStruct((n_path, N_EXPERTS, cap, 1), F32)],
        compiler_params=_cp(("parallel", "parallel")), name="invert",
    )(offs.reshape(-1), pos4, gate4)


def _ffn_body(idx0_ref, idxn_ref, gcol_ref, wg_ref, wu_ref, wd_ref, h_hbm, out_ref,
              xe, wgb, wub, wdb, sem, *, cap):
    e = pl.program_id(0)
    p = pl.program_id(1)
    nxt = 1 - p
    step = e * 2 + p

    def issue_rows(idx_ref, s, j0, n):
        for j in range(j0, j0 + n):
            pltpu.make_async_copy(h_hbm.at[pl.ds(idx_ref[0, 0, 0, j], 1), :],
                                  xe.at[s, pl.ds(j, 1), :], sem.at[s]).start()

    @pl.when(step == 0)
    def _():
        def issue(j, _):
            pltpu.make_async_copy(h_hbm.at[pl.ds(idx0_ref[0, 0, 0, j], 1), :],
                                  xe.at[0, pl.ds(j, 1), :], sem.at[0]).start()
            return 0
        lax.fori_loop(0, cap, issue, 0, unroll=8)

    @pl.when(p == 0)
    def _():
        wgb[...] = wg_ref[0].astype(BF16)
        wub[...] = wu_ref[0].astype(BF16)
        wdb[...] = wd_ref[0].astype(BF16)

    pltpu.make_async_copy(h_hbm.at[pl.ds(0, cap), :], xe.at[p], sem.at[p]).wait()

    ch = 256
    n_ch = cap // ch
    per = -(-cap // (n_ch - 1))
    for c in range(n_ch):
        sl = slice(c * ch, (c + 1) * ch)
        xb = xe[p, sl, :].astype(BF16)
        issue_rows(idxn_ref, nxt, min(c * per, cap), min((c + 1) * per, cap) - min(c * per, cap))
        g = jnp.dot(xb, wgb[...], preferred_element_type=F32)
        u = jnp.dot(xb, wub[...], preferred_element_type=F32)
        hid = (g * _sigmoid(g)) * u
        ye = jnp.dot(hid.astype(BF16), wdb[...], preferred_element_type=F32)
        out_ref[0, 0, sl, :] = ye * gcol_ref[0, 0, sl, :]

    @pl.when(step == 2 * N_EXPERTS - 1)
    def _():
        pltpu.make_async_copy(h_hbm.at[pl.ds(0, cap), :], xe.at[nxt], sem.at[nxt]).wait()


def _ffn_call(idx, gcol, w_gate, w_up, w_down, h2, cap):
    d_e = w_gate.shape[2]
    wspec = lambda a, b: pl.BlockSpec((1, a, b), lambda e, p: (e, 0, 0))
    return pl.pallas_call(
        functools.partial(_ffn_body, cap=cap),
        grid=(N_EXPERTS, 2),
        in_specs=[pl.BlockSpec((1, 1, 1, cap), lambda e, p: (0, 0, 0, 0), memory_space=pltpu.SMEM),
                  pl.BlockSpec((1, 1, 1, cap),
                               lambda e, p: (1 - p, jnp.minimum(e + p, N_EXPERTS - 1), 0, 0),
                               memory_space=pltpu.SMEM),
                  pl.BlockSpec((1, 1, cap, 1), lambda e, p: (p, e, 0, 0)),
                  wspec(D_MODEL, d_e), wspec(D_MODEL, d_e), wspec(d_e, D_MODEL),
                  pl.BlockSpec(memory_space=pl.ANY)],
        out_specs=pl.BlockSpec((1, 1, cap, D_MODEL), lambda e, p: (p, e, 0, 0)),
        out_shape=jax.ShapeDtypeStruct((2, N_EXPERTS, cap, D_MODEL), F32),
        scratch_shapes=[pltpu.VMEM((2, cap, D_MODEL), F32), pltpu.VMEM((D_MODEL, d_e), BF16),
                        pltpu.VMEM((D_MODEL, d_e), BF16), pltpu.VMEM((d_e, D_MODEL), BF16),
                        pltpu.SemaphoreType.DMA((2,))],
        compiler_params=_cp(("arbitrary", "arbitrary")), name="ffn",
    )(idx, idx, gcol, w_gate, w_up, w_down, h2)


def _combine_body(offs_ref, x1_ref, mod_ref, gf_ref, pos_ref, yb_hbm, y_ref, wins, acc_ref, sem,
                  *, cap, path):
    t = pl.program_id(0)
    n_t = pl.num_programs(0)
    slot = t % 2
    row0 = path * N_EXPERTS * LANES

    def kmax_of(tt):
        k = 0
        for e in range(N_EXPERTS):
            k = jnp.maximum(k, offs_ref[row0 + e * LANES + tt + 1] - offs_ref[row0 + e * LANES + tt])
        return k

    def geometry(tt, e, w, win):
        fetch = win + SUBLANES
        lo = offs_ref[row0 + e * LANES + tt] + w * win
        start = pl.multiple_of(jnp.minimum((lo >> 3) << 3, cap - fetch), SUBLANES)
        return lo, start, fetch

    def issue(tt, w, s, win):
        for e in range(N_EXPERTS):
            _, start, fetch = geometry(tt, e, w, win)
            pltpu.make_async_copy(yb_hbm.at[path, e, pl.ds(start, fetch), :],
                                  wins.at[s, pl.ds(e * fetch, fetch), :], sem.at[s]).start()

    def drain(s, win):
        rows = N_EXPERTS * (win + SUBLANES)
        pltpu.make_async_copy(yb_hbm.at[0, 0, pl.ds(0, rows), :], wins.at[s, pl.ds(0, rows), :],
                              sem.at[s]).wait()

    def expand(w, s, win):
        sub = lax.broadcasted_iota(I32, (win + SUBLANES, TOK_TILE), 0)
        rows = []
        for e in range(N_EXPERTS):
            lo, start, _ = geometry(t, e, w, win)
            pos = pos_ref[e:e + 1, :]
            in_round = jnp.logical_and(pos >= lo, pos < lo + win)
            rows.append(jnp.where(jnp.logical_and(pos - start == sub, in_round), 1.0, 0.0))
        onehot = jnp.concatenate(rows, axis=0).T.astype(BF16)
        data = wins[s, 0:N_EXPERTS * (win + SUBLANES), :]
        hi = data.astype(BF16)
        lo_part = (data - hi.astype(F32)).astype(BF16)
        return (jnp.dot(onehot, hi, preferred_element_type=F32)
                + jnp.dot(onehot, lo_part, preferred_element_type=F32))

    def issue_first(tt, s):
        wide = kmax_of(tt) > COMBINE_WIN

        @pl.when(wide)
        def _():
            issue(tt, 0, s, COMBINE_WIN_WIDE)

        @pl.when(jnp.logical_not(wide))
        def _():
            issue(tt, 0, s, COMBINE_WIN)

    @pl.when(t == 0)
    def _():
        issue_first(0, 0)

    @pl.when(t + 1 < n_t)
    def _():
        issue_first(t + 1, 1 - slot)

    kmax = kmax_of(t)

    @pl.when(kmax > COMBINE_WIN)
    def _():
        drain(slot, COMBINE_WIN_WIDE)
        acc_ref[...] = expand(0, slot, COMBINE_WIN_WIDE)

    @pl.when(kmax <= COMBINE_WIN)
    def _():
        drain(slot, COMBINE_WIN)
        acc_ref[...] = expand(0, slot, COMBINE_WIN)

    def more(w, _):
        issue(t, w, slot, COMBINE_WIN_WIDE)
        drain(slot, COMBINE_WIN_WIDE)
        acc_ref[...] += expand(w, slot, COMBINE_WIN_WIDE)
        return 0
    lax.fori_loop(1, (kmax + COMBINE_WIN_WIDE - 1) >> (COMBINE_WIN_WIDE.bit_length() - 1), more, 0)
    acc = acc_ref[...]

    gt2 = mod_ref[0][:, 5 * D_MODEL:6 * D_MODEL]
    x2 = x1_ref[...] + gt2 * acc
    ms = jnp.mean(x2 * x2, axis=-1, keepdims=True)
    y_ref[...] = x2 * lax.rsqrt(ms + EPS) * gf_ref[...]


def _combine_call(offs, pos, x1, mod3, gf, ybuf, mod_row, cap, path, n):
    first = path * n // TOK_TILE
    return pl.pallas_call(
        functools.partial(_combine_body, cap=cap, path=path),
        grid_spec=pltpu.PrefetchScalarGridSpec(
            num_scalar_prefetch=1, grid=(n // TOK_TILE,),
            in_specs=[pl.BlockSpec((TOK_TILE, D_MODEL), lambda i, o: (first + i, 0)),
                      pl.BlockSpec((1, 1, 6 * D_MODEL), lambda i, o: (mod_row(i), 0, 0)),
                      pl.BlockSpec((1, D_MODEL), lambda i, o: (0, 0)),
                      pl.BlockSpec((None, N_EXPERTS, TOK_TILE), lambda i, o: (path, 0, i)),
                      pl.BlockSpec(memory_space=pl.ANY)],
            out_specs=pl.BlockSpec((TOK_TILE, D_MODEL), lambda i, o: (i, 0)),
            scratch_shapes=[pltpu.VMEM((2, N_EXPERTS * (COMBINE_WIN_WIDE + SUBLANES), D_MODEL), F32),
                            pltpu.VMEM((TOK_TILE, D_MODEL), F32),
                            pltpu.SemaphoreType.DMA((2,))]),
        out_shape=jax.ShapeDtypeStruct((n, D_MODEL), F32),
        compiler_params=_cp(("arbitrary",)), name="combine",
    )(offs.reshape(-1), x1, mod3, gf, pos, ybuf)


def _lru_gate_weights(wa, wx):
    eye = jnp.eye(4, dtype=F32)

    def bd(w):
        w5 = w.reshape(4, 4, LRU_HEAD_DIM, LRU_HEAD_DIM)
        return jnp.einsum('khij,hg->khigj', w5, eye).reshape(4, 256, 256)

    return jnp.concatenate([bd(wa[0]), bd(wa[1]), bd(wx[0]), bd(wx[1])], axis=2).astype(BF16)


def kernel(x_prompt, x_sample, state_lru, state_s5_re, state_s5_im, c, c_ctx, w_mod, b_mod, g_norm1, g_norm2, w_in, conv_w, conv_b, lru_wa, lru_ba, lru_wx, lru_bx, lru_lambda, s5_lambda_re, s5_lambda_im, s5_log_step, s5_b_re, s5_b_im, s5_c_re, s5_c_im, s5_d, s5_w_glu, s5_b_glu, w_lru_out, w_s5_out, w_branch_gate, b_branch_gate, w_o, w_router, b_router, w_e_gate, w_e_up, w_e_down, g_final):
    bp, tp, _ = x_prompt.shape
    bs, ts, _ = x_sample.shape
    n_p, n_s = bp * tp, bs * ts
    l = 0

    c_all = jnp.zeros((16, D_MODEL), F32).at[0].set(c_ctx).at[1:1 + bs].set(c)
    mod3 = _mod_call(c_all, w_mod[l], b_mod[l][None, :]).reshape(16, 1, 6 * D_MODEL)

    w_in_b = w_in[l].astype(BF16)
    w_main, w_s5in = w_in_b[:, :2 * D_LRU], w_in_b[:, 2 * D_LRU:]
    g1 = g_norm1[l][None, :]
    g2 = g_norm2[l][None, :]
    wg = _lru_gate_weights(lru_wa[l], lru_wx[l])
    wts, wc, apow = _s5_weights(s5_lambda_re[l], s5_lambda_im[l], s5_log_step[l],
                                s5_b_re[l], s5_b_im[l], s5_c_re[l], s5_c_im[l])
    wbg = w_branch_gate[l].astype(BF16)
    bbg = b_branch_gate[l][None, :]
    wglu = s5_w_glu[l].astype(BF16)
    wso = w_s5_out[l].astype(BF16)
    wlo = w_lru_out[l].astype(BF16)
    wo = w_o[l].astype(BF16)
    wrt = w_router[l].T
    brt = b_router[l][:, None]

    xp2 = x_prompt.reshape(n_p, D_MODEL)
    xs2 = x_sample.reshape(n_s, D_MODEL)
    ctx_row = lambda i: 0
    tm_in = 512
    lat_row_in = lambda i: 1 + i // (ts // max(tm_in, ts))

    def s5_state(re, im):
        def part(a, d):
            return a[:, d].reshape(-1, S5_BLK, 8 * S5_N).transpose(1, 0, 2)
        return jnp.concatenate([part(re, 0), part(im, 0), part(re, 1), part(im, 1)], axis=2)

    xb_p, gg_p, us4_p = _inproj_call(xp2, mod3, g1, w_main, w_s5in, ctx_row, max(tm_in, tp), tp)
    ya_p, lru_fin = _lru_call(xb_p, gg_p, conv_w[l], conv_b[l][None, :], wg, lru_ba[l], lru_bx[l],
                              lru_lambda[l], jnp.zeros((bp, 2, D_LRU), F32), bp, tp)
    y4_p, s5_fin = _s5_call(us4_p, wts, wc, apow, jnp.zeros((S5_BLK, bp, S5_SW), F32), bp, tp,
                            with_h0=False, col_major=False)
    xb_s, gg_s, us4_s = _inproj_call(xs2, mod3, g1, w_main, w_s5in, lat_row_in, max(tm_in, ts), ts)
    ya_s, _ = _lru_call(xb_s, gg_s, conv_w[l], conv_b[l][None, :], wg, lru_ba[l], lru_bx[l],
                        lru_lambda[l], state_lru[:, l], bs, ts)
    y4_s, _ = _s5_call(us4_s, wts, wc, apow, s5_state(state_s5_re[:, l], state_s5_im[:, l]), bs, ts,
                       with_h0=True, col_major=True)

    tm_mo = 512
    tiles_p = n_p // tm_mo
    x1, h2, aff = _mixout_call(
        xp2, xs2, mod3, g1, g2, wbg, bbg, ya_p, ya_s, y4_p, y4_s, us4_p, us4_s,
        s5_d[l][None, :], wglu, s5_b_glu[l][None, :], wso, wlo, wo, wrt, brt,
        mod_row=lambda i: jnp.where(i < tiles_p, 0, 1 + (i - tiles_p) // (ts // tm_mo)), tm=tm_mo)

    assert n_p == n_s
    cap = (CAPACITY_FACTOR * n_p) // N_EXPERTS
    gate, pos, offs = _select_call(aff, cap, n_p)
    idx, gcol = _invert_call(offs, pos, gate, cap)
    ybuf = _ffn_call(idx, gcol, w_e_gate[l], w_e_up[l], w_e_down[l], h2, cap)
    gf = g_final[None, :]
    y_p = _combine_call(offs, pos, x1, mod3, gf, ybuf, ctx_row, cap, 0, n_p)
    y_s = _combine_call(offs, pos, x1, mod3, gf, ybuf, lambda i: 1 + i // (ts // TOK_TILE), cap, 1, n_s)

    new_lru = lru_fin[:, None]
    sf = s5_fin.transpose(1, 0, 2).reshape(bp, S5_BLK, 2, 2, 8, S5_N)
    sf = sf.transpose(0, 2, 3, 1, 4, 5).reshape(bp, 2, 2, S5_G, S5_N)
    new_s5r = sf[:, :, 0][:, None]
    new_s5i = sf[:, :, 1][:, None]
    return (y_p.reshape(bp, tp, D_MODEL), y_s.reshape(bs, ts, D_MODEL), new_lru, new_s5r, new_s5i)
```

```python
import functools
import math

import jax
import jax.numpy as jnp
from jax import lax
from jax.experimental import pallas as pl
from jax.experimental.pallas import tpu as pltpu

F32 = jnp.float32
BF16 = jnp.bfloat16
I32 = jnp.int32

D_MODEL = 1024
D_LRU = 1024
LRU_HEADS = 16
LRU_HEAD_DIM = 64
LRU_C = 8.0
CONV_W = 4
D_S5 = 512
S5_P = 16
S5_G = 32
S5_N = 64
GRID_W = 64
N_EXPERTS = 16
CAPACITY_FACTOR = 2
EPS = 1e-6

LANES = 128
SUBLANES = 8
S5_L = 8
S5_BLK = 4
S5_SW = 4 * 512
S5_SEG = 32
TOK_TILE = 128
SLOT_CHUNK = 64
COMBINE_WIN = 32
COMBINE_WIN_WIDE = 64
VMEM_LIMIT = 56 * 1024 * 1024


def _cp(sem, vmem=VMEM_LIMIT):
    return pltpu.CompilerParams(dimension_semantics=sem, vmem_limit_bytes=vmem)


def _split2(a):
    hi = a.astype(BF16)
    lo = (a - hi.astype(F32)).astype(BF16)
    return hi, lo


def _dot3(a, b, dims=(((1,), (0,)), ((), ()))):
    ah, al = _split2(a)
    bh, bl = _split2(b)
    d = functools.partial(lax.dot_general, dimension_numbers=dims, preferred_element_type=F32)
    return d(ah, bh) + (d(al, bh) + d(ah, bl))


def _sigmoid(x):
    return 0.5 * jnp.tanh(0.5 * x) + 0.5


def _gelu(x):
    c = math.sqrt(2.0 / math.pi)
    return 0.5 * x * (1.0 + jnp.tanh(c * (x + 0.044715 * (x * x * x))))


def _norm_mod(x, g, scale, shift):
    ms = jnp.mean(x * x, axis=-1, keepdims=True)
    return (x * lax.rsqrt(ms + EPS) * g) * (1.0 + scale) + shift


def _mod_body(c_ref, w_ref, b_ref, o_ref):
    c = c_ref[...]
    s = c * _sigmoid(c)
    o_ref[...] = _dot3(s, w_ref[...]) + b_ref[...]


def _mod_call(c_all, w_mod, b_mod):
    n = w_mod.shape[1]
    tn = 1536
    return pl.pallas_call(
        _mod_body,
        grid=(n // tn,),
        in_specs=[pl.BlockSpec((16, D_MODEL), lambda j: (0, 0)),
                  pl.BlockSpec((D_MODEL, tn), lambda j: (0, j)),
                  pl.BlockSpec((1, tn), lambda j: (0, j))],
        out_specs=pl.BlockSpec((16, tn), lambda j: (0, j)),
        out_shape=jax.ShapeDtypeStruct((16, n), F32),
        compiler_params=_cp(("arbitrary",)),
        name="mod",
    )(c_all, w_mod, b_mod)


def _inproj_body(x_ref, mod_ref, g_ref, w_ref, ws_ref, xb_ref, gg_ref, us_ref, *, seq_len):
    m = mod_ref[0]
    h = _norm_mod(x_ref[...], g_ref[...], m[:, D_MODEL:2 * D_MODEL], m[:, 0:D_MODEL])
    hb = h.astype(BF16)
    z = jnp.dot(hb, w_ref[...], preferred_element_type=F32)
    seg = seq_len // SUBLANES
    for s in range(x_ref.shape[0] // seq_len):
        for k in range(SUBLANES):
            t0 = s * seq_len + k * seg
            for c in range(D_LRU // LANES):
                xb_ref[c, pl.ds(s * seq_len + k, seg, stride=SUBLANES), :] = (
                    z[t0:t0 + seg, c * LANES:(c + 1) * LANES])
    gg_ref[...] = _gelu(z[:, D_LRU:]).astype(BF16)
    us = jnp.dot(hb, ws_ref[...], preferred_element_type=F32)
    for k in range(S5_BLK):
        us_ref[k] = us[:, k * LANES:(k + 1) * LANES]


def _inproj_call(x2d, mod3, g1, w_main, w_s5, mod_row, tm, seq_len):
    n = x2d.shape[0]
    return pl.pallas_call(
        functools.partial(_inproj_body, seq_len=seq_len), grid=(n // tm,),
        in_specs=[pl.BlockSpec((tm, D_MODEL), lambda i: (i, 0)),
                  pl.BlockSpec((1, 1, 6 * D_MODEL), lambda i: (mod_row(i), 0, 0)),
                  pl.BlockSpec((1, D_MODEL), lambda i: (0, 0)),
                  pl.BlockSpec((D_MODEL, 2 * D_LRU), lambda i: (0, 0)),
                  pl.BlockSpec((D_MODEL, D_S5), lambda i: (0, 0))],
        out_specs=[pl.BlockSpec((D_LRU // LANES, tm, LANES), lambda i: (0, i, 0)),
                   pl.BlockSpec((tm, D_LRU), lambda i: (i, 0)),
                   pl.BlockSpec((S5_BLK, tm, LANES), lambda i: (0, i, 0))],
        out_shape=[jax.ShapeDtypeStruct((D_LRU // LANES, n, LANES), F32),
                   jax.ShapeDtypeStruct((n, D_LRU), BF16),
                   jax.ShapeDtypeStruct((S5_BLK, n, LANES), F32)],
        compiler_params=_cp(("parallel",)), name="inproj",
    )(x2d, mod3, g1, w_main, w_s5)


def _lru_body(xb_ref, gg_ref, cw_ref, cb_ref, wg_ref, ba_ref, bx_ref, lam_ref, h0_ref,
              ya_ref, fin_ref, xpad, a_f, b_f, a_b, b_b, hs, *, T):
    seg = T // SUBLANES
    ch = 256
    nch = T // ch
    n_slab = D_LRU // LANES
    pad = 2 * SUBLANES
    sub = lax.broadcasted_iota(I32, (SUBLANES, D_LRU), 0)

    def rows_of(r0, n):
        return jnp.concatenate([xb_ref[c, r0:r0 + n, :] for c in range(n_slab)], axis=1)

    def from_prev_segment(tile):
        return jnp.where(sub >= 1, pltpu.roll(tile, 1, 0), 0.0)

    def from_next_segment(tile):
        return jnp.where(sub < SUBLANES - 1, pltpu.roll(tile, SUBLANES - 1, 0), 0.0)

    xpad[0:SUBLANES, :] = from_prev_segment(rows_of((seg - 2) * SUBLANES, SUBLANES))
    xpad[SUBLANES:pad, :] = from_prev_segment(rows_of((seg - 1) * SUBLANES, SUBLANES))
    xpad[pad + T:pad + T + SUBLANES, :] = from_next_segment(rows_of(0, SUBLANES))
    for c in range(nch):
        xpad[pad + c * ch:pad + (c + 1) * ch, :] = rows_of(c * ch, ch)

    nl = -lam_ref[...]
    softplus = jnp.maximum(nl, 0.0) + jnp.log1p(jnp.exp(-jnp.abs(nl)))
    decay = (-LRU_C) * softplus
    scr = ((a_f, b_f), (a_b, b_b))

    for c in range(nch):
        t0 = c * ch
        xc = cb_ref[...] + sum(
            xpad[t0 + k * SUBLANES:t0 + k * SUBLANES + ch, :] * cw_ref[k:k + 1, :]
            for k in range(CONV_W))
        xcb = xc.astype(BF16)
        pre = [jnp.dot(xcb[:, kb * 256:(kb + 1) * 256], wg_ref[kb], preferred_element_type=F32)
               for kb in range(4)]
        for d in range(2):
            ra = jnp.concatenate([p[:, d * 256:(d + 1) * 256] for p in pre], axis=1)
            gx = jnp.concatenate([p[:, 512 + d * 256:512 + (d + 1) * 256] for p in pre], axis=1)
            r = _sigmoid(ra + ba_ref[d:d + 1, :])
            gi = _sigmoid(gx + bx_ref[d:d + 1, :])
            a = jnp.exp(r * decay[d:d + 1, :])
            scr[d][0][t0:t0 + ch, :] = a
            scr[d][1][t0:t0 + ch, :] = jnp.exp(0.5 * jnp.log(1.0 - a * a)) * gi * xc

    def local_scan(a_ref, u_ref, reverse):
        def step(i, carry):
            h, p = carry
            r0 = pl.multiple_of((seg - 1 - i if reverse else i) * SUBLANES, SUBLANES)
            a = a_ref[pl.ds(r0, SUBLANES), :]
            h = a * h + u_ref[pl.ds(r0, SUBLANES), :]
            p = a * p
            u_ref[pl.ds(r0, SUBLANES), :] = h
            a_ref[pl.ds(r0, SUBLANES), :] = p
            return h, p
        init = (jnp.zeros((SUBLANES, D_LRU), F32), jnp.ones((SUBLANES, D_LRU), F32))
        return lax.fori_loop(0, seg, step, init, unroll=2)

    def carry_in(h_end, p_end, h0_row, reverse):
        edge = sub == (SUBLANES - 1 if reverse else 0)
        shift = SUBLANES - 1 if reverse else 1
        c = jnp.broadcast_to(h0_row, (SUBLANES, D_LRU))
        for _ in range(SUBLANES - 1):
            c = jnp.where(edge, h0_row, pltpu.roll(h_end + p_end * c, shift, 0))
        return c

    hf_end, pf_end = local_scan(a_f, b_f, False)
    hb_end, pb_end = local_scan(a_b, b_b, True)
    c_f = carry_in(hf_end, pf_end, h0_ref[0, 0:1, :], False)
    c_b = carry_in(hb_end, pb_end, h0_ref[0, 1:2, :], True)
    fin_ref[0, 0:1, :] = (hf_end + pf_end * c_f)[SUBLANES - 1:SUBLANES, :]
    fin_ref[0, 1:2, :] = (hb_end + pb_end * c_b)[0:1, :]

    for c in range(nch):
        sl = slice(c * ch, (c + 1) * ch)
        tile3 = (ch // SUBLANES, SUBLANES, D_LRU)
        hsum = ((b_f[sl, :].reshape(tile3) + a_f[sl, :].reshape(tile3) * c_f[None])
                + (b_b[sl, :].reshape(tile3) + a_b[sl, :].reshape(tile3) * c_b[None])).reshape(ch, D_LRU)
        for j in range(n_slab):
            hs[j, sl, :] = hsum[:, j * LANES:(j + 1) * LANES]

    for k in range(SUBLANES):
        rows = slice(k * seg, (k + 1) * seg)
        h_seg = jnp.concatenate([hs[j, pl.ds(k, seg, stride=SUBLANES), :] for j in range(n_slab)], axis=1)
        ya_ref[0, rows, :] = (gg_ref[0, rows, :].astype(F32) * h_seg).astype(BF16)


def _lru_call(xb8, gg, conv_w, conv_b, wg, ba, bx, lam, h0, n_seq, T):
    gg3 = gg.reshape(n_seq, T, D_LRU)
    n_slab = D_LRU // LANES
    full = lambda shape: pl.BlockSpec(shape, lambda i: (0,) * len(shape))
    seq = lambda shape: pl.BlockSpec(shape, lambda i: (i,) + (0,) * (len(shape) - 1))
    ya, fin = pl.pallas_call(
        functools.partial(_lru_body, T=T),
        grid=(n_seq,),
        in_specs=[pl.BlockSpec((n_slab, T, LANES), lambda i: (0, i, 0)), seq((1, T, D_LRU)),
                  full((CONV_W, D_LRU)), full((1, D_LRU)),
                  full((4, 256, 1024)), full((2, D_LRU)), full((2, D_LRU)), full((2, D_LRU)),
                  seq((1, 2, D_LRU))],
        out_specs=[seq((1, T, D_LRU)), seq((1, 2, D_LRU))],
        out_shape=[jax.ShapeDtypeStruct((n_seq, T, D_LRU), BF16),
                   jax.ShapeDtypeStruct((n_seq, 2, D_LRU), F32)],
        scratch_shapes=[pltpu.VMEM((T + 3 * SUBLANES, D_LRU), F32)] + [pltpu.VMEM((T, D_LRU), F32)] * 4
                       + [pltpu.VMEM((n_slab, T, LANES), F32)],
        compiler_params=_cp(("parallel",)), name="lru",
    )(xb8, gg3, conv_w, conv_b, wg, ba, bx, lam, h0)
    return ya.reshape(n_seq * T, D_LRU), fin


def _s5_body(u_ref, wts_ref, wc_ref, ap_ref, h0_ref, y_ref, fin_ref, s_scr, hp_scr, slab,
             *, rt, n_c, with_h0, col_major):
    q = 512
    m = S5_SEG
    grp = SUBLANES * m
    n_grp = rt // grp
    seg_per_seq = n_c // m
    seq_per_grp = SUBLANES // seg_per_seq
    seq_tok = n_c * S5_L
    halves = seq_tok // (GRID_W * S5_L)
    sub = lax.broadcasted_iota(I32, (SUBLANES, q), 0)

    def pieces(g, i):
        out = []
        for v in range(SUBLANES):
            s, sq = divmod(v, seg_per_seq)
            tok0 = (g * seq_per_grp + s) * seq_tok
            if col_major:
                for h in range(halves):
                    n = m // halves
                    out.append((pl.ds(g * grp + h * SUBLANES + v, n, stride=halves * SUBLANES),
                                pl.ds(tok0 + (h * S5_L + i) * GRID_W + sq * n, n)))
            else:
                out.append((pl.ds(g * grp + v, m, stride=SUBLANES),
                            pl.ds(tok0 + sq * m * S5_L + i, m, stride=S5_L)))
        return out

    for i in range(S5_L):
        for g in range(n_grp):
            for rows, toks in pieces(g, i):
                slab[i, rows, :] = u_ref[0, toks, :]
    ub = jnp.concatenate([slab[i] for i in range(S5_L)], axis=1).astype(BF16)
    r1 = jnp.dot(ub, wts_ref[0], preferred_element_type=F32)
    y_in = r1[:, :S5_L * LANES]
    s_scr[...] = r1[:, S5_L * LANES:]

    def cmul(ar, ai, xr, xi):
        return ar * xr - ai * xi, ar * xi + ai * xr

    def local_scan(base, col0, reverse):
        ar, ai = ap_ref[0, 0:1, col0:col0 + q], ap_ref[0, 0:1, col0 + q:col0 + 2 * q]

        def step(k, carry):
            hr, hi = carry
            r0 = pl.multiple_of(base + (m - 1 - k if reverse else k) * SUBLANES, SUBLANES)
            pr, pi = cmul(ar, ai, hr, hi)
            hr = pr + s_scr[pl.ds(r0, SUBLANES), col0:col0 + q]
            hi = pi + s_scr[pl.ds(r0, SUBLANES), col0 + q:col0 + 2 * q]
            s_scr[pl.ds(r0, SUBLANES), col0:col0 + q] = hr
            s_scr[pl.ds(r0, SUBLANES), col0 + q:col0 + 2 * q] = hi
            return hr, hi
        zero = jnp.zeros((SUBLANES, q), F32)
        return lax.fori_loop(0, m, step, (zero, zero), unroll=2)

    def chain(g, col0, reverse, h_end):
        base = g * grp
        first = (sub & (seg_per_seq - 1)) == (seg_per_seq - 1 if reverse else 0)
        h0r = jnp.zeros((SUBLANES, q), F32)
        h0i = jnp.zeros((SUBLANES, q), F32)
        if with_h0:
            for s in range(seq_per_grp):
                row = h0_ref[0, 0, g * seq_per_grp + s:g * seq_per_grp + s + 1, :]
                mine = (sub >> (seg_per_seq.bit_length() - 1)) == s
                h0r = jnp.where(mine, row[:, col0:col0 + q], h0r)
                h0i = jnp.where(mine, row[:, col0 + q:col0 + 2 * q], h0i)
        cr, ci = h0r, h0i
        if seg_per_seq > 1:
            er, ei = ap_ref[0, m - 1:m, col0:col0 + q], ap_ref[0, m - 1:m, col0 + q:col0 + 2 * q]
            shift = SUBLANES - 1 if reverse else 1
            for _ in range(seg_per_seq - 1):
                pr, pi = cmul(er, ei, cr, ci)
                cr = jnp.where(first, h0r, pltpu.roll(h_end[0] + pr, shift, 0))
                ci = jnp.where(first, h0i, pltpu.roll(h_end[1] + pi, shift, 0))
        if with_h0 or seg_per_seq > 1:
            def add(c, _):
                r0 = pl.multiple_of(base + c * SUBLANES, SUBLANES)
                k = m - 1 - c if reverse else c
                pr, pi = cmul(ap_ref[0, pl.ds(k, 1), col0:col0 + q],
                              ap_ref[0, pl.ds(k, 1), col0 + q:col0 + 2 * q], cr, ci)
                s_scr[pl.ds(r0, SUBLANES), col0:col0 + q] += pr
                s_scr[pl.ds(r0, SUBLANES), col0 + q:col0 + 2 * q] += pi
                return 0
            lax.fori_loop(0, m, add, 0, unroll=2)
        return cr, ci

    for g in range(n_grp):
        base = g * grp
        body = slice(base + SUBLANES, base + grp)
        head = slice(base, base + SUBLANES)
        prev = slice(base, base + grp - SUBLANES)
        tail = slice(base + grp - SUBLANES, base + grp)
        cr, ci = chain(g, 0, False, local_scan(base, 0, False))
        hp_scr[body, 0:2 * q] = s_scr[prev, 0:2 * q]
        hp_scr[head, 0:q] = cr
        hp_scr[head, q:2 * q] = ci
        cr, ci = chain(g, 2 * q, True, local_scan(base, 2 * q, True))
        hp_scr[prev, 2 * q:4 * q] = s_scr[body, 2 * q:4 * q]
        hp_scr[tail, 2 * q:3 * q] = cr
        hp_scr[tail, 3 * q:4 * q] = ci
        if seg_per_seq == 1:
            rows = slice(g * SUBLANES, (g + 1) * SUBLANES)
            fin_ref[0, 0, rows, 0:2 * q] = s_scr[tail, 0:2 * q]
            fin_ref[0, 0, rows, 2 * q:4 * q] = s_scr[head, 2 * q:4 * q]
    if seg_per_seq > 1:
        fin_ref[...] = jnp.zeros(fin_ref.shape, F32)

    y = y_in + jnp.dot(hp_scr[...].astype(BF16), wc_ref[0], preferred_element_type=F32)
    for i in range(S5_L):
        slab[i] = y[:, i * LANES:(i + 1) * LANES]
        for g in range(n_grp):
            for rows, toks in pieces(g, i):
                y_ref[0, toks, :] = slab[i, rows, :]


def _s5_call(us4, wts, wc, apow, h0, n_seq, T, with_h0, col_major):
    n_c = T // S5_L
    rows = n_seq * n_c
    rt = 512
    nseq_t = rt // n_c
    h0 = h0.reshape(S5_BLK, n_seq // nseq_t, nseq_t, S5_SW)
    y4, fin = pl.pallas_call(
        functools.partial(_s5_body, rt=rt, n_c=n_c, with_h0=with_h0, col_major=col_major),
        grid=(S5_BLK, rows // rt),
        in_specs=[pl.BlockSpec((1, rt * S5_L, LANES), lambda b, r: (b, r, 0)),
                  pl.BlockSpec((1, S5_L * LANES, S5_L * LANES + S5_SW), lambda b, r: (b, 0, 0)),
                  pl.BlockSpec((1, S5_SW, S5_L * LANES), lambda b, r: (b, 0, 0)),
                  pl.BlockSpec((1, S5_SEG, S5_SW), lambda b, r: (b, 0, 0)),
                  pl.BlockSpec((1, 1, nseq_t, S5_SW), lambda b, r: (b, r, 0, 0))],
        out_specs=[pl.BlockSpec((1, rt * S5_L, LANES), lambda b, r: (b, r, 0)),
                   pl.BlockSpec((1, 1, nseq_t, S5_SW), lambda b, r: (b, r, 0, 0))],
        out_shape=[jax.ShapeDtypeStruct((S5_BLK, n_seq * T, LANES), F32),
                   jax.ShapeDtypeStruct((S5_BLK, n_seq // nseq_t, nseq_t, S5_SW), F32)],
        scratch_shapes=[pltpu.VMEM((rt, S5_SW), F32), pltpu.VMEM((rt, S5_SW), F32),
                        pltpu.VMEM((S5_L, rt, LANES), F32)],
        compiler_params=_cp(("parallel", "parallel")), name="s5",
    )(us4, wts, wc, apow, h0)
    return y4, fin.reshape(S5_BLK, n_seq, S5_SW)


def _s5_disc(lr, li, st):
    mag = jnp.exp(lr * st)
    ang = li * st
    ar, ai = mag * jnp.cos(ang), mag * jnp.sin(ang)
    den = lr * lr + li * li
    fr = ((ar - 1.0) * lr + ai * li) / den
    fi = (ai * lr - (ar - 1.0) * li) / den
    return ar, ai, fr, fi


def _s5_powers(ar, ai, n):
    out = [(jnp.ones_like(ar), jnp.zeros_like(ar))]
    for _ in range(n):
        pr, pi = out[-1]
        out.append((pr * ar - pi * ai, pr * ai + pi * ar))
    return out


def _s5_prep_body(rowp_ref, colp_ref, flatp_ref, bt_ref, cm_ref, wts_ref, wc_ref, ap_ref):
    L, N, P = S5_L, S5_N, S5_P
    w = L * LANES
    div = lambda x, d: x >> (d.bit_length() - 1)
    mod = lambda x, d: x & (d - 1)
    one = lambda m: jnp.where(m, 1.0, 0.0).astype(BF16)
    r1 = lax.broadcasted_iota(I32, (LANES, LANES), 0)
    c1 = lax.broadcasted_iota(I32, (LANES, LANES), 1)
    first_half = c1 < N
    sgn = jnp.where(first_half, -1.0, 1.0)

    e_maps, taps = [], []
    same_group = div(r1, P) == div(c1, P)
    fold = one(mod(r1, P) == mod(c1, P))
    for d in range(2):
        ar, ai, fr, fi = _s5_disc(rowp_ref[0, d, 0], rowp_ref[1, d, 0], rowp_ref[2, d, 0])
        bx1 = bt_ref[d, 0]
        bx2 = pltpu.roll(bx1, N, 1)
        x1 = fr * bx1 + (fi * sgn) * bx2
        x2 = pltpu.roll(x1, N, 1)
        e_d, t_d = [], []
        for pr, pi in _s5_powers(ar, ai, L):
            e_k = pr * x1 + (pi * sgn) * x2
            e_d.append(e_k)
            full = _dot3(e_k, cm_ref[d, 0])
            kept = jnp.where(same_group, full, 0.0)
            hi = kept.astype(BF16)
            mid = (kept - hi.astype(F32)).astype(BF16)
            lo = (kept - hi.astype(F32) - mid.astype(F32)).astype(BF16)
            t_d.append(sum(jnp.dot(part, fold, preferred_element_type=F32) for part in (hi, mid, lo)))
        e_maps.append(e_d)
        taps.append(t_d)

    r = lax.broadcasted_iota(I32, (LANES, w), 0)
    c = lax.broadcasted_iota(I32, (LANES, w), 1)
    t_tap = one((div(r, P) == div(c, LANES)) & (mod(r, P) == mod(c, P)))
    m_tap = div(r, P) == div(mod(c, LANES), P)
    t_st = one((div(r, N) == div(c, 8 * N)) & (mod(r, N) == mod(c, N)))
    m_st = div(r, P) == div(mod(c, 8 * N), N)
    col_blk = div(c1, P)
    for j in range(L):
        rows = slice(j * LANES, (j + 1) * LANES)
        toe = jnp.zeros((LANES, LANES), F32)
        for k in range(L):
            toe = (toe + jnp.where(col_blk == j + k, taps[0][k], 0.0)
                   + jnp.where(col_blk == j - k, taps[1][k], 0.0))
        tap = jnp.dot(toe.astype(BF16), t_tap, preferred_element_type=F32)
        wts_ref[0, rows, 0:w] = jnp.where(m_tap, tap, 0.0).astype(BF16)
        for k, e_j in enumerate((e_maps[0][L - 1 - j], e_maps[1][j])):
            st = jnp.dot(e_j.astype(BF16), t_st, preferred_element_type=F32)
            wts_ref[0, rows, w + k * w:w + (k + 1) * w] = jnp.where(m_st, st, 0.0).astype(BF16)

    rr = lax.broadcasted_iota(I32, (w, LANES), 0)
    cc = lax.broadcasted_iota(I32, (w, LANES), 1)
    t_row = one((div(rr, 8 * N) == div(cc, N)) & (mod(rr, N) == mod(cc, N)))
    r2 = lax.broadcasted_iota(I32, (w, w), 0)
    c2 = lax.broadcasted_iota(I32, (w, w), 1)
    m_row = div(mod(r2, 8 * N), N) == div(mod(c2, LANES), P)
    for d in range(2):
        ar, ai, _, _ = _s5_disc(colp_ref[0, d, 0], colp_ref[1, d, 0], colp_ref[2, d, 0])
        a1 = cm_ref[d, 0]
        swapped = pltpu.roll(a1, N, 0)
        a2 = jnp.where(r1 < N, swapped, -swapped)
        pw = _s5_powers(ar, ai, L)
        ks = [i + 1 for i in range(L)] if d == 0 else [L - i for i in range(L)]
        g = jnp.concatenate([a1 * pw[k][0] + a2 * pw[k][1] for k in ks], axis=1)
        corr = jnp.dot(t_row, g.astype(BF16), preferred_element_type=F32)
        wc_ref[0, d * w:(d + 1) * w, :] = jnp.where(m_row, corr, 0.0).astype(BF16)

    al = []
    for d in range(2):
        ar, ai, _, _ = _s5_disc(flatp_ref[0, d, 0], flatp_ref[1, d, 0], flatp_ref[2, d, 0])
        al.append(_s5_powers(ar, ai, L)[L])
    cur = al
    for k in range(S5_SEG):
        ap_ref[0, k:k + 1, :] = jnp.concatenate([cur[0][0], cur[0][1], cur[1][0], cur[1][1]], axis=1)
        cur = [(cr * br - ci * bi, cr * bi + ci * br) for (cr, ci), (br, bi) in zip(cur, al)]


def _s5_weights(lam_re, lam_im, log_step, b_re, b_im, c_re, c_im):
    L, G, N, P = S5_L, S5_G, S5_N, S5_P
    w = L * LANES
    step = jnp.broadcast_to(jnp.exp(log_step)[..., None], lam_re.shape)
    prm = jnp.stack([lam_re, lam_im, step])
    rowp = jnp.broadcast_to(prm[:, :, :, None, None, :], (3, 2, G, P, 2, N)).reshape(3, 2, S5_BLK, LANES, 2 * N)
    colp = jnp.broadcast_to(prm.transpose(0, 1, 3, 2)[:, :, None, :, :, None], (3, 2, 2, N, G, P))
    colp = colp.reshape(3, 2, 2 * N, S5_BLK, LANES).transpose(0, 1, 3, 2, 4)
    flatp = prm.reshape(3, 2, S5_BLK, 1, 8 * N)
    bt = jnp.stack([b_re, b_im], axis=3).transpose(0, 1, 4, 3, 2)
    bt = bt.reshape(2, S5_BLK, LANES, 2 * N)
    cm = jnp.stack([c_re, -c_im], axis=1).transpose(0, 1, 4, 2, 3)
    cm = cm.reshape(2, 2 * N, S5_BLK, LANES).transpose(0, 2, 1, 3)
    blk = lambda lead, a, b_: pl.BlockSpec(lead + (1, a, b_), lambda i: (0,) * len(lead) + (i, 0, 0))
    out = lambda a, b_: pl.BlockSpec((1, a, b_), lambda i: (i, 0, 0))
    return pl.pallas_call(
        _s5_prep_body, grid=(S5_BLK,),
        in_specs=[blk((3, 2), LANES, LANES), blk((3, 2), LANES, LANES), blk((3, 2), 1, 8 * N),
                  blk((2,), LANES, LANES), blk((2,), LANES, LANES)],
        out_specs=[out(w, w + S5_SW), out(S5_SW, w), out(S5_SEG, S5_SW)],
        out_shape=[jax.ShapeDtypeStruct((S5_BLK, w, w + S5_SW), BF16),
                   jax.ShapeDtypeStruct((S5_BLK, S5_SW, w), BF16),
                   jax.ShapeDtypeStruct((S5_BLK, S5_SEG, S5_SW), F32)],
        compiler_params=_cp(("parallel",)), name="s5_prep",
    )(rowp, colp, flatp, bt, cm)


def _mixout_body(xp_ref, xs_ref, mod_ref, g1_ref, g2_ref, wbg_ref, bbg_ref, yap_ref, yas_ref,
                 y4p_ref, y4s_ref, u4p_ref, u4s_ref, *rest, tiles_p):
    i = pl.program_id(0)

    @pl.when(i < tiles_p)
    def _():
        _mixout_tile(xp_ref, mod_ref, g1_ref, g2_ref, wbg_ref, bbg_ref, yap_ref, y4p_ref, u4p_ref, *rest)

    @pl.when(i >= tiles_p)
    def _():
        _mixout_tile(xs_ref, mod_ref, g1_ref, g2_ref, wbg_ref, bbg_ref, yas_ref, y4s_ref, u4s_ref, *rest)


def _mixout_tile(x_ref, mod_ref, g1_ref, g2_ref, wbg_ref, bbg_ref, ya_ref, y4_ref, u4_ref,
                 dsk_ref, wglu_ref, bglu_ref, wso_ref, wlo_ref, wo_ref, wrt_ref, brt_ref,
                 x1_ref, h2_ref, aff_ref):
    m = mod_ref[0]
    sh1, sc1, gt1 = m[:, 0:D_MODEL], m[:, D_MODEL:2 * D_MODEL], m[:, 2 * D_MODEL:3 * D_MODEL]
    sh2, sc2 = m[:, 3 * D_MODEL:4 * D_MODEL], m[:, 4 * D_MODEL:5 * D_MODEL]
    x = x_ref[...]
    h = _norm_mod(x, g1_ref[...], sc1, sh1)
    gates = _sigmoid(jnp.dot(h.astype(BF16), wbg_ref[...], preferred_element_type=F32) + bbg_ref[...])

    def assemble(ref):
        return jnp.concatenate([ref[k] for k in range(S5_BLK)], axis=1)

    ys = assemble(y4_ref) + dsk_ref[...] * assemble(u4_ref)
    v = _gelu(ys)
    ob = v * _sigmoid(jnp.dot(v.astype(BF16), wglu_ref[...], preferred_element_type=F32) + bglu_ref[...])
    merged = (gates[:, :D_MODEL] * jnp.dot(ya_ref[...], wlo_ref[...], preferred_element_type=F32)
              + gates[:, D_MODEL:] * jnp.dot(ob.astype(BF16), wso_ref[...], preferred_element_type=F32))
    mix = jnp.dot(merged.astype(BF16), wo_ref[...], preferred_element_type=F32)
    x1 = x + gt1 * mix
    x1_ref[...] = x1
    h2 = _norm_mod(x1, g2_ref[...], sc2, sh2)
    h2_ref[...] = h2
    logits = _dot3(wrt_ref[...], h2, dims=(((1,), (1,)), ((), ()))) + brt_ref[...]
    mx = jnp.max(logits, axis=0, keepdims=True)
    ex = jnp.exp(logits - mx)
    aff_ref[...] = ex / jnp.sum(ex, axis=0, keepdims=True)


def _mixout_call(xp, xs, mod3, g1, g2, wbg, bbg, ya_p, ya_s, y4_p, y4_s, u4_p, u4_s,
                 dsk, wglu, bglu, wso, wlo, wo, wrt, brt, mod_row, tm):
    n_p, n_s = xp.shape[0], xs.shape[0]
    tiles_p = n_p // tm
    n = n_p + n_s
    full = lambda shape: pl.BlockSpec(shape, lambda i: (0,) * len(shape))
    at_p = lambda i: jnp.minimum(i, tiles_p - 1)
    at_s = lambda i: jnp.maximum(i - tiles_p, 0)
    tok = lambda w, at: pl.BlockSpec((tm, w), lambda i: (at(i), 0))
    s5 = lambda at: pl.BlockSpec((S5_BLK, tm, LANES), lambda i: (0, at(i), 0))
    out = lambda w: pl.BlockSpec((tm, w), lambda i: (i, 0))
    return pl.pallas_call(
        functools.partial(_mixout_body, tiles_p=tiles_p),
        grid=(n // tm,),
        in_specs=[tok(D_MODEL, at_p), tok(D_MODEL, at_s),
                  pl.BlockSpec((1, 1, 6 * D_MODEL), lambda i: (mod_row(i), 0, 0)),
                  full((1, D_MODEL)), full((1, D_MODEL)), full((D_MODEL, 2 * D_MODEL)),
                  full((1, 2 * D_MODEL)), tok(D_LRU, at_p), tok(D_LRU, at_s),
                  s5(at_p), s5(at_s), s5(at_p), s5(at_s), full((1, D_S5)),
                  full((D_S5, D_S5)), full((1, D_S5)), full((D_S5, D_MODEL)), full((D_LRU, D_MODEL)),
                  full((D_MODEL, D_MODEL)), full((N_EXPERTS, D_MODEL)), full((N_EXPERTS, 1))],
        out_specs=[out(D_MODEL), out(D_MODEL), pl.BlockSpec((N_EXPERTS, tm), lambda i: (0, i))],
        out_shape=[jax.ShapeDtypeStruct((n, D_MODEL), F32), jax.ShapeDtypeStruct((n, D_MODEL), F32),
                   jax.ShapeDtypeStruct((N_EXPERTS, n), F32)],
        compiler_params=_cp(("arbitrary",)), name="mixout",
    )(xp, xs, mod3, g1, g2, wbg, bbg, ya_p, ya_s, y4_p, y4_s, u4_p, u4_s,
      dsk, wglu, bglu, wso, wlo, wo, wrt, brt)


def _select_body(aff_ref, gate_ref, pos_ref, offs_ref, *, n_tok, cap):
    aff = aff_ref[...]
    capf = float(cap)

    def bis(_, lh):
        lo, hi = lh
        mid = lo + ((hi - lo + 1) >> 1)
        cnt = jnp.sum(jnp.where(aff >= pltpu.bitcast(mid, F32), 1.0, 0.0), axis=1, keepdims=True)
        ok = cnt >= capf
        return jnp.where(ok, mid, lo), jnp.where(ok, hi, mid - 1)

    lo0 = jnp.zeros((N_EXPERTS, 1), I32)
    hi0 = jnp.full((N_EXPERTS, 1), 0x7F800000, I32)
    thr_bits, _ = lax.fori_loop(0, 31, bis, (lo0, hi0))
    thr = pltpu.bitcast(thr_bits, F32)
    need = capf - jnp.sum(jnp.where(aff > thr, 1.0, 0.0), axis=1, keepdims=True)

    r = lax.broadcasted_iota(I32, (TOK_TILE, TOK_TILE), 0)
    c = lax.broadcasted_iota(I32, (TOK_TILE, TOK_TILE), 1)
    tri = jnp.where(r < c, 1.0, 0.0).astype(BF16)
    lane = lax.broadcasted_iota(I32, (N_EXPERTS, LANES), 1)
    n_tiles = n_tok // TOK_TILE
    run_eq = jnp.zeros((N_EXPERTS, 1), F32)
    run_sel = jnp.zeros((N_EXPERTS, 1), F32)
    offs = jnp.zeros((N_EXPERTS, LANES), F32)
    for t in range(n_tiles):
        sl = slice(t * TOK_TILE, (t + 1) * TOK_TILE)
        aff_t = aff_ref[:, sl]
        is_eq = aff_t == thr
        eq_t = jnp.where(is_eq, 1.0, 0.0)
        rank_eq = jnp.dot(eq_t.astype(BF16), tri, preferred_element_type=F32) + run_eq
        cand = (aff_t > thr) | (is_eq & (rank_eq < need))
        pos_t = jnp.dot(jnp.where(cand, 1.0, 0.0).astype(BF16), tri, preferred_element_type=F32) + run_sel
        sel_t = cand & (pos_t < capf)
        sel_f = jnp.where(sel_t, 1.0, 0.0)
        offs = jnp.where(lane == t, run_sel, offs)
        gate_ref[:, sl] = jnp.where(sel_t, aff_t, 0.0)
        pos_ref[:, sl] = jnp.where(sel_t, pos_t, -1.0).astype(I32)
        run_eq = run_eq + jnp.sum(eq_t, axis=1, keepdims=True)
        run_sel = run_sel + jnp.sum(sel_f, axis=1, keepdims=True)
    offs = jnp.where(lane >= n_tiles, run_sel, offs)
    offs_ref[...] = offs.astype(I32)


def _select_call(aff_t, cap, n_tok):
    n_path = aff_t.shape[1] // n_tok
    per_p = lambda a, b: pl.BlockSpec((None, a, b), lambda p: (p, 0, 0))
    return pl.pallas_call(
        functools.partial(_select_body, n_tok=n_tok, cap=cap),
        grid=(n_path,),
        in_specs=[pl.BlockSpec((N_EXPERTS, n_tok), lambda p: (0, p))],
        out_specs=[per_p(N_EXPERTS, n_tok), per_p(N_EXPERTS, n_tok), per_p(N_EXPERTS, LANES)],
        out_shape=[jax.ShapeDtypeStruct((n_path, N_EXPERTS, n_tok), F32),
                   jax.ShapeDtypeStruct((n_path, N_EXPERTS, n_tok), I32),
                   jax.ShapeDtypeStruct((n_path, N_EXPERTS, LANES), I32)],
        compiler_params=_cp(("parallel",)), name="select",
    )(aff_t)


def _invert_body(offs_ref, pos_ref, gate_ref, idx_ref, gcol_ref, acc_i, acc_g, *, n_tiles, cap):
    path = pl.program_id(0)
    base = (path * N_EXPERTS + pl.program_id(1)) * LANES
    acc_i[...] = jnp.zeros(acc_i.shape, F32)
    acc_g[...] = jnp.zeros(acc_g.shape, F32)
    jrow = lax.broadcasted_iota(I32, (SLOT_CHUNK, LANES), 0)
    lane = lax.broadcasted_iota(I32, (1, LANES), 1)

    def per_tile(t, _):
        first = (offs_ref[base + t] >> 3) << 3
        prow = pos_ref[0, 0, pl.ds(t, 1), :]
        grow = gate_ref[0, 0, pl.ds(t, 1), :]
        tok = (lane + (t * TOK_TILE + 1)).astype(F32)

        def window(w, _):
            start = pl.multiple_of(first + w * SLOT_CHUNK, SUBLANES)
            rows = pl.ds(start, SLOT_CHUNK)
            hit = prow == (jrow + start)
            acc_i[rows, :] = jnp.maximum(acc_i[rows, :], jnp.where(hit, tok, 0.0))
            acc_g[rows, :] = jnp.maximum(acc_g[rows, :], jnp.where(hit, grow, 0.0))
            return 0

        n_win = (offs_ref[base + t + 1] - first + SLOT_CHUNK - 1) >> (SLOT_CHUNK.bit_length() - 1)
        lax.fori_loop(0, n_win, window, 0)
        return 0

    lax.fori_loop(0, n_tiles, per_tile, 0)
    hit_i = acc_i[0:cap, :]
    hi = hit_i.astype(BF16)
    mid = (hit_i - hi.astype(F32)).astype(BF16)
    lo = (hit_i - hi.astype(F32) - mid.astype(F32)).astype(BF16)
    ones = jnp.ones((SUBLANES, LANES), BF16)
    nt = (((1,), (1,)), ((), ()))
    tok1 = sum(lax.dot_general(ones, part, nt, preferred_element_type=F32) for part in (hi, mid, lo))
    idx_ref[0, 0] = tok1[0:1, :].astype(I32) - 1 + path * (n_tiles * TOK_TILE)
    gcol_ref[0, 0] = jnp.max(acc_g[0:cap, :], axis=1, keepdims=True)


def _invert_call(offs, pos, gate, cap):
    n_path, _, n_tok = pos.shape
    n_tiles = n_tok // TOK_TILE
    pos4 = pos.reshape(n_path, N_EXPERTS, n_tiles, TOK_TILE)
    gate4 = gate.reshape(n_path, N_EXPERTS, n_tiles, TOK_TILE)
    per_e = lambda a, b: pl.BlockSpec((1, 1, a, b), lambda p, e, o: (p, e, 0, 0))
    return pl.pallas_call(
        functools.partial(_invert_body, n_tiles=n_tiles, cap=cap),
        grid_spec=pltpu.PrefetchScalarGridSpec(
            num_scalar_prefetch=1, grid=(n_path, N_EXPERTS),
            in_specs=[per_e(n_tiles, TOK_TILE), per_e(n_tiles, TOK_TILE)],
            out_specs=[per_e(1, cap), per_e(cap, 1)],
            scratch_shapes=[pltpu.VMEM((cap + SLOT_CHUNK, LANES), F32)] * 2),
        out_shape=[jax.ShapeDtypeStruct((n_path, N_EXPERTS, 1, cap), I32),
                   jax.ShapeDtypeStruct((n_path, N_EXPERTS, cap, 1), F32)],
        compiler_params=_cp(("parallel", "parallel")), name="invert",
    )(offs.reshape(-1), pos4, gate4)


def _ffn_body(idx0_ref, idxn_ref, gcol_ref, wg_ref, wu_ref, wd_ref, h_hbm, out_ref,
              xe, wgb, wub, wdb, sem, *, cap):
    e = pl.program_id(0)
    p = pl.program_id(1)
    nxt = 1 - p
    step = e * 2 + p

    def issue_rows(idx_ref, s, j0, n):
        for j in range(j0, j0 + n):
            pltpu.make_async_copy(h_hbm.at[pl.ds(idx_ref[0, 0, 0, j], 1), :],
                                  xe.at[s, pl.ds(j, 1), :], sem.at[s]).start()

    @pl.when(step == 0)
    def _():
        def issue(j, _):
            pltpu.make_async_copy(h_hbm.at[pl.ds(idx0_ref[0, 0, 0, j], 1), :],
                                  xe.at[0, pl.ds(j, 1), :], sem.at[0]).start()
            return 0
        lax.fori_loop(0, cap, issue, 0, unroll=8)

    @pl.when(p == 0)
    def _():
        wgb[...] = wg_ref[0].astype(BF16)
        wub[...] = wu_ref[0].astype(BF16)
        wdb[...] = wd_ref[0].astype(BF16)

    pltpu.make_async_copy(h_hbm.at[pl.ds(0, cap), :], xe.at[p], sem.at[p]).wait()

    ch = 256
    n_ch = cap // ch
    per = -(-cap // (n_ch - 1))
    for c in range(n_ch):
        sl = slice(c * ch, (c + 1) * ch)
        xb = xe[p, sl, :].astype(BF16)
        issue_rows(idxn_ref, nxt, min(c * per, cap), min((c + 1) * per, cap) - min(c * per, cap))
        g = jnp.dot(xb, wgb[...], preferred_element_type=F32)
        u = jnp.dot(xb, wub[...], preferred_element_type=F32)
        hid = (g * _sigmoid(g)) * u
        ye = jnp.dot(hid.astype(BF16), wdb[...], preferred_element_type=F32)
        out_ref[0, 0, sl, :] = ye * gcol_ref[0, 0, sl, :]

    @pl.when(step == 2 * N_EXPERTS - 1)
    def _():
        pltpu.make_async_copy(h_hbm.at[pl.ds(0, cap), :], xe.at[nxt], sem.at[nxt]).wait()


def _ffn_call(idx, gcol, w_gate, w_up, w_down, h2, cap):
    d_e = w_gate.shape[2]
    wspec = lambda a, b: pl.BlockSpec((1, a, b), lambda e, p: (e, 0, 0))
    return pl.pallas_call(
        functools.partial(_ffn_body, cap=cap),
        grid=(N_EXPERTS, 2),
        in_specs=[pl.BlockSpec((1, 1, 1, cap), lambda e, p: (0, 0, 0, 0), memory_space=pltpu.SMEM),
                  pl.BlockSpec((1, 1, 1, cap),
                               lambda e, p: (1 - p, jnp.minimum(e + p, N_EXPERTS - 1), 0, 0),
                               memory_space=pltpu.SMEM),
                  pl.BlockSpec((1, 1, cap, 1), lambda e, p: (p, e, 0, 0)),
                  wspec(D_MODEL, d_e), wspec(D_MODEL, d_e), wspec(d_e, D_MODEL),
                  pl.BlockSpec(memory_space=pl.ANY)],
        out_specs=pl.BlockSpec((1, 1, cap, D_MODEL), lambda e, p: (p, e, 0, 0)),
        out_shape=jax.ShapeDtypeStruct((2, N_EXPERTS, cap, D_MODEL), F32),
        scratch_shapes=[pltpu.VMEM((2, cap, D_MODEL), F32), pltpu.VMEM((D_MODEL, d_e), BF16),
                        pltpu.VMEM((D_MODEL, d_e), BF16), pltpu.VMEM((d_e, D_MODEL), BF16),
                        pltpu.SemaphoreType.DMA((2,))],
        compiler_params=_cp(("arbitrary", "arbitrary")), name="ffn",
    )(idx, idx, gcol, w_gate, w_up, w_down, h2)


def _combine_body(offs_ref, x1_ref, mod_ref, gf_ref, pos_ref, yb_hbm, y_ref, wins, acc_ref, sem,
                  *, cap, path):
    t = pl.program_id(0)
    n_t = pl.num_programs(0)
    slot = t % 2
    row0 = path * N_EXPERTS * LANES

    def kmax_of(tt):
        k = 0
        for e in range(N_EXPERTS):
            k = jnp.maximum(k, offs_ref[row0 + e * LANES + tt + 1] - offs_ref[row0 + e * LANES + tt])
        return k

    def geometry(tt, e, w, win):
        fetch = win + SUBLANES
        lo = offs_ref[row0 + e * LANES + tt] + w * win
        start = pl.multiple_of(jnp.minimum((lo >> 3) << 3, cap - fetch), SUBLANES)
        return lo, start, fetch

    def issue(tt, w, s, win):
        for e in range(N_EXPERTS):
            _, start, fetch = geometry(tt, e, w, win)
            pltpu.make_async_copy(yb_hbm.at[path, e, pl.ds(start, fetch), :],
                                  wins.at[s, pl.ds(e * fetch, fetch), :], sem.at[s]).start()

    def drain(s, win):
        rows = N_EXPERTS * (win + SUBLANES)
        pltpu.make_async_copy(yb_hbm.at[0, 0, pl.ds(0, rows), :], wins.at[s, pl.ds(0, rows), :],
                              sem.at[s]).wait()

    def expand(w, s, win):
        sub = lax.broadcasted_iota(I32, (win + SUBLANES, TOK_TILE), 0)
        rows = []
        for e in range(N_EXPERTS):
            lo, start, _ = geometry(t, e, w, win)
            pos = pos_ref[e:e + 1, :]
            in_round = jnp.logical_and(pos >= lo, pos < lo + win)
            rows.append(jnp.where(jnp.logical_and(pos - start == sub, in_round), 1.0, 0.0))
        onehot = jnp.concatenate(rows, axis=0).T.astype(BF16)
        data = wins[s, 0:N_EXPERTS * (win + SUBLANES), :]
        hi = data.astype(BF16)
        lo_part = (data - hi.astype(F32)).astype(BF16)
        return (jnp.dot(onehot, hi, preferred_element_type=F32)
                + jnp.dot(onehot, lo_part, preferred_element_type=F32))

    def issue_first(tt, s):
        wide = kmax_of(tt) > COMBINE_WIN

        @pl.when(wide)
        def _():
            issue(tt, 0, s, COMBINE_WIN_WIDE)

        @pl.when(jnp.logical_not(wide))
        def _():
            issue(tt, 0, s, COMBINE_WIN)

    @pl.when(t == 0)
    def _():
        issue_first(0, 0)

    @pl.when(t + 1 < n_t)
    def _():
        issue_first(t + 1, 1 - slot)

    kmax = kmax_of(t)

    @pl.when(kmax > COMBINE_WIN)
    def _():
        drain(slot, COMBINE_WIN_WIDE)
        acc_ref[...] = expand(0, slot, COMBINE_WIN_WIDE)

    @pl.when(kmax <= COMBINE_WIN)
    def _():
        drain(slot, COMBINE_WIN)
        acc_ref[...] = expand(0, slot, COMBINE_WIN)

    def more(w, _):
        issue(t, w, slot, COMBINE_WIN_WIDE)
        drain(slot, COMBINE_WIN_WIDE)
        acc_ref[...] += expand(w, slot, COMBINE_WIN_WIDE)
        return 0
    lax.fori_loop(1, (kmax + COMBINE_WIN_WIDE - 1) >> (COMBINE_WIN_WIDE.bit_length() - 1), more, 0)
    acc = acc_ref[...]

    gt2 = mod_ref[0][:, 5 * D_MODEL:6 * D_MODEL]
    x2 = x1_ref[...] + gt2 * acc
    ms = jnp.mean(x2 * x2, axis=-1, keepdims=True)
    y_ref[...] = x2 * lax.rsqrt(ms + EPS) * gf_ref[...]


def _combine_call(offs, pos, x1, mod3, gf, ybuf, mod_row, cap, path, n):
    first = path * n // TOK_TILE
    return pl.pallas_call(
        functools.partial(_combine_body, cap=cap, path=path),
        grid_spec=pltpu.PrefetchScalarGridSpec(
            num_scalar_prefetch=1, grid=(n // TOK_TILE,),
            in_specs=[pl.BlockSpec((TOK_TILE, D_MODEL), lambda i, o: (first + i, 0)),
                      pl.BlockSpec((1, 1, 6 * D_MODEL), lambda i, o: (mod_row(i), 0, 0)),
                      pl.BlockSpec((1, D_MODEL), lambda i, o: (0, 0)),
                      pl.BlockSpec((None, N_EXPERTS, TOK_TILE), lambda i, o: (path, 0, i)),
                      pl.BlockSpec(memory_space=pl.ANY)],
            out_specs=pl.BlockSpec((TOK_TILE, D_MODEL), lambda i, o: (i, 0)),
            scratch_shapes=[pltpu.VMEM((2, N_EXPERTS * (COMBINE_WIN_WIDE + SUBLANES), D_MODEL), F32),
                            pltpu.VMEM((TOK_TILE, D_MODEL), F32),
                            pltpu.SemaphoreType.DMA((2,))]),
        out_shape=jax.ShapeDtypeStruct((n, D_MODEL), F32),
        compiler_params=_cp(("arbitrary",)), name="combine",
    )(offs.reshape(-1), x1, mod3, gf, pos, ybuf)


def _lru_gate_weights(wa, wx):
    eye = jnp.eye(4, dtype=F32)

    def bd(w):
        w5 = w.reshape(4, 4, LRU_HEAD_DIM, LRU_HEAD_DIM)
        return jnp.einsum('khij,hg->khigj', w5, eye).reshape(4, 256, 256)

    return jnp.concatenate([bd(wa[0]), bd(wa[1]), bd(wx[0]), bd(wx[1])], axis=2).astype(BF16)


def kernel(x_prompt, x_sample, state_lru, state_s5_re, state_s5_im, c, c_ctx, w_mod, b_mod, g_norm1, g_norm2, w_in, conv_w, conv_b, lru_wa, lru_ba, lru_wx, lru_bx, lru_lambda, s5_lambda_re, s5_lambda_im, s5_log_step, s5_b_re, s5_b_im, s5_c_re, s5_c_im, s5_d, s5_w_glu, s5_b_glu, w_lru_out, w_s5_out, w_branch_gate, b_branch_gate, w_o, w_router, b_router, w_e_gate, w_e_up, w_e_down, g_final):
    bp, tp, _ = x_prompt.shape
    bs, ts, _ = x_sample.shape
    n_p, n_s = bp * tp, bs * ts
    l = 0

    c_all = jnp.zeros((16, D_MODEL), F32).at[0].set(c_ctx).at[1:1 + bs].set(c)
    mod3 = _mod_call(c_all, w_mod[l], b_mod[l][None, :]).reshape(16, 1, 6 * D_MODEL)

    w_in_b = w_in[l].astype(BF16)
    w_main, w_s5in = w_in_b[:, :2 * D_LRU], w_in_b[:, 2 * D_LRU:]
    g1 = g_norm1[l][None, :]
    g2 = g_norm2[l][None, :]
    wg = _lru_gate_weights(lru_wa[l], lru_wx[l])
    wts, wc, apow = _s5_weights(s5_lambda_re[l], s5_lambda_im[l], s5_log_step[l],
                                s5_b_re[l], s5_b_im[l], s5_c_re[l], s5_c_im[l])
    wbg = w_branch_gate[l].astype(BF16)
    bbg = b_branch_gate[l][None, :]
    wglu = s5_w_glu[l].astype(BF16)
    wso = w_s5_out[l].astype(BF16)
    wlo = w_lru_out[l].astype(BF16)
    wo = w_o[l].astype(BF16)
    wrt = w_router[l].T
    brt = b_router[l][:, None]

    xp2 = x_prompt.reshape(n_p, D_MODEL)
    xs2 = x_sample.reshape(n_s, D_MODEL)
    ctx_row = lambda i: 0
    tm_in = 512
    lat_row_in = lambda i: 1 + i // (ts // max(tm_in, ts))

    def s5_state(re, im):
        def part(a, d):
            return a[:, d].reshape(-1, S5_BLK, 8 * S5_N).transpose(1, 0, 2)
        return jnp.concatenate([part(re, 0), part(im, 0), part(re, 1), part(im, 1)], axis=2)

    xb_p, gg_p, us4_p = _inproj_call(xp2, mod3, g1, w_main, w_s5in, ctx_row, max(tm_in, tp), tp)
    ya_p, lru_fin = _lru_call(xb_p, gg_p, conv_w[l], conv_b[l][None, :], wg, lru_ba[l], lru_bx[l],
                              lru_lambda[l], jnp.zeros((bp, 2, D_LRU), F32), bp, tp)
    y4_p, s5_fin = _s5_call(us4_p, wts, wc, apow, jnp.zeros((S5_BLK, bp, S5_SW), F32), bp, tp,
                            with_h0=False, col_major=False)
    xb_s, gg_s, us4_s = _inproj_call(xs2, mod3, g1, w_main, w_s5in, lat_row_in, max(tm_in, ts), ts)
    ya_s, _ = _lru_call(xb_s, gg_s, conv_w[l], conv_b[l][None, :], wg, lru_ba[l], lru_bx[l],
                        lru_lambda[l], state_lru[:, l], bs, ts)
    y4_s, _ = _s5_call(us4_s, wts, wc, apow, s5_state(state_s5_re[:, l], state_s5_im[:, l]), bs, ts,
                       with_h0=True, col_major=True)

    tm_mo = 512
    tiles_p = n_p // tm_mo
    x1, h2, aff = _mixout_call(
        xp2, xs2, mod3, g1, g2, wbg, bbg, ya_p, ya_s, y4_p, y4_s, us4_p, us4_s,
        s5_d[l][None, :], wglu, s5_b_glu[l][None, :], wso, wlo, wo, wrt, brt,
        mod_row=lambda i: jnp.where(i < tiles_p, 0, 1 + (i - tiles_p) // (ts // tm_mo)), tm=tm_mo)

    assert n_p == n_s
    cap = (CAPACITY_FACTOR * n_p) // N_EXPERTS
    gate, pos, offs = _select_call(aff, cap, n_p)
    idx, gcol = _invert_call(offs, pos, gate, cap)
    ybuf = _ffn_call(idx, gcol, w_e_gate[l], w_e_up[l], w_e_down[l], h2, cap)
    gf = g_final[None, :]
    y_p = _combine_call(offs, pos, x1, mod3, gf, ybuf, ctx_row, cap, 0, n_p)
    y_s = _combine_call(offs, pos, x1, mod3, gf, ybuf, lambda i: 1 + i // (ts // TOK_TILE), cap, 1, n_s)

    new_lru = lru_fin[:, None]
    sf = s5_fin.transpose(1, 0, 2).reshape(bp, S5_BLK, 2, 2, 8, S5_N)
    sf = sf.transpose(0, 2, 3, 1, 4, 5).reshape(bp, 2, 2, S5_G, S5_N)
    new_s5r = sf[:, :, 0][:, None]
    new_s5i = sf[:, :, 1][:, None]
    return (y_p.reshape(bp, tp, D_MODEL), y_s.reshape(bs, ts, D_MODEL), new_lru, new_s5r, new_s5i)
```

```python
import functools
import math

import jax
import jax.numpy as jnp
from jax import lax
from jax.experimental import pallas as pl
from jax.experimental.pallas import tpu as pltpu

F32 = jnp.float32
BF16 = jnp.bfloat16
I32 = jnp.int32

D_MODEL = 1024
D_LRU = 1024
LRU_HEADS = 16
LRU_HEAD_DIM = 64
LRU_C = 8.0
CONV_W = 4
D_S5 = 512
S5_P = 16
S5_G = 32
S5_N = 64
GRID_W = 64
N_EXPERTS = 16
CAPACITY_FACTOR = 2
EPS = 1e-6

LANES = 128
SUBLANES = 8
S5_L = 8
S5_BLK = 4
S5_SW = 4 * 512
S5_SEG = 32
TOK_TILE = 128
SLOT_CHUNK = 32
COMBINE_WIN = 32
COMBINE_WIN_WIDE = 64
VMEM_LIMIT = 56 * 1024 * 1024


def _cp(sem, vmem=VMEM_LIMIT):
    return pltpu.CompilerParams(dimension_semantics=sem, vmem_limit_bytes=vmem)


def _split2(a):
    hi = a.astype(BF16)
    lo = (a - hi.astype(F32)).astype(BF16)
    return hi, lo


def _dot3(a, b, dims=(((1,), (0,)), ((), ()))):
    ah, al = _split2(a)
    bh, bl = _split2(b)
    d = functools.partial(lax.dot_general, dimension_numbers=dims, preferred_element_type=F32)
    return d(ah, bh) + (d(al, bh) + d(ah, bl))


def _sigmoid(x):
    return 0.5 * jnp.tanh(0.5 * x) + 0.5


def _gelu(x):
    c = math.sqrt(2.0 / math.pi)
    half = 0.5 * x
    return half + half * jnp.tanh(x * (c + (c * 0.044715) * (x * x)))


def _norm_mod(x, g, scale, shift):
    ms = jnp.mean(x * x, axis=-1, keepdims=True)
    return (x * lax.rsqrt(ms + EPS)) * (g * (1.0 + scale)) + shift


def _mod_body(c_ref, w_ref, b_ref, o_ref):
    c = c_ref[...]
    s = c * _sigmoid(c)
    o_ref[...] = _dot3(s, w_ref[...]) + b_ref[...]


def _mod_call(c_all, w_mod, b_mod):
    n = w_mod.shape[1]
    tn = 1536
    return pl.pallas_call(
        _mod_body,
        grid=(n // tn,),
        in_specs=[pl.BlockSpec((16, D_MODEL), lambda j: (0, 0)),
                  pl.BlockSpec((D_MODEL, tn), lambda j: (0, j)),
                  pl.BlockSpec((1, tn), lambda j: (0, j))],
        out_specs=pl.BlockSpec((16, tn), lambda j: (0, j)),
        out_shape=jax.ShapeDtypeStruct((16, n), F32),
        compiler_params=_cp(("arbitrary",)),
        name="mod",
    )(c_all, w_mod, b_mod)


def _inproj_body(x_ref, mod_ref, g_ref, w_ref, ws_ref, xb_ref, gg_ref, us_ref, *, seq_len):
    m = mod_ref[0]
    h = _norm_mod(x_ref[...], g_ref[...], m[:, D_MODEL:2 * D_MODEL], m[:, 0:D_MODEL])
    hb = h.astype(BF16)
    z = jnp.dot(hb, w_ref[...], preferred_element_type=F32)
    seg = seq_len // SUBLANES
    for s in range(x_ref.shape[0] // seq_len):
        for k in range(SUBLANES):
            t0 = s * seq_len + k * seg
            for c in range(D_LRU // LANES):
                xb_ref[c, pl.ds(s * seq_len + k, seg, stride=SUBLANES), :] = (
                    z[t0:t0 + seg, c * LANES:(c + 1) * LANES])
    gg_ref[...] = _gelu(z[:, D_LRU:]).astype(BF16)
    us = jnp.dot(hb, ws_ref[...], preferred_element_type=F32)
    for k in range(S5_BLK):
        us_ref[k] = us[:, k * LANES:(k + 1) * LANES]


def _inproj_call(x2d, mod3, g1, w_main, w_s5, mod_row, tm, seq_len):
    n = x2d.shape[0]
    return pl.pallas_call(
        functools.partial(_inproj_body, seq_len=seq_len), grid=(n // tm,),
        in_specs=[pl.BlockSpec((tm, D_MODEL), lambda i: (i, 0)),
                  pl.BlockSpec((1, 1, 6 * D_MODEL), lambda i: (mod_row(i), 0, 0)),
                  pl.BlockSpec((1, D_MODEL), lambda i: (0, 0)),
                  pl.BlockSpec((D_MODEL, 2 * D_LRU), lambda i: (0, 0)),
                  pl.BlockSpec((D_MODEL, D_S5), lambda i: (0, 0))],
        out_specs=[pl.BlockSpec((D_LRU // LANES, tm, LANES), lambda i: (0, i, 0)),
                   pl.BlockSpec((tm, D_LRU), lambda i: (i, 0)),
                   pl.BlockSpec((S5_BLK, tm, LANES), lambda i: (0, i, 0))],
        out_shape=[jax.ShapeDtypeStruct((D_LRU // LANES, n, LANES), F32),
                   jax.ShapeDtypeStruct((n, D_LRU), BF16),
                   jax.ShapeDtypeStruct((S5_BLK, n, LANES), F32)],
        compiler_params=_cp(("parallel",)), name="inproj",
    )(x2d, mod3, g1, w_main, w_s5)


def _lru_body(xb_ref, gg_ref, cw_ref, cb_ref, wg_ref, ba_ref, bx_ref, lam_ref, h0_ref,
              ya_ref, fin_ref, xpad, a_f, b_f, a_b, b_b, hs, *, T):
    seg = T // SUBLANES
    ch = 256
    nch = T // ch
    n_slab = D_LRU // LANES
    pad = 2 * SUBLANES
    sub = lax.broadcasted_iota(I32, (SUBLANES, D_LRU), 0)

    def rows_of(r0, n):
        return jnp.concatenate([xb_ref[c, r0:r0 + n, :] for c in range(n_slab)], axis=1)

    def from_prev_segment(tile):
        return jnp.where(sub >= 1, pltpu.roll(tile, 1, 0), 0.0)

    def from_next_segment(tile):
        return jnp.where(sub < SUBLANES - 1, pltpu.roll(tile, SUBLANES - 1, 0), 0.0)

    xpad[0:SUBLANES, :] = from_prev_segment(rows_of((seg - 2) * SUBLANES, SUBLANES))
    xpad[SUBLANES:pad, :] = from_prev_segment(rows_of((seg - 1) * SUBLANES, SUBLANES))
    xpad[pad + T:pad + T + SUBLANES, :] = from_next_segment(rows_of(0, SUBLANES))
    for c in range(nch):
        xpad[pad + c * ch:pad + (c + 1) * ch, :] = rows_of(c * ch, ch)

    nl = -lam_ref[...]
    softplus = jnp.maximum(nl, 0.0) + jnp.log1p(jnp.exp(-jnp.abs(nl)))
    decay = (-LRU_C) * softplus
    scr = ((a_f, b_f), (a_b, b_b))

    for c in range(nch):
        t0 = c * ch
        xc = cb_ref[...] + sum(
            xpad[t0 + k * SUBLANES:t0 + k * SUBLANES + ch, :] * cw_ref[k:k + 1, :]
            for k in range(CONV_W))
        xcb = xc.astype(BF16)
        pre = [jnp.dot(xcb[:, kb * 256:(kb + 1) * 256], wg_ref[kb], preferred_element_type=F32)
               for kb in range(4)]
        for d in range(2):
            ra = jnp.concatenate([p[:, d * 256:(d + 1) * 256] for p in pre], axis=1)
            gx = jnp.concatenate([p[:, 512 + d * 256:512 + (d + 1) * 256] for p in pre], axis=1)
            r = _sigmoid(ra + ba_ref[d:d + 1, :])
            gi = _sigmoid(gx + bx_ref[d:d + 1, :])
            a = jnp.exp(r * decay[d:d + 1, :])
            scr[d][0][t0:t0 + ch, :] = a
            scr[d][1][t0:t0 + ch, :] = jnp.exp(0.5 * jnp.log(1.0 - a * a)) * gi * xc

    def local_scan(a_ref, u_ref, reverse):
        def step(i, carry):
            h, p = carry
            r0 = pl.multiple_of((seg - 1 - i if reverse else i) * SUBLANES, SUBLANES)
            a = a_ref[pl.ds(r0, SUBLANES), :]
            h = a * h + u_ref[pl.ds(r0, SUBLANES), :]
            p = a * p
            u_ref[pl.ds(r0, SUBLANES), :] = h
            a_ref[pl.ds(r0, SUBLANES), :] = p
            return h, p
        init = (jnp.zeros((SUBLANES, D_LRU), F32), jnp.ones((SUBLANES, D_LRU), F32))
        return lax.fori_loop(0, seg, step, init, unroll=2)

    def carry_in(h_end, p_end, h0_row, reverse):
        edge = sub == (SUBLANES - 1 if reverse else 0)
        shift = SUBLANES - 1 if reverse else 1
        c = jnp.broadcast_to(h0_row, (SUBLANES, D_LRU))
        for _ in range(SUBLANES - 1):
            c = jnp.where(edge, h0_row, pltpu.roll(h_end + p_end * c, shift, 0))
        return c

    hf_end, pf_end = local_scan(a_f, b_f, False)
    hb_end, pb_end = local_scan(a_b, b_b, True)
    c_f = carry_in(hf_end, pf_end, h0_ref[0, 0:1, :], False)
    c_b = carry_in(hb_end, pb_end, h0_ref[0, 1:2, :], True)
    fin_ref[0, 0:1, :] = (hf_end + pf_end * c_f)[SUBLANES - 1:SUBLANES, :]
    fin_ref[0, 1:2, :] = (hb_end + pb_end * c_b)[0:1, :]

    for c in range(nch):
        sl = slice(c * ch, (c + 1) * ch)
        tile3 = (ch // SUBLANES, SUBLANES, D_LRU)
        hsum = ((b_f[sl, :].reshape(tile3) + a_f[sl, :].reshape(tile3) * c_f[None])
                + (b_b[sl, :].reshape(tile3) + a_b[sl, :].reshape(tile3) * c_b[None])).reshape(ch, D_LRU)
        for j in range(n_slab):
            hs[j, sl, :] = hsum[:, j * LANES:(j + 1) * LANES]

    for k in range(SUBLANES):
        rows = slice(k * seg, (k + 1) * seg)
        h_seg = jnp.concatenate([hs[j, pl.ds(k, seg, stride=SUBLANES), :] for j in range(n_slab)], axis=1)
        ya_ref[0, rows, :] = (gg_ref[0, rows, :].astype(F32) * h_seg).astype(BF16)


def _lru_call(xb8, gg, conv_w, conv_b, wg, ba, bx, lam, h0, n_seq, T):
    gg3 = gg.reshape(n_seq, T, D_LRU)
    n_slab = D_LRU // LANES
    full = lambda shape: pl.BlockSpec(shape, lambda i: (0,) * len(shape))
    seq = lambda shape: pl.BlockSpec(shape, lambda i: (i,) + (0,) * (len(shape) - 1))
    ya, fin = pl.pallas_call(
        functools.partial(_lru_body, T=T),
        grid=(n_seq,),
        in_specs=[pl.BlockSpec((n_slab, T, LANES), lambda i: (0, i, 0)), seq((1, T, D_LRU)),
                  full((CONV_W, D_LRU)), full((1, D_LRU)),
                  full((4, 256, 1024)), full((2, D_LRU)), full((2, D_LRU)), full((2, D_LRU)),
                  seq((1, 2, D_LRU))],
        out_specs=[seq((1, T, D_LRU)), seq((1, 2, D_LRU))],
        out_shape=[jax.ShapeDtypeStruct((n_seq, T, D_LRU), BF16),
                   jax.ShapeDtypeStruct((n_seq, 2, D_LRU), F32)],
        scratch_shapes=[pltpu.VMEM((T + 3 * SUBLANES, D_LRU), F32)] + [pltpu.VMEM((T, D_LRU), F32)] * 4
                       + [pltpu.VMEM((n_slab, T, LANES), F32)],
        compiler_params=_cp(("parallel",)), name="lru",
    )(xb8, gg3, conv_w, conv_b, wg, ba, bx, lam, h0)
    return ya.reshape(n_seq * T, D_LRU), fin


def _s5_body(u_ref, wts_ref, wc_ref, ap_ref, h0_ref, y_ref, fin_ref, s_scr, hp_scr, slab,
             *, rt, n_c, with_h0, col_major):
    q = 512
    m = S5_SEG
    grp = SUBLANES * m
    n_grp = rt // grp
    seg_per_seq = n_c // m
    seq_per_grp = SUBLANES // seg_per_seq
    seq_tok = n_c * S5_L
    halves = seq_tok // (GRID_W * S5_L)
    sub = lax.broadcasted_iota(I32, (SUBLANES, q), 0)

    def pieces(g, i):
        out = []
        for v in range(SUBLANES):
            s, sq = divmod(v, seg_per_seq)
            tok0 = (g * seq_per_grp + s) * seq_tok
            if col_major:
                for h in range(halves):
                    n = m // halves
                    out.append((pl.ds(g * grp + h * SUBLANES + v, n, stride=halves * SUBLANES),
                                pl.ds(tok0 + (h * S5_L + i) * GRID_W + sq * n, n)))
            else:
                out.append((pl.ds(g * grp + v, m, stride=SUBLANES),
                            pl.ds(tok0 + sq * m * S5_L + i, m, stride=S5_L)))
        return out

    for i in range(S5_L):
        for g in range(n_grp):
            for rows, toks in pieces(g, i):
                slab[i, rows, :] = u_ref[0, toks, :]
    ub = jnp.concatenate([slab[i] for i in range(S5_L)], axis=1).astype(BF16)
    r1 = jnp.dot(ub, wts_ref[0], preferred_element_type=F32)
    y_in = r1[:, :S5_L * LANES]
    s_scr[...] = r1[:, S5_L * LANES:]

    def cmul(ar, ai, xr, xi):
        return ar * xr - ai * xi, ar * xi + ai * xr

    def local_scan(base, col0, reverse):
        ar, ai = ap_ref[0, 0:1, col0:col0 + q], ap_ref[0, 0:1, col0 + q:col0 + 2 * q]

        def step(k, carry):
            hr, hi = carry
            r0 = pl.multiple_of(base + (m - 1 - k if reverse else k) * SUBLANES, SUBLANES)
            pr, pi = cmul(ar, ai, hr, hi)
            hr = pr + s_scr[pl.ds(r0, SUBLANES), col0:col0 + q]
            hi = pi + s_scr[pl.ds(r0, SUBLANES), col0 + q:col0 + 2 * q]
            s_scr[pl.ds(r0, SUBLANES), col0:col0 + q] = hr
            s_scr[pl.ds(r0, SUBLANES), col0 + q:col0 + 2 * q] = hi
            return hr, hi
        zero = jnp.zeros((SUBLANES, q), F32)
        return lax.fori_loop(0, m, step, (zero, zero), unroll=2)

    def chain(g, col0, reverse, h_end):
        base = g * grp
        first = (sub & (seg_per_seq - 1)) == (seg_per_seq - 1 if reverse else 0)
        h0r = jnp.zeros((SUBLANES, q), F32)
        h0i = jnp.zeros((SUBLANES, q), F32)
        if with_h0:
            for s in range(seq_per_grp):
                row = h0_ref[0, 0, g * seq_per_grp + s:g * seq_per_grp + s + 1, :]
                mine = (sub >> (seg_per_seq.bit_length() - 1)) == s
                h0r = jnp.where(mine, row[:, col0:col0 + q], h0r)
                h0i = jnp.where(mine, row[:, col0 + q:col0 + 2 * q], h0i)
        cr, ci = h0r, h0i
        if seg_per_seq > 1:
            er, ei = ap_ref[0, m - 1:m, col0:col0 + q], ap_ref[0, m - 1:m, col0 + q:col0 + 2 * q]
            shift = SUBLANES - 1 if reverse else 1
            for _ in range(seg_per_seq - 1):
                pr, pi = cmul(er, ei, cr, ci)
                cr = jnp.where(first, h0r, pltpu.roll(h_end[0] + pr, shift, 0))
                ci = jnp.where(first, h0i, pltpu.roll(h_end[1] + pi, shift, 0))
        if with_h0 or seg_per_seq > 1:
            def add(c, _):
                r0 = pl.multiple_of(base + c * SUBLANES, SUBLANES)
                k = m - 1 - c if reverse else c
                pr, pi = cmul(ap_ref[0, pl.ds(k, 1), col0:col0 + q],
                              ap_ref[0, pl.ds(k, 1), col0 + q:col0 + 2 * q], cr, ci)
                s_scr[pl.ds(r0, SUBLANES), col0:col0 + q] += pr
                s_scr[pl.ds(r0, SUBLANES), col0 + q:col0 + 2 * q] += pi
                return 0
            lax.fori_loop(0, m, add, 0, unroll=2)
        return cr, ci

    for g in range(n_grp):
        base = g * grp
        body = slice(base + SUBLANES, base + grp)
        head = slice(base, base + SUBLANES)
        prev = slice(base, base + grp - SUBLANES)
        tail = slice(base + grp - SUBLANES, base + grp)
        cr, ci = chain(g, 0, False, local_scan(base, 0, False))
        hp_scr[body, 0:2 * q] = s_scr[prev, 0:2 * q]
        hp_scr[head, 0:q] = cr
        hp_scr[head, q:2 * q] = ci
        cr, ci = chain(g, 2 * q, True, local_scan(base, 2 * q, True))
        hp_scr[prev, 2 * q:4 * q] = s_scr[body, 2 * q:4 * q]
        hp_scr[tail, 2 * q:3 * q] = cr
        hp_scr[tail, 3 * q:4 * q] = ci
        if seg_per_seq == 1:
            rows = slice(g * SUBLANES, (g + 1) * SUBLANES)
            fin_ref[0, 0, rows, 0:2 * q] = s_scr[tail, 0:2 * q]
            fin_ref[0, 0, rows, 2 * q:4 * q] = s_scr[head, 2 * q:4 * q]
    if seg_per_seq > 1:
        fin_ref[...] = jnp.zeros(fin_ref.shape, F32)

    y = y_in + jnp.dot(hp_scr[...].astype(BF16), wc_ref[0], preferred_element_type=F32)
    for i in range(S5_L):
        slab[i] = y[:, i * LANES:(i + 1) * LANES]
        for g in range(n_grp):
            for rows, toks in pieces(g, i):
                y_ref[0, toks, :] = slab[i, rows, :]


def _s5_call(us4, wts, wc, apow, h0, n_seq, T, with_h0, col_major):
    n_c = T // S5_L
    rows = n_seq * n_c
    rt = 512
    nseq_t = rt // n_c
    h0 = h0.reshape(S5_BLK, n_seq // nseq_t, nseq_t, S5_SW)
    y4, fin = pl.pallas_call(
        functools.partial(_s5_body, rt=rt, n_c=n_c, with_h0=with_h0, col_major=col_major),
        grid=(S5_BLK, rows // rt),
        in_specs=[pl.BlockSpec((1, rt * S5_L, LANES), lambda b, r: (b, r, 0)),
                  pl.BlockSpec((1, S5_L * LANES, S5_L * LANES + S5_SW), lambda b, r: (b, 0, 0)),
                  pl.BlockSpec((1, S5_SW, S5_L * LANES), lambda b, r: (b, 0, 0)),
                  pl.BlockSpec((1, S5_SEG, S5_SW), lambda b, r: (b, 0, 0)),
                  pl.BlockSpec((1, 1, nseq_t, S5_SW), lambda b, r: (b, r, 0, 0))],
        out_specs=[pl.BlockSpec((1, rt * S5_L, LANES), lambda b, r: (b, r, 0)),
                   pl.BlockSpec((1, 1, nseq_t, S5_SW), lambda b, r: (b, r, 0, 0))],
        out_shape=[jax.ShapeDtypeStruct((S5_BLK, n_seq * T, LANES), F32),
                   jax.ShapeDtypeStruct((S5_BLK, n_seq // nseq_t, nseq_t, S5_SW), F32)],
        scratch_shapes=[pltpu.VMEM((rt, S5_SW), F32), pltpu.VMEM((rt, S5_SW), F32),
                        pltpu.VMEM((S5_L, rt, LANES), F32)],
        compiler_params=_cp(("parallel", "parallel")), name="s5",
    )(us4, wts, wc, apow, h0)
    return y4, fin.reshape(S5_BLK, n_seq, S5_SW)


def _s5_disc(lr, li, st):
    mag = jnp.exp(lr * st)
    ang = li * st
    ar, ai = mag * jnp.cos(ang), mag * jnp.sin(ang)
    den = lr * lr + li * li
    fr = ((ar - 1.0) * lr + ai * li) / den
    fi = (ai * lr - (ar - 1.0) * li) / den
    return ar, ai, fr, fi


def _s5_powers(ar, ai, n):
    out = [(jnp.ones_like(ar), jnp.zeros_like(ar))]
    for _ in range(n):
        pr, pi = out[-1]
        out.append((pr * ar - pi * ai, pr * ai + pi * ar))
    return out


def _s5_prep_body(rowp_ref, colp_ref, flatp_ref, bt_ref, cm_ref, wts_ref, wc_ref, ap_ref):
    L, N, P = S5_L, S5_N, S5_P
    w = L * LANES
    div = lambda x, d: x >> (d.bit_length() - 1)
    mod = lambda x, d: x & (d - 1)
    one = lambda m: jnp.where(m, 1.0, 0.0).astype(BF16)
    r1 = lax.broadcasted_iota(I32, (LANES, LANES), 0)
    c1 = lax.broadcasted_iota(I32, (LANES, LANES), 1)
    first_half = c1 < N
    sgn = jnp.where(first_half, -1.0, 1.0)

    e_maps, taps = [], []
    same_group = div(r1, P) == div(c1, P)
    fold = one(mod(r1, P) == mod(c1, P))
    for d in range(2):
        ar, ai, fr, fi = _s5_disc(rowp_ref[0, d, 0], rowp_ref[1, d, 0], rowp_ref[2, d, 0])
        bx1 = bt_ref[d, 0]
        bx2 = pltpu.roll(bx1, N, 1)
        x1 = fr * bx1 + (fi * sgn) * bx2
        x2 = pltpu.roll(x1, N, 1)
        e_d, t_d = [], []
        for pr, pi in _s5_powers(ar, ai, L):
            e_k = pr * x1 + (pi * sgn) * x2
            e_d.append(e_k)
            full = _dot3(e_k, cm_ref[d, 0])
            kept = jnp.where(same_group, full, 0.0)
            hi = kept.astype(BF16)
            mid = (kept - hi.astype(F32)).astype(BF16)
            lo = (kept - hi.astype(F32) - mid.astype(F32)).astype(BF16)
            t_d.append(sum(jnp.dot(part, fold, preferred_element_type=F32) for part in (hi, mid, lo)))
        e_maps.append(e_d)
        taps.append(t_d)

    r = lax.broadcasted_iota(I32, (LANES, w), 0)
    c = lax.broadcasted_iota(I32, (LANES, w), 1)
    t_tap = one((div(r, P) == div(c, LANES)) & (mod(r, P) == mod(c, P)))
    m_tap = div(r, P) == div(mod(c, LANES), P)
    t_st = one((div(r, N) == div(c, 8 * N)) & (mod(r, N) == mod(c, N)))
    m_st = div(r, P) == div(mod(c, 8 * N), N)
    col_blk = div(c1, P)
    for j in range(L):
        rows = slice(j * LANES, (j + 1) * LANES)
        toe = jnp.zeros((LANES, LANES), F32)
        for k in range(L):
            toe = (toe + jnp.where(col_blk == j + k, taps[0][k], 0.0)
                   + jnp.where(col_blk == j - k, taps[1][k], 0.0))
        tap = jnp.dot(toe.astype(BF16), t_tap, preferred_element_type=F32)
        wts_ref[0, rows, 0:w] = jnp.where(m_tap, tap, 0.0).astype(BF16)
        for k, e_j in enumerate((e_maps[0][L - 1 - j], e_maps[1][j])):
            st = jnp.dot(e_j.astype(BF16), t_st, preferred_element_type=F32)
            wts_ref[0, rows, w + k * w:w + (k + 1) * w] = jnp.where(m_st, st, 0.0).astype(BF16)

    rr = lax.broadcasted_iota(I32, (w, LANES), 0)
    cc = lax.broadcasted_iota(I32, (w, LANES), 1)
    t_row = one((div(rr, 8 * N) == div(cc, N)) & (mod(rr, N) == mod(cc, N)))
    r2 = lax.broadcasted_iota(I32, (w, w), 0)
    c2 = lax.broadcasted_iota(I32, (w, w), 1)
    m_row = div(mod(r2, 8 * N), N) == div(mod(c2, LANES), P)
    for d in range(2):
        ar, ai, _, _ = _s5_disc(colp_ref[0, d, 0], colp_ref[1, d, 0], colp_ref[2, d, 0])
        a1 = cm_ref[d, 0]
        swapped = pltpu.roll(a1, N, 0)
        a2 = jnp.where(r1 < N, swapped, -swapped)
        pw = _s5_powers(ar, ai, L)
        ks = [i + 1 for i in range(L)] if d == 0 else [L - i for i in range(L)]
        g = jnp.concatenate([a1 * pw[k][0] + a2 * pw[k][1] for k in ks], axis=1)
        corr = jnp.dot(t_row, g.astype(BF16), preferred_element_type=F32)
        wc_ref[0, d * w:(d + 1) * w, :] = jnp.where(m_row, corr, 0.0).astype(BF16)

    al = []
    for d in range(2):
        ar, ai, _, _ = _s5_disc(flatp_ref[0, d, 0], flatp_ref[1, d, 0], flatp_ref[2, d, 0])
        al.append(_s5_powers(ar, ai, L)[L])
    cur = al
    for k in range(S5_SEG):
        ap_ref[0, k:k + 1, :] = jnp.concatenate([cur[0][0], cur[0][1], cur[1][0], cur[1][1]], axis=1)
        cur = [(cr * br - ci * bi, cr * bi + ci * br) for (cr, ci), (br, bi) in zip(cur, al)]


def _s5_weights(lam_re, lam_im, log_step, b_re, b_im, c_re, c_im):
    L, G, N, P = S5_L, S5_G, S5_N, S5_P
    w = L * LANES
    step = jnp.broadcast_to(jnp.exp(log_step)[..., None], lam_re.shape)
    prm = jnp.stack([lam_re, lam_im, step])
    rowp = jnp.broadcast_to(prm[:, :, :, None, None, :], (3, 2, G, P, 2, N)).reshape(3, 2, S5_BLK, LANES, 2 * N)
    colp = jnp.broadcast_to(prm.transpose(0, 1, 3, 2)[:, :, None, :, :, None], (3, 2, 2, N, G, P))
    colp = colp.reshape(3, 2, 2 * N, S5_BLK, LANES).transpose(0, 1, 3, 2, 4)
    flatp = prm.reshape(3, 2, S5_BLK, 1, 8 * N)
    bt = jnp.stack([b_re, b_im], axis=3).transpose(0, 1, 4, 3, 2)
    bt = bt.reshape(2, S5_BLK, LANES, 2 * N)
    cm = jnp.stack([c_re, -c_im], axis=1).transpose(0, 1, 4, 2, 3)
    cm = cm.reshape(2, 2 * N, S5_BLK, LANES).transpose(0, 2, 1, 3)
    blk = lambda lead, a, b_: pl.BlockSpec(lead + (1, a, b_), lambda i: (0,) * len(lead) + (i, 0, 0))
    out = lambda a, b_: pl.BlockSpec((1, a, b_), lambda i: (i, 0, 0))
    return pl.pallas_call(
        _s5_prep_body, grid=(S5_BLK,),
        in_specs=[blk((3, 2), LANES, LANES), blk((3, 2), LANES, LANES), blk((3, 2), 1, 8 * N),
                  blk((2,), LANES, LANES), blk((2,), LANES, LANES)],
        out_specs=[out(w, w + S5_SW), out(S5_SW, w), out(S5_SEG, S5_SW)],
        out_shape=[jax.ShapeDtypeStruct((S5_BLK, w, w + S5_SW), BF16),
                   jax.ShapeDtypeStruct((S5_BLK, S5_SW, w), BF16),
                   jax.ShapeDtypeStruct((S5_BLK, S5_SEG, S5_SW), F32)],
        compiler_params=_cp(("parallel",)), name="s5_prep",
    )(rowp, colp, flatp, bt, cm)


def _mixout_body(xp_ref, xs_ref, mod_ref, g1_ref, g2_ref, wbg_ref, bbg_ref, yap_ref, yas_ref,
                 y4p_ref, y4s_ref, u4p_ref, u4s_ref, *rest, tiles_p):
    i = pl.program_id(0)

    @pl.when(i < tiles_p)
    def _():
        _mixout_tile(xp_ref, mod_ref, g1_ref, g2_ref, wbg_ref, bbg_ref, yap_ref, y4p_ref, u4p_ref, *rest)

    @pl.when(i >= tiles_p)
    def _():
        _mixout_tile(xs_ref, mod_ref, g1_ref, g2_ref, wbg_ref, bbg_ref, yas_ref, y4s_ref, u4s_ref, *rest)


def _mixout_tile(x_ref, mod_ref, g1_ref, g2_ref, wbg_ref, bbg_ref, ya_ref, y4_ref, u4_ref,
                 dsk_ref, wglu_ref, bglu_ref, wso_ref, wlo_ref, wo_ref, wrt_ref, brt_ref,
                 x1_ref, h2_ref, aff_ref):
    m = mod_ref[0]
    sh1, sc1, gt1 = m[:, 0:D_MODEL], m[:, D_MODEL:2 * D_MODEL], m[:, 2 * D_MODEL:3 * D_MODEL]
    sh2, sc2 = m[:, 3 * D_MODEL:4 * D_MODEL], m[:, 4 * D_MODEL:5 * D_MODEL]
    x = x_ref[...]
    h = _norm_mod(x, g1_ref[...], sc1, sh1)
    gates = _sigmoid(jnp.dot(h.astype(BF16), wbg_ref[...], preferred_element_type=F32) + bbg_ref[...])

    def assemble(ref):
        return jnp.concatenate([ref[k] for k in range(S5_BLK)], axis=1)

    ys = assemble(y4_ref) + dsk_ref[...] * assemble(u4_ref)
    v = _gelu(ys)
    ob = v * _sigmoid(jnp.dot(v.astype(BF16), wglu_ref[...], preferred_element_type=F32) + bglu_ref[...])
    merged = (gates[:, :D_MODEL] * jnp.dot(ya_ref[...], wlo_ref[...], preferred_element_type=F32)
              + gates[:, D_MODEL:] * jnp.dot(ob.astype(BF16), wso_ref[...], preferred_element_type=F32))
    mix = jnp.dot(merged.astype(BF16), wo_ref[...], preferred_element_type=F32)
    x1 = x + gt1 * mix
    x1_ref[...] = x1
    h2 = _norm_mod(x1, g2_ref[...], sc2, sh2)
    h2_ref[...] = h2
    logits = _dot3(wrt_ref[...], h2, dims=(((1,), (1,)), ((), ()))) + brt_ref[...]
    mx = jnp.max(logits, axis=0, keepdims=True)
    ex = jnp.exp(logits - mx)
    aff_ref[...] = ex / jnp.sum(ex, axis=0, keepdims=True)


def _mixout_call(xp, xs, mod3, g1, g2, wbg, bbg, ya_p, ya_s, y4_p, y4_s, u4_p, u4_s,
                 dsk, wglu, bglu, wso, wlo, wo, wrt, brt, mod_row, tm):
    n_p, n_s = xp.shape[0], xs.shape[0]
    tiles_p = n_p // tm
    n = n_p + n_s
    full = lambda shape: pl.BlockSpec(shape, lambda i: (0,) * len(shape))
    at_p = lambda i: jnp.minimum(i, tiles_p - 1)
    at_s = lambda i: jnp.maximum(i - tiles_p, 0)
    tok = lambda w, at: pl.BlockSpec((tm, w), lambda i: (at(i), 0))
    s5 = lambda at: pl.BlockSpec((S5_BLK, tm, LANES), lambda i: (0, at(i), 0))
    out = lambda w: pl.BlockSpec((tm, w), lambda i: (i, 0))
    return pl.pallas_call(
        functools.partial(_mixout_body, tiles_p=tiles_p),
        grid=(n // tm,),
        in_specs=[tok(D_MODEL, at_p), tok(D_MODEL, at_s),
                  pl.BlockSpec((1, 1, 6 * D_MODEL), lambda i: (mod_row(i), 0, 0)),
                  full((1, D_MODEL)), full((1, D_MODEL)), full((D_MODEL, 2 * D_MODEL)),
                  full((1, 2 * D_MODEL)), tok(D_LRU, at_p), tok(D_LRU, at_s),
                  s5(at_p), s5(at_s), s5(at_p), s5(at_s), full((1, D_S5)),
                  full((D_S5, D_S5)), full((1, D_S5)), full((D_S5, D_MODEL)), full((D_LRU, D_MODEL)),
                  full((D_MODEL, D_MODEL)), full((N_EXPERTS, D_MODEL)), full((N_EXPERTS, 1))],
        out_specs=[out(D_MODEL), out(D_MODEL), pl.BlockSpec((N_EXPERTS, tm), lambda i: (0, i))],
        out_shape=[jax.ShapeDtypeStruct((n, D_MODEL), F32), jax.ShapeDtypeStruct((n, D_MODEL), F32),
                   jax.ShapeDtypeStruct((N_EXPERTS, n), F32)],
        compiler_params=_cp(("arbitrary",)), name="mixout",
    )(xp, xs, mod3, g1, g2, wbg, bbg, ya_p, ya_s, y4_p, y4_s, u4_p, u4_s,
      dsk, wglu, bglu, wso, wlo, wo, wrt, brt)


def _select_body(aff_ref, gate_ref, pos_ref, offs_ref, *, n_tok, cap):
    aff = aff_ref[...]
    capf = float(cap)

    def bis(_, lh):
        lo, hi = lh
        mid = lo + ((hi - lo + 1) >> 1)
        cnt = jnp.sum(jnp.where(aff >= pltpu.bitcast(mid, F32), 1.0, 0.0), axis=1, keepdims=True)
        ok = cnt >= capf
        return jnp.where(ok, mid, lo), jnp.where(ok, hi, mid - 1)

    lo0 = jnp.zeros((N_EXPERTS, 1), I32)
    hi0 = jnp.full((N_EXPERTS, 1), 0x7F800000, I32)
    thr_bits, _ = lax.fori_loop(0, 31, bis, (lo0, hi0))
    thr = pltpu.bitcast(thr_bits, F32)
    need = capf - jnp.sum(jnp.where(aff > thr, 1.0, 0.0), axis=1, keepdims=True)

    r = lax.broadcasted_iota(I32, (TOK_TILE, TOK_TILE), 0)
    c = lax.broadcasted_iota(I32, (TOK_TILE, TOK_TILE), 1)
    tri = jnp.where(r < c, 1.0, 0.0).astype(BF16)
    lane = lax.broadcasted_iota(I32, (N_EXPERTS, LANES), 1)
    n_tiles = n_tok // TOK_TILE
    run_eq = jnp.zeros((N_EXPERTS, 1), F32)
    run_sel = jnp.zeros((N_EXPERTS, 1), F32)
    offs = jnp.zeros((N_EXPERTS, LANES), F32)
    for t in range(n_tiles):
        sl = slice(t * TOK_TILE, (t + 1) * TOK_TILE)
        aff_t = aff_ref[:, sl]
        is_eq = aff_t == thr
        eq_t = jnp.where(is_eq, 1.0, 0.0)
        rank_eq = jnp.dot(eq_t.astype(BF16), tri, preferred_element_type=F32) + run_eq
        cand = (aff_t > thr) | (is_eq & (rank_eq < need))
        pos_t = jnp.dot(jnp.where(cand, 1.0, 0.0).astype(BF16), tri, preferred_element_type=F32) + run_sel
        sel_t = cand & (pos_t < capf)
        sel_f = jnp.where(sel_t, 1.0, 0.0)
        offs = jnp.where(lane == t, run_sel, offs)
        gate_ref[:, sl] = jnp.where(sel_t, aff_t, 0.0)
        pos_ref[:, sl] = jnp.where(sel_t, pos_t, -1.0).astype(I32)
        run_eq = run_eq + jnp.sum(eq_t, axis=1, keepdims=True)
        run_sel = run_sel + jnp.sum(sel_f, axis=1, keepdims=True)
    offs = jnp.where(lane >= n_tiles, run_sel, offs)
    offs_ref[...] = offs.astype(I32)


def _select_call(aff_t, cap, n_tok):
    n_path = aff_t.shape[1] // n_tok
    per_p = lambda a, b: pl.BlockSpec((None, a, b), lambda p: (p, 0, 0))
    return pl.pallas_call(
        functools.partial(_select_body, n_tok=n_tok, cap=cap),
        grid=(n_path,),
        in_specs=[pl.BlockSpec((N_EXPERTS, n_tok), lambda p: (0, p))],
        out_specs=[per_p(N_EXPERTS, n_tok), per_p(N_EXPERTS, n_tok), per_p(N_EXPERTS, LANES)],
        out_shape=[jax.ShapeDtypeStruct((n_path, N_EXPERTS, n_tok), F32),
                   jax.ShapeDtypeStruct((n_path, N_EXPERTS, n_tok), I32),
                   jax.ShapeDtypeStruct((n_path, N_EXPERTS, LANES), I32)],
        compiler_params=_cp(("parallel",)), name="select",
    )(aff_t)


def _invert_body(offs_ref, pos_ref, gate_ref, idx_ref, gcol_ref, acc_i, acc_g, *, n_tiles, cap):
    path = pl.program_id(0)
    base = (path * N_EXPERTS + pl.program_id(1)) * LANES
    acc_i[...] = jnp.zeros(acc_i.shape, F32)
    acc_g[...] = jnp.zeros(acc_g.shape, F32)
    jrow = lax.broadcasted_iota(I32, (SLOT_CHUNK, LANES), 0)
    lane = lax.broadcasted_iota(I32, (1, LANES), 1)

    def per_tile(t, _):
        first = (offs_ref[base + t] >> 3) << 3
        prow = pos_ref[0, 0, pl.ds(t, 1), :]
        grow = gate_ref[0, 0, pl.ds(t, 1), :]
        tok = (lane + (t * TOK_TILE + 1)).astype(F32)

        def window(w, _):
            start = pl.multiple_of(first + w * SLOT_CHUNK, SUBLANES)
            rows = pl.ds(start, SLOT_CHUNK)
            hit = prow == (jrow + start)
            acc_i[rows, :] = jnp.maximum(acc_i[rows, :], jnp.where(hit, tok, 0.0))
            acc_g[rows, :] = jnp.maximum(acc_g[rows, :], jnp.where(hit, grow, 0.0))
            return 0

        n_win = (offs_ref[base + t + 1] - first + SLOT_CHUNK - 1) >> (SLOT_CHUNK.bit_length() - 1)
        lax.fori_loop(0, n_win, window, 0)
        return 0

    lax.fori_loop(0, n_tiles, per_tile, 0)
    hit_i = acc_i[0:cap, :]
    hi = hit_i.astype(BF16)
    mid = (hit_i - hi.astype(F32)).astype(BF16)
    lo = (hit_i - hi.astype(F32) - mid.astype(F32)).astype(BF16)
    ones = jnp.ones((SUBLANES, LANES), BF16)
    nt = (((1,), (1,)), ((), ()))
    tok1 = sum(lax.dot_general(ones, part, nt, preferred_element_type=F32) for part in (hi, mid, lo))
    idx_ref[0, 0] = tok1[0:1, :].astype(I32) - 1 + path * (n_tiles * TOK_TILE)
    gcol_ref[0, 0] = jnp.max(acc_g[0:cap, :], axis=1, keepdims=True)


def _invert_call(offs, pos, gate, cap):
    n_path, _, n_tok = pos.shape
    n_tiles = n_tok // TOK_TILE
    pos4 = pos.reshape(n_path, N_EXPERTS, n_tiles, TOK_TILE)
    gate4 = gate.reshape(n_path, N_EXPERTS, n_tiles, TOK_TILE)
    per_e = lambda a, b: pl.BlockSpec((1, 1, a, b), lambda p, e, o: (p, e, 0, 0))
    return pl.pallas_call(
        functools.partial(_invert_body, n_tiles=n_tiles, cap=cap),
        grid_spec=pltpu.PrefetchScalarGridSpec(
            num_scalar_prefetch=1, grid=(n_path, N_EXPERTS),
            in_specs=[per_e(n_tiles, TOK_TILE), per_e(n_tiles, TOK_TILE)],
            out_specs=[per_e(1, cap), per_e(cap, 1)],
            scratch_shapes=[pltpu.VMEM((cap + SLOT_CHUNK, LANES), F32)] * 2),
        out_shape=[jax.ShapeDtypeStruct((n_path, N_EXPERTS, 1, cap), I32),
                   jax.ShapeDtypeStruct((n_path, N_EXPERTS, cap, 1), F32)],
        compiler_params=_cp(("parallel", "parallel")), name="invert",
    )(offs.reshape(-1), pos4, gate4)


def _ffn_body(idx0_ref, idxn_ref, gcol_ref, wg_ref, wu_ref, wd_ref, h_hbm, out_ref,
              xe, wgb, wub, wdb, sem, *, cap):
    e = pl.program_id(0)
    p = pl.program_id(1)
    nxt = 1 - p
    step = e * 2 + p

    def issue_rows(idx_ref, s, j0, n):
        for j in range(j0, j0 + n):
            pltpu.make_async_copy(h_hbm.at[pl.ds(idx_ref[0, 0, 0, j], 1), :],
                                  xe.at[s, pl.ds(j, 1), :], sem.at[s]).start()

    @pl.when(step == 0)
    def _():
        def issue(j, _):
            pltpu.make_async_copy(h_hbm.at[pl.ds(idx0_ref[0, 0, 0, j], 1), :],
                                  xe.at[0, pl.ds(j, 1), :], sem.at[0]).start()
            return 0
        lax.fori_loop(0, cap, issue, 0, unroll=8)

    @pl.when(p == 0)
    def _():
        wgb[...] = wg_ref[0].astype(BF16)
        wub[...] = wu_ref[0].astype(BF16)
        wdb[...] = wd_ref[0].astype(BF16)

    pltpu.make_async_copy(h_hbm.at[pl.ds(0, cap), :], xe.at[p], sem.at[p]).wait()

    ch = 256
    n_ch = cap // ch
    per = -(-cap // (n_ch - 1))
    for c in range(n_ch):
        sl = slice(c * ch, (c + 1) * ch)
        xb = xe[p, sl, :].astype(BF16)
        issue_rows(idxn_ref, nxt, min(c * per, cap), min((c + 1) * per, cap) - min(c * per, cap))
        g = jnp.dot(xb, wgb[...], preferred_element_type=F32)
        u = jnp.dot(xb, wub[...], preferred_element_type=F32)
        hid = (g * _sigmoid(g)) * u
        ye = jnp.dot(hid.astype(BF16), wdb[...], preferred_element_type=F32)
        out_ref[0, 0, sl, :] = ye * gcol_ref[0, 0, sl, :]

    @pl.when(step == 2 * N_EXPERTS - 1)
    def _():
        pltpu.make_async_copy(h_hbm.at[pl.ds(0, cap), :], xe.at[nxt], sem.at[nxt]).wait()


def _ffn_call(idx, gcol, w_gate, w_up, w_down, h2, cap):
    d_e = w_gate.shape[2]
    wspec = lambda a, b: pl.BlockSpec((1, a, b), lambda e, p: (e, 0, 0))
    return pl.pallas_call(
        functools.partial(_ffn_body, cap=cap),
        grid=(N_EXPERTS, 2),
        in_specs=[pl.BlockSpec((1, 1, 1, cap), lambda e, p: (0, 0, 0, 0), memory_space=pltpu.SMEM),
                  pl.BlockSpec((1, 1, 1, cap),
                               lambda e, p: (1 - p, jnp.minimum(e + p, N_EXPERTS - 1), 0, 0),
                               memory_space=pltpu.SMEM),
                  pl.BlockSpec((1, 1, cap, 1), lambda e, p: (p, e, 0, 0)),
                  wspec(D_MODEL, d_e), wspec(D_MODEL, d_e), wspec(d_e, D_MODEL),
                  pl.BlockSpec(memory_space=pl.ANY)],
        out_specs=pl.BlockSpec((1, 1, cap, D_MODEL), lambda e, p: (p, e, 0, 0)),
        out_shape=jax.ShapeDtypeStruct((2, N_EXPERTS, cap, D_MODEL), F32),
        scratch_shapes=[pltpu.VMEM((2, cap, D_MODEL), F32), pltpu.VMEM((D_MODEL, d_e), BF16),
                        pltpu.VMEM((D_MODEL, d_e), BF16), pltpu.VMEM((d_e, D_MODEL), BF16),
                        pltpu.SemaphoreType.DMA((2,))],
        compiler_params=_cp(("arbitrary", "arbitrary")), name="ffn",
    )(idx, idx, gcol, w_gate, w_up, w_down, h2)


def _combine_body(offs_ref, x1_ref, mod_ref, gf_ref, pos_ref, yb_hbm, y_ref, wins, acc_ref, sem,
                  *, cap, path):
    t = pl.program_id(0)
    n_t = pl.num_programs(0)
    slot = t % 2
    row0 = path * N_EXPERTS * LANES

    def kmax_of(tt):
        k = 0
        for e in range(N_EXPERTS):
            k = jnp.maximum(k, offs_ref[row0 + e * LANES + tt + 1] - offs_ref[row0 + e * LANES + tt])
        return k

    def geometry(tt, e, w, win):
        fetch = win + SUBLANES
        lo = offs_ref[row0 + e * LANES + tt] + w * win
        start = pl.multiple_of(jnp.minimum((lo >> 3) << 3, cap - fetch), SUBLANES)
        return lo, start, fetch

    def issue(tt, w, s, win):
        for e in range(N_EXPERTS):
            _, start, fetch = geometry(tt, e, w, win)
            pltpu.make_async_copy(yb_hbm.at[path, e, pl.ds(start, fetch), :],
                                  wins.at[s, pl.ds(e * fetch, fetch), :], sem.at[s]).start()

    def drain(s, win):
        rows = N_EXPERTS * (win + SUBLANES)
        pltpu.make_async_copy(yb_hbm.at[0, 0, pl.ds(0, rows), :], wins.at[s, pl.ds(0, rows), :],
                              sem.at[s]).wait()

    def expand(w, s, win):
        sub = lax.broadcasted_iota(I32, (win + SUBLANES, TOK_TILE), 0)
        rows = []
        for e in range(N_EXPERTS):
            lo, start, _ = geometry(t, e, w, win)
            pos = pos_ref[e:e + 1, :]
            in_round = jnp.logical_and(pos >= lo, pos < lo + win)
            rows.append(jnp.where(jnp.logical_and(pos - start == sub, in_round), 1.0, 0.0))
        onehot = jnp.concatenate(rows, axis=0).T.astype(BF16)
        data = wins[s, 0:N_EXPERTS * (win + SUBLANES), :]
        hi = data.astype(BF16)
        lo_part = (data - hi.astype(F32)).astype(BF16)
        return (jnp.dot(onehot, hi, preferred_element_type=F32)
                + jnp.dot(onehot, lo_part, preferred_element_type=F32))

    def issue_first(tt, s):
        wide = kmax_of(tt) > COMBINE_WIN

        @pl.when(wide)
        def _():
            issue(tt, 0, s, COMBINE_WIN_WIDE)

        @pl.when(jnp.logical_not(wide))
        def _():
            issue(tt, 0, s, COMBINE_WIN)

    @pl.when(t == 0)
    def _():
        issue_first(0, 0)

    @pl.when(t + 1 < n_t)
    def _():
        issue_first(t + 1, 1 - slot)

    kmax = kmax_of(t)

    @pl.when(kmax > COMBINE_WIN)
    def _():
        drain(slot, COMBINE_WIN_WIDE)
        acc_ref[...] = expand(0, slot, COMBINE_WIN_WIDE)

    @pl.when(kmax <= COMBINE_WIN)
    def _():
        drain(slot, COMBINE_WIN)
        acc_ref[...] = expand(0, slot, COMBINE_WIN)

    def more(w, _):
        issue(t, w, slot, COMBINE_WIN_WIDE)
        drain(slot, COMBINE_WIN_WIDE)
        acc_ref[...] += expand(w, slot, COMBINE_WIN_WIDE)
        return 0
    lax.fori_loop(1, (kmax + COMBINE_WIN_WIDE - 1) >> (COMBINE_WIN_WIDE.bit_length() - 1), more, 0)
    acc = acc_ref[...]

    gt2 = mod_ref[0][:, 5 * D_MODEL:6 * D_MODEL]
    x2 = x1_ref[...] + gt2 * acc
    ms = jnp.mean(x2 * x2, axis=-1, keepdims=True)
    y_ref[...] = x2 * lax.rsqrt(ms + EPS) * gf_ref[...]


def _combine_call(offs, pos, x1, mod3, gf, ybuf, mod_row, cap, path, n):
    first = path * n // TOK_TILE
    return pl.pallas_call(
        functools.partial(_combine_body, cap=cap, path=path),
        grid_spec=pltpu.PrefetchScalarGridSpec(
            num_scalar_prefetch=1, grid=(n // TOK_TILE,),
            in_specs=[pl.BlockSpec((TOK_TILE, D_MODEL), lambda i, o: (first + i, 0)),
                      pl.BlockSpec((1, 1, 6 * D_MODEL), lambda i, o: (mod_row(i), 0, 0)),
                      pl.BlockSpec((1, D_MODEL), lambda i, o: (0, 0)),
                      pl.BlockSpec((None, N_EXPERTS, TOK_TILE), lambda i, o: (path, 0, i)),
                      pl.BlockSpec(memory_space=pl.ANY)],
            out_specs=pl.BlockSpec((TOK_TILE, D_MODEL), lambda i, o: (i, 0)),
            scratch_shapes=[pltpu.VMEM((2, N_EXPERTS * (COMBINE_WIN_WIDE + SUBLANES), D_MODEL), F32),
                            pltpu.VMEM((TOK_TILE, D_MODEL), F32),
                            pltpu.SemaphoreType.DMA((2,))]),
        out_shape=jax.ShapeDtypeStruct((n, D_MODEL), F32),
        compiler_params=_cp(("arbitrary",)), name="combine",
    )(offs.reshape(-1), x1, mod3, gf, pos, ybuf)


def _lru_gate_weights(wa, wx):
    eye = jnp.eye(4, dtype=F32)

    def bd(w):
        w5 = w.reshape(4, 4, LRU_HEAD_DIM, LRU_HEAD_DIM)
        return jnp.einsum('khij,hg->khigj', w5, eye).reshape(4, 256, 256)

    return jnp.concatenate([bd(wa[0]), bd(wa[1]), bd(wx[0]), bd(wx[1])], axis=2).astype(BF16)


def kernel(x_prompt, x_sample, state_lru, state_s5_re, state_s5_im, c, c_ctx, w_mod, b_mod, g_norm1, g_norm2, w_in, conv_w, conv_b, lru_wa, lru_ba, lru_wx, lru_bx, lru_lambda, s5_lambda_re, s5_lambda_im, s5_log_step, s5_b_re, s5_b_im, s5_c_re, s5_c_im, s5_d, s5_w_glu, s5_b_glu, w_lru_out, w_s5_out, w_branch_gate, b_branch_gate, w_o, w_router, b_router, w_e_gate, w_e_up, w_e_down, g_final):
    bp, tp, _ = x_prompt.shape
    bs, ts, _ = x_sample.shape
    n_p, n_s = bp * tp, bs * ts
    l = 0

    c_all = jnp.zeros((16, D_MODEL), F32).at[0].set(c_ctx).at[1:1 + bs].set(c)
    mod3 = _mod_call(c_all, w_mod[l], b_mod[l][None, :]).reshape(16, 1, 6 * D_MODEL)

    w_in_b = w_in[l].astype(BF16)
    w_main, w_s5in = w_in_b[:, :2 * D_LRU], w_in_b[:, 2 * D_LRU:]
    g1 = g_norm1[l][None, :]
    g2 = g_norm2[l][None, :]
    wg = _lru_gate_weights(lru_wa[l], lru_wx[l])
    wts, wc, apow = _s5_weights(s5_lambda_re[l], s5_lambda_im[l], s5_log_step[l],
                                s5_b_re[l], s5_b_im[l], s5_c_re[l], s5_c_im[l])
    wbg = w_branch_gate[l].astype(BF16)
    bbg = b_branch_gate[l][None, :]
    wglu = s5_w_glu[l].astype(BF16)
    wso = w_s5_out[l].astype(BF16)
    wlo = w_lru_out[l].astype(BF16)
    wo = w_o[l].astype(BF16)
    wrt = w_router[l].T
    brt = b_router[l][:, None]

    xp2 = x_prompt.reshape(n_p, D_MODEL)
    xs2 = x_sample.reshape(n_s, D_MODEL)
    ctx_row = lambda i: 0
    tm_in = 1024
    lat_row_in = lambda i: 1 + i // (ts // max(tm_in, ts))

    def s5_state(re, im):
        def part(a, d):
            return a[:, d].reshape(-1, S5_BLK, 8 * S5_N).transpose(1, 0, 2)
        return jnp.concatenate([part(re, 0), part(im, 0), part(re, 1), part(im, 1)], axis=2)

    xb_p, gg_p, us4_p = _inproj_call(xp2, mod3, g1, w_main, w_s5in, ctx_row, max(tm_in, tp), tp)
    ya_p, lru_fin = _lru_call(xb_p, gg_p, conv_w[l], conv_b[l][None, :], wg, lru_ba[l], lru_bx[l],
                              lru_lambda[l], jnp.zeros((bp, 2, D_LRU), F32), bp, tp)
    y4_p, s5_fin = _s5_call(us4_p, wts, wc, apow, jnp.zeros((S5_BLK, bp, S5_SW), F32), bp, tp,
                            with_h0=False, col_major=False)
    xb_s, gg_s, us4_s = _inproj_call(xs2, mod3, g1, w_main, w_s5in, lat_row_in, max(tm_in, ts), ts)
    ya_s, _ = _lru_call(xb_s, gg_s, conv_w[l], conv_b[l][None, :], wg, lru_ba[l], lru_bx[l],
                        lru_lambda[l], state_lru[:, l], bs, ts)
    y4_s, _ = _s5_call(us4_s, wts, wc, apow, s5_state(state_s5_re[:, l], state_s5_im[:, l]), bs, ts,
                       with_h0=True, col_major=True)

    tm_mo = 512
    tiles_p = n_p // tm_mo
    x1, h2, aff = _mixout_call(
        xp2, xs2, mod3, g1, g2, wbg, bbg, ya_p, ya_s, y4_p, y4_s, us4_p, us4_s,
        s5_d[l][None, :], wglu, s5_b_glu[l][None, :], wso, wlo, wo, wrt, brt,
        mod_row=lambda i: jnp.where(i < tiles_p, 0, 1 + (i - tiles_p) // (ts // tm_mo)), tm=tm_mo)

    assert n_p == n_s
    cap = (CAPACITY_FACTOR * n_p) // N_EXPERTS
    gate, pos, offs = _select_call(aff, cap, n_p)
    idx, gcol = _invert_call(offs, pos, gate, cap)
    ybuf = _ffn_call(idx, gcol, w_e_gate[l], w_e_up[l], w_e_down[l], h2, cap)
    gf = g_final[None, :]
    y_p = _combine_call(offs, pos, x1, mod3, gf, ybuf, ctx_row, cap, 0, n_p)
    y_s = _combine_call(offs, pos, x1, mod3, gf, ybuf, lambda i: 1 + i // (ts // TOK_TILE), cap, 1, n_s)

    new_lru = lru_fin[:, None]
    sf = s5_fin.transpose(1, 0, 2).reshape(bp, S5_BLK, 2, 2, 8, S5_N)
    sf = sf.transpose(0, 2, 3, 1, 4, 5).reshape(bp, 2, 2, S5_G, S5_N)
    new_s5r = sf[:, :, 0][:, None]
    new_s5i = sf[:, :, 1][:, None]
    return (y_p.reshape(bp, tp, D_MODEL), y_s.reshape(bs, ts, D_MODEL), new_lru, new_s5r, new_s5i)
```

```python
import functools
import math

import jax
import jax.numpy as jnp
from jax import lax
from jax.experimental import pallas as pl
from jax.experimental.pallas import tpu as pltpu

F32 = jnp.float32
BF16 = jnp.bfloat16
I32 = jnp.int32

D_MODEL = 1024
D_LRU = 1024
LRU_HEADS = 16
LRU_HEAD_DIM = 64
LRU_C = 8.0
CONV_W = 4
D_S5 = 512
S5_P = 16
S5_G = 32
S5_N = 64
GRID_W = 64
N_EXPERTS = 16
CAPACITY_FACTOR = 2
EPS = 1e-6

LANES = 128
SUBLANES = 8
S5_L = 8
S5_BLK = 4
S5_SW = 4 * 512
S5_SEG = 32
TOK_TILE = 128
SLOT_CHUNK = 32
COMBINE_TILE = 256
COMBINE_WIN = 64
COMBINE_WIN_WIDE = 128
VMEM_LIMIT = 56 * 1024 * 1024


def _cp(sem, vmem=VMEM_LIMIT):
    return pltpu.CompilerParams(dimension_semantics=sem, vmem_limit_bytes=vmem)


def _split2(a):
    hi = a.astype(BF16)
    lo = (a - hi.astype(F32)).astype(BF16)
    return hi, lo


def _dot3(a, b, dims=(((1,), (0,)), ((), ()))):
    ah, al = _split2(a)
    bh, bl = _split2(b)
    d = functools.partial(lax.dot_general, dimension_numbers=dims, preferred_element_type=F32)
    return d(ah, bh) + (d(al, bh) + d(ah, bl))


def _sigmoid(x):
    return 0.5 * jnp.tanh(0.5 * x) + 0.5


def _gelu(x):
    c = math.sqrt(2.0 / math.pi)
    half = 0.5 * x
    return half + half * jnp.tanh(x * (c + (c * 0.044715) * (x * x)))


def _norm_mod(x, g, scale, shift):
    ms = jnp.mean(x * x, axis=-1, keepdims=True)
    return (x * lax.rsqrt(ms + EPS)) * (g * (1.0 + scale)) + shift


def _mod_body(c_ref, w_ref, b_ref, o_ref):
    c = c_ref[...]
    s = c * _sigmoid(c)
    o_ref[...] = _dot3(s, w_ref[...]) + b_ref[...]


def _mod_call(c_all, w_mod, b_mod):
    n = w_mod.shape[1]
    tn = 1536
    return pl.pallas_call(
        _mod_body,
        grid=(n // tn,),
        in_specs=[pl.BlockSpec((16, D_MODEL), lambda j: (0, 0)),
                  pl.BlockSpec((D_MODEL, tn), lambda j: (0, j)),
                  pl.BlockSpec((1, tn), lambda j: (0, j))],
        out_specs=pl.BlockSpec((16, tn), lambda j: (0, j)),
        out_shape=jax.ShapeDtypeStruct((16, n), F32),
        compiler_params=_cp(("arbitrary",)),
        name="mod",
    )(c_all, w_mod, b_mod)


def _inproj_body(x_ref, mod_ref, g_ref, w_ref, ws_ref, xb_ref, gg_ref, us_ref, *, seq_len):
    m = mod_ref[0]
    h = _norm_mod(x_ref[...], g_ref[...], m[:, D_MODEL:2 * D_MODEL], m[:, 0:D_MODEL])
    hb = h.astype(BF16)
    z = jnp.dot(hb, w_ref[...], preferred_element_type=F32)
    seg = seq_len // SUBLANES
    for s in range(x_ref.shape[0] // seq_len):
        for k in range(SUBLANES):
            t0 = s * seq_len + k * seg
            for c in range(D_LRU // LANES):
                xb_ref[c, pl.ds(s * seq_len + k, seg, stride=SUBLANES), :] = (
                    z[t0:t0 + seg, c * LANES:(c + 1) * LANES])
    gg_ref[...] = _gelu(z[:, D_LRU:]).astype(BF16)
    us = jnp.dot(hb, ws_ref[...], preferred_element_type=F32)
    for k in range(S5_BLK):
        us_ref[k] = us[:, k * LANES:(k + 1) * LANES]


def _inproj_call(x2d, mod3, g1, w_main, w_s5, mod_row, tm, seq_len):
    n = x2d.shape[0]
    return pl.pallas_call(
        functools.partial(_inproj_body, seq_len=seq_len), grid=(n // tm,),
        in_specs=[pl.BlockSpec((tm, D_MODEL), lambda i: (i, 0)),
                  pl.BlockSpec((1, 1, 6 * D_MODEL), lambda i: (mod_row(i), 0, 0)),
                  pl.BlockSpec((1, D_MODEL), lambda i: (0, 0)),
                  pl.BlockSpec((D_MODEL, 2 * D_LRU), lambda i: (0, 0)),
                  pl.BlockSpec((D_MODEL, D_S5), lambda i: (0, 0))],
        out_specs=[pl.BlockSpec((D_LRU // LANES, tm, LANES), lambda i: (0, i, 0)),
                   pl.BlockSpec((tm, D_LRU), lambda i: (i, 0)),
                   pl.BlockSpec((S5_BLK, tm, LANES), lambda i: (0, i, 0))],
        out_shape=[jax.ShapeDtypeStruct((D_LRU // LANES, n, LANES), F32),
                   jax.ShapeDtypeStruct((n, D_LRU), BF16),
                   jax.ShapeDtypeStruct((S5_BLK, n, LANES), F32)],
        compiler_params=_cp(("parallel",)), name="inproj",
    )(x2d, mod3, g1, w_main, w_s5)


def _lru_body(xb_ref, gg_ref, cw_ref, cb_ref, wg_ref, ba_ref, bx_ref, lam_ref, h0_ref,
              ya_ref, fin_ref, xpad, a_f, b_f, a_b, b_b, hs, *, T):
    seg = T // SUBLANES
    ch = 256
    nch = T // ch
    n_slab = D_LRU // LANES
    pad = 2 * SUBLANES
    sub = lax.broadcasted_iota(I32, (SUBLANES, D_LRU), 0)

    def rows_of(r0, n):
        return jnp.concatenate([xb_ref[c, r0:r0 + n, :] for c in range(n_slab)], axis=1)

    def from_prev_segment(tile):
        return jnp.where(sub >= 1, pltpu.roll(tile, 1, 0), 0.0)

    def from_next_segment(tile):
        return jnp.where(sub < SUBLANES - 1, pltpu.roll(tile, SUBLANES - 1, 0), 0.0)

    xpad[0:SUBLANES, :] = from_prev_segment(rows_of((seg - 2) * SUBLANES, SUBLANES))
    xpad[SUBLANES:pad, :] = from_prev_segment(rows_of((seg - 1) * SUBLANES, SUBLANES))
    xpad[pad + T:pad + T + SUBLANES, :] = from_next_segment(rows_of(0, SUBLANES))
    for c in range(nch):
        xpad[pad + c * ch:pad + (c + 1) * ch, :] = rows_of(c * ch, ch)

    nl = -lam_ref[...]
    softplus = jnp.maximum(nl, 0.0) + jnp.log1p(jnp.exp(-jnp.abs(nl)))
    decay = (-LRU_C) * softplus
    scr = ((a_f, b_f), (a_b, b_b))

    for c in range(nch):
        t0 = c * ch
        xc = cb_ref[...] + sum(
            xpad[t0 + k * SUBLANES:t0 + k * SUBLANES + ch, :] * cw_ref[k:k + 1, :]
            for k in range(CONV_W))
        xcb = xc.astype(BF16)
        pre = [jnp.dot(xcb[:, kb * 256:(kb + 1) * 256], wg_ref[kb], preferred_element_type=F32)
               for kb in range(4)]
        for d in range(2):
            ra = jnp.concatenate([p[:, d * 256:(d + 1) * 256] for p in pre], axis=1)
            gx = jnp.concatenate([p[:, 512 + d * 256:512 + (d + 1) * 256] for p in pre], axis=1)
            r = _sigmoid(ra + ba_ref[d:d + 1, :])
            gi = _sigmoid(gx + bx_ref[d:d + 1, :])
            a = jnp.exp(r * decay[d:d + 1, :])
            scr[d][0][t0:t0 + ch, :] = a
            scr[d][1][t0:t0 + ch, :] = jnp.exp(0.5 * jnp.log(1.0 - a * a)) * gi * xc

    def local_scan(a_ref, u_ref, reverse):
        def step(i, carry):
            h, p = carry
            r0 = pl.multiple_of((seg - 1 - i if reverse else i) * SUBLANES, SUBLANES)
            a = a_ref[pl.ds(r0, SUBLANES), :]
            h = a * h + u_ref[pl.ds(r0, SUBLANES), :]
            p = a * p
            u_ref[pl.ds(r0, SUBLANES), :] = h
            a_ref[pl.ds(r0, SUBLANES), :] = p
            return h, p
        init = (jnp.zeros((SUBLANES, D_LRU), F32), jnp.ones((SUBLANES, D_LRU), F32))
        return lax.fori_loop(0, seg, step, init, unroll=2)

    def carry_in(h_end, p_end, h0_row, reverse):
        edge = sub == (SUBLANES - 1 if reverse else 0)
        shift = SUBLANES - 1 if reverse else 1
        c = jnp.broadcast_to(h0_row, (SUBLANES, D_LRU))
        for _ in range(SUBLANES - 1):
            c = jnp.where(edge, h0_row, pltpu.roll(h_end + p_end * c, shift, 0))
        return c

    hf_end, pf_end = local_scan(a_f, b_f, False)
    hb_end, pb_end = local_scan(a_b, b_b, True)
    c_f = carry_in(hf_end, pf_end, h0_ref[0, 0:1, :], False)
    c_b = carry_in(hb_end, pb_end, h0_ref[0, 1:2, :], True)
    fin_ref[0, 0:1, :] = (hf_end + pf_end * c_f)[SUBLANES - 1:SUBLANES, :]
    fin_ref[0, 1:2, :] = (hb_end + pb_end * c_b)[0:1, :]

    for c in range(nch):
        sl = slice(c * ch, (c + 1) * ch)
        tile3 = (ch // SUBLANES, SUBLANES, D_LRU)
        hsum = ((b_f[sl, :].reshape(tile3) + a_f[sl, :].reshape(tile3) * c_f[None])
                + (b_b[sl, :].reshape(tile3) + a_b[sl, :].reshape(tile3) * c_b[None])).reshape(ch, D_LRU)
        for j in range(n_slab):
            hs[j, sl, :] = hsum[:, j * LANES:(j + 1) * LANES]

    for k in range(SUBLANES):
        rows = slice(k * seg, (k + 1) * seg)
        h_seg = jnp.concatenate([hs[j, pl.ds(k, seg, stride=SUBLANES), :] for j in range(n_slab)], axis=1)
        ya_ref[0, rows, :] = (gg_ref[0, rows, :].astype(F32) * h_seg).astype(BF16)


def _lru_call(xb8, gg, conv_w, conv_b, wg, ba, bx, lam, h0, n_seq, T):
    gg3 = gg.reshape(n_seq, T, D_LRU)
    n_slab = D_LRU // LANES
    full = lambda shape: pl.BlockSpec(shape, lambda i: (0,) * len(shape))
    seq = lambda shape: pl.BlockSpec(shape, lambda i: (i,) + (0,) * (len(shape) - 1))
    ya, fin = pl.pallas_call(
        functools.partial(_lru_body, T=T),
        grid=(n_seq,),
        in_specs=[pl.BlockSpec((n_slab, T, LANES), lambda i: (0, i, 0)), seq((1, T, D_LRU)),
                  full((CONV_W, D_LRU)), full((1, D_LRU)),
                  full((4, 256, 1024)), full((2, D_LRU)), full((2, D_LRU)), full((2, D_LRU)),
                  seq((1, 2, D_LRU))],
        out_specs=[seq((1, T, D_LRU)), seq((1, 2, D_LRU))],
        out_shape=[jax.ShapeDtypeStruct((n_seq, T, D_LRU), BF16),
                   jax.ShapeDtypeStruct((n_seq, 2, D_LRU), F32)],
        scratch_shapes=[pltpu.VMEM((T + 3 * SUBLANES, D_LRU), F32)] + [pltpu.VMEM((T, D_LRU), F32)] * 4
                       + [pltpu.VMEM((n_slab, T, LANES), F32)],
        compiler_params=_cp(("parallel",)), name="lru",
    )(xb8, gg3, conv_w, conv_b, wg, ba, bx, lam, h0)
    return ya.reshape(n_seq * T, D_LRU), fin


def _s5_body(u_ref, wts_ref, wc_ref, ap_ref, h0_ref, y_ref, fin_ref, s_scr, hp_scr, slab,
             *, rt, n_c, with_h0, col_major):
    q = 512
    m = S5_SEG
    grp = SUBLANES * m
    n_grp = rt // grp
    seg_per_seq = n_c // m
    seq_per_grp = SUBLANES // seg_per_seq
    seq_tok = n_c * S5_L
    halves = seq_tok // (GRID_W * S5_L)
    sub = lax.broadcasted_iota(I32, (SUBLANES, q), 0)

    def pieces(g, i):
        out = []
        for v in range(SUBLANES):
            s, sq = divmod(v, seg_per_seq)
            tok0 = (g * seq_per_grp + s) * seq_tok
            if col_major:
                for h in range(halves):
                    n = m // halves
                    out.append((pl.ds(g * grp + h * SUBLANES + v, n, stride=halves * SUBLANES),
                                pl.ds(tok0 + (h * S5_L + i) * GRID_W + sq * n, n)))
            else:
                out.append((pl.ds(g * grp + v, m, stride=SUBLANES),
                            pl.ds(tok0 + sq * m * S5_L + i, m, stride=S5_L)))
        return out

    for i in range(S5_L):
        for g in range(n_grp):
            for rows, toks in pieces(g, i):
                slab[i, rows, :] = u_ref[0, toks, :]
    ub = jnp.concatenate([slab[i] for i in range(S5_L)], axis=1).astype(BF16)
    r1 = jnp.dot(ub, wts_ref[0], preferred_element_type=F32)
    y_in = r1[:, :S5_L * LANES]
    s_scr[...] = r1[:, S5_L * LANES:]

    def cmul(ar, ai, xr, xi):
        return ar * xr - ai * xi, ar * xi + ai * xr

    def local_scan(base, col0, reverse):
        ar, ai = ap_ref[0, 0:1, col0:col0 + q], ap_ref[0, 0:1, col0 + q:col0 + 2 * q]

        def step(k, carry):
            hr, hi = carry
            r0 = pl.multiple_of(base + (m - 1 - k if reverse else k) * SUBLANES, SUBLANES)
            pr, pi = cmul(ar, ai, hr, hi)
            hr = pr + s_scr[pl.ds(r0, SUBLANES), col0:col0 + q]
            hi = pi + s_scr[pl.ds(r0, SUBLANES), col0 + q:col0 + 2 * q]
            s_scr[pl.ds(r0, SUBLANES), col0:col0 + q] = hr
            s_scr[pl.ds(r0, SUBLANES), col0 + q:col0 + 2 * q] = hi
            return hr, hi
        zero = jnp.zeros((SUBLANES, q), F32)
        return lax.fori_loop(0, m, step, (zero, zero), unroll=2)

    def chain(g, col0, reverse, h_end):
        base = g * grp
        first = (sub & (seg_per_seq - 1)) == (seg_per_seq - 1 if reverse else 0)
        h0r = jnp.zeros((SUBLANES, q), F32)
        h0i = jnp.zeros((SUBLANES, q), F32)
        if with_h0:
            for s in range(seq_per_grp):
                row = h0_ref[0, 0, g * seq_per_grp + s:g * seq_per_grp + s + 1, :]
                mine = (sub >> (seg_per_seq.bit_length() - 1)) == s
                h0r = jnp.where(mine, row[:, col0:col0 + q], h0r)
                h0i = jnp.where(mine, row[:, col0 + q:col0 + 2 * q], h0i)
        cr, ci = h0r, h0i
        if seg_per_seq > 1:
            er, ei = ap_ref[0, m - 1:m, col0:col0 + q], ap_ref[0, m - 1:m, col0 + q:col0 + 2 * q]
            shift = SUBLANES - 1 if reverse else 1
            for _ in range(seg_per_seq - 1):
                pr, pi = cmul(er, ei, cr, ci)
                cr = jnp.where(first, h0r, pltpu.roll(h_end[0] + pr, shift, 0))
                ci = jnp.where(first, h0i, pltpu.roll(h_end[1] + pi, shift, 0))
        if with_h0 or seg_per_seq > 1:
            def add(c, _):
                r0 = pl.multiple_of(base + c * SUBLANES, SUBLANES)
                k = m - 1 - c if reverse else c
                pr, pi = cmul(ap_ref[0, pl.ds(k, 1), col0:col0 + q],
                              ap_ref[0, pl.ds(k, 1), col0 + q:col0 + 2 * q], cr, ci)
                s_scr[pl.ds(r0, SUBLANES), col0:col0 + q] += pr
                s_scr[pl.ds(r0, SUBLANES), col0 + q:col0 + 2 * q] += pi
                return 0
            lax.fori_loop(0, m, add, 0, unroll=2)
        return cr, ci

    for g in range(n_grp):
        base = g * grp
        body = slice(base + SUBLANES, base + grp)
        head = slice(base, base + SUBLANES)
        prev = slice(base, base + grp - SUBLANES)
        tail = slice(base + grp - SUBLANES, base + grp)
        cr, ci = chain(g, 0, False, local_scan(base, 0, False))
        hp_scr[body, 0:2 * q] = s_scr[prev, 0:2 * q]
        hp_scr[head, 0:q] = cr
        hp_scr[head, q:2 * q] = ci
        cr, ci = chain(g, 2 * q, True, local_scan(base, 2 * q, True))
        hp_scr[prev, 2 * q:4 * q] = s_scr[body, 2 * q:4 * q]
        hp_scr[tail, 2 * q:3 * q] = cr
        hp_scr[tail, 3 * q:4 * q] = ci
        if seg_per_seq == 1:
            rows = slice(g * SUBLANES, (g + 1) * SUBLANES)
            fin_ref[0, 0, rows, 0:2 * q] = s_scr[tail, 0:2 * q]
            fin_ref[0, 0, rows, 2 * q:4 * q] = s_scr[head, 2 * q:4 * q]
    if seg_per_seq > 1:
        fin_ref[...] = jnp.zeros(fin_ref.shape, F32)

    y = y_in + jnp.dot(hp_scr[...].astype(BF16), wc_ref[0], preferred_element_type=F32)
    for i in range(S5_L):
        slab[i] = y[:, i * LANES:(i + 1) * LANES]
        for g in range(n_grp):
            for rows, toks in pieces(g, i):
                y_ref[0, toks, :] = slab[i, rows, :]


def _s5_call(us4, wts, wc, apow, h0, n_seq, T, with_h0, col_major):
    n_c = T // S5_L
    rows = n_seq * n_c
    rt = 512
    nseq_t = rt // n_c
    h0 = h0.reshape(S5_BLK, n_seq // nseq_t, nseq_t, S5_SW)
    y4, fin = pl.pallas_call(
        functools.partial(_s5_body, rt=rt, n_c=n_c, with_h0=with_h0, col_major=col_major),
        grid=(S5_BLK, rows // rt),
        in_specs=[pl.BlockSpec((1, rt * S5_L, LANES), lambda b, r: (b, r, 0)),
                  pl.BlockSpec((1, S5_L * LANES, S5_L * LANES + S5_SW), lambda b, r: (b, 0, 0)),
                  pl.BlockSpec((1, S5_SW, S5_L * LANES), lambda b, r: (b, 0, 0)),
                  pl.BlockSpec((1, S5_SEG, S5_SW), lambda b, r: (b, 0, 0)),
                  pl.BlockSpec((1, 1, nseq_t, S5_SW), lambda b, r: (b, r, 0, 0))],
        out_specs=[pl.BlockSpec((1, rt * S5_L, LANES), lambda b, r: (b, r, 0)),
                   pl.BlockSpec((1, 1, nseq_t, S5_SW), lambda b, r: (b, r, 0, 0))],
        out_shape=[jax.ShapeDtypeStruct((S5_BLK, n_seq * T, LANES), F32),
                   jax.ShapeDtypeStruct((S5_BLK, n_seq // nseq_t, nseq_t, S5_SW), F32)],
        scratch_shapes=[pltpu.VMEM((rt, S5_SW), F32), pltpu.VMEM((rt, S5_SW), F32),
                        pltpu.VMEM((S5_L, rt, LANES), F32)],
        compiler_params=_cp(("parallel", "parallel")), name="s5",
    )(us4, wts, wc, apow, h0)
    return y4, fin.reshape(S5_BLK, n_seq, S5_SW)


def _s5_disc(lr, li, st):
    mag = jnp.exp(lr * st)
    ang = li * st
    ar, ai = mag * jnp.cos(ang), mag * jnp.sin(ang)
    den = lr * lr + li * li
    fr = ((ar - 1.0) * lr + ai * li) / den
    fi = (ai * lr - (ar - 1.0) * li) / den
    return ar, ai, fr, fi


def _s5_powers(ar, ai, n):
    out = [(jnp.ones_like(ar), jnp.zeros_like(ar))]
    for _ in range(n):
        pr, pi = out[-1]
        out.append((pr * ar - pi * ai, pr * ai + pi * ar))
    return out


def _s5_prep_body(rowp_ref, colp_ref, flatp_ref, bt_ref, cm_ref, wts_ref, wc_ref, ap_ref):
    L, N, P = S5_L, S5_N, S5_P
    w = L * LANES
    div = lambda x, d: x >> (d.bit_length() - 1)
    mod = lambda x, d: x & (d - 1)
    one = lambda m: jnp.where(m, 1.0, 0.0).astype(BF16)
    r1 = lax.broadcasted_iota(I32, (LANES, LANES), 0)
    c1 = lax.broadcasted_iota(I32, (LANES, LANES), 1)
    first_half = c1 < N
    sgn = jnp.where(first_half, -1.0, 1.0)

    e_maps, taps = [], []
    same_group = div(r1, P) == div(c1, P)
    fold = one(mod(r1, P) == mod(c1, P))
    for d in range(2):
        ar, ai, fr, fi = _s5_disc(rowp_ref[0, d, 0], rowp_ref[1, d, 0], rowp_ref[2, d, 0])
        bx1 = bt_ref[d, 0]
        bx2 = pltpu.roll(bx1, N, 1)
        x1 = fr * bx1 + (fi * sgn) * bx2
        x2 = pltpu.roll(x1, N, 1)
        e_d, t_d = [], []
        for pr, pi in _s5_powers(ar, ai, L):
            e_k = pr * x1 + (pi * sgn) * x2
            e_d.append(e_k)
            full = _dot3(e_k, cm_ref[d, 0])
            kept = jnp.where(same_group, full, 0.0)
            hi = kept.astype(BF16)
            mid = (kept - hi.astype(F32)).astype(BF16)
            lo = (kept - hi.astype(F32) - mid.astype(F32)).astype(BF16)
            t_d.append(sum(jnp.dot(part, fold, preferred_element_type=F32) for part in (hi, mid, lo)))
        e_maps.append(e_d)
        taps.append(t_d)

    r = lax.broadcasted_iota(I32, (LANES, w), 0)
    c = lax.broadcasted_iota(I32, (LANES, w), 1)
    t_tap = one((div(r, P) == div(c, LANES)) & (mod(r, P) == mod(c, P)))
    m_tap = div(r, P) == div(mod(c, LANES), P)
    t_st = one((div(r, N) == div(c, 8 * N)) & (mod(r, N) == mod(c, N)))
    m_st = div(r, P) == div(mod(c, 8 * N), N)
    col_blk = div(c1, P)
    for j in range(L):
        rows = slice(j * LANES, (j + 1) * LANES)
        toe = jnp.zeros((LANES, LANES), F32)
        for k in range(L):
            toe = (toe + jnp.where(col_blk == j + k, taps[0][k], 0.0)
                   + jnp.where(col_blk == j - k, taps[1][k], 0.0))
        tap = jnp.dot(toe.astype(BF16), t_tap, preferred_element_type=F32)
        wts_ref[0, rows, 0:w] = jnp.where(m_tap, tap, 0.0).astype(BF16)
        for k, e_j in enumerate((e_maps[0][L - 1 - j], e_maps[1][j])):
            st = jnp.dot(e_j.astype(BF16), t_st, preferred_element_type=F32)
            wts_ref[0, rows, w + k * w:w + (k + 1) * w] = jnp.where(m_st, st, 0.0).astype(BF16)

    rr = lax.broadcasted_iota(I32, (w, LANES), 0)
    cc = lax.broadcasted_iota(I32, (w, LANES), 1)
    t_row = one((div(rr, 8 * N) == div(cc, N)) & (mod(rr, N) == mod(cc, N)))
    r2 = lax.broadcasted_iota(I32, (w, w), 0)
    c2 = lax.broadcasted_iota(I32, (w, w), 1)
    m_row = div(mod(r2, 8 * N), N) == div(mod(c2, LANES), P)
    for d in range(2):
        ar, ai, _, _ = _s5_disc(colp_ref[0, d, 0], colp_ref[1, d, 0], colp_ref[2, d, 0])
        a1 = cm_ref[d, 0]
        swapped = pltpu.roll(a1, N, 0)
        a2 = jnp.where(r1 < N, swapped, -swapped)
        pw = _s5_powers(ar, ai, L)
        ks = [i + 1 for i in range(L)] if d == 0 else [L - i for i in range(L)]
        g = jnp.concatenate([a1 * pw[k][0] + a2 * pw[k][1] for k in ks], axis=1)
        corr = jnp.dot(t_row, g.astype(BF16), preferred_element_type=F32)
        wc_ref[0, d * w:(d + 1) * w, :] = jnp.where(m_row, corr, 0.0).astype(BF16)

    al = []
    for d in range(2):
        ar, ai, _, _ = _s5_disc(flatp_ref[0, d, 0], flatp_ref[1, d, 0], flatp_ref[2, d, 0])
        al.append(_s5_powers(ar, ai, L)[L])
    cur = al
    for k in range(S5_SEG):
        ap_ref[0, k:k + 1, :] = jnp.concatenate([cur[0][0], cur[0][1], cur[1][0], cur[1][1]], axis=1)
        cur = [(cr * br - ci * bi, cr * bi + ci * br) for (cr, ci), (br, bi) in zip(cur, al)]


def _s5_weights(lam_re, lam_im, log_step, b_re, b_im, c_re, c_im):
    L, G, N, P = S5_L, S5_G, S5_N, S5_P
    w = L * LANES
    step = jnp.broadcast_to(jnp.exp(log_step)[..., None], lam_re.shape)
    prm = jnp.stack([lam_re, lam_im, step])
    rowp = jnp.broadcast_to(prm[:, :, :, None, None, :], (3, 2, G, P, 2, N)).reshape(3, 2, S5_BLK, LANES, 2 * N)
    colp = jnp.broadcast_to(prm.transpose(0, 1, 3, 2)[:, :, None, :, :, None], (3, 2, 2, N, G, P))
    colp = colp.reshape(3, 2, 2 * N, S5_BLK, LANES).transpose(0, 1, 3, 2, 4)
    flatp = prm.reshape(3, 2, S5_BLK, 1, 8 * N)
    bt = jnp.stack([b_re, b_im], axis=3).transpose(0, 1, 4, 3, 2)
    bt = bt.reshape(2, S5_BLK, LANES, 2 * N)
    cm = jnp.stack([c_re, -c_im], axis=1).transpose(0, 1, 4, 2, 3)
    cm = cm.reshape(2, 2 * N, S5_BLK, LANES).transpose(0, 2, 1, 3)
    blk = lambda lead, a, b_: pl.BlockSpec(lead + (1, a, b_), lambda i: (0,) * len(lead) + (i, 0, 0))
    out = lambda a, b_: pl.BlockSpec((1, a, b_), lambda i: (i, 0, 0))
    return pl.pallas_call(
        _s5_prep_body, grid=(S5_BLK,),
        in_specs=[blk((3, 2), LANES, LANES), blk((3, 2), LANES, LANES), blk((3, 2), 1, 8 * N),
                  blk((2,), LANES, LANES), blk((2,), LANES, LANES)],
        out_specs=[out(w, w + S5_SW), out(S5_SW, w), out(S5_SEG, S5_SW)],
        out_shape=[jax.ShapeDtypeStruct((S5_BLK, w, w + S5_SW), BF16),
                   jax.ShapeDtypeStruct((S5_BLK, S5_SW, w), BF16),
                   jax.ShapeDtypeStruct((S5_BLK, S5_SEG, S5_SW), F32)],
        compiler_params=_cp(("parallel",)), name="s5_prep",
    )(rowp, colp, flatp, bt, cm)


def _mixout_body(xp_ref, xs_ref, mod_ref, g1_ref, g2_ref, wbg_ref, bbg_ref, yap_ref, yas_ref,
                 y4p_ref, y4s_ref, u4p_ref, u4s_ref, *rest, tiles_p):
    i = pl.program_id(0)

    @pl.when(i < tiles_p)
    def _():
        _mixout_tile(xp_ref, mod_ref, g1_ref, g2_ref, wbg_ref, bbg_ref, yap_ref, y4p_ref, u4p_ref, *rest)

    @pl.when(i >= tiles_p)
    def _():
        _mixout_tile(xs_ref, mod_ref, g1_ref, g2_ref, wbg_ref, bbg_ref, yas_ref, y4s_ref, u4s_ref, *rest)


def _mixout_tile(x_ref, mod_ref, g1_ref, g2_ref, wbg_ref, bbg_ref, ya_ref, y4_ref, u4_ref,
                 dsk_ref, wglu_ref, bglu_ref, wso_ref, wlo_ref, wo_ref, wrt_ref, brt_ref,
                 x1_ref, h2_ref, aff_ref):
    m = mod_ref[0]
    sh1, sc1, gt1 = m[:, 0:D_MODEL], m[:, D_MODEL:2 * D_MODEL], m[:, 2 * D_MODEL:3 * D_MODEL]
    sh2, sc2 = m[:, 3 * D_MODEL:4 * D_MODEL], m[:, 4 * D_MODEL:5 * D_MODEL]
    x = x_ref[...]
    h = _norm_mod(x, g1_ref[...], sc1, sh1)
    gates = _sigmoid(jnp.dot(h.astype(BF16), wbg_ref[...], preferred_element_type=F32) + bbg_ref[...])

    def assemble(ref):
        return jnp.concatenate([ref[k] for k in range(S5_BLK)], axis=1)

    ys = assemble(y4_ref) + dsk_ref[...] * assemble(u4_ref)
    v = _gelu(ys)
    ob = v * _sigmoid(jnp.dot(v.astype(BF16), wglu_ref[...], preferred_element_type=F32) + bglu_ref[...])
    merged = (gates[:, :D_MODEL] * jnp.dot(ya_ref[...], wlo_ref[...], preferred_element_type=F32)
              + gates[:, D_MODEL:] * jnp.dot(ob.astype(BF16), wso_ref[...], preferred_element_type=F32))
    mix = jnp.dot(merged.astype(BF16), wo_ref[...], preferred_element_type=F32)
    x1 = x + gt1 * mix
    x1_ref[...] = x1
    h2 = _norm_mod(x1, g2_ref[...], sc2, sh2)
    h2_ref[...] = h2
    logits = _dot3(wrt_ref[...], h2, dims=(((1,), (1,)), ((), ()))) + brt_ref[...]
    mx = jnp.max(logits, axis=0, keepdims=True)
    ex = jnp.exp(logits - mx)
    aff_ref[...] = ex / jnp.sum(ex, axis=0, keepdims=True)


def _mixout_call(xp, xs, mod3, g1, g2, wbg, bbg, ya_p, ya_s, y4_p, y4_s, u4_p, u4_s,
                 dsk, wglu, bglu, wso, wlo, wo, wrt, brt, mod_row, tm):
    n_p, n_s = xp.shape[0], xs.shape[0]
    tiles_p = n_p // tm
    n = n_p + n_s
    full = lambda shape: pl.BlockSpec(shape, lambda i: (0,) * len(shape))
    at_p = lambda i: jnp.minimum(i, tiles_p - 1)
    at_s = lambda i: jnp.maximum(i - tiles_p, 0)
    tok = lambda w, at: pl.BlockSpec((tm, w), lambda i: (at(i), 0))
    s5 = lambda at: pl.BlockSpec((S5_BLK, tm, LANES), lambda i: (0, at(i), 0))
    out = lambda w: pl.BlockSpec((tm, w), lambda i: (i, 0))
    return pl.pallas_call(
        functools.partial(_mixout_body, tiles_p=tiles_p),
        grid=(n // tm,),
        in_specs=[tok(D_MODEL, at_p), tok(D_MODEL, at_s),
                  pl.BlockSpec((1, 1, 6 * D_MODEL), lambda i: (mod_row(i), 0, 0)),
                  full((1, D_MODEL)), full((1, D_MODEL)), full((D_MODEL, 2 * D_MODEL)),
                  full((1, 2 * D_MODEL)), tok(D_LRU, at_p), tok(D_LRU, at_s),
                  s5(at_p), s5(at_s), s5(at_p), s5(at_s), full((1, D_S5)),
                  full((D_S5, D_S5)), full((1, D_S5)), full((D_S5, D_MODEL)), full((D_LRU, D_MODEL)),
                  full((D_MODEL, D_MODEL)), full((N_EXPERTS, D_MODEL)), full((N_EXPERTS, 1))],
        out_specs=[out(D_MODEL), out(D_MODEL), pl.BlockSpec((N_EXPERTS, tm), lambda i: (0, i))],
        out_shape=[jax.ShapeDtypeStruct((n, D_MODEL), F32), jax.ShapeDtypeStruct((n, D_MODEL), F32),
                   jax.ShapeDtypeStruct((N_EXPERTS, n), F32)],
        compiler_params=_cp(("arbitrary",)), name="mixout",
    )(xp, xs, mod3, g1, g2, wbg, bbg, ya_p, ya_s, y4_p, y4_s, u4_p, u4_s,
      dsk, wglu, bglu, wso, wlo, wo, wrt, brt)


def _select_body(aff_ref, gate_ref, pos_ref, offs_ref, *, n_tok, cap):
    aff = aff_ref[...]
    capf = float(cap)

    def bis(_, lh):
        lo, hi = lh
        mid = lo + ((hi - lo + 1) >> 1)
        cnt = jnp.sum(jnp.where(aff >= pltpu.bitcast(mid, F32), 1.0, 0.0), axis=1, keepdims=True)
        ok = cnt >= capf
        return jnp.where(ok, mid, lo), jnp.where(ok, hi, mid - 1)

    lo0 = jnp.zeros((N_EXPERTS, 1), I32)
    hi0 = jnp.full((N_EXPERTS, 1), 0x7F800000, I32)
    thr_bits, _ = lax.fori_loop(0, 31, bis, (lo0, hi0))
    thr = pltpu.bitcast(thr_bits, F32)
    need = capf - jnp.sum(jnp.where(aff > thr, 1.0, 0.0), axis=1, keepdims=True)

    r = lax.broadcasted_iota(I32, (TOK_TILE, TOK_TILE), 0)
    c = lax.broadcasted_iota(I32, (TOK_TILE, TOK_TILE), 1)
    tri = jnp.where(r < c, 1.0, 0.0).astype(BF16)
    lane = lax.broadcasted_iota(I32, (N_EXPERTS, LANES), 1)
    n_tiles = n_tok // TOK_TILE
    run_eq = jnp.zeros((N_EXPERTS, 1), F32)
    run_sel = jnp.zeros((N_EXPERTS, 1), F32)
    offs = jnp.zeros((N_EXPERTS, LANES), F32)
    for t in range(n_tiles):
        sl = slice(t * TOK_TILE, (t + 1) * TOK_TILE)
        aff_t = aff_ref[:, sl]
        is_eq = aff_t == thr
        eq_t = jnp.where(is_eq, 1.0, 0.0)
        rank_eq = jnp.dot(eq_t.astype(BF16), tri, preferred_element_type=F32) + run_eq
        cand = (aff_t > thr) | (is_eq & (rank_eq < need))
        pos_t = jnp.dot(jnp.where(cand, 1.0, 0.0).astype(BF16), tri, preferred_element_type=F32) + run_sel
        sel_t = cand & (pos_t < capf)
        sel_f = jnp.where(sel_t, 1.0, 0.0)
        offs = jnp.where(lane == t, run_sel, offs)
        gate_ref[:, sl] = jnp.where(sel_t, aff_t, 0.0)
        pos_ref[:, sl] = jnp.where(sel_t, pos_t, -1.0).astype(I32)
        run_eq = run_eq + jnp.sum(eq_t, axis=1, keepdims=True)
        run_sel = run_sel + jnp.sum(sel_f, axis=1, keepdims=True)
    offs = jnp.where(lane >= n_tiles, run_sel, offs)
    offs_ref[...] = offs.astype(I32)


def _select_call(aff_t, cap, n_tok):
    n_path = aff_t.shape[1] // n_tok
    per_p = lambda a, b: pl.BlockSpec((None, a, b), lambda p: (p, 0, 0))
    return pl.pallas_call(
        functools.partial(_select_body, n_tok=n_tok, cap=cap),
        grid=(n_path,),
        in_specs=[pl.BlockSpec((N_EXPERTS, n_tok), lambda p: (0, p))],
        out_specs=[per_p(N_EXPERTS, n_tok), per_p(N_EXPERTS, n_tok), per_p(N_EXPERTS, LANES)],
        out_shape=[jax.ShapeDtypeStruct((n_path, N_EXPERTS, n_tok), F32),
                   jax.ShapeDtypeStruct((n_path, N_EXPERTS, n_tok), I32),
                   jax.ShapeDtypeStruct((n_path, N_EXPERTS, LANES), I32)],
        compiler_params=_cp(("parallel",)), name="select",
    )(aff_t)


def _invert_body(offs_ref, pos_ref, gate_ref, idx_ref, gcol_ref, acc_i, acc_g, *, n_tiles, cap):
    path = pl.program_id(0)
    base = (path * N_EXPERTS + pl.program_id(1)) * LANES
    acc_i[...] = jnp.zeros(acc_i.shape, F32)
    acc_g[...] = jnp.zeros(acc_g.shape, F32)
    jrow = lax.broadcasted_iota(I32, (SLOT_CHUNK, LANES), 0)
    lane = lax.broadcasted_iota(I32, (1, LANES), 1)

    def per_tile(t, _):
        first = (offs_ref[base + t] >> 3) << 3
        prow = pos_ref[0, 0, pl.ds(t, 1), :]
        grow = gate_ref[0, 0, pl.ds(t, 1), :]
        tok = (lane + (t * TOK_TILE + 1)).astype(F32)

        def window(w, _):
            start = pl.multiple_of(first + w * SLOT_CHUNK, SUBLANES)
            rows = pl.ds(start, SLOT_CHUNK)
            hit = prow == (jrow + start)
            acc_i[rows, :] = jnp.maximum(acc_i[rows, :], jnp.where(hit, tok, 0.0))
            acc_g[rows, :] = jnp.maximum(acc_g[rows, :], jnp.where(hit, grow, 0.0))
            return 0

        n_win = (offs_ref[base + t + 1] - first + SLOT_CHUNK - 1) >> (SLOT_CHUNK.bit_length() - 1)
        lax.fori_loop(0, n_win, window, 0)
        return 0

    lax.fori_loop(0, n_tiles, per_tile, 0)
    hit_i = acc_i[0:cap, :]
    hi = hit_i.astype(BF16)
    mid = (hit_i - hi.astype(F32)).astype(BF16)
    lo = (hit_i - hi.astype(F32) - mid.astype(F32)).astype(BF16)
    ones = jnp.ones((SUBLANES, LANES), BF16)
    nt = (((1,), (1,)), ((), ()))
    tok1 = sum(lax.dot_general(ones, part, nt, preferred_element_type=F32) for part in (hi, mid, lo))
    idx_ref[0, 0] = tok1[0:1, :].astype(I32) - 1 + path * (n_tiles * TOK_TILE)
    gcol_ref[0, 0] = jnp.max(acc_g[0:cap, :], axis=1, keepdims=True)


def _invert_call(offs, pos, gate, cap):
    n_path, _, n_tok = pos.shape
    n_tiles = n_tok // TOK_TILE
    pos4 = pos.reshape(n_path, N_EXPERTS, n_tiles, TOK_TILE)
    gate4 = gate.reshape(n_path, N_EXPERTS, n_tiles, TOK_TILE)
    per_e = lambda a, b: pl.BlockSpec((1, 1, a, b), lambda p, e, o: (p, e, 0, 0))
    return pl.pallas_call(
        functools.partial(_invert_body, n_tiles=n_tiles, cap=cap),
        grid_spec=pltpu.PrefetchScalarGridSpec(
            num_scalar_prefetch=1, grid=(n_path, N_EXPERTS),
            in_specs=[per_e(n_tiles, TOK_TILE), per_e(n_tiles, TOK_TILE)],
            out_specs=[per_e(1, cap), per_e(cap, 1)],
            scratch_shapes=[pltpu.VMEM((cap + SLOT_CHUNK, LANES), F32)] * 2),
        out_shape=[jax.ShapeDtypeStruct((n_path, N_EXPERTS, 1, cap), I32),
                   jax.ShapeDtypeStruct((n_path, N_EXPERTS, cap, 1), F32)],
        compiler_params=_cp(("parallel", "parallel")), name="invert",
    )(offs.reshape(-1), pos4, gate4)


def _ffn_body(idx0_ref, idxn_ref, gcol_ref, wg_ref, wu_ref, wd_ref, h_hbm, out_ref,
              xe, wgb, wub, wdb, sem, *, cap):
    e = pl.program_id(0)
    p = pl.program_id(1)
    nxt = 1 - p
    step = e * 2 + p

    def issue_rows(idx_ref, s, j0, n):
        for j in range(j0, j0 + n):
            pltpu.make_async_copy(h_hbm.at[pl.ds(idx_ref[0, 0, 0, j], 1), :],
                                  xe.at[s, pl.ds(j, 1), :], sem.at[s]).start()

    @pl.when(step == 0)
    def _():
        def issue(j, _):
            pltpu.make_async_copy(h_hbm.at[pl.ds(idx0_ref[0, 0, 0, j], 1), :],
                                  xe.at[0, pl.ds(j, 1), :], sem.at[0]).start()
            return 0
        lax.fori_loop(0, cap, issue, 0, unroll=8)

    @pl.when(p == 0)
    def _():
        wgb[...] = wg_ref[0].astype(BF16)
        wub[...] = wu_ref[0].astype(BF16)
        wdb[...] = wd_ref[0].astype(BF16)

    pltpu.make_async_copy(h_hbm.at[pl.ds(0, cap), :], xe.at[p], sem.at[p]).wait()

    ch = 256
    n_ch = cap // ch
    per = -(-cap // (n_ch - 1))
    for c in range(n_ch):
        sl = slice(c * ch, (c + 1) * ch)
        xb = xe[p, sl, :].astype(BF16)
        issue_rows(idxn_ref, nxt, min(c * per, cap), min((c + 1) * per, cap) - min(c * per, cap))
        g = jnp.dot(xb, wgb[...], preferred_element_type=F32)
        u = jnp.dot(xb, wub[...], preferred_element_type=F32)
        hid = (g * _sigmoid(g)) * u
        ye = jnp.dot(hid.astype(BF16), wdb[...], preferred_element_type=F32)
        out_ref[0, 0, sl, :] = ye * gcol_ref[0, 0, sl, :]

    @pl.when(step == 2 * N_EXPERTS - 1)
    def _():
        pltpu.make_async_copy(h_hbm.at[pl.ds(0, cap), :], xe.at[nxt], sem.at[nxt]).wait()


def _ffn_call(idx, gcol, w_gate, w_up, w_down, h2, cap):
    d_e = w_gate.shape[2]
    wspec = lambda a, b: pl.BlockSpec((1, a, b), lambda e, p: (e, 0, 0))
    return pl.pallas_call(
        functools.partial(_ffn_body, cap=cap),
        grid=(N_EXPERTS, 2),
        in_specs=[pl.BlockSpec((1, 1, 1, cap), lambda e, p: (0, 0, 0, 0), memory_space=pltpu.SMEM),
                  pl.BlockSpec((1, 1, 1, cap),
                               lambda e, p: (1 - p, jnp.minimum(e + p, N_EXPERTS - 1), 0, 0),
                               memory_space=pltpu.SMEM),
                  pl.BlockSpec((1, 1, cap, 1), lambda e, p: (p, e, 0, 0)),
                  wspec(D_MODEL, d_e), wspec(D_MODEL, d_e), wspec(d_e, D_MODEL),
                  pl.BlockSpec(memory_space=pl.ANY)],
        out_specs=pl.BlockSpec((1, 1, cap, D_MODEL), lambda e, p: (p, e, 0, 0)),
        out_shape=jax.ShapeDtypeStruct((2, N_EXPERTS, cap, D_MODEL), F32),
        scratch_shapes=[pltpu.VMEM((2, cap, D_MODEL), F32), pltpu.VMEM((D_MODEL, d_e), BF16),
                        pltpu.VMEM((D_MODEL, d_e), BF16), pltpu.VMEM((d_e, D_MODEL), BF16),
                        pltpu.SemaphoreType.DMA((2,))],
        compiler_params=_cp(("arbitrary", "arbitrary")), name="ffn",
    )(idx, idx, gcol, w_gate, w_up, w_down, h2)


def _combine_body(offs_ref, x1_ref, mod_ref, gf_ref, pos_ref, yb_hbm, y_ref, wins, acc_ref, sem,
                  *, cap, path):
    t = pl.program_id(0)
    n_t = pl.num_programs(0)
    slot = t % 2
    row0 = path * N_EXPERTS * LANES
    per = COMBINE_TILE // TOK_TILE

    def first_slot(tt, e):
        return offs_ref[row0 + e * LANES + tt * per]

    def kmax_of(tt):
        k = 0
        for e in range(N_EXPERTS):
            k = jnp.maximum(k, first_slot(tt + 1, e) - first_slot(tt, e))
        return k

    def geometry(tt, e, w, win):
        fetch = win + SUBLANES
        lo = first_slot(tt, e) + w * win
        start = pl.multiple_of(jnp.minimum((lo >> 3) << 3, cap - fetch), SUBLANES)
        return lo, start, fetch

    def issue(tt, w, s, win):
        for e in range(N_EXPERTS):
            _, start, fetch = geometry(tt, e, w, win)
            pltpu.make_async_copy(yb_hbm.at[path, e, pl.ds(start, fetch), :],
                                  wins.at[s, pl.ds(e * fetch, fetch), :], sem.at[s]).start()

    def drain(s, win):
        rows = N_EXPERTS * (win + SUBLANES)
        pltpu.make_async_copy(wins.at[1 - s, pl.ds(0, rows), :], wins.at[s, pl.ds(0, rows), :],
                              sem.at[s]).wait()

    def expand(w, s, win):
        sub = lax.broadcasted_iota(I32, (win + SUBLANES, COMBINE_TILE), 0)
        rows = []
        for e in range(N_EXPERTS):
            lo, start, _ = geometry(t, e, w, win)
            pos = pos_ref[e:e + 1, :]
            in_round = jnp.logical_and(pos >= lo, pos < lo + win)
            rows.append(jnp.where(jnp.logical_and(pos - start == sub, in_round), 1.0, 0.0))
        onehot = jnp.concatenate(rows, axis=0).T.astype(BF16)
        data = wins[s, 0:N_EXPERTS * (win + SUBLANES), :]
        hi = data.astype(BF16)
        lo_part = (data - hi.astype(F32)).astype(BF16)
        return (jnp.dot(onehot, hi, preferred_element_type=F32)
                + jnp.dot(onehot, lo_part, preferred_element_type=F32))

    def issue_first(tt, s):
        wide = kmax_of(tt) > COMBINE_WIN

        @pl.when(wide)
        def _():
            issue(tt, 0, s, COMBINE_WIN_WIDE)

        @pl.when(jnp.logical_not(wide))
        def _():
            issue(tt, 0, s, COMBINE_WIN)

    @pl.when(t == 0)
    def _():
        issue_first(0, 0)

    @pl.when(t + 1 < n_t)
    def _():
        issue_first(t + 1, 1 - slot)

    kmax = kmax_of(t)

    @pl.when(kmax > COMBINE_WIN)
    def _():
        drain(slot, COMBINE_WIN_WIDE)
        acc_ref[...] = expand(0, slot, COMBINE_WIN_WIDE)

    @pl.when(kmax <= COMBINE_WIN)
    def _():
        drain(slot, COMBINE_WIN)
        acc_ref[...] = expand(0, slot, COMBINE_WIN)

    def more(w, _):
        issue(t, w, slot, COMBINE_WIN_WIDE)
        drain(slot, COMBINE_WIN_WIDE)
        acc_ref[...] += expand(w, slot, COMBINE_WIN_WIDE)
        return 0
    lax.fori_loop(1, (kmax + COMBINE_WIN_WIDE - 1) >> (COMBINE_WIN_WIDE.bit_length() - 1), more, 0)
    acc = acc_ref[...]

    gt2 = mod_ref[0][:, 5 * D_MODEL:6 * D_MODEL]
    x2 = x1_ref[...] + gt2 * acc
    ms = jnp.mean(x2 * x2, axis=-1, keepdims=True)
    y_ref[...] = x2 * lax.rsqrt(ms + EPS) * gf_ref[...]


def _combine_call(offs, pos, x1, mod3, gf, ybuf, mod_row, cap, path, n):
    first = path * n // COMBINE_TILE
    return pl.pallas_call(
        functools.partial(_combine_body, cap=cap, path=path),
        grid_spec=pltpu.PrefetchScalarGridSpec(
            num_scalar_prefetch=1, grid=(n // COMBINE_TILE,),
            in_specs=[pl.BlockSpec((COMBINE_TILE, D_MODEL), lambda i, o: (first + i, 0)),
                      pl.BlockSpec((1, 1, 6 * D_MODEL), lambda i, o: (mod_row(i), 0, 0)),
                      pl.BlockSpec((1, D_MODEL), lambda i, o: (0, 0)),
                      pl.BlockSpec((None, N_EXPERTS, COMBINE_TILE), lambda i, o: (path, 0, i)),
                      pl.BlockSpec(memory_space=pl.ANY)],
            out_specs=pl.BlockSpec((COMBINE_TILE, D_MODEL), lambda i, o: (i, 0)),
            scratch_shapes=[pltpu.VMEM((2, N_EXPERTS * (COMBINE_WIN_WIDE + SUBLANES), D_MODEL), F32),
                            pltpu.VMEM((COMBINE_TILE, D_MODEL), F32),
                            pltpu.SemaphoreType.DMA((2,))]),
        out_shape=jax.ShapeDtypeStruct((n, D_MODEL), F32),
        compiler_params=_cp(("arbitrary",)), name="combine",
    )(offs.reshape(-1), x1, mod3, gf, pos, ybuf)


def _lru_gate_weights(wa, wx):
    eye = jnp.eye(4, dtype=F32)

    def bd(w):
        w5 = w.reshape(4, 4, LRU_HEAD_DIM, LRU_HEAD_DIM)
        return jnp.einsum('khij,hg->khigj', w5, eye).reshape(4, 256, 256)

    return jnp.concatenate([bd(wa[0]), bd(wa[1]), bd(wx[0]), bd(wx[1])], axis=2).astype(BF16)


def kernel(x_prompt, x_sample, state_lru, state_s5_re, state_s5_im, c, c_ctx, w_mod, b_mod, g_norm1, g_norm2, w_in, conv_w, conv_b, lru_wa, lru_ba, lru_wx, lru_bx, lru_lambda, s5_lambda_re, s5_lambda_im, s5_log_step, s5_b_re, s5_b_im, s5_c_re, s5_c_im, s5_d, s5_w_glu, s5_b_glu, w_lru_out, w_s5_out, w_branch_gate, b_branch_gate, w_o, w_router, b_router, w_e_gate, w_e_up, w_e_down, g_final):
    bp, tp, _ = x_prompt.shape
    bs, ts, _ = x_sample.shape
    n_p, n_s = bp * tp, bs * ts
    l = 0

    c_all = jnp.zeros((16, D_MODEL), F32).at[0].set(c_ctx).at[1:1 + bs].set(c)
    mod3 = _mod_call(c_all, w_mod[l], b_mod[l][None, :]).reshape(16, 1, 6 * D_MODEL)

    w_in_b = w_in[l].astype(BF16)
    w_main, w_s5in = w_in_b[:, :2 * D_LRU], w_in_b[:, 2 * D_LRU:]
    g1 = g_norm1[l][None, :]
    g2 = g_norm2[l][None, :]
    wg = _lru_gate_weights(lru_wa[l], lru_wx[l])
    wts, wc, apow = _s5_weights(s5_lambda_re[l], s5_lambda_im[l], s5_log_step[l],
                                s5_b_re[l], s5_b_im[l], s5_c_re[l], s5_c_im[l])
    wbg = w_branch_gate[l].astype(BF16)
    bbg = b_branch_gate[l][None, :]
    wglu = s5_w_glu[l].astype(BF16)
    wso = w_s5_out[l].astype(BF16)
    wlo = w_lru_out[l].astype(BF16)
    wo = w_o[l].astype(BF16)
    wrt = w_router[l].T
    brt = b_router[l][:, None]

    xp2 = x_prompt.reshape(n_p, D_MODEL)
    xs2 = x_sample.reshape(n_s, D_MODEL)
    ctx_row = lambda i: 0
    tm_in = 1024
    lat_row_in = lambda i: 1 + i // (ts // max(tm_in, ts))

    def s5_state(re, im):
        def part(a, d):
            return a[:, d].reshape(-1, S5_BLK, 8 * S5_N).transpose(1, 0, 2)
        return jnp.concatenate([part(re, 0), part(im, 0), part(re, 1), part(im, 1)], axis=2)

    xb_p, gg_p, us4_p = _inproj_call(xp2, mod3, g1, w_main, w_s5in, ctx_row, max(tm_in, tp), tp)
    ya_p, lru_fin = _lru_call(xb_p, gg_p, conv_w[l], conv_b[l][None, :], wg, lru_ba[l], lru_bx[l],
                              lru_lambda[l], jnp.zeros((bp, 2, D_LRU), F32), bp, tp)
    y4_p, s5_fin = _s5_call(us4_p, wts, wc, apow, jnp.zeros((S5_BLK, bp, S5_SW), F32), bp, tp,
                            with_h0=False, col_major=False)
    xb_s, gg_s, us4_s = _inproj_call(xs2, mod3, g1, w_main, w_s5in, lat_row_in, max(tm_in, ts), ts)
    ya_s, _ = _lru_call(xb_s, gg_s, conv_w[l], conv_b[l][None, :], wg, lru_ba[l], lru_bx[l],
                        lru_lambda[l], state_lru[:, l], bs, ts)
    y4_s, _ = _s5_call(us4_s, wts, wc, apow, s5_state(state_s5_re[:, l], state_s5_im[:, l]), bs, ts,
                       with_h0=True, col_major=True)

    tm_mo = 512
    tiles_p = n_p // tm_mo
    x1, h2, aff = _mixout_call(
        xp2, xs2, mod3, g1, g2, wbg, bbg, ya_p, ya_s, y4_p, y4_s, us4_p, us4_s,
        s5_d[l][None, :], wglu, s5_b_glu[l][None, :], wso, wlo, wo, wrt, brt,
        mod_row=lambda i: jnp.where(i < tiles_p, 0, 1 + (i - tiles_p) // (ts // tm_mo)), tm=tm_mo)

    assert n_p == n_s
    cap = (CAPACITY_FACTOR * n_p) // N_EXPERTS
    gate, pos, offs = _select_call(aff, cap, n_p)
    idx, gcol = _invert_call(offs, pos, gate, cap)
    ybuf = _ffn_call(idx, gcol, w_e_gate[l], w_e_up[l], w_e_down[l], h2, cap)
    gf = g_final[None, :]
    y_p = _combine_call(offs, pos, x1, mod3, gf, ybuf, ctx_row, cap, 0, n_p)
    y_s = _combine_call(offs, pos, x1, mod3, gf, ybuf, lambda i: 1 + i // (ts // COMBINE_TILE), cap, 1, n_s)

    new_lru = lru_fin[:, None]
    sf = s5_fin.transpose(1, 0, 2).reshape(bp, S5_BLK, 2, 2, 8, S5_N)
    sf = sf.transpose(0, 2, 3, 1, 4, 5).reshape(bp, 2, 2, S5_G, S5_N)
    new_s5r = sf[:, :, 0][:, None]
    new_s5i = sf[:, :, 1][:, None]
    return (y_p.reshape(bp, tp, D_MODEL), y_s.reshape(bs, ts, D_MODEL), new_lru, new_s5r, new_s5i)
```

```python
import functools
import math

import jax
import jax.numpy as jnp
from jax import lax
from jax.experimental import pallas as pl
from jax.experimental.pallas import tpu as pltpu

F32 = jnp.float32
BF16 = jnp.bfloat16
I32 = jnp.int32

D_MODEL = 1024
D_LRU = 1024
LRU_HEADS = 16
LRU_HEAD_DIM = 64
LRU_C = 8.0
CONV_W = 4
D_S5 = 512
S5_P = 16
S5_G = 32
S5_N = 64
GRID_W = 64
N_EXPERTS = 16
CAPACITY_FACTOR = 2
EPS = 1e-6

LANES = 128
SUBLANES = 8
MXU_DIM = 256
ROW_CHUNK = 256
S5_L = 8
S5_BLK = 4
S5_SW = 4 * (LANES // S5_P) * S5_N
S5_SEG = 32
TOK_TILE = 128
SLOT_CHUNK = 32
COMBINE_TILE = 256
COMBINE_WIN = 64
COMBINE_WIN_WIDE = 128
VMEM_LIMIT = 56 * 1024 * 1024


def _cp(sem, vmem=VMEM_LIMIT):
    return pltpu.CompilerParams(dimension_semantics=sem, vmem_limit_bytes=vmem)


def _split2(a):
    hi = a.astype(BF16)
    lo = (a - hi.astype(F32)).astype(BF16)
    return hi, lo


def _dot3(a, b, dims=(((1,), (0,)), ((), ()))):
    ah, al = _split2(a)
    bh, bl = _split2(b)
    d = functools.partial(lax.dot_general, dimension_numbers=dims, preferred_element_type=F32)
    return d(ah, bh) + (d(al, bh) + d(ah, bl))


def _sigmoid(x):
    return 0.5 * jnp.tanh(0.5 * x) + 0.5


def _gelu(x):
    c = math.sqrt(2.0 / math.pi)
    half = 0.5 * x
    return half + half * jnp.tanh(x * (c + (c * 0.044715) * (x * x)))


def _norm_mod(x, g, scale, shift):
    ms = jnp.mean(x * x, axis=-1, keepdims=True)
    return (x * lax.rsqrt(ms + EPS)) * (g * (1.0 + scale)) + shift


def _mod_body(c_ref, w_ref, b_ref, o_ref):
    c = c_ref[...]
    s = c * _sigmoid(c)
    o_ref[...] = _dot3(s, w_ref[...]) + b_ref[...]


def _mod_call(c_all, w_mod, b_mod):
    n = w_mod.shape[1]
    tn = 1536
    return pl.pallas_call(
        _mod_body,
        grid=(n // tn,),
        in_specs=[pl.BlockSpec((16, D_MODEL), lambda j: (0, 0)),
                  pl.BlockSpec((D_MODEL, tn), lambda j: (0, j)),
                  pl.BlockSpec((1, tn), lambda j: (0, j))],
        out_specs=pl.BlockSpec((16, tn), lambda j: (0, j)),
        out_shape=jax.ShapeDtypeStruct((16, n), F32),
        compiler_params=_cp(("arbitrary",)),
        name="mod",
    )(c_all, w_mod, b_mod)


def _inproj_body(x_ref, mod_ref, g_ref, w_ref, ws_ref, xb_ref, gg_ref, us_ref, *, seq_len):
    m = mod_ref[0]
    h = _norm_mod(x_ref[...], g_ref[...], m[:, D_MODEL:2 * D_MODEL], m[:, 0:D_MODEL])
    hb = h.astype(BF16)
    z = jnp.dot(hb, w_ref[...], preferred_element_type=F32)
    seg = seq_len // SUBLANES
    for s in range(x_ref.shape[0] // seq_len):
        for k in range(SUBLANES):
            t0 = s * seq_len + k * seg
            for c in range(D_LRU // LANES):
                xb_ref[c, pl.ds(s * seq_len + k, seg, stride=SUBLANES), :] = (
                    z[t0:t0 + seg, c * LANES:(c + 1) * LANES])
    gg_ref[...] = _gelu(z[:, D_LRU:]).astype(BF16)
    us = jnp.dot(hb, ws_ref[...], preferred_element_type=F32)
    for k in range(S5_BLK):
        us_ref[k] = us[:, k * LANES:(k + 1) * LANES]


def _inproj_call(x2d, mod3, g1, w_main, w_s5, mod_row, tm, seq_len):
    n = x2d.shape[0]
    return pl.pallas_call(
        functools.partial(_inproj_body, seq_len=seq_len), grid=(n // tm,),
        in_specs=[pl.BlockSpec((tm, D_MODEL), lambda i: (i, 0)),
                  pl.BlockSpec((1, 1, 6 * D_MODEL), lambda i: (mod_row(i), 0, 0)),
                  pl.BlockSpec((1, D_MODEL), lambda i: (0, 0)),
                  pl.BlockSpec((D_MODEL, 2 * D_LRU), lambda i: (0, 0)),
                  pl.BlockSpec((D_MODEL, D_S5), lambda i: (0, 0))],
        out_specs=[pl.BlockSpec((D_LRU // LANES, tm, LANES), lambda i: (0, i, 0)),
                   pl.BlockSpec((tm, D_LRU), lambda i: (i, 0)),
                   pl.BlockSpec((S5_BLK, tm, LANES), lambda i: (0, i, 0))],
        out_shape=[jax.ShapeDtypeStruct((D_LRU // LANES, n, LANES), F32),
                   jax.ShapeDtypeStruct((n, D_LRU), BF16),
                   jax.ShapeDtypeStruct((S5_BLK, n, LANES), F32)],
        compiler_params=_cp(("parallel",)), name="inproj",
    )(x2d, mod3, g1, w_main, w_s5)


def _lru_body(xb_ref, gg_ref, cw_ref, cb_ref, wg_ref, ba_ref, bx_ref, lam_ref, h0_ref,
              ya_ref, fin_ref, xpad, a_f, b_f, a_b, b_b, hs, *, T):
    seg = T // SUBLANES
    ch = ROW_CHUNK
    nch = T // ch
    n_slab = D_LRU // LANES
    pad = 2 * SUBLANES
    sub = lax.broadcasted_iota(I32, (SUBLANES, D_LRU), 0)

    def rows_of(r0, n):
        return jnp.concatenate([xb_ref[c, r0:r0 + n, :] for c in range(n_slab)], axis=1)

    def from_prev_segment(tile):
        return jnp.where(sub >= 1, pltpu.roll(tile, 1, 0), 0.0)

    def from_next_segment(tile):
        return jnp.where(sub < SUBLANES - 1, pltpu.roll(tile, SUBLANES - 1, 0), 0.0)

    xpad[0:SUBLANES, :] = from_prev_segment(rows_of((seg - 2) * SUBLANES, SUBLANES))
    xpad[SUBLANES:pad, :] = from_prev_segment(rows_of((seg - 1) * SUBLANES, SUBLANES))
    xpad[pad + T:pad + T + SUBLANES, :] = from_next_segment(rows_of(0, SUBLANES))
    for c in range(nch):
        xpad[pad + c * ch:pad + (c + 1) * ch, :] = rows_of(c * ch, ch)

    nl = -lam_ref[...]
    softplus = jnp.maximum(nl, 0.0) + jnp.log1p(jnp.exp(-jnp.abs(nl)))
    decay = (-LRU_C) * softplus
    scr = ((a_f, b_f), (a_b, b_b))

    for c in range(nch):
        t0 = c * ch
        xc = cb_ref[...] + sum(
            xpad[t0 + k * SUBLANES:t0 + k * SUBLANES + ch, :] * cw_ref[k:k + 1, :]
            for k in range(CONV_W))
        xcb = xc.astype(BF16)
        w = MXU_DIM
        pre = [jnp.dot(xcb[:, kb * w:(kb + 1) * w], wg_ref[kb], preferred_element_type=F32)
               for kb in range(D_LRU // w)]
        for d in range(2):
            ra = jnp.concatenate([p[:, d * w:(d + 1) * w] for p in pre], axis=1)
            gx = jnp.concatenate([p[:, (2 + d) * w:(3 + d) * w] for p in pre], axis=1)
            r = _sigmoid(ra + ba_ref[d:d + 1, :])
            gi = _sigmoid(gx + bx_ref[d:d + 1, :])
            a = jnp.exp(r * decay[d:d + 1, :])
            scr[d][0][t0:t0 + ch, :] = a
            scr[d][1][t0:t0 + ch, :] = jnp.exp(0.5 * jnp.log(1.0 - a * a)) * gi * xc

    def local_scan(a_ref, u_ref, reverse):
        def step(i, carry):
            h, p = carry
            r0 = pl.multiple_of((seg - 1 - i if reverse else i) * SUBLANES, SUBLANES)
            a = a_ref[pl.ds(r0, SUBLANES), :]
            h = a * h + u_ref[pl.ds(r0, SUBLANES), :]
            p = a * p
            u_ref[pl.ds(r0, SUBLANES), :] = h
            a_ref[pl.ds(r0, SUBLANES), :] = p
            return h, p
        init = (jnp.zeros((SUBLANES, D_LRU), F32), jnp.ones((SUBLANES, D_LRU), F32))
        return lax.fori_loop(0, seg, step, init, unroll=2)

    def carry_in(h_end, p_end, h0_row, reverse):
        edge = sub == (SUBLANES - 1 if reverse else 0)
        shift = SUBLANES - 1 if reverse else 1
        c = jnp.broadcast_to(h0_row, (SUBLANES, D_LRU))
        for _ in range(SUBLANES - 1):
            c = jnp.where(edge, h0_row, pltpu.roll(h_end + p_end * c, shift, 0))
        return c

    hf_end, pf_end = local_scan(a_f, b_f, False)
    hb_end, pb_end = local_scan(a_b, b_b, True)
    c_f = carry_in(hf_end, pf_end, h0_ref[0, 0:1, :], False)
    c_b = carry_in(hb_end, pb_end, h0_ref[0, 1:2, :], True)
    fin_ref[0, 0:1, :] = (hf_end + pf_end * c_f)[SUBLANES - 1:SUBLANES, :]
    fin_ref[0, 1:2, :] = (hb_end + pb_end * c_b)[0:1, :]

    for c in range(nch):
        sl = slice(c * ch, (c + 1) * ch)
        tile3 = (ch // SUBLANES, SUBLANES, D_LRU)
        hsum = ((b_f[sl, :].reshape(tile3) + a_f[sl, :].reshape(tile3) * c_f[None])
                + (b_b[sl, :].reshape(tile3) + a_b[sl, :].reshape(tile3) * c_b[None])).reshape(ch, D_LRU)
        for j in range(n_slab):
            hs[j, sl, :] = hsum[:, j * LANES:(j + 1) * LANES]

    for k in range(SUBLANES):
        rows = slice(k * seg, (k + 1) * seg)
        h_seg = jnp.concatenate([hs[j, pl.ds(k, seg, stride=SUBLANES), :] for j in range(n_slab)], axis=1)
        ya_ref[0, rows, :] = (gg_ref[0, rows, :].astype(F32) * h_seg).astype(BF16)


def _lru_call(xb8, gg, conv_w, conv_b, wg, ba, bx, lam, h0, n_seq, T):
    gg3 = gg.reshape(n_seq, T, D_LRU)
    n_slab = D_LRU // LANES
    full = lambda shape: pl.BlockSpec(shape, lambda i: (0,) * len(shape))
    seq = lambda shape: pl.BlockSpec(shape, lambda i: (i,) + (0,) * (len(shape) - 1))
    ya, fin = pl.pallas_call(
        functools.partial(_lru_body, T=T),
        grid=(n_seq,),
        in_specs=[pl.BlockSpec((n_slab, T, LANES), lambda i: (0, i, 0)), seq((1, T, D_LRU)),
                  full((CONV_W, D_LRU)), full((1, D_LRU)),
                  full((D_LRU // MXU_DIM, MXU_DIM, 4 * MXU_DIM)), full((2, D_LRU)), full((2, D_LRU)),
                  full((2, D_LRU)),
                  seq((1, 2, D_LRU))],
        out_specs=[seq((1, T, D_LRU)), seq((1, 2, D_LRU))],
        out_shape=[jax.ShapeDtypeStruct((n_seq, T, D_LRU), BF16),
                   jax.ShapeDtypeStruct((n_seq, 2, D_LRU), F32)],
        scratch_shapes=[pltpu.VMEM((T + 3 * SUBLANES, D_LRU), F32)] + [pltpu.VMEM((T, D_LRU), F32)] * 4
                       + [pltpu.VMEM((n_slab, T, LANES), F32)],
        compiler_params=_cp(("parallel",)), name="lru",
    )(xb8, gg3, conv_w, conv_b, wg, ba, bx, lam, h0)
    return ya.reshape(n_seq * T, D_LRU), fin


def _s5_body(u_ref, wts_ref, wc_ref, ap_ref, h0_ref, y_ref, fin_ref, s_scr, hp_scr, slab,
             *, rt, n_c, with_h0, col_major):
    q = S5_SW // 4
    m = S5_SEG
    grp = SUBLANES * m
    n_grp = rt // grp
    seg_per_seq = n_c // m
    seq_per_grp = SUBLANES // seg_per_seq
    seq_tok = n_c * S5_L
    halves = seq_tok // (GRID_W * S5_L)
    sub = lax.broadcasted_iota(I32, (SUBLANES, q), 0)

    def pieces(g, i):
        out = []
        for v in range(SUBLANES):
            s, sq = divmod(v, seg_per_seq)
            tok0 = (g * seq_per_grp + s) * seq_tok
            if col_major:
                for h in range(halves):
                    n = m // halves
                    out.append((pl.ds(g * grp + h * SUBLANES + v, n, stride=halves * SUBLANES),
                                pl.ds(tok0 + (h * S5_L + i) * GRID_W + sq * n, n)))
            else:
                out.append((pl.ds(g * grp + v, m, stride=SUBLANES),
                            pl.ds(tok0 + sq * m * S5_L + i, m, stride=S5_L)))
        return out

    for i in range(S5_L):
        for g in range(n_grp):
            for rows, toks in pieces(g, i):
                slab[i, rows, :] = u_ref[0, toks, :]
    ub = jnp.concatenate([slab[i] for i in range(S5_L)], axis=1).astype(BF16)
    r1 = jnp.dot(ub, wts_ref[0], preferred_element_type=F32)
    y_in = r1[:, :S5_L * LANES]
    s_scr[...] = r1[:, S5_L * LANES:]

    def cmul(ar, ai, xr, xi):
        return ar * xr - ai * xi, ar * xi + ai * xr

    def local_scan(base, col0, reverse):
        ar, ai = ap_ref[0, 0:1, col0:col0 + q], ap_ref[0, 0:1, col0 + q:col0 + 2 * q]

        def step(k, carry):
            hr, hi = carry
            r0 = pl.multiple_of(base + (m - 1 - k if reverse else k) * SUBLANES, SUBLANES)
            pr, pi = cmul(ar, ai, hr, hi)
            hr = pr + s_scr[pl.ds(r0, SUBLANES), col0:col0 + q]
            hi = pi + s_scr[pl.ds(r0, SUBLANES), col0 + q:col0 + 2 * q]
            s_scr[pl.ds(r0, SUBLANES), col0:col0 + q] = hr
            s_scr[pl.ds(r0, SUBLANES), col0 + q:col0 + 2 * q] = hi
            return hr, hi
        zero = jnp.zeros((SUBLANES, q), F32)
        return lax.fori_loop(0, m, step, (zero, zero), unroll=2)

    def chain(g, col0, reverse, h_end):
        base = g * grp
        first = (sub & (seg_per_seq - 1)) == (seg_per_seq - 1 if reverse else 0)
        h0r = jnp.zeros((SUBLANES, q), F32)
        h0i = jnp.zeros((SUBLANES, q), F32)
        if with_h0:
            for s in range(seq_per_grp):
                row = h0_ref[0, 0, g * seq_per_grp + s:g * seq_per_grp + s + 1, :]
                mine = (sub >> (seg_per_seq.bit_length() - 1)) == s
                h0r = jnp.where(mine, row[:, col0:col0 + q], h0r)
                h0i = jnp.where(mine, row[:, col0 + q:col0 + 2 * q], h0i)
        cr, ci = h0r, h0i
        if seg_per_seq > 1:
            er, ei = ap_ref[0, m - 1:m, col0:col0 + q], ap_ref[0, m - 1:m, col0 + q:col0 + 2 * q]
            shift = SUBLANES - 1 if reverse else 1
            for _ in range(seg_per_seq - 1):
                pr, pi = cmul(er, ei, cr, ci)
                cr = jnp.where(first, h0r, pltpu.roll(h_end[0] + pr, shift, 0))
                ci = jnp.where(first, h0i, pltpu.roll(h_end[1] + pi, shift, 0))
        if with_h0 or seg_per_seq > 1:
            def add(c, _):
                r0 = pl.multiple_of(base + c * SUBLANES, SUBLANES)
                k = m - 1 - c if reverse else c
                pr, pi = cmul(ap_ref[0, pl.ds(k, 1), col0:col0 + q],
                              ap_ref[0, pl.ds(k, 1), col0 + q:col0 + 2 * q], cr, ci)
                s_scr[pl.ds(r0, SUBLANES), col0:col0 + q] += pr
                s_scr[pl.ds(r0, SUBLANES), col0 + q:col0 + 2 * q] += pi
                return 0
            lax.fori_loop(0, m, add, 0, unroll=2)
        return cr, ci

    for g in range(n_grp):
        base = g * grp
        body = slice(base + SUBLANES, base + grp)
        head = slice(base, base + SUBLANES)
        prev = slice(base, base + grp - SUBLANES)
        tail = slice(base + grp - SUBLANES, base + grp)
        cr, ci = chain(g, 0, False, local_scan(base, 0, False))
        hp_scr[body, 0:2 * q] = s_scr[prev, 0:2 * q]
        hp_scr[head, 0:q] = cr
        hp_scr[head, q:2 * q] = ci
        cr, ci = chain(g, 2 * q, True, local_scan(base, 2 * q, True))
        hp_scr[prev, 2 * q:4 * q] = s_scr[body, 2 * q:4 * q]
        hp_scr[tail, 2 * q:3 * q] = cr
        hp_scr[tail, 3 * q:4 * q] = ci
        if seg_per_seq == 1:
            rows = slice(g * SUBLANES, (g + 1) * SUBLANES)
            fin_ref[0, 0, rows, 0:2 * q] = s_scr[tail, 0:2 * q]
            fin_ref[0, 0, rows, 2 * q:4 * q] = s_scr[head, 2 * q:4 * q]
    if seg_per_seq > 1:
        fin_ref[...] = jnp.zeros(fin_ref.shape, F32)

    y = y_in + jnp.dot(hp_scr[...].astype(BF16), wc_ref[0], preferred_element_type=F32)
    for i in range(S5_L):
        slab[i] = y[:, i * LANES:(i + 1) * LANES]
        for g in range(n_grp):
            for rows, toks in pieces(g, i):
                y_ref[0, toks, :] = slab[i, rows, :]


def _s5_call(us4, wts, wc, apow, h0, n_seq, T, with_h0, col_major):
    n_c = T // S5_L
    rows = n_seq * n_c
    rt = 512
    nseq_t = rt // n_c
    h0 = h0.reshape(S5_BLK, n_seq // nseq_t, nseq_t, S5_SW)
    y4, fin = pl.pallas_call(
        functools.partial(_s5_body, rt=rt, n_c=n_c, with_h0=with_h0, col_major=col_major),
        grid=(S5_BLK, rows // rt),
        in_specs=[pl.BlockSpec((1, rt * S5_L, LANES), lambda b, r: (b, r, 0)),
                  pl.BlockSpec((1, S5_L * LANES, S5_L * LANES + S5_SW), lambda b, r: (b, 0, 0)),
                  pl.BlockSpec((1, S5_SW, S5_L * LANES), lambda b, r: (b, 0, 0)),
                  pl.BlockSpec((1, S5_SEG, S5_SW), lambda b, r: (b, 0, 0)),
                  pl.BlockSpec((1, 1, nseq_t, S5_SW), lambda b, r: (b, r, 0, 0))],
        out_specs=[pl.BlockSpec((1, rt * S5_L, LANES), lambda b, r: (b, r, 0)),
                   pl.BlockSpec((1, 1, nseq_t, S5_SW), lambda b, r: (b, r, 0, 0))],
        out_shape=[jax.ShapeDtypeStruct((S5_BLK, n_seq * T, LANES), F32),
                   jax.ShapeDtypeStruct((S5_BLK, n_seq // nseq_t, nseq_t, S5_SW), F32)],
        scratch_shapes=[pltpu.VMEM((rt, S5_SW), F32), pltpu.VMEM((rt, S5_SW), F32),
                        pltpu.VMEM((S5_L, rt, LANES), F32)],
        compiler_params=_cp(("parallel", "parallel")), name="s5",
    )(us4, wts, wc, apow, h0)
    return y4, fin.reshape(S5_BLK, n_seq, S5_SW)


def _s5_disc(lr, li, st):
    mag = jnp.exp(lr * st)
    ang = li * st
    ar, ai = mag * jnp.cos(ang), mag * jnp.sin(ang)
    den = lr * lr + li * li
    fr = ((ar - 1.0) * lr + ai * li) / den
    fi = (ai * lr - (ar - 1.0) * li) / den
    return ar, ai, fr, fi


def _s5_powers(ar, ai, n):
    out = [(jnp.ones_like(ar), jnp.zeros_like(ar))]
    for _ in range(n):
        pr, pi = out[-1]
        out.append((pr * ar - pi * ai, pr * ai + pi * ar))
    return out


def _s5_prep_body(rowp_ref, colp_ref, flatp_ref, bt_ref, cm_ref, wts_ref, wc_ref, ap_ref):
    L, N, P = S5_L, S5_N, S5_P
    w = L * LANES
    div = lambda x, d: x >> (d.bit_length() - 1)
    mod = lambda x, d: x & (d - 1)
    one = lambda m: jnp.where(m, 1.0, 0.0).astype(BF16)
    r1 = lax.broadcasted_iota(I32, (LANES, LANES), 0)
    c1 = lax.broadcasted_iota(I32, (LANES, LANES), 1)
    first_half = c1 < N
    sgn = jnp.where(first_half, -1.0, 1.0)

    e_maps, taps = [], []
    same_group = div(r1, P) == div(c1, P)
    fold = one(mod(r1, P) == mod(c1, P))
    for d in range(2):
        ar, ai, fr, fi = _s5_disc(rowp_ref[0, d, 0], rowp_ref[1, d, 0], rowp_ref[2, d, 0])
        bx1 = bt_ref[d, 0]
        bx2 = pltpu.roll(bx1, N, 1)
        x1 = fr * bx1 + (fi * sgn) * bx2
        x2 = pltpu.roll(x1, N, 1)
        e_d, t_d = [], []
        for pr, pi in _s5_powers(ar, ai, L):
            e_k = pr * x1 + (pi * sgn) * x2
            e_d.append(e_k)
            full = _dot3(e_k, cm_ref[d, 0])
            kept = jnp.where(same_group, full, 0.0)
            hi = kept.astype(BF16)
            mid = (kept - hi.astype(F32)).astype(BF16)
            lo = (kept - hi.astype(F32) - mid.astype(F32)).astype(BF16)
            t_d.append(sum(jnp.dot(part, fold, preferred_element_type=F32) for part in (hi, mid, lo)))
        e_maps.append(e_d)
        taps.append(t_d)

    r = lax.broadcasted_iota(I32, (LANES, w), 0)
    c = lax.broadcasted_iota(I32, (LANES, w), 1)
    t_tap = one((div(r, P) == div(c, LANES)) & (mod(r, P) == mod(c, P)))
    m_tap = div(r, P) == div(mod(c, LANES), P)
    t_st = one((div(r, N) == div(c, 8 * N)) & (mod(r, N) == mod(c, N)))
    m_st = div(r, P) == div(mod(c, 8 * N), N)
    col_blk = div(c1, P)
    for j in range(L):
        rows = slice(j * LANES, (j + 1) * LANES)
        toe = jnp.zeros((LANES, LANES), F32)
        for k in range(L):
            toe = (toe + jnp.where(col_blk == j + k, taps[0][k], 0.0)
                   + jnp.where(col_blk == j - k, taps[1][k], 0.0))
        tap = jnp.dot(toe.astype(BF16), t_tap, preferred_element_type=F32)
        wts_ref[0, rows, 0:w] = jnp.where(m_tap, tap, 0.0).astype(BF16)
        for k, e_j in enumerate((e_maps[0][L - 1 - j], e_maps[1][j])):
            st = jnp.dot(e_j.astype(BF16), t_st, preferred_element_type=F32)
            wts_ref[0, rows, w + k * w:w + (k + 1) * w] = jnp.where(m_st, st, 0.0).astype(BF16)

    rr = lax.broadcasted_iota(I32, (w, LANES), 0)
    cc = lax.broadcasted_iota(I32, (w, LANES), 1)
    t_row = one((div(rr, 8 * N) == div(cc, N)) & (mod(rr, N) == mod(cc, N)))
    r2 = lax.broadcasted_iota(I32, (w, w), 0)
    c2 = lax.broadcasted_iota(I32, (w, w), 1)
    m_row = div(mod(r2, 8 * N), N) == div(mod(c2, LANES), P)
    for d in range(2):
        ar, ai, _, _ = _s5_disc(colp_ref[0, d, 0], colp_ref[1, d, 0], colp_ref[2, d, 0])
        a1 = cm_ref[d, 0]
        swapped = pltpu.roll(a1, N, 0)
        a2 = jnp.where(r1 < N, swapped, -swapped)
        pw = _s5_powers(ar, ai, L)
        ks = [i + 1 for i in range(L)] if d == 0 else [L - i for i in range(L)]
        g = jnp.concatenate([a1 * pw[k][0] + a2 * pw[k][1] for k in ks], axis=1)
        corr = jnp.dot(t_row, g.astype(BF16), preferred_element_type=F32)
        wc_ref[0, d * w:(d + 1) * w, :] = jnp.where(m_row, corr, 0.0).astype(BF16)

    al = []
    for d in range(2):
        ar, ai, _, _ = _s5_disc(flatp_ref[0, d, 0], flatp_ref[1, d, 0], flatp_ref[2, d, 0])
        al.append(_s5_powers(ar, ai, L)[L])
    cur = al
    for k in range(S5_SEG):
        ap_ref[0, k:k + 1, :] = jnp.concatenate([cur[0][0], cur[0][1], cur[1][0], cur[1][1]], axis=1)
        cur = [(cr * br - ci * bi, cr * bi + ci * br) for (cr, ci), (br, bi) in zip(cur, al)]


def _s5_weights(lam_re, lam_im, log_step, b_re, b_im, c_re, c_im):
    L, G, N, P = S5_L, S5_G, S5_N, S5_P
    w = L * LANES
    step = jnp.broadcast_to(jnp.exp(log_step)[..., None], lam_re.shape)
    prm = jnp.stack([lam_re, lam_im, step])
    rowp = jnp.broadcast_to(prm[:, :, :, None, None, :], (3, 2, G, P, 2, N)).reshape(3, 2, S5_BLK, LANES, 2 * N)
    colp = jnp.broadcast_to(prm.transpose(0, 1, 3, 2)[:, :, None, :, :, None], (3, 2, 2, N, G, P))
    colp = colp.reshape(3, 2, 2 * N, S5_BLK, LANES).transpose(0, 1, 3, 2, 4)
    flatp = prm.reshape(3, 2, S5_BLK, 1, 8 * N)
    bt = jnp.stack([b_re, b_im], axis=3).transpose(0, 1, 4, 3, 2)
    bt = bt.reshape(2, S5_BLK, LANES, 2 * N)
    cm = jnp.stack([c_re, -c_im], axis=1).transpose(0, 1, 4, 2, 3)
    cm = cm.reshape(2, 2 * N, S5_BLK, LANES).transpose(0, 2, 1, 3)
    blk = lambda lead, a, b_: pl.BlockSpec(lead + (1, a, b_), lambda i: (0,) * len(lead) + (i, 0, 0))
    out = lambda a, b_: pl.BlockSpec((1, a, b_), lambda i: (i, 0, 0))
    return pl.pallas_call(
        _s5_prep_body, grid=(S5_BLK,),
        in_specs=[blk((3, 2), LANES, LANES), blk((3, 2), LANES, LANES), blk((3, 2), 1, 8 * N),
                  blk((2,), LANES, LANES), blk((2,), LANES, LANES)],
        out_specs=[out(w, w + S5_SW), out(S5_SW, w), out(S5_SEG, S5_SW)],
        out_shape=[jax.ShapeDtypeStruct((S5_BLK, w, w + S5_SW), BF16),
                   jax.ShapeDtypeStruct((S5_BLK, S5_SW, w), BF16),
                   jax.ShapeDtypeStruct((S5_BLK, S5_SEG, S5_SW), F32)],
        compiler_params=_cp(("parallel",)), name="s5_prep",
    )(rowp, colp, flatp, bt, cm)


def _mixout_body(xp_ref, xs_ref, mod_ref, g1_ref, g2_ref, wbg_ref, bbg_ref, yap_ref, yas_ref,
                 y4p_ref, y4s_ref, u4p_ref, u4s_ref, *rest, tiles_p):
    i = pl.program_id(0)

    @pl.when(i < tiles_p)
    def _():
        _mixout_tile(xp_ref, mod_ref, g1_ref, g2_ref, wbg_ref, bbg_ref, yap_ref, y4p_ref, u4p_ref, *rest)

    @pl.when(i >= tiles_p)
    def _():
        _mixout_tile(xs_ref, mod_ref, g1_ref, g2_ref, wbg_ref, bbg_ref, yas_ref, y4s_ref, u4s_ref, *rest)


def _mixout_tile(x_ref, mod_ref, g1_ref, g2_ref, wbg_ref, bbg_ref, ya_ref, y4_ref, u4_ref,
                 dsk_ref, wglu_ref, bglu_ref, wso_ref, wlo_ref, wo_ref, wrt_ref, brt_ref,
                 x1_ref, h2_ref, aff_ref):
    m = mod_ref[0]
    sh1, sc1, gt1 = m[:, 0:D_MODEL], m[:, D_MODEL:2 * D_MODEL], m[:, 2 * D_MODEL:3 * D_MODEL]
    sh2, sc2 = m[:, 3 * D_MODEL:4 * D_MODEL], m[:, 4 * D_MODEL:5 * D_MODEL]
    x = x_ref[...]
    h = _norm_mod(x, g1_ref[...], sc1, sh1)
    gates = _sigmoid(jnp.dot(h.astype(BF16), wbg_ref[...], preferred_element_type=F32) + bbg_ref[...])

    def assemble(ref):
        return jnp.concatenate([ref[k] for k in range(S5_BLK)], axis=1)

    ys = assemble(y4_ref) + dsk_ref[...] * assemble(u4_ref)
    v = _gelu(ys)
    ob = v * _sigmoid(jnp.dot(v.astype(BF16), wglu_ref[...], preferred_element_type=F32) + bglu_ref[...])
    merged = (gates[:, :D_MODEL] * jnp.dot(ya_ref[...], wlo_ref[...], preferred_element_type=F32)
              + gates[:, D_MODEL:] * jnp.dot(ob.astype(BF16), wso_ref[...], preferred_element_type=F32))
    mix = jnp.dot(merged.astype(BF16), wo_ref[...], preferred_element_type=F32)
    x1 = x + gt1 * mix
    x1_ref[...] = x1
    h2 = _norm_mod(x1, g2_ref[...], sc2, sh2)
    h2_ref[...] = h2
    logits = _dot3(wrt_ref[...], h2, dims=(((1,), (1,)), ((), ()))) + brt_ref[...]
    mx = jnp.max(logits, axis=0, keepdims=True)
    ex = jnp.exp(logits - mx)
    aff_ref[...] = ex / jnp.sum(ex, axis=0, keepdims=True)


def _mixout_call(xp, xs, mod3, g1, g2, wbg, bbg, ya_p, ya_s, y4_p, y4_s, u4_p, u4_s,
                 dsk, wglu, bglu, wso, wlo, wo, wrt, brt, mod_row, tm):
    n_p, n_s = xp.shape[0], xs.shape[0]
    tiles_p = n_p // tm
    n = n_p + n_s
    full = lambda shape: pl.BlockSpec(shape, lambda i: (0,) * len(shape))
    at_p = lambda i: jnp.minimum(i, tiles_p - 1)
    at_s = lambda i: jnp.maximum(i - tiles_p, 0)
    tok = lambda w, at: pl.BlockSpec((tm, w), lambda i: (at(i), 0))
    s5 = lambda at: pl.BlockSpec((S5_BLK, tm, LANES), lambda i: (0, at(i), 0))
    out = lambda w: pl.BlockSpec((tm, w), lambda i: (i, 0))
    return pl.pallas_call(
        functools.partial(_mixout_body, tiles_p=tiles_p),
        grid=(n // tm,),
        in_specs=[tok(D_MODEL, at_p), tok(D_MODEL, at_s),
                  pl.BlockSpec((1, 1, 6 * D_MODEL), lambda i: (mod_row(i), 0, 0)),
                  full((1, D_MODEL)), full((1, D_MODEL)), full((D_MODEL, 2 * D_MODEL)),
                  full((1, 2 * D_MODEL)), tok(D_LRU, at_p), tok(D_LRU, at_s),
                  s5(at_p), s5(at_s), s5(at_p), s5(at_s), full((1, D_S5)),
                  full((D_S5, D_S5)), full((1, D_S5)), full((D_S5, D_MODEL)), full((D_LRU, D_MODEL)),
                  full((D_MODEL, D_MODEL)), full((N_EXPERTS, D_MODEL)), full((N_EXPERTS, 1))],
        out_specs=[out(D_MODEL), out(D_MODEL), pl.BlockSpec((N_EXPERTS, tm), lambda i: (0, i))],
        out_shape=[jax.ShapeDtypeStruct((n, D_MODEL), F32), jax.ShapeDtypeStruct((n, D_MODEL), F32),
                   jax.ShapeDtypeStruct((N_EXPERTS, n), F32)],
        compiler_params=_cp(("arbitrary",)), name="mixout",
    )(xp, xs, mod3, g1, g2, wbg, bbg, ya_p, ya_s, y4_p, y4_s, u4_p, u4_s,
      dsk, wglu, bglu, wso, wlo, wo, wrt, brt)


def _select_body(aff_ref, gate_ref, pos_ref, offs_ref, *, n_tok, cap):
    aff = aff_ref[...]
    capf = float(cap)

    def bis(_, lh):
        lo, hi = lh
        mid = lo + ((hi - lo + 1) >> 1)
        cnt = jnp.sum(jnp.where(aff >= pltpu.bitcast(mid, F32), 1.0, 0.0), axis=1, keepdims=True)
        ok = cnt >= capf
        return jnp.where(ok, mid, lo), jnp.where(ok, hi, mid - 1)

    lo0 = jnp.zeros((N_EXPERTS, 1), I32)
    hi0 = jnp.full((N_EXPERTS, 1), 0x7F800000, I32)
    thr_bits, _ = lax.fori_loop(0, 31, bis, (lo0, hi0))
    thr = pltpu.bitcast(thr_bits, F32)
    need = capf - jnp.sum(jnp.where(aff > thr, 1.0, 0.0), axis=1, keepdims=True)

    r = lax.broadcasted_iota(I32, (TOK_TILE, TOK_TILE), 0)
    c = lax.broadcasted_iota(I32, (TOK_TILE, TOK_TILE), 1)
    tri = jnp.where(r < c, 1.0, 0.0).astype(BF16)
    lane = lax.broadcasted_iota(I32, (N_EXPERTS, LANES), 1)
    n_tiles = n_tok // TOK_TILE
    run_eq = jnp.zeros((N_EXPERTS, 1), F32)
    run_sel = jnp.zeros((N_EXPERTS, 1), F32)
    offs = jnp.zeros((N_EXPERTS, LANES), F32)
    for t in range(n_tiles):
        sl = slice(t * TOK_TILE, (t + 1) * TOK_TILE)
        aff_t = aff_ref[:, sl]
        is_eq = aff_t == thr
        eq_t = jnp.where(is_eq, 1.0, 0.0)
        rank_eq = jnp.dot(eq_t.astype(BF16), tri, preferred_element_type=F32) + run_eq
        cand = (aff_t > thr) | (is_eq & (rank_eq < need))
        pos_t = jnp.dot(jnp.where(cand, 1.0, 0.0).astype(BF16), tri, preferred_element_type=F32) + run_sel
        sel_t = cand & (pos_t < capf)
        sel_f = jnp.where(sel_t, 1.0, 0.0)
        offs = jnp.where(lane == t, run_sel, offs)
        gate_ref[:, sl] = jnp.where(sel_t, aff_t, 0.0)
        pos_ref[:, sl] = jnp.where(sel_t, pos_t, -1.0).astype(I32)
        run_eq = run_eq + jnp.sum(eq_t, axis=1, keepdims=True)
        run_sel = run_sel + jnp.sum(sel_f, axis=1, keepdims=True)
    offs = jnp.where(lane >= n_tiles, run_sel, offs)
    offs_ref[...] = offs.astype(I32)


def _select_call(aff_t, cap, n_tok):
    n_path = aff_t.shape[1] // n_tok
    per_p = lambda a, b: pl.BlockSpec((None, a, b), lambda p: (p, 0, 0))
    return pl.pallas_call(
        functools.partial(_select_body, n_tok=n_tok, cap=cap),
        grid=(n_path,),
        in_specs=[pl.BlockSpec((N_EXPERTS, n_tok), lambda p: (0, p))],
        out_specs=[per_p(N_EXPERTS, n_tok), per_p(N_EXPERTS, n_tok), per_p(N_EXPERTS, LANES)],
        out_shape=[jax.ShapeDtypeStruct((n_path, N_EXPERTS, n_tok), F32),
                   jax.ShapeDtypeStruct((n_path, N_EXPERTS, n_tok), I32),
                   jax.ShapeDtypeStruct((n_path, N_EXPERTS, LANES), I32)],
        compiler_params=_cp(("parallel",)), name="select",
    )(aff_t)


def _invert_body(offs_ref, pos_ref, gate_ref, idx_ref, gcol_ref, acc_i, acc_g, *, n_tiles, cap):
    path = pl.program_id(0)
    base = (path * N_EXPERTS + pl.program_id(1)) * LANES
    acc_i[...] = jnp.zeros(acc_i.shape, F32)
    acc_g[...] = jnp.zeros(acc_g.shape, F32)
    jrow = lax.broadcasted_iota(I32, (SLOT_CHUNK, LANES), 0)
    lane = lax.broadcasted_iota(I32, (1, LANES), 1)

    def per_tile(t, _):
        first = (offs_ref[base + t] >> 3) << 3
        prow = pos_ref[0, 0, pl.ds(t, 1), :]
        grow = gate_ref[0, 0, pl.ds(t, 1), :]
        tok = (lane + (t * TOK_TILE + 1)).astype(F32)

        def window(w, _):
            start = pl.multiple_of(first + w * SLOT_CHUNK, SUBLANES)
            rows = pl.ds(start, SLOT_CHUNK)
            hit = prow == (jrow + start)
            acc_i[rows, :] = jnp.maximum(acc_i[rows, :], jnp.where(hit, tok, 0.0))
            acc_g[rows, :] = jnp.maximum(acc_g[rows, :], jnp.where(hit, grow, 0.0))
            return 0

        n_win = (offs_ref[base + t + 1] - first + SLOT_CHUNK - 1) >> (SLOT_CHUNK.bit_length() - 1)
        lax.fori_loop(0, n_win, window, 0)
        return 0

    lax.fori_loop(0, n_tiles, per_tile, 0)
    hit_i = acc_i[0:cap, :]
    hi = hit_i.astype(BF16)
    mid = (hit_i - hi.astype(F32)).astype(BF16)
    lo = (hit_i - hi.astype(F32) - mid.astype(F32)).astype(BF16)
    ones = jnp.ones((SUBLANES, LANES), BF16)
    nt = (((1,), (1,)), ((), ()))
    tok1 = sum(lax.dot_general(ones, part, nt, preferred_element_type=F32) for part in (hi, mid, lo))
    idx_ref[0, 0] = tok1[0:1, :].astype(I32) - 1 + path * (n_tiles * TOK_TILE)
    gcol_ref[0, 0] = jnp.max(acc_g[0:cap, :], axis=1, keepdims=True)


def _invert_call(offs, pos, gate, cap):
    n_path, _, n_tok = pos.shape
    n_tiles = n_tok // TOK_TILE
    pos4 = pos.reshape(n_path, N_EXPERTS, n_tiles, TOK_TILE)
    gate4 = gate.reshape(n_path, N_EXPERTS, n_tiles, TOK_TILE)
    per_e = lambda a, b: pl.BlockSpec((1, 1, a, b), lambda p, e, o: (p, e, 0, 0))
    return pl.pallas_call(
        functools.partial(_invert_body, n_tiles=n_tiles, cap=cap),
        grid_spec=pltpu.PrefetchScalarGridSpec(
            num_scalar_prefetch=1, grid=(n_path, N_EXPERTS),
            in_specs=[per_e(n_tiles, TOK_TILE), per_e(n_tiles, TOK_TILE)],
            out_specs=[per_e(1, cap), per_e(cap, 1)],
            scratch_shapes=[pltpu.VMEM((cap + SLOT_CHUNK, LANES), F32)] * 2),
        out_shape=[jax.ShapeDtypeStruct((n_path, N_EXPERTS, 1, cap), I32),
                   jax.ShapeDtypeStruct((n_path, N_EXPERTS, cap, 1), F32)],
        compiler_params=_cp(("parallel", "parallel")), name="invert",
    )(offs.reshape(-1), pos4, gate4)


def _ffn_body(idx0_ref, idxn_ref, gcol_ref, wg_ref, wu_ref, wd_ref, h_hbm, out_ref,
              xe, wgb, wub, wdb, sem, *, cap):
    e = pl.program_id(0)
    p = pl.program_id(1)
    nxt = 1 - p
    step = e * 2 + p

    def issue_rows(idx_ref, s, j0, n):
        for j in range(j0, j0 + n):
            pltpu.make_async_copy(h_hbm.at[pl.ds(idx_ref[0, 0, 0, j], 1), :],
                                  xe.at[s, pl.ds(j, 1), :], sem.at[s]).start()

    @pl.when(step == 0)
    def _():
        def issue(j, _):
            pltpu.make_async_copy(h_hbm.at[pl.ds(idx0_ref[0, 0, 0, j], 1), :],
                                  xe.at[0, pl.ds(j, 1), :], sem.at[0]).start()
            return 0
        lax.fori_loop(0, cap, issue, 0, unroll=8)

    @pl.when(p == 0)
    def _():
        wgb[...] = wg_ref[0].astype(BF16)
        wub[...] = wu_ref[0].astype(BF16)
        wdb[...] = wd_ref[0].astype(BF16)

    pltpu.make_async_copy(h_hbm.at[pl.ds(0, cap), :], xe.at[p], sem.at[p]).wait()

    ch = ROW_CHUNK
    n_ch = cap // ch
    per = -(-cap // (n_ch - 1))
    for c in range(n_ch):
        sl = slice(c * ch, (c + 1) * ch)
        xb = xe[p, sl, :].astype(BF16)
        issue_rows(idxn_ref, nxt, min(c * per, cap), min((c + 1) * per, cap) - min(c * per, cap))
        g = jnp.dot(xb, wgb[...], preferred_element_type=F32)
        u = jnp.dot(xb, wub[...], preferred_element_type=F32)
        hid = (g * _sigmoid(g)) * u
        ye = jnp.dot(hid.astype(BF16), wdb[...], preferred_element_type=F32)
        out_ref[0, 0, sl, :] = ye * gcol_ref[0, 0, sl, :]

    @pl.when(step == 2 * N_EXPERTS - 1)
    def _():
        pltpu.make_async_copy(h_hbm.at[pl.ds(0, cap), :], xe.at[nxt], sem.at[nxt]).wait()


def _ffn_call(idx, gcol, w_gate, w_up, w_down, h2, cap):
    d_e = w_gate.shape[2]
    wspec = lambda a, b: pl.BlockSpec((1, a, b), lambda e, p: (e, 0, 0))
    return pl.pallas_call(
        functools.partial(_ffn_body, cap=cap),
        grid=(N_EXPERTS, 2),
        in_specs=[pl.BlockSpec((1, 1, 1, cap), lambda e, p: (0, 0, 0, 0), memory_space=pltpu.SMEM),
                  pl.BlockSpec((1, 1, 1, cap),
                               lambda e, p: (1 - p, jnp.minimum(e + p, N_EXPERTS - 1), 0, 0),
                               memory_space=pltpu.SMEM),
                  pl.BlockSpec((1, 1, cap, 1), lambda e, p: (p, e, 0, 0)),
                  wspec(D_MODEL, d_e), wspec(D_MODEL, d_e), wspec(d_e, D_MODEL),
                  pl.BlockSpec(memory_space=pl.ANY)],
        out_specs=pl.BlockSpec((1, 1, cap, D_MODEL), lambda e, p: (p, e, 0, 0)),
        out_shape=jax.ShapeDtypeStruct((2, N_EXPERTS, cap, D_MODEL), F32),
        scratch_shapes=[pltpu.VMEM((2, cap, D_MODEL), F32), pltpu.VMEM((D_MODEL, d_e), BF16),
                        pltpu.VMEM((D_MODEL, d_e), BF16), pltpu.VMEM((d_e, D_MODEL), BF16),
                        pltpu.SemaphoreType.DMA((2,))],
        compiler_params=_cp(("arbitrary", "arbitrary")), name="ffn",
    )(idx, idx, gcol, w_gate, w_up, w_down, h2)


def _combine_body(offs_ref, x1_ref, mod_ref, gf_ref, pos_ref, yb_hbm, y_ref, wins, acc_ref, sem,
                  *, cap, path):
    t = pl.program_id(0)
    n_t = pl.num_programs(0)
    slot = t % 2
    row0 = path * N_EXPERTS * LANES
    per = COMBINE_TILE // TOK_TILE

    def first_slot(tt, e):
        return offs_ref[row0 + e * LANES + tt * per]

    def kmax_of(tt):
        k = 0
        for e in range(N_EXPERTS):
            k = jnp.maximum(k, first_slot(tt + 1, e) - first_slot(tt, e))
        return k

    def geometry(tt, e, w, win):
        fetch = win + SUBLANES
        lo = first_slot(tt, e) + w * win
        start = pl.multiple_of(jnp.minimum((lo >> 3) << 3, cap - fetch), SUBLANES)
        return lo, start, fetch

    def issue(tt, w, s, win):
        for e in range(N_EXPERTS):
            _, start, fetch = geometry(tt, e, w, win)
            pltpu.make_async_copy(yb_hbm.at[path, e, pl.ds(start, fetch), :],
                                  wins.at[s, pl.ds(e * fetch, fetch), :], sem.at[s]).start()

    def drain(s, win):
        rows = N_EXPERTS * (win + SUBLANES)
        pltpu.make_async_copy(wins.at[1 - s, pl.ds(0, rows), :], wins.at[s, pl.ds(0, rows), :],
                              sem.at[s]).wait()

    def expand(w, s, win):
        sub = lax.broadcasted_iota(I32, (win + SUBLANES, COMBINE_TILE), 0)
        rows = []
        for e in range(N_EXPERTS):
            lo, start, _ = geometry(t, e, w, win)
            pos = pos_ref[e:e + 1, :]
            in_round = jnp.logical_and(pos >= lo, pos < lo + win)
            rows.append(jnp.where(jnp.logical_and(pos - start == sub, in_round), 1.0, 0.0))
        onehot = jnp.concatenate(rows, axis=0).T.astype(BF16)
        data = wins[s, 0:N_EXPERTS * (win + SUBLANES), :]
        hi = data.astype(BF16)
        lo_part = (data - hi.astype(F32)).astype(BF16)
        return (jnp.dot(onehot, hi, preferred_element_type=F32)
                + jnp.dot(onehot, lo_part, preferred_element_type=F32))

    def issue_first(tt, s):
        wide = kmax_of(tt) > COMBINE_WIN

        @pl.when(wide)
        def _():
            issue(tt, 0, s, COMBINE_WIN_WIDE)

        @pl.when(jnp.logical_not(wide))
        def _():
            issue(tt, 0, s, COMBINE_WIN)

    @pl.when(t == 0)
    def _():
        issue_first(0, 0)

    @pl.when(t + 1 < n_t)
    def _():
        issue_first(t + 1, 1 - slot)

    kmax = kmax_of(t)

    @pl.when(kmax > COMBINE_WIN)
    def _():
        drain(slot, COMBINE_WIN_WIDE)
        acc_ref[...] = expand(0, slot, COMBINE_WIN_WIDE)

    @pl.when(kmax <= COMBINE_WIN)
    def _():
        drain(slot, COMBINE_WIN)
        acc_ref[...] = expand(0, slot, COMBINE_WIN)

    def more(w, _):
        issue(t, w, slot, COMBINE_WIN_WIDE)
        drain(slot, COMBINE_WIN_WIDE)
        acc_ref[...] += expand(w, slot, COMBINE_WIN_WIDE)
        return 0
    lax.fori_loop(1, (kmax + COMBINE_WIN_WIDE - 1) >> (COMBINE_WIN_WIDE.bit_length() - 1), more, 0)
    acc = acc_ref[...]

    gt2 = mod_ref[0][:, 5 * D_MODEL:6 * D_MODEL]
    x2 = x1_ref[...] + gt2 * acc
    ms = jnp.mean(x2 * x2, axis=-1, keepdims=True)
    y_ref[...] = x2 * lax.rsqrt(ms + EPS) * gf_ref[...]


def _combine_call(offs, pos, x1, mod3, gf, ybuf, mod_row, cap, path, n):
    first = path * n // COMBINE_TILE
    return pl.pallas_call(
        functools.partial(_combine_body, cap=cap, path=path),
        grid_spec=pltpu.PrefetchScalarGridSpec(
            num_scalar_prefetch=1, grid=(n // COMBINE_TILE,),
            in_specs=[pl.BlockSpec((COMBINE_TILE, D_MODEL), lambda i, o: (first + i, 0)),
                      pl.BlockSpec((1, 1, 6 * D_MODEL), lambda i, o: (mod_row(i), 0, 0)),
                      pl.BlockSpec((1, D_MODEL), lambda i, o: (0, 0)),
                      pl.BlockSpec((None, N_EXPERTS, COMBINE_TILE), lambda i, o: (path, 0, i)),
                      pl.BlockSpec(memory_space=pl.ANY)],
            out_specs=pl.BlockSpec((COMBINE_TILE, D_MODEL), lambda i, o: (i, 0)),
            scratch_shapes=[pltpu.VMEM((2, N_EXPERTS * (COMBINE_WIN_WIDE + SUBLANES), D_MODEL), F32),
                            pltpu.VMEM((COMBINE_TILE, D_MODEL), F32),
                            pltpu.SemaphoreType.DMA((2,))]),
        out_shape=jax.ShapeDtypeStruct((n, D_MODEL), F32),
        compiler_params=_cp(("arbitrary",)), name="combine",
    )(offs.reshape(-1), x1, mod3, gf, pos, ybuf)


def _lru_gate_weights(wa, wx):
    hb = MXU_DIM // LRU_HEAD_DIM
    nb = LRU_HEADS // hb
    eye = jnp.eye(hb, dtype=F32)

    def bd(w):
        w5 = w.reshape(nb, hb, LRU_HEAD_DIM, LRU_HEAD_DIM)
        return jnp.einsum('khij,hg->khigj', w5, eye).reshape(nb, MXU_DIM, MXU_DIM)

    return jnp.concatenate([bd(wa[0]), bd(wa[1]), bd(wx[0]), bd(wx[1])], axis=2).astype(BF16)


def kernel(x_prompt, x_sample, state_lru, state_s5_re, state_s5_im, c, c_ctx, w_mod, b_mod, g_norm1, g_norm2, w_in, conv_w, conv_b, lru_wa, lru_ba, lru_wx, lru_bx, lru_lambda, s5_lambda_re, s5_lambda_im, s5_log_step, s5_b_re, s5_b_im, s5_c_re, s5_c_im, s5_d, s5_w_glu, s5_b_glu, w_lru_out, w_s5_out, w_branch_gate, b_branch_gate, w_o, w_router, b_router, w_e_gate, w_e_up, w_e_down, g_final):
    bp, tp, _ = x_prompt.shape
    bs, ts, _ = x_sample.shape
    n_p, n_s = bp * tp, bs * ts
    l = 0

    c_all = jnp.zeros((16, D_MODEL), F32).at[0].set(c_ctx).at[1:1 + bs].set(c)
    mod3 = _mod_call(c_all, w_mod[l], b_mod[l][None, :]).reshape(16, 1, 6 * D_MODEL)

    w_in_b = w_in[l].astype(BF16)
    w_main, w_s5in = w_in_b[:, :2 * D_LRU], w_in_b[:, 2 * D_LRU:]
    g1 = g_norm1[l][None, :]
    g2 = g_norm2[l][None, :]
    wg = _lru_gate_weights(lru_wa[l], lru_wx[l])
    wts, wc, apow = _s5_weights(s5_lambda_re[l], s5_lambda_im[l], s5_log_step[l],
                                s5_b_re[l], s5_b_im[l], s5_c_re[l], s5_c_im[l])
    wbg = w_branch_gate[l].astype(BF16)
    bbg = b_branch_gate[l][None, :]
    wglu = s5_w_glu[l].astype(BF16)
    wso = w_s5_out[l].astype(BF16)
    wlo = w_lru_out[l].astype(BF16)
    wo = w_o[l].astype(BF16)
    wrt = w_router[l].T
    brt = b_router[l][:, None]

    xp2 = x_prompt.reshape(n_p, D_MODEL)
    xs2 = x_sample.reshape(n_s, D_MODEL)
    ctx_row = lambda i: 0
    tm_in = 1024
    lat_row_in = lambda i: 1 + i // (ts // max(tm_in, ts))

    def s5_state(re, im):
        def part(a, d):
            return a[:, d].reshape(-1, S5_BLK, 8 * S5_N).transpose(1, 0, 2)
        return jnp.concatenate([part(re, 0), part(im, 0), part(re, 1), part(im, 1)], axis=2)

    xb_p, gg_p, us4_p = _inproj_call(xp2, mod3, g1, w_main, w_s5in, ctx_row, max(tm_in, tp), tp)
    ya_p, lru_fin = _lru_call(xb_p, gg_p, conv_w[l], conv_b[l][None, :], wg, lru_ba[l], lru_bx[l],
                              lru_lambda[l], jnp.zeros((bp, 2, D_LRU), F32), bp, tp)
    y4_p, s5_fin = _s5_call(us4_p, wts, wc, apow, jnp.zeros((S5_BLK, bp, S5_SW), F32), bp, tp,
                            with_h0=False, col_major=False)
    xb_s, gg_s, us4_s = _inproj_call(xs2, mod3, g1, w_main, w_s5in, lat_row_in, max(tm_in, ts), ts)
    ya_s, _ = _lru_call(xb_s, gg_s, conv_w[l], conv_b[l][None, :], wg, lru_ba[l], lru_bx[l],
                        lru_lambda[l], state_lru[:, l], bs, ts)
    y4_s, _ = _s5_call(us4_s, wts, wc, apow, s5_state(state_s5_re[:, l], state_s5_im[:, l]), bs, ts,
                       with_h0=True, col_major=True)

    tm_mo = 512
    tiles_p = n_p // tm_mo
    x1, h2, aff = _mixout_call(
        xp2, xs2, mod3, g1, g2, wbg, bbg, ya_p, ya_s, y4_p, y4_s, us4_p, us4_s,
        s5_d[l][None, :], wglu, s5_b_glu[l][None, :], wso, wlo, wo, wrt, brt,
        mod_row=lambda i: jnp.where(i < tiles_p, 0, 1 + (i - tiles_p) // (ts // tm_mo)), tm=tm_mo)

    assert n_p == n_s
    cap = (CAPACITY_FACTOR * n_p) // N_EXPERTS
    gate, pos, offs = _select_call(aff, cap, n_p)
    idx, gcol = _invert_call(offs, pos, gate, cap)
    ybuf = _ffn_call(idx, gcol, w_e_gate[l], w_e_up[l], w_e_down[l], h2, cap)
    gf = g_final[None, :]
    y_p = _combine_call(offs, pos, x1, mod3, gf, ybuf, ctx_row, cap, 0, n_p)
    y_s = _combine_call(offs, pos, x1, mod3, gf, ybuf, lambda i: 1 + i // (ts // COMBINE_TILE), cap, 1, n_s)

    new_lru = lru_fin[:, None]
    sf = s5_fin.transpose(1, 0, 2).reshape(bp, S5_BLK, 2, 2, 8, S5_N)
    sf = sf.transpose(0, 2, 3, 1, 4, 5).reshape(bp, 2, 2, S5_G, S5_N)
    new_s5r = sf[:, :, 0][:, None]
    new_s5i = sf[:, :, 1][:, None]
    return (y_p.reshape(bp, tp, D_MODEL), y_s.reshape(bs, ts, D_MODEL), new_lru, new_s5r, new_s5i)
```

```python
import functools
import math

import jax
import jax.numpy as jnp
from jax import lax
from jax.experimental import pallas as pl
from jax.experimental.pallas import tpu as pltpu

F32 = jnp.float32
BF16 = jnp.bfloat16
I32 = jnp.int32

D_MODEL = 1024
D_LRU = 1024
LRU_HEADS = 16
LRU_HEAD_DIM = 64
LRU_C = 8.0
CONV_W = 4
D_S5 = 512
S5_P = 16
S5_G = 32
S5_N = 64
GRID_W = 64
N_EXPERTS = 16
CAPACITY_FACTOR = 2
EPS = 1e-6

LANES = 128
SUBLANES = 8
MXU_DIM = 256
ROW_CHUNK = 256
S5_L = 8
S5_BLK = 4
S5_SW = 4 * (LANES // S5_P) * S5_N
S5_SEG = 32
TOK_TILE = 128
SLOT_CHUNK = 32
COMBINE_TILE = 256
COMBINE_WIN = 64
COMBINE_WIN_WIDE = 128
VMEM_LIMIT = 56 * 1024 * 1024


def _cp(sem, vmem=VMEM_LIMIT):
    return pltpu.CompilerParams(dimension_semantics=sem, vmem_limit_bytes=vmem)


def _split2(a):
    hi = a.astype(BF16)
    lo = (a - hi.astype(F32)).astype(BF16)
    return hi, lo


def _dot3(a, b, dims=(((1,), (0,)), ((), ()))):
    ah, al = _split2(a)
    bh, bl = _split2(b)
    d = functools.partial(lax.dot_general, dimension_numbers=dims, preferred_element_type=F32)
    return d(ah, bh) + (d(al, bh) + d(ah, bl))


def _sigmoid(x):
    return 0.5 * jnp.tanh(0.5 * x) + 0.5


def _gelu(x):
    c = math.sqrt(2.0 / math.pi)
    half = 0.5 * x
    return half + half * jnp.tanh(x * (c + (c * 0.044715) * (x * x)))


def _norm_mod(x, g, scale, shift):
    ms = jnp.mean(x * x, axis=-1, keepdims=True)
    return (x * lax.rsqrt(ms + EPS)) * (g * (1.0 + scale)) + shift


def _mod_body(c_ref, w_ref, b_ref, o_ref):
    c = c_ref[...]
    s = c * _sigmoid(c)
    o_ref[...] = _dot3(s, w_ref[...]) + b_ref[...]


def _mod_call(c_all, w_mod, b_mod):
    n = w_mod.shape[1]
    tn = 1536
    return pl.pallas_call(
        _mod_body,
        grid=(n // tn,),
        in_specs=[pl.BlockSpec((16, D_MODEL), lambda j: (0, 0)),
                  pl.BlockSpec((D_MODEL, tn), lambda j: (0, j)),
                  pl.BlockSpec((1, tn), lambda j: (0, j))],
        out_specs=pl.BlockSpec((16, tn), lambda j: (0, j)),
        out_shape=jax.ShapeDtypeStruct((16, n), F32),
        compiler_params=_cp(("arbitrary",)),
        name="mod",
    )(c_all, w_mod, b_mod)


def _inproj_body(x_ref, mod_ref, g_ref, w_ref, ws_ref, xb_ref, gg_ref, us_ref, *, seq_len):
    m = mod_ref[0]
    h = _norm_mod(x_ref[...], g_ref[...], m[:, D_MODEL:2 * D_MODEL], m[:, 0:D_MODEL])
    hb = h.astype(BF16)
    z = jnp.dot(hb, w_ref[...], preferred_element_type=F32)
    seg = seq_len // SUBLANES
    for s in range(x_ref.shape[0] // seq_len):
        for k in range(SUBLANES):
            t0 = s * seq_len + k * seg
            for c in range(D_LRU // LANES):
                xb_ref[c, pl.ds(s * seq_len + k, seg, stride=SUBLANES), :] = (
                    z[t0:t0 + seg, c * LANES:(c + 1) * LANES])
    gg_ref[...] = _gelu(z[:, D_LRU:]).astype(BF16)
    us = jnp.dot(hb, ws_ref[...], preferred_element_type=F32)
    for k in range(S5_BLK):
        us_ref[k] = us[:, k * LANES:(k + 1) * LANES]


def _inproj_call(x2d, mod3, g1, w_main, w_s5, mod_row, tm, seq_len):
    n = x2d.shape[0]
    return pl.pallas_call(
        functools.partial(_inproj_body, seq_len=seq_len), grid=(n // tm,),
        in_specs=[pl.BlockSpec((tm, D_MODEL), lambda i: (i, 0)),
                  pl.BlockSpec((1, 1, 6 * D_MODEL), lambda i: (mod_row(i), 0, 0)),
                  pl.BlockSpec((1, D_MODEL), lambda i: (0, 0)),
                  pl.BlockSpec((D_MODEL, 2 * D_LRU), lambda i: (0, 0)),
                  pl.BlockSpec((D_MODEL, D_S5), lambda i: (0, 0))],
        out_specs=[pl.BlockSpec((D_LRU // LANES, tm, LANES), lambda i: (0, i, 0)),
                   pl.BlockSpec((tm, D_LRU), lambda i: (i, 0)),
                   pl.BlockSpec((S5_BLK, tm, LANES), lambda i: (0, i, 0))],
        out_shape=[jax.ShapeDtypeStruct((D_LRU // LANES, n, LANES), F32),
                   jax.ShapeDtypeStruct((n, D_LRU), BF16),
                   jax.ShapeDtypeStruct((S5_BLK, n, LANES), F32)],
        compiler_params=_cp(("parallel",)), name="inproj",
    )(x2d, mod3, g1, w_main, w_s5)


def _lru_body(xb_ref, gg_ref, cw_ref, cb_ref, wg_ref, ba_ref, bx_ref, lam_ref, h0_ref,
              ya_ref, fin_ref, xpad, a_f, b_f, a_b, b_b, hs, *, T):
    seg = T // SUBLANES
    ch = ROW_CHUNK
    nch = T // ch
    n_slab = D_LRU // LANES
    pad = 2 * SUBLANES
    sub = lax.broadcasted_iota(I32, (SUBLANES, D_LRU), 0)

    def rows_of(r0, n):
        return jnp.concatenate([xb_ref[c, r0:r0 + n, :] for c in range(n_slab)], axis=1)

    def from_prev_segment(tile):
        return jnp.where(sub >= 1, pltpu.roll(tile, 1, 0), 0.0)

    def from_next_segment(tile):
        return jnp.where(sub < SUBLANES - 1, pltpu.roll(tile, SUBLANES - 1, 0), 0.0)

    xpad[0:SUBLANES, :] = from_prev_segment(rows_of((seg - 2) * SUBLANES, SUBLANES))
    xpad[SUBLANES:pad, :] = from_prev_segment(rows_of((seg - 1) * SUBLANES, SUBLANES))
    xpad[pad + T:pad + T + SUBLANES, :] = from_next_segment(rows_of(0, SUBLANES))
    for c in range(nch):
        xpad[pad + c * ch:pad + (c + 1) * ch, :] = rows_of(c * ch, ch)

    nl = -lam_ref[...]
    softplus = jnp.maximum(nl, 0.0) + jnp.log1p(jnp.exp(-jnp.abs(nl)))
    decay = (-LRU_C) * softplus
    scr = ((a_f, b_f), (a_b, b_b))

    for c in range(nch):
        t0 = c * ch
        xc = cb_ref[...] + sum(
            xpad[t0 + k * SUBLANES:t0 + k * SUBLANES + ch, :] * cw_ref[k:k + 1, :]
            for k in range(CONV_W))
        xcb = xc.astype(BF16)
        w = MXU_DIM
        pre = [jnp.dot(xcb[:, kb * w:(kb + 1) * w], wg_ref[kb], preferred_element_type=F32)
               for kb in range(D_LRU // w)]
        for d in range(2):
            ra = jnp.concatenate([p[:, d * w:(d + 1) * w] for p in pre], axis=1)
            gx = jnp.concatenate([p[:, (2 + d) * w:(3 + d) * w] for p in pre], axis=1)
            r = _sigmoid(ra + ba_ref[d:d + 1, :])
            gi = _sigmoid(gx + bx_ref[d:d + 1, :])
            a = jnp.exp(r * decay[d:d + 1, :])
            scr[d][0][t0:t0 + ch, :] = a
            scr[d][1][t0:t0 + ch, :] = jnp.exp(0.5 * jnp.log(1.0 - a * a)) * gi * xc

    def local_scan(a_ref, u_ref, reverse):
        def step(i, carry):
            h, p = carry
            r0 = pl.multiple_of((seg - 1 - i if reverse else i) * SUBLANES, SUBLANES)
            a = a_ref[pl.ds(r0, SUBLANES), :]
            h = a * h + u_ref[pl.ds(r0, SUBLANES), :]
            p = a * p
            u_ref[pl.ds(r0, SUBLANES), :] = h
            a_ref[pl.ds(r0, SUBLANES), :] = p
            return h, p
        init = (jnp.zeros((SUBLANES, D_LRU), F32), jnp.ones((SUBLANES, D_LRU), F32))
        return lax.fori_loop(0, seg, step, init, unroll=2)

    def carry_in(h_end, p_end, h0_row, reverse):
        edge = sub == (SUBLANES - 1 if reverse else 0)
        shift = SUBLANES - 1 if reverse else 1
        c = jnp.broadcast_to(h0_row, (SUBLANES, D_LRU))
        for _ in range(SUBLANES - 1):
            c = jnp.where(edge, h0_row, pltpu.roll(h_end + p_end * c, shift, 0))
        return c

    hf_end, pf_end = local_scan(a_f, b_f, False)
    hb_end, pb_end = local_scan(a_b, b_b, True)
    c_f = carry_in(hf_end, pf_end, h0_ref[0, 0:1, :], False)
    c_b = carry_in(hb_end, pb_end, h0_ref[0, 1:2, :], True)
    fin_ref[0, 0:1, :] = (hf_end + pf_end * c_f)[SUBLANES - 1:SUBLANES, :]
    fin_ref[0, 1:2, :] = (hb_end + pb_end * c_b)[0:1, :]

    for c in range(nch):
        sl = slice(c * ch, (c + 1) * ch)
        tile3 = (ch // SUBLANES, SUBLANES, D_LRU)
        hsum = ((b_f[sl, :].reshape(tile3) + a_f[sl, :].reshape(tile3) * c_f[None])
                + (b_b[sl, :].reshape(tile3) + a_b[sl, :].reshape(tile3) * c_b[None])).reshape(ch, D_LRU)
        for j in range(n_slab):
            hs[j, sl, :] = hsum[:, j * LANES:(j + 1) * LANES]

    for k in range(SUBLANES):
        rows = slice(k * seg, (k + 1) * seg)
        h_seg = jnp.concatenate([hs[j, pl.ds(k, seg, stride=SUBLANES), :] for j in range(n_slab)], axis=1)
        ya_ref[0, rows, :] = (gg_ref[0, rows, :].astype(F32) * h_seg).astype(BF16)


def _lru_call(xb8, gg, conv_w, conv_b, wg, ba, bx, lam, h0, n_seq, T):
    gg3 = gg.reshape(n_seq, T, D_LRU)
    n_slab = D_LRU // LANES
    full = lambda shape: pl.BlockSpec(shape, lambda i: (0,) * len(shape))
    seq = lambda shape: pl.BlockSpec(shape, lambda i: (i,) + (0,) * (len(shape) - 1))
    ya, fin = pl.pallas_call(
        functools.partial(_lru_body, T=T),
        grid=(n_seq,),
        in_specs=[pl.BlockSpec((n_slab, T, LANES), lambda i: (0, i, 0)), seq((1, T, D_LRU)),
                  full((CONV_W, D_LRU)), full((1, D_LRU)),
                  full((D_LRU // MXU_DIM, MXU_DIM, 4 * MXU_DIM)), full((2, D_LRU)), full((2, D_LRU)),
                  full((2, D_LRU)),
                  seq((1, 2, D_LRU))],
        out_specs=[seq((1, T, D_LRU)), seq((1, 2, D_LRU))],
        out_shape=[jax.ShapeDtypeStruct((n_seq, T, D_LRU), BF16),
                   jax.ShapeDtypeStruct((n_seq, 2, D_LRU), F32)],
        scratch_shapes=[pltpu.VMEM((T + 3 * SUBLANES, D_LRU), F32)] + [pltpu.VMEM((T, D_LRU), F32)] * 4
                       + [pltpu.VMEM((n_slab, T, LANES), F32)],
        compiler_params=_cp(("parallel",)), name="lru",
    )(xb8, gg3, conv_w, conv_b, wg, ba, bx, lam, h0)
    return ya.reshape(n_seq * T, D_LRU), fin


def _s5_body(u_ref, wts_ref, wc_ref, ap_ref, h0_ref, y_ref, fin_ref, s_scr, hp_scr, slab,
             *, rt, n_c, with_h0, col_major):
    q = S5_SW // 4
    m = S5_SEG
    grp = SUBLANES * m
    n_grp = rt // grp
    seg_per_seq = n_c // m
    seq_per_grp = SUBLANES // seg_per_seq
    seq_tok = n_c * S5_L
    halves = seq_tok // (GRID_W * S5_L)
    sub = lax.broadcasted_iota(I32, (SUBLANES, q), 0)

    def pieces(g, i):
        out = []
        for v in range(SUBLANES):
            s, sq = divmod(v, seg_per_seq)
            tok0 = (g * seq_per_grp + s) * seq_tok
            if col_major:
                for h in range(halves):
                    n = m // halves
                    out.append((pl.ds(g * grp + h * SUBLANES + v, n, stride=halves * SUBLANES),
                                pl.ds(tok0 + (h * S5_L + i) * GRID_W + sq * n, n)))
            else:
                out.append((pl.ds(g * grp + v, m, stride=SUBLANES),
                            pl.ds(tok0 + sq * m * S5_L + i, m, stride=S5_L)))
        return out

    for i in range(S5_L):
        for g in range(n_grp):
            for rows, toks in pieces(g, i):
                slab[i, rows, :] = u_ref[0, toks, :]
    ub = jnp.concatenate([slab[i] for i in range(S5_L)], axis=1).astype(BF16)
    r1 = jnp.dot(ub, wts_ref[0], preferred_element_type=F32)
    y_in = r1[:, :S5_L * LANES]
    s_scr[...] = r1[:, S5_L * LANES:]

    def cmul(ar, ai, xr, xi):
        return ar * xr - ai * xi, ar * xi + ai * xr

    def local_scan(base, col0, reverse):
        ar, ai = ap_ref[0, 0:1, col0:col0 + q], ap_ref[0, 0:1, col0 + q:col0 + 2 * q]

        def step(k, carry):
            hr, hi = carry
            r0 = pl.multiple_of(base + (m - 1 - k if reverse else k) * SUBLANES, SUBLANES)
            pr, pi = cmul(ar, ai, hr, hi)
            hr = pr + s_scr[pl.ds(r0, SUBLANES), col0:col0 + q]
            hi = pi + s_scr[pl.ds(r0, SUBLANES), col0 + q:col0 + 2 * q]
            s_scr[pl.ds(r0, SUBLANES), col0:col0 + q] = hr
            s_scr[pl.ds(r0, SUBLANES), col0 + q:col0 + 2 * q] = hi
            return hr, hi
        zero = jnp.zeros((SUBLANES, q), F32)
        return lax.fori_loop(0, m, step, (zero, zero), unroll=2)

    def chain(g, col0, reverse, h_end):
        base = g * grp
        first = (sub & (seg_per_seq - 1)) == (seg_per_seq - 1 if reverse else 0)
        h0r = jnp.zeros((SUBLANES, q), F32)
        h0i = jnp.zeros((SUBLANES, q), F32)
        if with_h0:
            for s in range(seq_per_grp):
                row = h0_ref[0, 0, g * seq_per_grp + s:g * seq_per_grp + s + 1, :]
                mine = (sub >> (seg_per_seq.bit_length() - 1)) == s
                h0r = jnp.where(mine, row[:, col0:col0 + q], h0r)
                h0i = jnp.where(mine, row[:, col0 + q:col0 + 2 * q], h0i)
        cr, ci = h0r, h0i
        if seg_per_seq > 1:
            er, ei = ap_ref[0, m - 1:m, col0:col0 + q], ap_ref[0, m - 1:m, col0 + q:col0 + 2 * q]
            shift = SUBLANES - 1 if reverse else 1
            for _ in range(seg_per_seq - 1):
                pr, pi = cmul(er, ei, cr, ci)
                cr = jnp.where(first, h0r, pltpu.roll(h_end[0] + pr, shift, 0))
                ci = jnp.where(first, h0i, pltpu.roll(h_end[1] + pi, shift, 0))
        if with_h0 or seg_per_seq > 1:
            def add(c, _):
                r0 = pl.multiple_of(base + c * SUBLANES, SUBLANES)
                k = m - 1 - c if reverse else c
                pr, pi = cmul(ap_ref[0, pl.ds(k, 1), col0:col0 + q],
                              ap_ref[0, pl.ds(k, 1), col0 + q:col0 + 2 * q], cr, ci)
                s_scr[pl.ds(r0, SUBLANES), col0:col0 + q] += pr
                s_scr[pl.ds(r0, SUBLANES), col0 + q:col0 + 2 * q] += pi
                return 0
            lax.fori_loop(0, m, add, 0, unroll=2)
        return cr, ci

    for g in range(n_grp):
        base = g * grp
        body = slice(base + SUBLANES, base + grp)
        head = slice(base, base + SUBLANES)
        prev = slice(base, base + grp - SUBLANES)
        tail = slice(base + grp - SUBLANES, base + grp)
        cr, ci = chain(g, 0, False, local_scan(base, 0, False))
        hp_scr[body, 0:2 * q] = s_scr[prev, 0:2 * q]
        hp_scr[head, 0:q] = cr
        hp_scr[head, q:2 * q] = ci
        cr, ci = chain(g, 2 * q, True, local_scan(base, 2 * q, True))
        hp_scr[prev, 2 * q:4 * q] = s_scr[body, 2 * q:4 * q]
        hp_scr[tail, 2 * q:3 * q] = cr
        hp_scr[tail, 3 * q:4 * q] = ci
        if seg_per_seq == 1:
            rows = slice(g * SUBLANES, (g + 1) * SUBLANES)
            fin_ref[0, 0, rows, 0:2 * q] = s_scr[tail, 0:2 * q]
            fin_ref[0, 0, rows, 2 * q:4 * q] = s_scr[head, 2 * q:4 * q]
    if seg_per_seq > 1:
        fin_ref[...] = jnp.zeros(fin_ref.shape, F32)

    y = y_in + jnp.dot(hp_scr[...].astype(BF16), wc_ref[0], preferred_element_type=F32)
    for i in range(S5_L):
        slab[i] = y[:, i * LANES:(i + 1) * LANES]
        for g in range(n_grp):
            for rows, toks in pieces(g, i):
                y_ref[0, toks, :] = slab[i, rows, :]


def _s5_call(us4, wts, wc, apow, h0, n_seq, T, with_h0, col_major):
    n_c = T // S5_L
    rows = n_seq * n_c
    rt = 512
    nseq_t = rt // n_c
    h0 = h0.reshape(S5_BLK, n_seq // nseq_t, nseq_t, S5_SW)
    y4, fin = pl.pallas_call(
        functools.partial(_s5_body, rt=rt, n_c=n_c, with_h0=with_h0, col_major=col_major),
        grid=(S5_BLK, rows // rt),
        in_specs=[pl.BlockSpec((1, rt * S5_L, LANES), lambda b, r: (b, r, 0)),
                  pl.BlockSpec((1, S5_L * LANES, S5_L * LANES + S5_SW), lambda b, r: (b, 0, 0)),
                  pl.BlockSpec((1, S5_SW, S5_L * LANES), lambda b, r: (b, 0, 0)),
                  pl.BlockSpec((1, S5_SEG, S5_SW), lambda b, r: (b, 0, 0)),
                  pl.BlockSpec((1, 1, nseq_t, S5_SW), lambda b, r: (b, r, 0, 0))],
        out_specs=[pl.BlockSpec((1, rt * S5_L, LANES), lambda b, r: (b, r, 0)),
                   pl.BlockSpec((1, 1, nseq_t, S5_SW), lambda b, r: (b, r, 0, 0))],
        out_shape=[jax.ShapeDtypeStruct((S5_BLK, n_seq * T, LANES), F32),
                   jax.ShapeDtypeStruct((S5_BLK, n_seq // nseq_t, nseq_t, S5_SW), F32)],
        scratch_shapes=[pltpu.VMEM((rt, S5_SW), F32), pltpu.VMEM((rt, S5_SW), F32),
                        pltpu.VMEM((S5_L, rt, LANES), F32)],
        compiler_params=_cp(("parallel", "parallel")), name="s5",
    )(us4, wts, wc, apow, h0)
    return y4, fin.reshape(S5_BLK, n_seq, S5_SW)


def _s5_disc(lr, li, st):
    mag = jnp.exp(lr * st)
    ang = li * st
    ar, ai = mag * jnp.cos(ang), mag * jnp.sin(ang)
    den = lr * lr + li * li
    fr = ((ar - 1.0) * lr + ai * li) / den
    fi = (ai * lr - (ar - 1.0) * li) / den
    return ar, ai, fr, fi


def _s5_powers(ar, ai, n):
    out = [(jnp.ones_like(ar), jnp.zeros_like(ar))]
    for _ in range(n):
        pr, pi = out[-1]
        out.append((pr * ar - pi * ai, pr * ai + pi * ar))
    return out


def _s5_prep_body(rowp_ref, colp_ref, flatp_ref, bt_ref, cm_ref, wts_ref, wc_ref, ap_ref):
    L, N, P = S5_L, S5_N, S5_P
    w = L * LANES
    div = lambda x, d: x >> (d.bit_length() - 1)
    mod = lambda x, d: x & (d - 1)
    one = lambda m: jnp.where(m, 1.0, 0.0).astype(BF16)
    r1 = lax.broadcasted_iota(I32, (LANES, LANES), 0)
    c1 = lax.broadcasted_iota(I32, (LANES, LANES), 1)
    first_half = c1 < N
    sgn = jnp.where(first_half, -1.0, 1.0)

    e_maps, taps = [], []
    same_group = div(r1, P) == div(c1, P)
    fold = one(mod(r1, P) == mod(c1, P))
    for d in range(2):
        ar, ai, fr, fi = _s5_disc(rowp_ref[0, d, 0], rowp_ref[1, d, 0], rowp_ref[2, d, 0])
        bx1 = bt_ref[d, 0]
        bx2 = pltpu.roll(bx1, N, 1)
        x1 = fr * bx1 + (fi * sgn) * bx2
        x2 = pltpu.roll(x1, N, 1)
        e_d, t_d = [], []
        for pr, pi in _s5_powers(ar, ai, L):
            e_k = pr * x1 + (pi * sgn) * x2
            e_d.append(e_k)
            full = _dot3(e_k, cm_ref[d, 0])
            kept = jnp.where(same_group, full, 0.0)
            hi = kept.astype(BF16)
            mid = (kept - hi.astype(F32)).astype(BF16)
            lo = (kept - hi.astype(F32) - mid.astype(F32)).astype(BF16)
            t_d.append(sum(jnp.dot(part, fold, preferred_element_type=F32) for part in (hi, mid, lo)))
        e_maps.append(e_d)
        taps.append(t_d)

    r = lax.broadcasted_iota(I32, (LANES, w), 0)
    c = lax.broadcasted_iota(I32, (LANES, w), 1)
    t_tap = one((div(r, P) == div(c, LANES)) & (mod(r, P) == mod(c, P)))
    m_tap = div(r, P) == div(mod(c, LANES), P)
    t_st = one((div(r, N) == div(c, 8 * N)) & (mod(r, N) == mod(c, N)))
    m_st = div(r, P) == div(mod(c, 8 * N), N)
    col_blk = div(c1, P)
    for j in range(L):
        rows = slice(j * LANES, (j + 1) * LANES)
        toe = jnp.zeros((LANES, LANES), F32)
        for k in range(L):
            toe = (toe + jnp.where(col_blk == j + k, taps[0][k], 0.0)
                   + jnp.where(col_blk == j - k, taps[1][k], 0.0))
        tap = jnp.dot(toe.astype(BF16), t_tap, preferred_element_type=F32)
        wts_ref[0, rows, 0:w] = jnp.where(m_tap, tap, 0.0).astype(BF16)
        for k, e_j in enumerate((e_maps[0][L - 1 - j], e_maps[1][j])):
            st = jnp.dot(e_j.astype(BF16), t_st, preferred_element_type=F32)
            wts_ref[0, rows, w + k * w:w + (k + 1) * w] = jnp.where(m_st, st, 0.0).astype(BF16)

    rr = lax.broadcasted_iota(I32, (w, LANES), 0)
    cc = lax.broadcasted_iota(I32, (w, LANES), 1)
    t_row = one((div(rr, 8 * N) == div(cc, N)) & (mod(rr, N) == mod(cc, N)))
    r2 = lax.broadcasted_iota(I32, (w, w), 0)
    c2 = lax.broadcasted_iota(I32, (w, w), 1)
    m_row = div(mod(r2, 8 * N), N) == div(mod(c2, LANES), P)
    for d in range(2):
        ar, ai, _, _ = _s5_disc(colp_ref[0, d, 0], colp_ref[1, d, 0], colp_ref[2, d, 0])
        a1 = cm_ref[d, 0]
        swapped = pltpu.roll(a1, N, 0)
        a2 = jnp.where(r1 < N, swapped, -swapped)
        pw = _s5_powers(ar, ai, L)
        ks = [i + 1 for i in range(L)] if d == 0 else [L - i for i in range(L)]
        g = jnp.concatenate([a1 * pw[k][0] + a2 * pw[k][1] for k in ks], axis=1)
        corr = jnp.dot(t_row, g.astype(BF16), preferred_element_type=F32)
        wc_ref[0, d * w:(d + 1) * w, :] = jnp.where(m_row, corr, 0.0).astype(BF16)

    al = []
    for d in range(2):
        ar, ai, _, _ = _s5_disc(flatp_ref[0, d, 0], flatp_ref[1, d, 0], flatp_ref[2, d, 0])
        al.append(_s5_powers(ar, ai, L)[L])
    cur = al
    for k in range(S5_SEG):
        ap_ref[0, k:k + 1, :] = jnp.concatenate([cur[0][0], cur[0][1], cur[1][0], cur[1][1]], axis=1)
        cur = [(cr * br - ci * bi, cr * bi + ci * br) for (cr, ci), (br, bi) in zip(cur, al)]


def _s5_weights(lam_re, lam_im, log_step, b_re, b_im, c_re, c_im):
    L, G, N, P = S5_L, S5_G, S5_N, S5_P
    w = L * LANES
    step = jnp.broadcast_to(jnp.exp(log_step)[..., None], lam_re.shape)
    prm = jnp.stack([lam_re, lam_im, step])
    rowp = jnp.broadcast_to(prm[:, :, :, None, None, :], (3, 2, G, P, 2, N)).reshape(3, 2, S5_BLK, LANES, 2 * N)
    colp = jnp.broadcast_to(prm.transpose(0, 1, 3, 2)[:, :, None, :, :, None], (3, 2, 2, N, G, P))
    colp = colp.reshape(3, 2, 2 * N, S5_BLK, LANES).transpose(0, 1, 3, 2, 4)
    flatp = prm.reshape(3, 2, S5_BLK, 1, 8 * N)
    bt = jnp.stack([b_re, b_im], axis=3).transpose(0, 1, 4, 3, 2)
    bt = bt.reshape(2, S5_BLK, LANES, 2 * N)
    cm = jnp.stack([c_re, -c_im], axis=1).transpose(0, 1, 4, 2, 3)
    cm = cm.reshape(2, 2 * N, S5_BLK, LANES).transpose(0, 2, 1, 3)
    blk = lambda lead, a, b_: pl.BlockSpec(lead + (1, a, b_), lambda i: (0,) * len(lead) + (i, 0, 0))
    out = lambda a, b_: pl.BlockSpec((1, a, b_), lambda i: (i, 0, 0))
    return pl.pallas_call(
        _s5_prep_body, grid=(S5_BLK,),
        in_specs=[blk((3, 2), LANES, LANES), blk((3, 2), LANES, LANES), blk((3, 2), 1, 8 * N),
                  blk((2,), LANES, LANES), blk((2,), LANES, LANES)],
        out_specs=[out(w, w + S5_SW), out(S5_SW, w), out(S5_SEG, S5_SW)],
        out_shape=[jax.ShapeDtypeStruct((S5_BLK, w, w + S5_SW), BF16),
                   jax.ShapeDtypeStruct((S5_BLK, S5_SW, w), BF16),
                   jax.ShapeDtypeStruct((S5_BLK, S5_SEG, S5_SW), F32)],
        compiler_params=_cp(("parallel",)), name="s5_prep",
    )(rowp, colp, flatp, bt, cm)


def _mixout_body(xp_ref, xs_ref, mod_ref, g1_ref, g2_ref, wbg_ref, bbg_ref, yap_ref, yas_ref,
                 y4p_ref, y4s_ref, u4p_ref, u4s_ref, *rest, tiles_p):
    i = pl.program_id(0)

    @pl.when(i < tiles_p)
    def _():
        _mixout_tile(xp_ref, mod_ref, g1_ref, g2_ref, wbg_ref, bbg_ref, yap_ref, y4p_ref, u4p_ref, *rest)

    @pl.when(i >= tiles_p)
    def _():
        _mixout_tile(xs_ref, mod_ref, g1_ref, g2_ref, wbg_ref, bbg_ref, yas_ref, y4s_ref, u4s_ref, *rest)


def _mixout_tile(x_ref, mod_ref, g1_ref, g2_ref, wbg_ref, bbg_ref, ya_ref, y4_ref, u4_ref,
                 dsk_ref, wglu_ref, bglu_ref, wso_ref, wlo_ref, wo_ref, wrt_ref, brt_ref,
                 x1_ref, h2_ref, aff_ref):
    m = mod_ref[0]
    sh1, sc1, gt1 = m[:, 0:D_MODEL], m[:, D_MODEL:2 * D_MODEL], m[:, 2 * D_MODEL:3 * D_MODEL]
    sh2, sc2 = m[:, 3 * D_MODEL:4 * D_MODEL], m[:, 4 * D_MODEL:5 * D_MODEL]
    x = x_ref[...]
    h = _norm_mod(x, g1_ref[...], sc1, sh1)
    gates = _sigmoid(jnp.dot(h.astype(BF16), wbg_ref[...], preferred_element_type=F32) + bbg_ref[...])

    def assemble(ref):
        return jnp.concatenate([ref[k] for k in range(S5_BLK)], axis=1)

    ys = assemble(y4_ref) + dsk_ref[...] * assemble(u4_ref)
    v = _gelu(ys)
    ob = v * _sigmoid(jnp.dot(v.astype(BF16), wglu_ref[...], preferred_element_type=F32) + bglu_ref[...])
    merged = (gates[:, :D_MODEL] * jnp.dot(ya_ref[...], wlo_ref[...], preferred_element_type=F32)
              + gates[:, D_MODEL:] * jnp.dot(ob.astype(BF16), wso_ref[...], preferred_element_type=F32))
    mix = jnp.dot(merged.astype(BF16), wo_ref[...], preferred_element_type=F32)
    x1 = x + gt1 * mix
    x1_ref[...] = x1
    h2 = _norm_mod(x1, g2_ref[...], sc2, sh2)
    h2_ref[...] = h2
    logits = _dot3(wrt_ref[...], h2, dims=(((1,), (1,)), ((), ()))) + brt_ref[...]
    mx = jnp.max(logits, axis=0, keepdims=True)
    ex = jnp.exp(logits - mx)
    aff_ref[...] = ex / jnp.sum(ex, axis=0, keepdims=True)


def _mixout_call(xp, xs, mod3, g1, g2, wbg, bbg, ya_p, ya_s, y4_p, y4_s, u4_p, u4_s,
                 dsk, wglu, bglu, wso, wlo, wo, wrt, brt, mod_row, tm):
    n_p, n_s = xp.shape[0], xs.shape[0]
    tiles_p = n_p // tm
    n = n_p + n_s
    full = lambda shape: pl.BlockSpec(shape, lambda i: (0,) * len(shape))
    at_p = lambda i: jnp.minimum(i, tiles_p - 1)
    at_s = lambda i: jnp.maximum(i - tiles_p, 0)
    tok = lambda w, at: pl.BlockSpec((tm, w), lambda i: (at(i), 0))
    s5 = lambda at: pl.BlockSpec((S5_BLK, tm, LANES), lambda i: (0, at(i), 0))
    out = lambda w: pl.BlockSpec((tm, w), lambda i: (i, 0))
    return pl.pallas_call(
        functools.partial(_mixout_body, tiles_p=tiles_p),
        grid=(n // tm,),
        in_specs=[tok(D_MODEL, at_p), tok(D_MODEL, at_s),
                  pl.BlockSpec((1, 1, 6 * D_MODEL), lambda i: (mod_row(i), 0, 0)),
                  full((1, D_MODEL)), full((1, D_MODEL)), full((D_MODEL, 2 * D_MODEL)),
                  full((1, 2 * D_MODEL)), tok(D_LRU, at_p), tok(D_LRU, at_s),
                  s5(at_p), s5(at_s), s5(at_p), s5(at_s), full((1, D_S5)),
                  full((D_S5, D_S5)), full((1, D_S5)), full((D_S5, D_MODEL)), full((D_LRU, D_MODEL)),
                  full((D_MODEL, D_MODEL)), full((N_EXPERTS, D_MODEL)), full((N_EXPERTS, 1))],
        out_specs=[out(D_MODEL), out(D_MODEL), pl.BlockSpec((N_EXPERTS, tm), lambda i: (0, i))],
        out_shape=[jax.ShapeDtypeStruct((n, D_MODEL), F32), jax.ShapeDtypeStruct((n, D_MODEL), F32),
                   jax.ShapeDtypeStruct((N_EXPERTS, n), F32)],
        compiler_params=_cp(("arbitrary",)), name="mixout",
    )(xp, xs, mod3, g1, g2, wbg, bbg, ya_p, ya_s, y4_p, y4_s, u4_p, u4_s,
      dsk, wglu, bglu, wso, wlo, wo, wrt, brt)


def _select_body(aff_ref, gate_ref, pos_ref, offs_ref, *, n_tok, cap):
    aff = aff_ref[...]
    capf = float(cap)

    def bis(_, lh):
        lo, hi = lh
        mid = lo + ((hi - lo + 1) >> 1)
        cnt = jnp.sum(jnp.where(aff >= pltpu.bitcast(mid, F32), 1.0, 0.0), axis=1, keepdims=True)
        ok = cnt >= capf
        return jnp.where(ok, mid, lo), jnp.where(ok, hi, mid - 1)

    lo0 = jnp.zeros((N_EXPERTS, 1), I32)
    hi0 = jnp.full((N_EXPERTS, 1), 0x7F800000, I32)
    thr_bits, _ = lax.fori_loop(0, 31, bis, (lo0, hi0))
    thr = pltpu.bitcast(thr_bits, F32)
    need = capf - jnp.sum(jnp.where(aff > thr, 1.0, 0.0), axis=1, keepdims=True)

    r = lax.broadcasted_iota(I32, (TOK_TILE, TOK_TILE), 0)
    c = lax.broadcasted_iota(I32, (TOK_TILE, TOK_TILE), 1)
    tri = jnp.where(r < c, 1.0, 0.0).astype(BF16)
    lane = lax.broadcasted_iota(I32, (N_EXPERTS, LANES), 1)
    n_tiles = n_tok // TOK_TILE
    run_eq = jnp.zeros((N_EXPERTS, 1), F32)
    run_sel = jnp.zeros((N_EXPERTS, 1), F32)
    offs = jnp.zeros((N_EXPERTS, LANES), F32)
    for t in range(n_tiles):
        sl = slice(t * TOK_TILE, (t + 1) * TOK_TILE)
        aff_t = aff_ref[:, sl]
        is_eq = aff_t == thr
        eq_t = jnp.where(is_eq, 1.0, 0.0)
        rank_eq = jnp.dot(eq_t.astype(BF16), tri, preferred_element_type=F32) + run_eq
        cand = (aff_t > thr) | (is_eq & (rank_eq < need))
        pos_t = jnp.dot(jnp.where(cand, 1.0, 0.0).astype(BF16), tri, preferred_element_type=F32) + run_sel
        sel_t = cand & (pos_t < capf)
        sel_f = jnp.where(sel_t, 1.0, 0.0)
        offs = jnp.where(lane == t, run_sel, offs)
        gate_ref[:, sl] = jnp.where(sel_t, aff_t, 0.0)
        pos_ref[:, sl] = jnp.where(sel_t, pos_t, -1.0).astype(I32)
        run_eq = run_eq + jnp.sum(eq_t, axis=1, keepdims=True)
        run_sel = run_sel + jnp.sum(sel_f, axis=1, keepdims=True)
    offs = jnp.where(lane >= n_tiles, run_sel, offs)
    offs_ref[...] = offs.astype(I32)


def _select_call(aff_t, cap, n_tok):
    n_path = aff_t.shape[1] // n_tok
    per_p = lambda a, b: pl.BlockSpec((None, a, b), lambda p: (p, 0, 0))
    return pl.pallas_call(
        functools.partial(_select_body, n_tok=n_tok, cap=cap),
        grid=(n_path,),
        in_specs=[pl.BlockSpec((N_EXPERTS, n_tok), lambda p: (0, p))],
        out_specs=[per_p(N_EXPERTS, n_tok), per_p(N_EXPERTS, n_tok), per_p(N_EXPERTS, LANES)],
        out_shape=[jax.ShapeDtypeStruct((n_path, N_EXPERTS, n_tok), F32),
                   jax.ShapeDtypeStruct((n_path, N_EXPERTS, n_tok), I32),
                   jax.ShapeDtypeStruct((n_path, N_EXPERTS, LANES), I32)],
        compiler_params=_cp(("parallel",)), name="select",
    )(aff_t)


def _invert_body(offs_ref, pos_ref, gate_ref, idx_ref, gcol_ref, acc_i, acc_g, *, n_tiles, cap):
    path = pl.program_id(0)
    base = (path * N_EXPERTS + pl.program_id(1)) * LANES
    acc_i[...] = jnp.zeros(acc_i.shape, F32)
    acc_g[...] = jnp.zeros(acc_g.shape, F32)
    jrow = lax.broadcasted_iota(I32, (SLOT_CHUNK, LANES), 0)
    lane = lax.broadcasted_iota(I32, (1, LANES), 1)

    def per_tile(t, _):
        first = (offs_ref[base + t] >> 3) << 3
        prow = pos_ref[0, 0, pl.ds(t, 1), :]
        grow = gate_ref[0, 0, pl.ds(t, 1), :]
        tok = (lane + (t * TOK_TILE + 1)).astype(F32)

        def window(w, _):
            start = pl.multiple_of(first + w * SLOT_CHUNK, SUBLANES)
            rows = pl.ds(start, SLOT_CHUNK)
            hit = prow == (jrow + start)
            acc_i[rows, :] = jnp.maximum(acc_i[rows, :], jnp.where(hit, tok, 0.0))
            acc_g[rows, :] = jnp.maximum(acc_g[rows, :], jnp.where(hit, grow, 0.0))
            return 0

        n_win = (offs_ref[base + t + 1] - first + SLOT_CHUNK - 1) >> (SLOT_CHUNK.bit_length() - 1)
        lax.fori_loop(0, n_win, window, 0)
        return 0

    lax.fori_loop(0, n_tiles, per_tile, 0, unroll=2)
    hit_i = acc_i[0:cap, :]
    hi = hit_i.astype(BF16)
    mid = (hit_i - hi.astype(F32)).astype(BF16)
    lo = (hit_i - hi.astype(F32) - mid.astype(F32)).astype(BF16)
    ones = jnp.ones((SUBLANES, LANES), BF16)
    nt = (((1,), (1,)), ((), ()))
    tok1 = sum(lax.dot_general(ones, part, nt, preferred_element_type=F32) for part in (hi, mid, lo))
    idx_ref[0, 0] = tok1[0:1, :].astype(I32) - 1 + path * (n_tiles * TOK_TILE)
    gcol_ref[0, 0] = jnp.max(acc_g[0:cap, :], axis=1, keepdims=True)


def _invert_call(offs, pos, gate, cap):
    n_path, _, n_tok = pos.shape
    n_tiles = n_tok // TOK_TILE
    pos4 = pos.reshape(n_path, N_EXPERTS, n_tiles, TOK_TILE)
    gate4 = gate.reshape(n_path, N_EXPERTS, n_tiles, TOK_TILE)
    per_e = lambda a, b: pl.BlockSpec((1, 1, a, b), lambda p, e, o: (p, e, 0, 0))
    return pl.pallas_call(
        functools.partial(_invert_body, n_tiles=n_tiles, cap=cap),
        grid_spec=pltpu.PrefetchScalarGridSpec(
            num_scalar_prefetch=1, grid=(n_path, N_EXPERTS),
            in_specs=[per_e(n_tiles, TOK_TILE), per_e(n_tiles, TOK_TILE)],
            out_specs=[per_e(1, cap), per_e(cap, 1)],
            scratch_shapes=[pltpu.VMEM((cap + SLOT_CHUNK, LANES), F32)] * 2),
        out_shape=[jax.ShapeDtypeStruct((n_path, N_EXPERTS, 1, cap), I32),
                   jax.ShapeDtypeStruct((n_path, N_EXPERTS, cap, 1), F32)],
        compiler_params=_cp(("parallel", "parallel")), name="invert",
    )(offs.reshape(-1), pos4, gate4)


def _ffn_body(idx0_ref, idxn_ref, gcol_ref, wg_ref, wu_ref, wd_ref, h_hbm, out_ref,
              xe, wgb, wub, wdb, sem, *, cap):
    e = pl.program_id(0)
    p = pl.program_id(1)
    nxt = 1 - p
    step = e * 2 + p

    def issue_rows(idx_ref, s, j0, n):
        for j in range(j0, j0 + n):
            pltpu.make_async_copy(h_hbm.at[pl.ds(idx_ref[0, 0, 0, j], 1), :],
                                  xe.at[s, pl.ds(j, 1), :], sem.at[s]).start()

    @pl.when(step == 0)
    def _():
        def issue(j, _):
            pltpu.make_async_copy(h_hbm.at[pl.ds(idx0_ref[0, 0, 0, j], 1), :],
                                  xe.at[0, pl.ds(j, 1), :], sem.at[0]).start()
            return 0
        lax.fori_loop(0, cap, issue, 0, unroll=8)

    @pl.when(p == 0)
    def _():
        wgb[...] = wg_ref[0].astype(BF16)
        wub[...] = wu_ref[0].astype(BF16)
        wdb[...] = wd_ref[0].astype(BF16)

    pltpu.make_async_copy(h_hbm.at[pl.ds(0, cap), :], xe.at[p], sem.at[p]).wait()

    ch = ROW_CHUNK
    n_ch = cap // ch
    per = -(-cap // (n_ch - 1))
    for c in range(n_ch):
        sl = slice(c * ch, (c + 1) * ch)
        xb = xe[p, sl, :].astype(BF16)
        issue_rows(idxn_ref, nxt, min(c * per, cap), min((c + 1) * per, cap) - min(c * per, cap))
        g = jnp.dot(xb, wgb[...], preferred_element_type=F32)
        u = jnp.dot(xb, wub[...], preferred_element_type=F32)
        hid = (g * _sigmoid(g)) * u
        ye = jnp.dot(hid.astype(BF16), wdb[...], preferred_element_type=F32)
        out_ref[0, 0, sl, :] = ye * gcol_ref[0, 0, sl, :]

    @pl.when(step == 2 * N_EXPERTS - 1)
    def _():
        pltpu.make_async_copy(h_hbm.at[pl.ds(0, cap), :], xe.at[nxt], sem.at[nxt]).wait()


def _ffn_call(idx, gcol, w_gate, w_up, w_down, h2, cap):
    d_e = w_gate.shape[2]
    wspec = lambda a, b: pl.BlockSpec((1, a, b), lambda e, p: (e, 0, 0))
    return pl.pallas_call(
        functools.partial(_ffn_body, cap=cap),
        grid=(N_EXPERTS, 2),
        in_specs=[pl.BlockSpec((1, 1, 1, cap), lambda e, p: (0, 0, 0, 0), memory_space=pltpu.SMEM),
                  pl.BlockSpec((1, 1, 1, cap),
                               lambda e, p: (1 - p, jnp.minimum(e + p, N_EXPERTS - 1), 0, 0),
                               memory_space=pltpu.SMEM),
                  pl.BlockSpec((1, 1, cap, 1), lambda e, p: (p, e, 0, 0)),
                  wspec(D_MODEL, d_e), wspec(D_MODEL, d_e), wspec(d_e, D_MODEL),
                  pl.BlockSpec(memory_space=pl.ANY)],
        out_specs=pl.BlockSpec((1, 1, cap, D_MODEL), lambda e, p: (p, e, 0, 0)),
        out_shape=jax.ShapeDtypeStruct((2, N_EXPERTS, cap, D_MODEL), F32),
        scratch_shapes=[pltpu.VMEM((2, cap, D_MODEL), F32), pltpu.VMEM((D_MODEL, d_e), BF16),
                        pltpu.VMEM((D_MODEL, d_e), BF16), pltpu.VMEM((d_e, D_MODEL), BF16),
                        pltpu.SemaphoreType.DMA((2,))],
        compiler_params=_cp(("arbitrary", "arbitrary")), name="ffn",
    )(idx, idx, gcol, w_gate, w_up, w_down, h2)


def _combine_body(offs_ref, x1_ref, mod_ref, gf_ref, pos_ref, yb_hbm, y_ref, wins, acc_ref, sem,
                  *, cap, path, win_narrow):
    t = pl.program_id(0)
    n_t = pl.num_programs(0)
    slot = t % 2
    row0 = path * N_EXPERTS * LANES
    per = COMBINE_TILE // TOK_TILE

    def first_slot(tt, e):
        return offs_ref[row0 + e * LANES + tt * per]

    def kmax_of(tt):
        k = 0
        for e in range(N_EXPERTS):
            k = jnp.maximum(k, first_slot(tt + 1, e) - first_slot(tt, e))
        return k

    def geometry(tt, e, w, win):
        fetch = win + SUBLANES
        lo = first_slot(tt, e) + w * win
        start = pl.multiple_of(jnp.minimum((lo >> 3) << 3, cap - fetch), SUBLANES)
        return lo, start, fetch

    def issue(tt, w, s, win):
        for e in range(N_EXPERTS):
            _, start, fetch = geometry(tt, e, w, win)
            pltpu.make_async_copy(yb_hbm.at[path, e, pl.ds(start, fetch), :],
                                  wins.at[s, pl.ds(e * fetch, fetch), :], sem.at[s]).start()

    def drain(s, win):
        rows = N_EXPERTS * (win + SUBLANES)
        pltpu.make_async_copy(wins.at[1 - s, pl.ds(0, rows), :], wins.at[s, pl.ds(0, rows), :],
                              sem.at[s]).wait()

    def expand(w, s, win):
        sub = lax.broadcasted_iota(I32, (win + SUBLANES, COMBINE_TILE), 0)
        rows = []
        for e in range(N_EXPERTS):
            lo, start, _ = geometry(t, e, w, win)
            pos = pos_ref[e:e + 1, :]
            in_round = jnp.logical_and(pos >= lo, pos < lo + win)
            rows.append(jnp.where(jnp.logical_and(pos - start == sub, in_round), 1.0, 0.0))
        onehot = jnp.concatenate(rows, axis=0).T.astype(BF16)
        data = wins[s, 0:N_EXPERTS * (win + SUBLANES), :]
        hi = data.astype(BF16)
        lo_part = (data - hi.astype(F32)).astype(BF16)
        return (jnp.dot(onehot, hi, preferred_element_type=F32)
                + jnp.dot(onehot, lo_part, preferred_element_type=F32))

    def issue_first(tt, s):
        wide = kmax_of(tt) > win_narrow

        @pl.when(wide)
        def _():
            issue(tt, 0, s, COMBINE_WIN_WIDE)

        @pl.when(jnp.logical_not(wide))
        def _():
            issue(tt, 0, s, win_narrow)

    @pl.when(t == 0)
    def _():
        issue_first(0, 0)

    @pl.when(t + 1 < n_t)
    def _():
        issue_first(t + 1, 1 - slot)

    kmax = kmax_of(t)

    @pl.when(kmax > win_narrow)
    def _():
        drain(slot, COMBINE_WIN_WIDE)
        acc_ref[...] = expand(0, slot, COMBINE_WIN_WIDE)

    @pl.when(kmax <= win_narrow)
    def _():
        drain(slot, win_narrow)
        acc_ref[...] = expand(0, slot, win_narrow)

    def more(w, _):
        issue(t, w, slot, COMBINE_WIN_WIDE)
        drain(slot, COMBINE_WIN_WIDE)
        acc_ref[...] += expand(w, slot, COMBINE_WIN_WIDE)
        return 0
    lax.fori_loop(1, (kmax + COMBINE_WIN_WIDE - 1) >> (COMBINE_WIN_WIDE.bit_length() - 1), more, 0)
    acc = acc_ref[...]

    gt2 = mod_ref[0][:, 5 * D_MODEL:6 * D_MODEL]
    x2 = x1_ref[...] + gt2 * acc
    ms = jnp.mean(x2 * x2, axis=-1, keepdims=True)
    y_ref[...] = x2 * lax.rsqrt(ms + EPS) * gf_ref[...]


def _combine_call(offs, pos, x1, mod3, gf, ybuf, mod_row, cap, path, n, win_narrow=COMBINE_WIN):
    first = path * n // COMBINE_TILE
    return pl.pallas_call(
        functools.partial(_combine_body, cap=cap, path=path, win_narrow=win_narrow),
        grid_spec=pltpu.PrefetchScalarGridSpec(
            num_scalar_prefetch=1, grid=(n // COMBINE_TILE,),
            in_specs=[pl.BlockSpec((COMBINE_TILE, D_MODEL), lambda i, o: (first + i, 0)),
                      pl.BlockSpec((1, 1, 6 * D_MODEL), lambda i, o: (mod_row(i), 0, 0)),
                      pl.BlockSpec((1, D_MODEL), lambda i, o: (0, 0)),
                      pl.BlockSpec((None, N_EXPERTS, COMBINE_TILE), lambda i, o: (path, 0, i)),
                      pl.BlockSpec(memory_space=pl.ANY)],
            out_specs=pl.BlockSpec((COMBINE_TILE, D_MODEL), lambda i, o: (i, 0)),
            scratch_shapes=[pltpu.VMEM((2, N_EXPERTS * (COMBINE_WIN_WIDE + SUBLANES), D_MODEL), F32),
                            pltpu.VMEM((COMBINE_TILE, D_MODEL), F32),
                            pltpu.SemaphoreType.DMA((2,))]),
        out_shape=jax.ShapeDtypeStruct((n, D_MODEL), F32),
        compiler_params=_cp(("arbitrary",)), name="combine",
    )(offs.reshape(-1), x1, mod3, gf, pos, ybuf)


def _lru_gate_weights(wa, wx):
    hb = MXU_DIM // LRU_HEAD_DIM
    nb = LRU_HEADS // hb
    eye = jnp.eye(hb, dtype=F32)

    def bd(w):
        w5 = w.reshape(nb, hb, LRU_HEAD_DIM, LRU_HEAD_DIM)
        return jnp.einsum('khij,hg->khigj', w5, eye).reshape(nb, MXU_DIM, MXU_DIM)

    return jnp.concatenate([bd(wa[0]), bd(wa[1]), bd(wx[0]), bd(wx[1])], axis=2).astype(BF16)


def kernel(x_prompt, x_sample, state_lru, state_s5_re, state_s5_im, c, c_ctx, w_mod, b_mod, g_norm1, g_norm2, w_in, conv_w, conv_b, lru_wa, lru_ba, lru_wx, lru_bx, lru_lambda, s5_lambda_re, s5_lambda_im, s5_log_step, s5_b_re, s5_b_im, s5_c_re, s5_c_im, s5_d, s5_w_glu, s5_b_glu, w_lru_out, w_s5_out, w_branch_gate, b_branch_gate, w_o, w_router, b_router, w_e_gate, w_e_up, w_e_down, g_final):
    bp, tp, _ = x_prompt.shape
    bs, ts, _ = x_sample.shape
    n_p, n_s = bp * tp, bs * ts
    l = 0

    c_all = jnp.zeros((16, D_MODEL), F32).at[0].set(c_ctx).at[1:1 + bs].set(c)
    mod3 = _mod_call(c_all, w_mod[l], b_mod[l][None, :]).reshape(16, 1, 6 * D_MODEL)

    w_in_b = w_in[l].astype(BF16)
    w_main, w_s5in = w_in_b[:, :2 * D_LRU], w_in_b[:, 2 * D_LRU:]
    g1 = g_norm1[l][None, :]
    g2 = g_norm2[l][None, :]
    wg = _lru_gate_weights(lru_wa[l], lru_wx[l])
    wts, wc, apow = _s5_weights(s5_lambda_re[l], s5_lambda_im[l], s5_log_step[l],
                                s5_b_re[l], s5_b_im[l], s5_c_re[l], s5_c_im[l])
    wbg = w_branch_gate[l].astype(BF16)
    bbg = b_branch_gate[l][None, :]
    wglu = s5_w_glu[l].astype(BF16)
    wso = w_s5_out[l].astype(BF16)
    wlo = w_lru_out[l].astype(BF16)
    wo = w_o[l].astype(BF16)
    wrt = w_router[l].T
    brt = b_router[l][:, None]

    xp2 = x_prompt.reshape(n_p, D_MODEL)
    xs2 = x_sample.reshape(n_s, D_MODEL)
    ctx_row = lambda i: 0
    tm_in = 1024
    lat_row_in = lambda i: 1 + i // (ts // max(tm_in, ts))

    def s5_state(re, im):
        def part(a, d):
            return a[:, d].reshape(-1, S5_BLK, 8 * S5_N).transpose(1, 0, 2)
        return jnp.concatenate([part(re, 0), part(im, 0), part(re, 1), part(im, 1)], axis=2)

    xb_p, gg_p, us4_p = _inproj_call(xp2, mod3, g1, w_main, w_s5in, ctx_row, max(tm_in, tp), tp)
    ya_p, lru_fin = _lru_call(xb_p, gg_p, conv_w[l], conv_b[l][None, :], wg, lru_ba[l], lru_bx[l],
                              lru_lambda[l], jnp.zeros((bp, 2, D_LRU), F32), bp, tp)
    y4_p, s5_fin = _s5_call(us4_p, wts, wc, apow, jnp.zeros((S5_BLK, bp, S5_SW), F32), bp, tp,
                            with_h0=False, col_major=False)
    xb_s, gg_s, us4_s = _inproj_call(xs2, mod3, g1, w_main, w_s5in, lat_row_in, max(tm_in, ts), ts)
    ya_s, _ = _lru_call(xb_s, gg_s, conv_w[l], conv_b[l][None, :], wg, lru_ba[l], lru_bx[l],
                        lru_lambda[l], state_lru[:, l], bs, ts)
    y4_s, _ = _s5_call(us4_s, wts, wc, apow, s5_state(state_s5_re[:, l], state_s5_im[:, l]), bs, ts,
                       with_h0=True, col_major=True)

    tm_mo = 512
    tiles_p = n_p // tm_mo
    x1, h2, aff = _mixout_call(
        xp2, xs2, mod3, g1, g2, wbg, bbg, ya_p, ya_s, y4_p, y4_s, us4_p, us4_s,
        s5_d[l][None, :], wglu, s5_b_glu[l][None, :], wso, wlo, wo, wrt, brt,
        mod_row=lambda i: jnp.where(i < tiles_p, 0, 1 + (i - tiles_p) // (ts // tm_mo)), tm=tm_mo)

    assert n_p == n_s
    cap = (CAPACITY_FACTOR * n_p) // N_EXPERTS
    gate, pos, offs = _select_call(aff, cap, n_p)
    idx, gcol = _invert_call(offs, pos, gate, cap)
    ybuf = _ffn_call(idx, gcol, w_e_gate[l], w_e_up[l], w_e_down[l], h2, cap)
    gf = g_final[None, :]
    y_p = _combine_call(offs, pos, x1, mod3, gf, ybuf, ctx_row, cap, 0, n_p, win_narrow=COMBINE_WIN - 16)
    y_s = _combine_call(offs, pos, x1, mod3, gf, ybuf, lambda i: 1 + i // (ts // COMBINE_TILE), cap, 1, n_s)

    new_lru = lru_fin[:, None]
    sf = s5_fin.transpose(1, 0, 2).reshape(bp, S5_BLK, 2, 2, 8, S5_N)
    sf = sf.transpose(0, 2, 3, 1, 4, 5).reshape(bp, 2, 2, S5_G, S5_N)
    new_s5r = sf[:, :, 0][:, None]
    new_s5i = sf[:, :, 1][:, None]
    return (y_p.reshape(bp, tp, D_MODEL), y_s.reshape(bs, ts, D_MODEL), new_lru, new_s5r, new_s5i)
```

```python
import functools
import math

import jax
import jax.numpy as jnp
from jax import lax
from jax.experimental import pallas as pl
from jax.experimental.pallas import tpu as pltpu

F32 = jnp.float32
BF16 = jnp.bfloat16
I32 = jnp.int32

D_MODEL = 1024
D_LRU = 1024
LRU_HEADS = 16
LRU_HEAD_DIM = 64
LRU_C = 8.0
CONV_W = 4
D_S5 = 512
S5_P = 16
S5_G = 32
S5_N = 64
GRID_W = 64
N_EXPERTS = 16
CAPACITY_FACTOR = 2
EPS = 1e-6

LANES = 128
SUBLANES = 8
MXU_DIM = 256
ROW_CHUNK = 256
S5_L = 8
S5_BLK = 4
S5_SW = 4 * (LANES // S5_P) * S5_N
S5_SEG = 32
TOK_TILE = 128
SLOT_CHUNK = 32
COMBINE_TILE = 256
COMBINE_WIN = 64
COMBINE_WIN_WIDE = 128
VMEM_LIMIT = 56 * 1024 * 1024


def _cp(sem, vmem=VMEM_LIMIT):
    return pltpu.CompilerParams(dimension_semantics=sem, vmem_limit_bytes=vmem)


def _split2(a):
    hi = a.astype(BF16)
    lo = (a - hi.astype(F32)).astype(BF16)
    return hi, lo


def _dot3(a, b, dims=(((1,), (0,)), ((), ()))):
    ah, al = _split2(a)
    bh, bl = _split2(b)
    d = functools.partial(lax.dot_general, dimension_numbers=dims, preferred_element_type=F32)
    return d(ah, bh) + (d(al, bh) + d(ah, bl))


def _sigmoid(x):
    return 0.5 * jnp.tanh(0.5 * x) + 0.5


def _gelu(x):
    c = math.sqrt(2.0 / math.pi)
    half = 0.5 * x
    return half + half * jnp.tanh(x * (c + (c * 0.044715) * (x * x)))


def _norm_mod(x, g, scale, shift):
    ms = jnp.mean(x * x, axis=-1, keepdims=True)
    return (x * lax.rsqrt(ms + EPS)) * (g * (1.0 + scale)) + shift


def _mod_body(c_ref, w_ref, b_ref, o_ref):
    c = c_ref[...]
    s = c * _sigmoid(c)
    o_ref[...] = _dot3(s, w_ref[...]) + b_ref[...]


def _mod_call(c_all, w_mod, b_mod):
    n = w_mod.shape[1]
    tn = 1536
    return pl.pallas_call(
        _mod_body,
        grid=(n // tn,),
        in_specs=[pl.BlockSpec((16, D_MODEL), lambda j: (0, 0)),
                  pl.BlockSpec((D_MODEL, tn), lambda j: (0, j)),
                  pl.BlockSpec((1, tn), lambda j: (0, j))],
        out_specs=pl.BlockSpec((16, tn), lambda j: (0, j)),
        out_shape=jax.ShapeDtypeStruct((16, n), F32),
        compiler_params=_cp(("arbitrary",)),
        name="mod",
    )(c_all, w_mod, b_mod)


def _inproj_body(x_ref, mod_ref, g_ref, w_ref, ws_ref, xb_ref, gg_ref, us_ref, *, seq_len):
    m = mod_ref[0]
    h = _norm_mod(x_ref[...], g_ref[...], m[:, D_MODEL:2 * D_MODEL], m[:, 0:D_MODEL])
    hb = h.astype(BF16)
    z = jnp.dot(hb, w_ref[...], preferred_element_type=F32)
    seg = seq_len // SUBLANES
    for s in range(x_ref.shape[0] // seq_len):
        for k in range(SUBLANES):
            t0 = s * seq_len + k * seg
            for c in range(D_LRU // LANES):
                xb_ref[c, pl.ds(s * seq_len + k, seg, stride=SUBLANES), :] = (
                    z[t0:t0 + seg, c * LANES:(c + 1) * LANES])
    gg_ref[...] = _gelu(z[:, D_LRU:]).astype(BF16)
    us = jnp.dot(hb, ws_ref[...], preferred_element_type=F32)
    for k in range(S5_BLK):
        us_ref[k] = us[:, k * LANES:(k + 1) * LANES]


def _inproj_call(x2d, mod3, g1, w_main, w_s5, mod_row, tm, seq_len):
    n = x2d.shape[0]
    return pl.pallas_call(
        functools.partial(_inproj_body, seq_len=seq_len), grid=(n // tm,),
        in_specs=[pl.BlockSpec((tm, D_MODEL), lambda i: (i, 0)),
                  pl.BlockSpec((1, 1, 6 * D_MODEL), lambda i: (mod_row(i), 0, 0)),
                  pl.BlockSpec((1, D_MODEL), lambda i: (0, 0)),
                  pl.BlockSpec((D_MODEL, 2 * D_LRU), lambda i: (0, 0)),
                  pl.BlockSpec((D_MODEL, D_S5), lambda i: (0, 0))],
        out_specs=[pl.BlockSpec((D_LRU // LANES, tm, LANES), lambda i: (0, i, 0)),
                   pl.BlockSpec((tm, D_LRU), lambda i: (i, 0)),
                   pl.BlockSpec((S5_BLK, tm, LANES), lambda i: (0, i, 0))],
        out_shape=[jax.ShapeDtypeStruct((D_LRU // LANES, n, LANES), F32),
                   jax.ShapeDtypeStruct((n, D_LRU), BF16),
                   jax.ShapeDtypeStruct((S5_BLK, n, LANES), F32)],
        compiler_params=_cp(("parallel",)), name="inproj",
    )(x2d, mod3, g1, w_main, w_s5)


def _lru_body(xb_ref, gg_ref, cw_ref, cb_ref, wg_ref, ba_ref, bx_ref, lam_ref, h0_ref,
              ya_ref, fin_ref, xpad, a_f, b_f, a_b, b_b, hs, *, T):
    seg = T // SUBLANES
    ch = ROW_CHUNK
    nch = T // ch
    n_slab = D_LRU // LANES
    pad = 2 * SUBLANES
    sub = lax.broadcasted_iota(I32, (SUBLANES, D_LRU), 0)

    def rows_of(r0, n):
        return jnp.concatenate([xb_ref[c, r0:r0 + n, :] for c in range(n_slab)], axis=1)

    def from_prev_segment(tile):
        return jnp.where(sub >= 1, pltpu.roll(tile, 1, 0), 0.0)

    def from_next_segment(tile):
        return jnp.where(sub < SUBLANES - 1, pltpu.roll(tile, SUBLANES - 1, 0), 0.0)

    xpad[0:SUBLANES, :] = from_prev_segment(rows_of((seg - 2) * SUBLANES, SUBLANES))
    xpad[SUBLANES:pad, :] = from_prev_segment(rows_of((seg - 1) * SUBLANES, SUBLANES))
    xpad[pad + T:pad + T + SUBLANES, :] = from_next_segment(rows_of(0, SUBLANES))
    for c in range(nch):
        xpad[pad + c * ch:pad + (c + 1) * ch, :] = rows_of(c * ch, ch)

    nl = -lam_ref[...]
    softplus = jnp.maximum(nl, 0.0) + jnp.log1p(jnp.exp(-jnp.abs(nl)))
    decay = (-LRU_C) * softplus
    scr = ((a_f, b_f), (a_b, b_b))

    for c in range(nch):
        t0 = c * ch
        xc = cb_ref[...] + sum(
            xpad[t0 + k * SUBLANES:t0 + k * SUBLANES + ch, :] * cw_ref[k:k + 1, :]
            for k in range(CONV_W))
        xcb = xc.astype(BF16)
        w = MXU_DIM
        pre = [jnp.dot(xcb[:, kb * w:(kb + 1) * w], wg_ref[kb], preferred_element_type=F32)
               for kb in range(D_LRU // w)]
        for d in range(2):
            ra = jnp.concatenate([p[:, d * w:(d + 1) * w] for p in pre], axis=1)
            gx = jnp.concatenate([p[:, (2 + d) * w:(3 + d) * w] for p in pre], axis=1)
            r = _sigmoid(ra + ba_ref[d:d + 1, :])
            gi = _sigmoid(gx + bx_ref[d:d + 1, :])
            a = jnp.exp(r * decay[d:d + 1, :])
            scr[d][0][t0:t0 + ch, :] = a
            scr[d][1][t0:t0 + ch, :] = jnp.exp(0.5 * jnp.log(1.0 - a * a)) * gi * xc

    def local_scan(a_ref, u_ref, reverse):
        def step(i, carry):
            h, p = carry
            r0 = pl.multiple_of((seg - 1 - i if reverse else i) * SUBLANES, SUBLANES)
            a = a_ref[pl.ds(r0, SUBLANES), :]
            h = a * h + u_ref[pl.ds(r0, SUBLANES), :]
            p = a * p
            u_ref[pl.ds(r0, SUBLANES), :] = h
            a_ref[pl.ds(r0, SUBLANES), :] = p
            return h, p
        init = (jnp.zeros((SUBLANES, D_LRU), F32), jnp.ones((SUBLANES, D_LRU), F32))
        return lax.fori_loop(0, seg, step, init, unroll=2)

    def carry_in(h_end, p_end, h0_row, reverse):
        edge = sub == (SUBLANES - 1 if reverse else 0)
        shift = SUBLANES - 1 if reverse else 1
        c = jnp.broadcast_to(h0_row, (SUBLANES, D_LRU))
        for _ in range(SUBLANES - 1):
            c = jnp.where(edge, h0_row, pltpu.roll(h_end + p_end * c, shift, 0))
        return c

    hf_end, pf_end = local_scan(a_f, b_f, False)
    hb_end, pb_end = local_scan(a_b, b_b, True)
    c_f = carry_in(hf_end, pf_end, h0_ref[0, 0:1, :], False)
    c_b = carry_in(hb_end, pb_end, h0_ref[0, 1:2, :], True)
    fin_ref[0, 0:1, :] = (hf_end + pf_end * c_f)[SUBLANES - 1:SUBLANES, :]
    fin_ref[0, 1:2, :] = (hb_end + pb_end * c_b)[0:1, :]

    for c in range(nch):
        sl = slice(c * ch, (c + 1) * ch)
        tile3 = (ch // SUBLANES, SUBLANES, D_LRU)
        hsum = ((b_f[sl, :].reshape(tile3) + a_f[sl, :].reshape(tile3) * c_f[None])
                + (b_b[sl, :].reshape(tile3) + a_b[sl, :].reshape(tile3) * c_b[None])).reshape(ch, D_LRU)
        for j in range(n_slab):
            hs[j, sl, :] = hsum[:, j * LANES:(j + 1) * LANES]

    for k in range(SUBLANES):
        rows = slice(k * seg, (k + 1) * seg)
        h_seg = jnp.concatenate([hs[j, pl.ds(k, seg, stride=SUBLANES), :] for j in range(n_slab)], axis=1)
        ya_ref[0, rows, :] = (gg_ref[0, rows, :].astype(F32) * h_seg).astype(BF16)


def _lru_call(xb8, gg, conv_w, conv_b, wg, ba, bx, lam, h0, n_seq, T):
    gg3 = gg.reshape(n_seq, T, D_LRU)
    n_slab = D_LRU // LANES
    full = lambda shape: pl.BlockSpec(shape, lambda i: (0,) * len(shape))
    seq = lambda shape: pl.BlockSpec(shape, lambda i: (i,) + (0,) * (len(shape) - 1))
    ya, fin = pl.pallas_call(
        functools.partial(_lru_body, T=T),
        grid=(n_seq,),
        in_specs=[pl.BlockSpec((n_slab, T, LANES), lambda i: (0, i, 0)), seq((1, T, D_LRU)),
                  full((CONV_W, D_LRU)), full((1, D_LRU)),
                  full((D_LRU // MXU_DIM, MXU_DIM, 4 * MXU_DIM)), full((2, D_LRU)), full((2, D_LRU)),
                  full((2, D_LRU)),
                  seq((1, 2, D_LRU))],
        out_specs=[seq((1, T, D_LRU)), seq((1, 2, D_LRU))],
        out_shape=[jax.ShapeDtypeStruct((n_seq, T, D_LRU), BF16),
                   jax.ShapeDtypeStruct((n_seq, 2, D_LRU), F32)],
        scratch_shapes=[pltpu.VMEM((T + 3 * SUBLANES, D_LRU), F32)] + [pltpu.VMEM((T, D_LRU), F32)] * 4
                       + [pltpu.VMEM((n_slab, T, LANES), F32)],
        compiler_params=_cp(("parallel",)), name="lru",
    )(xb8, gg3, conv_w, conv_b, wg, ba, bx, lam, h0)
    return ya.reshape(n_seq * T, D_LRU), fin


def _s5_body(u_ref, wts_ref, wc_ref, ap_ref, h0_ref, y_ref, fin_ref, s_scr, hp_scr, slab,
             *, rt, n_c, with_h0, col_major):
    q = S5_SW // 4
    m = S5_SEG
    grp = SUBLANES * m
    n_grp = rt // grp
    seg_per_seq = n_c // m
    seq_per_grp = SUBLANES // seg_per_seq
    seq_tok = n_c * S5_L
    halves = seq_tok // (GRID_W * S5_L)
    sub = lax.broadcasted_iota(I32, (SUBLANES, q), 0)

    def pieces(g, i):
        out = []
        for v in range(SUBLANES):
            s, sq = divmod(v, seg_per_seq)
            tok0 = (g * seq_per_grp + s) * seq_tok
            if col_major:
                for h in range(halves):
                    n = m // halves
                    out.append((pl.ds(g * grp + h * SUBLANES + v, n, stride=halves * SUBLANES),
                                pl.ds(tok0 + (h * S5_L + i) * GRID_W + sq * n, n)))
            else:
                out.append((pl.ds(g * grp + v, m, stride=SUBLANES),
                            pl.ds(tok0 + sq * m * S5_L + i, m, stride=S5_L)))
        return out

    for i in range(S5_L):
        for g in range(n_grp):
            for rows, toks in pieces(g, i):
                slab[i, rows, :] = u_ref[0, toks, :]
    ub = jnp.concatenate([slab[i] for i in range(S5_L)], axis=1).astype(BF16)
    r1 = jnp.dot(ub, wts_ref[0], preferred_element_type=F32)
    y_in = r1[:, :S5_L * LANES]
    s_scr[...] = r1[:, S5_L * LANES:]

    def cmul(ar, ai, xr, xi):
        return ar * xr - ai * xi, ar * xi + ai * xr

    def local_scan(base, col0, reverse):
        ar, ai = ap_ref[0, 0:1, col0:col0 + q], ap_ref[0, 0:1, col0 + q:col0 + 2 * q]

        def step(k, carry):
            hr, hi = carry
            r0 = pl.multiple_of(base + (m - 1 - k if reverse else k) * SUBLANES, SUBLANES)
            pr, pi = cmul(ar, ai, hr, hi)
            hr = pr + s_scr[pl.ds(r0, SUBLANES), col0:col0 + q]
            hi = pi + s_scr[pl.ds(r0, SUBLANES), col0 + q:col0 + 2 * q]
            s_scr[pl.ds(r0, SUBLANES), col0:col0 + q] = hr
            s_scr[pl.ds(r0, SUBLANES), col0 + q:col0 + 2 * q] = hi
            return hr, hi
        zero = jnp.zeros((SUBLANES, q), F32)
        return lax.fori_loop(0, m, step, (zero, zero), unroll=2)

    def chain(g, col0, reverse, h_end):
        base = g * grp
        first = (sub & (seg_per_seq - 1)) == (seg_per_seq - 1 if reverse else 0)
        h0r = jnp.zeros((SUBLANES, q), F32)
        h0i = jnp.zeros((SUBLANES, q), F32)
        if with_h0:
            for s in range(seq_per_grp):
                row = h0_ref[0, 0, g * seq_per_grp + s:g * seq_per_grp + s + 1, :]
                mine = (sub >> (seg_per_seq.bit_length() - 1)) == s
                h0r = jnp.where(mine, row[:, col0:col0 + q], h0r)
                h0i = jnp.where(mine, row[:, col0 + q:col0 + 2 * q], h0i)
        cr, ci = h0r, h0i
        if seg_per_seq > 1:
            er, ei = ap_ref[0, m - 1:m, col0:col0 + q], ap_ref[0, m - 1:m, col0 + q:col0 + 2 * q]
            shift = SUBLANES - 1 if reverse else 1
            for _ in range(seg_per_seq - 1):
                pr, pi = cmul(er, ei, cr, ci)
                cr = jnp.where(first, h0r, pltpu.roll(h_end[0] + pr, shift, 0))
                ci = jnp.where(first, h0i, pltpu.roll(h_end[1] + pi, shift, 0))
        if with_h0 or seg_per_seq > 1:
            def add(c, _):
                r0 = pl.multiple_of(base + c * SUBLANES, SUBLANES)
                k = m - 1 - c if reverse else c
                pr, pi = cmul(ap_ref[0, pl.ds(k, 1), col0:col0 + q],
                              ap_ref[0, pl.ds(k, 1), col0 + q:col0 + 2 * q], cr, ci)
                s_scr[pl.ds(r0, SUBLANES), col0:col0 + q] += pr
                s_scr[pl.ds(r0, SUBLANES), col0 + q:col0 + 2 * q] += pi
                return 0
            lax.fori_loop(0, m, add, 0, unroll=2)
        return cr, ci

    for g in range(n_grp):
        base = g * grp
        body = slice(base + SUBLANES, base + grp)
        head = slice(base, base + SUBLANES)
        prev = slice(base, base + grp - SUBLANES)
        tail = slice(base + grp - SUBLANES, base + grp)
        cr, ci = chain(g, 0, False, local_scan(base, 0, False))
        hp_scr[body, 0:2 * q] = s_scr[prev, 0:2 * q]
        hp_scr[head, 0:q] = cr
        hp_scr[head, q:2 * q] = ci
        cr, ci = chain(g, 2 * q, True, local_scan(base, 2 * q, True))
        hp_scr[prev, 2 * q:4 * q] = s_scr[body, 2 * q:4 * q]
        hp_scr[tail, 2 * q:3 * q] = cr
        hp_scr[tail, 3 * q:4 * q] = ci
        if seg_per_seq == 1:
            rows = slice(g * SUBLANES, (g + 1) * SUBLANES)
            fin_ref[0, 0, rows, 0:2 * q] = s_scr[tail, 0:2 * q]
            fin_ref[0, 0, rows, 2 * q:4 * q] = s_scr[head, 2 * q:4 * q]
    if seg_per_seq > 1:
        fin_ref[...] = jnp.zeros(fin_ref.shape, F32)

    y = y_in + jnp.dot(hp_scr[...].astype(BF16), wc_ref[0], preferred_element_type=F32)
    for i in range(S5_L):
        slab[i] = y[:, i * LANES:(i + 1) * LANES]
        for g in range(n_grp):
            for rows, toks in pieces(g, i):
                y_ref[0, toks, :] = slab[i, rows, :]


def _s5_call(us4, wts, wc, apow, h0, n_seq, T, with_h0, col_major):
    n_c = T // S5_L
    rows = n_seq * n_c
    rt = 512
    nseq_t = rt // n_c
    h0 = h0.reshape(S5_BLK, n_seq // nseq_t, nseq_t, S5_SW)
    y4, fin = pl.pallas_call(
        functools.partial(_s5_body, rt=rt, n_c=n_c, with_h0=with_h0, col_major=col_major),
        grid=(S5_BLK, rows // rt),
        in_specs=[pl.BlockSpec((1, rt * S5_L, LANES), lambda b, r: (b, r, 0)),
                  pl.BlockSpec((1, S5_L * LANES, S5_L * LANES + S5_SW), lambda b, r: (b, 0, 0)),
                  pl.BlockSpec((1, S5_SW, S5_L * LANES), lambda b, r: (b, 0, 0)),
                  pl.BlockSpec((1, S5_SEG, S5_SW), lambda b, r: (b, 0, 0)),
                  pl.BlockSpec((1, 1, nseq_t, S5_SW), lambda b, r: (b, r, 0, 0))],
        out_specs=[pl.BlockSpec((1, rt * S5_L, LANES), lambda b, r: (b, r, 0)),
                   pl.BlockSpec((1, 1, nseq_t, S5_SW), lambda b, r: (b, r, 0, 0))],
        out_shape=[jax.ShapeDtypeStruct((S5_BLK, n_seq * T, LANES), F32),
                   jax.ShapeDtypeStruct((S5_BLK, n_seq // nseq_t, nseq_t, S5_SW), F32)],
        scratch_shapes=[pltpu.VMEM((rt, S5_SW), F32), pltpu.VMEM((rt, S5_SW), F32),
                        pltpu.VMEM((S5_L, rt, LANES), F32)],
        compiler_params=_cp(("parallel", "parallel")), name="s5",
    )(us4, wts, wc, apow, h0)
    return y4, fin.reshape(S5_BLK, n_seq, S5_SW)


def _s5_disc(lr, li, st):
    mag = jnp.exp(lr * st)
    ang = li * st
    ar, ai = mag * jnp.cos(ang), mag * jnp.sin(ang)
    den = lr * lr + li * li
    fr = ((ar - 1.0) * lr + ai * li) / den
    fi = (ai * lr - (ar - 1.0) * li) / den
    return ar, ai, fr, fi


def _s5_powers(ar, ai, n):
    out = [(jnp.ones_like(ar), jnp.zeros_like(ar))]
    for _ in range(n):
        pr, pi = out[-1]
        out.append((pr * ar - pi * ai, pr * ai + pi * ar))
    return out


def _s5_prep_body(rowp_ref, colp_ref, flatp_ref, bt_ref, cm_ref, wts_ref, wc_ref, ap_ref):
    L, N, P = S5_L, S5_N, S5_P
    w = L * LANES
    div = lambda x, d: x >> (d.bit_length() - 1)
    mod = lambda x, d: x & (d - 1)
    one = lambda m: jnp.where(m, 1.0, 0.0).astype(BF16)
    r1 = lax.broadcasted_iota(I32, (LANES, LANES), 0)
    c1 = lax.broadcasted_iota(I32, (LANES, LANES), 1)
    first_half = c1 < N
    sgn = jnp.where(first_half, -1.0, 1.0)

    e_maps, taps = [], []
    same_group = div(r1, P) == div(c1, P)
    fold = one(mod(r1, P) == mod(c1, P))
    for d in range(2):
        ar, ai, fr, fi = _s5_disc(rowp_ref[0, d, 0], rowp_ref[1, d, 0], rowp_ref[2, d, 0])
        bx1 = bt_ref[d, 0]
        bx2 = pltpu.roll(bx1, N, 1)
        x1 = fr * bx1 + (fi * sgn) * bx2
        x2 = pltpu.roll(x1, N, 1)
        e_d, t_d = [], []
        for pr, pi in _s5_powers(ar, ai, L):
            e_k = pr * x1 + (pi * sgn) * x2
            e_d.append(e_k)
            full = _dot3(e_k, cm_ref[d, 0])
            kept = jnp.where(same_group, full, 0.0)
            hi = kept.astype(BF16)
            mid = (kept - hi.astype(F32)).astype(BF16)
            lo = (kept - hi.astype(F32) - mid.astype(F32)).astype(BF16)
            t_d.append(sum(jnp.dot(part, fold, preferred_element_type=F32) for part in (hi, mid, lo)))
        e_maps.append(e_d)
        taps.append(t_d)

    r = lax.broadcasted_iota(I32, (LANES, w), 0)
    c = lax.broadcasted_iota(I32, (LANES, w), 1)
    t_tap = one((div(r, P) == div(c, LANES)) & (mod(r, P) == mod(c, P)))
    m_tap = div(r, P) == div(mod(c, LANES), P)
    t_st = one((div(r, N) == div(c, 8 * N)) & (mod(r, N) == mod(c, N)))
    m_st = div(r, P) == div(mod(c, 8 * N), N)
    col_blk = div(c1, P)
    for j in range(L):
        rows = slice(j * LANES, (j + 1) * LANES)
        toe = jnp.zeros((LANES, LANES), F32)
        for k in range(L):
            toe = (toe + jnp.where(col_blk == j + k, taps[0][k], 0.0)
                   + jnp.where(col_blk == j - k, taps[1][k], 0.0))
        tap = jnp.dot(toe.astype(BF16), t_tap, preferred_element_type=F32)
        wts_ref[0, rows, 0:w] = jnp.where(m_tap, tap, 0.0).astype(BF16)
        for k, e_j in enumerate((e_maps[0][L - 1 - j], e_maps[1][j])):
            st = jnp.dot(e_j.astype(BF16), t_st, preferred_element_type=F32)
            wts_ref[0, rows, w + k * w:w + (k + 1) * w] = jnp.where(m_st, st, 0.0).astype(BF16)

    rr = lax.broadcasted_iota(I32, (w, LANES), 0)
    cc = lax.broadcasted_iota(I32, (w, LANES), 1)
    t_row = one((div(rr, 8 * N) == div(cc, N)) & (mod(rr, N) == mod(cc, N)))
    r2 = lax.broadcasted_iota(I32, (w, w), 0)
    c2 = lax.broadcasted_iota(I32, (w, w), 1)
    m_row = div(mod(r2, 8 * N), N) == div(mod(c2, LANES), P)
    for d in range(2):
        ar, ai, _, _ = _s5_disc(colp_ref[0, d, 0], colp_ref[1, d, 0], colp_ref[2, d, 0])
        a1 = cm_ref[d, 0]
        swapped = pltpu.roll(a1, N, 0)
        a2 = jnp.where(r1 < N, swapped, -swapped)
        pw = _s5_powers(ar, ai, L)
        ks = [i + 1 for i in range(L)] if d == 0 else [L - i for i in range(L)]
        g = jnp.concatenate([a1 * pw[k][0] + a2 * pw[k][1] for k in ks], axis=1)
        corr = jnp.dot(t_row, g.astype(BF16), preferred_element_type=F32)
        wc_ref[0, d * w:(d + 1) * w, :] = jnp.where(m_row, corr, 0.0).astype(BF16)

    al = []
    for d in range(2):
        ar, ai, _, _ = _s5_disc(flatp_ref[0, d, 0], flatp_ref[1, d, 0], flatp_ref[2, d, 0])
        al.append(_s5_powers(ar, ai, L)[L])
    cur = al
    for k in range(S5_SEG):
        ap_ref[0, k:k + 1, :] = jnp.concatenate([cur[0][0], cur[0][1], cur[1][0], cur[1][1]], axis=1)
        cur = [(cr * br - ci * bi, cr * bi + ci * br) for (cr, ci), (br, bi) in zip(cur, al)]


def _s5_weights(lam_re, lam_im, log_step, b_re, b_im, c_re, c_im):
    L, G, N, P = S5_L, S5_G, S5_N, S5_P
    w = L * LANES
    step = jnp.broadcast_to(jnp.exp(log_step)[..., None], lam_re.shape)
    prm = jnp.stack([lam_re, lam_im, step])
    rowp = jnp.broadcast_to(prm[:, :, :, None, None, :], (3, 2, G, P, 2, N)).reshape(3, 2, S5_BLK, LANES, 2 * N)
    colp = jnp.broadcast_to(prm.transpose(0, 1, 3, 2)[:, :, None, :, :, None], (3, 2, 2, N, G, P))
    colp = colp.reshape(3, 2, 2 * N, S5_BLK, LANES).transpose(0, 1, 3, 2, 4)
    flatp = prm.reshape(3, 2, S5_BLK, 1, 8 * N)
    bt = jnp.stack([b_re, b_im], axis=3).transpose(0, 1, 4, 3, 2)
    bt = bt.reshape(2, S5_BLK, LANES, 2 * N)
    cm = jnp.stack([c_re, -c_im], axis=1).transpose(0, 1, 4, 2, 3)
    cm = cm.reshape(2, 2 * N, S5_BLK, LANES).transpose(0, 2, 1, 3)
    blk = lambda lead, a, b_: pl.BlockSpec(lead + (1, a, b_), lambda i: (0,) * len(lead) + (i, 0, 0))
    out = lambda a, b_: pl.BlockSpec((1, a, b_), lambda i: (i, 0, 0))
    return pl.pallas_call(
        _s5_prep_body, grid=(S5_BLK,),
        in_specs=[blk((3, 2), LANES, LANES), blk((3, 2), LANES, LANES), blk((3, 2), 1, 8 * N),
                  blk((2,), LANES, LANES), blk((2,), LANES, LANES)],
        out_specs=[out(w, w + S5_SW), out(S5_SW, w), out(S5_SEG, S5_SW)],
        out_shape=[jax.ShapeDtypeStruct((S5_BLK, w, w + S5_SW), BF16),
                   jax.ShapeDtypeStruct((S5_BLK, S5_SW, w), BF16),
                   jax.ShapeDtypeStruct((S5_BLK, S5_SEG, S5_SW), F32)],
        compiler_params=_cp(("parallel",)), name="s5_prep",
    )(rowp, colp, flatp, bt, cm)


def _mixout_body(xp_ref, xs_ref, mod_ref, g1_ref, g2_ref, wbg_ref, bbg_ref, yap_ref, yas_ref,
                 y4p_ref, y4s_ref, u4p_ref, u4s_ref, *rest, tiles_p):
    i = pl.program_id(0)

    @pl.when(i < tiles_p)
    def _():
        _mixout_tile(xp_ref, mod_ref, g1_ref, g2_ref, wbg_ref, bbg_ref, yap_ref, y4p_ref, u4p_ref, *rest)

    @pl.when(i >= tiles_p)
    def _():
        _mixout_tile(xs_ref, mod_ref, g1_ref, g2_ref, wbg_ref, bbg_ref, yas_ref, y4s_ref, u4s_ref, *rest)


def _mixout_tile(x_ref, mod_ref, g1_ref, g2_ref, wbg_ref, bbg_ref, ya_ref, y4_ref, u4_ref,
                 dsk_ref, wglu_ref, bglu_ref, wso_ref, wlo_ref, wo_ref, wrt_ref, brt_ref,
                 x1_ref, h2_ref, aff_ref):
    m = mod_ref[0]
    sh1, sc1, gt1 = m[:, 0:D_MODEL], m[:, D_MODEL:2 * D_MODEL], m[:, 2 * D_MODEL:3 * D_MODEL]
    sh2, sc2 = m[:, 3 * D_MODEL:4 * D_MODEL], m[:, 4 * D_MODEL:5 * D_MODEL]
    x = x_ref[...]
    h = _norm_mod(x, g1_ref[...], sc1, sh1)
    gates = _sigmoid(jnp.dot(h.astype(BF16), wbg_ref[...], preferred_element_type=F32) + bbg_ref[...])

    def assemble(ref):
        return jnp.concatenate([ref[k] for k in range(S5_BLK)], axis=1)

    ys = assemble(y4_ref) + dsk_ref[...] * assemble(u4_ref)
    v = _gelu(ys)
    ob = v * _sigmoid(jnp.dot(v.astype(BF16), wglu_ref[...], preferred_element_type=F32) + bglu_ref[...])
    merged = (gates[:, :D_MODEL] * jnp.dot(ya_ref[...], wlo_ref[...], preferred_element_type=F32)
              + gates[:, D_MODEL:] * jnp.dot(ob.astype(BF16), wso_ref[...], preferred_element_type=F32))
    mix = jnp.dot(merged.astype(BF16), wo_ref[...], preferred_element_type=F32)
    x1 = x + gt1 * mix
    x1_ref[...] = x1
    h2 = _norm_mod(x1, g2_ref[...], sc2, sh2)
    h2_ref[...] = h2
    logits = _dot3(wrt_ref[...], h2, dims=(((1,), (1,)), ((), ()))) + brt_ref[...]
    mx = jnp.max(logits, axis=0, keepdims=True)
    ex = jnp.exp(logits - mx)
    aff_ref[...] = ex / jnp.sum(ex, axis=0, keepdims=True)


def _mixout_call(xp, xs, mod3, g1, g2, wbg, bbg, ya_p, ya_s, y4_p, y4_s, u4_p, u4_s,
                 dsk, wglu, bglu, wso, wlo, wo, wrt, brt, mod_row, tm):
    n_p, n_s = xp.shape[0], xs.shape[0]
    tiles_p = n_p // tm
    n = n_p + n_s
    full = lambda shape: pl.BlockSpec(shape, lambda i: (0,) * len(shape))
    at_p = lambda i: jnp.minimum(i, tiles_p - 1)
    at_s = lambda i: jnp.maximum(i - tiles_p, 0)
    tok = lambda w, at: pl.BlockSpec((tm, w), lambda i: (at(i), 0))
    s5 = lambda at: pl.BlockSpec((S5_BLK, tm, LANES), lambda i: (0, at(i), 0))
    out = lambda w: pl.BlockSpec((tm, w), lambda i: (i, 0))
    return pl.pallas_call(
        functools.partial(_mixout_body, tiles_p=tiles_p),
        grid=(n // tm,),
        in_specs=[tok(D_MODEL, at_p), tok(D_MODEL, at_s),
                  pl.BlockSpec((1, 1, 6 * D_MODEL), lambda i: (mod_row(i), 0, 0)),
                  full((1, D_MODEL)), full((1, D_MODEL)), full((D_MODEL, 2 * D_MODEL)),
                  full((1, 2 * D_MODEL)), tok(D_LRU, at_p), tok(D_LRU, at_s),
                  s5(at_p), s5(at_s), s5(at_p), s5(at_s), full((1, D_S5)),
                  full((D_S5, D_S5)), full((1, D_S5)), full((D_S5, D_MODEL)), full((D_LRU, D_MODEL)),
                  full((D_MODEL, D_MODEL)), full((N_EXPERTS, D_MODEL)), full((N_EXPERTS, 1))],
        out_specs=[out(D_MODEL), out(D_MODEL), pl.BlockSpec((N_EXPERTS, tm), lambda i: (0, i))],
        out_shape=[jax.ShapeDtypeStruct((n, D_MODEL), F32), jax.ShapeDtypeStruct((n, D_MODEL), F32),
                   jax.ShapeDtypeStruct((N_EXPERTS, n), F32)],
        compiler_params=_cp(("arbitrary",)), name="mixout",
    )(xp, xs, mod3, g1, g2, wbg, bbg, ya_p, ya_s, y4_p, y4_s, u4_p, u4_s,
      dsk, wglu, bglu, wso, wlo, wo, wrt, brt)


def _select_body(aff_ref, gate_ref, pos_ref, offs_ref, *, n_tok, cap):
    aff = aff_ref[...]
    capf = float(cap)

    def bis(_, lh):
        lo, hi = lh
        mid = lo + ((hi - lo + 1) >> 1)
        cnt = jnp.sum(jnp.where(aff >= pltpu.bitcast(mid, F32), 1.0, 0.0), axis=1, keepdims=True)
        ok = cnt >= capf
        return jnp.where(ok, mid, lo), jnp.where(ok, hi, mid - 1)

    lo0 = jnp.zeros((N_EXPERTS, 1), I32)
    hi0 = jnp.full((N_EXPERTS, 1), 0x7F800000, I32)
    thr_bits, _ = lax.fori_loop(0, 31, bis, (lo0, hi0))
    thr = pltpu.bitcast(thr_bits, F32)
    need = capf - jnp.sum(jnp.where(aff > thr, 1.0, 0.0), axis=1, keepdims=True)

    r = lax.broadcasted_iota(I32, (TOK_TILE, TOK_TILE), 0)
    c = lax.broadcasted_iota(I32, (TOK_TILE, TOK_TILE), 1)
    tri = jnp.where(r < c, 1.0, 0.0).astype(BF16)
    lane = lax.broadcasted_iota(I32, (N_EXPERTS, LANES), 1)
    n_tiles = n_tok // TOK_TILE
    run_eq = jnp.zeros((N_EXPERTS, 1), F32)
    run_sel = jnp.zeros((N_EXPERTS, 1), F32)
    offs = jnp.zeros((N_EXPERTS, LANES), F32)
    for t in range(n_tiles):
        sl = slice(t * TOK_TILE, (t + 1) * TOK_TILE)
        aff_t = aff_ref[:, sl]
        is_eq = aff_t == thr
        eq_t = jnp.where(is_eq, 1.0, 0.0)
        rank_eq = jnp.dot(eq_t.astype(BF16), tri, preferred_element_type=F32) + run_eq
        cand = (aff_t > thr) | (is_eq & (rank_eq < need))
        pos_t = jnp.dot(jnp.where(cand, 1.0, 0.0).astype(BF16), tri, preferred_element_type=F32) + run_sel
        sel_t = cand & (pos_t < capf)
        sel_f = jnp.where(sel_t, 1.0, 0.0)
        offs = jnp.where(lane == t, run_sel, offs)
        gate_ref[:, sl] = jnp.where(sel_t, aff_t, 0.0)
        pos_ref[:, sl] = jnp.where(sel_t, pos_t, -1.0).astype(I32)
        run_eq = run_eq + jnp.sum(eq_t, axis=1, keepdims=True)
        run_sel = run_sel + jnp.sum(sel_f, axis=1, keepdims=True)
    offs = jnp.where(lane >= n_tiles, run_sel, offs)
    offs_ref[...] = offs.astype(I32)


def _select_call(aff_t, cap, n_tok):
    n_path = aff_t.shape[1] // n_tok
    per_p = lambda a, b: pl.BlockSpec((None, a, b), lambda p: (p, 0, 0))
    return pl.pallas_call(
        functools.partial(_select_body, n_tok=n_tok, cap=cap),
        grid=(n_path,),
        in_specs=[pl.BlockSpec((N_EXPERTS, n_tok), lambda p: (0, p))],
        out_specs=[per_p(N_EXPERTS, n_tok), per_p(N_EXPERTS, n_tok), per_p(N_EXPERTS, LANES)],
        out_shape=[jax.ShapeDtypeStruct((n_path, N_EXPERTS, n_tok), F32),
                   jax.ShapeDtypeStruct((n_path, N_EXPERTS, n_tok), I32),
                   jax.ShapeDtypeStruct((n_path, N_EXPERTS, LANES), I32)],
        compiler_params=_cp(("parallel",)), name="select",
    )(aff_t)


def _invert_body(offs_ref, pos_ref, gate_ref, idx_ref, gcol_ref, acc_i, acc_g, *, n_tiles, cap):
    path = pl.program_id(0)
    base = (path * N_EXPERTS + pl.program_id(1)) * LANES
    acc_i[...] = jnp.zeros(acc_i.shape, F32)
    acc_g[...] = jnp.zeros(acc_g.shape, F32)
    jrow = lax.broadcasted_iota(I32, (SLOT_CHUNK, LANES), 0)
    lane = lax.broadcasted_iota(I32, (1, LANES), 1)

    def per_tile(t, _):
        first = (offs_ref[base + t] >> 3) << 3
        prow = pos_ref[0, 0, pl.ds(t, 1), :]
        grow = gate_ref[0, 0, pl.ds(t, 1), :]
        tok = (lane + (t * TOK_TILE + 1)).astype(F32)

        def window(w, _):
            start = pl.multiple_of(first + w * SLOT_CHUNK, SUBLANES)
            rows = pl.ds(start, SLOT_CHUNK)
            hit = prow == (jrow + start)
            acc_i[rows, :] = jnp.maximum(acc_i[rows, :], jnp.where(hit, tok, 0.0))
            acc_g[rows, :] = jnp.maximum(acc_g[rows, :], jnp.where(hit, grow, 0.0))
            return 0

        n_win = (offs_ref[base + t + 1] - first + SLOT_CHUNK - 1) >> (SLOT_CHUNK.bit_length() - 1)
        lax.fori_loop(0, n_win, window, 0)
        return 0

    lax.fori_loop(0, n_tiles, per_tile, 0, unroll=2)
    hit_i = acc_i[0:cap, :]
    hi = hit_i.astype(BF16)
    mid = (hit_i - hi.astype(F32)).astype(BF16)
    lo = (hit_i - hi.astype(F32) - mid.astype(F32)).astype(BF16)
    ones = jnp.ones((SUBLANES, LANES), BF16)
    nt = (((1,), (1,)), ((), ()))
    tok1 = sum(lax.dot_general(ones, part, nt, preferred_element_type=F32) for part in (hi, mid, lo))
    idx_ref[0, 0] = tok1[0:1, :].astype(I32) - 1 + path * (n_tiles * TOK_TILE)
    gcol_ref[0, 0] = jnp.max(acc_g[0:cap, :], axis=1, keepdims=True)


def _invert_call(offs, pos, gate, cap):
    n_path, _, n_tok = pos.shape
    n_tiles = n_tok // TOK_TILE
    pos4 = pos.reshape(n_path, N_EXPERTS, n_tiles, TOK_TILE)
    gate4 = gate.reshape(n_path, N_EXPERTS, n_tiles, TOK_TILE)
    per_e = lambda a, b: pl.BlockSpec((1, 1, a, b), lambda p, e, o: (p, e, 0, 0))
    return pl.pallas_call(
        functools.partial(_invert_body, n_tiles=n_tiles, cap=cap),
        grid_spec=pltpu.PrefetchScalarGridSpec(
            num_scalar_prefetch=1, grid=(n_path, N_EXPERTS),
            in_specs=[per_e(n_tiles, TOK_TILE), per_e(n_tiles, TOK_TILE)],
            out_specs=[per_e(1, cap), per_e(cap, 1)],
            scratch_shapes=[pltpu.VMEM((cap + SLOT_CHUNK, LANES), F32)] * 2),
        out_shape=[jax.ShapeDtypeStruct((n_path, N_EXPERTS, 1, cap), I32),
                   jax.ShapeDtypeStruct((n_path, N_EXPERTS, cap, 1), F32)],
        compiler_params=_cp(("parallel", "parallel")), name="invert",
    )(offs.reshape(-1), pos4, gate4)


def _ffn_body(idx0_ref, idxn_ref, gcol_ref, wg_ref, wu_ref, wd_ref, h_hbm, out_ref,
              xe, wgb, wub, wdb, sem, *, cap):
    e = pl.program_id(0)
    p = pl.program_id(1)
    nxt = 1 - p
    step = e * 2 + p

    def issue_rows(idx_ref, s, j0, n):
        for j in range(j0, j0 + n):
            pltpu.make_async_copy(h_hbm.at[pl.ds(idx_ref[0, 0, 0, j], 1), :],
                                  xe.at[s, pl.ds(j, 1), :], sem.at[s]).start()

    @pl.when(step == 0)
    def _():
        def issue(j, _):
            pltpu.make_async_copy(h_hbm.at[pl.ds(idx0_ref[0, 0, 0, j], 1), :],
                                  xe.at[0, pl.ds(j, 1), :], sem.at[0]).start()
            return 0
        lax.fori_loop(0, cap, issue, 0, unroll=8)

    @pl.when(p == 0)
    def _():
        wgb[...] = wg_ref[0].astype(BF16)
        wub[...] = wu_ref[0].astype(BF16)
        wdb[...] = wd_ref[0].astype(BF16)

    pltpu.make_async_copy(h_hbm.at[pl.ds(0, cap), :], xe.at[p], sem.at[p]).wait()

    ch = ROW_CHUNK
    n_ch = cap // ch
    per = -(-cap // (n_ch - 1))
    for c in range(n_ch):
        sl = slice(c * ch, (c + 1) * ch)
        xb = xe[p, sl, :].astype(BF16)
        issue_rows(idxn_ref, nxt, min(c * per, cap), min((c + 1) * per, cap) - min(c * per, cap))
        g = jnp.dot(xb, wgb[...], preferred_element_type=F32)
        u = jnp.dot(xb, wub[...], preferred_element_type=F32)
        hid = (g * _sigmoid(g)) * u
        ye = jnp.dot(hid.astype(BF16), wdb[...], preferred_element_type=F32)
        out_ref[0, 0, sl, :] = ye * gcol_ref[0, 0, sl, :]

    @pl.when(step == 2 * N_EXPERTS - 1)
    def _():
        pltpu.make_async_copy(h_hbm.at[pl.ds(0, cap), :], xe.at[nxt], sem.at[nxt]).wait()


def _ffn_call(idx, gcol, w_gate, w_up, w_down, h2, cap):
    d_e = w_gate.shape[2]
    wspec = lambda a, b: pl.BlockSpec((1, a, b), lambda e, p: (e, 0, 0))
    return pl.pallas_call(
        functools.partial(_ffn_body, cap=cap),
        grid=(N_EXPERTS, 2),
        in_specs=[pl.BlockSpec((1, 1, 1, cap), lambda e, p: (0, 0, 0, 0), memory_space=pltpu.SMEM),
                  pl.BlockSpec((1, 1, 1, cap),
                               lambda e, p: (1 - p, jnp.minimum(e + p, N_EXPERTS - 1), 0, 0),
                               memory_space=pltpu.SMEM),
                  pl.BlockSpec((1, 1, cap, 1), lambda e, p: (p, e, 0, 0)),
                  wspec(D_MODEL, d_e), wspec(D_MODEL, d_e), wspec(d_e, D_MODEL),
                  pl.BlockSpec(memory_space=pl.ANY)],
        out_specs=pl.BlockSpec((1, 1, cap, D_MODEL), lambda e, p: (p, e, 0, 0)),
        out_shape=jax.ShapeDtypeStruct((2, N_EXPERTS, cap, D_MODEL), F32),
        scratch_shapes=[pltpu.VMEM((2, cap, D_MODEL), F32), pltpu.VMEM((D_MODEL, d_e), BF16),
                        pltpu.VMEM((D_MODEL, d_e), BF16), pltpu.VMEM((d_e, D_MODEL), BF16),
                        pltpu.SemaphoreType.DMA((2,))],
        compiler_params=_cp(("arbitrary", "arbitrary")), name="ffn",
    )(idx, idx, gcol, w_gate, w_up, w_down, h2)


def _combine_body(offs_ref, x1_ref, mod_ref, gf_ref, pos_ref, yb_hbm, y_ref, wins, acc_ref, sem,
                  *, cap, path, win_narrow):
    t = pl.program_id(0)
    n_t = pl.num_programs(0)
    slot = t % 2
    row0 = path * N_EXPERTS * LANES
    per = COMBINE_TILE // TOK_TILE

    def first_slot(tt, e):
        return offs_ref[row0 + e * LANES + tt * per]

    def kmax_of(tt):
        k = 0
        for e in range(N_EXPERTS):
            k = jnp.maximum(k, first_slot(tt + 1, e) - first_slot(tt, e))
        return k

    def geometry(tt, e, w, win):
        fetch = win + SUBLANES
        lo = first_slot(tt, e) + w * win
        start = pl.multiple_of(jnp.minimum((lo >> 3) << 3, cap - fetch), SUBLANES)
        return lo, start, fetch

    def issue(tt, w, s, win):
        for e in range(N_EXPERTS):
            _, start, fetch = geometry(tt, e, w, win)
            pltpu.make_async_copy(yb_hbm.at[path, e, pl.ds(start, fetch), :],
                                  wins.at[s, pl.ds(e * fetch, fetch), :], sem.at[s]).start()

    def drain(s, win):
        rows = N_EXPERTS * (win + SUBLANES)
        pltpu.make_async_copy(wins.at[1 - s, pl.ds(0, rows), :], wins.at[s, pl.ds(0, rows), :],
                              sem.at[s]).wait()

    def expand(w, s, win):
        sub = lax.broadcasted_iota(I32, (win + SUBLANES, COMBINE_TILE), 0)
        rows = []
        for e in range(N_EXPERTS):
            lo, start, _ = geometry(t, e, w, win)
            pos = pos_ref[e:e + 1, :]
            in_round = jnp.logical_and(pos >= lo, pos < lo + win)
            rows.append(jnp.where(jnp.logical_and(pos - start == sub, in_round), 1.0, 0.0))
        onehot = jnp.concatenate(rows, axis=0).T.astype(BF16)
        data = wins[s, 0:N_EXPERTS * (win + SUBLANES), :]
        hi = data.astype(BF16)
        lo_part = (data - hi.astype(F32)).astype(BF16)
        return (jnp.dot(onehot, hi, preferred_element_type=F32)
                + jnp.dot(onehot, lo_part, preferred_element_type=F32))

    def issue_first(tt, s):
        wide = kmax_of(tt) > win_narrow

        @pl.when(wide)
        def _():
            issue(tt, 0, s, COMBINE_WIN_WIDE)

        @pl.when(jnp.logical_not(wide))
        def _():
            issue(tt, 0, s, win_narrow)

    @pl.when(t == 0)
    def _():
        issue_first(0, 0)

    @pl.when(t + 1 < n_t)
    def _():
        issue_first(t + 1, 1 - slot)

    kmax = kmax_of(t)

    @pl.when(kmax > win_narrow)
    def _():
        drain(slot, COMBINE_WIN_WIDE)
        acc_ref[...] = expand(0, slot, COMBINE_WIN_WIDE)

    @pl.when(kmax <= win_narrow)
    def _():
        drain(slot, win_narrow)
        acc_ref[...] = expand(0, slot, win_narrow)

    def more(w, _):
        issue(t, w, slot, COMBINE_WIN_WIDE)
        drain(slot, COMBINE_WIN_WIDE)
        acc_ref[...] += expand(w, slot, COMBINE_WIN_WIDE)
        return 0
    lax.fori_loop(1, (kmax + COMBINE_WIN_WIDE - 1) >> (COMBINE_WIN_WIDE.bit_length() - 1), more, 0)
    acc = acc_ref[...]

    gt2 = mod_ref[0][:, 5 * D_MODEL:6 * D_MODEL]
    x2 = x1_ref[...] + gt2 * acc
    ms = jnp.mean(x2 * x2, axis=-1, keepdims=True)
    y_ref[...] = x2 * lax.rsqrt(ms + EPS) * gf_ref[...]


def _combine_call(offs, pos, x1, mod3, gf, ybuf, mod_row, cap, path, n, win_narrow=COMBINE_WIN):
    first = path * n // COMBINE_TILE
    return pl.pallas_call(
        functools.partial(_combine_body, cap=cap, path=path, win_narrow=win_narrow),
        grid_spec=pltpu.PrefetchScalarGridSpec(
            num_scalar_prefetch=1, grid=(n // COMBINE_TILE,),
            in_specs=[pl.BlockSpec((COMBINE_TILE, D_MODEL), lambda i, o: (first + i, 0)),
                      pl.BlockSpec((1, 1, 6 * D_MODEL), lambda i, o: (mod_row(i), 0, 0)),
                      pl.BlockSpec((1, D_MODEL), lambda i, o: (0, 0)),
                      pl.BlockSpec((None, N_EXPERTS, COMBINE_TILE), lambda i, o: (path, 0, i)),
                      pl.BlockSpec(memory_space=pl.ANY)],
            out_specs=pl.BlockSpec((COMBINE_TILE, D_MODEL), lambda i, o: (i, 0)),
            scratch_shapes=[pltpu.VMEM((2, N_EXPERTS * (COMBINE_WIN_WIDE + SUBLANES), D_MODEL), F32),
                            pltpu.VMEM((COMBINE_TILE, D_MODEL), F32),
                            pltpu.SemaphoreType.DMA((2,))]),
        out_shape=jax.ShapeDtypeStruct((n, D_MODEL), F32),
        compiler_params=_cp(("arbitrary",)), name="combine",
    )(offs.reshape(-1), x1, mod3, gf, pos, ybuf)


def _lru_gate_weights(wa, wx):
    hb = MXU_DIM // LRU_HEAD_DIM
    nb = LRU_HEADS // hb
    eye = jnp.eye(hb, dtype=F32)

    def bd(w):
        w5 = w.reshape(nb, hb, LRU_HEAD_DIM, LRU_HEAD_DIM)
        return jnp.einsum('khij,hg->khigj', w5, eye).reshape(nb, MXU_DIM, MXU_DIM)

    return jnp.concatenate([bd(wa[0]), bd(wa[1]), bd(wx[0]), bd(wx[1])], axis=2).astype(BF16)


def kernel(x_prompt, x_sample, state_lru, state_s5_re, state_s5_im, c, c_ctx, w_mod, b_mod, g_norm1, g_norm2, w_in, conv_w, conv_b, lru_wa, lru_ba, lru_wx, lru_bx, lru_lambda, s5_lambda_re, s5_lambda_im, s5_log_step, s5_b_re, s5_b_im, s5_c_re, s5_c_im, s5_d, s5_w_glu, s5_b_glu, w_lru_out, w_s5_out, w_branch_gate, b_branch_gate, w_o, w_router, b_router, w_e_gate, w_e_up, w_e_down, g_final):
    bp, tp, _ = x_prompt.shape
    bs, ts, _ = x_sample.shape
    n_p, n_s = bp * tp, bs * ts
    l = 0

    c_all = jnp.zeros((16, D_MODEL), F32).at[0].set(c_ctx).at[1:1 + bs].set(c)
    mod3 = _mod_call(c_all, w_mod[l], b_mod[l][None, :]).reshape(16, 1, 6 * D_MODEL)

    w_in_b = w_in[l].astype(BF16)
    w_main, w_s5in = w_in_b[:, :2 * D_LRU], w_in_b[:, 2 * D_LRU:]
    g1 = g_norm1[l][None, :]
    g2 = g_norm2[l][None, :]
    wg = _lru_gate_weights(lru_wa[l], lru_wx[l])
    wts, wc, apow = _s5_weights(s5_lambda_re[l], s5_lambda_im[l], s5_log_step[l],
                                s5_b_re[l], s5_b_im[l], s5_c_re[l], s5_c_im[l])
    wbg = w_branch_gate[l].astype(BF16)
    bbg = b_branch_gate[l][None, :]
    wglu = s5_w_glu[l].astype(BF16)
    wso = w_s5_out[l].astype(BF16)
    wlo = w_lru_out[l].astype(BF16)
    wo = w_o[l].astype(BF16)
    wrt = w_router[l].T
    brt = b_router[l][:, None]

    xp2 = x_prompt.reshape(n_p, D_MODEL)
    xs2 = x_sample.reshape(n_s, D_MODEL)
    ctx_row = lambda i: 0
    tm_in = 1024
    lat_row_in = lambda i: 1 + i // (ts // max(tm_in, ts))

    def s5_state(re, im):
        def part(a, d):
            return a[:, d].reshape(-1, S5_BLK, 8 * S5_N).transpose(1, 0, 2)
        return jnp.concatenate([part(re, 0), part(im, 0), part(re, 1), part(im, 1)], axis=2)

    xb_p, gg_p, us4_p = _inproj_call(xp2, mod3, g1, w_main, w_s5in, ctx_row, max(tm_in, tp), tp)
    ya_p, lru_fin = _lru_call(xb_p, gg_p, conv_w[l], conv_b[l][None, :], wg, lru_ba[l], lru_bx[l],
                              lru_lambda[l], jnp.zeros((bp, 2, D_LRU), F32), bp, tp)
    y4_p, s5_fin = _s5_call(us4_p, wts, wc, apow, jnp.zeros((S5_BLK, bp, S5_SW), F32), bp, tp,
                            with_h0=False, col_major=False)
    xb_s, gg_s, us4_s = _inproj_call(xs2, mod3, g1, w_main, w_s5in, lat_row_in, max(tm_in, ts), ts)
    ya_s, _ = _lru_call(xb_s, gg_s, conv_w[l], conv_b[l][None, :], wg, lru_ba[l], lru_bx[l],
                        lru_lambda[l], state_lru[:, l], bs, ts)
    y4_s, _ = _s5_call(us4_s, wts, wc, apow, s5_state(state_s5_re[:, l], state_s5_im[:, l]), bs, ts,
                       with_h0=True, col_major=True)

    tm_mo = 512
    tiles_p = n_p // tm_mo
    x1, h2, aff = _mixout_call(
        xp2, xs2, mod3, g1, g2, wbg, bbg, ya_p, ya_s, y4_p, y4_s, us4_p, us4_s,
        s5_d[l][None, :], wglu, s5_b_glu[l][None, :], wso, wlo, wo, wrt, brt,
        mod_row=lambda i: jnp.where(i < tiles_p, 0, 1 + (i - tiles_p) // (ts // tm_mo)), tm=tm_mo)

    assert n_p == n_s
    cap = (CAPACITY_FACTOR * n_p) // N_EXPERTS
    gate, pos, offs = _select_call(aff, cap, n_p)
    idx, gcol = _invert_call(offs, pos, gate, cap)
    ybuf = _ffn_call(idx, gcol, w_e_gate[l], w_e_up[l], w_e_down[l], h2, cap)
    gf = g_final[None, :]
    y_p = _combine_call(offs, pos, x1, mod3, gf, ybuf, ctx_row, cap, 0, n_p, win_narrow=COMBINE_WIN - 16)
    y_s = _combine_call(offs, pos, x1, mod3, gf, ybuf, lambda i: 1 + i // (ts // COMBINE_TILE), cap, 1, n_s,
                        win_narrow=COMBINE_WIN + 16)

    new_lru = lru_fin[:, None]
    sf = s5_fin.transpose(1, 0, 2).reshape(bp, S5_BLK, 2, 2, 8, S5_N)
    sf = sf.transpose(0, 2, 3, 1, 4, 5).reshape(bp, 2, 2, S5_G, S5_N)
    new_s5r = sf[:, :, 0][:, None]
    new_s5i = sf[:, :, 1][:, None]
    return (y_p.reshape(bp, tp, D_MODEL), y_s.reshape(bs, ts, D_MODEL), new_lru, new_s5r, new_s5i)
```

```python
import functools
import math

import jax
import jax.numpy as jnp
from jax import lax
from jax.experimental import pallas as pl
from jax.experimental.pallas import tpu as pltpu

F32 = jnp.float32
BF16 = jnp.bfloat16
I32 = jnp.int32

D_MODEL = 1024
D_LRU = 1024
LRU_HEADS = 16
LRU_HEAD_DIM = 64
LRU_C = 8.0
CONV_W = 4
D_S5 = 512
S5_P = 16
S5_G = 32
S5_N = 64
GRID_W = 64
N_EXPERTS = 16
CAPACITY_FACTOR = 2
EPS = 1e-6

LANES = 128
SUBLANES = 8
MXU_DIM = 256
ROW_CHUNK = 256
S5_L = 8
S5_BLK = 4
S5_SW = 4 * (LANES // S5_P) * S5_N
S5_SEG = 32
TOK_TILE = 128
SLOT_CHUNK = 32
COMBINE_TILE = 256
COMBINE_WIN = 64
COMBINE_WIN_WIDE = 128
VMEM_LIMIT = 56 * 1024 * 1024


def _cp(sem, vmem=VMEM_LIMIT):
    return pltpu.CompilerParams(dimension_semantics=sem, vmem_limit_bytes=vmem)


def _split2(a):
    hi = a.astype(BF16)
    lo = (a - hi.astype(F32)).astype(BF16)
    return hi, lo


def _dot3(a, b, dims=(((1,), (0,)), ((), ()))):
    ah, al = _split2(a)
    bh, bl = _split2(b)
    d = functools.partial(lax.dot_general, dimension_numbers=dims, preferred_element_type=F32)
    return d(ah, bh) + (d(al, bh) + d(ah, bl))


def _sigmoid(x):
    return 0.5 * jnp.tanh(0.5 * x) + 0.5


def _gelu(x):
    c = math.sqrt(2.0 / math.pi)
    half = 0.5 * x
    return half + half * jnp.tanh(x * (c + (c * 0.044715) * (x * x)))


def _norm_mod(x, g, scale, shift):
    ms = jnp.mean(x * x, axis=-1, keepdims=True)
    return (x * lax.rsqrt(ms + EPS)) * (g * (1.0 + scale)) + shift


def _mod_body(c_ref, w_ref, b_ref, o_ref):
    c = c_ref[...]
    s = c * _sigmoid(c)
    o_ref[...] = _dot3(s, w_ref[...]) + b_ref[...]


def _mod_call(c_all, w_mod, b_mod):
    n = w_mod.shape[1]
    tn = 1536
    return pl.pallas_call(
        _mod_body,
        grid=(n // tn,),
        in_specs=[pl.BlockSpec((16, D_MODEL), lambda j: (0, 0)),
                  pl.BlockSpec((D_MODEL, tn), lambda j: (0, j)),
                  pl.BlockSpec((1, tn), lambda j: (0, j))],
        out_specs=pl.BlockSpec((16, tn), lambda j: (0, j)),
        out_shape=jax.ShapeDtypeStruct((16, n), F32),
        compiler_params=_cp(("arbitrary",)),
        name="mod",
    )(c_all, w_mod, b_mod)


def _inproj_body(x_ref, mod_ref, g_ref, w_ref, ws_ref, xb_ref, gg_ref, us_ref, *, seq_len):
    m = mod_ref[0]
    h = _norm_mod(x_ref[...], g_ref[...], m[:, D_MODEL:2 * D_MODEL], m[:, 0:D_MODEL])
    hb = h.astype(BF16)
    z = jnp.dot(hb, w_ref[...], preferred_element_type=F32)
    seg = seq_len // SUBLANES
    for s in range(x_ref.shape[0] // seq_len):
        for k in range(SUBLANES):
            t0 = s * seq_len + k * seg
            for c in range(D_LRU // LANES):
                xb_ref[c, pl.ds(s * seq_len + k, seg, stride=SUBLANES), :] = (
                    z[t0:t0 + seg, c * LANES:(c + 1) * LANES])
    gg_ref[...] = _gelu(z[:, D_LRU:]).astype(BF16)
    us = jnp.dot(hb, ws_ref[...], preferred_element_type=F32)
    for k in range(S5_BLK):
        us_ref[k] = us[:, k * LANES:(k + 1) * LANES]


def _inproj_call(x2d, mod3, g1, w_main, w_s5, mod_row, tm, seq_len):
    n = x2d.shape[0]
    return pl.pallas_call(
        functools.partial(_inproj_body, seq_len=seq_len), grid=(n // tm,),
        in_specs=[pl.BlockSpec((tm, D_MODEL), lambda i: (i, 0)),
                  pl.BlockSpec((1, 1, 6 * D_MODEL), lambda i: (mod_row(i), 0, 0)),
                  pl.BlockSpec((1, D_MODEL), lambda i: (0, 0)),
                  pl.BlockSpec((D_MODEL, 2 * D_LRU), lambda i: (0, 0)),
                  pl.BlockSpec((D_MODEL, D_S5), lambda i: (0, 0))],
        out_specs=[pl.BlockSpec((D_LRU // LANES, tm, LANES), lambda i: (0, i, 0)),
                   pl.BlockSpec((tm, D_LRU), lambda i: (i, 0)),
                   pl.BlockSpec((S5_BLK, tm, LANES), lambda i: (0, i, 0))],
        out_shape=[jax.ShapeDtypeStruct((D_LRU // LANES, n, LANES), F32),
                   jax.ShapeDtypeStruct((n, D_LRU), BF16),
                   jax.ShapeDtypeStruct((S5_BLK, n, LANES), F32)],
        compiler_params=_cp(("parallel",)), name="inproj",
    )(x2d, mod3, g1, w_main, w_s5)


def _lru_body(xb_ref, gg_ref, cw_ref, cb_ref, wg_ref, ba_ref, bx_ref, lam_ref, h0_ref,
              ya_ref, fin_ref, xpad, a_f, b_f, a_b, b_b, hs, *, T):
    seg = T // SUBLANES
    ch = ROW_CHUNK
    nch = T // ch
    n_slab = D_LRU // LANES
    pad = 2 * SUBLANES
    sub = lax.broadcasted_iota(I32, (SUBLANES, D_LRU), 0)

    def rows_of(r0, n):
        return jnp.concatenate([xb_ref[c, r0:r0 + n, :] for c in range(n_slab)], axis=1)

    def from_prev_segment(tile):
        return jnp.where(sub >= 1, pltpu.roll(tile, 1, 0), 0.0)

    def from_next_segment(tile):
        return jnp.where(sub < SUBLANES - 1, pltpu.roll(tile, SUBLANES - 1, 0), 0.0)

    xpad[0:SUBLANES, :] = from_prev_segment(rows_of((seg - 2) * SUBLANES, SUBLANES))
    xpad[SUBLANES:pad, :] = from_prev_segment(rows_of((seg - 1) * SUBLANES, SUBLANES))
    xpad[pad + T:pad + T + SUBLANES, :] = from_next_segment(rows_of(0, SUBLANES))
    for c in range(nch):
        xpad[pad + c * ch:pad + (c + 1) * ch, :] = rows_of(c * ch, ch)

    nl = -lam_ref[...]
    softplus = jnp.maximum(nl, 0.0) + jnp.log1p(jnp.exp(-jnp.abs(nl)))
    decay = (-LRU_C) * softplus
    scr = ((a_f, b_f), (a_b, b_b))

    for c in range(nch):
        t0 = c * ch
        xc = cb_ref[...] + sum(
            xpad[t0 + k * SUBLANES:t0 + k * SUBLANES + ch, :] * cw_ref[k:k + 1, :]
            for k in range(CONV_W))
        xcb = xc.astype(BF16)
        w = MXU_DIM
        pre = [jnp.dot(xcb[:, kb * w:(kb + 1) * w], wg_ref[kb], preferred_element_type=F32)
               for kb in range(D_LRU // w)]
        for d in range(2):
            ra = jnp.concatenate([p[:, d * w:(d + 1) * w] for p in pre], axis=1)
            gx = jnp.concatenate([p[:, (2 + d) * w:(3 + d) * w] for p in pre], axis=1)
            r = _sigmoid(ra + ba_ref[d:d + 1, :])
            gi = _sigmoid(gx + bx_ref[d:d + 1, :])
            a = jnp.exp(r * decay[d:d + 1, :])
            scr[d][0][t0:t0 + ch, :] = a
            scr[d][1][t0:t0 + ch, :] = jnp.exp(0.5 * jnp.log(1.0 - a * a)) * gi * xc

    def local_scan(a_ref, u_ref, reverse):
        def step(i, carry):
            h, p = carry
            r0 = pl.multiple_of((seg - 1 - i if reverse else i) * SUBLANES, SUBLANES)
            a = a_ref[pl.ds(r0, SUBLANES), :]
            h = a * h + u_ref[pl.ds(r0, SUBLANES), :]
            p = a * p
            u_ref[pl.ds(r0, SUBLANES), :] = h
            a_ref[pl.ds(r0, SUBLANES), :] = p
            return h, p
        init = (jnp.zeros((SUBLANES, D_LRU), F32), jnp.ones((SUBLANES, D_LRU), F32))
        return lax.fori_loop(0, seg, step, init, unroll=2)

    def carry_in(h_end, p_end, h0_row, reverse):
        edge = sub == (SUBLANES - 1 if reverse else 0)
        shift = SUBLANES - 1 if reverse else 1
        c = jnp.broadcast_to(h0_row, (SUBLANES, D_LRU))
        for _ in range(SUBLANES - 1):
            c = jnp.where(edge, h0_row, pltpu.roll(h_end + p_end * c, shift, 0))
        return c

    hf_end, pf_end = local_scan(a_f, b_f, False)
    hb_end, pb_end = local_scan(a_b, b_b, True)
    c_f = carry_in(hf_end, pf_end, h0_ref[0, 0:1, :], False)
    c_b = carry_in(hb_end, pb_end, h0_ref[0, 1:2, :], True)
    fin_ref[0, 0:1, :] = (hf_end + pf_end * c_f)[SUBLANES - 1:SUBLANES, :]
    fin_ref[0, 1:2, :] = (hb_end + pb_end * c_b)[0:1, :]

    for c in range(nch):
        sl = slice(c * ch, (c + 1) * ch)
        tile3 = (ch // SUBLANES, SUBLANES, D_LRU)
        hsum = ((b_f[sl, :].reshape(tile3) + a_f[sl, :].reshape(tile3) * c_f[None])
                + (b_b[sl, :].reshape(tile3) + a_b[sl, :].reshape(tile3) * c_b[None])).reshape(ch, D_LRU)
        for j in range(n_slab):
            hs[j, sl, :] = hsum[:, j * LANES:(j + 1) * LANES]

    for k in range(SUBLANES):
        rows = slice(k * seg, (k + 1) * seg)
        h_seg = jnp.concatenate([hs[j, pl.ds(k, seg, stride=SUBLANES), :] for j in range(n_slab)], axis=1)
        ya_ref[0, rows, :] = (gg_ref[0, rows, :].astype(F32) * h_seg).astype(BF16)


def _lru_call(xb8, gg, conv_w, conv_b, wg, ba, bx, lam, h0, n_seq, T):
    gg3 = gg.reshape(n_seq, T, D_LRU)
    n_slab = D_LRU // LANES
    full = lambda shape: pl.BlockSpec(shape, lambda i: (0,) * len(shape))
    seq = lambda shape: pl.BlockSpec(shape, lambda i: (i,) + (0,) * (len(shape) - 1))
    ya, fin = pl.pallas_call(
        functools.partial(_lru_body, T=T),
        grid=(n_seq,),
        in_specs=[pl.BlockSpec((n_slab, T, LANES), lambda i: (0, i, 0)), seq((1, T, D_LRU)),
                  full((CONV_W, D_LRU)), full((1, D_LRU)),
                  full((D_LRU // MXU_DIM, MXU_DIM, 4 * MXU_DIM)), full((2, D_LRU)), full((2, D_LRU)),
                  full((2, D_LRU)),
                  seq((1, 2, D_LRU))],
        out_specs=[seq((1, T, D_LRU)), seq((1, 2, D_LRU))],
        out_shape=[jax.ShapeDtypeStruct((n_seq, T, D_LRU), BF16),
                   jax.ShapeDtypeStruct((n_seq, 2, D_LRU), F32)],
        scratch_shapes=[pltpu.VMEM((T + 3 * SUBLANES, D_LRU), F32)] + [pltpu.VMEM((T, D_LRU), F32)] * 4
                       + [pltpu.VMEM((n_slab, T, LANES), F32)],
        compiler_params=_cp(("parallel",)), name="lru",
    )(xb8, gg3, conv_w, conv_b, wg, ba, bx, lam, h0)
    return ya.reshape(n_seq * T, D_LRU), fin


def _s5_body(u_ref, wts_ref, wc_ref, ap_ref, h0_ref, y_ref, fin_ref, s_scr, hp_scr, slab,
             *, rt, n_c, with_h0, col_major):
    q = S5_SW // 4
    m = S5_SEG
    grp = SUBLANES * m
    n_grp = rt // grp
    seg_per_seq = n_c // m
    seq_per_grp = SUBLANES // seg_per_seq
    seq_tok = n_c * S5_L
    halves = seq_tok // (GRID_W * S5_L)
    sub = lax.broadcasted_iota(I32, (SUBLANES, q), 0)

    def pieces(g, i):
        out = []
        for v in range(SUBLANES):
            s, sq = divmod(v, seg_per_seq)
            tok0 = (g * seq_per_grp + s) * seq_tok
            if col_major:
                for h in range(halves):
                    n = m // halves
                    out.append((pl.ds(g * grp + h * SUBLANES + v, n, stride=halves * SUBLANES),
                                pl.ds(tok0 + (h * S5_L + i) * GRID_W + sq * n, n)))
            else:
                out.append((pl.ds(g * grp + v, m, stride=SUBLANES),
                            pl.ds(tok0 + sq * m * S5_L + i, m, stride=S5_L)))
        return out

    for i in range(S5_L):
        for g in range(n_grp):
            for rows, toks in pieces(g, i):
                slab[i, rows, :] = u_ref[0, toks, :]
    ub = jnp.concatenate([slab[i] for i in range(S5_L)], axis=1).astype(BF16)
    r1 = jnp.dot(ub, wts_ref[0], preferred_element_type=F32)
    y_in = r1[:, :S5_L * LANES]
    s_scr[...] = r1[:, S5_L * LANES:]

    def cmul(ar, ai, xr, xi):
        return ar * xr - ai * xi, ar * xi + ai * xr

    def local_scan(base, col0, reverse):
        ar, ai = ap_ref[0, 0:1, col0:col0 + q], ap_ref[0, 0:1, col0 + q:col0 + 2 * q]

        def step(k, carry):
            hr, hi = carry
            r0 = pl.multiple_of(base + (m - 1 - k if reverse else k) * SUBLANES, SUBLANES)
            pr, pi = cmul(ar, ai, hr, hi)
            hr = pr + s_scr[pl.ds(r0, SUBLANES), col0:col0 + q]
            hi = pi + s_scr[pl.ds(r0, SUBLANES), col0 + q:col0 + 2 * q]
            s_scr[pl.ds(r0, SUBLANES), col0:col0 + q] = hr
            s_scr[pl.ds(r0, SUBLANES), col0 + q:col0 + 2 * q] = hi
            return hr, hi
        zero = jnp.zeros((SUBLANES, q), F32)
        return lax.fori_loop(0, m, step, (zero, zero), unroll=2)

    def chain(g, col0, reverse, h_end):
        base = g * grp
        first = (sub & (seg_per_seq - 1)) == (seg_per_seq - 1 if reverse else 0)
        h0r = jnp.zeros((SUBLANES, q), F32)
        h0i = jnp.zeros((SUBLANES, q), F32)
        if with_h0:
            for s in range(seq_per_grp):
                row = h0_ref[0, 0, g * seq_per_grp + s:g * seq_per_grp + s + 1, :]
                mine = (sub >> (seg_per_seq.bit_length() - 1)) == s
                h0r = jnp.where(mine, row[:, col0:col0 + q], h0r)
                h0i = jnp.where(mine, row[:, col0 + q:col0 + 2 * q], h0i)
        cr, ci = h0r, h0i
        if seg_per_seq > 1:
            er, ei = ap_ref[0, m - 1:m, col0:col0 + q], ap_ref[0, m - 1:m, col0 + q:col0 + 2 * q]
            shift = SUBLANES - 1 if reverse else 1
            for _ in range(seg_per_seq - 1):
                pr, pi = cmul(er, ei, cr, ci)
                cr = jnp.where(first, h0r, pltpu.roll(h_end[0] + pr, shift, 0))
                ci = jnp.where(first, h0i, pltpu.roll(h_end[1] + pi, shift, 0))
        if with_h0 or seg_per_seq > 1:
            def add(c, _):
                r0 = pl.multiple_of(base + c * SUBLANES, SUBLANES)
                k = m - 1 - c if reverse else c
                pr, pi = cmul(ap_ref[0, pl.ds(k, 1), col0:col0 + q],
                              ap_ref[0, pl.ds(k, 1), col0 + q:col0 + 2 * q], cr, ci)
                s_scr[pl.ds(r0, SUBLANES), col0:col0 + q] += pr
                s_scr[pl.ds(r0, SUBLANES), col0 + q:col0 + 2 * q] += pi
                return 0
            lax.fori_loop(0, m, add, 0, unroll=2)
        return cr, ci

    for g in range(n_grp):
        base = g * grp
        body = slice(base + SUBLANES, base + grp)
        head = slice(base, base + SUBLANES)
        prev = slice(base, base + grp - SUBLANES)
        tail = slice(base + grp - SUBLANES, base + grp)
        cr, ci = chain(g, 0, False, local_scan(base, 0, False))
        hp_scr[body, 0:2 * q] = s_scr[prev, 0:2 * q]
        hp_scr[head, 0:q] = cr
        hp_scr[head, q:2 * q] = ci
        cr, ci = chain(g, 2 * q, True, local_scan(base, 2 * q, True))
        hp_scr[prev, 2 * q:4 * q] = s_scr[body, 2 * q:4 * q]
        hp_scr[tail, 2 * q:3 * q] = cr
        hp_scr[tail, 3 * q:4 * q] = ci
        if seg_per_seq == 1:
            rows = slice(g * SUBLANES, (g + 1) * SUBLANES)
            fin_ref[0, 0, rows, 0:2 * q] = s_scr[tail, 0:2 * q]
            fin_ref[0, 0, rows, 2 * q:4 * q] = s_scr[head, 2 * q:4 * q]
    if seg_per_seq > 1:
        fin_ref[...] = jnp.zeros(fin_ref.shape, F32)

    y = y_in + jnp.dot(hp_scr[...].astype(BF16), wc_ref[0], preferred_element_type=F32)
    for i in range(S5_L):
        slab[i] = y[:, i * LANES:(i + 1) * LANES]
        for g in range(n_grp):
            for rows, toks in pieces(g, i):
                y_ref[0, toks, :] = slab[i, rows, :]


def _s5_call(us4, wts, wc, apow, h0, n_seq, T, with_h0, col_major):
    n_c = T // S5_L
    rows = n_seq * n_c
    rt = 512
    nseq_t = rt // n_c
    h0 = h0.reshape(S5_BLK, n_seq // nseq_t, nseq_t, S5_SW)
    y4, fin = pl.pallas_call(
        functools.partial(_s5_body, rt=rt, n_c=n_c, with_h0=with_h0, col_major=col_major),
        grid=(S5_BLK, rows // rt),
        in_specs=[pl.BlockSpec((1, rt * S5_L, LANES), lambda b, r: (b, r, 0)),
                  pl.BlockSpec((1, S5_L * LANES, S5_L * LANES + S5_SW), lambda b, r: (b, 0, 0)),
                  pl.BlockSpec((1, S5_SW, S5_L * LANES), lambda b, r: (b, 0, 0)),
                  pl.BlockSpec((1, S5_SEG, S5_SW), lambda b, r: (b, 0, 0)),
                  pl.BlockSpec((1, 1, nseq_t, S5_SW), lambda b, r: (b, r, 0, 0))],
        out_specs=[pl.BlockSpec((1, rt * S5_L, LANES), lambda b, r: (b, r, 0)),
                   pl.BlockSpec((1, 1, nseq_t, S5_SW), lambda b, r: (b, r, 0, 0))],
        out_shape=[jax.ShapeDtypeStruct((S5_BLK, n_seq * T, LANES), F32),
                   jax.ShapeDtypeStruct((S5_BLK, n_seq // nseq_t, nseq_t, S5_SW), F32)],
        scratch_shapes=[pltpu.VMEM((rt, S5_SW), F32), pltpu.VMEM((rt, S5_SW), F32),
                        pltpu.VMEM((S5_L, rt, LANES), F32)],
        compiler_params=_cp(("parallel", "parallel")), name="s5",
    )(us4, wts, wc, apow, h0)
    return y4, fin.reshape(S5_BLK, n_seq, S5_SW)


def _s5_disc(lr, li, st):
    mag = jnp.exp(lr * st)
    ang = li * st
    ar, ai = mag * jnp.cos(ang), mag * jnp.sin(ang)
    den = lr * lr + li * li
    fr = ((ar - 1.0) * lr + ai * li) / den
    fi = (ai * lr - (ar - 1.0) * li) / den
    return ar, ai, fr, fi


def _s5_powers(ar, ai, n):
    out = [(jnp.ones_like(ar), jnp.zeros_like(ar))]
    for _ in range(n):
        pr, pi = out[-1]
        out.append((pr * ar - pi * ai, pr * ai + pi * ar))
    return out


def _s5_prep_body(rowp_ref, colp_ref, flatp_ref, bt_ref, cm_ref, wts_ref, wc_ref, ap_ref):
    L, N, P = S5_L, S5_N, S5_P
    w = L * LANES
    div = lambda x, d: x >> (d.bit_length() - 1)
    mod = lambda x, d: x & (d - 1)
    one = lambda m: jnp.where(m, 1.0, 0.0).astype(BF16)
    r1 = lax.broadcasted_iota(I32, (LANES, LANES), 0)
    c1 = lax.broadcasted_iota(I32, (LANES, LANES), 1)
    first_half = c1 < N
    sgn = jnp.where(first_half, -1.0, 1.0)

    e_maps, taps = [], []
    same_group = div(r1, P) == div(c1, P)
    fold = one(mod(r1, P) == mod(c1, P))
    for d in range(2):
        ar, ai, fr, fi = _s5_disc(rowp_ref[0, d, 0], rowp_ref[1, d, 0], rowp_ref[2, d, 0])
        bx1 = bt_ref[d, 0]
        bx2 = pltpu.roll(bx1, N, 1)
        x1 = fr * bx1 + (fi * sgn) * bx2
        x2 = pltpu.roll(x1, N, 1)
        e_d, t_d = [], []
        for pr, pi in _s5_powers(ar, ai, L):
            e_k = pr * x1 + (pi * sgn) * x2
            e_d.append(e_k)
            full = _dot3(e_k, cm_ref[d, 0])
            kept = jnp.where(same_group, full, 0.0)
            hi = kept.astype(BF16)
            mid = (kept - hi.astype(F32)).astype(BF16)
            lo = (kept - hi.astype(F32) - mid.astype(F32)).astype(BF16)
            t_d.append(sum(jnp.dot(part, fold, preferred_element_type=F32) for part in (hi, mid, lo)))
        e_maps.append(e_d)
        taps.append(t_d)

    r = lax.broadcasted_iota(I32, (LANES, w), 0)
    c = lax.broadcasted_iota(I32, (LANES, w), 1)
    t_tap = one((div(r, P) == div(c, LANES)) & (mod(r, P) == mod(c, P)))
    m_tap = div(r, P) == div(mod(c, LANES), P)
    t_st = one((div(r, N) == div(c, 8 * N)) & (mod(r, N) == mod(c, N)))
    m_st = div(r, P) == div(mod(c, 8 * N), N)
    col_blk = div(c1, P)
    for j in range(L):
        rows = slice(j * LANES, (j + 1) * LANES)
        toe = jnp.zeros((LANES, LANES), F32)
        for k in range(L):
            toe = (toe + jnp.where(col_blk == j + k, taps[0][k], 0.0)
                   + jnp.where(col_blk == j - k, taps[1][k], 0.0))
        tap = jnp.dot(toe.astype(BF16), t_tap, preferred_element_type=F32)
        wts_ref[0, rows, 0:w] = jnp.where(m_tap, tap, 0.0).astype(BF16)
        for k, e_j in enumerate((e_maps[0][L - 1 - j], e_maps[1][j])):
            st = jnp.dot(e_j.astype(BF16), t_st, preferred_element_type=F32)
            wts_ref[0, rows, w + k * w:w + (k + 1) * w] = jnp.where(m_st, st, 0.0).astype(BF16)

    rr = lax.broadcasted_iota(I32, (w, LANES), 0)
    cc = lax.broadcasted_iota(I32, (w, LANES), 1)
    t_row = one((div(rr, 8 * N) == div(cc, N)) & (mod(rr, N) == mod(cc, N)))
    r2 = lax.broadcasted_iota(I32, (w, w), 0)
    c2 = lax.broadcasted_iota(I32, (w, w), 1)
    m_row = div(mod(r2, 8 * N), N) == div(mod(c2, LANES), P)
    for d in range(2):
        ar, ai, _, _ = _s5_disc(colp_ref[0, d, 0], colp_ref[1, d, 0], colp_ref[2, d, 0])
        a1 = cm_ref[d, 0]
        swapped = pltpu.roll(a1, N, 0)
        a2 = jnp.where(r1 < N, swapped, -swapped)
        pw = _s5_powers(ar, ai, L)
        ks = [i + 1 for i in range(L)] if d == 0 else [L - i for i in range(L)]
        g = jnp.concatenate([a1 * pw[k][0] + a2 * pw[k][1] for k in ks], axis=1)
        corr = jnp.dot(t_row, g.astype(BF16), preferred_element_type=F32)
        wc_ref[0, d * w:(d + 1) * w, :] = jnp.where(m_row, corr, 0.0).astype(BF16)

    al = []
    for d in range(2):
        ar, ai, _, _ = _s5_disc(flatp_ref[0, d, 0], flatp_ref[1, d, 0], flatp_ref[2, d, 0])
        al.append(_s5_powers(ar, ai, L)[L])
    cur = al
    for k in range(S5_SEG):
        ap_ref[0, k:k + 1, :] = jnp.concatenate([cur[0][0], cur[0][1], cur[1][0], cur[1][1]], axis=1)
        cur = [(cr * br - ci * bi, cr * bi + ci * br) for (cr, ci), (br, bi) in zip(cur, al)]


def _s5_weights(lam_re, lam_im, log_step, b_re, b_im, c_re, c_im):
    L, G, N, P = S5_L, S5_G, S5_N, S5_P
    w = L * LANES
    step = jnp.broadcast_to(jnp.exp(log_step)[..., None], lam_re.shape)
    prm = jnp.stack([lam_re, lam_im, step])
    rowp = jnp.broadcast_to(prm[:, :, :, None, None, :], (3, 2, G, P, 2, N)).reshape(3, 2, S5_BLK, LANES, 2 * N)
    colp = jnp.broadcast_to(prm.transpose(0, 1, 3, 2)[:, :, None, :, :, None], (3, 2, 2, N, G, P))
    colp = colp.reshape(3, 2, 2 * N, S5_BLK, LANES).transpose(0, 1, 3, 2, 4)
    flatp = prm.reshape(3, 2, S5_BLK, 1, 8 * N)
    bt = jnp.stack([b_re, b_im], axis=3).transpose(0, 1, 4, 3, 2)
    bt = bt.reshape(2, S5_BLK, LANES, 2 * N)
    cm = jnp.stack([c_re, -c_im], axis=1).transpose(0, 1, 4, 2, 3)
    cm = cm.reshape(2, 2 * N, S5_BLK, LANES).transpose(0, 2, 1, 3)
    blk = lambda lead, a, b_: pl.BlockSpec(lead + (1, a, b_), lambda i: (0,) * len(lead) + (i, 0, 0))
    out = lambda a, b_: pl.BlockSpec((1, a, b_), lambda i: (i, 0, 0))
    return pl.pallas_call(
        _s5_prep_body, grid=(S5_BLK,),
        in_specs=[blk((3, 2), LANES, LANES), blk((3, 2), LANES, LANES), blk((3, 2), 1, 8 * N),
                  blk((2,), LANES, LANES), blk((2,), LANES, LANES)],
        out_specs=[out(w, w + S5_SW), out(S5_SW, w), out(S5_SEG, S5_SW)],
        out_shape=[jax.ShapeDtypeStruct((S5_BLK, w, w + S5_SW), BF16),
                   jax.ShapeDtypeStruct((S5_BLK, S5_SW, w), BF16),
                   jax.ShapeDtypeStruct((S5_BLK, S5_SEG, S5_SW), F32)],
        compiler_params=_cp(("parallel",)), name="s5_prep",
    )(rowp, colp, flatp, bt, cm)


def _mixout_body(xp_ref, xs_ref, mod_ref, g1_ref, g2_ref, wbg_ref, bbg_ref, yap_ref, yas_ref,
                 y4p_ref, y4s_ref, u4p_ref, u4s_ref, *rest, tiles_p):
    i = pl.program_id(0)

    @pl.when(i < tiles_p)
    def _():
        _mixout_tile(xp_ref, mod_ref, g1_ref, g2_ref, wbg_ref, bbg_ref, yap_ref, y4p_ref, u4p_ref, *rest)

    @pl.when(i >= tiles_p)
    def _():
        _mixout_tile(xs_ref, mod_ref, g1_ref, g2_ref, wbg_ref, bbg_ref, yas_ref, y4s_ref, u4s_ref, *rest)


def _mixout_tile(x_ref, mod_ref, g1_ref, g2_ref, wbg_ref, bbg_ref, ya_ref, y4_ref, u4_ref,
                 dsk_ref, wglu_ref, bglu_ref, wso_ref, wlo_ref, wo_ref, wrt_ref, brt_ref,
                 x1_ref, h2_ref, aff_ref):
    m = mod_ref[0]
    sh1, sc1, gt1 = m[:, 0:D_MODEL], m[:, D_MODEL:2 * D_MODEL], m[:, 2 * D_MODEL:3 * D_MODEL]
    sh2, sc2 = m[:, 3 * D_MODEL:4 * D_MODEL], m[:, 4 * D_MODEL:5 * D_MODEL]
    x = x_ref[...]
    h = _norm_mod(x, g1_ref[...], sc1, sh1)
    gates = _sigmoid(jnp.dot(h.astype(BF16), wbg_ref[...], preferred_element_type=F32) + bbg_ref[...])

    def assemble(ref):
        return jnp.concatenate([ref[k] for k in range(S5_BLK)], axis=1)

    ys = assemble(y4_ref) + dsk_ref[...] * assemble(u4_ref)
    v = _gelu(ys)
    ob = v * _sigmoid(jnp.dot(v.astype(BF16), wglu_ref[...], preferred_element_type=F32) + bglu_ref[...])
    merged = (gates[:, :D_MODEL] * jnp.dot(ya_ref[...], wlo_ref[...], preferred_element_type=F32)
              + gates[:, D_MODEL:] * jnp.dot(ob.astype(BF16), wso_ref[...], preferred_element_type=F32))
    mix = jnp.dot(merged.astype(BF16), wo_ref[...], preferred_element_type=F32)
    x1 = x + gt1 * mix
    x1_ref[...] = x1
    h2 = _norm_mod(x1, g2_ref[...], sc2, sh2)
    h2_ref[...] = h2
    logits = _dot3(wrt_ref[...], h2, dims=(((1,), (1,)), ((), ()))) + brt_ref[...]
    mx = jnp.max(logits, axis=0, keepdims=True)
    ex = jnp.exp(logits - mx)
    aff_ref[...] = ex / jnp.sum(ex, axis=0, keepdims=True)


def _mixout_call(xp, xs, mod3, g1, g2, wbg, bbg, ya_p, ya_s, y4_p, y4_s, u4_p, u4_s,
                 dsk, wglu, bglu, wso, wlo, wo, wrt, brt, mod_row, tm):
    n_p, n_s = xp.shape[0], xs.shape[0]
    tiles_p = n_p // tm
    n = n_p + n_s
    full = lambda shape: pl.BlockSpec(shape, lambda i: (0,) * len(shape))
    at_p = lambda i: jnp.minimum(i, tiles_p - 1)
    at_s = lambda i: jnp.maximum(i - tiles_p, 0)
    tok = lambda w, at: pl.BlockSpec((tm, w), lambda i: (at(i), 0))
    s5 = lambda at: pl.BlockSpec((S5_BLK, tm, LANES), lambda i: (0, at(i), 0))
    out = lambda w: pl.BlockSpec((tm, w), lambda i: (i, 0))
    return pl.pallas_call(
        functools.partial(_mixout_body, tiles_p=tiles_p),
        grid=(n // tm,),
        in_specs=[tok(D_MODEL, at_p), tok(D_MODEL, at_s),
                  pl.BlockSpec((1, 1, 6 * D_MODEL), lambda i: (mod_row(i), 0, 0)),
                  full((1, D_MODEL)), full((1, D_MODEL)), full((D_MODEL, 2 * D_MODEL)),
                  full((1, 2 * D_MODEL)), tok(D_LRU, at_p), tok(D_LRU, at_s),
                  s5(at_p), s5(at_s), s5(at_p), s5(at_s), full((1, D_S5)),
                  full((D_S5, D_S5)), full((1, D_S5)), full((D_S5, D_MODEL)), full((D_LRU, D_MODEL)),
                  full((D_MODEL, D_MODEL)), full((N_EXPERTS, D_MODEL)), full((N_EXPERTS, 1))],
        out_specs=[out(D_MODEL), out(D_MODEL), pl.BlockSpec((N_EXPERTS, tm), lambda i: (0, i))],
        out_shape=[jax.ShapeDtypeStruct((n, D_MODEL), F32), jax.ShapeDtypeStruct((n, D_MODEL), F32),
                   jax.ShapeDtypeStruct((N_EXPERTS, n), F32)],
        compiler_params=_cp(("arbitrary",)), name="mixout",
    )(xp, xs, mod3, g1, g2, wbg, bbg, ya_p, ya_s, y4_p, y4_s, u4_p, u4_s,
      dsk, wglu, bglu, wso, wlo, wo, wrt, brt)


def _select_body(aff_ref, gate_ref, pos_ref, offs_ref, *, n_tok, cap):
    aff = aff_ref[...]
    capf = float(cap)

    def bis(_, lh):
        lo, hi = lh
        mid = lo + ((hi - lo + 1) >> 1)
        cnt = jnp.sum(jnp.where(aff >= pltpu.bitcast(mid, F32), 1.0, 0.0), axis=1, keepdims=True)
        ok = cnt >= capf
        return jnp.where(ok, mid, lo), jnp.where(ok, hi, mid - 1)

    lo0 = jnp.zeros((N_EXPERTS, 1), I32)
    hi0 = jnp.full((N_EXPERTS, 1), 0x7F800000, I32)
    thr_bits, _ = lax.fori_loop(0, 31, bis, (lo0, hi0))
    thr = pltpu.bitcast(thr_bits, F32)
    need = capf - jnp.sum(jnp.where(aff > thr, 1.0, 0.0), axis=1, keepdims=True)

    r = lax.broadcasted_iota(I32, (TOK_TILE, TOK_TILE), 0)
    c = lax.broadcasted_iota(I32, (TOK_TILE, TOK_TILE), 1)
    tri = jnp.where(r < c, 1.0, 0.0).astype(BF16)
    lane = lax.broadcasted_iota(I32, (N_EXPERTS, LANES), 1)
    n_tiles = n_tok // TOK_TILE
    run_eq = jnp.zeros((N_EXPERTS, 1), F32)
    run_sel = jnp.zeros((N_EXPERTS, 1), F32)
    offs = jnp.zeros((N_EXPERTS, LANES), F32)
    for t in range(n_tiles):
        sl = slice(t * TOK_TILE, (t + 1) * TOK_TILE)
        aff_t = aff_ref[:, sl]
        is_eq = aff_t == thr
        eq_t = jnp.where(is_eq, 1.0, 0.0)
        rank_eq = jnp.dot(eq_t.astype(BF16), tri, preferred_element_type=F32) + run_eq
        cand = (aff_t > thr) | (is_eq & (rank_eq < need))
        pos_t = jnp.dot(jnp.where(cand, 1.0, 0.0).astype(BF16), tri, preferred_element_type=F32) + run_sel
        sel_t = cand & (pos_t < capf)
        sel_f = jnp.where(sel_t, 1.0, 0.0)
        offs = jnp.where(lane == t, run_sel, offs)
        gate_ref[:, sl] = jnp.where(sel_t, aff_t, 0.0)
        pos_ref[:, sl] = jnp.where(sel_t, pos_t, -1.0).astype(I32)
        run_eq = run_eq + jnp.sum(eq_t, axis=1, keepdims=True)
        run_sel = run_sel + jnp.sum(sel_f, axis=1, keepdims=True)
    offs = jnp.where(lane >= n_tiles, run_sel, offs)
    offs_ref[...] = offs.astype(I32)


def _select_call(aff_t, cap, n_tok):
    n_path = aff_t.shape[1] // n_tok
    per_p = lambda a, b: pl.BlockSpec((None, a, b), lambda p: (p, 0, 0))
    return pl.pallas_call(
        functools.partial(_select_body, n_tok=n_tok, cap=cap),
        grid=(n_path,),
        in_specs=[pl.BlockSpec((N_EXPERTS, n_tok), lambda p: (0, p))],
        out_specs=[per_p(N_EXPERTS, n_tok), per_p(N_EXPERTS, n_tok), per_p(N_EXPERTS, LANES)],
        out_shape=[jax.ShapeDtypeStruct((n_path, N_EXPERTS, n_tok), F32),
                   jax.ShapeDtypeStruct((n_path, N_EXPERTS, n_tok), I32),
                   jax.ShapeDtypeStruct((n_path, N_EXPERTS, LANES), I32)],
        compiler_params=_cp(("parallel",)), name="select",
    )(aff_t)


def _invert_body(offs_ref, pos_ref, gate_ref, idx_ref, gcol_ref, acc_i, acc_g, *, n_tiles, cap):
    path = pl.program_id(0)
    base = (path * N_EXPERTS + pl.program_id(1)) * LANES
    acc_i[...] = jnp.zeros(acc_i.shape, F32)
    acc_g[...] = jnp.zeros(acc_g.shape, F32)
    jrow = lax.broadcasted_iota(I32, (SLOT_CHUNK, LANES), 0)
    lane = lax.broadcasted_iota(I32, (1, LANES), 1)

    def per_tile(t, _):
        first = (offs_ref[base + t] >> 3) << 3
        prow = pos_ref[0, 0, pl.ds(t, 1), :]
        grow = gate_ref[0, 0, pl.ds(t, 1), :]
        tok = (lane + (t * TOK_TILE + 1)).astype(F32)

        def window(w, _):
            start = pl.multiple_of(first + w * SLOT_CHUNK, SUBLANES)
            rows = pl.ds(start, SLOT_CHUNK)
            hit = prow == (jrow + start)
            acc_i[rows, :] = jnp.maximum(acc_i[rows, :], jnp.where(hit, tok, 0.0))
            acc_g[rows, :] = jnp.maximum(acc_g[rows, :], jnp.where(hit, grow, 0.0))
            return 0

        n_win = (offs_ref[base + t + 1] - first + SLOT_CHUNK - 1) >> (SLOT_CHUNK.bit_length() - 1)
        lax.fori_loop(0, n_win, window, 0)
        return 0

    lax.fori_loop(0, n_tiles, per_tile, 0, unroll=2)
    hit_i = acc_i[0:cap, :]
    hi = hit_i.astype(BF16)
    mid = (hit_i - hi.astype(F32)).astype(BF16)
    lo = (hit_i - hi.astype(F32) - mid.astype(F32)).astype(BF16)
    ones = jnp.ones((SUBLANES, LANES), BF16)
    nt = (((1,), (1,)), ((), ()))
    tok1 = sum(lax.dot_general(ones, part, nt, preferred_element_type=F32) for part in (hi, mid, lo))
    idx_ref[0, 0] = tok1[0:1, :].astype(I32) - 1 + path * (n_tiles * TOK_TILE)
    gcol_ref[0, 0] = jnp.max(acc_g[0:cap, :], axis=1, keepdims=True)


def _invert_call(offs, pos, gate, cap):
    n_path, _, n_tok = pos.shape
    n_tiles = n_tok // TOK_TILE
    pos4 = pos.reshape(n_path, N_EXPERTS, n_tiles, TOK_TILE)
    gate4 = gate.reshape(n_path, N_EXPERTS, n_tiles, TOK_TILE)
    per_e = lambda a, b: pl.BlockSpec((1, 1, a, b), lambda p, e, o: (p, e, 0, 0))
    return pl.pallas_call(
        functools.partial(_invert_body, n_tiles=n_tiles, cap=cap),
        grid_spec=pltpu.PrefetchScalarGridSpec(
            num_scalar_prefetch=1, grid=(n_path, N_EXPERTS),
            in_specs=[per_e(n_tiles, TOK_TILE), per_e(n_tiles, TOK_TILE)],
            out_specs=[per_e(1, cap), per_e(cap, 1)],
            scratch_shapes=[pltpu.VMEM((cap + SLOT_CHUNK, LANES), F32)] * 2),
        out_shape=[jax.ShapeDtypeStruct((n_path, N_EXPERTS, 1, cap), I32),
                   jax.ShapeDtypeStruct((n_path, N_EXPERTS, cap, 1), F32)],
        compiler_params=_cp(("parallel", "parallel")), name="invert",
    )(offs.reshape(-1), pos4, gate4)


def _ffn_body(idx0_ref, idxn_ref, gcol_ref, wg_ref, wu_ref, wd_ref, h_hbm, out_ref,
              xe, wgb, wub, wdb, sem, *, cap):
    e = pl.program_id(0)
    p = pl.program_id(1)
    nxt = 1 - p
    step = e * 2 + p

    def issue_rows(idx_ref, s, j0, n):
        for j in range(j0, j0 + n):
            pltpu.make_async_copy(h_hbm.at[pl.ds(idx_ref[0, 0, 0, j], 1), :],
                                  xe.at[s, pl.ds(j, 1), :], sem.at[s]).start(priority=j % 2)

    @pl.when(step == 0)
    def _():
        def issue(j, _):
            pltpu.make_async_copy(h_hbm.at[pl.ds(idx0_ref[0, 0, 0, j], 1), :],
                                  xe.at[0, pl.ds(j, 1), :], sem.at[0]).start()
            return 0
        lax.fori_loop(0, cap, issue, 0, unroll=8)

    @pl.when(p == 0)
    def _():
        wgb[...] = wg_ref[0].astype(BF16)
        wub[...] = wu_ref[0].astype(BF16)
        wdb[...] = wd_ref[0].astype(BF16)

    pltpu.make_async_copy(h_hbm.at[pl.ds(0, cap), :], xe.at[p], sem.at[p]).wait()

    ch = ROW_CHUNK
    n_ch = cap // ch
    per = -(-cap // (n_ch - 1))
    for c in range(n_ch):
        sl = slice(c * ch, (c + 1) * ch)
        xb = xe[p, sl, :].astype(BF16)
        issue_rows(idxn_ref, nxt, min(c * per, cap), min((c + 1) * per, cap) - min(c * per, cap))
        g = jnp.dot(xb, wgb[...], preferred_element_type=F32)
        u = jnp.dot(xb, wub[...], preferred_element_type=F32)
        hid = (g * _sigmoid(g)) * u
        ye = jnp.dot(hid.astype(BF16), wdb[...], preferred_element_type=F32)
        out_ref[0, 0, sl, :] = ye * gcol_ref[0, 0, sl, :]

    @pl.when(step == 2 * N_EXPERTS - 1)
    def _():
        pltpu.make_async_copy(h_hbm.at[pl.ds(0, cap), :], xe.at[nxt], sem.at[nxt]).wait()


def _ffn_call(idx, gcol, w_gate, w_up, w_down, h2, cap):
    d_e = w_gate.shape[2]
    wspec = lambda a, b: pl.BlockSpec((1, a, b), lambda e, p: (e, 0, 0))
    return pl.pallas_call(
        functools.partial(_ffn_body, cap=cap),
        grid=(N_EXPERTS, 2),
        in_specs=[pl.BlockSpec((1, 1, 1, cap), lambda e, p: (0, 0, 0, 0), memory_space=pltpu.SMEM),
                  pl.BlockSpec((1, 1, 1, cap),
                               lambda e, p: (1 - p, jnp.minimum(e + p, N_EXPERTS - 1), 0, 0),
                               memory_space=pltpu.SMEM),
                  pl.BlockSpec((1, 1, cap, 1), lambda e, p: (p, e, 0, 0)),
                  wspec(D_MODEL, d_e), wspec(D_MODEL, d_e), wspec(d_e, D_MODEL),
                  pl.BlockSpec(memory_space=pl.ANY)],
        out_specs=pl.BlockSpec((1, 1, cap, D_MODEL), lambda e, p: (p, e, 0, 0)),
        out_shape=jax.ShapeDtypeStruct((2, N_EXPERTS, cap, D_MODEL), F32),
        scratch_shapes=[pltpu.VMEM((2, cap, D_MODEL), F32), pltpu.VMEM((D_MODEL, d_e), BF16),
                        pltpu.VMEM((D_MODEL, d_e), BF16), pltpu.VMEM((d_e, D_MODEL), BF16),
                        pltpu.SemaphoreType.DMA((2,))],
        compiler_params=_cp(("arbitrary", "arbitrary")), name="ffn",
    )(idx, idx, gcol, w_gate, w_up, w_down, h2)


def _combine_body(offs_ref, x1_ref, mod_ref, gf_ref, pos_ref, yb_hbm, y_ref, wins, acc_ref, sem,
                  *, cap, path, win_narrow):
    t = pl.program_id(0)
    n_t = pl.num_programs(0)
    slot = t % 2
    row0 = path * N_EXPERTS * LANES
    per = COMBINE_TILE // TOK_TILE

    def first_slot(tt, e):
        return offs_ref[row0 + e * LANES + tt * per]

    def kmax_of(tt):
        k = 0
        for e in range(N_EXPERTS):
            k = jnp.maximum(k, first_slot(tt + 1, e) - first_slot(tt, e))
        return k

    def geometry(tt, e, w, win):
        fetch = win + SUBLANES
        lo = first_slot(tt, e) + w * win
        start = pl.multiple_of(jnp.minimum((lo >> 3) << 3, cap - fetch), SUBLANES)
        return lo, start, fetch

    def issue(tt, w, s, win):
        for e in range(N_EXPERTS):
            _, start, fetch = geometry(tt, e, w, win)
            pltpu.make_async_copy(yb_hbm.at[path, e, pl.ds(start, fetch), :],
                                  wins.at[s, pl.ds(e * fetch, fetch), :], sem.at[s]).start()

    def drain(s, win):
        rows = N_EXPERTS * (win + SUBLANES)
        pltpu.make_async_copy(wins.at[1 - s, pl.ds(0, rows), :], wins.at[s, pl.ds(0, rows), :],
                              sem.at[s]).wait()

    def expand(w, s, win):
        sub = lax.broadcasted_iota(I32, (win + SUBLANES, COMBINE_TILE), 0)
        rows = []
        for e in range(N_EXPERTS):
            lo, start, _ = geometry(t, e, w, win)
            pos = pos_ref[e:e + 1, :]
            in_round = jnp.logical_and(pos >= lo, pos < lo + win)
            rows.append(jnp.where(jnp.logical_and(pos - start == sub, in_round), 1.0, 0.0))
        onehot = jnp.concatenate(rows, axis=0).T.astype(BF16)
        data = wins[s, 0:N_EXPERTS * (win + SUBLANES), :]
        hi = data.astype(BF16)
        lo_part = (data - hi.astype(F32)).astype(BF16)
        return (jnp.dot(onehot, hi, preferred_element_type=F32)
                + jnp.dot(onehot, lo_part, preferred_element_type=F32))

    def issue_first(tt, s):
        wide = kmax_of(tt) > win_narrow

        @pl.when(wide)
        def _():
            issue(tt, 0, s, COMBINE_WIN_WIDE)

        @pl.when(jnp.logical_not(wide))
        def _():
            issue(tt, 0, s, win_narrow)

    @pl.when(t == 0)
    def _():
        issue_first(0, 0)

    @pl.when(t + 1 < n_t)
    def _():
        issue_first(t + 1, 1 - slot)

    kmax = kmax_of(t)

    @pl.when(kmax > win_narrow)
    def _():
        drain(slot, COMBINE_WIN_WIDE)
        acc_ref[...] = expand(0, slot, COMBINE_WIN_WIDE)

    @pl.when(kmax <= win_narrow)
    def _():
        drain(slot, win_narrow)
        acc_ref[...] = expand(0, slot, win_narrow)

    def more(w, _):
        issue(t, w, slot, COMBINE_WIN_WIDE)
        drain(slot, COMBINE_WIN_WIDE)
        acc_ref[...] += expand(w, slot, COMBINE_WIN_WIDE)
        return 0
    lax.fori_loop(1, (kmax + COMBINE_WIN_WIDE - 1) >> (COMBINE_WIN_WIDE.bit_length() - 1), more, 0)
    acc = acc_ref[...]

    gt2 = mod_ref[0][:, 5 * D_MODEL:6 * D_MODEL]
    x2 = x1_ref[...] + gt2 * acc
    ms = jnp.mean(x2 * x2, axis=-1, keepdims=True)
    y_ref[...] = x2 * lax.rsqrt(ms + EPS) * gf_ref[...]


def _combine_call(offs, pos, x1, mod3, gf, ybuf, mod_row, cap, path, n, win_narrow=COMBINE_WIN):
    first = path * n // COMBINE_TILE
    return pl.pallas_call(
        functools.partial(_combine_body, cap=cap, path=path, win_narrow=win_narrow),
        grid_spec=pltpu.PrefetchScalarGridSpec(
            num_scalar_prefetch=1, grid=(n // COMBINE_TILE,),
            in_specs=[pl.BlockSpec((COMBINE_TILE, D_MODEL), lambda i, o: (first + i, 0)),
                      pl.BlockSpec((1, 1, 6 * D_MODEL), lambda i, o: (mod_row(i), 0, 0)),
                      pl.BlockSpec((1, D_MODEL), lambda i, o: (0, 0)),
                      pl.BlockSpec((None, N_EXPERTS, COMBINE_TILE), lambda i, o: (path, 0, i)),
                      pl.BlockSpec(memory_space=pl.ANY)],
            out_specs=pl.BlockSpec((COMBINE_TILE, D_MODEL), lambda i, o: (i, 0)),
            scratch_shapes=[pltpu.VMEM((2, N_EXPERTS * (COMBINE_WIN_WIDE + SUBLANES), D_MODEL), F32),
                            pltpu.VMEM((COMBINE_TILE, D_MODEL), F32),
                            pltpu.SemaphoreType.DMA((2,))]),
        out_shape=jax.ShapeDtypeStruct((n, D_MODEL), F32),
        compiler_params=_cp(("arbitrary",)), name="combine",
    )(offs.reshape(-1), x1, mod3, gf, pos, ybuf)


def _lru_gate_weights(wa, wx):
    hb = MXU_DIM // LRU_HEAD_DIM
    nb = LRU_HEADS // hb
    eye = jnp.eye(hb, dtype=F32)

    def bd(w):
        w5 = w.reshape(nb, hb, LRU_HEAD_DIM, LRU_HEAD_DIM)
        return jnp.einsum('khij,hg->khigj', w5, eye).reshape(nb, MXU_DIM, MXU_DIM)

    return jnp.concatenate([bd(wa[0]), bd(wa[1]), bd(wx[0]), bd(wx[1])], axis=2).astype(BF16)


def kernel(x_prompt, x_sample, state_lru, state_s5_re, state_s5_im, c, c_ctx, w_mod, b_mod, g_norm1, g_norm2, w_in, conv_w, conv_b, lru_wa, lru_ba, lru_wx, lru_bx, lru_lambda, s5_lambda_re, s5_lambda_im, s5_log_step, s5_b_re, s5_b_im, s5_c_re, s5_c_im, s5_d, s5_w_glu, s5_b_glu, w_lru_out, w_s5_out, w_branch_gate, b_branch_gate, w_o, w_router, b_router, w_e_gate, w_e_up, w_e_down, g_final):
    bp, tp, _ = x_prompt.shape
    bs, ts, _ = x_sample.shape
    n_p, n_s = bp * tp, bs * ts
    l = 0

    c_all = jnp.zeros((16, D_MODEL), F32).at[0].set(c_ctx).at[1:1 + bs].set(c)
    mod3 = _mod_call(c_all, w_mod[l], b_mod[l][None, :]).reshape(16, 1, 6 * D_MODEL)

    w_in_b = w_in[l].astype(BF16)
    w_main, w_s5in = w_in_b[:, :2 * D_LRU], w_in_b[:, 2 * D_LRU:]
    g1 = g_norm1[l][None, :]
    g2 = g_norm2[l][None, :]
    wg = _lru_gate_weights(lru_wa[l], lru_wx[l])
    wts, wc, apow = _s5_weights(s5_lambda_re[l], s5_lambda_im[l], s5_log_step[l],
                                s5_b_re[l], s5_b_im[l], s5_c_re[l], s5_c_im[l])
    wbg = w_branch_gate[l].astype(BF16)
    bbg = b_branch_gate[l][None, :]
    wglu = s5_w_glu[l].astype(BF16)
    wso = w_s5_out[l].astype(BF16)
    wlo = w_lru_out[l].astype(BF16)
    wo = w_o[l].astype(BF16)
    wrt = w_router[l].T
    brt = b_router[l][:, None]

    xp2 = x_prompt.reshape(n_p, D_MODEL)
    xs2 = x_sample.reshape(n_s, D_MODEL)
    ctx_row = lambda i: 0
    tm_in = 1024
    lat_row_in = lambda i: 1 + i // (ts // max(tm_in, ts))

    def s5_state(re, im):
        def part(a, d):
            return a[:, d].reshape(-1, S5_BLK, 8 * S5_N).transpose(1, 0, 2)
        return jnp.concatenate([part(re, 0), part(im, 0), part(re, 1), part(im, 1)], axis=2)

    xb_p, gg_p, us4_p = _inproj_call(xp2, mod3, g1, w_main, w_s5in, ctx_row, max(tm_in, tp), tp)
    ya_p, lru_fin = _lru_call(xb_p, gg_p, conv_w[l], conv_b[l][None, :], wg, lru_ba[l], lru_bx[l],
                              lru_lambda[l], jnp.zeros((bp, 2, D_LRU), F32), bp, tp)
    y4_p, s5_fin = _s5_call(us4_p, wts, wc, apow, jnp.zeros((S5_BLK, bp, S5_SW), F32), bp, tp,
                            with_h0=False, col_major=False)
    xb_s, gg_s, us4_s = _inproj_call(xs2, mod3, g1, w_main, w_s5in, lat_row_in, max(tm_in, ts), ts)
    ya_s, _ = _lru_call(xb_s, gg_s, conv_w[l], conv_b[l][None, :], wg, lru_ba[l], lru_bx[l],
                        lru_lambda[l], state_lru[:, l], bs, ts)
    y4_s, _ = _s5_call(us4_s, wts, wc, apow, s5_state(state_s5_re[:, l], state_s5_im[:, l]), bs, ts,
                       with_h0=True, col_major=True)

    tm_mo = 512
    tiles_p = n_p // tm_mo
    x1, h2, aff = _mixout_call(
        xp2, xs2, mod3, g1, g2, wbg, bbg, ya_p, ya_s, y4_p, y4_s, us4_p, us4_s,
        s5_d[l][None, :], wglu, s5_b_glu[l][None, :], wso, wlo, wo, wrt, brt,
        mod_row=lambda i: jnp.where(i < tiles_p, 0, 1 + (i - tiles_p) // (ts // tm_mo)), tm=tm_mo)

    assert n_p == n_s
    cap = (CAPACITY_FACTOR * n_p) // N_EXPERTS
    gate, pos, offs = _select_call(aff, cap, n_p)
    idx, gcol = _invert_call(offs, pos, gate, cap)
    ybuf = _ffn_call(idx, gcol, w_e_gate[l], w_e_up[l], w_e_down[l], h2, cap)
    gf = g_final[None, :]
    y_p = _combine_call(offs, pos, x1, mod3, gf, ybuf, ctx_row, cap, 0, n_p, win_narrow=COMBINE_WIN - 16)
    y_s = _combine_call(offs, pos, x1, mod3, gf, ybuf, lambda i: 1 + i // (ts // COMBINE_TILE), cap, 1, n_s,
                        win_narrow=COMBINE_WIN + 16)

    new_lru = lru_fin[:, None]
    sf = s5_fin.transpose(1, 0, 2).reshape(bp, S5_BLK, 2, 2, 8, S5_N)
    sf = sf.transpose(0, 2, 3, 1, 4, 5).reshape(bp, 2, 2, S5_G, S5_N)
    new_s5r = sf[:, :, 0][:, None]
    new_s5i = sf[:, :, 1][:, None]
    return (y_p.reshape(bp, tp, D_MODEL), y_s.reshape(bs, ts, D_MODEL), new_lru, new_s5r, new_s5i)
```
